```python
import jax, jax.numpy as jnp
from jax import lax
import numpy as np

D_MODEL = 1024
BATCH = 2
SEQ = 8192
DEPTH = 1

CHUNK = 64
HEAD_SIZE = 64
D_RWKV = D_MODEL // 2
N_RWKV_HEADS = D_RWKV // HEAD_SIZE
D_SGU = D_MODEL - D_RWKV
N_SGU_HEADS = D_SGU // HEAD_SIZE
SGU_BLOCK = 128
D_DECAY_LORA = 64
D_AAA_LORA = 64
D_GATE_LORA = 128
N_SHIFT = 3 * D_RWKV + D_DECAY_LORA + D_AAA_LORA + D_GATE_LORA
D_IN = N_SHIFT + 2 * D_SGU
N_EXPERTS = 64
TOP_K = 8
N_GROUPS = 8
TOPK_GROUPS = 4
D_EXPERT = 256
ROUTED_SCALE = 2.5
MOE_BLOCK = 128
RMS_EPS = 1e-6
LN_EPS = 1e-5
LN_X_EPS = 64e-5

kernel_name = 'hymba_rwkv7_sgu_moe_adaln_block'


def rmsnorm(x, g):
    xf = x.astype(jnp.float32)
    y = xf * lax.rsqrt(jnp.mean(xf * xf, axis=-1, keepdims=True) + RMS_EPS)
    return (y * g.astype(jnp.float32)).astype(x.dtype)


def layernorm(x, g, b):
    xf = x.astype(jnp.float32)
    mu = jnp.mean(xf, axis=-1, keepdims=True)
    var = jnp.mean((xf - mu) ** 2, axis=-1, keepdims=True)
    y = (xf - mu) * lax.rsqrt(var + LN_EPS)
    return (y * g.astype(jnp.float32) + b.astype(jnp.float32)).astype(x.dtype)


def swiglu(h, w1, w3, w2):
    return (jax.nn.silu(h @ w1) * (h @ w3)) @ w2


def rwkv7_mix(ps, w0, w_decay_up, a0, w_a_up, w_g_up, k_k, k_a, r_k, lnx_g, lnx_b):
    B, S, _ = ps.shape
    f32 = jnp.float32
    c1, c2, c3 = D_RWKV, 2 * D_RWKV, 3 * D_RWKV
    c4, c5 = c3 + D_DECAY_LORA, c3 + D_DECAY_LORA + D_AAA_LORA
    r, k, v = ps[..., :c1], ps[..., c1:c2], ps[..., c2:c3]
    xw, xa, xg = ps[..., c3:c4], ps[..., c4:c5], ps[..., c5:]
    w = -jax.nn.softplus(-(w0 + jnp.tanh(xw) @ w_decay_up).astype(f32)) - 0.5
    decay = jnp.exp(-jnp.exp(w))
    a = jax.nn.sigmoid((a0 + xa @ w_a_up).astype(f32))
    g = jax.nn.sigmoid(xg) @ w_g_up
    heads = lambda t: t.reshape(B, S, N_RWKV_HEADS, HEAD_SIZE)
    kk = heads((k * k_k).astype(f32))
    kk = kk / jnp.maximum(jnp.sqrt(jnp.sum(kk * kk, axis=-1, keepdims=True)), 1e-12)
    a_h = heads(a)
    k_h = heads(k.astype(f32) * (1.0 + (a - 1.0) * k_a.astype(f32)))
    r_h = heads(r.astype(f32))
    v_h = heads(v.astype(f32))
    tm = lambda t: jnp.swapaxes(t, 0, 1)

    def step(state, inp):
        r_t, w_t, k_t, v_t, kk_t, a_t = inp
        sa = jnp.einsum('bhij,bhj->bhi', state, -kk_t)
        state = (state * w_t[:, :, None, :]
                 + sa[..., None] * (kk_t * a_t)[:, :, None, :]
                 + v_t[..., None] * k_t[:, :, None, :])
        y_t = jnp.einsum('bhij,bhj->bhi', state, r_t)
        return state, y_t

    s0 = jnp.zeros((B, N_RWKV_HEADS, HEAD_SIZE, HEAD_SIZE), f32)
    _, y = lax.scan(step, s0, (tm(r_h), tm(heads(decay)), tm(k_h), tm(v_h), tm(kk), tm(a_h)))
    y = tm(y)
    mu = jnp.mean(y, axis=-1, keepdims=True)
    var = jnp.mean((y - mu) ** 2, axis=-1, keepdims=True)
    y = ((y - mu) * lax.rsqrt(var + LN_X_EPS)).reshape(B, S, D_RWKV)
    y = y * lnx_g.astype(f32) + lnx_b.astype(f32)
    bonus = jnp.sum(r_h * k_h * r_k.astype(f32), axis=-1, keepdims=True) * v_h
    y = y + bonus.reshape(B, S, D_RWKV)
    return (y * g.astype(f32)).astype(ps.dtype)


def sgu_mix(pb, ln_g, ln_b, w_spatial, b_spatial):
    B, S, _ = pb.shape
    z = jax.nn.gelu(pb)
    u, v = z[..., :D_SGU], z[..., D_SGU:]
    v = layernorm(v, ln_g, ln_b)
    nb = S // SGU_BLOCK
    v = v.reshape(B, nb, SGU_BLOCK, N_SGU_HEADS, HEAD_SIZE)
    pos_chunk = jnp.arange(SGU_BLOCK) // CHUNK
    mask = pos_chunk[None, :] <= pos_chunk[:, None]
    ws = jnp.where(mask[None], w_spatial, 0.0).astype(v.dtype)
    sp = jnp.einsum('hts,bnshc->bnthc', ws, v) + b_spatial.T[:, :, None].astype(v.dtype)
    return u * sp.reshape(B, S, D_SGU)


def routed_moe(h, w_router, e_bias, w1_e, w3_e, w2_e):
    T, D = h.shape
    f32 = jnp.float32
    scores = jax.nn.sigmoid((h @ w_router).astype(f32))
    biased = scores + e_bias.astype(f32)
    per_group = N_EXPERTS // N_GROUPS
    grp_score = jnp.sum(lax.top_k(biased.reshape(T, N_GROUPS, per_group), 2)[0], axis=-1)
    _, gidx = lax.top_k(grp_score, TOPK_GROUPS)
    gmask = jnp.sum(jax.nn.one_hot(gidx, N_GROUPS, dtype=f32), axis=-2) > 0
    emask = jnp.repeat(gmask, per_group, axis=-1)
    _, eidx = lax.top_k(jnp.where(emask, biased, -jnp.inf), TOP_K)
    wts = jnp.take_along_axis(scores, eidx, axis=-1)
    wts = wts / jnp.sum(wts, axis=-1, keepdims=True) * ROUTED_SCALE

    A = T * TOP_K
    flat_e = eidx.reshape(A).astype(jnp.int32)
    flat_t = jnp.repeat(jnp.arange(T, dtype=jnp.int32), TOP_K)
    flat_w = wts.reshape(A)
    order = jnp.argsort(flat_e)
    se = flat_e[order]
    counts = jnp.bincount(flat_e, length=N_EXPERTS).astype(jnp.int32)
    padded = (counts + MOE_BLOCK - 1) // MOE_BLOCK * MOE_BLOCK
    start = jnp.cumsum(counts) - counts
    pend = jnp.cumsum(padded)
    pstart = pend - padded
    dest = pstart[se] + jnp.arange(A, dtype=jnp.int32) - start[se]
    n_blocks = (A + N_EXPERTS * (MOE_BLOCK - 1) + MOE_BLOCK - 1) // MOE_BLOCK
    P = n_blocks * MOE_BLOCK
    buf_t = jnp.zeros((P,), jnp.int32).at[dest].set(flat_t[order])
    buf_w = jnp.zeros((P,), f32).at[dest].set(flat_w[order])
    block_e = jnp.searchsorted(pend, jnp.arange(n_blocks, dtype=jnp.int32) * MOE_BLOCK, side='right')
    block_e = jnp.minimum(block_e, N_EXPERTS - 1)

    def block_fn(args):
        tok, e = args
        xb = h[tok]
        return swiglu(xb, w1_e[e], w3_e[e], w2_e[e])

    yb = lax.map(block_fn, (buf_t.reshape(n_blocks, MOE_BLOCK), block_e))
    yb = yb.reshape(P, D) * buf_w[:, None].astype(h.dtype)
    return jnp.zeros_like(h).at[buf_t].add(yb)


def setup_inputs(seed: int = 0) -> dict:
    key = jax.random.key(seed)
    ks = jax.random.split(key, 40)
    f32 = jnp.float32
    nrm = lambda k, shape, s: jax.random.normal(k, shape, f32) * s
    L, D = DEPTH, D_MODEL
    return {
        'x': nrm(ks[0], (BATCH, SEQ, D), 1.0),
        'c': nrm(ks[1], (BATCH, D), 1.0),
        'w_ada': nrm(ks[2], (L, D, 6 * D), 0.5 * D ** -0.5),
        'b_ada': nrm(ks[3], (L, 6 * D), 0.02),
        'norm1_g': 1.0 + nrm(ks[4], (L, D), 0.02),
        'w_in': nrm(ks[5], (L, D, D_IN), D ** -0.5),
        'mu_shift': jax.random.uniform(ks[6], (L, N_SHIFT), f32),
        'w0': jax.random.uniform(ks[7], (L, D_RWKV), f32, -3.0, 0.5),
        'w_decay_up': nrm(ks[8], (L, D_DECAY_LORA, D_RWKV), 0.1),
        'a0': nrm(ks[9], (L, D_RWKV), 0.1),
        'w_a_up': nrm(ks[10], (L, D_AAA_LORA, D_RWKV), D_AAA_LORA ** -0.5),
        'w_g_up': nrm(ks[11], (L, D_GATE_LORA, D_RWKV), D_GATE_LORA ** -0.5),
        'k_k': 0.85 + nrm(ks[12], (L, D_RWKV), 0.02),
        'k_a': 1.0 + nrm(ks[13], (L, D_RWKV), 0.02),
        'r_k': nrm(ks[14], (L, N_RWKV_HEADS, HEAD_SIZE), 0.1),
        'lnx_g': 1.0 + nrm(ks[15], (L, D_RWKV), 0.02),
        'lnx_b': nrm(ks[16], (L, D_RWKV), 0.02),
        'sgu_ln_g': 1.0 + nrm(ks[17], (L, D_SGU), 0.02),
        'sgu_ln_b': nrm(ks[18], (L, D_SGU), 0.02),
        'w_spatial': nrm(ks[19], (L, N_SGU_HEADS, SGU_BLOCK, SGU_BLOCK), SGU_BLOCK ** -0.5),
        'b_spatial': 1.0 + nrm(ks[20], (L, N_SGU_HEADS, SGU_BLOCK), 0.01),
        'w_out': nrm(ks[21], (L, D, D), D ** -0.5),
        'norm2_g': 1.0 + nrm(ks[22], (L, D), 0.02),
        'w_router': nrm(ks[23], (L, D, N_EXPERTS), D ** -0.5),
        'e_bias': nrm(ks[24], (L, N_EXPERTS), 0.01),
        'w1_e': nrm(ks[25], (L, N_EXPERTS, D, D_EXPERT), D ** -0.5),
        'w3_e': nrm(ks[26], (L, N_EXPERTS, D, D_EXPERT), D ** -0.5),
        'w2_e': nrm(ks[27], (L, N_EXPERTS, D_EXPERT, D), D_EXPERT ** -0.5),
        'w1_s': nrm(ks[28], (L, D, D_EXPERT), D ** -0.5),
        'w3_s': nrm(ks[29], (L, D, D_EXPERT), D ** -0.5),
        'w2_s': nrm(ks[30], (L, D_EXPERT, D), D_EXPERT ** -0.5),
        'norm_f_g': 1.0 + nrm(ks[31], (D,), 0.02),
    }


def reference(x, c, w_ada, b_ada, norm1_g, w_in, mu_shift, w0, w_decay_up, a0, w_a_up, w_g_up,
              k_k, k_a, r_k, lnx_g, lnx_b, sgu_ln_g, sgu_ln_b, w_spatial, b_spatial, w_out,
              norm2_g, w_router, e_bias, w1_e, w3_e, w2_e, w1_s, w3_s, w2_s, norm_f_g):
    B, S, D = x.shape
    for l in range(DEPTH):
        mod = (jax.nn.silu(c) @ w_ada[l] + b_ada[l])[:, None, :]
        sh1, sc1, g1, sh2, sc2, g2 = jnp.split(mod, 6, axis=-1)

        h = rmsnorm(x, norm1_g[l]) * (1.0 + sc1) + sh1
        proj = h @ w_in[l]
        ps = proj[..., :N_SHIFT]
        prev = jnp.pad(ps, ((0, 0), (1, 0), (0, 0)))[:, :-1]
        ps = ps + (prev - ps) * mu_shift[l]
        y_a = rwkv7_mix(ps, w0[l], w_decay_up[l], a0[l], w_a_up[l], w_g_up[l],
                        k_k[l], k_a[l], r_k[l], lnx_g[l], lnx_b[l])
        y_b = sgu_mix(proj[..., N_SHIFT:], sgu_ln_g[l], sgu_ln_b[l],
                      w_spatial[l], b_spatial[l])
        y = jnp.concatenate([y_a, y_b], axis=-1) @ w_out[l]
        x = x + g1 * y

        h2 = (rmsnorm(x, norm2_g[l]) * (1.0 + sc2) + sh2).reshape(B * S, D)
        m = swiglu(h2, w1_s[l], w3_s[l], w2_s[l]) + routed_moe(h2, w_router[l], e_bias[l],
                                                               w1_e[l], w3_e[l], w2_e[l])
        x = x + g2 * m.reshape(B, S, D)
    return rmsnorm(x, norm_f_g)
```

```python
import functools

import jax
import jax.numpy as jnp
from jax import lax
from jax.experimental import pallas as pl
from jax.experimental.pallas import tpu as pltpu

F32 = jnp.float32
BF16 = jnp.bfloat16

HEAD = 64
LANES = 128
SUBLANES = 8
D_RWKV = 512
D_SGU = 512
N_RWKV_PAIRS = D_RWKV // LANES
D_DECAY_LORA = 64
D_AAA_LORA = 64
D_GATE_LORA = 128
N_SHIFT = 3 * D_RWKV + D_DECAY_LORA + D_AAA_LORA + D_GATE_LORA
SGU_BLOCK = 128
CHUNK = 64
N_EXPERTS = 64
N_GROUPS = 8
GROUP_SIZE = N_EXPERTS // N_GROUPS
TOPK_GROUPS = 4
TOP_K = 8
D_EXPERT = 256
ROUTED_SCALE = 2.5
RMS_EPS = 1e-6
LN_EPS = 1e-5
LN_X_EPS = 64e-5
VMEM_LIMIT_BYTES = 56 * 1024 * 1024


def _params(semantics):
    return pltpu.CompilerParams(dimension_semantics=semantics,
                                vmem_limit_bytes=VMEM_LIMIT_BYTES)


def _split3(x):
    hi = x.astype(BF16)
    r1 = x - hi.astype(F32)
    mid = r1.astype(BF16)
    lo = (r1 - mid.astype(F32)).astype(BF16)
    return hi, mid, lo


def _dot_exact_rhs(x, rhs_bf16):
    hi, mid, lo = _split3(x)
    acc = jnp.dot(hi, rhs_bf16, preferred_element_type=F32)
    acc += jnp.dot(mid, rhs_bf16, preferred_element_type=F32)
    acc += jnp.dot(lo, rhs_bf16, preferred_element_type=F32)
    return acc


def _dot3(a, b, dims=(((1,), (0,)), ((), ()))):
    a_hi = a.astype(BF16)
    a_lo = (a - a_hi.astype(F32)).astype(BF16)
    b_hi = b.astype(BF16)
    b_lo = (b - b_hi.astype(F32)).astype(BF16)
    dg = functools.partial(lax.dot_general, dimension_numbers=dims,
                           preferred_element_type=F32)
    return dg(a_hi, b_hi) + dg(a_hi, b_lo) + dg(a_lo, b_hi)


def _head_ones():
    r = lax.broadcasted_iota(jnp.int32, (LANES, LANES), 0) // HEAD
    c = lax.broadcasted_iota(jnp.int32, (LANES, LANES), 1) // HEAD
    return jnp.where(r == c, 1.0, 0.0).astype(BF16)


def _head_sum(z, ones):
    parts = [_dot_exact_rhs(z[:, LANES * p:LANES * (p + 1)], ones)
             for p in range(z.shape[1] // LANES)]
    return jnp.concatenate(parts, axis=-1)


def _sigmoid(x):
    return 1.0 / (1.0 + jnp.exp(-x))


def _silu(x):
    return x * _sigmoid(x)


def _mod_kernel(c_ref, w_ref, b_ref, o_ref):
    s = _silu(c_ref[...])
    o_ref[...] = _dot3(s, w_ref[...]) + b_ref[...]


def _mod(c_pad, w_ada, b_ada):
    rows, d = c_pad.shape
    n = w_ada.shape[1]
    tn = 1024
    return pl.pallas_call(
        _mod_kernel,
        grid=(n // tn,),
        in_specs=[pl.BlockSpec((rows, d), lambda j: (0, 0)),
                  pl.BlockSpec((d, tn), lambda j: (0, j)),
                  pl.BlockSpec((1, tn), lambda j: (0, j))],
        out_specs=pl.BlockSpec((rows, tn), lambda j: (0, j)),
        out_shape=jax.ShapeDtypeStruct((rows, n), F32),
        compiler_params=_params(("arbitrary",)),
        name="adaln_mod",
    )(c_pad, w_ada, b_ada)


def _inproj_kernel(x_ref, sc_ref, sh_ref, g_ref, w_ref, mu_ref, ps_ref, pb_ref, carry_ref):
    @pl.when(pl.program_id(1) == 0)
    def _():
        carry_ref[...] = jnp.zeros_like(carry_ref)

    x = x_ref[...]
    tm = x.shape[0]
    h = x * lax.rsqrt(jnp.mean(x * x, axis=-1, keepdims=True) + RMS_EPS) * g_ref[...]
    h = h * (1.0 + sc_ref[...]) + sh_ref[...]
    proj = jnp.dot(h.astype(BF16), w_ref[...], preferred_element_type=F32)
    ps = proj[:, :N_SHIFT]
    prev = pltpu.roll(ps, 1, 0)
    first = lax.broadcasted_iota(jnp.int32, ps.shape, 0) == 0
    prev = jnp.where(first, carry_ref[0:1, :], prev)
    carry_ref[0:1, :] = ps[tm - 1:tm, :]
    ps_ref[...] = ps + (prev - ps) * mu_ref[...]
    pb_ref[...] = proj[:, N_SHIFT:]


def _inproj(x, sc1, sh1, g, w_in_bf16, mu, tm):
    b, s, d = x.shape
    d_in = w_in_bf16.shape[1]
    row = lambda bi, i: (bi, 0, 0)
    return pl.pallas_call(
        _inproj_kernel,
        grid=(b, s // tm),
        in_specs=[pl.BlockSpec((None, tm, d), lambda bi, i: (bi, i, 0)),
                  pl.BlockSpec((None, 1, d), row),
                  pl.BlockSpec((None, 1, d), row),
                  pl.BlockSpec((1, d), lambda bi, i: (0, 0)),
                  pl.BlockSpec((d, d_in), lambda bi, i: (0, 0)),
                  pl.BlockSpec((1, N_SHIFT), lambda bi, i: (0, 0))],
        out_specs=[pl.BlockSpec((None, tm, N_SHIFT), lambda bi, i: (bi, i, 0)),
                   pl.BlockSpec((None, tm, d_in - N_SHIFT), lambda bi, i: (bi, i, 0))],
        out_shape=[jax.ShapeDtypeStruct((b, s, N_SHIFT), F32),
                   jax.ShapeDtypeStruct((b, s, d_in - N_SHIFT), F32)],
        scratch_shapes=[pltpu.VMEM((8, N_SHIFT), F32)],
        compiler_params=_params(("arbitrary", "arbitrary")),
        name="inproj_shift",
    )(x, sc1, sh1, g, w_in_bf16, mu)


def _prep_kernel(ps_ref, w0_ref, a0_ref, wlora_ref, wg_ref, kk_ref, ka_ref, rk_ref,
                 r_o, w_o, k_o, v_o, kk_o, b_o, g_o, bonus_o):
    ps = ps_ref[...]
    c1, c2, c3 = D_RWKV, 2 * D_RWKV, 3 * D_RWKV
    r, k, v = ps[:, :c1], ps[:, c1:c2], ps[:, c2:c3]
    xwa = ps[:, c3:c3 + LANES]
    xg = ps[:, c3 + LANES:]
    is_w = lax.broadcasted_iota(jnp.int32, xwa.shape, 1) < D_DECAY_LORA
    lora = _dot3(jnp.where(is_w, jnp.tanh(xwa), xwa), wlora_ref[...])
    wlog = -jax.nn.softplus(-(w0_ref[...] + lora[:, :c1])) - 0.5
    decay = jnp.exp(-jnp.exp(wlog))
    a = _sigmoid(a0_ref[...] + lora[:, c1:])
    g = _dot3(_sigmoid(xg), wg_ref[...])
    ones = _head_ones()
    kk = k * kk_ref[...]
    kk = kk / jnp.maximum(jnp.sqrt(_head_sum(kk * kk, ones)), 1e-12)
    kh = k * (1.0 + (a - 1.0) * ka_ref[...])
    r_o[...] = r
    w_o[...] = decay
    k_o[...] = kh
    v_o[...] = v
    kk_o[...] = kk
    b_o[...] = kk * a
    g_o[...] = g
    bonus_o[...] = _head_sum(r * kh * rk_ref[...], ones) * v


def _prep(ps, w0, a0, wlora, wg, k_k, k_a, r_k, tm):
    t = ps.shape[0]
    const = lambda shape: pl.BlockSpec(shape, lambda i: (0, 0))
    out = jax.ShapeDtypeStruct((t, D_RWKV), F32)
    return pl.pallas_call(
        _prep_kernel,
        grid=(t // tm,),
        in_specs=[pl.BlockSpec((tm, N_SHIFT), lambda i: (i, 0)),
                  const((1, D_RWKV)), const((1, D_RWKV)),
                  const((LANES, 2 * D_RWKV)), const((D_GATE_LORA, D_RWKV)),
                  const((1, D_RWKV)), const((1, D_RWKV)), const((1, D_RWKV))],
        out_specs=[pl.BlockSpec((tm, D_RWKV), lambda i: (i, 0))] * 8,
        out_shape=[out] * 8,
        compiler_params=_params(("arbitrary",)),
        name="rwkv_prep",
    )(ps, w0, a0, wlora, wg, k_k, k_a, r_k)


def _rwkv_kernel(r_ref, w_ref, k_ref, v_ref, kk_ref, b_ref, y_ref, state_ref):
    @pl.when(pl.program_id(0) == 0)
    def _():
        state_ref[...] = jnp.zeros_like(state_ref)

    nb, tt, _ = r_ref.shape
    ones = _head_ones()
    eye2 = (lax.broadcasted_iota(jnp.int32, (HEAD, LANES), 0)
            == lax.broadcasted_iota(jnp.int32, (HEAD, LANES), 1) % HEAD)

    def split(rows):
        hi = pltpu.bitcast(pltpu.bitcast(rows, jnp.uint32) & jnp.uint32(0xFFFF0000), F32)
        return hi, rows - hi

    def column_tile(parts, j):
        hi, lo = parts
        d_hi = jnp.where(eye2, hi[j:j + 1, :], 0.0).astype(BF16)
        d_lo = jnp.where(eye2, lo[j:j + 1, :], 0.0).astype(BF16)
        return (jnp.dot(d_hi, ones, preferred_element_type=F32)
                + jnp.dot(d_lo, ones, preferred_element_type=F32))

    def group(gi, carry):
        t0 = pl.multiple_of(gi * SUBLANES, SUBLANES)
        for bi in range(nb):
            for p in range(N_RWKV_PAIRS):
                sl = slice(LANES * p, LANES * (p + 1))
                rows = lambda ref: ref[bi, pl.ds(t0, SUBLANES), sl]
                kk8, w8, b8 = split(rows(kk_ref)), split(rows(w_ref)), split(rows(b_ref))
                k8, r8 = split(rows(k_ref)), split(rows(r_ref))
                v8 = rows(v_ref)
                idx = bi * N_RWKV_PAIRS + p
                s = state_ref[idx]
                ys = []
                for j in range(SUBLANES):
                    sa = jnp.sum(s * column_tile(kk8, j), axis=0, keepdims=True)
                    s = (s * column_tile(w8, j) - column_tile(b8, j) * sa
                         + column_tile(k8, j) * v8[j:j + 1, :])
                    ys.append(jnp.sum(s * column_tile(r8, j), axis=0, keepdims=True))
                state_ref[idx] = s
                y_ref[bi, pl.ds(t0, SUBLANES), sl] = jnp.concatenate(ys, axis=0)
        return carry

    lax.fori_loop(0, tt // SUBLANES, group, 0)


def _rwkv(r, w, k, v, kk, b, tt):
    nb, s, d = r.shape
    spec = pl.BlockSpec((nb, tt, d), lambda i: (0, i, 0))
    return pl.pallas_call(
        _rwkv_kernel,
        grid=(s // tt,),
        in_specs=[spec] * 6,
        out_specs=spec,
        out_shape=jax.ShapeDtypeStruct((nb, s, d), F32),
        scratch_shapes=[pltpu.VMEM((nb * N_RWKV_PAIRS, HEAD, LANES), F32)],
        compiler_params=_params(("arbitrary",)),
        name="rwkv_scan",
    )(r, w, k, v, kk, b)


def _gelu_tanh(x):
    return 0.5 * x * (1.0 + jnp.tanh(0.7978845608028654 * (x + 0.044715 * (x * x * x))))


def _sgu_kernel(pb_ref, g_ref, b_ref, ws_ref, bias_ref, o_ref):
    z = _gelu_tanh(pb_ref[...])
    u, v = z[:, :D_SGU], z[:, D_SGU:]
    mu = jnp.mean(v, axis=-1, keepdims=True)
    vc = v - mu
    var = jnp.mean(vc * vc, axis=-1, keepdims=True)
    vn = (vc * lax.rsqrt(var + LN_EPS) * g_ref[...] + b_ref[...]).astype(BF16)
    tpos = lax.broadcasted_iota(jnp.int32, (SGU_BLOCK, SGU_BLOCK), 0) // CHUNK
    spos = lax.broadcasted_iota(jnp.int32, (SGU_BLOCK, SGU_BLOCK), 1) // CHUNK
    causal = spos <= tpos
    first_head = lax.broadcasted_iota(jnp.int32, (SGU_BLOCK, LANES), 1) < HEAD
    for p in range(D_SGU // LANES):
        sl = slice(LANES * p, LANES * (p + 1))
        ws0 = jnp.where(causal, ws_ref[2 * p], 0.0).astype(BF16)
        ws1 = jnp.where(causal, ws_ref[2 * p + 1], 0.0).astype(BF16)
        for n in range(pb_ref.shape[0] // SGU_BLOCK):
            rows = slice(SGU_BLOCK * n, SGU_BLOCK * (n + 1))
            vb = vn[rows, sl]
            sp = jnp.where(first_head,
                           jnp.dot(ws0, vb, preferred_element_type=F32),
                           jnp.dot(ws1, vb, preferred_element_type=F32))
            o_ref[rows, sl] = (u[rows, sl] * (sp + bias_ref[:, sl])).astype(o_ref.dtype)


def _sgu(pb, ln_g, ln_b, w_spatial, bias_tile, tm):
    t = pb.shape[0]
    nh = w_spatial.shape[0]
    return pl.pallas_call(
        _sgu_kernel,
        grid=(t // tm,),
        in_specs=[pl.BlockSpec((tm, 2 * D_SGU), lambda i: (i, 0)),
                  pl.BlockSpec((1, D_SGU), lambda i: (0, 0)),
                  pl.BlockSpec((1, D_SGU), lambda i: (0, 0)),
                  pl.BlockSpec((nh, SGU_BLOCK, SGU_BLOCK), lambda i: (0, 0, 0)),
                  pl.BlockSpec((SGU_BLOCK, D_SGU), lambda i: (0, 0))],
        out_specs=pl.BlockSpec((tm, D_SGU), lambda i: (i, 0)),
        out_shape=jax.ShapeDtypeStruct((t, D_SGU), BF16),
        compiler_params=_params(("arbitrary",)),
        name="sgu_mix",
    )(pb, ln_g, ln_b, w_spatial, bias_tile)


def _mid_kernel(y_ref, g_ref, bonus_ref, lng_ref, lnb_ref, yb_ref, x_ref,
                g1_ref, sc2_ref, sh2_ref, g2_ref, n2_ref,
                woa_ref, wob_ref, wrt_ref, w13_ref, w2_ref,
                h2_ref, logit_ref, base_ref):
    ones = _head_ones()
    y = y_ref[...]
    mu = _head_sum(y, ones) * (1.0 / HEAD)
    yc = y - mu
    var = _head_sum(yc * yc, ones) * (1.0 / HEAD)
    ya = yc * lax.rsqrt(var + LN_X_EPS) * lng_ref[...] + lnb_ref[...]
    ya = ((ya + bonus_ref[...]) * g_ref[...]).astype(BF16)
    yo = (jnp.dot(ya, woa_ref[...], preferred_element_type=F32)
          + jnp.dot(yb_ref[...], wob_ref[...], preferred_element_type=F32))
    x1 = x_ref[...] + g1_ref[...] * yo
    h2 = x1 * lax.rsqrt(jnp.mean(x1 * x1, axis=-1, keepdims=True) + RMS_EPS) * n2_ref[...]
    h2 = h2 * (1.0 + sc2_ref[...]) + sh2_ref[...]
    h2b = h2.astype(BF16)
    h2_ref[...] = h2b
    logit_ref[...] = _dot3(wrt_ref[...], h2, dims=(((1,), (1,)), ((), ())))
    hid = jnp.dot(h2b, w13_ref[...], preferred_element_type=F32)
    act = (_silu(hid[:, :D_EXPERT]) * hid[:, D_EXPERT:]).astype(BF16)
    ms = jnp.dot(act, w2_ref[...], preferred_element_type=F32)
    base_ref[...] = x1 + g2_ref[...] * ms


def _mid(y, g, bonus, lnx_g, lnx_b, yb, x, g1, sc2, sh2, g2, n2g,
         wo_a, wo_b, wr_t, w13s, w2s, tm, tiles_per_batch):
    t, d = x.shape
    tok = lambda w: pl.BlockSpec((tm, w), lambda i: (i, 0))
    const = lambda shape: pl.BlockSpec(shape, lambda i: (0,) * len(shape))
    mod = pl.BlockSpec((None, 1, d), lambda i: (i // tiles_per_batch, 0, 0))
    return pl.pallas_call(
        _mid_kernel,
        grid=(t // tm,),
        in_specs=[tok(D_RWKV), tok(D_RWKV), tok(D_RWKV),
                  const((1, D_RWKV)), const((1, D_RWKV)),
                  tok(D_SGU), tok(d), mod, mod, mod, mod, const((1, d)),
                  const(wo_a.shape), const(wo_b.shape), const(wr_t.shape),
                  const(w13s.shape), const(w2s.shape)],
        out_specs=[tok(d), pl.BlockSpec((LANES, tm), lambda i: (0, i)), tok(d)],
        out_shape=[jax.ShapeDtypeStruct((t, d), BF16),
                   jax.ShapeDtypeStruct((LANES, t), F32),
                   jax.ShapeDtypeStruct((t, d), F32)],
        compiler_params=_params(("arbitrary",)),
        name="outproj_norm_router_shared",
    )(y, g, bonus, lnx_g, lnx_b, yb, x, g1, sc2, sh2, g2, n2g,
      wo_a, wo_b, wr_t, w13s, w2s)


def _route_kernel(logit_ref, bias_ref, gate_ref):
    tt = logit_ref.shape[1]
    scores = _sigmoid(logit_ref[0:N_EXPERTS, :])
    biased = scores + bias_ref[...]
    neg_inf = -jnp.inf

    group_blocks, group_scores = [], []
    for gi in range(N_GROUPS):
        blk = biased[GROUP_SIZE * gi:GROUP_SIZE * (gi + 1), :]
        m1 = jnp.max(blk, axis=0, keepdims=True)
        is_max = blk == m1
        n_max = jnp.sum(jnp.where(is_max, 1.0, 0.0), axis=0, keepdims=True)
        m2 = jnp.max(jnp.where(is_max, neg_inf, blk), axis=0, keepdims=True)
        group_blocks.append(blk)
        group_scores.append(m1 + jnp.where(n_max >= 2.0, m1, m2))

    masked = []
    for gi in range(N_GROUPS):
        rank = jnp.zeros((1, tt), F32)
        for gj in range(N_GROUPS):
            if gj == gi:
                continue
            ahead = (group_scores[gj] >= group_scores[gi]) if gj < gi else \
                    (group_scores[gj] > group_scores[gi])
            rank += jnp.where(ahead, 1.0, 0.0)
        masked.append(jnp.where(rank < float(TOPK_GROUPS), group_blocks[gi], neg_inf))
    sub = lax.broadcasted_iota(jnp.int32, (GROUP_SIZE, tt), 0)
    ranks = [jnp.zeros((GROUP_SIZE, tt), F32) for _ in range(N_GROUPS)]
    for ej in range(N_EXPERTS):
        gj, oj = divmod(ej, GROUP_SIZE)
        other = masked[gj][oj:oj + 1, :]
        for gi in range(N_GROUPS):
            blk = masked[gi]
            if gi > gj:
                ahead = jnp.where(other >= blk, 1.0, 0.0)
            elif gi < gj:
                ahead = jnp.where(other > blk, 1.0, 0.0)
            else:
                ahead = jnp.where(sub > oj, jnp.where(other >= blk, 1.0, 0.0),
                                  jnp.where(other > blk, 1.0, 0.0))
            ranks[gi] = ranks[gi] + ahead
    rank = jnp.concatenate(ranks, axis=0)
    sw = jnp.where(rank < float(TOP_K), scores, 0.0)
    gates = sw / jnp.sum(sw, axis=0, keepdims=True) * ROUTED_SCALE
    gates = jnp.concatenate([gates, jnp.zeros((LANES - N_EXPERTS, tt), F32)], axis=0)
    gate_ref[...] = gates.T


def _route(logits_t, bias_tile, tt):
    t = logits_t.shape[1]
    return pl.pallas_call(
        _route_kernel,
        grid=(t // tt,),
        in_specs=[pl.BlockSpec((LANES, tt), lambda i: (0, i)),
                  pl.BlockSpec((N_EXPERTS, tt), lambda i: (0, 0))],
        out_specs=pl.BlockSpec((tt, LANES), lambda i: (i, 0)),
        out_shape=jax.ShapeDtypeStruct((t, LANES), F32),
        compiler_params=_params(("arbitrary",)),
        name="route_topk",
    )(logits_t, bias_tile)


def _moe_kernel(h2_ref, gate_ref, w13_ref, w2_ref, base_ref, g2_ref, nf_ref, o_ref, acc_ref):
    e = pl.program_id(1)

    @pl.when(e == 0)
    def _():
        acc_ref[...] = jnp.zeros_like(acc_ref)

    sel = jnp.where(lax.broadcasted_iota(jnp.int32, (LANES, D_EXPERT), 0) == e,
                    1.0, 0.0).astype(BF16)
    gcol = _dot_exact_rhs(gate_ref[...], sel)
    hid = jnp.dot(h2_ref[...], w13_ref[...], preferred_element_type=F32)
    act = (_silu(hid[:, :D_EXPERT]) * hid[:, D_EXPERT:] * gcol).astype(BF16)
    acc_ref[...] += jnp.dot(act, w2_ref[...], preferred_element_type=F32)

    @pl.when(e == pl.num_programs(1) - 1)
    def _():
        xo = base_ref[...] + g2_ref[...] * acc_ref[...]
        o_ref[...] = (xo * lax.rsqrt(jnp.mean(xo * xo, axis=-1, keepdims=True) + RMS_EPS)
                      * nf_ref[...])


def _moe(h2, gates, w13e, w2e, base, g2, nf, tm, tiles_per_batch):
    t, d = base.shape
    ne = w13e.shape[0]
    tok = lambda w: pl.BlockSpec((tm, w), lambda i, e: (i, 0))
    return pl.pallas_call(
        _moe_kernel,
        grid=(t // tm, ne),
        in_specs=[tok(d), tok(LANES),
                  pl.BlockSpec((None, d, 2 * D_EXPERT), lambda i, e: (e, 0, 0)),
                  pl.BlockSpec((None, D_EXPERT, d), lambda i, e: (e, 0, 0)),
                  tok(d),
                  pl.BlockSpec((None, 1, d), lambda i, e: (i // tiles_per_batch, 0, 0)),
                  pl.BlockSpec((1, d), lambda i, e: (0, 0))],
        out_specs=tok(d),
        out_shape=jax.ShapeDtypeStruct((t, d), F32),
        scratch_shapes=[pltpu.VMEM((tm, d), F32)],
        compiler_params=_params(("arbitrary", "arbitrary")),
        name="moe_experts_final",
    )(h2, gates, w13e, w2e, base, g2, nf)


def _tile(n, want):
    t = min(n, want)
    assert n % t == 0, (n, t)
    return t


def kernel(x, c, w_ada, b_ada, norm1_g, w_in, mu_shift, w0, w_decay_up, a0, w_a_up, w_g_up,
           k_k, k_a, r_k, lnx_g, lnx_b, sgu_ln_g, sgu_ln_b, w_spatial, b_spatial, w_out,
           norm2_g, w_router, e_bias, w1_e, w3_e, w2_e, w1_s, w3_s, w2_s, norm_f_g):
    b, s, d = x.shape
    t = b * s
    assert w_ada.shape[0] == 1, "single-layer block"
    assert d == 2 * D_RWKV and s % SGU_BLOCK == 0
    row = lambda vec: vec.reshape(1, -1)

    c_pad = jnp.pad(c, ((0, 8 - b), (0, 0)))
    mod = _mod(c_pad, w_ada[0], b_ada)[:b]
    sh1, sc1, g1, sh2, sc2, g2 = [m[:, None, :] for m in jnp.split(mod, 6, axis=-1)]

    ps, pb = _inproj(x, sc1, sh1, norm1_g, w_in[0].astype(BF16), mu_shift, _tile(s, 512))

    wlora = jnp.zeros((LANES, 2 * D_RWKV), F32)
    wlora = wlora.at[:D_DECAY_LORA, :D_RWKV].set(w_decay_up[0])
    wlora = wlora.at[D_DECAY_LORA:, D_RWKV:].set(w_a_up[0])
    r, wdec, kh, v, kk, bb, g, bonus = _prep(
        ps.reshape(t, N_SHIFT), w0, a0, wlora, w_g_up[0], k_k, k_a, row(r_k), _tile(t, 512))

    seq = lambda z: z.reshape(b, s, D_RWKV)
    y = _rwkv(seq(r), seq(wdec), seq(kh), seq(v), seq(kk), seq(bb), _tile(s, 256))

    bias_tile = jnp.repeat(b_spatial[0].T, HEAD, axis=1)
    yb = _sgu(pb.reshape(t, 2 * D_SGU), sgu_ln_g, sgu_ln_b, w_spatial[0], bias_tile,
              _tile(t, 512))

    tm_mid = _tile(s, 512)
    wr_t = jnp.pad(w_router[0].T, ((0, LANES - N_EXPERTS), (0, 0)))
    w13s = jnp.concatenate([w1_s[0], w3_s[0]], axis=1).astype(BF16)
    wo = w_out[0].astype(BF16)
    h2, logits_t, base = _mid(
        y.reshape(t, D_RWKV), g, bonus, lnx_g, lnx_b, yb, x.reshape(t, d),
        g1, sc2, sh2, g2, norm2_g, wo[:D_RWKV], wo[D_RWKV:], wr_t, w13s,
        w2_s[0].astype(BF16), tm_mid, s // tm_mid)

    tt_route = _tile(t, 1024)
    gates = _route(logits_t, jnp.broadcast_to(e_bias[0][:, None], (N_EXPERTS, tt_route)),
                   tt_route)

    tm_moe = _tile(s, 1024)
    w13e = jnp.concatenate([w1_e[0], w3_e[0]], axis=2).astype(BF16)
    out = _moe(h2, gates, w13e, w2_e[0].astype(BF16), base, g2, row(norm_f_g),
               tm_moe, s // tm_moe)
    return out.reshape(b, s, d)
```

```python
import functools

import jax
import jax.numpy as jnp
from jax import lax
from jax.experimental import pallas as pl
from jax.experimental.pallas import tpu as pltpu

F32 = jnp.float32
BF16 = jnp.bfloat16

HEAD = 64
LANES = 128
SUBLANES = 8
D_RWKV = 512
D_SGU = 512
N_RWKV_PAIRS = D_RWKV // LANES
D_DECAY_LORA = 64
D_AAA_LORA = 64
D_GATE_LORA = 128
N_SHIFT = 3 * D_RWKV + D_DECAY_LORA + D_AAA_LORA + D_GATE_LORA
SGU_BLOCK = 128
CHUNK = 64
N_EXPERTS = 64
N_GROUPS = 8
GROUP_SIZE = N_EXPERTS // N_GROUPS
TOPK_GROUPS = 4
TOP_K = 8
D_EXPERT = 256
ROUTED_SCALE = 2.5
RMS_EPS = 1e-6
LN_EPS = 1e-5
LN_X_EPS = 64e-5
VMEM_LIMIT_BYTES = 56 * 1024 * 1024


def _params(semantics):
    return pltpu.CompilerParams(dimension_semantics=semantics,
                                vmem_limit_bytes=VMEM_LIMIT_BYTES)


def _split3(x):
    hi = x.astype(BF16)
    r1 = x - hi.astype(F32)
    mid = r1.astype(BF16)
    lo = (r1 - mid.astype(F32)).astype(BF16)
    return hi, mid, lo


def _dot_exact_rhs(x, rhs_bf16):
    hi, mid, lo = _split3(x)
    acc = jnp.dot(hi, rhs_bf16, preferred_element_type=F32)
    acc += jnp.dot(mid, rhs_bf16, preferred_element_type=F32)
    acc += jnp.dot(lo, rhs_bf16, preferred_element_type=F32)
    return acc


def _dot3(a, b, dims=(((1,), (0,)), ((), ()))):
    a_hi = a.astype(BF16)
    a_lo = (a - a_hi.astype(F32)).astype(BF16)
    b_hi = b.astype(BF16)
    b_lo = (b - b_hi.astype(F32)).astype(BF16)
    dg = functools.partial(lax.dot_general, dimension_numbers=dims,
                           preferred_element_type=F32)
    return dg(a_hi, b_hi) + dg(a_hi, b_lo) + dg(a_lo, b_hi)


def _head_ones():
    r = lax.broadcasted_iota(jnp.int32, (LANES, LANES), 0) // HEAD
    c = lax.broadcasted_iota(jnp.int32, (LANES, LANES), 1) // HEAD
    return jnp.where(r == c, 1.0, 0.0).astype(BF16)


def _head_sum(z, ones):
    parts = [_dot_exact_rhs(z[:, LANES * p:LANES * (p + 1)], ones)
             for p in range(z.shape[1] // LANES)]
    return jnp.concatenate(parts, axis=-1)


def _sigmoid(x):
    return 1.0 / (1.0 + jnp.exp(-x))


def _silu(x):
    return x * _sigmoid(x)


def _mod_kernel(c_ref, w_ref, b_ref, o_ref):
    s = _silu(c_ref[...])
    o_ref[...] = _dot3(s, w_ref[...]) + b_ref[...]


def _mod(c_pad, w_ada, b_ada):
    rows, d = c_pad.shape
    n = w_ada.shape[1]
    tn = 1024
    return pl.pallas_call(
        _mod_kernel,
        grid=(n // tn,),
        in_specs=[pl.BlockSpec((rows, d), lambda j: (0, 0)),
                  pl.BlockSpec((d, tn), lambda j: (0, j)),
                  pl.BlockSpec((1, tn), lambda j: (0, j))],
        out_specs=pl.BlockSpec((rows, tn), lambda j: (0, j)),
        out_shape=jax.ShapeDtypeStruct((rows, n), F32),
        compiler_params=_params(("arbitrary",)),
        name="adaln_mod",
    )(c_pad, w_ada, b_ada)


def _inproj_kernel(x_ref, sc_ref, sh_ref, g_ref, w_ref, mu_ref, ps_ref, pb_ref, carry_ref):
    @pl.when(pl.program_id(1) == 0)
    def _():
        carry_ref[...] = jnp.zeros_like(carry_ref)

    x = x_ref[...]
    tm = x.shape[0]
    h = x * lax.rsqrt(jnp.mean(x * x, axis=-1, keepdims=True) + RMS_EPS) * g_ref[...]
    h = h * (1.0 + sc_ref[...]) + sh_ref[...]
    proj = jnp.dot(h.astype(BF16), w_ref[...], preferred_element_type=F32)
    ps = proj[:, :N_SHIFT]
    prev = pltpu.roll(ps, 1, 0)
    first = lax.broadcasted_iota(jnp.int32, ps.shape, 0) == 0
    prev = jnp.where(first, carry_ref[0:1, :], prev)
    carry_ref[0:1, :] = ps[tm - 1:tm, :]
    ps_ref[...] = ps + (prev - ps) * mu_ref[...]
    pb_ref[...] = proj[:, N_SHIFT:]


def _inproj(x, sc1, sh1, g, w_in_bf16, mu, tm):
    b, s, d = x.shape
    d_in = w_in_bf16.shape[1]
    row = lambda bi, i: (bi, 0, 0)
    return pl.pallas_call(
        _inproj_kernel,
        grid=(b, s // tm),
        in_specs=[pl.BlockSpec((None, tm, d), lambda bi, i: (bi, i, 0)),
                  pl.BlockSpec((None, 1, d), row),
                  pl.BlockSpec((None, 1, d), row),
                  pl.BlockSpec((1, d), lambda bi, i: (0, 0)),
                  pl.BlockSpec((d, d_in), lambda bi, i: (0, 0)),
                  pl.BlockSpec((1, N_SHIFT), lambda bi, i: (0, 0))],
        out_specs=[pl.BlockSpec((None, tm, N_SHIFT), lambda bi, i: (bi, i, 0)),
                   pl.BlockSpec((None, tm, d_in - N_SHIFT), lambda bi, i: (bi, i, 0))],
        out_shape=[jax.ShapeDtypeStruct((b, s, N_SHIFT), F32),
                   jax.ShapeDtypeStruct((b, s, d_in - N_SHIFT), F32)],
        scratch_shapes=[pltpu.VMEM((8, N_SHIFT), F32)],
        compiler_params=_params(("arbitrary", "arbitrary")),
        name="inproj_shift",
    )(x, sc1, sh1, g, w_in_bf16, mu)


def _prep_kernel(ps_ref, w0_ref, a0_ref, wlora_ref, wg_ref, kk_ref, ka_ref, rk_ref,
                 r_o, w_o, k_o, v_o, kk_o, b_o, g_o, bonus_o):
    ps = ps_ref[...]
    c1, c2, c3 = D_RWKV, 2 * D_RWKV, 3 * D_RWKV
    r, k, v = ps[:, :c1], ps[:, c1:c2], ps[:, c2:c3]
    xwa = ps[:, c3:c3 + LANES]
    xg = ps[:, c3 + LANES:]
    is_w = lax.broadcasted_iota(jnp.int32, xwa.shape, 1) < D_DECAY_LORA
    lora = _dot3(jnp.where(is_w, jnp.tanh(xwa), xwa), wlora_ref[...])
    wlog = -jax.nn.softplus(-(w0_ref[...] + lora[:, :c1])) - 0.5
    log_decay = -jnp.exp(wlog)
    a = _sigmoid(a0_ref[...] + lora[:, c1:])
    g = _dot3(_sigmoid(xg), wg_ref[...])
    ones = _head_ones()
    kk = k * kk_ref[...]
    kk = kk / jnp.maximum(jnp.sqrt(_head_sum(kk * kk, ones)), 1e-12)
    kh = k * (1.0 + (a - 1.0) * ka_ref[...])
    r_o[...] = r
    w_o[...] = log_decay
    k_o[...] = kh
    v_o[...] = v
    kk_o[...] = kk
    b_o[...] = kk * a
    g_o[...] = g
    bonus_o[...] = _head_sum(r * kh * rk_ref[...], ones) * v


def _prep(ps, w0, a0, wlora, wg, k_k, k_a, r_k, tm):
    t = ps.shape[0]
    const = lambda shape: pl.BlockSpec(shape, lambda i: (0, 0))
    out = jax.ShapeDtypeStruct((t, D_RWKV), F32)
    return pl.pallas_call(
        _prep_kernel,
        grid=(t // tm,),
        in_specs=[pl.BlockSpec((tm, N_SHIFT), lambda i: (i, 0)),
                  const((1, D_RWKV)), const((1, D_RWKV)),
                  const((LANES, 2 * D_RWKV)), const((D_GATE_LORA, D_RWKV)),
                  const((1, D_RWKV)), const((1, D_RWKV)), const((1, D_RWKV))],
        out_specs=[pl.BlockSpec((tm, D_RWKV), lambda i: (i, 0))] * 8,
        out_shape=[out] * 8,
        compiler_params=_params(("arbitrary",)),
        name="rwkv_prep",
    )(ps, w0, a0, wlora, wg, k_k, k_a, r_k)


RWKV_CHUNK = 16
CHUNK_TILE = 128

_NN = (((1,), (0,)), ((), ()))
_NT = (((1,), (1,)), ((), ()))


def _dot1(a, b, dims=_NN):
    return lax.dot_general(a.astype(BF16), b.astype(BF16), dims, preferred_element_type=F32)


def _dot_exact_lhs(lhs_bf16, x):
    hi, mid, lo = _split3(x)
    acc = jnp.dot(lhs_bf16, hi, preferred_element_type=F32)
    acc += jnp.dot(lhs_bf16, mid, preferred_element_type=F32)
    acc += jnp.dot(lhs_bf16, lo, preferred_element_type=F32)
    return acc


def _chunk_kernel(r_ref, lw_ref, k_ref, v_ref, kk_ref, b_ref,
                  w_o, rh_o, u0_o, y0_o, bh_o, kh_o, pl_o):
    n = CHUNK_TILE
    tpos = lax.broadcasted_iota(jnp.int32, (n, n), 0)
    spos = lax.broadcasted_iota(jnp.int32, (n, n), 1)
    same = (tpos // RWKV_CHUNK) == (spos // RWKV_CHUNK)
    incl = same & (spos <= tpos)
    strict = same & (spos < tpos)
    tri_ones = jnp.where(incl, 1.0, 0.0).astype(BF16)
    blk_ones = jnp.where(same, 1.0, 0.0).astype(BF16)
    eye = jnp.where(tpos == spos, 1.0, 0.0)
    lane_head = lax.broadcasted_iota(jnp.int32, (n, LANES), 1) // HEAD

    for p in range(N_RWKV_PAIRS):
        sl = slice(LANES * p, LANES * (p + 1))
        lw = lw_ref[:, sl]
        cum = _dot_exact_lhs(tri_ones, lw)
        tot = _dot_exact_lhs(blk_ones, lw)
        at = -kk_ref[:, sl] * jnp.exp(cum - lw)
        rt = r_ref[:, sl] * jnp.exp(cum)
        inv = jnp.exp(-cum)
        bt = b_ref[:, sl] * inv
        kt = k_ref[:, sl] * inv
        rem = jnp.exp(tot - cum)
        bh_o[:, sl] = b_ref[:, sl] * rem
        kh_o[:, sl] = k_ref[:, sl] * rem
        pl_o[:, sl] = jnp.exp(tot)
        v = v_ref[:, sl]

        w_acc = rh_acc = u0_acc = y0_acc = None
        for h in range(LANES // HEAD):
            mine = lane_head == h
            a_h = jnp.where(mine, at, 0.0)
            r_h = jnp.where(mine, rt, 0.0)
            v_h = jnp.where(mine, v, 0.0)
            a_ab = jnp.where(strict, _dot3(a_h, bt, _NT), 0.0)
            a_ak = jnp.where(strict, _dot3(a_h, kt, _NT), 0.0)
            m_rb = jnp.where(incl, _dot1(r_h, bt, _NT), 0.0)
            m_rk = jnp.where(incl, _dot1(r_h, kt, _NT), 0.0)
            tinv = eye + a_ab
            x = a_ab
            for _ in range(RWKV_CHUNK.bit_length() - 2):
                x = _dot3(x, x)
                tinv = tinv + _dot3(tinv, x)
            w_h = _dot3(tinv, a_h)
            u0_h = _dot3(tinv, _dot3(a_ak, v_h))
            rh_h = r_h + _dot1(m_rb, w_h)
            y0_h = _dot1(m_rb, u0_h) + _dot1(m_rk, v_h)
            if h == 0:
                w_acc, rh_acc, u0_acc, y0_acc = w_h, rh_h, u0_h, y0_h
            else:
                w_acc, rh_acc, u0_acc, y0_acc = (w_acc + w_h, rh_acc + rh_h,
                                                 u0_acc + u0_h, y0_acc + y0_h)
        w_o[:, sl] = w_acc
        rh_o[:, sl] = rh_acc
        u0_o[:, sl] = u0_acc
        y0_o[:, sl] = y0_acc


def _chunk(r, lw, k, v, kk, b):
    t, d = r.shape
    spec = pl.BlockSpec((CHUNK_TILE, d), lambda i: (i, 0))
    out = jax.ShapeDtypeStruct((t, d), F32)
    return pl.pallas_call(
        _chunk_kernel,
        grid=(t // CHUNK_TILE,),
        in_specs=[spec] * 6,
        out_specs=[spec] * 7,
        out_shape=[out] * 7,
        compiler_params=_params(("arbitrary",)),
        name="rwkv_chunk_prep",
    )(r, lw, k, v, kk, b)


def _rwkv_kernel(w_ref, rh_ref, u0_ref, y0_ref, bh_ref, kh_ref, v_ref, pl_ref,
                 y_ref, state_ref):
    @pl.when(pl.program_id(0) == 0)
    def _():
        state_ref[...] = jnp.zeros_like(state_ref)

    nb, tt, _ = w_ref.shape
    L = RWKV_CHUNK
    row_head = lax.broadcasted_iota(jnp.int32, (LANES, LANES), 0) // HEAD
    col_head = lax.broadcasted_iota(jnp.int32, (LANES, LANES), 1) // HEAD
    same_head = row_head == col_head
    eye = (lax.broadcasted_iota(jnp.int32, (LANES, LANES), 0)
           == lax.broadcasted_iota(jnp.int32, (LANES, LANES), 1))
    all_ones = jnp.ones((LANES, LANES), BF16)
    pad = jnp.zeros((LANES - 2 * L, LANES), F32)

    def chunk(ci, carry):
        t0 = pl.multiple_of(ci * L, L)
        for bi in range(nb):
            for p in range(N_RWKV_PAIRS):
                sl = slice(LANES * p, LANES * (p + 1))
                rows = lambda ref: ref[bi, pl.ds(t0, L), sl]
                idx = bi * N_RWKV_PAIRS + p
                s = state_ref[idx]
                g = _dot3(jnp.concatenate([rows(w_ref), rows(rh_ref)], axis=0), s)
                u = g[:L] + rows(u0_ref)
                y_ref[bi, pl.ds(t0, L), sl] = g[L:] + rows(y0_ref)
                bk_t = jnp.concatenate([rows(bh_ref), rows(kh_ref), pad], axis=0).T
                uv = jnp.concatenate([u, rows(v_ref), pad], axis=0)
                upd = _dot3(bk_t, uv)
                p_row = pl_ref[bi, pl.ds(t0, SUBLANES), sl][0:1, :]
                p_hi = pltpu.bitcast(pltpu.bitcast(p_row, jnp.uint32) & jnp.uint32(0xFFFF0000), F32)
                p_col = (jnp.dot(jnp.where(eye, p_hi, 0.0).astype(BF16), all_ones,
                                 preferred_element_type=F32)
                         + jnp.dot(jnp.where(eye, p_row - p_hi, 0.0).astype(BF16), all_ones,
                                   preferred_element_type=F32))
                state_ref[idx] = p_col * s + jnp.where(same_head, upd, 0.0)
        return carry

    lax.fori_loop(0, tt // L, chunk, 0)


def _rwkv(w, rh, u0, y0, bh, kh, v, p_l, tt):
    nb, s, d = w.shape
    spec = pl.BlockSpec((nb, tt, d), lambda i: (0, i, 0))
    return pl.pallas_call(
        _rwkv_kernel,
        grid=(s // tt,),
        in_specs=[spec] * 8,
        out_specs=spec,
        out_shape=jax.ShapeDtypeStruct((nb, s, d), F32),
        scratch_shapes=[pltpu.VMEM((nb * N_RWKV_PAIRS, LANES, LANES), F32)],
        compiler_params=_params(("arbitrary",)),
        name="rwkv_scan",
    )(w, rh, u0, y0, bh, kh, v, p_l)


def _gelu_tanh(x):
    return 0.5 * x * (1.0 + jnp.tanh(0.7978845608028654 * (x + 0.044715 * (x * x * x))))


def _sgu_kernel(pb_ref, g_ref, b_ref, ws_ref, bias_ref, o_ref):
    z = _gelu_tanh(pb_ref[...])
    u, v = z[:, :D_SGU], z[:, D_SGU:]
    mu = jnp.mean(v, axis=-1, keepdims=True)
    vc = v - mu
    var = jnp.mean(vc * vc, axis=-1, keepdims=True)
    vn = (vc * lax.rsqrt(var + LN_EPS) * g_ref[...] + b_ref[...]).astype(BF16)
    tpos = lax.broadcasted_iota(jnp.int32, (SGU_BLOCK, SGU_BLOCK), 0) // CHUNK
    spos = lax.broadcasted_iota(jnp.int32, (SGU_BLOCK, SGU_BLOCK), 1) // CHUNK
    causal = spos <= tpos
    first_head = lax.broadcasted_iota(jnp.int32, (SGU_BLOCK, LANES), 1) < HEAD
    for p in range(D_SGU // LANES):
        sl = slice(LANES * p, LANES * (p + 1))
        ws0 = jnp.where(causal, ws_ref[2 * p], 0.0).astype(BF16)
        ws1 = jnp.where(causal, ws_ref[2 * p + 1], 0.0).astype(BF16)
        for n in range(pb_ref.shape[0] // SGU_BLOCK):
            rows = slice(SGU_BLOCK * n, SGU_BLOCK * (n + 1))
            vb = vn[rows, sl]
            sp = jnp.where(first_head,
                           jnp.dot(ws0, vb, preferred_element_type=F32),
                           jnp.dot(ws1, vb, preferred_element_type=F32))
            o_ref[rows, sl] = (u[rows, sl] * (sp + bias_ref[:, sl])).astype(o_ref.dtype)


def _sgu(pb, ln_g, ln_b, w_spatial, bias_tile, tm):
    t = pb.shape[0]
    nh = w_spatial.shape[0]
    return pl.pallas_call(
        _sgu_kernel,
        grid=(t // tm,),
        in_specs=[pl.BlockSpec((tm, 2 * D_SGU), lambda i: (i, 0)),
                  pl.BlockSpec((1, D_SGU), lambda i: (0, 0)),
                  pl.BlockSpec((1, D_SGU), lambda i: (0, 0)),
                  pl.BlockSpec((nh, SGU_BLOCK, SGU_BLOCK), lambda i: (0, 0, 0)),
                  pl.BlockSpec((SGU_BLOCK, D_SGU), lambda i: (0, 0))],
        out_specs=pl.BlockSpec((tm, D_SGU), lambda i: (i, 0)),
        out_shape=jax.ShapeDtypeStruct((t, D_SGU), BF16),
        compiler_params=_params(("arbitrary",)),
        name="sgu_mix",
    )(pb, ln_g, ln_b, w_spatial, bias_tile)


def _mid_kernel(y_ref, g_ref, bonus_ref, lng_ref, lnb_ref, yb_ref, x_ref,
                g1_ref, sc2_ref, sh2_ref, g2_ref, n2_ref,
                woa_ref, wob_ref, wrt_ref, w13_ref, w2_ref,
                h2_ref, logit_ref, base_ref):
    ones = _head_ones()
    y = y_ref[...]
    mu = _head_sum(y, ones) * (1.0 / HEAD)
    yc = y - mu
    var = _head_sum(yc * yc, ones) * (1.0 / HEAD)
    ya = yc * lax.rsqrt(var + LN_X_EPS) * lng_ref[...] + lnb_ref[...]
    ya = ((ya + bonus_ref[...]) * g_ref[...]).astype(BF16)
    yo = (jnp.dot(ya, woa_ref[...], preferred_element_type=F32)
          + jnp.dot(yb_ref[...], wob_ref[...], preferred_element_type=F32))
    x1 = x_ref[...] + g1_ref[...] * yo
    h2 = x1 * lax.rsqrt(jnp.mean(x1 * x1, axis=-1, keepdims=True) + RMS_EPS) * n2_ref[...]
    h2 = h2 * (1.0 + sc2_ref[...]) + sh2_ref[...]
    h2b = h2.astype(BF16)
    h2_ref[...] = h2b
    logit_ref[...] = _dot3(wrt_ref[...], h2, dims=(((1,), (1,)), ((), ())))
    hid = jnp.dot(h2b, w13_ref[...], preferred_element_type=F32)
    act = (_silu(hid[:, :D_EXPERT]) * hid[:, D_EXPERT:]).astype(BF16)
    ms = jnp.dot(act, w2_ref[...], preferred_element_type=F32)
    base_ref[...] = x1 + g2_ref[...] * ms


def _mid(y, g, bonus, lnx_g, lnx_b, yb, x, g1, sc2, sh2, g2, n2g,
         wo_a, wo_b, wr_t, w13s, w2s, tm, tiles_per_batch):
    t, d = x.shape
    tok = lambda w: pl.BlockSpec((tm, w), lambda i: (i, 0))
    const = lambda shape: pl.BlockSpec(shape, lambda i: (0,) * len(shape))
    mod = pl.BlockSpec((None, 1, d), lambda i: (i // tiles_per_batch, 0, 0))
    return pl.pallas_call(
        _mid_kernel,
        grid=(t // tm,),
        in_specs=[tok(D_RWKV), tok(D_RWKV), tok(D_RWKV),
                  const((1, D_RWKV)), const((1, D_RWKV)),
                  tok(D_SGU), tok(d), mod, mod, mod, mod, const((1, d)),
                  const(wo_a.shape), const(wo_b.shape), const(wr_t.shape),
                  const(w13s.shape), const(w2s.shape)],
        out_specs=[tok(d), pl.BlockSpec((LANES, tm), lambda i: (0, i)), tok(d)],
        out_shape=[jax.ShapeDtypeStruct((t, d), BF16),
                   jax.ShapeDtypeStruct((LANES, t), F32),
                   jax.ShapeDtypeStruct((t, d), F32)],
        compiler_params=_params(("arbitrary",)),
        name="outproj_norm_router_shared",
    )(y, g, bonus, lnx_g, lnx_b, yb, x, g1, sc2, sh2, g2, n2g,
      wo_a, wo_b, wr_t, w13s, w2s)


def _route_kernel(logit_ref, bias_ref, gate_ref):
    tt = logit_ref.shape[1]
    scores = _sigmoid(logit_ref[0:N_EXPERTS, :])
    biased = scores + bias_ref[...]
    neg_inf = -jnp.inf

    group_blocks, group_scores = [], []
    for gi in range(N_GROUPS):
        blk = biased[GROUP_SIZE * gi:GROUP_SIZE * (gi + 1), :]
        m1 = jnp.max(blk, axis=0, keepdims=True)
        is_max = blk == m1
        n_max = jnp.sum(jnp.where(is_max, 1.0, 0.0), axis=0, keepdims=True)
        m2 = jnp.max(jnp.where(is_max, neg_inf, blk), axis=0, keepdims=True)
        group_blocks.append(blk)
        group_scores.append(m1 + jnp.where(n_max >= 2.0, m1, m2))

    masked = []
    for gi in range(N_GROUPS):
        rank = jnp.zeros((1, tt), F32)
        for gj in range(N_GROUPS):
            if gj == gi:
                continue
            ahead = (group_scores[gj] >= group_scores[gi]) if gj < gi else \
                    (group_scores[gj] > group_scores[gi])
            rank += jnp.where(ahead, 1.0, 0.0)
        masked.append(jnp.where(rank < float(TOPK_GROUPS), group_blocks[gi], neg_inf))
    sub = lax.broadcasted_iota(jnp.int32, (GROUP_SIZE, tt), 0)
    ranks = [jnp.zeros((GROUP_SIZE, tt), F32) for _ in range(N_GROUPS)]
    for ej in range(N_EXPERTS):
        gj, oj = divmod(ej, GROUP_SIZE)
        other = masked[gj][oj:oj + 1, :]
        for gi in range(N_GROUPS):
            blk = masked[gi]
            if gi > gj:
                ahead = jnp.where(other >= blk, 1.0, 0.0)
            elif gi < gj:
                ahead = jnp.where(other > blk, 1.0, 0.0)
            else:
                ahead = jnp.where(sub > oj, jnp.where(other >= blk, 1.0, 0.0),
                                  jnp.where(other > blk, 1.0, 0.0))
            ranks[gi] = ranks[gi] + ahead
    rank = jnp.concatenate(ranks, axis=0)
    sw = jnp.where(rank < float(TOP_K), scores, 0.0)
    gates = sw / jnp.sum(sw, axis=0, keepdims=True) * ROUTED_SCALE
    gates = jnp.concatenate([gates, jnp.zeros((LANES - N_EXPERTS, tt), F32)], axis=0)
    gate_ref[...] = gates.T


def _route(logits_t, bias_tile, tt):
    t = logits_t.shape[1]
    return pl.pallas_call(
        _route_kernel,
        grid=(t // tt,),
        in_specs=[pl.BlockSpec((LANES, tt), lambda i: (0, i)),
                  pl.BlockSpec((N_EXPERTS, tt), lambda i: (0, 0))],
        out_specs=pl.BlockSpec((tt, LANES), lambda i: (i, 0)),
        out_shape=jax.ShapeDtypeStruct((t, LANES), F32),
        compiler_params=_params(("arbitrary",)),
        name="route_topk",
    )(logits_t, bias_tile)


def _moe_kernel(h2_ref, gate_ref, w13_ref, w2_ref, base_ref, g2_ref, nf_ref, o_ref, acc_ref):
    e = pl.program_id(1)

    @pl.when(e == 0)
    def _():
        acc_ref[...] = jnp.zeros_like(acc_ref)

    sel = jnp.where(lax.broadcasted_iota(jnp.int32, (LANES, D_EXPERT), 0) == e,
                    1.0, 0.0).astype(BF16)
    gcol = _dot_exact_rhs(gate_ref[...], sel)
    hid = jnp.dot(h2_ref[...], w13_ref[...], preferred_element_type=F32)
    act = (_silu(hid[:, :D_EXPERT]) * hid[:, D_EXPERT:] * gcol).astype(BF16)
    acc_ref[...] += jnp.dot(act, w2_ref[...], preferred_element_type=F32)

    @pl.when(e == pl.num_programs(1) - 1)
    def _():
        xo = base_ref[...] + g2_ref[...] * acc_ref[...]
        o_ref[...] = (xo * lax.rsqrt(jnp.mean(xo * xo, axis=-1, keepdims=True) + RMS_EPS)
                      * nf_ref[...])


def _moe(h2, gates, w13e, w2e, base, g2, nf, tm, tiles_per_batch):
    t, d = base.shape
    ne = w13e.shape[0]
    tok = lambda w: pl.BlockSpec((tm, w), lambda i, e: (i, 0))
    return pl.pallas_call(
        _moe_kernel,
        grid=(t // tm, ne),
        in_specs=[tok(d), tok(LANES),
                  pl.BlockSpec((None, d, 2 * D_EXPERT), lambda i, e: (e, 0, 0)),
                  pl.BlockSpec((None, D_EXPERT, d), lambda i, e: (e, 0, 0)),
                  tok(d),
                  pl.BlockSpec((None, 1, d), lambda i, e: (i // tiles_per_batch, 0, 0)),
                  pl.BlockSpec((1, d), lambda i, e: (0, 0))],
        out_specs=tok(d),
        out_shape=jax.ShapeDtypeStruct((t, d), F32),
        scratch_shapes=[pltpu.VMEM((tm, d), F32)],
        compiler_params=_params(("arbitrary", "arbitrary")),
        name="moe_experts_final",
    )(h2, gates, w13e, w2e, base, g2, nf)


def _tile(n, want):
    t = min(n, want)
    assert n % t == 0, (n, t)
    return t


def kernel(x, c, w_ada, b_ada, norm1_g, w_in, mu_shift, w0, w_decay_up, a0, w_a_up, w_g_up,
           k_k, k_a, r_k, lnx_g, lnx_b, sgu_ln_g, sgu_ln_b, w_spatial, b_spatial, w_out,
           norm2_g, w_router, e_bias, w1_e, w3_e, w2_e, w1_s, w3_s, w2_s, norm_f_g):
    b, s, d = x.shape
    t = b * s
    assert w_ada.shape[0] == 1, "single-layer block"
    assert d == 2 * D_RWKV and s % SGU_BLOCK == 0
    row = lambda vec: vec.reshape(1, -1)

    c_pad = jnp.pad(c, ((0, 8 - b), (0, 0)))
    mod = _mod(c_pad, w_ada[0], b_ada)[:b]
    sh1, sc1, g1, sh2, sc2, g2 = [m[:, None, :] for m in jnp.split(mod, 6, axis=-1)]

    ps, pb = _inproj(x, sc1, sh1, norm1_g, w_in[0].astype(BF16), mu_shift, _tile(s, 512))

    wlora = jnp.zeros((LANES, 2 * D_RWKV), F32)
    wlora = wlora.at[:D_DECAY_LORA, :D_RWKV].set(w_decay_up[0])
    wlora = wlora.at[D_DECAY_LORA:, D_RWKV:].set(w_a_up[0])
    r, wdec, kh, v, kk, bb, g, bonus = _prep(
        ps.reshape(t, N_SHIFT), w0, a0, wlora, w_g_up[0], k_k, k_a, row(r_k), _tile(t, 512))

    seq = lambda z: z.reshape(b, s, D_RWKV)
    w_c, rh_c, u0_c, y0_c, bh_c, kh_c, pl_c = _chunk(r, wdec, kh, v, kk, bb)
    y = _rwkv(seq(w_c), seq(rh_c), seq(u0_c), seq(y0_c), seq(bh_c), seq(kh_c), seq(v),
              seq(pl_c), _tile(s, 256))

    bias_tile = jnp.repeat(b_spatial[0].T, HEAD, axis=1)
    yb = _sgu(pb.reshape(t, 2 * D_SGU), sgu_ln_g, sgu_ln_b, w_spatial[0], bias_tile,
              _tile(t, 512))

    tm_mid = _tile(s, 512)
    wr_t = jnp.pad(w_router[0].T, ((0, LANES - N_EXPERTS), (0, 0)))
    w13s = jnp.concatenate([w1_s[0], w3_s[0]], axis=1).astype(BF16)
    wo = w_out[0].astype(BF16)
    h2, logits_t, base = _mid(
        y.reshape(t, D_RWKV), g, bonus, lnx_g, lnx_b, yb, x.reshape(t, d),
        g1, sc2, sh2, g2, norm2_g, wo[:D_RWKV], wo[D_RWKV:], wr_t, w13s,
        w2_s[0].astype(BF16), tm_mid, s // tm_mid)

    tt_route = _tile(t, 1024)
    gates = _route(logits_t, jnp.broadcast_to(e_bias[0][:, None], (N_EXPERTS, tt_route)),
                   tt_route)

    tm_moe = _tile(s, 1024)
    w13e = jnp.concatenate([w1_e[0], w3_e[0]], axis=2).astype(BF16)
    out = _moe(h2, gates, w13e, w2_e[0].astype(BF16), base, g2, row(norm_f_g),
               tm_moe, s // tm_moe)
    return out.reshape(b, s, d)
```

```python
import functools

import jax
import jax.numpy as jnp
from jax import lax
from jax.experimental import pallas as pl
from jax.experimental.pallas import tpu as pltpu

F32 = jnp.float32
BF16 = jnp.bfloat16

HEAD = 64
LANES = 128
SUBLANES = 8
D_RWKV = 512
D_SGU = 512
N_RWKV_PAIRS = D_RWKV // LANES
D_DECAY_LORA = 64
D_AAA_LORA = 64
D_GATE_LORA = 128
N_SHIFT = 3 * D_RWKV + D_DECAY_LORA + D_AAA_LORA + D_GATE_LORA
SGU_BLOCK = 128
CHUNK = 64
N_EXPERTS = 64
N_GROUPS = 8
GROUP_SIZE = N_EXPERTS // N_GROUPS
TOPK_GROUPS = 4
TOP_K = 8
D_EXPERT = 256
ROUTED_SCALE = 2.5
RMS_EPS = 1e-6
LN_EPS = 1e-5
LN_X_EPS = 64e-5
VMEM_LIMIT_BYTES = 56 * 1024 * 1024


def _params(semantics):
    return pltpu.CompilerParams(dimension_semantics=semantics,
                                vmem_limit_bytes=VMEM_LIMIT_BYTES)


def _split3(x):
    hi = x.astype(BF16)
    r1 = x - hi.astype(F32)
    mid = r1.astype(BF16)
    lo = (r1 - mid.astype(F32)).astype(BF16)
    return hi, mid, lo


def _dot_exact_rhs(x, rhs_bf16):
    hi, mid, lo = _split3(x)
    acc = jnp.dot(hi, rhs_bf16, preferred_element_type=F32)
    acc += jnp.dot(mid, rhs_bf16, preferred_element_type=F32)
    acc += jnp.dot(lo, rhs_bf16, preferred_element_type=F32)
    return acc


def _dot3(a, b, dims=(((1,), (0,)), ((), ()))):
    a_hi = a.astype(BF16)
    a_lo = (a - a_hi.astype(F32)).astype(BF16)
    b_hi = b.astype(BF16)
    b_lo = (b - b_hi.astype(F32)).astype(BF16)
    dg = functools.partial(lax.dot_general, dimension_numbers=dims,
                           preferred_element_type=F32)
    return dg(a_hi, b_hi) + dg(a_hi, b_lo) + dg(a_lo, b_hi)


def _head_ones():
    r = lax.broadcasted_iota(jnp.int32, (LANES, LANES), 0) // HEAD
    c = lax.broadcasted_iota(jnp.int32, (LANES, LANES), 1) // HEAD
    return jnp.where(r == c, 1.0, 0.0).astype(BF16)


def _head_sum(z, ones):
    parts = [_dot_exact_rhs(z[:, LANES * p:LANES * (p + 1)], ones)
             for p in range(z.shape[1] // LANES)]
    return jnp.concatenate(parts, axis=-1)


def _sigmoid(x):
    return 1.0 / (1.0 + jnp.exp(-x))


def _silu(x):
    return x * _sigmoid(x)


def _mod_kernel(c_ref, w_ref, b_ref, o_ref):
    s = _silu(c_ref[...])
    o_ref[...] = _dot3(s, w_ref[...]) + b_ref[...]


def _mod(c_pad, w_ada, b_ada):
    rows, d = c_pad.shape
    n = w_ada.shape[1]
    tn = 1024
    return pl.pallas_call(
        _mod_kernel,
        grid=(n // tn,),
        in_specs=[pl.BlockSpec((rows, d), lambda j: (0, 0)),
                  pl.BlockSpec((d, tn), lambda j: (0, j)),
                  pl.BlockSpec((1, tn), lambda j: (0, j))],
        out_specs=pl.BlockSpec((rows, tn), lambda j: (0, j)),
        out_shape=jax.ShapeDtypeStruct((rows, n), F32),
        compiler_params=_params(("arbitrary",)),
        name="adaln_mod",
    )(c_pad, w_ada, b_ada)


def _inproj_kernel(x_ref, sc_ref, sh_ref, g_ref, w_ref, mu_ref, ps_ref, pb_ref, carry_ref):
    @pl.when(pl.program_id(1) == 0)
    def _():
        carry_ref[...] = jnp.zeros_like(carry_ref)

    x = x_ref[...]
    tm = x.shape[0]
    h = x * lax.rsqrt(jnp.mean(x * x, axis=-1, keepdims=True) + RMS_EPS) * g_ref[...]
    h = h * (1.0 + sc_ref[...]) + sh_ref[...]
    proj = jnp.dot(h.astype(BF16), w_ref[...], preferred_element_type=F32)
    ps = proj[:, :N_SHIFT]
    prev = pltpu.roll(ps, 1, 0)
    first = lax.broadcasted_iota(jnp.int32, ps.shape, 0) == 0
    prev = jnp.where(first, carry_ref[0:1, :], prev)
    carry_ref[0:1, :] = ps[tm - 1:tm, :]
    ps_ref[...] = ps + (prev - ps) * mu_ref[...]
    pb_ref[...] = proj[:, N_SHIFT:]


def _inproj(x, sc1, sh1, g, w_in_bf16, mu, tm):
    b, s, d = x.shape
    d_in = w_in_bf16.shape[1]
    row = lambda bi, i: (bi, 0, 0)
    return pl.pallas_call(
        _inproj_kernel,
        grid=(b, s // tm),
        in_specs=[pl.BlockSpec((None, tm, d), lambda bi, i: (bi, i, 0)),
                  pl.BlockSpec((None, 1, d), row),
                  pl.BlockSpec((None, 1, d), row),
                  pl.BlockSpec((1, d), lambda bi, i: (0, 0)),
                  pl.BlockSpec((d, d_in), lambda bi, i: (0, 0)),
                  pl.BlockSpec((1, N_SHIFT), lambda bi, i: (0, 0))],
        out_specs=[pl.BlockSpec((None, tm, N_SHIFT), lambda bi, i: (bi, i, 0)),
                   pl.BlockSpec((None, tm, d_in - N_SHIFT), lambda bi, i: (bi, i, 0))],
        out_shape=[jax.ShapeDtypeStruct((b, s, N_SHIFT), F32),
                   jax.ShapeDtypeStruct((b, s, d_in - N_SHIFT), F32)],
        scratch_shapes=[pltpu.VMEM((8, N_SHIFT), F32)],
        compiler_params=_params(("arbitrary", "arbitrary")),
        name="inproj_shift",
    )(x, sc1, sh1, g, w_in_bf16, mu)


def _prep_kernel(ps_ref, w0_ref, a0_ref, wlora_ref, wg_ref, kk_ref, ka_ref, rk_ref,
                 r_o, w_o, k_o, v_o, kk_o, b_o, g_o, bonus_o):
    ps = ps_ref[...]
    c1, c2, c3 = D_RWKV, 2 * D_RWKV, 3 * D_RWKV
    r, k, v = ps[:, :c1], ps[:, c1:c2], ps[:, c2:c3]
    xwa = ps[:, c3:c3 + LANES]
    xg = ps[:, c3 + LANES:]
    is_w = lax.broadcasted_iota(jnp.int32, xwa.shape, 1) < D_DECAY_LORA
    lora = _dot3(jnp.where(is_w, jnp.tanh(xwa), xwa), wlora_ref[...])
    wlog = -jax.nn.softplus(-(w0_ref[...] + lora[:, :c1])) - 0.5
    log_decay = -jnp.exp(wlog)
    a = _sigmoid(a0_ref[...] + lora[:, c1:])
    g = _dot3(_sigmoid(xg), wg_ref[...])
    ones = _head_ones()
    kk = k * kk_ref[...]
    kk = kk / jnp.maximum(jnp.sqrt(_head_sum(kk * kk, ones)), 1e-12)
    kh = k * (1.0 + (a - 1.0) * ka_ref[...])
    r_o[...] = r
    w_o[...] = log_decay
    k_o[...] = kh
    v_o[...] = v
    kk_o[...] = kk
    b_o[...] = kk * a
    g_o[...] = g
    bonus_o[...] = _head_sum(r * kh * rk_ref[...], ones) * v


def _prep(ps, w0, a0, wlora, wg, k_k, k_a, r_k, tm):
    t = ps.shape[0]
    const = lambda shape: pl.BlockSpec(shape, lambda i: (0, 0))
    out = jax.ShapeDtypeStruct((t, D_RWKV), F32)
    return pl.pallas_call(
        _prep_kernel,
        grid=(t // tm,),
        in_specs=[pl.BlockSpec((tm, N_SHIFT), lambda i: (i, 0)),
                  const((1, D_RWKV)), const((1, D_RWKV)),
                  const((LANES, 2 * D_RWKV)), const((D_GATE_LORA, D_RWKV)),
                  const((1, D_RWKV)), const((1, D_RWKV)), const((1, D_RWKV))],
        out_specs=[pl.BlockSpec((tm, D_RWKV), lambda i: (i, 0))] * 8,
        out_shape=[out] * 8,
        compiler_params=_params(("arbitrary",)),
        name="rwkv_prep",
    )(ps, w0, a0, wlora, wg, k_k, k_a, r_k)


RWKV_CHUNK = 16
CHUNK_TILE = 128

_NN = (((1,), (0,)), ((), ()))
_NT = (((1,), (1,)), ((), ()))


def _dot1(a, b, dims=_NN):
    return lax.dot_general(a.astype(BF16), b.astype(BF16), dims, preferred_element_type=F32)


def _dot_exact_lhs(lhs_bf16, x):
    hi, mid, lo = _split3(x)
    acc = jnp.dot(lhs_bf16, hi, preferred_element_type=F32)
    acc += jnp.dot(lhs_bf16, mid, preferred_element_type=F32)
    acc += jnp.dot(lhs_bf16, lo, preferred_element_type=F32)
    return acc


def _chunk_kernel(r_ref, lw_ref, k_ref, v_ref, kk_ref, b_ref,
                  w_o, rh_o, u0_o, y0_o, bh_o, kh_o, pl_o):
    n = CHUNK_TILE
    tpos = lax.broadcasted_iota(jnp.int32, (n, n), 0)
    spos = lax.broadcasted_iota(jnp.int32, (n, n), 1)
    same = (tpos // RWKV_CHUNK) == (spos // RWKV_CHUNK)
    incl = same & (spos <= tpos)
    strict = same & (spos < tpos)
    tri_ones = jnp.where(incl, 1.0, 0.0).astype(BF16)
    blk_ones = jnp.where(same, 1.0, 0.0).astype(BF16)
    eye = jnp.where(tpos == spos, 1.0, 0.0)
    lane_head = lax.broadcasted_iota(jnp.int32, (n, LANES), 1) // HEAD

    pairs = range(N_RWKV_PAIRS)
    slabs = [slice(LANES * p, LANES * (p + 1)) for p in pairs]
    lw, r, k, v, kk, b = ([ref[:, sl] for sl in slabs]
                          for ref in (lw_ref, r_ref, k_ref, v_ref, kk_ref, b_ref))
    cum = [_dot_exact_lhs(tri_ones, x) for x in lw]
    tot = [_dot_exact_lhs(blk_ones, x) for x in lw]
    at = [-kk[p] * jnp.exp(cum[p] - lw[p]) for p in pairs]
    rt = [r[p] * jnp.exp(cum[p]) for p in pairs]
    inv = [jnp.exp(-cum[p]) for p in pairs]
    bt = [b[p] * inv[p] for p in pairs]
    kt = [k[p] * inv[p] for p in pairs]

    chains = [(p, h) for p in pairs for h in range(LANES // HEAD)]
    mine = [lane_head == h for h in range(LANES // HEAD)]
    a_h = [jnp.where(mine[h], at[p], 0.0) for p, h in chains]
    r_h = [jnp.where(mine[h], rt[p], 0.0) for p, h in chains]
    v_h = [jnp.where(mine[h], v[p], 0.0) for p, h in chains]
    each = range(len(chains))
    a_ab = [jnp.where(strict, _dot3(a_h[c], bt[chains[c][0]], _NT), 0.0) for c in each]
    a_ak = [jnp.where(strict, _dot3(a_h[c], kt[chains[c][0]], _NT), 0.0) for c in each]
    tinv = [eye + a for a in a_ab]
    x = a_ab
    for _ in range(RWKV_CHUNK.bit_length() - 2):
        x = [_dot3(xc, xc) for xc in x]
        tinv = [tc + _dot3(tc, xc) for tc, xc in zip(tinv, x)]
    akv = [_dot3(a_ak[c], v_h[c]) for c in each]
    m_rb = [jnp.where(incl, _dot1(r_h[c], bt[chains[c][0]], _NT), 0.0) for c in each]
    m_rk = [jnp.where(incl, _dot1(r_h[c], kt[chains[c][0]], _NT), 0.0) for c in each]
    w_h = [_dot3(tinv[c], a_h[c]) for c in each]
    u0_h = [_dot3(tinv[c], akv[c]) for c in each]
    rh_h = [r_h[c] + _dot1(m_rb[c], w_h[c]) for c in each]
    y0_h = [_dot1(m_rb[c], u0_h[c]) + _dot1(m_rk[c], v_h[c]) for c in each]

    for p in pairs:
        c0, c1 = 2 * p, 2 * p + 1
        rem = jnp.exp(tot[p] - cum[p])
        w_o[:, slabs[p]] = w_h[c0] + w_h[c1]
        rh_o[:, slabs[p]] = rh_h[c0] + rh_h[c1]
        u0_o[:, slabs[p]] = u0_h[c0] + u0_h[c1]
        y0_o[:, slabs[p]] = y0_h[c0] + y0_h[c1]
        bh_o[:, slabs[p]] = b[p] * rem
        kh_o[:, slabs[p]] = k[p] * rem
        pl_o[:, slabs[p]] = jnp.exp(tot[p])


def _chunk(r, lw, k, v, kk, b):
    t, d = r.shape
    spec = pl.BlockSpec((CHUNK_TILE, d), lambda i: (i, 0))
    out = jax.ShapeDtypeStruct((t, d), F32)
    return pl.pallas_call(
        _chunk_kernel,
        grid=(t // CHUNK_TILE,),
        in_specs=[spec] * 6,
        out_specs=[spec] * 7,
        out_shape=[out] * 7,
        compiler_params=_params(("arbitrary",)),
        name="rwkv_chunk_prep",
    )(r, lw, k, v, kk, b)


def _rwkv_kernel(w_ref, rh_ref, u0_ref, y0_ref, bh_ref, kh_ref, v_ref, pl_ref,
                 y_ref, state_ref):
    @pl.when(pl.program_id(0) == 0)
    def _():
        state_ref[...] = jnp.zeros_like(state_ref)

    nb, tt, _ = w_ref.shape
    L = RWKV_CHUNK
    row_head = lax.broadcasted_iota(jnp.int32, (LANES, LANES), 0) // HEAD
    col_head = lax.broadcasted_iota(jnp.int32, (LANES, LANES), 1) // HEAD
    same_head = row_head == col_head
    eye = (lax.broadcasted_iota(jnp.int32, (LANES, LANES), 0)
           == lax.broadcasted_iota(jnp.int32, (LANES, LANES), 1))
    all_ones = jnp.ones((LANES, LANES), BF16)
    pad = jnp.zeros((LANES - 2 * L, LANES), F32)

    def column_of(p_row):
        p_hi = pltpu.bitcast(pltpu.bitcast(p_row, jnp.uint32) & jnp.uint32(0xFFFF0000), F32)
        return (jnp.dot(jnp.where(eye, p_hi, 0.0).astype(BF16), all_ones,
                        preferred_element_type=F32)
                + jnp.dot(jnp.where(eye, p_row - p_hi, 0.0).astype(BF16), all_ones,
                          preferred_element_type=F32))

    def chunk(ci, carry):
        t0 = pl.multiple_of(ci * L, L)
        streams = [(bi, p) for bi in range(nb) for p in range(N_RWKV_PAIRS)]

        def rows(ref, n=L):
            return [ref[bi, pl.ds(t0, n), LANES * p:LANES * (p + 1)] for bi, p in streams]

        each = range(len(streams))
        s = [state_ref[bi * N_RWKV_PAIRS + p] for bi, p in streams]
        w, rh, u0, y0, bh, kh, v = (rows(ref) for ref in
                                   (w_ref, rh_ref, u0_ref, y0_ref, bh_ref, kh_ref, v_ref))
        p_col = [column_of(pr[0:1, :]) for pr in rows(pl_ref, SUBLANES)]
        bk_t = [jnp.concatenate([bh[c], kh[c], pad], axis=0).T for c in each]
        g = [_dot3(jnp.concatenate([w[c], rh[c]], axis=0), s[c]) for c in each]
        u = [g[c][:L] + u0[c] for c in each]
        upd = [_dot3(bk_t[c], jnp.concatenate([u[c], v[c], pad], axis=0)) for c in each]
        for c, (bi, p) in enumerate(streams):
            y_ref[bi, pl.ds(t0, L), LANES * p:LANES * (p + 1)] = g[c][L:] + y0[c]
            state_ref[bi * N_RWKV_PAIRS + p] = (p_col[c] * s[c]
                                                + jnp.where(same_head, upd[c], 0.0))
        return carry

    lax.fori_loop(0, tt // L, chunk, 0)


def _rwkv(w, rh, u0, y0, bh, kh, v, p_l, tt):
    nb, s, d = w.shape
    spec = pl.BlockSpec((nb, tt, d), lambda i: (0, i, 0))
    return pl.pallas_call(
        _rwkv_kernel,
        grid=(s // tt,),
        in_specs=[spec] * 8,
        out_specs=spec,
        out_shape=jax.ShapeDtypeStruct((nb, s, d), F32),
        scratch_shapes=[pltpu.VMEM((nb * N_RWKV_PAIRS, LANES, LANES), F32)],
        compiler_params=_params(("arbitrary",)),
        name="rwkv_scan",
    )(w, rh, u0, y0, bh, kh, v, p_l)


def _gelu_tanh(x):
    return 0.5 * x * (1.0 + jnp.tanh(0.7978845608028654 * (x + 0.044715 * (x * x * x))))


def _sgu_kernel(pb_ref, g_ref, b_ref, ws_ref, bias_ref, o_ref):
    z = _gelu_tanh(pb_ref[...])
    u, v = z[:, :D_SGU], z[:, D_SGU:]
    mu = jnp.mean(v, axis=-1, keepdims=True)
    vc = v - mu
    var = jnp.mean(vc * vc, axis=-1, keepdims=True)
    vn = (vc * lax.rsqrt(var + LN_EPS) * g_ref[...] + b_ref[...]).astype(BF16)
    tpos = lax.broadcasted_iota(jnp.int32, (SGU_BLOCK, SGU_BLOCK), 0) // CHUNK
    spos = lax.broadcasted_iota(jnp.int32, (SGU_BLOCK, SGU_BLOCK), 1) // CHUNK
    causal = spos <= tpos
    first_head = lax.broadcasted_iota(jnp.int32, (SGU_BLOCK, LANES), 1) < HEAD
    for p in range(D_SGU // LANES):
        sl = slice(LANES * p, LANES * (p + 1))
        ws0 = jnp.where(causal, ws_ref[2 * p], 0.0).astype(BF16)
        ws1 = jnp.where(causal, ws_ref[2 * p + 1], 0.0).astype(BF16)
        for n in range(pb_ref.shape[0] // SGU_BLOCK):
            rows = slice(SGU_BLOCK * n, SGU_BLOCK * (n + 1))
            vb = vn[rows, sl]
            sp = jnp.where(first_head,
                           jnp.dot(ws0, vb, preferred_element_type=F32),
                           jnp.dot(ws1, vb, preferred_element_type=F32))
            o_ref[rows, sl] = (u[rows, sl] * (sp + bias_ref[:, sl])).astype(o_ref.dtype)


def _sgu(pb, ln_g, ln_b, w_spatial, bias_tile, tm):
    t = pb.shape[0]
    nh = w_spatial.shape[0]
    return pl.pallas_call(
        _sgu_kernel,
        grid=(t // tm,),
        in_specs=[pl.BlockSpec((tm, 2 * D_SGU), lambda i: (i, 0)),
                  pl.BlockSpec((1, D_SGU), lambda i: (0, 0)),
                  pl.BlockSpec((1, D_SGU), lambda i: (0, 0)),
                  pl.BlockSpec((nh, SGU_BLOCK, SGU_BLOCK), lambda i: (0, 0, 0)),
                  pl.BlockSpec((SGU_BLOCK, D_SGU), lambda i: (0, 0))],
        out_specs=pl.BlockSpec((tm, D_SGU), lambda i: (i, 0)),
        out_shape=jax.ShapeDtypeStruct((t, D_SGU), BF16),
        compiler_params=_params(("arbitrary",)),
        name="sgu_mix",
    )(pb, ln_g, ln_b, w_spatial, bias_tile)


def _mid_kernel(y_ref, g_ref, bonus_ref, lng_ref, lnb_ref, yb_ref, x_ref,
                g1_ref, sc2_ref, sh2_ref, g2_ref, n2_ref,
                woa_ref, wob_ref, wrt_ref, w13_ref, w2_ref,
                h2_ref, logit_ref, base_ref):
    ones = _head_ones()
    y = y_ref[...]
    mu = _head_sum(y, ones) * (1.0 / HEAD)
    yc = y - mu
    var = _head_sum(yc * yc, ones) * (1.0 / HEAD)
    ya = yc * lax.rsqrt(var + LN_X_EPS) * lng_ref[...] + lnb_ref[...]
    ya = ((ya + bonus_ref[...]) * g_ref[...]).astype(BF16)
    yo = (jnp.dot(ya, woa_ref[...], preferred_element_type=F32)
          + jnp.dot(yb_ref[...], wob_ref[...], preferred_element_type=F32))
    x1 = x_ref[...] + g1_ref[...] * yo
    h2 = x1 * lax.rsqrt(jnp.mean(x1 * x1, axis=-1, keepdims=True) + RMS_EPS) * n2_ref[...]
    h2 = h2 * (1.0 + sc2_ref[...]) + sh2_ref[...]
    h2b = h2.astype(BF16)
    h2_ref[...] = h2b
    logit_ref[...] = _dot3(wrt_ref[...], h2, dims=(((1,), (1,)), ((), ())))
    hid = jnp.dot(h2b, w13_ref[...], preferred_element_type=F32)
    act = (_silu(hid[:, :D_EXPERT]) * hid[:, D_EXPERT:]).astype(BF16)
    ms = jnp.dot(act, w2_ref[...], preferred_element_type=F32)
    base_ref[...] = x1 + g2_ref[...] * ms


def _mid(y, g, bonus, lnx_g, lnx_b, yb, x, g1, sc2, sh2, g2, n2g,
         wo_a, wo_b, wr_t, w13s, w2s, tm, tiles_per_batch):
    t, d = x.shape
    tok = lambda w: pl.BlockSpec((tm, w), lambda i: (i, 0))
    const = lambda shape: pl.BlockSpec(shape, lambda i: (0,) * len(shape))
    mod = pl.BlockSpec((None, 1, d), lambda i: (i // tiles_per_batch, 0, 0))
    return pl.pallas_call(
        _mid_kernel,
        grid=(t // tm,),
        in_specs=[tok(D_RWKV), tok(D_RWKV), tok(D_RWKV),
                  const((1, D_RWKV)), const((1, D_RWKV)),
                  tok(D_SGU), tok(d), mod, mod, mod, mod, const((1, d)),
                  const(wo_a.shape), const(wo_b.shape), const(wr_t.shape),
                  const(w13s.shape), const(w2s.shape)],
        out_specs=[tok(d), pl.BlockSpec((LANES, tm), lambda i: (0, i)), tok(d)],
        out_shape=[jax.ShapeDtypeStruct((t, d), BF16),
                   jax.ShapeDtypeStruct((LANES, t), F32),
                   jax.ShapeDtypeStruct((t, d), F32)],
        compiler_params=_params(("arbitrary",)),
        name="outproj_norm_router_shared",
    )(y, g, bonus, lnx_g, lnx_b, yb, x, g1, sc2, sh2, g2, n2g,
      wo_a, wo_b, wr_t, w13s, w2s)


def _route_kernel(logit_ref, bias_ref, gate_ref):
    tt = logit_ref.shape[1]
    scores = _sigmoid(logit_ref[0:N_EXPERTS, :])
    biased = scores + bias_ref[...]
    neg_inf = -jnp.inf

    group_blocks, group_scores = [], []
    for gi in range(N_GROUPS):
        blk = biased[GROUP_SIZE * gi:GROUP_SIZE * (gi + 1), :]
        m1 = jnp.max(blk, axis=0, keepdims=True)
        is_max = blk == m1
        n_max = jnp.sum(jnp.where(is_max, 1.0, 0.0), axis=0, keepdims=True)
        m2 = jnp.max(jnp.where(is_max, neg_inf, blk), axis=0, keepdims=True)
        group_blocks.append(blk)
        group_scores.append(m1 + jnp.where(n_max >= 2.0, m1, m2))

    masked = []
    for gi in range(N_GROUPS):
        rank = jnp.zeros((1, tt), F32)
        for gj in range(N_GROUPS):
            if gj == gi:
                continue
            ahead = (group_scores[gj] >= group_scores[gi]) if gj < gi else \
                    (group_scores[gj] > group_scores[gi])
            rank += jnp.where(ahead, 1.0, 0.0)
        masked.append(jnp.where(rank < float(TOPK_GROUPS), group_blocks[gi], neg_inf))
    sub = lax.broadcasted_iota(jnp.int32, (GROUP_SIZE, tt), 0)
    ranks = [jnp.zeros((GROUP_SIZE, tt), F32) for _ in range(N_GROUPS)]
    for ej in range(N_EXPERTS):
        gj, oj = divmod(ej, GROUP_SIZE)
        other = masked[gj][oj:oj + 1, :]
        for gi in range(N_GROUPS):
            blk = masked[gi]
            if gi > gj:
                ahead = jnp.where(other >= blk, 1.0, 0.0)
            elif gi < gj:
                ahead = jnp.where(other > blk, 1.0, 0.0)
            else:
                ahead = jnp.where(sub > oj, jnp.where(other >= blk, 1.0, 0.0),
                                  jnp.where(other > blk, 1.0, 0.0))
            ranks[gi] = ranks[gi] + ahead
    rank = jnp.concatenate(ranks, axis=0)
    sw = jnp.where(rank < float(TOP_K), scores, 0.0)
    gates = sw / jnp.sum(sw, axis=0, keepdims=True) * ROUTED_SCALE
    gates = jnp.concatenate([gates, jnp.zeros((LANES - N_EXPERTS, tt), F32)], axis=0)
    gate_ref[...] = gates.T


def _route(logits_t, bias_tile, tt):
    t = logits_t.shape[1]
    return pl.pallas_call(
        _route_kernel,
        grid=(t // tt,),
        in_specs=[pl.BlockSpec((LANES, tt), lambda i: (0, i)),
                  pl.BlockSpec((N_EXPERTS, tt), lambda i: (0, 0))],
        out_specs=pl.BlockSpec((tt, LANES), lambda i: (i, 0)),
        out_shape=jax.ShapeDtypeStruct((t, LANES), F32),
        compiler_params=_params(("arbitrary",)),
        name="route_topk",
    )(logits_t, bias_tile)


def _moe_kernel(h2_ref, gate_ref, w13_ref, w2_ref, base_ref, g2_ref, nf_ref, o_ref, acc_ref):
    e = pl.program_id(1)

    @pl.when(e == 0)
    def _():
        acc_ref[...] = jnp.zeros_like(acc_ref)

    sel = jnp.where(lax.broadcasted_iota(jnp.int32, (LANES, D_EXPERT), 0) == e,
                    1.0, 0.0).astype(BF16)
    gcol = _dot_exact_rhs(gate_ref[...], sel)
    hid = jnp.dot(h2_ref[...], w13_ref[...], preferred_element_type=F32)
    act = (_silu(hid[:, :D_EXPERT]) * hid[:, D_EXPERT:] * gcol).astype(BF16)
    acc_ref[...] += jnp.dot(act, w2_ref[...], preferred_element_type=F32)

    @pl.when(e == pl.num_programs(1) - 1)
    def _():
        xo = base_ref[...] + g2_ref[...] * acc_ref[...]
        o_ref[...] = (xo * lax.rsqrt(jnp.mean(xo * xo, axis=-1, keepdims=True) + RMS_EPS)
                      * nf_ref[...])


def _moe(h2, gates, w13e, w2e, base, g2, nf, tm, tiles_per_batch):
    t, d = base.shape
    ne = w13e.shape[0]
    tok = lambda w: pl.BlockSpec((tm, w), lambda i, e: (i, 0))
    return pl.pallas_call(
        _moe_kernel,
        grid=(t // tm, ne),
        in_specs=[tok(d), tok(LANES),
                  pl.BlockSpec((None, d, 2 * D_EXPERT), lambda i, e: (e, 0, 0)),
                  pl.BlockSpec((None, D_EXPERT, d), lambda i, e: (e, 0, 0)),
                  tok(d),
                  pl.BlockSpec((None, 1, d), lambda i, e: (i // tiles_per_batch, 0, 0)),
                  pl.BlockSpec((1, d), lambda i, e: (0, 0))],
        out_specs=tok(d),
        out_shape=jax.ShapeDtypeStruct((t, d), F32),
        scratch_shapes=[pltpu.VMEM((tm, d), F32)],
        compiler_params=_params(("arbitrary", "arbitrary")),
        name="moe_experts_final",
    )(h2, gates, w13e, w2e, base, g2, nf)


def _tile(n, want):
    t = min(n, want)
    assert n % t == 0, (n, t)
    return t


def kernel(x, c, w_ada, b_ada, norm1_g, w_in, mu_shift, w0, w_decay_up, a0, w_a_up, w_g_up,
           k_k, k_a, r_k, lnx_g, lnx_b, sgu_ln_g, sgu_ln_b, w_spatial, b_spatial, w_out,
           norm2_g, w_router, e_bias, w1_e, w3_e, w2_e, w1_s, w3_s, w2_s, norm_f_g):
    b, s, d = x.shape
    t = b * s
    assert w_ada.shape[0] == 1, "single-layer block"
    assert d == 2 * D_RWKV and s % SGU_BLOCK == 0
    row = lambda vec: vec.reshape(1, -1)

    c_pad = jnp.pad(c, ((0, 8 - b), (0, 0)))
    mod = _mod(c_pad, w_ada[0], b_ada)[:b]
    sh1, sc1, g1, sh2, sc2, g2 = [m[:, None, :] for m in jnp.split(mod, 6, axis=-1)]

    ps, pb = _inproj(x, sc1, sh1, norm1_g, w_in[0].astype(BF16), mu_shift, _tile(s, 512))

    wlora = jnp.zeros((LANES, 2 * D_RWKV), F32)
    wlora = wlora.at[:D_DECAY_LORA, :D_RWKV].set(w_decay_up[0])
    wlora = wlora.at[D_DECAY_LORA:, D_RWKV:].set(w_a_up[0])
    r, wdec, kh, v, kk, bb, g, bonus = _prep(
        ps.reshape(t, N_SHIFT), w0, a0, wlora, w_g_up[0], k_k, k_a, row(r_k), _tile(t, 512))

    seq = lambda z: z.reshape(b, s, D_RWKV)
    w_c, rh_c, u0_c, y0_c, bh_c, kh_c, pl_c = _chunk(r, wdec, kh, v, kk, bb)
    y = _rwkv(seq(w_c), seq(rh_c), seq(u0_c), seq(y0_c), seq(bh_c), seq(kh_c), seq(v),
              seq(pl_c), _tile(s, 256))

    bias_tile = jnp.repeat(b_spatial[0].T, HEAD, axis=1)
    yb = _sgu(pb.reshape(t, 2 * D_SGU), sgu_ln_g, sgu_ln_b, w_spatial[0], bias_tile,
              _tile(t, 512))

    tm_mid = _tile(s, 512)
    wr_t = jnp.pad(w_router[0].T, ((0, LANES - N_EXPERTS), (0, 0)))
    w13s = jnp.concatenate([w1_s[0], w3_s[0]], axis=1).astype(BF16)
    wo = w_out[0].astype(BF16)
    h2, logits_t, base = _mid(
        y.reshape(t, D_RWKV), g, bonus, lnx_g, lnx_b, yb, x.reshape(t, d),
        g1, sc2, sh2, g2, norm2_g, wo[:D_RWKV], wo[D_RWKV:], wr_t, w13s,
        w2_s[0].astype(BF16), tm_mid, s // tm_mid)

    tt_route = _tile(t, 1024)
    gates = _route(logits_t, jnp.broadcast_to(e_bias[0][:, None], (N_EXPERTS, tt_route)),
                   tt_route)

    tm_moe = _tile(s, 1024)
    w13e = jnp.concatenate([w1_e[0], w3_e[0]], axis=2).astype(BF16)
    out = _moe(h2, gates, w13e, w2_e[0].astype(BF16), base, g2, row(norm_f_g),
               tm_moe, s // tm_moe)
    return out.reshape(b, s, d)
```

```python
import functools

import jax
import jax.numpy as jnp
from jax import lax
from jax.experimental import pallas as pl
from jax.experimental.pallas import tpu as pltpu

F32 = jnp.float32
BF16 = jnp.bfloat16

HEAD = 64
LANES = 128
SUBLANES = 8
D_RWKV = 512
D_SGU = 512
N_RWKV_PAIRS = D_RWKV // LANES
D_DECAY_LORA = 64
D_AAA_LORA = 64
D_GATE_LORA = 128
N_SHIFT = 3 * D_RWKV + D_DECAY_LORA + D_AAA_LORA + D_GATE_LORA
SGU_BLOCK = 128
CHUNK = 64
N_EXPERTS = 64
N_GROUPS = 8
GROUP_SIZE = N_EXPERTS // N_GROUPS
TOPK_GROUPS = 4
TOP_K = 8
D_EXPERT = 256
ROUTED_SCALE = 2.5
RMS_EPS = 1e-6
LN_EPS = 1e-5
LN_X_EPS = 64e-5
VMEM_LIMIT_BYTES = 56 * 1024 * 1024


def _params(semantics):
    return pltpu.CompilerParams(dimension_semantics=semantics,
                                vmem_limit_bytes=VMEM_LIMIT_BYTES)


def _split3(x):
    hi = x.astype(BF16)
    r1 = x - hi.astype(F32)
    mid = r1.astype(BF16)
    lo = (r1 - mid.astype(F32)).astype(BF16)
    return hi, mid, lo


def _dot_exact_rhs(x, rhs_bf16):
    hi, mid, lo = _split3(x)
    acc = jnp.dot(hi, rhs_bf16, preferred_element_type=F32)
    acc += jnp.dot(mid, rhs_bf16, preferred_element_type=F32)
    acc += jnp.dot(lo, rhs_bf16, preferred_element_type=F32)
    return acc


def _dot3(a, b, dims=(((1,), (0,)), ((), ()))):
    a_hi = a.astype(BF16)
    a_lo = (a - a_hi.astype(F32)).astype(BF16)
    b_hi = b.astype(BF16)
    b_lo = (b - b_hi.astype(F32)).astype(BF16)
    dg = functools.partial(lax.dot_general, dimension_numbers=dims,
                           preferred_element_type=F32)
    return dg(a_hi, b_hi) + dg(a_hi, b_lo) + dg(a_lo, b_hi)


def _head_ones():
    r = lax.broadcasted_iota(jnp.int32, (LANES, LANES), 0) // HEAD
    c = lax.broadcasted_iota(jnp.int32, (LANES, LANES), 1) // HEAD
    return jnp.where(r == c, 1.0, 0.0).astype(BF16)


def _head_sum(z, ones):
    parts = [_dot_exact_rhs(z[:, LANES * p:LANES * (p + 1)], ones)
             for p in range(z.shape[1] // LANES)]
    return jnp.concatenate(parts, axis=-1)


def _sigmoid(x):
    return 1.0 / (1.0 + jnp.exp(-x))


def _silu(x):
    return x * _sigmoid(x)


def _mod_kernel(c_ref, w_ref, b_ref, o_ref):
    s = _silu(c_ref[...])
    o_ref[...] = _dot3(s, w_ref[...]) + b_ref[...]


def _mod(c_pad, w_ada, b_ada):
    rows, d = c_pad.shape
    n = w_ada.shape[1]
    tn = 1024
    return pl.pallas_call(
        _mod_kernel,
        grid=(n // tn,),
        in_specs=[pl.BlockSpec((rows, d), lambda j: (0, 0)),
                  pl.BlockSpec((d, tn), lambda j: (0, j)),
                  pl.BlockSpec((1, tn), lambda j: (0, j))],
        out_specs=pl.BlockSpec((rows, tn), lambda j: (0, j)),
        out_shape=jax.ShapeDtypeStruct((rows, n), F32),
        compiler_params=_params(("arbitrary",)),
        name="adaln_mod",
    )(c_pad, w_ada, b_ada)


def _inproj_kernel(x_ref, sc_ref, sh_ref, g_ref, w_ref, mu_ref, ps_ref, pb_ref, carry_ref):
    @pl.when(pl.program_id(1) == 0)
    def _():
        carry_ref[...] = jnp.zeros_like(carry_ref)

    x = x_ref[...]
    tm = x.shape[0]
    h = x * lax.rsqrt(jnp.mean(x * x, axis=-1, keepdims=True) + RMS_EPS) * g_ref[...]
    h = h * (1.0 + sc_ref[...]) + sh_ref[...]
    proj = jnp.dot(h.astype(BF16), w_ref[...], preferred_element_type=F32)
    ps = proj[:, :N_SHIFT]
    prev = pltpu.roll(ps, 1, 0)
    first = lax.broadcasted_iota(jnp.int32, ps.shape, 0) == 0
    prev = jnp.where(first, carry_ref[0:1, :], prev)
    carry_ref[0:1, :] = ps[tm - 1:tm, :]
    ps_ref[...] = ps + (prev - ps) * mu_ref[...]
    pb_ref[...] = proj[:, N_SHIFT:]


def _inproj(x, sc1, sh1, g, w_in_bf16, mu, tm):
    b, s, d = x.shape
    d_in = w_in_bf16.shape[1]
    row = lambda bi, i: (bi, 0, 0)
    return pl.pallas_call(
        _inproj_kernel,
        grid=(b, s // tm),
        in_specs=[pl.BlockSpec((None, tm, d), lambda bi, i: (bi, i, 0)),
                  pl.BlockSpec((None, 1, d), row),
                  pl.BlockSpec((None, 1, d), row),
                  pl.BlockSpec((1, d), lambda bi, i: (0, 0)),
                  pl.BlockSpec((d, d_in), lambda bi, i: (0, 0)),
                  pl.BlockSpec((1, N_SHIFT), lambda bi, i: (0, 0))],
        out_specs=[pl.BlockSpec((None, tm, N_SHIFT), lambda bi, i: (bi, i, 0)),
                   pl.BlockSpec((None, tm, d_in - N_SHIFT), lambda bi, i: (bi, i, 0))],
        out_shape=[jax.ShapeDtypeStruct((b, s, N_SHIFT), F32),
                   jax.ShapeDtypeStruct((b, s, d_in - N_SHIFT), F32)],
        scratch_shapes=[pltpu.VMEM((8, N_SHIFT), F32)],
        compiler_params=_params(("arbitrary", "arbitrary")),
        name="inproj_shift",
    )(x, sc1, sh1, g, w_in_bf16, mu)


def _prep_kernel(ps_ref, w0_ref, a0_ref, wlora_ref, wg_ref, kk_ref, ka_ref, rk_ref,
                 r_o, w_o, k_o, v_o, kk_o, b_o, g_o, bonus_o):
    ps = ps_ref[...]
    c1, c2, c3 = D_RWKV, 2 * D_RWKV, 3 * D_RWKV
    r, k, v = ps[:, :c1], ps[:, c1:c2], ps[:, c2:c3]
    xwa = ps[:, c3:c3 + LANES]
    xg = ps[:, c3 + LANES:]
    is_w = lax.broadcasted_iota(jnp.int32, xwa.shape, 1) < D_DECAY_LORA
    lora = _dot3(jnp.where(is_w, jnp.tanh(xwa), xwa), wlora_ref[...])
    wlog = -jax.nn.softplus(-(w0_ref[...] + lora[:, :c1])) - 0.5
    log_decay = -jnp.exp(wlog)
    a = _sigmoid(a0_ref[...] + lora[:, c1:])
    g = _dot3(_sigmoid(xg), wg_ref[...])
    ones = _head_ones()
    kk = k * kk_ref[...]
    kk = kk / jnp.maximum(jnp.sqrt(_head_sum(kk * kk, ones)), 1e-12)
    kh = k * (1.0 + (a - 1.0) * ka_ref[...])
    r_o[...] = r
    w_o[...] = log_decay
    k_o[...] = kh
    v_o[...] = v
    kk_o[...] = kk
    b_o[...] = kk * a
    g_o[...] = g
    bonus_o[...] = _head_sum(r * kh * rk_ref[...], ones) * v


def _prep(ps, w0, a0, wlora, wg, k_k, k_a, r_k, tm):
    t = ps.shape[0]
    const = lambda shape: pl.BlockSpec(shape, lambda i: (0, 0))
    out = jax.ShapeDtypeStruct((t, D_RWKV), F32)
    return pl.pallas_call(
        _prep_kernel,
        grid=(t // tm,),
        in_specs=[pl.BlockSpec((tm, N_SHIFT), lambda i: (i, 0)),
                  const((1, D_RWKV)), const((1, D_RWKV)),
                  const((LANES, 2 * D_RWKV)), const((D_GATE_LORA, D_RWKV)),
                  const((1, D_RWKV)), const((1, D_RWKV)), const((1, D_RWKV))],
        out_specs=[pl.BlockSpec((tm, D_RWKV), lambda i: (i, 0))] * 8,
        out_shape=[out] * 8,
        compiler_params=_params(("arbitrary",)),
        name="rwkv_prep",
    )(ps, w0, a0, wlora, wg, k_k, k_a, r_k)


RWKV_CHUNK = 16
CHUNK_TILE = 128

_NN = (((1,), (0,)), ((), ()))
_NT = (((1,), (1,)), ((), ()))


def _dot1(a, b, dims=_NN):
    return lax.dot_general(a.astype(BF16), b.astype(BF16), dims, preferred_element_type=F32)


def _dot_exact_lhs(lhs_bf16, x):
    hi, mid, lo = _split3(x)
    acc = jnp.dot(lhs_bf16, hi, preferred_element_type=F32)
    acc += jnp.dot(lhs_bf16, mid, preferred_element_type=F32)
    acc += jnp.dot(lhs_bf16, lo, preferred_element_type=F32)
    return acc


def _chunk_kernel(r_ref, lw_ref, k_ref, v_ref, kk_ref, b_ref,
                  w_o, rh_o, u0_o, y0_o, bh_o, kh_o, pl_o):
    n = CHUNK_TILE
    tpos = lax.broadcasted_iota(jnp.int32, (n, n), 0)
    spos = lax.broadcasted_iota(jnp.int32, (n, n), 1)
    same = (tpos // RWKV_CHUNK) == (spos // RWKV_CHUNK)
    incl = same & (spos <= tpos)
    strict = same & (spos < tpos)
    tri_ones = jnp.where(incl, 1.0, 0.0).astype(BF16)
    blk_ones = jnp.where(same, 1.0, 0.0).astype(BF16)
    eye = jnp.where(tpos == spos, 1.0, 0.0)
    lane_head = lax.broadcasted_iota(jnp.int32, (n, LANES), 1) // HEAD

    pairs = range(N_RWKV_PAIRS)
    slabs = [slice(LANES * p, LANES * (p + 1)) for p in pairs]
    lw, r, k, v, kk, b = ([ref[:, sl] for sl in slabs]
                          for ref in (lw_ref, r_ref, k_ref, v_ref, kk_ref, b_ref))
    cum = [_dot_exact_lhs(tri_ones, x) for x in lw]
    tot = [_dot_exact_lhs(blk_ones, x) for x in lw]
    at = [-kk[p] * jnp.exp(cum[p] - lw[p]) for p in pairs]
    rt = [r[p] * jnp.exp(cum[p]) for p in pairs]
    inv = [jnp.exp(-cum[p]) for p in pairs]
    bt = [b[p] * inv[p] for p in pairs]
    kt = [k[p] * inv[p] for p in pairs]

    chains = [(p, h) for p in pairs for h in range(LANES // HEAD)]
    mine = [lane_head == h for h in range(LANES // HEAD)]
    a_h = [jnp.where(mine[h], at[p], 0.0) for p, h in chains]
    r_h = [jnp.where(mine[h], rt[p], 0.0) for p, h in chains]
    v_h = [jnp.where(mine[h], v[p], 0.0) for p, h in chains]
    each = range(len(chains))
    a_ab = [jnp.where(strict, _dot3(a_h[c], bt[chains[c][0]], _NT), 0.0) for c in each]
    a_ak = [jnp.where(strict, _dot3(a_h[c], kt[chains[c][0]], _NT), 0.0) for c in each]
    tinv = [eye + a for a in a_ab]
    x = a_ab
    for _ in range(RWKV_CHUNK.bit_length() - 2):
        x = [_dot3(xc, xc) for xc in x]
        tinv = [tc + _dot3(tc, xc) for tc, xc in zip(tinv, x)]
    akv = [_dot3(a_ak[c], v_h[c]) for c in each]
    m_rb = [jnp.where(incl, _dot1(r_h[c], bt[chains[c][0]], _NT), 0.0) for c in each]
    m_rk = [jnp.where(incl, _dot1(r_h[c], kt[chains[c][0]], _NT), 0.0) for c in each]
    w_h = [_dot3(tinv[c], a_h[c]) for c in each]
    u0_h = [_dot3(tinv[c], akv[c]) for c in each]
    rh_h = [r_h[c] + _dot1(m_rb[c], w_h[c]) for c in each]
    y0_h = [_dot1(m_rb[c], u0_h[c]) + _dot1(m_rk[c], v_h[c]) for c in each]

    for p in pairs:
        c0, c1 = 2 * p, 2 * p + 1
        rem = jnp.exp(tot[p] - cum[p])
        w_o[:, slabs[p]] = w_h[c0] + w_h[c1]
        rh_o[:, slabs[p]] = rh_h[c0] + rh_h[c1]
        u0_o[:, slabs[p]] = u0_h[c0] + u0_h[c1]
        y0_o[:, slabs[p]] = y0_h[c0] + y0_h[c1]
        bh_o[:, slabs[p]] = b[p] * rem
        kh_o[:, slabs[p]] = k[p] * rem
        pl_o[:, slabs[p]] = jnp.exp(tot[p])


def _chunk(r, lw, k, v, kk, b):
    t, d = r.shape
    spec = pl.BlockSpec((CHUNK_TILE, d), lambda i: (i, 0))
    out = jax.ShapeDtypeStruct((t, d), F32)
    return pl.pallas_call(
        _chunk_kernel,
        grid=(t // CHUNK_TILE,),
        in_specs=[spec] * 6,
        out_specs=[spec] * 7,
        out_shape=[out] * 7,
        compiler_params=_params(("arbitrary",)),
        name="rwkv_chunk_prep",
    )(r, lw, k, v, kk, b)


def _rwkv_kernel(w_ref, rh_ref, u0_ref, y0_ref, bh_ref, kh_ref, v_ref, pl_ref,
                 y_ref, state_ref):
    @pl.when(pl.program_id(0) == 0)
    def _():
        state_ref[...] = jnp.zeros_like(state_ref)

    nb, tt, _ = w_ref.shape
    L = RWKV_CHUNK
    row_head = lax.broadcasted_iota(jnp.int32, (LANES, LANES), 0) // HEAD
    col_head = lax.broadcasted_iota(jnp.int32, (LANES, LANES), 1) // HEAD
    same_head = row_head == col_head
    eye = (lax.broadcasted_iota(jnp.int32, (LANES, LANES), 0)
           == lax.broadcasted_iota(jnp.int32, (LANES, LANES), 1))
    all_ones = jnp.ones((LANES, LANES), BF16)
    pad = jnp.zeros((LANES - 2 * L, LANES), F32)

    def column_of(p_row):
        p_hi = pltpu.bitcast(pltpu.bitcast(p_row, jnp.uint32) & jnp.uint32(0xFFFF0000), F32)
        return (jnp.dot(jnp.where(eye, p_hi, 0.0).astype(BF16), all_ones,
                        preferred_element_type=F32)
                + jnp.dot(jnp.where(eye, p_row - p_hi, 0.0).astype(BF16), all_ones,
                          preferred_element_type=F32))

    def chunk(ci, carry):
        t0 = pl.multiple_of(ci * L, L)
        streams = [(bi, p) for bi in range(nb) for p in range(N_RWKV_PAIRS)]

        def rows(ref, n=L):
            return [ref[bi, pl.ds(t0, n), LANES * p:LANES * (p + 1)] for bi, p in streams]

        each = range(len(streams))
        s = [state_ref[bi * N_RWKV_PAIRS + p] for bi, p in streams]
        w, rh, u0, y0, bh, kh, v = (rows(ref) for ref in
                                   (w_ref, rh_ref, u0_ref, y0_ref, bh_ref, kh_ref, v_ref))
        p_col = [column_of(pr[0:1, :]) for pr in rows(pl_ref, SUBLANES)]
        bk_t = [jnp.concatenate([bh[c], kh[c], pad], axis=0).T for c in each]
        g = [_dot3(jnp.concatenate([w[c], rh[c]], axis=0), s[c]) for c in each]
        u = [g[c][:L] + u0[c] for c in each]
        upd = [_dot3(bk_t[c], jnp.concatenate([u[c], v[c], pad], axis=0)) for c in each]
        for c, (bi, p) in enumerate(streams):
            y_ref[bi, pl.ds(t0, L), LANES * p:LANES * (p + 1)] = g[c][L:] + y0[c]
            state_ref[bi * N_RWKV_PAIRS + p] = (p_col[c] * s[c]
                                                + jnp.where(same_head, upd[c], 0.0))
        return carry

    lax.fori_loop(0, tt // L, chunk, 0)


def _rwkv(w, rh, u0, y0, bh, kh, v, p_l, tt):
    nb, s, d = w.shape
    spec = pl.BlockSpec((nb, tt, d), lambda i: (0, i, 0))
    return pl.pallas_call(
        _rwkv_kernel,
        grid=(s // tt,),
        in_specs=[spec] * 8,
        out_specs=spec,
        out_shape=jax.ShapeDtypeStruct((nb, s, d), F32),
        scratch_shapes=[pltpu.VMEM((nb * N_RWKV_PAIRS, LANES, LANES), F32)],
        compiler_params=_params(("arbitrary",)),
        name="rwkv_scan",
    )(w, rh, u0, y0, bh, kh, v, p_l)


def _gelu_tanh(x):
    return 0.5 * x * (1.0 + jnp.tanh(0.7978845608028654 * (x + 0.044715 * (x * x * x))))


def _sgu_kernel(pb_ref, g_ref, b_ref, ws_ref, bias_ref, o_ref):
    z = _gelu_tanh(pb_ref[...])
    u, v = z[:, :D_SGU], z[:, D_SGU:]
    mu = jnp.mean(v, axis=-1, keepdims=True)
    vc = v - mu
    var = jnp.mean(vc * vc, axis=-1, keepdims=True)
    vn = (vc * lax.rsqrt(var + LN_EPS) * g_ref[...] + b_ref[...]).astype(BF16)
    tpos = lax.broadcasted_iota(jnp.int32, (SGU_BLOCK, SGU_BLOCK), 0) // CHUNK
    spos = lax.broadcasted_iota(jnp.int32, (SGU_BLOCK, SGU_BLOCK), 1) // CHUNK
    causal = spos <= tpos
    first_head = lax.broadcasted_iota(jnp.int32, (SGU_BLOCK, LANES), 1) < HEAD
    for p in range(D_SGU // LANES):
        sl = slice(LANES * p, LANES * (p + 1))
        ws0 = jnp.where(causal, ws_ref[2 * p], 0.0).astype(BF16)
        ws1 = jnp.where(causal, ws_ref[2 * p + 1], 0.0).astype(BF16)
        for n in range(pb_ref.shape[0] // SGU_BLOCK):
            rows = slice(SGU_BLOCK * n, SGU_BLOCK * (n + 1))
            vb = vn[rows, sl]
            sp = jnp.where(first_head,
                           jnp.dot(ws0, vb, preferred_element_type=F32),
                           jnp.dot(ws1, vb, preferred_element_type=F32))
            o_ref[rows, sl] = (u[rows, sl] * (sp + bias_ref[:, sl])).astype(o_ref.dtype)


def _sgu(pb, ln_g, ln_b, w_spatial, bias_tile, tm):
    t = pb.shape[0]
    nh = w_spatial.shape[0]
    return pl.pallas_call(
        _sgu_kernel,
        grid=(t // tm,),
        in_specs=[pl.BlockSpec((tm, 2 * D_SGU), lambda i: (i, 0)),
                  pl.BlockSpec((1, D_SGU), lambda i: (0, 0)),
                  pl.BlockSpec((1, D_SGU), lambda i: (0, 0)),
                  pl.BlockSpec((nh, SGU_BLOCK, SGU_BLOCK), lambda i: (0, 0, 0)),
                  pl.BlockSpec((SGU_BLOCK, D_SGU), lambda i: (0, 0))],
        out_specs=pl.BlockSpec((tm, D_SGU), lambda i: (i, 0)),
        out_shape=jax.ShapeDtypeStruct((t, D_SGU), BF16),
        compiler_params=_params(("arbitrary",)),
        name="sgu_mix",
    )(pb, ln_g, ln_b, w_spatial, bias_tile)


def _mid_kernel(y_ref, g_ref, bonus_ref, lng_ref, lnb_ref, yb_ref, x_ref,
                g1_ref, sc2_ref, sh2_ref, g2_ref, n2_ref,
                woa_ref, wob_ref, wrt_ref, w13_ref, w2_ref,
                h2_ref, logit_ref, base_ref):
    ones = _head_ones()
    y = y_ref[...]
    mu = _head_sum(y, ones) * (1.0 / HEAD)
    yc = y - mu
    var = _head_sum(yc * yc, ones) * (1.0 / HEAD)
    ya = yc * lax.rsqrt(var + LN_X_EPS) * lng_ref[...] + lnb_ref[...]
    ya = ((ya + bonus_ref[...]) * g_ref[...]).astype(BF16)
    yo = (jnp.dot(ya, woa_ref[...], preferred_element_type=F32)
          + jnp.dot(yb_ref[...], wob_ref[...], preferred_element_type=F32))
    x1 = x_ref[...] + g1_ref[...] * yo
    h2 = x1 * lax.rsqrt(jnp.mean(x1 * x1, axis=-1, keepdims=True) + RMS_EPS) * n2_ref[...]
    h2 = h2 * (1.0 + sc2_ref[...]) + sh2_ref[...]
    h2b = h2.astype(BF16)
    h2_ref[...] = h2b
    logit_ref[...] = _dot3(wrt_ref[...], h2, dims=(((1,), (1,)), ((), ())))
    hid = jnp.dot(h2b, w13_ref[...], preferred_element_type=F32)
    act = (_silu(hid[:, :D_EXPERT]) * hid[:, D_EXPERT:]).astype(BF16)
    ms = jnp.dot(act, w2_ref[...], preferred_element_type=F32)
    base_ref[...] = x1 + g2_ref[...] * ms


def _mid(y, g, bonus, lnx_g, lnx_b, yb, x, g1, sc2, sh2, g2, n2g,
         wo_a, wo_b, wr_t, w13s, w2s, tm, tiles_per_batch):
    t, d = x.shape
    tok = lambda w: pl.BlockSpec((tm, w), lambda i: (i, 0))
    const = lambda shape: pl.BlockSpec(shape, lambda i: (0,) * len(shape))
    mod = pl.BlockSpec((None, 1, d), lambda i: (i // tiles_per_batch, 0, 0))
    return pl.pallas_call(
        _mid_kernel,
        grid=(t // tm,),
        in_specs=[tok(D_RWKV), tok(D_RWKV), tok(D_RWKV),
                  const((1, D_RWKV)), const((1, D_RWKV)),
                  tok(D_SGU), tok(d), mod, mod, mod, mod, const((1, d)),
                  const(wo_a.shape), const(wo_b.shape), const(wr_t.shape),
                  const(w13s.shape), const(w2s.shape)],
        out_specs=[tok(d), pl.BlockSpec((LANES, tm), lambda i: (0, i)), tok(d)],
        out_shape=[jax.ShapeDtypeStruct((t, d), BF16),
                   jax.ShapeDtypeStruct((LANES, t), F32),
                   jax.ShapeDtypeStruct((t, d), F32)],
        compiler_params=_params(("arbitrary",)),
        name="outproj_norm_router_shared",
    )(y, g, bonus, lnx_g, lnx_b, yb, x, g1, sc2, sh2, g2, n2g,
      wo_a, wo_b, wr_t, w13s, w2s)


def _route_kernel(logit_ref, bias_ref, gate_ref):
    tt = logit_ref.shape[1]
    scores = _sigmoid(logit_ref[0:N_EXPERTS, :])
    biased = scores + bias_ref[...]
    neg_inf = -jnp.inf

    group_blocks, group_scores = [], []
    for gi in range(N_GROUPS):
        blk = biased[GROUP_SIZE * gi:GROUP_SIZE * (gi + 1), :]
        m1 = jnp.max(blk, axis=0, keepdims=True)
        is_max = blk == m1
        n_max = jnp.sum(jnp.where(is_max, 1.0, 0.0), axis=0, keepdims=True)
        m2 = jnp.max(jnp.where(is_max, neg_inf, blk), axis=0, keepdims=True)
        group_blocks.append(blk)
        group_scores.append(m1 + jnp.where(n_max >= 2.0, m1, m2))

    masked = []
    for gi in range(N_GROUPS):
        rank = jnp.zeros((1, tt), F32)
        for gj in range(N_GROUPS):
            if gj == gi:
                continue
            ahead = (group_scores[gj] >= group_scores[gi]) if gj < gi else \
                    (group_scores[gj] > group_scores[gi])
            rank += jnp.where(ahead, 1.0, 0.0)
        masked.append(jnp.where(rank < float(TOPK_GROUPS), group_blocks[gi], neg_inf))
    sub = lax.broadcasted_iota(jnp.int32, (GROUP_SIZE, tt), 0)
    ranks = [jnp.zeros((GROUP_SIZE, tt), F32) for _ in range(N_GROUPS)]
    for ej in range(N_EXPERTS):
        gj, oj = divmod(ej, GROUP_SIZE)
        other = masked[gj][oj:oj + 1, :]
        for gi in range(N_GROUPS):
            blk = masked[gi]
            if gi > gj:
                ahead = jnp.where(other >= blk, 1.0, 0.0)
            elif gi < gj:
                ahead = jnp.where(other > blk, 1.0, 0.0)
            else:
                ahead = jnp.where(sub > oj, jnp.where(other >= blk, 1.0, 0.0),
                                  jnp.where(other > blk, 1.0, 0.0))
            ranks[gi] = ranks[gi] + ahead
    rank = jnp.concatenate(ranks, axis=0)
    sw = jnp.where(rank < float(TOP_K), scores, 0.0)
    gates = sw / jnp.sum(sw, axis=0, keepdims=True) * ROUTED_SCALE
    gates = jnp.concatenate([gates, jnp.zeros((LANES - N_EXPERTS, tt), F32)], axis=0)
    gate_ref[...] = gates.T


def _route(logits_t, bias_tile, tt):
    t = logits_t.shape[1]
    return pl.pallas_call(
        _route_kernel,
        grid=(t // tt,),
        in_specs=[pl.BlockSpec((LANES, tt), lambda i: (0, i)),
                  pl.BlockSpec((N_EXPERTS, tt), lambda i: (0, 0))],
        out_specs=pl.BlockSpec((tt, LANES), lambda i: (i, 0)),
        out_shape=jax.ShapeDtypeStruct((t, LANES), F32),
        compiler_params=_params(("arbitrary",)),
        name="route_topk",
    )(logits_t, bias_tile)


MOE_EXPERTS_PER_STEP = 4
MOE_SUB_ROWS = 256


def _moe_kernel(h2_ref, gate_ref, w13_ref, w2_ref, base_ref, g2_ref, nf_ref, o_ref, acc_ref):
    eg = pl.program_id(1)
    n_exp = w13_ref.shape[0]
    n_sub = h2_ref.shape[0] // MOE_SUB_ROWS

    @pl.when(eg == 0)
    def _():
        acc_ref[...] = jnp.zeros_like(acc_ref)

    lane = lax.broadcasted_iota(jnp.int32, (MOE_SUB_ROWS, LANES), 1)

    def gated_hidden(i):
        rows = slice(MOE_SUB_ROWS * i, MOE_SUB_ROWS * (i + 1))
        x = h2_ref[rows, :]
        gates = gate_ref[rows, :]
        acts = []
        for j in range(n_exp):
            hid = jnp.dot(x, w13_ref[j], preferred_element_type=F32)
            gcol = jnp.sum(jnp.where(lane == eg * n_exp + j, gates, 0.0),
                           axis=-1, keepdims=True)
            acts.append((_silu(hid[:, :D_EXPERT]) * hid[:, D_EXPERT:] * gcol).astype(BF16))
        return acts

    def project(i, acts):
        rows = slice(MOE_SUB_ROWS * i, MOE_SUB_ROWS * (i + 1))
        out = jnp.dot(acts[0], w2_ref[0], preferred_element_type=F32)
        for j in range(1, n_exp):
            out += jnp.dot(acts[j], w2_ref[j], preferred_element_type=F32)
        acc_ref[rows, :] += out

    pending = gated_hidden(0)
    for i in range(1, n_sub):
        nxt = gated_hidden(i)
        project(i - 1, pending)
        pending = nxt
    project(n_sub - 1, pending)

    @pl.when(eg == pl.num_programs(1) - 1)
    def _():
        xo = base_ref[...] + g2_ref[...] * acc_ref[...]
        o_ref[...] = (xo * lax.rsqrt(jnp.mean(xo * xo, axis=-1, keepdims=True) + RMS_EPS)
                      * nf_ref[...])


def _moe(h2, gates, w13e, w2e, base, g2, nf, tm, tiles_per_batch):
    t, d = base.shape
    ne = w13e.shape[0]
    ge = MOE_EXPERTS_PER_STEP
    assert ne % ge == 0 and tm % MOE_SUB_ROWS == 0
    tok = lambda w: pl.BlockSpec((tm, w), lambda i, e: (i, 0))
    return pl.pallas_call(
        _moe_kernel,
        grid=(t // tm, ne // ge),
        in_specs=[tok(d), tok(LANES),
                  pl.BlockSpec((ge, d, 2 * D_EXPERT), lambda i, e: (e, 0, 0)),
                  pl.BlockSpec((ge, D_EXPERT, d), lambda i, e: (e, 0, 0)),
                  tok(d),
                  pl.BlockSpec((None, 1, d), lambda i, e: (i // tiles_per_batch, 0, 0)),
                  pl.BlockSpec((1, d), lambda i, e: (0, 0))],
        out_specs=tok(d),
        out_shape=jax.ShapeDtypeStruct((t, d), F32),
        scratch_shapes=[pltpu.VMEM((tm, d), F32)],
        compiler_params=_params(("arbitrary", "arbitrary")),
        name="moe_experts_final",
    )(h2, gates, w13e, w2e, base, g2, nf)


def _tile(n, want):
    t = min(n, want)
    assert n % t == 0, (n, t)
    return t


def kernel(x, c, w_ada, b_ada, norm1_g, w_in, mu_shift, w0, w_decay_up, a0, w_a_up, w_g_up,
           k_k, k_a, r_k, lnx_g, lnx_b, sgu_ln_g, sgu_ln_b, w_spatial, b_spatial, w_out,
           norm2_g, w_router, e_bias, w1_e, w3_e, w2_e, w1_s, w3_s, w2_s, norm_f_g):
    b, s, d = x.shape
    t = b * s
    assert w_ada.shape[0] == 1, "single-layer block"
    assert d == 2 * D_RWKV and s % SGU_BLOCK == 0
    row = lambda vec: vec.reshape(1, -1)

    c_pad = jnp.pad(c, ((0, 8 - b), (0, 0)))
    mod = _mod(c_pad, w_ada[0], b_ada)[:b]
    sh1, sc1, g1, sh2, sc2, g2 = [m[:, None, :] for m in jnp.split(mod, 6, axis=-1)]

    ps, pb = _inproj(x, sc1, sh1, norm1_g, w_in[0].astype(BF16), mu_shift, _tile(s, 512))

    wlora = jnp.zeros((LANES, 2 * D_RWKV), F32)
    wlora = wlora.at[:D_DECAY_LORA, :D_RWKV].set(w_decay_up[0])
    wlora = wlora.at[D_DECAY_LORA:, D_RWKV:].set(w_a_up[0])
    r, wdec, kh, v, kk, bb, g, bonus = _prep(
        ps.reshape(t, N_SHIFT), w0, a0, wlora, w_g_up[0], k_k, k_a, row(r_k), _tile(t, 512))

    seq = lambda z: z.reshape(b, s, D_RWKV)
    w_c, rh_c, u0_c, y0_c, bh_c, kh_c, pl_c = _chunk(r, wdec, kh, v, kk, bb)
    y = _rwkv(seq(w_c), seq(rh_c), seq(u0_c), seq(y0_c), seq(bh_c), seq(kh_c), seq(v),
              seq(pl_c), _tile(s, 256))

    bias_tile = jnp.repeat(b_spatial[0].T, HEAD, axis=1)
    yb = _sgu(pb.reshape(t, 2 * D_SGU), sgu_ln_g, sgu_ln_b, w_spatial[0], bias_tile,
              _tile(t, 512))

    tm_mid = _tile(s, 512)
    wr_t = jnp.pad(w_router[0].T, ((0, LANES - N_EXPERTS), (0, 0)))
    w13s = jnp.concatenate([w1_s[0], w3_s[0]], axis=1).astype(BF16)
    wo = w_out[0].astype(BF16)
    h2, logits_t, base = _mid(
        y.reshape(t, D_RWKV), g, bonus, lnx_g, lnx_b, yb, x.reshape(t, d),
        g1, sc2, sh2, g2, norm2_g, wo[:D_RWKV], wo[D_RWKV:], wr_t, w13s,
        w2_s[0].astype(BF16), tm_mid, s // tm_mid)

    tt_route = _tile(t, 1024)
    gates = _route(logits_t, jnp.broadcast_to(e_bias[0][:, None], (N_EXPERTS, tt_route)),
                   tt_route)

    tm_moe = _tile(s, 1024)
    w13e = jnp.concatenate([w1_e[0], w3_e[0]], axis=2).astype(BF16)
    out = _moe(h2, gates, w13e, w2_e[0].astype(BF16), base, g2, row(norm_f_g),
               tm_moe, s // tm_moe)
    return out.reshape(b, s, d)
```

```python
import functools

import jax
import jax.numpy as jnp
from jax import lax
from jax.experimental import pallas as pl
from jax.experimental.pallas import tpu as pltpu

F32 = jnp.float32
BF16 = jnp.bfloat16

HEAD = 64
LANES = 128
SUBLANES = 8
D_RWKV = 512
D_SGU = 512
N_RWKV_PAIRS = D_RWKV // LANES
D_DECAY_LORA = 64
D_AAA_LORA = 64
D_GATE_LORA = 128
N_SHIFT = 3 * D_RWKV + D_DECAY_LORA + D_AAA_LORA + D_GATE_LORA
SGU_BLOCK = 128
CHUNK = 64
N_EXPERTS = 64
N_GROUPS = 8
GROUP_SIZE = N_EXPERTS // N_GROUPS
TOPK_GROUPS = 4
TOP_K = 8
D_EXPERT = 256
ROUTED_SCALE = 2.5
RMS_EPS = 1e-6
LN_EPS = 1e-5
LN_X_EPS = 64e-5
VMEM_LIMIT_BYTES = 56 * 1024 * 1024


def _params(semantics):
    return pltpu.CompilerParams(dimension_semantics=semantics,
                                vmem_limit_bytes=VMEM_LIMIT_BYTES)


def _split3(x):
    hi = x.astype(BF16)
    r1 = x - hi.astype(F32)
    mid = r1.astype(BF16)
    lo = (r1 - mid.astype(F32)).astype(BF16)
    return hi, mid, lo


def _dot_exact_rhs(x, rhs_bf16):
    hi, mid, lo = _split3(x)
    acc = jnp.dot(hi, rhs_bf16, preferred_element_type=F32)
    acc += jnp.dot(mid, rhs_bf16, preferred_element_type=F32)
    acc += jnp.dot(lo, rhs_bf16, preferred_element_type=F32)
    return acc


def _dot3(a, b, dims=(((1,), (0,)), ((), ()))):
    a_hi = a.astype(BF16)
    a_lo = (a - a_hi.astype(F32)).astype(BF16)
    b_hi = b.astype(BF16)
    b_lo = (b - b_hi.astype(F32)).astype(BF16)
    dg = functools.partial(lax.dot_general, dimension_numbers=dims,
                           preferred_element_type=F32)
    return dg(a_hi, b_hi) + dg(a_hi, b_lo) + dg(a_lo, b_hi)


def _head_ones():
    r = lax.broadcasted_iota(jnp.int32, (LANES, LANES), 0) // HEAD
    c = lax.broadcasted_iota(jnp.int32, (LANES, LANES), 1) // HEAD
    return jnp.where(r == c, 1.0, 0.0).astype(BF16)


def _head_sum(z, ones):
    parts = [_dot_exact_rhs(z[:, LANES * p:LANES * (p + 1)], ones)
             for p in range(z.shape[1] // LANES)]
    return jnp.concatenate(parts, axis=-1)


def _sigmoid(x):
    return 1.0 / (1.0 + jnp.exp(-x))


def _silu(x):
    return x * _sigmoid(x)


def _mod_kernel(c_ref, w_ref, b_ref, o_ref):
    s = _silu(c_ref[...])
    o_ref[...] = _dot3(s, w_ref[...]) + b_ref[...]


def _mod(c_pad, w_ada, b_ada):
    rows, d = c_pad.shape
    n = w_ada.shape[1]
    tn = 1024
    return pl.pallas_call(
        _mod_kernel,
        grid=(n // tn,),
        in_specs=[pl.BlockSpec((rows, d), lambda j: (0, 0)),
                  pl.BlockSpec((d, tn), lambda j: (0, j)),
                  pl.BlockSpec((1, tn), lambda j: (0, j))],
        out_specs=pl.BlockSpec((rows, tn), lambda j: (0, j)),
        out_shape=jax.ShapeDtypeStruct((rows, n), F32),
        compiler_params=_params(("arbitrary",)),
        name="adaln_mod",
    )(c_pad, w_ada, b_ada)


def _inproj_kernel(x_ref, sc_ref, sh_ref, g_ref, w_ref, mu_ref, ps_ref, pb_ref, carry_ref):
    @pl.when(pl.program_id(1) == 0)
    def _():
        carry_ref[...] = jnp.zeros_like(carry_ref)

    x = x_ref[...]
    tm = x.shape[0]
    h = x * lax.rsqrt(jnp.mean(x * x, axis=-1, keepdims=True) + RMS_EPS) * g_ref[...]
    h = h * (1.0 + sc_ref[...]) + sh_ref[...]
    proj = jnp.dot(h.astype(BF16), w_ref[...], preferred_element_type=F32)
    ps = proj[:, :N_SHIFT]
    prev = pltpu.roll(ps, 1, 0)
    first = lax.broadcasted_iota(jnp.int32, ps.shape, 0) == 0
    prev = jnp.where(first, carry_ref[0:1, :], prev)
    carry_ref[0:1, :] = ps[tm - 1:tm, :]
    ps_ref[...] = ps + (prev - ps) * mu_ref[...]
    pb_ref[...] = proj[:, N_SHIFT:]


def _inproj(x, sc1, sh1, g, w_in_bf16, mu, tm):
    b, s, d = x.shape
    d_in = w_in_bf16.shape[1]
    row = lambda bi, i: (bi, 0, 0)
    return pl.pallas_call(
        _inproj_kernel,
        grid=(b, s // tm),
        in_specs=[pl.BlockSpec((None, tm, d), lambda bi, i: (bi, i, 0)),
                  pl.BlockSpec((None, 1, d), row),
                  pl.BlockSpec((None, 1, d), row),
                  pl.BlockSpec((1, d), lambda bi, i: (0, 0)),
                  pl.BlockSpec((d, d_in), lambda bi, i: (0, 0)),
                  pl.BlockSpec((1, N_SHIFT), lambda bi, i: (0, 0))],
        out_specs=[pl.BlockSpec((None, tm, N_SHIFT), lambda bi, i: (bi, i, 0)),
                   pl.BlockSpec((None, tm, d_in - N_SHIFT), lambda bi, i: (bi, i, 0))],
        out_shape=[jax.ShapeDtypeStruct((b, s, N_SHIFT), F32),
                   jax.ShapeDtypeStruct((b, s, d_in - N_SHIFT), F32)],
        scratch_shapes=[pltpu.VMEM((8, N_SHIFT), F32)],
        compiler_params=_params(("arbitrary", "arbitrary")),
        name="inproj_shift",
    )(x, sc1, sh1, g, w_in_bf16, mu)


def _prep_kernel(ps_ref, w0_ref, a0_ref, wlora_ref, wg_ref, kk_ref, ka_ref, rk_ref,
                 r_o, w_o, k_o, v_o, kk_o, b_o, g_o, bonus_o):
    ps = ps_ref[...]
    c1, c2, c3 = D_RWKV, 2 * D_RWKV, 3 * D_RWKV
    r, k, v = ps[:, :c1], ps[:, c1:c2], ps[:, c2:c3]
    xwa = ps[:, c3:c3 + LANES]
    xg = ps[:, c3 + LANES:]
    is_w = lax.broadcasted_iota(jnp.int32, xwa.shape, 1) < D_DECAY_LORA
    lora = _dot3(jnp.where(is_w, jnp.tanh(xwa), xwa), wlora_ref[...])
    wlog = -jax.nn.softplus(-(w0_ref[...] + lora[:, :c1])) - 0.5
    log_decay = -jnp.exp(wlog)
    a = _sigmoid(a0_ref[...] + lora[:, c1:])
    g = _dot3(_sigmoid(xg), wg_ref[...])
    ones = _head_ones()
    kk = k * kk_ref[...]
    kk = kk / jnp.maximum(jnp.sqrt(_head_sum(kk * kk, ones)), 1e-12)
    kh = k * (1.0 + (a - 1.0) * ka_ref[...])
    r_o[...] = r
    w_o[...] = log_decay
    k_o[...] = kh
    v_o[...] = v
    kk_o[...] = kk
    b_o[...] = kk * a
    g_o[...] = g
    bonus_o[...] = _head_sum(r * kh * rk_ref[...], ones) * v


def _prep(ps, w0, a0, wlora, wg, k_k, k_a, r_k, tm):
    t = ps.shape[0]
    const = lambda shape: pl.BlockSpec(shape, lambda i: (0, 0))
    out = jax.ShapeDtypeStruct((t, D_RWKV), F32)
    return pl.pallas_call(
        _prep_kernel,
        grid=(t // tm,),
        in_specs=[pl.BlockSpec((tm, N_SHIFT), lambda i: (i, 0)),
                  const((1, D_RWKV)), const((1, D_RWKV)),
                  const((LANES, 2 * D_RWKV)), const((D_GATE_LORA, D_RWKV)),
                  const((1, D_RWKV)), const((1, D_RWKV)), const((1, D_RWKV))],
        out_specs=[pl.BlockSpec((tm, D_RWKV), lambda i: (i, 0))] * 8,
        out_shape=[out] * 8,
        compiler_params=_params(("arbitrary",)),
        name="rwkv_prep",
    )(ps, w0, a0, wlora, wg, k_k, k_a, r_k)


RWKV_CHUNK = 16
CHUNK_TILE = 128

_NN = (((1,), (0,)), ((), ()))
_NT = (((1,), (1,)), ((), ()))


def _dot1(a, b, dims=_NN):
    return lax.dot_general(a.astype(BF16), b.astype(BF16), dims, preferred_element_type=F32)


def _dot_exact_lhs(lhs_bf16, x):
    hi, mid, lo = _split3(x)
    acc = jnp.dot(lhs_bf16, hi, preferred_element_type=F32)
    acc += jnp.dot(lhs_bf16, mid, preferred_element_type=F32)
    acc += jnp.dot(lhs_bf16, lo, preferred_element_type=F32)
    return acc


def _chunk_kernel(r_ref, lw_ref, k_ref, v_ref, kk_ref, b_ref,
                  w_o, rh_o, u0_o, y0_o, bh_o, kh_o, pl_o):
    n = CHUNK_TILE
    tpos = lax.broadcasted_iota(jnp.int32, (n, n), 0)
    spos = lax.broadcasted_iota(jnp.int32, (n, n), 1)
    same = (tpos // RWKV_CHUNK) == (spos // RWKV_CHUNK)
    incl = same & (spos <= tpos)
    strict = same & (spos < tpos)
    tri_ones = jnp.where(incl, 1.0, 0.0).astype(BF16)
    blk_ones = jnp.where(same, 1.0, 0.0).astype(BF16)
    eye = jnp.where(tpos == spos, 1.0, 0.0)
    lane_head = lax.broadcasted_iota(jnp.int32, (n, LANES), 1) // HEAD

    pairs = range(N_RWKV_PAIRS)
    slabs = [slice(LANES * p, LANES * (p + 1)) for p in pairs]
    lw, r, k, v, kk, b = ([ref[:, sl] for sl in slabs]
                          for ref in (lw_ref, r_ref, k_ref, v_ref, kk_ref, b_ref))
    cum = [_dot_exact_lhs(tri_ones, x) for x in lw]
    tot = [_dot_exact_lhs(blk_ones, x) for x in lw]
    at = [-kk[p] * jnp.exp(cum[p] - lw[p]) for p in pairs]
    rt = [r[p] * jnp.exp(cum[p]) for p in pairs]
    inv = [jnp.exp(-cum[p]) for p in pairs]
    bt = [b[p] * inv[p] for p in pairs]
    kt = [k[p] * inv[p] for p in pairs]

    chains = [(p, h) for p in pairs for h in range(LANES // HEAD)]
    mine = [lane_head == h for h in range(LANES // HEAD)]
    a_h = [jnp.where(mine[h], at[p], 0.0) for p, h in chains]
    r_h = [jnp.where(mine[h], rt[p], 0.0) for p, h in chains]
    v_h = [jnp.where(mine[h], v[p], 0.0) for p, h in chains]
    each = range(len(chains))
    a_ab = [jnp.where(strict, _dot3(a_h[c], bt[chains[c][0]], _NT), 0.0) for c in each]
    a_ak = [jnp.where(strict, _dot3(a_h[c], kt[chains[c][0]], _NT), 0.0) for c in each]
    tinv = [eye + a for a in a_ab]
    x = a_ab
    for _ in range(RWKV_CHUNK.bit_length() - 2):
        x = [_dot3(xc, xc) for xc in x]
        tinv = [tc + _dot3(tc, xc) for tc, xc in zip(tinv, x)]
    akv = [_dot3(a_ak[c], v_h[c]) for c in each]
    m_rb = [jnp.where(incl, _dot1(r_h[c], bt[chains[c][0]], _NT), 0.0) for c in each]
    m_rk = [jnp.where(incl, _dot1(r_h[c], kt[chains[c][0]], _NT), 0.0) for c in each]
    w_h = [_dot3(tinv[c], a_h[c]) for c in each]
    u0_h = [_dot3(tinv[c], akv[c]) for c in each]
    rh_h = [r_h[c] + _dot1(m_rb[c], w_h[c]) for c in each]
    y0_h = [_dot1(m_rb[c], u0_h[c]) + _dot1(m_rk[c], v_h[c]) for c in each]

    for p in pairs:
        c0, c1 = 2 * p, 2 * p + 1
        rem = jnp.exp(tot[p] - cum[p])
        w_o[:, slabs[p]] = w_h[c0] + w_h[c1]
        rh_o[:, slabs[p]] = rh_h[c0] + rh_h[c1]
        u0_o[:, slabs[p]] = u0_h[c0] + u0_h[c1]
        y0_o[:, slabs[p]] = y0_h[c0] + y0_h[c1]
        bh_o[:, slabs[p]] = b[p] * rem
        kh_o[:, slabs[p]] = k[p] * rem
        pl_o[:, slabs[p]] = jnp.exp(tot[p])


def _chunk(r, lw, k, v, kk, b):
    t, d = r.shape
    spec = pl.BlockSpec((CHUNK_TILE, d), lambda i: (i, 0))
    out = jax.ShapeDtypeStruct((t, d), F32)
    return pl.pallas_call(
        _chunk_kernel,
        grid=(t // CHUNK_TILE,),
        in_specs=[spec] * 6,
        out_specs=[spec] * 7,
        out_shape=[out] * 7,
        compiler_params=_params(("arbitrary",)),
        name="rwkv_chunk_prep",
    )(r, lw, k, v, kk, b)


def _rwkv_kernel(w_ref, rh_ref, u0_ref, y0_ref, bh_ref, kh_ref, v_ref, pl_ref,
                 y_ref, state_ref):
    @pl.when(pl.program_id(0) == 0)
    def _():
        state_ref[...] = jnp.zeros_like(state_ref)

    nb, tt, _ = w_ref.shape
    L = RWKV_CHUNK
    row_head = lax.broadcasted_iota(jnp.int32, (LANES, LANES), 0) // HEAD
    col_head = lax.broadcasted_iota(jnp.int32, (LANES, LANES), 1) // HEAD
    same_head = row_head == col_head
    eye = (lax.broadcasted_iota(jnp.int32, (LANES, LANES), 0)
           == lax.broadcasted_iota(jnp.int32, (LANES, LANES), 1))
    all_ones = jnp.ones((LANES, LANES), BF16)
    pad = jnp.zeros((LANES - 2 * L, LANES), F32)

    def column_of(p_row):
        p_hi = pltpu.bitcast(pltpu.bitcast(p_row, jnp.uint32) & jnp.uint32(0xFFFF0000), F32)
        return (jnp.dot(jnp.where(eye, p_hi, 0.0).astype(BF16), all_ones,
                        preferred_element_type=F32)
                + jnp.dot(jnp.where(eye, p_row - p_hi, 0.0).astype(BF16), all_ones,
                          preferred_element_type=F32))

    def chunk(ci, carry):
        t0 = pl.multiple_of(ci * L, L)
        streams = [(bi, p) for bi in range(nb) for p in range(N_RWKV_PAIRS)]

        def rows(ref, n=L):
            return [ref[bi, pl.ds(t0, n), LANES * p:LANES * (p + 1)] for bi, p in streams]

        each = range(len(streams))
        s = [state_ref[bi * N_RWKV_PAIRS + p] for bi, p in streams]
        w, rh, u0, y0, bh, kh, v = (rows(ref) for ref in
                                   (w_ref, rh_ref, u0_ref, y0_ref, bh_ref, kh_ref, v_ref))
        p_col = [column_of(pr[0:1, :]) for pr in rows(pl_ref, SUBLANES)]
        bk_t = [jnp.concatenate([bh[c], kh[c], pad], axis=0).T for c in each]
        g = [_dot3(jnp.concatenate([w[c], rh[c]], axis=0), s[c]) for c in each]
        u = [g[c][:L] + u0[c] for c in each]
        upd = [_dot3(bk_t[c], jnp.concatenate([u[c], v[c], pad], axis=0)) for c in each]
        for c, (bi, p) in enumerate(streams):
            y_ref[bi, pl.ds(t0, L), LANES * p:LANES * (p + 1)] = g[c][L:] + y0[c]
            state_ref[bi * N_RWKV_PAIRS + p] = (p_col[c] * s[c]
                                                + jnp.where(same_head, upd[c], 0.0))
        return carry

    lax.fori_loop(0, tt // L, chunk, 0)


def _rwkv(w, rh, u0, y0, bh, kh, v, p_l, tt):
    nb, s, d = w.shape
    spec = pl.BlockSpec((nb, tt, d), lambda i: (0, i, 0))
    return pl.pallas_call(
        _rwkv_kernel,
        grid=(s // tt,),
        in_specs=[spec] * 8,
        out_specs=spec,
        out_shape=jax.ShapeDtypeStruct((nb, s, d), F32),
        scratch_shapes=[pltpu.VMEM((nb * N_RWKV_PAIRS, LANES, LANES), F32)],
        compiler_params=_params(("arbitrary",)),
        name="rwkv_scan",
    )(w, rh, u0, y0, bh, kh, v, p_l)


def _gelu_tanh(x):
    return 0.5 * x * (1.0 + jnp.tanh(0.7978845608028654 * (x + 0.044715 * (x * x * x))))


def _sgu_kernel(pb_ref, g_ref, b_ref, ws_ref, bias_ref, o_ref):
    z = _gelu_tanh(pb_ref[...])
    u, v = z[:, :D_SGU], z[:, D_SGU:]
    mu = jnp.mean(v, axis=-1, keepdims=True)
    vc = v - mu
    var = jnp.mean(vc * vc, axis=-1, keepdims=True)
    vn = (vc * lax.rsqrt(var + LN_EPS) * g_ref[...] + b_ref[...]).astype(BF16)
    tpos = lax.broadcasted_iota(jnp.int32, (SGU_BLOCK, SGU_BLOCK), 0) // CHUNK
    spos = lax.broadcasted_iota(jnp.int32, (SGU_BLOCK, SGU_BLOCK), 1) // CHUNK
    causal = spos <= tpos
    first_head = lax.broadcasted_iota(jnp.int32, (SGU_BLOCK, LANES), 1) < HEAD
    for p in range(D_SGU // LANES):
        sl = slice(LANES * p, LANES * (p + 1))
        ws0 = jnp.where(causal, ws_ref[2 * p], 0.0).astype(BF16)
        ws1 = jnp.where(causal, ws_ref[2 * p + 1], 0.0).astype(BF16)
        for n in range(pb_ref.shape[0] // SGU_BLOCK):
            rows = slice(SGU_BLOCK * n, SGU_BLOCK * (n + 1))
            vb = vn[rows, sl]
            sp = jnp.where(first_head,
                           jnp.dot(ws0, vb, preferred_element_type=F32),
                           jnp.dot(ws1, vb, preferred_element_type=F32))
            o_ref[rows, sl] = (u[rows, sl] * (sp + bias_ref[:, sl])).astype(o_ref.dtype)


def _sgu(pb, ln_g, ln_b, w_spatial, bias_tile, tm):
    t = pb.shape[0]
    nh = w_spatial.shape[0]
    return pl.pallas_call(
        _sgu_kernel,
        grid=(t // tm,),
        in_specs=[pl.BlockSpec((tm, 2 * D_SGU), lambda i: (i, 0)),
                  pl.BlockSpec((1, D_SGU), lambda i: (0, 0)),
                  pl.BlockSpec((1, D_SGU), lambda i: (0, 0)),
                  pl.BlockSpec((nh, SGU_BLOCK, SGU_BLOCK), lambda i: (0, 0, 0)),
                  pl.BlockSpec((SGU_BLOCK, D_SGU), lambda i: (0, 0))],
        out_specs=pl.BlockSpec((tm, D_SGU), lambda i: (i, 0)),
        out_shape=jax.ShapeDtypeStruct((t, D_SGU), BF16),
        compiler_params=_params(("arbitrary",)),
        name="sgu_mix",
    )(pb, ln_g, ln_b, w_spatial, bias_tile)


def _mid_kernel(y_ref, g_ref, bonus_ref, lng_ref, lnb_ref, yb_ref, x_ref,
                g1_ref, sc2_ref, sh2_ref, g2_ref, n2_ref,
                woa_ref, wob_ref, wrt_ref, w13_ref, w2_ref,
                h2_ref, logit_ref, base_ref):
    ones = _head_ones()
    y = y_ref[...]
    mu = _head_sum(y, ones) * (1.0 / HEAD)
    yc = y - mu
    var = _head_sum(yc * yc, ones) * (1.0 / HEAD)
    ya = yc * lax.rsqrt(var + LN_X_EPS) * lng_ref[...] + lnb_ref[...]
    ya = ((ya + bonus_ref[...]) * g_ref[...]).astype(BF16)
    yo = (jnp.dot(ya, woa_ref[...], preferred_element_type=F32)
          + jnp.dot(yb_ref[...], wob_ref[...], preferred_element_type=F32))
    x1 = x_ref[...] + g1_ref[...] * yo
    h2 = x1 * lax.rsqrt(jnp.mean(x1 * x1, axis=-1, keepdims=True) + RMS_EPS) * n2_ref[...]
    h2 = h2 * (1.0 + sc2_ref[...]) + sh2_ref[...]
    h2b = h2.astype(BF16)
    h2_ref[...] = h2b
    logit_ref[...] = _dot3(wrt_ref[...], h2, dims=(((1,), (1,)), ((), ())))
    hid = jnp.dot(h2b, w13_ref[...], preferred_element_type=F32)
    act = (_silu(hid[:, :D_EXPERT]) * hid[:, D_EXPERT:]).astype(BF16)
    ms = jnp.dot(act, w2_ref[...], preferred_element_type=F32)
    base_ref[...] = x1 + g2_ref[...] * ms


def _mid(y, g, bonus, lnx_g, lnx_b, yb, x, g1, sc2, sh2, g2, n2g,
         wo_a, wo_b, wr_t, w13s, w2s, tm, tiles_per_batch):
    t, d = x.shape
    tok = lambda w: pl.BlockSpec((tm, w), lambda i: (i, 0))
    const = lambda shape: pl.BlockSpec(shape, lambda i: (0,) * len(shape))
    mod = pl.BlockSpec((None, 1, d), lambda i: (i // tiles_per_batch, 0, 0))
    return pl.pallas_call(
        _mid_kernel,
        grid=(t // tm,),
        in_specs=[tok(D_RWKV), tok(D_RWKV), tok(D_RWKV),
                  const((1, D_RWKV)), const((1, D_RWKV)),
                  tok(D_SGU), tok(d), mod, mod, mod, mod, const((1, d)),
                  const(wo_a.shape), const(wo_b.shape), const(wr_t.shape),
                  const(w13s.shape), const(w2s.shape)],
        out_specs=[tok(d), pl.BlockSpec((LANES, tm), lambda i: (0, i)), tok(d)],
        out_shape=[jax.ShapeDtypeStruct((t, d), BF16),
                   jax.ShapeDtypeStruct((LANES, t), F32),
                   jax.ShapeDtypeStruct((t, d), F32)],
        compiler_params=_params(("arbitrary",)),
        name="outproj_norm_router_shared",
    )(y, g, bonus, lnx_g, lnx_b, yb, x, g1, sc2, sh2, g2, n2g,
      wo_a, wo_b, wr_t, w13s, w2s)


MOE_SUB = 128
MOE_CAP = 32
MOE_GROUP = 8


def _route_kernel(logit_ref, bias_ref, gate_ref, pos_ref, flag_ref):
    tt = logit_ref.shape[1]
    scores = _sigmoid(logit_ref[0:N_EXPERTS, :])
    biased = scores + bias_ref[...]
    neg_inf = -jnp.inf

    group_blocks, group_scores = [], []
    for gi in range(N_GROUPS):
        blk = biased[GROUP_SIZE * gi:GROUP_SIZE * (gi + 1), :]
        m1 = jnp.max(blk, axis=0, keepdims=True)
        is_max = blk == m1
        n_max = jnp.sum(jnp.where(is_max, 1.0, 0.0), axis=0, keepdims=True)
        m2 = jnp.max(jnp.where(is_max, neg_inf, blk), axis=0, keepdims=True)
        group_blocks.append(blk)
        group_scores.append(m1 + jnp.where(n_max >= 2.0, m1, m2))

    masked = []
    for gi in range(N_GROUPS):
        rank = jnp.zeros((1, tt), F32)
        for gj in range(N_GROUPS):
            if gj == gi:
                continue
            ahead = (group_scores[gj] >= group_scores[gi]) if gj < gi else \
                    (group_scores[gj] > group_scores[gi])
            rank += jnp.where(ahead, 1.0, 0.0)
        masked.append(jnp.where(rank < float(TOPK_GROUPS), group_blocks[gi], neg_inf))
    sub = lax.broadcasted_iota(jnp.int32, (GROUP_SIZE, tt), 0)
    ranks = [jnp.zeros((GROUP_SIZE, tt), F32) for _ in range(N_GROUPS)]
    for ej in range(N_EXPERTS):
        gj, oj = divmod(ej, GROUP_SIZE)
        other = masked[gj][oj:oj + 1, :]
        for gi in range(N_GROUPS):
            blk = masked[gi]
            if gi > gj:
                ahead = jnp.where(other >= blk, 1.0, 0.0)
            elif gi < gj:
                ahead = jnp.where(other > blk, 1.0, 0.0)
            else:
                ahead = jnp.where(sub > oj, jnp.where(other >= blk, 1.0, 0.0),
                                  jnp.where(other > blk, 1.0, 0.0))
            ranks[gi] = ranks[gi] + ahead
    rank = jnp.concatenate(ranks, axis=0)
    chosen = jnp.where(rank < float(TOP_K), 1.0, 0.0)
    sw = chosen * scores
    gate_ref[...] = sw / jnp.sum(sw, axis=0, keepdims=True) * ROUTED_SCALE

    before = (lax.broadcasted_iota(jnp.int32, (MOE_SUB, MOE_SUB), 0)
              < lax.broadcasted_iota(jnp.int32, (MOE_SUB, MOE_SUB), 1))
    prefix_ones = jnp.where(before, 1.0, 0.0).astype(BF16)
    all_ones = jnp.ones((MOE_SUB, MOE_SUB), BF16)
    positions, fullest = [], jnp.zeros((N_EXPERTS, MOE_SUB), F32)
    for s in range(tt // MOE_SUB):
        blk = chosen[:, MOE_SUB * s:MOE_SUB * (s + 1)].astype(BF16)
        prefix = jnp.dot(blk, prefix_ones, preferred_element_type=F32)
        positions.append(jnp.where(blk > 0, prefix, -1.0))
        fullest = jnp.maximum(fullest, jnp.dot(blk, all_ones, preferred_element_type=F32))
    pos_ref[...] = jnp.concatenate(positions, axis=1)
    worst = jnp.max(fullest, axis=0, keepdims=True)
    flag_ref[...] = jnp.broadcast_to(jnp.where(worst > float(MOE_CAP), 1, 0),
                                     flag_ref.shape).astype(jnp.int32)


def _route(logits_t, bias_tile, tt):
    t = logits_t.shape[1]
    n_tiles = t // tt
    tok = pl.BlockSpec((N_EXPERTS, tt), lambda i: (0, i))
    return pl.pallas_call(
        _route_kernel,
        grid=(n_tiles,),
        in_specs=[pl.BlockSpec((LANES, tt), lambda i: (0, i)),
                  pl.BlockSpec((N_EXPERTS, tt), lambda i: (0, 0))],
        out_specs=[tok, tok, pl.BlockSpec((SUBLANES, LANES), lambda i: (i, 0))],
        out_shape=[jax.ShapeDtypeStruct((N_EXPERTS, t), F32),
                   jax.ShapeDtypeStruct((N_EXPERTS, t), F32),
                   jax.ShapeDtypeStruct((n_tiles * SUBLANES, LANES), jnp.int32)],
        compiler_params=_params(("arbitrary",)),
        name="route_topk",
    )(logits_t, bias_tile)


def _moe_kernel(flag_ref, h2_ref, gate_ref, pos_ref, w13_ref, w2_ref, o_ref,
                xg_ref, gb_ref, y_ref, pt_ref):
    tile, eg = pl.program_id(0), pl.program_id(1)
    tm = h2_ref.shape[0]
    n_sub = tm // MOE_SUB
    ones = jnp.ones((MOE_SUB, D_EXPERT), BF16)

    @pl.when(eg == 0)
    def _():
        o_ref[...] = jnp.zeros_like(o_ref)

    def lane_broadcast(m):
        hi = m.astype(BF16)
        lo = (m - hi.astype(F32)).astype(BF16)
        return (jnp.dot(hi, ones, preferred_element_type=F32)
                + jnp.dot(lo, ones, preferred_element_type=F32))

    def hidden(x, gate_rows, j):
        hid = jnp.dot(x, w13_ref[j], preferred_element_type=F32)
        return (_silu(hid[:, :D_EXPERT]) * hid[:, D_EXPERT:] * gate_rows).astype(BF16)

    def pipelined(n, first, second):
        pending = first(0)
        for i in range(1, n):
            nxt = first(i)
            second(i - 1, pending)
            pending = nxt
        second(n - 1, pending)

    @pl.when(flag_ref[tile] == 0)
    def _():
        cap = MOE_CAP
        rows_e = n_sub * cap
        slot = lax.broadcasted_iota(jnp.int32, (cap, MOE_SUB), 0).astype(F32)
        for s in range(n_sub):
            cols = slice(MOE_SUB * s, MOE_SUB * (s + 1))
            pos, gate = pos_ref[:, cols], gate_ref[:, cols]
            hot = [jnp.where(slot == pos[j:j + 1, :], 1.0, 0.0) for j in range(MOE_GROUP)]
            p = jnp.concatenate(hot, axis=0)
            pg = jnp.concatenate([hot[j] * gate[j:j + 1, :] for j in range(MOE_GROUP)], axis=0)
            xg = jnp.dot(p.astype(BF16), h2_ref[cols, :],
                         preferred_element_type=F32).astype(BF16)
            gb = lane_broadcast(pg)
            for j in range(MOE_GROUP):
                dst = slice(j * rows_e + s * cap, j * rows_e + (s + 1) * cap)
                xg_ref[dst, :] = xg[j * cap:(j + 1) * cap]
                gb_ref[dst, :] = gb[j * cap:(j + 1) * cap]
            pt_ref[s] = p.T.astype(BF16)

        def first(j):
            rows = slice(j * rows_e, (j + 1) * rows_e)
            return hidden(xg_ref[rows, :], gb_ref[rows, :], j)

        def second(j, act):
            y_ref[j * rows_e:(j + 1) * rows_e, :] = jnp.dot(
                act, w2_ref[j], preferred_element_type=F32).astype(BF16)

        pipelined(MOE_GROUP, first, second)
        for s in range(n_sub):
            y = jnp.concatenate([y_ref[j * rows_e + s * cap:j * rows_e + (s + 1) * cap, :]
                                 for j in range(MOE_GROUP)], axis=0)
            o_ref[MOE_SUB * s:MOE_SUB * (s + 1), :] += jnp.dot(
                pt_ref[s], y, preferred_element_type=F32)

    @pl.when(flag_ref[tile] != 0)
    def _():
        eye = (lax.broadcasted_iota(jnp.int32, (MOE_SUB, MOE_SUB), 0)
               == lax.broadcasted_iota(jnp.int32, (MOE_SUB, MOE_SUB), 1))
        for j in range(MOE_GROUP):
            for s in range(n_sub):
                cols = slice(MOE_SUB * s, MOE_SUB * (s + 1))
                gb_ref[cols, :] = lane_broadcast(
                    jnp.where(eye, gate_ref[j:j + 1, cols], 0.0))
            act = hidden(h2_ref[...], gb_ref[0:tm, :], j)
            o_ref[...] += jnp.dot(act, w2_ref[j], preferred_element_type=F32)


def _moe(flags, h2, gates_t, pos_t, w13e, w2e, tm):
    t, d = h2.shape
    ne = w13e.shape[0]
    assert ne % MOE_GROUP == 0 and tm % MOE_SUB == 0
    n_sub = tm // MOE_SUB
    rows = MOE_GROUP * n_sub * MOE_CAP
    assert rows >= tm, "gate scratch is reused by the dense path"
    tok = pl.BlockSpec((tm, d), lambda i, e, f: (i, 0))
    per_expert = pl.BlockSpec((MOE_GROUP, tm), lambda i, e, f: (e, i))
    return pl.pallas_call(
        _moe_kernel,
        grid_spec=pltpu.PrefetchScalarGridSpec(
            num_scalar_prefetch=1,
            grid=(t // tm, ne // MOE_GROUP),
            in_specs=[tok, per_expert, per_expert,
                      pl.BlockSpec((MOE_GROUP, d, 2 * D_EXPERT), lambda i, e, f: (e, 0, 0)),
                      pl.BlockSpec((MOE_GROUP, D_EXPERT, d), lambda i, e, f: (e, 0, 0))],
            out_specs=tok,
            scratch_shapes=[pltpu.VMEM((rows, d), BF16),
                            pltpu.VMEM((rows, D_EXPERT), F32),
                            pltpu.VMEM((rows, d), BF16),
                            pltpu.VMEM((n_sub, MOE_SUB, MOE_GROUP * MOE_CAP), BF16)]),
        out_shape=jax.ShapeDtypeStruct((t, d), F32),
        compiler_params=_params(("arbitrary", "arbitrary")),
        name="moe_experts",
    )(flags, h2, gates_t, pos_t, w13e, w2e)


def _final_kernel(base_ref, routed_ref, g2_ref, nf_ref, o_ref):
    xo = base_ref[...] + g2_ref[...] * routed_ref[...]
    o_ref[...] = (xo * lax.rsqrt(jnp.mean(xo * xo, axis=-1, keepdims=True) + RMS_EPS)
                  * nf_ref[...])


def _final(base, routed, g2, nf, tm, tiles_per_batch):
    t, d = base.shape
    tok = pl.BlockSpec((tm, d), lambda i: (i, 0))
    return pl.pallas_call(
        _final_kernel,
        grid=(t // tm,),
        in_specs=[tok, tok,
                  pl.BlockSpec((None, 1, d), lambda i: (i // tiles_per_batch, 0, 0)),
                  pl.BlockSpec((1, d), lambda i: (0, 0))],
        out_specs=tok,
        out_shape=jax.ShapeDtypeStruct((t, d), F32),
        compiler_params=_params(("arbitrary",)),
        name="residual_final_norm",
    )(base, routed, g2, nf)


def _tile(n, want):
    t = min(n, want)
    assert n % t == 0, (n, t)
    return t


def kernel(x, c, w_ada, b_ada, norm1_g, w_in, mu_shift, w0, w_decay_up, a0, w_a_up, w_g_up,
           k_k, k_a, r_k, lnx_g, lnx_b, sgu_ln_g, sgu_ln_b, w_spatial, b_spatial, w_out,
           norm2_g, w_router, e_bias, w1_e, w3_e, w2_e, w1_s, w3_s, w2_s, norm_f_g):
    b, s, d = x.shape
    t = b * s
    assert w_ada.shape[0] == 1, "single-layer block"
    assert d == 2 * D_RWKV and s % SGU_BLOCK == 0
    row = lambda vec: vec.reshape(1, -1)

    c_pad = jnp.pad(c, ((0, 8 - b), (0, 0)))
    mod = _mod(c_pad, w_ada[0], b_ada)[:b]
    sh1, sc1, g1, sh2, sc2, g2 = [m[:, None, :] for m in jnp.split(mod, 6, axis=-1)]

    ps, pb = _inproj(x, sc1, sh1, norm1_g, w_in[0].astype(BF16), mu_shift, _tile(s, 512))

    wlora = jnp.zeros((LANES, 2 * D_RWKV), F32)
    wlora = wlora.at[:D_DECAY_LORA, :D_RWKV].set(w_decay_up[0])
    wlora = wlora.at[D_DECAY_LORA:, D_RWKV:].set(w_a_up[0])
    r, wdec, kh, v, kk, bb, g, bonus = _prep(
        ps.reshape(t, N_SHIFT), w0, a0, wlora, w_g_up[0], k_k, k_a, row(r_k), _tile(t, 512))

    seq = lambda z: z.reshape(b, s, D_RWKV)
    w_c, rh_c, u0_c, y0_c, bh_c, kh_c, pl_c = _chunk(r, wdec, kh, v, kk, bb)
    y = _rwkv(seq(w_c), seq(rh_c), seq(u0_c), seq(y0_c), seq(bh_c), seq(kh_c), seq(v),
              seq(pl_c), _tile(s, 256))

    bias_tile = jnp.repeat(b_spatial[0].T, HEAD, axis=1)
    yb = _sgu(pb.reshape(t, 2 * D_SGU), sgu_ln_g, sgu_ln_b, w_spatial[0], bias_tile,
              _tile(t, 512))

    tm_mid = _tile(s, 512)
    wr_t = jnp.pad(w_router[0].T, ((0, LANES - N_EXPERTS), (0, 0)))
    w13s = jnp.concatenate([w1_s[0], w3_s[0]], axis=1).astype(BF16)
    wo = w_out[0].astype(BF16)
    h2, logits_t, base = _mid(
        y.reshape(t, D_RWKV), g, bonus, lnx_g, lnx_b, yb, x.reshape(t, d),
        g1, sc2, sh2, g2, norm2_g, wo[:D_RWKV], wo[D_RWKV:], wr_t, w13s,
        w2_s[0].astype(BF16), tm_mid, s // tm_mid)

    tm_moe = _tile(t, 1024)
    gates_t, pos_t, flag_tiles = _route(
        logits_t, jnp.broadcast_to(e_bias[0][:, None], (N_EXPERTS, tm_moe)), tm_moe)
    flags = flag_tiles[::SUBLANES, 0]

    w13e = jnp.concatenate([w1_e[0], w3_e[0]], axis=2).astype(BF16)
    routed = _moe(flags, h2, gates_t, pos_t, w13e, w2_e[0].astype(BF16), tm_moe)
    tm_fin = _tile(s, 512)
    out = _final(base, routed, g2, row(norm_f_g), tm_fin, s // tm_fin)
    return out.reshape(b, s, d)
```

```python
import functools

import jax
import jax.numpy as jnp
from jax import lax
from jax.experimental import pallas as pl
from jax.experimental.pallas import tpu as pltpu

F32 = jnp.float32
BF16 = jnp.bfloat16

HEAD = 64
LANES = 128
SUBLANES = 8
D_RWKV = 512
D_SGU = 512
N_RWKV_PAIRS = D_RWKV // LANES
D_DECAY_LORA = 64
D_AAA_LORA = 64
D_GATE_LORA = 128
N_SHIFT = 3 * D_RWKV + D_DECAY_LORA + D_AAA_LORA + D_GATE_LORA
SGU_BLOCK = 128
CHUNK = 64
N_EXPERTS = 64
N_GROUPS = 8
GROUP_SIZE = N_EXPERTS // N_GROUPS
TOPK_GROUPS = 4
TOP_K = 8
D_EXPERT = 256
ROUTED_SCALE = 2.5
RMS_EPS = 1e-6
LN_EPS = 1e-5
LN_X_EPS = 64e-5
VMEM_LIMIT_BYTES = 56 * 1024 * 1024


def _params(semantics):
    return pltpu.CompilerParams(dimension_semantics=semantics,
                                vmem_limit_bytes=VMEM_LIMIT_BYTES)


def _split3(x):
    hi = x.astype(BF16)
    r1 = x - hi.astype(F32)
    mid = r1.astype(BF16)
    lo = (r1 - mid.astype(F32)).astype(BF16)
    return hi, mid, lo


def _dot_exact_rhs(x, rhs_bf16):
    hi, mid, lo = _split3(x)
    acc = jnp.dot(hi, rhs_bf16, preferred_element_type=F32)
    acc += jnp.dot(mid, rhs_bf16, preferred_element_type=F32)
    acc += jnp.dot(lo, rhs_bf16, preferred_element_type=F32)
    return acc


def _dot3(a, b, dims=(((1,), (0,)), ((), ()))):
    a_hi = a.astype(BF16)
    a_lo = (a - a_hi.astype(F32)).astype(BF16)
    b_hi = b.astype(BF16)
    b_lo = (b - b_hi.astype(F32)).astype(BF16)
    dg = functools.partial(lax.dot_general, dimension_numbers=dims,
                           preferred_element_type=F32)
    return dg(a_hi, b_hi) + dg(a_hi, b_lo) + dg(a_lo, b_hi)


def _head_ones():
    r = lax.broadcasted_iota(jnp.int32, (LANES, LANES), 0) // HEAD
    c = lax.broadcasted_iota(jnp.int32, (LANES, LANES), 1) // HEAD
    return jnp.where(r == c, 1.0, 0.0).astype(BF16)


def _head_sum(z, ones):
    parts = [_dot_exact_rhs(z[:, LANES * p:LANES * (p + 1)], ones)
             for p in range(z.shape[1] // LANES)]
    return jnp.concatenate(parts, axis=-1)


def _sigmoid(x):
    return 1.0 / (1.0 + jnp.exp(-x))


def _silu(x):
    return x * _sigmoid(x)


def _mod_kernel(c_ref, w_ref, b_ref, o_ref):
    s = _silu(c_ref[...])
    o_ref[...] = _dot3(s, w_ref[...]) + b_ref[...]


def _mod(c_pad, w_ada, b_ada):
    rows, d = c_pad.shape
    n = w_ada.shape[1]
    tn = 1024
    return pl.pallas_call(
        _mod_kernel,
        grid=(n // tn,),
        in_specs=[pl.BlockSpec((rows, d), lambda j: (0, 0)),
                  pl.BlockSpec((d, tn), lambda j: (0, j)),
                  pl.BlockSpec((1, tn), lambda j: (0, j))],
        out_specs=pl.BlockSpec((rows, tn), lambda j: (0, j)),
        out_shape=jax.ShapeDtypeStruct((rows, n), F32),
        compiler_params=_params(("arbitrary",)),
        name="adaln_mod",
    )(c_pad, w_ada, b_ada)


def _inproj_kernel(x_ref, sc_ref, sh_ref, g_ref, w_ref, mu_ref, ps_ref, pb_ref, carry_ref):
    @pl.when(pl.program_id(1) == 0)
    def _():
        carry_ref[...] = jnp.zeros_like(carry_ref)

    x = x_ref[...]
    tm = x.shape[0]
    h = x * lax.rsqrt(jnp.mean(x * x, axis=-1, keepdims=True) + RMS_EPS) * g_ref[...]
    h = h * (1.0 + sc_ref[...]) + sh_ref[...]
    proj = jnp.dot(h.astype(BF16), w_ref[...], preferred_element_type=F32)
    ps = proj[:, :N_SHIFT]
    prev = pltpu.roll(ps, 1, 0)
    first = lax.broadcasted_iota(jnp.int32, ps.shape, 0) == 0
    prev = jnp.where(first, carry_ref[0:1, :], prev)
    carry_ref[0:1, :] = ps[tm - 1:tm, :]
    ps_ref[...] = ps + (prev - ps) * mu_ref[...]
    pb_ref[...] = proj[:, N_SHIFT:]


def _inproj(x, sc1, sh1, g, w_in_bf16, mu, tm):
    b, s, d = x.shape
    d_in = w_in_bf16.shape[1]
    row = lambda bi, i: (bi, 0, 0)
    return pl.pallas_call(
        _inproj_kernel,
        grid=(b, s // tm),
        in_specs=[pl.BlockSpec((None, tm, d), lambda bi, i: (bi, i, 0)),
                  pl.BlockSpec((None, 1, d), row),
                  pl.BlockSpec((None, 1, d), row),
                  pl.BlockSpec((1, d), lambda bi, i: (0, 0)),
                  pl.BlockSpec((d, d_in), lambda bi, i: (0, 0)),
                  pl.BlockSpec((1, N_SHIFT), lambda bi, i: (0, 0))],
        out_specs=[pl.BlockSpec((None, tm, N_SHIFT), lambda bi, i: (bi, i, 0)),
                   pl.BlockSpec((None, tm, d_in - N_SHIFT), lambda bi, i: (bi, i, 0))],
        out_shape=[jax.ShapeDtypeStruct((b, s, N_SHIFT), F32),
                   jax.ShapeDtypeStruct((b, s, d_in - N_SHIFT), F32)],
        scratch_shapes=[pltpu.VMEM((8, N_SHIFT), F32)],
        compiler_params=_params(("arbitrary", "arbitrary")),
        name="inproj_shift",
    )(x, sc1, sh1, g, w_in_bf16, mu)


def _prep_kernel(ps_ref, w0_ref, a0_ref, wlora_ref, wg_ref, kk_ref, ka_ref, rk_ref,
                 r_o, w_o, k_o, v_o, kk_o, b_o, g_o, bonus_o):
    ps = ps_ref[...]
    c1, c2, c3 = D_RWKV, 2 * D_RWKV, 3 * D_RWKV
    r, k, v = ps[:, :c1], ps[:, c1:c2], ps[:, c2:c3]
    xwa = ps[:, c3:c3 + LANES]
    xg = ps[:, c3 + LANES:]
    is_w = lax.broadcasted_iota(jnp.int32, xwa.shape, 1) < D_DECAY_LORA
    lora = _dot3(jnp.where(is_w, jnp.tanh(xwa), xwa), wlora_ref[...])
    wlog = -jax.nn.softplus(-(w0_ref[...] + lora[:, :c1])) - 0.5
    log_decay = -jnp.exp(wlog)
    a = _sigmoid(a0_ref[...] + lora[:, c1:])
    g = _dot3(_sigmoid(xg), wg_ref[...])
    ones = _head_ones()
    kk = k * kk_ref[...]
    kk = kk / jnp.maximum(jnp.sqrt(_head_sum(kk * kk, ones)), 1e-12)
    kh = k * (1.0 + (a - 1.0) * ka_ref[...])
    r_o[...] = r
    w_o[...] = log_decay
    k_o[...] = kh
    v_o[...] = v
    kk_o[...] = kk
    b_o[...] = kk * a
    g_o[...] = g
    bonus_o[...] = _head_sum(r * kh * rk_ref[...], ones) * v


def _prep(ps, w0, a0, wlora, wg, k_k, k_a, r_k, tm):
    t = ps.shape[0]
    const = lambda shape: pl.BlockSpec(shape, lambda i: (0, 0))
    out = jax.ShapeDtypeStruct((t, D_RWKV), F32)
    return pl.pallas_call(
        _prep_kernel,
        grid=(t // tm,),
        in_specs=[pl.BlockSpec((tm, N_SHIFT), lambda i: (i, 0)),
                  const((1, D_RWKV)), const((1, D_RWKV)),
                  const((LANES, 2 * D_RWKV)), const((D_GATE_LORA, D_RWKV)),
                  const((1, D_RWKV)), const((1, D_RWKV)), const((1, D_RWKV))],
        out_specs=[pl.BlockSpec((tm, D_RWKV), lambda i: (i, 0))] * 8,
        out_shape=[out] * 8,
        compiler_params=_params(("arbitrary",)),
        name="rwkv_prep",
    )(ps, w0, a0, wlora, wg, k_k, k_a, r_k)


RWKV_CHUNK = 16
CHUNK_TILE = 128

_NN = (((1,), (0,)), ((), ()))
_NT = (((1,), (1,)), ((), ()))


def _dot1(a, b, dims=_NN):
    return lax.dot_general(a.astype(BF16), b.astype(BF16), dims, preferred_element_type=F32)


def _dot_exact_lhs(lhs_bf16, x):
    hi, mid, lo = _split3(x)
    acc = jnp.dot(lhs_bf16, hi, preferred_element_type=F32)
    acc += jnp.dot(lhs_bf16, mid, preferred_element_type=F32)
    acc += jnp.dot(lhs_bf16, lo, preferred_element_type=F32)
    return acc


def _chunk_kernel(r_ref, lw_ref, k_ref, v_ref, kk_ref, b_ref,
                  w_o, rh_o, u0_o, y0_o, bh_o, kh_o, pl_o):
    n = CHUNK_TILE
    tpos = lax.broadcasted_iota(jnp.int32, (n, n), 0)
    spos = lax.broadcasted_iota(jnp.int32, (n, n), 1)
    same = (tpos // RWKV_CHUNK) == (spos // RWKV_CHUNK)
    incl = same & (spos <= tpos)
    strict = same & (spos < tpos)
    tri_ones = jnp.where(incl, 1.0, 0.0).astype(BF16)
    blk_ones = jnp.where(same, 1.0, 0.0).astype(BF16)
    eye = jnp.where(tpos == spos, 1.0, 0.0)
    lane_head = lax.broadcasted_iota(jnp.int32, (n, LANES), 1) // HEAD

    pairs = range(N_RWKV_PAIRS)
    slabs = [slice(LANES * p, LANES * (p + 1)) for p in pairs]
    lw, r, k, v, kk, b = ([ref[:, sl] for sl in slabs]
                          for ref in (lw_ref, r_ref, k_ref, v_ref, kk_ref, b_ref))
    cum = [_dot_exact_lhs(tri_ones, x) for x in lw]
    tot = [_dot_exact_lhs(blk_ones, x) for x in lw]
    at = [-kk[p] * jnp.exp(cum[p] - lw[p]) for p in pairs]
    rt = [r[p] * jnp.exp(cum[p]) for p in pairs]
    inv = [jnp.exp(-cum[p]) for p in pairs]
    bt = [b[p] * inv[p] for p in pairs]
    kt = [k[p] * inv[p] for p in pairs]

    chains = [(p, h) for p in pairs for h in range(LANES // HEAD)]
    mine = [lane_head == h for h in range(LANES // HEAD)]
    a_h = [jnp.where(mine[h], at[p], 0.0) for p, h in chains]
    r_h = [jnp.where(mine[h], rt[p], 0.0) for p, h in chains]
    v_h = [jnp.where(mine[h], v[p], 0.0) for p, h in chains]
    each = range(len(chains))
    a_ab = [jnp.where(strict, _dot3(a_h[c], bt[chains[c][0]], _NT), 0.0) for c in each]
    a_ak = [jnp.where(strict, _dot3(a_h[c], kt[chains[c][0]], _NT), 0.0) for c in each]
    tinv = [eye + a for a in a_ab]
    x = a_ab
    for _ in range(RWKV_CHUNK.bit_length() - 2):
        x = [_dot3(xc, xc) for xc in x]
        tinv = [tc + _dot3(tc, xc) for tc, xc in zip(tinv, x)]
    akv = [_dot3(a_ak[c], v_h[c]) for c in each]
    m_rb = [jnp.where(incl, _dot1(r_h[c], bt[chains[c][0]], _NT), 0.0) for c in each]
    m_rk = [jnp.where(incl, _dot1(r_h[c], kt[chains[c][0]], _NT), 0.0) for c in each]
    w_h = [_dot3(tinv[c], a_h[c]) for c in each]
    u0_h = [_dot3(tinv[c], akv[c]) for c in each]
    rh_h = [r_h[c] + _dot1(m_rb[c], w_h[c]) for c in each]
    y0_h = [_dot1(m_rb[c], u0_h[c]) + _dot1(m_rk[c], v_h[c]) for c in each]

    for p in pairs:
        c0, c1 = 2 * p, 2 * p + 1
        rem = jnp.exp(tot[p] - cum[p])
        w_o[:, slabs[p]] = w_h[c0] + w_h[c1]
        rh_o[:, slabs[p]] = rh_h[c0] + rh_h[c1]
        u0_o[:, slabs[p]] = u0_h[c0] + u0_h[c1]
        y0_o[:, slabs[p]] = y0_h[c0] + y0_h[c1]
        bh_o[:, slabs[p]] = b[p] * rem
        kh_o[:, slabs[p]] = k[p] * rem
        pl_o[:, slabs[p]] = jnp.exp(tot[p])


def _chunk(r, lw, k, v, kk, b):
    t, d = r.shape
    spec = pl.BlockSpec((CHUNK_TILE, d), lambda i: (i, 0))
    out = jax.ShapeDtypeStruct((t, d), F32)
    return pl.pallas_call(
        _chunk_kernel,
        grid=(t // CHUNK_TILE,),
        in_specs=[spec] * 6,
        out_specs=[spec] * 7,
        out_shape=[out] * 7,
        compiler_params=_params(("arbitrary",)),
        name="rwkv_chunk_prep",
    )(r, lw, k, v, kk, b)


def _rwkv_kernel(w_ref, rh_ref, u0_ref, y0_ref, bh_ref, kh_ref, v_ref, pl_ref,
                 y_ref, state_ref):
    @pl.when(pl.program_id(0) == 0)
    def _():
        state_ref[...] = jnp.zeros_like(state_ref)

    nb, tt, _ = w_ref.shape
    L = RWKV_CHUNK
    row_head = lax.broadcasted_iota(jnp.int32, (LANES, LANES), 0) // HEAD
    col_head = lax.broadcasted_iota(jnp.int32, (LANES, LANES), 1) // HEAD
    same_head = row_head == col_head
    eye = (lax.broadcasted_iota(jnp.int32, (LANES, LANES), 0)
           == lax.broadcasted_iota(jnp.int32, (LANES, LANES), 1))
    all_ones = jnp.ones((LANES, LANES), BF16)
    pad = jnp.zeros((LANES - 2 * L, LANES), F32)

    def column_of(p_row):
        p_hi = pltpu.bitcast(pltpu.bitcast(p_row, jnp.uint32) & jnp.uint32(0xFFFF0000), F32)
        return (jnp.dot(jnp.where(eye, p_hi, 0.0).astype(BF16), all_ones,
                        preferred_element_type=F32)
                + jnp.dot(jnp.where(eye, p_row - p_hi, 0.0).astype(BF16), all_ones,
                          preferred_element_type=F32))

    def chunk(ci, carry):
        t0 = pl.multiple_of(ci * L, L)
        streams = [(bi, p) for bi in range(nb) for p in range(N_RWKV_PAIRS)]

        def rows(ref, n=L):
            return [ref[bi, pl.ds(t0, n), LANES * p:LANES * (p + 1)] for bi, p in streams]

        each = range(len(streams))
        s = [state_ref[bi * N_RWKV_PAIRS + p] for bi, p in streams]
        w, rh, u0, y0, bh, kh, v = (rows(ref) for ref in
                                   (w_ref, rh_ref, u0_ref, y0_ref, bh_ref, kh_ref, v_ref))
        p_col = [column_of(pr[0:1, :]) for pr in rows(pl_ref, SUBLANES)]
        bk_t = [jnp.concatenate([bh[c], kh[c], pad], axis=0).T for c in each]
        g = [_dot3(jnp.concatenate([w[c], rh[c]], axis=0), s[c]) for c in each]
        u = [g[c][:L] + u0[c] for c in each]
        upd = [_dot3(bk_t[c], jnp.concatenate([u[c], v[c], pad], axis=0)) for c in each]
        for c, (bi, p) in enumerate(streams):
            y_ref[bi, pl.ds(t0, L), LANES * p:LANES * (p + 1)] = g[c][L:] + y0[c]
            state_ref[bi * N_RWKV_PAIRS + p] = (p_col[c] * s[c]
                                                + jnp.where(same_head, upd[c], 0.0))
        return carry

    lax.fori_loop(0, tt // L, chunk, 0)


def _rwkv(w, rh, u0, y0, bh, kh, v, p_l, tt):
    nb, s, d = w.shape
    spec = pl.BlockSpec((nb, tt, d), lambda i: (0, i, 0))
    return pl.pallas_call(
        _rwkv_kernel,
        grid=(s // tt,),
        in_specs=[spec] * 8,
        out_specs=spec,
        out_shape=jax.ShapeDtypeStruct((nb, s, d), F32),
        scratch_shapes=[pltpu.VMEM((nb * N_RWKV_PAIRS, LANES, LANES), F32)],
        compiler_params=_params(("arbitrary",)),
        name="rwkv_scan",
    )(w, rh, u0, y0, bh, kh, v, p_l)


def _gelu_tanh(x):
    return 0.5 * x * (1.0 + jnp.tanh(0.7978845608028654 * (x + 0.044715 * (x * x * x))))


def _sgu_kernel(pb_ref, g_ref, b_ref, ws_ref, bias_ref, o_ref):
    z = _gelu_tanh(pb_ref[...])
    u, v = z[:, :D_SGU], z[:, D_SGU:]
    mu = jnp.mean(v, axis=-1, keepdims=True)
    vc = v - mu
    var = jnp.mean(vc * vc, axis=-1, keepdims=True)
    vn = (vc * lax.rsqrt(var + LN_EPS) * g_ref[...] + b_ref[...]).astype(BF16)
    tpos = lax.broadcasted_iota(jnp.int32, (SGU_BLOCK, SGU_BLOCK), 0) // CHUNK
    spos = lax.broadcasted_iota(jnp.int32, (SGU_BLOCK, SGU_BLOCK), 1) // CHUNK
    causal = spos <= tpos
    first_head = lax.broadcasted_iota(jnp.int32, (SGU_BLOCK, LANES), 1) < HEAD
    for p in range(D_SGU // LANES):
        sl = slice(LANES * p, LANES * (p + 1))
        ws0 = jnp.where(causal, ws_ref[2 * p], 0.0).astype(BF16)
        ws1 = jnp.where(causal, ws_ref[2 * p + 1], 0.0).astype(BF16)
        for n in range(pb_ref.shape[0] // SGU_BLOCK):
            rows = slice(SGU_BLOCK * n, SGU_BLOCK * (n + 1))
            vb = vn[rows, sl]
            sp = jnp.where(first_head,
                           jnp.dot(ws0, vb, preferred_element_type=F32),
                           jnp.dot(ws1, vb, preferred_element_type=F32))
            o_ref[rows, sl] = (u[rows, sl] * (sp + bias_ref[:, sl])).astype(o_ref.dtype)


def _sgu(pb, ln_g, ln_b, w_spatial, bias_tile, tm):
    t = pb.shape[0]
    nh = w_spatial.shape[0]
    return pl.pallas_call(
        _sgu_kernel,
        grid=(t // tm,),
        in_specs=[pl.BlockSpec((tm, 2 * D_SGU), lambda i: (i, 0)),
                  pl.BlockSpec((1, D_SGU), lambda i: (0, 0)),
                  pl.BlockSpec((1, D_SGU), lambda i: (0, 0)),
                  pl.BlockSpec((nh, SGU_BLOCK, SGU_BLOCK), lambda i: (0, 0, 0)),
                  pl.BlockSpec((SGU_BLOCK, D_SGU), lambda i: (0, 0))],
        out_specs=pl.BlockSpec((tm, D_SGU), lambda i: (i, 0)),
        out_shape=jax.ShapeDtypeStruct((t, D_SGU), BF16),
        compiler_params=_params(("arbitrary",)),
        name="sgu_mix",
    )(pb, ln_g, ln_b, w_spatial, bias_tile)


def _mid_kernel(y_ref, g_ref, bonus_ref, lng_ref, lnb_ref, yb_ref, x_ref,
                g1_ref, sc2_ref, sh2_ref, g2_ref, n2_ref,
                woa_ref, wob_ref, wrt_ref, w13_ref, w2_ref,
                h2_ref, logit_ref, base_ref):
    ones = _head_ones()
    y = y_ref[...]
    mu = _head_sum(y, ones) * (1.0 / HEAD)
    yc = y - mu
    var = _head_sum(yc * yc, ones) * (1.0 / HEAD)
    ya = yc * lax.rsqrt(var + LN_X_EPS) * lng_ref[...] + lnb_ref[...]
    ya = ((ya + bonus_ref[...]) * g_ref[...]).astype(BF16)
    yo = (jnp.dot(ya, woa_ref[...], preferred_element_type=F32)
          + jnp.dot(yb_ref[...], wob_ref[...], preferred_element_type=F32))
    x1 = x_ref[...] + g1_ref[...] * yo
    h2 = x1 * lax.rsqrt(jnp.mean(x1 * x1, axis=-1, keepdims=True) + RMS_EPS) * n2_ref[...]
    h2 = h2 * (1.0 + sc2_ref[...]) + sh2_ref[...]
    h2b = h2.astype(BF16)
    h2_ref[...] = h2b
    logit_ref[...] = _dot3(wrt_ref[...], h2, dims=(((1,), (1,)), ((), ())))
    hid = jnp.dot(h2b, w13_ref[...], preferred_element_type=F32)
    act = (_silu(hid[:, :D_EXPERT]) * hid[:, D_EXPERT:]).astype(BF16)
    ms = jnp.dot(act, w2_ref[...], preferred_element_type=F32)
    base_ref[...] = x1 + g2_ref[...] * ms


def _mid(y, g, bonus, lnx_g, lnx_b, yb, x, g1, sc2, sh2, g2, n2g,
         wo_a, wo_b, wr_t, w13s, w2s, tm, tiles_per_batch):
    t, d = x.shape
    tok = lambda w: pl.BlockSpec((tm, w), lambda i: (i, 0))
    const = lambda shape: pl.BlockSpec(shape, lambda i: (0,) * len(shape))
    mod = pl.BlockSpec((None, 1, d), lambda i: (i // tiles_per_batch, 0, 0))
    return pl.pallas_call(
        _mid_kernel,
        grid=(t // tm,),
        in_specs=[tok(D_RWKV), tok(D_RWKV), tok(D_RWKV),
                  const((1, D_RWKV)), const((1, D_RWKV)),
                  tok(D_SGU), tok(d), mod, mod, mod, mod, const((1, d)),
                  const(wo_a.shape), const(wo_b.shape), const(wr_t.shape),
                  const(w13s.shape), const(w2s.shape)],
        out_specs=[tok(d), pl.BlockSpec((LANES, tm), lambda i: (0, i)), tok(d)],
        out_shape=[jax.ShapeDtypeStruct((t, d), BF16),
                   jax.ShapeDtypeStruct((LANES, t), F32),
                   jax.ShapeDtypeStruct((t, d), F32)],
        compiler_params=_params(("arbitrary",)),
        name="outproj_norm_router_shared",
    )(y, g, bonus, lnx_g, lnx_b, yb, x, g1, sc2, sh2, g2, n2g,
      wo_a, wo_b, wr_t, w13s, w2s)


MOE_SUB = 128
MOE_CAP = 32
MOE_GROUP = 8


def _route_kernel(logit_ref, bias_ref, gate_ref, pos_ref, fexp_ref, fsub_ref, fdense_ref):
    tt = logit_ref.shape[1]
    scores = _sigmoid(logit_ref[0:N_EXPERTS, :])
    biased = scores + bias_ref[...]
    neg_inf = -jnp.inf

    group_blocks, group_scores = [], []
    for gi in range(N_GROUPS):
        blk = biased[GROUP_SIZE * gi:GROUP_SIZE * (gi + 1), :]
        m1 = jnp.max(blk, axis=0, keepdims=True)
        is_max = blk == m1
        n_max = jnp.sum(jnp.where(is_max, 1.0, 0.0), axis=0, keepdims=True)
        m2 = jnp.max(jnp.where(is_max, neg_inf, blk), axis=0, keepdims=True)
        group_blocks.append(blk)
        group_scores.append(m1 + jnp.where(n_max >= 2.0, m1, m2))

    masked = []
    for gi in range(N_GROUPS):
        rank = jnp.zeros((1, tt), F32)
        for gj in range(N_GROUPS):
            if gj == gi:
                continue
            ahead = (group_scores[gj] >= group_scores[gi]) if gj < gi else \
                    (group_scores[gj] > group_scores[gi])
            rank += jnp.where(ahead, 1.0, 0.0)
        masked.append(jnp.where(rank < float(TOPK_GROUPS), group_blocks[gi], neg_inf))
    sub = lax.broadcasted_iota(jnp.int32, (GROUP_SIZE, tt), 0)
    ranks = [jnp.zeros((GROUP_SIZE, tt), F32) for _ in range(N_GROUPS)]
    for ej in range(N_EXPERTS):
        gj, oj = divmod(ej, GROUP_SIZE)
        other = masked[gj][oj:oj + 1, :]
        for gi in range(N_GROUPS):
            blk = masked[gi]
            if gi > gj:
                ahead = jnp.where(other >= blk, 1.0, 0.0)
            elif gi < gj:
                ahead = jnp.where(other > blk, 1.0, 0.0)
            else:
                ahead = jnp.where(sub > oj, jnp.where(other >= blk, 1.0, 0.0),
                                  jnp.where(other > blk, 1.0, 0.0))
            ranks[gi] = ranks[gi] + ahead
    rank = jnp.concatenate(ranks, axis=0)
    chosen = jnp.where(rank < float(TOP_K), 1.0, 0.0)
    sw = chosen * scores
    gate_ref[...] = sw / jnp.sum(sw, axis=0, keepdims=True) * ROUTED_SCALE

    before = (lax.broadcasted_iota(jnp.int32, (MOE_SUB, MOE_SUB), 0)
              < lax.broadcasted_iota(jnp.int32, (MOE_SUB, MOE_SUB), 1))
    prefix_ones = jnp.where(before, 1.0, 0.0).astype(BF16)
    all_ones = jnp.ones((MOE_SUB, MOE_SUB), BF16)
    lane = lax.broadcasted_iota(jnp.int32, (N_EXPERTS, LANES), 1)
    positions, counts = [], jnp.zeros((N_EXPERTS, LANES), F32)
    for s in range(tt // MOE_SUB):
        blk = chosen[:, MOE_SUB * s:MOE_SUB * (s + 1)].astype(BF16)
        prefix = jnp.dot(blk, prefix_ones, preferred_element_type=F32)
        positions.append(jnp.where(blk > 0, prefix, -1.0))
        total = jnp.dot(blk, all_ones, preferred_element_type=F32)
        counts = counts + jnp.where(lane == s, total, 0.0)
    pos_ref[...] = jnp.concatenate(positions, axis=1)
    per_expert = jnp.max(counts, axis=1, keepdims=True)
    fexp_ref[...] = jnp.broadcast_to(jnp.where(per_expert > float(MOE_CAP), 1, 0),
                                     fexp_ref.shape).astype(jnp.int32)
    per_group = jnp.concatenate(
        [jnp.max(counts[MOE_GROUP * g:MOE_GROUP * (g + 1), :], axis=0, keepdims=True)
         for g in range(N_EXPERTS // MOE_GROUP)], axis=0)
    fsub_ref[...] = jnp.where(per_group > float(MOE_CAP), 1, 0).astype(jnp.int32)
    group_worst = jnp.max(per_group, axis=1, keepdims=True)
    fdense_ref[...] = jnp.broadcast_to(jnp.where(group_worst > float(2 * MOE_CAP), 1, 0),
                                       fdense_ref.shape).astype(jnp.int32)


def _route(logits_t, bias_tile, tt):
    t = logits_t.shape[1]
    n_tiles = t // tt
    n_groups = N_EXPERTS // MOE_GROUP
    tok = pl.BlockSpec((N_EXPERTS, tt), lambda i: (0, i))
    return pl.pallas_call(
        _route_kernel,
        grid=(n_tiles,),
        in_specs=[pl.BlockSpec((LANES, tt), lambda i: (0, i)),
                  pl.BlockSpec((N_EXPERTS, tt), lambda i: (0, 0))],
        out_specs=[tok, tok,
                   pl.BlockSpec((N_EXPERTS, LANES), lambda i: (i, 0)),
                   pl.BlockSpec((n_groups, LANES), lambda i: (i, 0)),
                   pl.BlockSpec((n_groups, LANES), lambda i: (i, 0))],
        out_shape=[jax.ShapeDtypeStruct((N_EXPERTS, t), F32),
                   jax.ShapeDtypeStruct((N_EXPERTS, t), F32),
                   jax.ShapeDtypeStruct((n_tiles * N_EXPERTS, LANES), jnp.int32),
                   jax.ShapeDtypeStruct((n_tiles * n_groups, LANES), jnp.int32),
                   jax.ShapeDtypeStruct((n_tiles * n_groups, LANES), jnp.int32)],
        compiler_params=_params(("arbitrary",)),
        name="route_topk",
    )(logits_t, bias_tile)


def _moe_kernel(fexp_ref, fsub_ref, fdense_ref, h2_ref, gate_ref, pos_ref, w13_ref, w2_ref,
                o_ref, xy_ref, gb_ref, pt_ref):
    tile, eg = pl.program_id(0), pl.program_id(1)
    n_groups = pl.num_programs(1)
    tm = h2_ref.shape[0]
    n_sub = tm // MOE_SUB
    ones = jnp.ones((MOE_SUB, D_EXPERT), BF16)

    @pl.when(eg == 0)
    def _():
        o_ref[...] = jnp.zeros_like(o_ref)

    @pl.when((tile == 0) & (eg == 0))
    def _():
        xy_ref[...] = jnp.zeros_like(xy_ref)
        gb_ref[...] = jnp.zeros_like(gb_ref)

    def lane_broadcast(m):
        hi = m.astype(BF16)
        lo = (m - hi.astype(F32)).astype(BF16)
        return (jnp.dot(hi, ones, preferred_element_type=F32)
                + jnp.dot(lo, ones, preferred_element_type=F32))

    def hidden(x, gate_rows, j):
        hid = jnp.dot(x, w13_ref[j], preferred_element_type=F32)
        return (_silu(hid[:, :D_EXPERT]) * hid[:, D_EXPERT:] * gate_rows).astype(BF16)

    def pipelined(n, first, second):
        pending = first(0)
        for i in range(1, n):
            nxt = first(i)
            second(i - 1, pending)
            pending = nxt
        second(n - 1, pending)

    cap = MOE_CAP
    slab_rows = n_sub * cap
    go_dense = fdense_ref[tile * n_groups + eg] != 0

    def needs_second(s):
        return fsub_ref[(tile * n_groups + eg) * n_sub + s] != 0

    def slab_slice(j, q, s=None):
        start = (2 * j + q) * slab_rows
        if s is None:
            return slice(start, start + slab_rows)
        return slice(start + s * cap, start + (s + 1) * cap)

    def gather(s, q):
        cols = slice(MOE_SUB * s, MOE_SUB * (s + 1))
        slot = lax.broadcasted_iota(jnp.int32, (cap, MOE_SUB), 0).astype(F32) + float(q * cap)
        pos, gate = pos_ref[:, cols], gate_ref[:, cols]
        hot = [jnp.where(slot == pos[j:j + 1, :], 1.0, 0.0) for j in range(MOE_GROUP)]
        p = jnp.concatenate(hot, axis=0)
        pg = jnp.concatenate([hot[j] * gate[j:j + 1, :] for j in range(MOE_GROUP)], axis=0)
        xg = jnp.dot(p.astype(BF16), h2_ref[cols, :],
                     preferred_element_type=F32).astype(BF16)
        gb = lane_broadcast(pg)
        for j in range(MOE_GROUP):
            xy_ref[slab_slice(j, q, s), :] = xg[j * cap:(j + 1) * cap]
            gb_ref[slab_slice(j, q, s), :] = gb[j * cap:(j + 1) * cap]
        pt_ref[s, :, q * MOE_GROUP * cap:(q + 1) * MOE_GROUP * cap] = p.T.astype(BF16)

    def scatter(s, q):
        y = jnp.concatenate([xy_ref[slab_slice(j, q, s), :] for j in range(MOE_GROUP)], axis=0)
        o_ref[MOE_SUB * s:MOE_SUB * (s + 1), :] += jnp.dot(
            pt_ref[s, :, q * MOE_GROUP * cap:(q + 1) * MOE_GROUP * cap], y,
            preferred_element_type=F32)

    def project(j, q, act):
        xy_ref[slab_slice(j, q), :] = jnp.dot(
            act, w2_ref[j], preferred_element_type=F32).astype(BF16)

    @pl.when(jnp.logical_not(go_dense))
    def _():
        for s in range(n_sub):
            gather(s, 0)
            pl.when(needs_second(s))(functools.partial(gather, s, 1))
        pipelined(MOE_GROUP,
                  lambda j: hidden(xy_ref[slab_slice(j, 0), :], gb_ref[slab_slice(j, 0), :], j),
                  lambda j, act: project(j, 0, act))
        for j in range(MOE_GROUP):
            @pl.when(fexp_ref[tile * N_EXPERTS + eg * MOE_GROUP + j] != 0)
            def _(j=j):
                project(j, 1, hidden(xy_ref[slab_slice(j, 1), :],
                                     gb_ref[slab_slice(j, 1), :], j))
        for s in range(n_sub):
            scatter(s, 0)
            pl.when(needs_second(s))(functools.partial(scatter, s, 1))

    @pl.when(go_dense)
    def _():
        eye = (lax.broadcasted_iota(jnp.int32, (MOE_SUB, MOE_SUB), 0)
               == lax.broadcasted_iota(jnp.int32, (MOE_SUB, MOE_SUB), 1))
        for j in range(MOE_GROUP):
            for s in range(n_sub):
                cols = slice(MOE_SUB * s, MOE_SUB * (s + 1))
                gb_ref[cols, :] = lane_broadcast(
                    jnp.where(eye, gate_ref[j:j + 1, cols], 0.0))
            act = hidden(h2_ref[...], gb_ref[0:tm, :], j)
            o_ref[...] += jnp.dot(act, w2_ref[j], preferred_element_type=F32)


def _moe(fexp, fsub, fdense, h2, gates_t, pos_t, w13e, w2e, tm):
    t, d = h2.shape
    ne = w13e.shape[0]
    assert ne % MOE_GROUP == 0 and tm % MOE_SUB == 0
    n_sub = tm // MOE_SUB
    rows = MOE_GROUP * 2 * n_sub * MOE_CAP
    assert rows >= tm, "gate scratch is reused by the dense path"
    tok = pl.BlockSpec((tm, d), lambda i, e, *_: (i, 0))
    per_expert = pl.BlockSpec((MOE_GROUP, tm), lambda i, e, *_: (e, i))
    return pl.pallas_call(
        _moe_kernel,
        grid_spec=pltpu.PrefetchScalarGridSpec(
            num_scalar_prefetch=3,
            grid=(t // tm, ne // MOE_GROUP),
            in_specs=[tok, per_expert, per_expert,
                      pl.BlockSpec((MOE_GROUP, d, 2 * D_EXPERT), lambda i, e, *_: (e, 0, 0)),
                      pl.BlockSpec((MOE_GROUP, D_EXPERT, d), lambda i, e, *_: (e, 0, 0))],
            out_specs=tok,
            scratch_shapes=[pltpu.VMEM((rows, d), BF16),
                            pltpu.VMEM((rows, D_EXPERT), F32),
                            pltpu.VMEM((n_sub, MOE_SUB, 2 * MOE_GROUP * MOE_CAP), BF16)]),
        out_shape=jax.ShapeDtypeStruct((t, d), F32),
        compiler_params=_params(("arbitrary", "arbitrary")),
        name="moe_experts",
    )(fexp, fsub, fdense, h2, gates_t, pos_t, w13e, w2e)


def _final_kernel(base_ref, routed_ref, g2_ref, nf_ref, o_ref):
    xo = base_ref[...] + g2_ref[...] * routed_ref[...]
    o_ref[...] = (xo * lax.rsqrt(jnp.mean(xo * xo, axis=-1, keepdims=True) + RMS_EPS)
                  * nf_ref[...])


def _final(base, routed, g2, nf, tm, tiles_per_batch):
    t, d = base.shape
    tok = pl.BlockSpec((tm, d), lambda i: (i, 0))
    return pl.pallas_call(
        _final_kernel,
        grid=(t // tm,),
        in_specs=[tok, tok,
                  pl.BlockSpec((None, 1, d), lambda i: (i // tiles_per_batch, 0, 0)),
                  pl.BlockSpec((1, d), lambda i: (0, 0))],
        out_specs=tok,
        out_shape=jax.ShapeDtypeStruct((t, d), F32),
        compiler_params=_params(("arbitrary",)),
        name="residual_final_norm",
    )(base, routed, g2, nf)


def _tile(n, want):
    t = min(n, want)
    assert n % t == 0, (n, t)
    return t


def kernel(x, c, w_ada, b_ada, norm1_g, w_in, mu_shift, w0, w_decay_up, a0, w_a_up, w_g_up,
           k_k, k_a, r_k, lnx_g, lnx_b, sgu_ln_g, sgu_ln_b, w_spatial, b_spatial, w_out,
           norm2_g, w_router, e_bias, w1_e, w3_e, w2_e, w1_s, w3_s, w2_s, norm_f_g):
    b, s, d = x.shape
    t = b * s
    assert w_ada.shape[0] == 1, "single-layer block"
    assert d == 2 * D_RWKV and s % SGU_BLOCK == 0
    row = lambda vec: vec.reshape(1, -1)

    c_pad = jnp.pad(c, ((0, 8 - b), (0, 0)))
    mod = _mod(c_pad, w_ada[0], b_ada)[:b]
    sh1, sc1, g1, sh2, sc2, g2 = [m[:, None, :] for m in jnp.split(mod, 6, axis=-1)]

    ps, pb = _inproj(x, sc1, sh1, norm1_g, w_in[0].astype(BF16), mu_shift, _tile(s, 512))

    wlora = jnp.zeros((LANES, 2 * D_RWKV), F32)
    wlora = wlora.at[:D_DECAY_LORA, :D_RWKV].set(w_decay_up[0])
    wlora = wlora.at[D_DECAY_LORA:, D_RWKV:].set(w_a_up[0])
    r, wdec, kh, v, kk, bb, g, bonus = _prep(
        ps.reshape(t, N_SHIFT), w0, a0, wlora, w_g_up[0], k_k, k_a, row(r_k), _tile(t, 512))

    seq = lambda z: z.reshape(b, s, D_RWKV)
    w_c, rh_c, u0_c, y0_c, bh_c, kh_c, pl_c = _chunk(r, wdec, kh, v, kk, bb)
    y = _rwkv(seq(w_c), seq(rh_c), seq(u0_c), seq(y0_c), seq(bh_c), seq(kh_c), seq(v),
              seq(pl_c), _tile(s, 256))

    bias_tile = jnp.repeat(b_spatial[0].T, HEAD, axis=1)
    yb = _sgu(pb.reshape(t, 2 * D_SGU), sgu_ln_g, sgu_ln_b, w_spatial[0], bias_tile,
              _tile(t, 512))

    tm_mid = _tile(s, 512)
    wr_t = jnp.pad(w_router[0].T, ((0, LANES - N_EXPERTS), (0, 0)))
    w13s = jnp.concatenate([w1_s[0], w3_s[0]], axis=1).astype(BF16)
    wo = w_out[0].astype(BF16)
    h2, logits_t, base = _mid(
        y.reshape(t, D_RWKV), g, bonus, lnx_g, lnx_b, yb, x.reshape(t, d),
        g1, sc2, sh2, g2, norm2_g, wo[:D_RWKV], wo[D_RWKV:], wr_t, w13s,
        w2_s[0].astype(BF16), tm_mid, s // tm_mid)

    tm_moe = _tile(t, 1024)
    gates_t, pos_t, fexp, fsub, fdense = _route(
        logits_t, jnp.broadcast_to(e_bias[0][:, None], (N_EXPERTS, tm_moe)), tm_moe)

    w13e = jnp.concatenate([w1_e[0], w3_e[0]], axis=2).astype(BF16)
    routed = _moe(fexp[:, 0], fsub[:, :tm_moe // MOE_SUB].reshape(-1), fdense[:, 0],
                  h2, gates_t, pos_t, w13e, w2_e[0].astype(BF16), tm_moe)
    tm_fin = _tile(s, 512)
    out = _final(base, routed, g2, row(norm_f_g), tm_fin, s // tm_fin)
    return out.reshape(b, s, d)
```

```python
import functools

import jax
import jax.numpy as jnp
from jax import lax
from jax.experimental import pallas as pl
from jax.experimental.pallas import tpu as pltpu

F32 = jnp.float32
BF16 = jnp.bfloat16

HEAD = 64
LANES = 128
SUBLANES = 8
D_RWKV = 512
D_SGU = 512
N_RWKV_PAIRS = D_RWKV // LANES
D_DECAY_LORA = 64
D_AAA_LORA = 64
D_GATE_LORA = 128
N_SHIFT = 3 * D_RWKV + D_DECAY_LORA + D_AAA_LORA + D_GATE_LORA
SGU_BLOCK = 128
CHUNK = 64
N_EXPERTS = 64
N_GROUPS = 8
GROUP_SIZE = N_EXPERTS // N_GROUPS
TOPK_GROUPS = 4
TOP_K = 8
D_EXPERT = 256
ROUTED_SCALE = 2.5
RMS_EPS = 1e-6
LN_EPS = 1e-5
LN_X_EPS = 64e-5
VMEM_LIMIT_BYTES = 56 * 1024 * 1024


def _params(semantics):
    return pltpu.CompilerParams(dimension_semantics=semantics,
                                vmem_limit_bytes=VMEM_LIMIT_BYTES)


def _split3(x):
    hi = x.astype(BF16)
    r1 = x - hi.astype(F32)
    mid = r1.astype(BF16)
    lo = (r1 - mid.astype(F32)).astype(BF16)
    return hi, mid, lo


def _dot_exact_rhs(x, rhs_bf16):
    hi, mid, lo = _split3(x)
    acc = jnp.dot(hi, rhs_bf16, preferred_element_type=F32)
    acc += jnp.dot(mid, rhs_bf16, preferred_element_type=F32)
    acc += jnp.dot(lo, rhs_bf16, preferred_element_type=F32)
    return acc


def _dot3(a, b, dims=(((1,), (0,)), ((), ()))):
    a_hi = a.astype(BF16)
    a_lo = (a - a_hi.astype(F32)).astype(BF16)
    b_hi = b.astype(BF16)
    b_lo = (b - b_hi.astype(F32)).astype(BF16)
    dg = functools.partial(lax.dot_general, dimension_numbers=dims,
                           preferred_element_type=F32)
    return dg(a_hi, b_hi) + dg(a_hi, b_lo) + dg(a_lo, b_hi)


def _head_ones():
    r = lax.broadcasted_iota(jnp.int32, (LANES, LANES), 0) // HEAD
    c = lax.broadcasted_iota(jnp.int32, (LANES, LANES), 1) // HEAD
    return jnp.where(r == c, 1.0, 0.0).astype(BF16)


def _head_sum(z, ones):
    parts = [_dot_exact_rhs(z[:, LANES * p:LANES * (p + 1)], ones)
             for p in range(z.shape[1] // LANES)]
    return jnp.concatenate(parts, axis=-1)


def _sigmoid(x):
    return 1.0 / (1.0 + jnp.exp(-x))


def _silu(x):
    return x * _sigmoid(x)


def _mod_kernel(c_ref, w_ref, b_ref, o_ref):
    s = _silu(c_ref[...])
    o_ref[...] = _dot3(s, w_ref[...]) + b_ref[...]


def _mod(c_pad, w_ada, b_ada):
    rows, d = c_pad.shape
    n = w_ada.shape[1]
    tn = 1024
    return pl.pallas_call(
        _mod_kernel,
        grid=(n // tn,),
        in_specs=[pl.BlockSpec((rows, d), lambda j: (0, 0)),
                  pl.BlockSpec((d, tn), lambda j: (0, j)),
                  pl.BlockSpec((1, tn), lambda j: (0, j))],
        out_specs=pl.BlockSpec((rows, tn), lambda j: (0, j)),
        out_shape=jax.ShapeDtypeStruct((rows, n), F32),
        compiler_params=_params(("arbitrary",)),
        name="adaln_mod",
    )(c_pad, w_ada, b_ada)


def _inproj_kernel(x_ref, sc_ref, sh_ref, g_ref, w_ref, mu_ref, ps_ref, pb_ref, carry_ref):
    @pl.when(pl.program_id(1) == 0)
    def _():
        carry_ref[...] = jnp.zeros_like(carry_ref)

    x = x_ref[...]
    tm = x.shape[0]
    h = x * lax.rsqrt(jnp.mean(x * x, axis=-1, keepdims=True) + RMS_EPS) * g_ref[...]
    h = h * (1.0 + sc_ref[...]) + sh_ref[...]
    proj = jnp.dot(h.astype(BF16), w_ref[...], preferred_element_type=F32)
    ps = proj[:, :N_SHIFT]
    prev = pltpu.roll(ps, 1, 0)
    first = lax.broadcasted_iota(jnp.int32, ps.shape, 0) == 0
    prev = jnp.where(first, carry_ref[0:1, :], prev)
    carry_ref[0:1, :] = ps[tm - 1:tm, :]
    ps_ref[...] = ps + (prev - ps) * mu_ref[...]
    pb_ref[...] = proj[:, N_SHIFT:]


def _inproj(x, sc1, sh1, g, w_in_bf16, mu, tm):
    b, s, d = x.shape
    d_in = w_in_bf16.shape[1]
    row = lambda bi, i: (bi, 0, 0)
    return pl.pallas_call(
        _inproj_kernel,
        grid=(b, s // tm),
        in_specs=[pl.BlockSpec((None, tm, d), lambda bi, i: (bi, i, 0)),
                  pl.BlockSpec((None, 1, d), row),
                  pl.BlockSpec((None, 1, d), row),
                  pl.BlockSpec((1, d), lambda bi, i: (0, 0)),
                  pl.BlockSpec((d, d_in), lambda bi, i: (0, 0)),
                  pl.BlockSpec((1, N_SHIFT), lambda bi, i: (0, 0))],
        out_specs=[pl.BlockSpec((None, tm, N_SHIFT), lambda bi, i: (bi, i, 0)),
                   pl.BlockSpec((None, tm, d_in - N_SHIFT), lambda bi, i: (bi, i, 0))],
        out_shape=[jax.ShapeDtypeStruct((b, s, N_SHIFT), F32),
                   jax.ShapeDtypeStruct((b, s, d_in - N_SHIFT), F32)],
        scratch_shapes=[pltpu.VMEM((8, N_SHIFT), F32)],
        compiler_params=_params(("arbitrary", "arbitrary")),
        name="inproj_shift",
    )(x, sc1, sh1, g, w_in_bf16, mu)


def _prep_kernel(ps_ref, w0_ref, a0_ref, wlora_ref, wg_ref, kk_ref, ka_ref, rk_ref,
                 r_o, w_o, k_o, v_o, kk_o, b_o, g_o, bonus_o):
    ps = ps_ref[...]
    c1, c2, c3 = D_RWKV, 2 * D_RWKV, 3 * D_RWKV
    r, k, v = ps[:, :c1], ps[:, c1:c2], ps[:, c2:c3]
    xwa = ps[:, c3:c3 + LANES]
    xg = ps[:, c3 + LANES:]
    is_w = lax.broadcasted_iota(jnp.int32, xwa.shape, 1) < D_DECAY_LORA
    lora = _dot3(jnp.where(is_w, jnp.tanh(xwa), xwa), wlora_ref[...])
    wlog = -jax.nn.softplus(-(w0_ref[...] + lora[:, :c1])) - 0.5
    log_decay = -jnp.exp(wlog)
    a = _sigmoid(a0_ref[...] + lora[:, c1:])
    g = _dot3(_sigmoid(xg), wg_ref[...])
    ones = _head_ones()
    kk = k * kk_ref[...]
    kk = kk / jnp.maximum(jnp.sqrt(_head_sum(kk * kk, ones)), 1e-12)
    kh = k * (1.0 + (a - 1.0) * ka_ref[...])
    r_o[...] = r
    w_o[...] = log_decay
    k_o[...] = kh
    v_o[...] = v
    kk_o[...] = kk
    b_o[...] = kk * a
    g_o[...] = g
    bonus_o[...] = _head_sum(r * kh * rk_ref[...], ones) * v


def _prep(ps, w0, a0, wlora, wg, k_k, k_a, r_k, tm):
    t = ps.shape[0]
    const = lambda shape: pl.BlockSpec(shape, lambda i: (0, 0))
    out = jax.ShapeDtypeStruct((t, D_RWKV), F32)
    return pl.pallas_call(
        _prep_kernel,
        grid=(t // tm,),
        in_specs=[pl.BlockSpec((tm, N_SHIFT), lambda i: (i, 0)),
                  const((1, D_RWKV)), const((1, D_RWKV)),
                  const((LANES, 2 * D_RWKV)), const((D_GATE_LORA, D_RWKV)),
                  const((1, D_RWKV)), const((1, D_RWKV)), const((1, D_RWKV))],
        out_specs=[pl.BlockSpec((tm, D_RWKV), lambda i: (i, 0))] * 8,
        out_shape=[out] * 8,
        compiler_params=_params(("arbitrary",)),
        name="rwkv_prep",
    )(ps, w0, a0, wlora, wg, k_k, k_a, r_k)


RWKV_CHUNK = 16
CHUNK_TILE = 128

_NN = (((1,), (0,)), ((), ()))
_NT = (((1,), (1,)), ((), ()))


def _dot1(a, b, dims=_NN):
    return lax.dot_general(a.astype(BF16), b.astype(BF16), dims, preferred_element_type=F32)


def _dot_exact_lhs(lhs_bf16, x):
    hi, mid, lo = _split3(x)
    acc = jnp.dot(lhs_bf16, hi, preferred_element_type=F32)
    acc += jnp.dot(lhs_bf16, mid, preferred_element_type=F32)
    acc += jnp.dot(lhs_bf16, lo, preferred_element_type=F32)
    return acc


def _chunk_kernel(r_ref, lw_ref, k_ref, v_ref, kk_ref, b_ref,
                  w_o, rh_o, u0_o, y0_o, bh_o, kh_o, pl_o):
    n = CHUNK_TILE
    tpos = lax.broadcasted_iota(jnp.int32, (n, n), 0)
    spos = lax.broadcasted_iota(jnp.int32, (n, n), 1)
    same = (tpos // RWKV_CHUNK) == (spos // RWKV_CHUNK)
    incl = same & (spos <= tpos)
    strict = same & (spos < tpos)
    tri_ones = jnp.where(incl, 1.0, 0.0).astype(BF16)
    blk_ones = jnp.where(same, 1.0, 0.0).astype(BF16)
    eye = jnp.where(tpos == spos, 1.0, 0.0)
    lane_head = lax.broadcasted_iota(jnp.int32, (n, LANES), 1) // HEAD

    pairs = range(N_RWKV_PAIRS)
    slabs = [slice(LANES * p, LANES * (p + 1)) for p in pairs]
    lw, r, k, v, kk, b = ([ref[:, sl] for sl in slabs]
                          for ref in (lw_ref, r_ref, k_ref, v_ref, kk_ref, b_ref))
    cum = [_dot_exact_lhs(tri_ones, x) for x in lw]
    tot = [_dot_exact_lhs(blk_ones, x) for x in lw]
    at = [-kk[p] * jnp.exp(cum[p] - lw[p]) for p in pairs]
    rt = [r[p] * jnp.exp(cum[p]) for p in pairs]
    inv = [jnp.exp(-cum[p]) for p in pairs]
    bt = [b[p] * inv[p] for p in pairs]
    kt = [k[p] * inv[p] for p in pairs]

    chains = [(p, h) for p in pairs for h in range(LANES // HEAD)]
    mine = [lane_head == h for h in range(LANES // HEAD)]
    a_h = [jnp.where(mine[h], at[p], 0.0) for p, h in chains]
    r_h = [jnp.where(mine[h], rt[p], 0.0) for p, h in chains]
    v_h = [jnp.where(mine[h], v[p], 0.0) for p, h in chains]
    each = range(len(chains))
    a_ab = [jnp.where(strict, _dot3(a_h[c], bt[chains[c][0]], _NT), 0.0) for c in each]
    a_ak = [jnp.where(strict, _dot3(a_h[c], kt[chains[c][0]], _NT), 0.0) for c in each]
    tinv = [eye + a for a in a_ab]
    x = a_ab
    for _ in range(RWKV_CHUNK.bit_length() - 2):
        x = [_dot3(xc, xc) for xc in x]
        tinv = [tc + _dot3(tc, xc) for tc, xc in zip(tinv, x)]
    akv = [_dot3(a_ak[c], v_h[c]) for c in each]
    m_rb = [jnp.where(incl, _dot1(r_h[c], bt[chains[c][0]], _NT), 0.0) for c in each]
    m_rk = [jnp.where(incl, _dot1(r_h[c], kt[chains[c][0]], _NT), 0.0) for c in each]
    w_h = [_dot3(tinv[c], a_h[c]) for c in each]
    u0_h = [_dot3(tinv[c], akv[c]) for c in each]
    rh_h = [r_h[c] + _dot1(m_rb[c], w_h[c]) for c in each]
    y0_h = [_dot1(m_rb[c], u0_h[c]) + _dot1(m_rk[c], v_h[c]) for c in each]

    for p in pairs:
        c0, c1 = 2 * p, 2 * p + 1
        rem = jnp.exp(tot[p] - cum[p])
        w_o[:, slabs[p]] = w_h[c0] + w_h[c1]
        rh_o[:, slabs[p]] = rh_h[c0] + rh_h[c1]
        u0_o[:, slabs[p]] = u0_h[c0] + u0_h[c1]
        y0_o[:, slabs[p]] = y0_h[c0] + y0_h[c1]
        bh_o[:, slabs[p]] = b[p] * rem
        kh_o[:, slabs[p]] = k[p] * rem
        pl_o[:, slabs[p]] = jnp.exp(tot[p])


def _chunk(r, lw, k, v, kk, b):
    t, d = r.shape
    spec = pl.BlockSpec((CHUNK_TILE, d), lambda i: (i, 0))
    out = jax.ShapeDtypeStruct((t, d), F32)
    return pl.pallas_call(
        _chunk_kernel,
        grid=(t // CHUNK_TILE,),
        in_specs=[spec] * 6,
        out_specs=[spec] * 7,
        out_shape=[out] * 7,
        compiler_params=_params(("arbitrary",)),
        name="rwkv_chunk_prep",
    )(r, lw, k, v, kk, b)


def _rwkv_kernel(w_ref, rh_ref, u0_ref, y0_ref, bh_ref, kh_ref, v_ref, pl_ref,
                 y_ref, state_ref):
    @pl.when(pl.program_id(0) == 0)
    def _():
        state_ref[...] = jnp.zeros_like(state_ref)

    nb, tt, _ = w_ref.shape
    L = RWKV_CHUNK
    row_head = lax.broadcasted_iota(jnp.int32, (LANES, LANES), 0) // HEAD
    col_head = lax.broadcasted_iota(jnp.int32, (LANES, LANES), 1) // HEAD
    same_head = row_head == col_head
    eye = (lax.broadcasted_iota(jnp.int32, (LANES, LANES), 0)
           == lax.broadcasted_iota(jnp.int32, (LANES, LANES), 1))
    all_ones = jnp.ones((LANES, LANES), BF16)
    pad = jnp.zeros((LANES - 2 * L, LANES), F32)

    def column_of(p_row):
        p_hi = pltpu.bitcast(pltpu.bitcast(p_row, jnp.uint32) & jnp.uint32(0xFFFF0000), F32)
        return (jnp.dot(jnp.where(eye, p_hi, 0.0).astype(BF16), all_ones,
                        preferred_element_type=F32)
                + jnp.dot(jnp.where(eye, p_row - p_hi, 0.0).astype(BF16), all_ones,
                          preferred_element_type=F32))

    def chunk(ci, carry):
        t0 = pl.multiple_of(ci * L, L)
        streams = [(bi, p) for bi in range(nb) for p in range(N_RWKV_PAIRS)]

        def rows(ref, n=L):
            return [ref[bi, pl.ds(t0, n), LANES * p:LANES * (p + 1)] for bi, p in streams]

        each = range(len(streams))
        s = [state_ref[bi * N_RWKV_PAIRS + p] for bi, p in streams]
        w, rh, u0, y0, bh, kh, v = (rows(ref) for ref in
                                   (w_ref, rh_ref, u0_ref, y0_ref, bh_ref, kh_ref, v_ref))
        p_col = [column_of(pr[0:1, :]) for pr in rows(pl_ref, SUBLANES)]
        bk_t = [jnp.concatenate([bh[c], kh[c], pad], axis=0).T for c in each]
        g = [_dot3(jnp.concatenate([w[c], rh[c]], axis=0), s[c]) for c in each]
        u = [g[c][:L] + u0[c] for c in each]
        upd = [_dot3(bk_t[c], jnp.concatenate([u[c], v[c], pad], axis=0)) for c in each]
        for c, (bi, p) in enumerate(streams):
            y_ref[bi, pl.ds(t0, L), LANES * p:LANES * (p + 1)] = g[c][L:] + y0[c]
            state_ref[bi * N_RWKV_PAIRS + p] = (p_col[c] * s[c]
                                                + jnp.where(same_head, upd[c], 0.0))
        return carry

    lax.fori_loop(0, tt // L, chunk, 0)


def _rwkv(w, rh, u0, y0, bh, kh, v, p_l, tt):
    nb, s, d = w.shape
    spec = pl.BlockSpec((nb, tt, d), lambda i: (0, i, 0))
    return pl.pallas_call(
        _rwkv_kernel,
        grid=(s // tt,),
        in_specs=[spec] * 8,
        out_specs=spec,
        out_shape=jax.ShapeDtypeStruct((nb, s, d), F32),
        scratch_shapes=[pltpu.VMEM((nb * N_RWKV_PAIRS, LANES, LANES), F32)],
        compiler_params=_params(("arbitrary",)),
        name="rwkv_scan",
    )(w, rh, u0, y0, bh, kh, v, p_l)


def _gelu_tanh(x):
    return 0.5 * x * (1.0 + jnp.tanh(0.7978845608028654 * (x + 0.044715 * (x * x * x))))


def _sgu_kernel(pb_ref, g_ref, b_ref, ws_ref, bias_ref, o_ref):
    z = _gelu_tanh(pb_ref[...])
    u, v = z[:, :D_SGU], z[:, D_SGU:]
    mu = jnp.mean(v, axis=-1, keepdims=True)
    vc = v - mu
    var = jnp.mean(vc * vc, axis=-1, keepdims=True)
    vn = (vc * lax.rsqrt(var + LN_EPS) * g_ref[...] + b_ref[...]).astype(BF16)
    tpos = lax.broadcasted_iota(jnp.int32, (SGU_BLOCK, SGU_BLOCK), 0) // CHUNK
    spos = lax.broadcasted_iota(jnp.int32, (SGU_BLOCK, SGU_BLOCK), 1) // CHUNK
    causal = spos <= tpos
    first_head = lax.broadcasted_iota(jnp.int32, (SGU_BLOCK, LANES), 1) < HEAD
    for p in range(D_SGU // LANES):
        sl = slice(LANES * p, LANES * (p + 1))
        ws0 = jnp.where(causal, ws_ref[2 * p], 0.0).astype(BF16)
        ws1 = jnp.where(causal, ws_ref[2 * p + 1], 0.0).astype(BF16)
        for n in range(pb_ref.shape[0] // SGU_BLOCK):
            rows = slice(SGU_BLOCK * n, SGU_BLOCK * (n + 1))
            vb = vn[rows, sl]
            sp = jnp.where(first_head,
                           jnp.dot(ws0, vb, preferred_element_type=F32),
                           jnp.dot(ws1, vb, preferred_element_type=F32))
            o_ref[rows, sl] = (u[rows, sl] * (sp + bias_ref[:, sl])).astype(o_ref.dtype)


def _sgu(pb, ln_g, ln_b, w_spatial, bias_tile, tm):
    t = pb.shape[0]
    nh = w_spatial.shape[0]
    return pl.pallas_call(
        _sgu_kernel,
        grid=(t // tm,),
        in_specs=[pl.BlockSpec((tm, 2 * D_SGU), lambda i: (i, 0)),
                  pl.BlockSpec((1, D_SGU), lambda i: (0, 0)),
                  pl.BlockSpec((1, D_SGU), lambda i: (0, 0)),
                  pl.BlockSpec((nh, SGU_BLOCK, SGU_BLOCK), lambda i: (0, 0, 0)),
                  pl.BlockSpec((SGU_BLOCK, D_SGU), lambda i: (0, 0))],
        out_specs=pl.BlockSpec((tm, D_SGU), lambda i: (i, 0)),
        out_shape=jax.ShapeDtypeStruct((t, D_SGU), BF16),
        compiler_params=_params(("arbitrary",)),
        name="sgu_mix",
    )(pb, ln_g, ln_b, w_spatial, bias_tile)


def _mid_kernel(y_ref, g_ref, bonus_ref, lng_ref, lnb_ref, yb_ref, x_ref,
                g1_ref, sc2_ref, sh2_ref, g2_ref, n2_ref,
                woa_ref, wob_ref, wrt_ref, w13_ref, w2_ref,
                h2_ref, logit_ref, base_ref):
    ones = _head_ones()
    y = y_ref[...]
    mu = _head_sum(y, ones) * (1.0 / HEAD)
    yc = y - mu
    var = _head_sum(yc * yc, ones) * (1.0 / HEAD)
    ya = yc * lax.rsqrt(var + LN_X_EPS) * lng_ref[...] + lnb_ref[...]
    ya = ((ya + bonus_ref[...]) * g_ref[...]).astype(BF16)
    yo = (jnp.dot(ya, woa_ref[...], preferred_element_type=F32)
          + jnp.dot(yb_ref[...], wob_ref[...], preferred_element_type=F32))
    x1 = x_ref[...] + g1_ref[...] * yo
    h2 = x1 * lax.rsqrt(jnp.mean(x1 * x1, axis=-1, keepdims=True) + RMS_EPS) * n2_ref[...]
    h2 = h2 * (1.0 + sc2_ref[...]) + sh2_ref[...]
    h2b = h2.astype(BF16)
    h2_ref[...] = h2b
    logit_ref[...] = _dot3(wrt_ref[...], h2, dims=(((1,), (1,)), ((), ())))
    hid = jnp.dot(h2b, w13_ref[...], preferred_element_type=F32)
    act = (_silu(hid[:, :D_EXPERT]) * hid[:, D_EXPERT:]).astype(BF16)
    ms = jnp.dot(act, w2_ref[...], preferred_element_type=F32)
    base_ref[...] = x1 + g2_ref[...] * ms


def _mid(y, g, bonus, lnx_g, lnx_b, yb, x, g1, sc2, sh2, g2, n2g,
         wo_a, wo_b, wr_t, w13s, w2s, tm, tiles_per_batch):
    t, d = x.shape
    tok = lambda w: pl.BlockSpec((tm, w), lambda i: (i, 0))
    const = lambda shape: pl.BlockSpec(shape, lambda i: (0,) * len(shape))
    mod = pl.BlockSpec((None, 1, d), lambda i: (i // tiles_per_batch, 0, 0))
    return pl.pallas_call(
        _mid_kernel,
        grid=(t // tm,),
        in_specs=[tok(D_RWKV), tok(D_RWKV), tok(D_RWKV),
                  const((1, D_RWKV)), const((1, D_RWKV)),
                  tok(D_SGU), tok(d), mod, mod, mod, mod, const((1, d)),
                  const(wo_a.shape), const(wo_b.shape), const(wr_t.shape),
                  const(w13s.shape), const(w2s.shape)],
        out_specs=[tok(d), pl.BlockSpec((LANES, tm), lambda i: (0, i)), tok(d)],
        out_shape=[jax.ShapeDtypeStruct((t, d), BF16),
                   jax.ShapeDtypeStruct((LANES, t), F32),
                   jax.ShapeDtypeStruct((t, d), F32)],
        compiler_params=_params(("arbitrary",)),
        name="outproj_norm_router_shared",
    )(y, g, bonus, lnx_g, lnx_b, yb, x, g1, sc2, sh2, g2, n2g,
      wo_a, wo_b, wr_t, w13s, w2s)


MOE_SUB = 128
MOE_CAP = 32
MOE_GROUP = 8


def _route_kernel(logit_ref, bias_ref, gate_ref, pos_ref, fexp_ref, fsub_ref, fdense_ref):
    tt = logit_ref.shape[1]
    scores = _sigmoid(logit_ref[0:N_EXPERTS, :])
    biased = scores + bias_ref[...]
    neg_inf = -jnp.inf

    group_blocks, group_scores = [], []
    for gi in range(N_GROUPS):
        blk = biased[GROUP_SIZE * gi:GROUP_SIZE * (gi + 1), :]
        m1 = jnp.max(blk, axis=0, keepdims=True)
        is_max = blk == m1
        n_max = jnp.sum(jnp.where(is_max, 1.0, 0.0), axis=0, keepdims=True)
        m2 = jnp.max(jnp.where(is_max, neg_inf, blk), axis=0, keepdims=True)
        group_blocks.append(blk)
        group_scores.append(m1 + jnp.where(n_max >= 2.0, m1, m2))

    masked = []
    for gi in range(N_GROUPS):
        rank = jnp.zeros((1, tt), F32)
        for gj in range(N_GROUPS):
            if gj == gi:
                continue
            ahead = (group_scores[gj] >= group_scores[gi]) if gj < gi else \
                    (group_scores[gj] > group_scores[gi])
            rank += jnp.where(ahead, 1.0, 0.0)
        masked.append(jnp.where(rank < float(TOPK_GROUPS), group_blocks[gi], neg_inf))
    sub = lax.broadcasted_iota(jnp.int32, (GROUP_SIZE, tt), 0)
    ranks = [jnp.zeros((GROUP_SIZE, tt), F32) for _ in range(N_GROUPS)]
    for ej in range(N_EXPERTS):
        gj, oj = divmod(ej, GROUP_SIZE)
        other = masked[gj][oj:oj + 1, :]
        for gi in range(N_GROUPS):
            blk = masked[gi]
            if gi > gj:
                ahead = jnp.where(other >= blk, 1.0, 0.0)
            elif gi < gj:
                ahead = jnp.where(other > blk, 1.0, 0.0)
            else:
                ahead = jnp.where(sub > oj, jnp.where(other >= blk, 1.0, 0.0),
                                  jnp.where(other > blk, 1.0, 0.0))
            ranks[gi] = ranks[gi] + ahead
    rank = jnp.concatenate(ranks, axis=0)
    chosen = jnp.where(rank < float(TOP_K), 1.0, 0.0)
    sw = chosen * scores
    gate_ref[...] = sw / jnp.sum(sw, axis=0, keepdims=True) * ROUTED_SCALE

    before = (lax.broadcasted_iota(jnp.int32, (MOE_SUB, MOE_SUB), 0)
              < lax.broadcasted_iota(jnp.int32, (MOE_SUB, MOE_SUB), 1))
    prefix_ones = jnp.where(before, 1.0, 0.0).astype(BF16)
    all_ones = jnp.ones((MOE_SUB, MOE_SUB), BF16)
    lane = lax.broadcasted_iota(jnp.int32, (N_EXPERTS, LANES), 1)
    positions, counts = [], jnp.zeros((N_EXPERTS, LANES), F32)
    for s in range(tt // MOE_SUB):
        blk = chosen[:, MOE_SUB * s:MOE_SUB * (s + 1)].astype(BF16)
        prefix = jnp.dot(blk, prefix_ones, preferred_element_type=F32)
        positions.append(jnp.where(blk > 0, prefix, -1.0))
        total = jnp.dot(blk, all_ones, preferred_element_type=F32)
        counts = counts + jnp.where(lane == s, total, 0.0)
    pos_ref[...] = jnp.concatenate(positions, axis=1)
    per_expert = jnp.max(counts, axis=1, keepdims=True)
    fexp_ref[...] = jnp.broadcast_to(jnp.where(per_expert > float(MOE_CAP), 1, 0),
                                     fexp_ref.shape).astype(jnp.int32)
    per_group = jnp.concatenate(
        [jnp.max(counts[MOE_GROUP * g:MOE_GROUP * (g + 1), :], axis=0, keepdims=True)
         for g in range(N_EXPERTS // MOE_GROUP)], axis=0)
    fsub_ref[...] = jnp.where(per_group > float(MOE_CAP), 1, 0).astype(jnp.int32)
    group_worst = jnp.max(per_group, axis=1, keepdims=True)
    fdense_ref[...] = jnp.broadcast_to(jnp.where(group_worst > float(2 * MOE_CAP), 1, 0),
                                       fdense_ref.shape).astype(jnp.int32)


def _route(logits_t, bias_tile, tt):
    t = logits_t.shape[1]
    n_tiles = t // tt
    n_groups = N_EXPERTS // MOE_GROUP
    tok = pl.BlockSpec((N_EXPERTS, tt), lambda i: (0, i))
    return pl.pallas_call(
        _route_kernel,
        grid=(n_tiles,),
        in_specs=[pl.BlockSpec((LANES, tt), lambda i: (0, i)),
                  pl.BlockSpec((N_EXPERTS, tt), lambda i: (0, 0))],
        out_specs=[tok, tok,
                   pl.BlockSpec((N_EXPERTS, LANES), lambda i: (i, 0)),
                   pl.BlockSpec((n_groups, LANES), lambda i: (i, 0)),
                   pl.BlockSpec((n_groups, LANES), lambda i: (i, 0))],
        out_shape=[jax.ShapeDtypeStruct((N_EXPERTS, t), F32),
                   jax.ShapeDtypeStruct((N_EXPERTS, t), F32),
                   jax.ShapeDtypeStruct((n_tiles * N_EXPERTS, LANES), jnp.int32),
                   jax.ShapeDtypeStruct((n_tiles * n_groups, LANES), jnp.int32),
                   jax.ShapeDtypeStruct((n_tiles * n_groups, LANES), jnp.int32)],
        compiler_params=_params(("arbitrary",)),
        name="route_topk",
    )(logits_t, bias_tile)


def _moe_kernel(fexp_ref, fsub_ref, fdense_ref, h2_ref, gate_ref, pos_ref, w13_ref, w2_ref,
                o_ref, xy_ref, gb_ref, pt_ref):
    tile, eg = pl.program_id(0), pl.program_id(1)
    n_groups = pl.num_programs(1)
    tm = h2_ref.shape[0]
    n_sub = tm // MOE_SUB
    ones = jnp.ones((MOE_SUB, D_EXPERT), BF16)

    @pl.when(eg == 0)
    def _():
        o_ref[...] = jnp.zeros_like(o_ref)

    @pl.when((tile == 0) & (eg == 0))
    def _():
        xy_ref[...] = jnp.zeros_like(xy_ref)
        gb_ref[...] = jnp.zeros_like(gb_ref)

    def lane_broadcast(m):
        hi = m.astype(BF16)
        lo = (m - hi.astype(F32)).astype(BF16)
        return (jnp.dot(hi, ones, preferred_element_type=F32)
                + jnp.dot(lo, ones, preferred_element_type=F32))

    def hidden(x, gate_rows, j):
        hid = jnp.dot(x, w13_ref[j], preferred_element_type=F32)
        return (_silu(hid[:, :D_EXPERT]) * hid[:, D_EXPERT:] * gate_rows).astype(BF16)

    def pipelined(n, first, second):
        pending = first(0)
        for i in range(1, n):
            nxt = first(i)
            second(i - 1, pending)
            pending = nxt
        second(n - 1, pending)

    cap = MOE_CAP
    slab_rows = n_sub * cap
    go_dense = fdense_ref[tile * n_groups + eg] != 0

    def needs_second(s):
        return fsub_ref[(tile * n_groups + eg) * n_sub + s] != 0

    def slab_slice(j, q, s=None):
        start = (2 * j + q) * slab_rows
        if s is None:
            return slice(start, start + slab_rows)
        return slice(start + s * cap, start + (s + 1) * cap)

    def gather(s, q):
        cols = slice(MOE_SUB * s, MOE_SUB * (s + 1))
        slot = lax.broadcasted_iota(jnp.int32, (cap, MOE_SUB), 0).astype(F32) + float(q * cap)
        pos, gate = pos_ref[:, cols], gate_ref[:, cols]
        hot = [jnp.where(slot == pos[j:j + 1, :], 1.0, 0.0) for j in range(MOE_GROUP)]
        p = jnp.concatenate(hot, axis=0)
        pg = jnp.concatenate([hot[j] * gate[j:j + 1, :] for j in range(MOE_GROUP)], axis=0)
        xg = jnp.dot(p.astype(BF16), h2_ref[cols, :],
                     preferred_element_type=F32).astype(BF16)
        gb = lane_broadcast(pg)
        for j in range(MOE_GROUP):
            xy_ref[slab_slice(j, q, s), :] = xg[j * cap:(j + 1) * cap]
            gb_ref[slab_slice(j, q, s), :] = gb[j * cap:(j + 1) * cap]
        pt_ref[s, :, q * MOE_GROUP * cap:(q + 1) * MOE_GROUP * cap] = p.T.astype(BF16)

    def scatter(s, q):
        y = jnp.concatenate([xy_ref[slab_slice(j, q, s), :] for j in range(MOE_GROUP)], axis=0)
        o_ref[MOE_SUB * s:MOE_SUB * (s + 1), :] += jnp.dot(
            pt_ref[s, :, q * MOE_GROUP * cap:(q + 1) * MOE_GROUP * cap], y,
            preferred_element_type=F32)

    def project(j, q, act):
        xy_ref[slab_slice(j, q), :] = jnp.dot(
            act, w2_ref[j], preferred_element_type=F32).astype(BF16)

    @pl.when(jnp.logical_not(go_dense))
    def _():
        for s in range(n_sub):
            gather(s, 0)
        pipelined(MOE_GROUP,
                  lambda j: hidden(xy_ref[slab_slice(j, 0), :], gb_ref[slab_slice(j, 0), :], j),
                  lambda j, act: project(j, 0, act))
        for s in range(n_sub):
            scatter(s, 0)
        for s in range(n_sub):
            pl.when(needs_second(s))(functools.partial(gather, s, 1))
        for j in range(MOE_GROUP):
            @pl.when(fexp_ref[tile * N_EXPERTS + eg * MOE_GROUP + j] != 0)
            def _(j=j):
                project(j, 1, hidden(xy_ref[slab_slice(j, 1), :],
                                     gb_ref[slab_slice(j, 1), :], j))
        for s in range(n_sub):
            pl.when(needs_second(s))(functools.partial(scatter, s, 1))

    @pl.when(go_dense)
    def _():
        eye = (lax.broadcasted_iota(jnp.int32, (MOE_SUB, MOE_SUB), 0)
               == lax.broadcasted_iota(jnp.int32, (MOE_SUB, MOE_SUB), 1))
        for j in range(MOE_GROUP):
            for s in range(n_sub):
                cols = slice(MOE_SUB * s, MOE_SUB * (s + 1))
                gb_ref[cols, :] = lane_broadcast(
                    jnp.where(eye, gate_ref[j:j + 1, cols], 0.0))
            act = hidden(h2_ref[...], gb_ref[0:tm, :], j)
            o_ref[...] += jnp.dot(act, w2_ref[j], preferred_element_type=F32)


def _moe(fexp, fsub, fdense, h2, gates_t, pos_t, w13e, w2e, tm):
    t, d = h2.shape
    ne = w13e.shape[0]
    assert ne % MOE_GROUP == 0 and tm % MOE_SUB == 0
    n_sub = tm // MOE_SUB
    rows = MOE_GROUP * 2 * n_sub * MOE_CAP
    assert rows >= tm, "gate scratch is reused by the dense path"
    tok = pl.BlockSpec((tm, d), lambda i, e, *_: (i, 0))
    per_expert = pl.BlockSpec((MOE_GROUP, tm), lambda i, e, *_: (e, i))
    return pl.pallas_call(
        _moe_kernel,
        grid_spec=pltpu.PrefetchScalarGridSpec(
            num_scalar_prefetch=3,
            grid=(t // tm, ne // MOE_GROUP),
            in_specs=[tok, per_expert, per_expert,
                      pl.BlockSpec((MOE_GROUP, d, 2 * D_EXPERT), lambda i, e, *_: (e, 0, 0)),
                      pl.BlockSpec((MOE_GROUP, D_EXPERT, d), lambda i, e, *_: (e, 0, 0))],
            out_specs=tok,
            scratch_shapes=[pltpu.VMEM((rows, d), BF16),
                            pltpu.VMEM((rows, D_EXPERT), F32),
                            pltpu.VMEM((n_sub, MOE_SUB, 2 * MOE_GROUP * MOE_CAP), BF16)]),
        out_shape=jax.ShapeDtypeStruct((t, d), F32),
        compiler_params=_params(("arbitrary", "arbitrary")),
        name="moe_experts",
    )(fexp, fsub, fdense, h2, gates_t, pos_t, w13e, w2e)


def _final_kernel(base_ref, routed_ref, g2_ref, nf_ref, o_ref):
    xo = base_ref[...] + g2_ref[...] * routed_ref[...]
    o_ref[...] = (xo * lax.rsqrt(jnp.mean(xo * xo, axis=-1, keepdims=True) + RMS_EPS)
                  * nf_ref[...])


def _final(base, routed, g2, nf, tm, tiles_per_batch):
    t, d = base.shape
    tok = pl.BlockSpec((tm, d), lambda i: (i, 0))
    return pl.pallas_call(
        _final_kernel,
        grid=(t // tm,),
        in_specs=[tok, tok,
                  pl.BlockSpec((None, 1, d), lambda i: (i // tiles_per_batch, 0, 0)),
                  pl.BlockSpec((1, d), lambda i: (0, 0))],
        out_specs=tok,
        out_shape=jax.ShapeDtypeStruct((t, d), F32),
        compiler_params=_params(("arbitrary",)),
        name="residual_final_norm",
    )(base, routed, g2, nf)


def _tile(n, want):
    t = min(n, want)
    assert n % t == 0, (n, t)
    return t


def kernel(x, c, w_ada, b_ada, norm1_g, w_in, mu_shift, w0, w_decay_up, a0, w_a_up, w_g_up,
           k_k, k_a, r_k, lnx_g, lnx_b, sgu_ln_g, sgu_ln_b, w_spatial, b_spatial, w_out,
           norm2_g, w_router, e_bias, w1_e, w3_e, w2_e, w1_s, w3_s, w2_s, norm_f_g):
    b, s, d = x.shape
    t = b * s
    assert w_ada.shape[0] == 1, "single-layer block"
    assert d == 2 * D_RWKV and s % SGU_BLOCK == 0
    row = lambda vec: vec.reshape(1, -1)

    c_pad = jnp.pad(c, ((0, 8 - b), (0, 0)))
    mod = _mod(c_pad, w_ada[0], b_ada)[:b]
    sh1, sc1, g1, sh2, sc2, g2 = [m[:, None, :] for m in jnp.split(mod, 6, axis=-1)]

    ps, pb = _inproj(x, sc1, sh1, norm1_g, w_in[0].astype(BF16), mu_shift, _tile(s, 512))

    wlora = jnp.zeros((LANES, 2 * D_RWKV), F32)
    wlora = wlora.at[:D_DECAY_LORA, :D_RWKV].set(w_decay_up[0])
    wlora = wlora.at[D_DECAY_LORA:, D_RWKV:].set(w_a_up[0])
    r, wdec, kh, v, kk, bb, g, bonus = _prep(
        ps.reshape(t, N_SHIFT), w0, a0, wlora, w_g_up[0], k_k, k_a, row(r_k), _tile(t, 512))

    seq = lambda z: z.reshape(b, s, D_RWKV)
    w_c, rh_c, u0_c, y0_c, bh_c, kh_c, pl_c = _chunk(r, wdec, kh, v, kk, bb)
    y = _rwkv(seq(w_c), seq(rh_c), seq(u0_c), seq(y0_c), seq(bh_c), seq(kh_c), seq(v),
              seq(pl_c), _tile(s, 256))

    bias_tile = jnp.repeat(b_spatial[0].T, HEAD, axis=1)
    yb = _sgu(pb.reshape(t, 2 * D_SGU), sgu_ln_g, sgu_ln_b, w_spatial[0], bias_tile,
              _tile(t, 512))

    tm_mid = _tile(s, 512)
    wr_t = jnp.pad(w_router[0].T, ((0, LANES - N_EXPERTS), (0, 0)))
    w13s = jnp.concatenate([w1_s[0], w3_s[0]], axis=1).astype(BF16)
    wo = w_out[0].astype(BF16)
    h2, logits_t, base = _mid(
        y.reshape(t, D_RWKV), g, bonus, lnx_g, lnx_b, yb, x.reshape(t, d),
        g1, sc2, sh2, g2, norm2_g, wo[:D_RWKV], wo[D_RWKV:], wr_t, w13s,
        w2_s[0].astype(BF16), tm_mid, s // tm_mid)

    tm_moe = _tile(t, 1024)
    gates_t, pos_t, fexp, fsub, fdense = _route(
        logits_t, jnp.broadcast_to(e_bias[0][:, None], (N_EXPERTS, tm_moe)), tm_moe)

    w13e = jnp.concatenate([w1_e[0], w3_e[0]], axis=2).astype(BF16)
    routed = _moe(fexp[:, 0], fsub[:, :tm_moe // MOE_SUB].reshape(-1), fdense[:, 0],
                  h2, gates_t, pos_t, w13e, w2_e[0].astype(BF16), tm_moe)
    tm_fin = _tile(s, 512)
    out = _final(base, routed, g2, row(norm_f_g), tm_fin, s // tm_fin)
    return out.reshape(b, s, d)
```

```python
import functools

import jax
import jax.numpy as jnp
from jax import lax
from jax.experimental import pallas as pl
from jax.experimental.pallas import tpu as pltpu

F32 = jnp.float32
BF16 = jnp.bfloat16

HEAD = 64
LANES = 128
SUBLANES = 8
D_RWKV = 512
D_SGU = 512
N_RWKV_PAIRS = D_RWKV // LANES
D_DECAY_LORA = 64
D_AAA_LORA = 64
D_GATE_LORA = 128
N_SHIFT = 3 * D_RWKV + D_DECAY_LORA + D_AAA_LORA + D_GATE_LORA
SGU_BLOCK = 128
CHUNK = 64
N_EXPERTS = 64
N_GROUPS = 8
GROUP_SIZE = N_EXPERTS // N_GROUPS
TOPK_GROUPS = 4
TOP_K = 8
D_EXPERT = 256
ROUTED_SCALE = 2.5
RMS_EPS = 1e-6
LN_EPS = 1e-5
LN_X_EPS = 64e-5
VMEM_LIMIT_BYTES = 56 * 1024 * 1024


def _params(semantics):
    return pltpu.CompilerParams(dimension_semantics=semantics,
                                vmem_limit_bytes=VMEM_LIMIT_BYTES)


def _split3(x):
    hi = x.astype(BF16)
    r1 = x - hi.astype(F32)
    mid = r1.astype(BF16)
    lo = (r1 - mid.astype(F32)).astype(BF16)
    return hi, mid, lo


def _dot_exact_rhs(x, rhs_bf16):
    hi, mid, lo = _split3(x)
    acc = jnp.dot(hi, rhs_bf16, preferred_element_type=F32)
    acc += jnp.dot(mid, rhs_bf16, preferred_element_type=F32)
    acc += jnp.dot(lo, rhs_bf16, preferred_element_type=F32)
    return acc


def _dot3(a, b, dims=(((1,), (0,)), ((), ()))):
    a_hi = a.astype(BF16)
    a_lo = (a - a_hi.astype(F32)).astype(BF16)
    b_hi = b.astype(BF16)
    b_lo = (b - b_hi.astype(F32)).astype(BF16)
    dg = functools.partial(lax.dot_general, dimension_numbers=dims,
                           preferred_element_type=F32)
    return dg(a_hi, b_hi) + dg(a_hi, b_lo) + dg(a_lo, b_hi)


def _head_ones():
    r = lax.broadcasted_iota(jnp.int32, (LANES, LANES), 0) // HEAD
    c = lax.broadcasted_iota(jnp.int32, (LANES, LANES), 1) // HEAD
    return jnp.where(r == c, 1.0, 0.0).astype(BF16)


def _head_sum(z, ones):
    parts = [_dot_exact_rhs(z[:, LANES * p:LANES * (p + 1)], ones)
             for p in range(z.shape[1] // LANES)]
    return jnp.concatenate(parts, axis=-1)


def _sigmoid(x):
    return 1.0 / (1.0 + jnp.exp(-x))


def _silu(x):
    return x * _sigmoid(x)


def _mod_kernel(c_ref, w_ref, b_ref, o_ref):
    s = _silu(c_ref[...])
    o_ref[...] = _dot3(s, w_ref[...]) + b_ref[...]


def _mod(c_pad, w_ada, b_ada):
    rows, d = c_pad.shape
    n = w_ada.shape[1]
    tn = 1024
    return pl.pallas_call(
        _mod_kernel,
        grid=(n // tn,),
        in_specs=[pl.BlockSpec((rows, d), lambda j: (0, 0)),
                  pl.BlockSpec((d, tn), lambda j: (0, j)),
                  pl.BlockSpec((1, tn), lambda j: (0, j))],
        out_specs=pl.BlockSpec((rows, tn), lambda j: (0, j)),
        out_shape=jax.ShapeDtypeStruct((rows, n), F32),
        compiler_params=_params(("arbitrary",)),
        name="adaln_mod",
    )(c_pad, w_ada, b_ada)


def _inproj_kernel(x_ref, sc_ref, sh_ref, g_ref, w_ref, mu_ref, ps_ref, pb_ref, carry_ref):
    @pl.when(pl.program_id(1) == 0)
    def _():
        carry_ref[...] = jnp.zeros_like(carry_ref)

    x = x_ref[...]
    tm = x.shape[0]
    h = x * lax.rsqrt(jnp.mean(x * x, axis=-1, keepdims=True) + RMS_EPS) * g_ref[...]
    h = h * (1.0 + sc_ref[...]) + sh_ref[...]
    proj = jnp.dot(h.astype(BF16), w_ref[...], preferred_element_type=F32)
    ps = proj[:, :N_SHIFT]
    prev = pltpu.roll(ps, 1, 0)
    first = lax.broadcasted_iota(jnp.int32, ps.shape, 0) == 0
    prev = jnp.where(first, carry_ref[0:1, :], prev)
    carry_ref[0:1, :] = ps[tm - 1:tm, :]
    ps_ref[...] = ps + (prev - ps) * mu_ref[...]
    pb_ref[...] = proj[:, N_SHIFT:]


def _inproj(x, sc1, sh1, g, w_in_bf16, mu, tm):
    b, s, d = x.shape
    d_in = w_in_bf16.shape[1]
    row = lambda bi, i: (bi, 0, 0)
    return pl.pallas_call(
        _inproj_kernel,
        grid=(b, s // tm),
        in_specs=[pl.BlockSpec((None, tm, d), lambda bi, i: (bi, i, 0)),
                  pl.BlockSpec((None, 1, d), row),
                  pl.BlockSpec((None, 1, d), row),
                  pl.BlockSpec((1, d), lambda bi, i: (0, 0)),
                  pl.BlockSpec((d, d_in), lambda bi, i: (0, 0)),
                  pl.BlockSpec((1, N_SHIFT), lambda bi, i: (0, 0))],
        out_specs=[pl.BlockSpec((None, tm, N_SHIFT), lambda bi, i: (bi, i, 0)),
                   pl.BlockSpec((None, tm, d_in - N_SHIFT), lambda bi, i: (bi, i, 0))],
        out_shape=[jax.ShapeDtypeStruct((b, s, N_SHIFT), F32),
                   jax.ShapeDtypeStruct((b, s, d_in - N_SHIFT), F32)],
        scratch_shapes=[pltpu.VMEM((8, N_SHIFT), F32)],
        compiler_params=_params(("arbitrary", "arbitrary")),
        name="inproj_shift",
    )(x, sc1, sh1, g, w_in_bf16, mu)


def _prep_kernel(ps_ref, w0_ref, a0_ref, wlora_ref, wg_ref, kk_ref, ka_ref, rk_ref,
                 r_o, w_o, k_o, v_o, kk_o, b_o, g_o, bonus_o):
    ps = ps_ref[...]
    c1, c2, c3 = D_RWKV, 2 * D_RWKV, 3 * D_RWKV
    r, k, v = ps[:, :c1], ps[:, c1:c2], ps[:, c2:c3]
    xwa = ps[:, c3:c3 + LANES]
    xg = ps[:, c3 + LANES:]
    is_w = lax.broadcasted_iota(jnp.int32, xwa.shape, 1) < D_DECAY_LORA
    lora = _dot3(jnp.where(is_w, jnp.tanh(xwa), xwa), wlora_ref[...])
    wlog = -jax.nn.softplus(-(w0_ref[...] + lora[:, :c1])) - 0.5
    log_decay = -jnp.exp(wlog)
    a = _sigmoid(a0_ref[...] + lora[:, c1:])
    g = _dot3(_sigmoid(xg), wg_ref[...])
    ones = _head_ones()
    kk = k * kk_ref[...]
    kk = kk / jnp.maximum(jnp.sqrt(_head_sum(kk * kk, ones)), 1e-12)
    kh = k * (1.0 + (a - 1.0) * ka_ref[...])
    r_o[...] = r
    w_o[...] = log_decay
    k_o[...] = kh
    v_o[...] = v
    kk_o[...] = kk
    b_o[...] = kk * a
    g_o[...] = g
    bonus_o[...] = _head_sum(r * kh * rk_ref[...], ones) * v


def _prep(ps, w0, a0, wlora, wg, k_k, k_a, r_k, tm):
    t = ps.shape[0]
    const = lambda shape: pl.BlockSpec(shape, lambda i: (0, 0))
    out = jax.ShapeDtypeStruct((t, D_RWKV), F32)
    return pl.pallas_call(
        _prep_kernel,
        grid=(t // tm,),
        in_specs=[pl.BlockSpec((tm, N_SHIFT), lambda i: (i, 0)),
                  const((1, D_RWKV)), const((1, D_RWKV)),
                  const((LANES, 2 * D_RWKV)), const((D_GATE_LORA, D_RWKV)),
                  const((1, D_RWKV)), const((1, D_RWKV)), const((1, D_RWKV))],
        out_specs=[pl.BlockSpec((tm, D_RWKV), lambda i: (i, 0))] * 8,
        out_shape=[out] * 8,
        compiler_params=_params(("arbitrary",)),
        name="rwkv_prep",
    )(ps, w0, a0, wlora, wg, k_k, k_a, r_k)


RWKV_CHUNK = 16
CHUNK_TILE = 128

_NN = (((1,), (0,)), ((), ()))
_NT = (((1,), (1,)), ((), ()))


def _dot1(a, b, dims=_NN):
    return lax.dot_general(a.astype(BF16), b.astype(BF16), dims, preferred_element_type=F32)


def _dot_exact_lhs(lhs_bf16, x):
    hi, mid, lo = _split3(x)
    acc = jnp.dot(lhs_bf16, hi, preferred_element_type=F32)
    acc += jnp.dot(lhs_bf16, mid, preferred_element_type=F32)
    acc += jnp.dot(lhs_bf16, lo, preferred_element_type=F32)
    return acc


def _chunk_kernel(r_ref, lw_ref, k_ref, v_ref, kk_ref, b_ref,
                  w_o, rh_o, u0_o, y0_o, bh_o, kh_o, pl_o):
    n = CHUNK_TILE
    tpos = lax.broadcasted_iota(jnp.int32, (n, n), 0)
    spos = lax.broadcasted_iota(jnp.int32, (n, n), 1)
    same = (tpos // RWKV_CHUNK) == (spos // RWKV_CHUNK)
    incl = same & (spos <= tpos)
    strict = same & (spos < tpos)
    tri_ones = jnp.where(incl, 1.0, 0.0).astype(BF16)
    blk_ones = jnp.where(same, 1.0, 0.0).astype(BF16)
    eye = jnp.where(tpos == spos, 1.0, 0.0)
    lane_head = lax.broadcasted_iota(jnp.int32, (n, LANES), 1) // HEAD

    pairs = range(N_RWKV_PAIRS)
    slabs = [slice(LANES * p, LANES * (p + 1)) for p in pairs]
    lw, r, k, v, kk, b = ([ref[:, sl] for sl in slabs]
                          for ref in (lw_ref, r_ref, k_ref, v_ref, kk_ref, b_ref))
    cum = [_dot_exact_lhs(tri_ones, x) for x in lw]
    tot = [_dot_exact_lhs(blk_ones, x) for x in lw]
    at = [-kk[p] * jnp.exp(cum[p] - lw[p]) for p in pairs]
    rt = [r[p] * jnp.exp(cum[p]) for p in pairs]
    inv = [jnp.exp(-cum[p]) for p in pairs]
    bt = [b[p] * inv[p] for p in pairs]
    kt = [k[p] * inv[p] for p in pairs]

    chains = [(p, h) for p in pairs for h in range(LANES // HEAD)]
    mine = [lane_head == h for h in range(LANES // HEAD)]
    a_h = [jnp.where(mine[h], at[p], 0.0) for p, h in chains]
    r_h = [jnp.where(mine[h], rt[p], 0.0) for p, h in chains]
    v_h = [jnp.where(mine[h], v[p], 0.0) for p, h in chains]
    each = range(len(chains))
    a_ab = [jnp.where(strict, _dot3(a_h[c], bt[chains[c][0]], _NT), 0.0) for c in each]
    a_ak = [jnp.where(strict, _dot3(a_h[c], kt[chains[c][0]], _NT), 0.0) for c in each]
    tinv = [eye + a for a in a_ab]
    x = a_ab
    for _ in range(RWKV_CHUNK.bit_length() - 2):
        x = [_dot1(xc, xc) for xc in x]
        tinv = [tc + _dot1(tc, xc) for tc, xc in zip(tinv, x)]
    akv = [_dot3(a_ak[c], v_h[c]) for c in each]
    m_rb = [jnp.where(incl, _dot1(r_h[c], bt[chains[c][0]], _NT), 0.0) for c in each]
    m_rk = [jnp.where(incl, _dot1(r_h[c], kt[chains[c][0]], _NT), 0.0) for c in each]
    w_h = [_dot3(tinv[c], a_h[c]) for c in each]
    u0_h = [_dot3(tinv[c], akv[c]) for c in each]
    rh_h = [r_h[c] + _dot1(m_rb[c], w_h[c]) for c in each]
    y0_h = [_dot1(m_rb[c], u0_h[c]) + _dot1(m_rk[c], v_h[c]) for c in each]

    for p in pairs:
        c0, c1 = 2 * p, 2 * p + 1
        rem = jnp.exp(tot[p] - cum[p])
        w_o[:, slabs[p]] = w_h[c0] + w_h[c1]
        rh_o[:, slabs[p]] = rh_h[c0] + rh_h[c1]
        u0_o[:, slabs[p]] = u0_h[c0] + u0_h[c1]
        y0_o[:, slabs[p]] = y0_h[c0] + y0_h[c1]
        bh_o[:, slabs[p]] = b[p] * rem
        kh_o[:, slabs[p]] = k[p] * rem
        pl_o[:, slabs[p]] = jnp.exp(tot[p])


def _chunk(r, lw, k, v, kk, b):
    t, d = r.shape
    spec = pl.BlockSpec((CHUNK_TILE, d), lambda i: (i, 0))
    out = jax.ShapeDtypeStruct((t, d), F32)
    return pl.pallas_call(
        _chunk_kernel,
        grid=(t // CHUNK_TILE,),
        in_specs=[spec] * 6,
        out_specs=[spec] * 7,
        out_shape=[out] * 7,
        compiler_params=_params(("arbitrary",)),
        name="rwkv_chunk_prep",
    )(r, lw, k, v, kk, b)


def _rwkv_kernel(w_ref, rh_ref, u0_ref, y0_ref, bh_ref, kh_ref, v_ref, pl_ref,
                 y_ref, state_ref):
    @pl.when(pl.program_id(0) == 0)
    def _():
        state_ref[...] = jnp.zeros_like(state_ref)

    nb, tt, _ = w_ref.shape
    L = RWKV_CHUNK
    assert 6 * L + SUBLANES <= LANES
    row_head = lax.broadcasted_iota(jnp.int32, (LANES, LANES), 0) // HEAD
    col_head = lax.broadcasted_iota(jnp.int32, (LANES, LANES), 1) // HEAD
    same_head = row_head == col_head
    pad = jnp.zeros((LANES - 6 * L - SUBLANES, LANES), F32)
    zeros_tail = jnp.zeros((LANES - 6 * L, LANES), BF16)

    def split(x):
        hi = x.astype(BF16)
        return hi, (x - hi.astype(F32)).astype(BF16)

    def chunk(ci, carry):
        t0 = pl.multiple_of(ci * L, L)
        streams = [(bi, p) for bi in range(nb) for p in range(N_RWKV_PAIRS)]

        def rows(ref, n=L):
            return [ref[bi, pl.ds(t0, n), LANES * p:LANES * (p + 1)] for bi, p in streams]

        each = range(len(streams))
        s = [state_ref[bi * N_RWKV_PAIRS + p] for bi, p in streams]
        w, rh, u0, y0, bh, kh, v = (rows(ref) for ref in
                                   (w_ref, rh_ref, u0_ref, y0_ref, bh_ref, kh_ref, v_ref))
        p_rows = rows(pl_ref, SUBLANES)
        left, p_col = [], []
        for c in each:
            (b_hi, b_lo), (k_hi, k_lo) = split(bh[c]), split(kh[c])
            block = jnp.concatenate([x.astype(F32) for x in (b_hi, b_lo, b_hi, k_hi, k_lo, k_hi)]
                                    + [p_rows[c], pad], axis=0).T
            left.append(block.astype(BF16))
            p_col.append(jnp.broadcast_to(block[:, 6 * L:6 * L + 1], (LANES, LANES)))
        g = []
        for c in each:
            (l_hi, l_lo), (s_hi, s_lo) = split(jnp.concatenate([w[c], rh[c]], axis=0)), split(s[c])
            g.append(jnp.dot(jnp.concatenate([l_hi, l_lo], axis=1),
                             jnp.concatenate([s_hi, s_hi], axis=0), preferred_element_type=F32)
                     + jnp.dot(l_hi, s_lo, preferred_element_type=F32))
        u = [g[c][:L] + u0[c] for c in each]
        upd = []
        for c in each:
            (u_hi, u_lo), (v_hi, v_lo) = split(u[c]), split(v[c])
            right = jnp.concatenate([u_hi, u_hi, u_lo, v_hi, v_hi, v_lo, zeros_tail], axis=0)
            upd.append(jnp.dot(left[c], right, preferred_element_type=F32))
        for c, (bi, p) in enumerate(streams):
            y_ref[bi, pl.ds(t0, L), LANES * p:LANES * (p + 1)] = g[c][L:] + y0[c]
            state_ref[bi * N_RWKV_PAIRS + p] = (p_col[c] * s[c]
                                                + jnp.where(same_head, upd[c], 0.0))
        return carry

    lax.fori_loop(0, tt // L, chunk, 0)


def _rwkv(w, rh, u0, y0, bh, kh, v, p_l, tt):
    nb, s, d = w.shape
    spec = pl.BlockSpec((nb, tt, d), lambda i: (0, i, 0))
    return pl.pallas_call(
        _rwkv_kernel,
        grid=(s // tt,),
        in_specs=[spec] * 8,
        out_specs=spec,
        out_shape=jax.ShapeDtypeStruct((nb, s, d), F32),
        scratch_shapes=[pltpu.VMEM((nb * N_RWKV_PAIRS, LANES, LANES), F32)],
        compiler_params=_params(("arbitrary",)),
        name="rwkv_scan",
    )(w, rh, u0, y0, bh, kh, v, p_l)


def _gelu_tanh(x):
    return 0.5 * x * (1.0 + jnp.tanh(0.7978845608028654 * (x + 0.044715 * (x * x * x))))


def _sgu_kernel(pb_ref, g_ref, b_ref, ws_ref, bias_ref, o_ref):
    z = _gelu_tanh(pb_ref[...])
    u, v = z[:, :D_SGU], z[:, D_SGU:]
    mu = jnp.mean(v, axis=-1, keepdims=True)
    vc = v - mu
    var = jnp.mean(vc * vc, axis=-1, keepdims=True)
    vn = (vc * lax.rsqrt(var + LN_EPS) * g_ref[...] + b_ref[...]).astype(BF16)
    tpos = lax.broadcasted_iota(jnp.int32, (SGU_BLOCK, SGU_BLOCK), 0) // CHUNK
    spos = lax.broadcasted_iota(jnp.int32, (SGU_BLOCK, SGU_BLOCK), 1) // CHUNK
    causal = spos <= tpos
    first_head = lax.broadcasted_iota(jnp.int32, (SGU_BLOCK, LANES), 1) < HEAD
    for p in range(D_SGU // LANES):
        sl = slice(LANES * p, LANES * (p + 1))
        ws0 = jnp.where(causal, ws_ref[2 * p], 0.0).astype(BF16)
        ws1 = jnp.where(causal, ws_ref[2 * p + 1], 0.0).astype(BF16)
        for n in range(pb_ref.shape[0] // SGU_BLOCK):
            rows = slice(SGU_BLOCK * n, SGU_BLOCK * (n + 1))
            vb = vn[rows, sl]
            sp = jnp.where(first_head,
                           jnp.dot(ws0, vb, preferred_element_type=F32),
                           jnp.dot(ws1, vb, preferred_element_type=F32))
            o_ref[rows, sl] = (u[rows, sl] * (sp + bias_ref[:, sl])).astype(o_ref.dtype)


def _sgu(pb, ln_g, ln_b, w_spatial, bias_tile, tm):
    t = pb.shape[0]
    nh = w_spatial.shape[0]
    return pl.pallas_call(
        _sgu_kernel,
        grid=(t // tm,),
        in_specs=[pl.BlockSpec((tm, 2 * D_SGU), lambda i: (i, 0)),
                  pl.BlockSpec((1, D_SGU), lambda i: (0, 0)),
                  pl.BlockSpec((1, D_SGU), lambda i: (0, 0)),
                  pl.BlockSpec((nh, SGU_BLOCK, SGU_BLOCK), lambda i: (0, 0, 0)),
                  pl.BlockSpec((SGU_BLOCK, D_SGU), lambda i: (0, 0))],
        out_specs=pl.BlockSpec((tm, D_SGU), lambda i: (i, 0)),
        out_shape=jax.ShapeDtypeStruct((t, D_SGU), BF16),
        compiler_params=_params(("arbitrary",)),
        name="sgu_mix",
    )(pb, ln_g, ln_b, w_spatial, bias_tile)


def _mid_kernel(y_ref, g_ref, bonus_ref, lng_ref, lnb_ref, yb_ref, x_ref,
                g1_ref, sc2_ref, sh2_ref, g2_ref, n2_ref,
                woa_ref, wob_ref, wrt_ref, w13_ref, w2_ref,
                h2_ref, logit_ref, base_ref):
    ones = _head_ones()
    y = y_ref[...]
    mu = _head_sum(y, ones) * (1.0 / HEAD)
    yc = y - mu
    var = _head_sum(yc * yc, ones) * (1.0 / HEAD)
    ya = yc * lax.rsqrt(var + LN_X_EPS) * lng_ref[...] + lnb_ref[...]
    ya = ((ya + bonus_ref[...]) * g_ref[...]).astype(BF16)
    yo = (jnp.dot(ya, woa_ref[...], preferred_element_type=F32)
          + jnp.dot(yb_ref[...], wob_ref[...], preferred_element_type=F32))
    x1 = x_ref[...] + g1_ref[...] * yo
    h2 = x1 * lax.rsqrt(jnp.mean(x1 * x1, axis=-1, keepdims=True) + RMS_EPS) * n2_ref[...]
    h2 = h2 * (1.0 + sc2_ref[...]) + sh2_ref[...]
    h2b = h2.astype(BF16)
    h2_ref[...] = h2b
    logit_ref[...] = _dot3(wrt_ref[...], h2, dims=(((1,), (1,)), ((), ())))
    hid = jnp.dot(h2b, w13_ref[...], preferred_element_type=F32)
    act = (_silu(hid[:, :D_EXPERT]) * hid[:, D_EXPERT:]).astype(BF16)
    ms = jnp.dot(act, w2_ref[...], preferred_element_type=F32)
    base_ref[...] = x1 + g2_ref[...] * ms


def _mid(y, g, bonus, lnx_g, lnx_b, yb, x, g1, sc2, sh2, g2, n2g,
         wo_a, wo_b, wr_t, w13s, w2s, tm, tiles_per_batch):
    t, d = x.shape
    tok = lambda w: pl.BlockSpec((tm, w), lambda i: (i, 0))
    const = lambda shape: pl.BlockSpec(shape, lambda i: (0,) * len(shape))
    mod = pl.BlockSpec((None, 1, d), lambda i: (i // tiles_per_batch, 0, 0))
    return pl.pallas_call(
        _mid_kernel,
        grid=(t // tm,),
        in_specs=[tok(D_RWKV), tok(D_RWKV), tok(D_RWKV),
                  const((1, D_RWKV)), const((1, D_RWKV)),
                  tok(D_SGU), tok(d), mod, mod, mod, mod, const((1, d)),
                  const(wo_a.shape), const(wo_b.shape), const(wr_t.shape),
                  const(w13s.shape), const(w2s.shape)],
        out_specs=[tok(d), pl.BlockSpec((LANES, tm), lambda i: (0, i)), tok(d)],
        out_shape=[jax.ShapeDtypeStruct((t, d), BF16),
                   jax.ShapeDtypeStruct((LANES, t), F32),
                   jax.ShapeDtypeStruct((t, d), F32)],
        compiler_params=_params(("arbitrary",)),
        name="outproj_norm_router_shared",
    )(y, g, bonus, lnx_g, lnx_b, yb, x, g1, sc2, sh2, g2, n2g,
      wo_a, wo_b, wr_t, w13s, w2s)


MOE_SUB = 128
MOE_CAP = 32
MOE_GROUP = 8


def _route_kernel(logit_ref, bias_ref, gate_ref, pos_ref, fexp_ref, fsub_ref, fdense_ref):
    tt = logit_ref.shape[1]
    scores = _sigmoid(logit_ref[0:N_EXPERTS, :])
    biased = scores + bias_ref[...]
    neg_inf = -jnp.inf

    group_blocks, group_scores = [], []
    for gi in range(N_GROUPS):
        blk = biased[GROUP_SIZE * gi:GROUP_SIZE * (gi + 1), :]
        m1 = jnp.max(blk, axis=0, keepdims=True)
        is_max = blk == m1
        n_max = jnp.sum(jnp.where(is_max, 1.0, 0.0), axis=0, keepdims=True)
        m2 = jnp.max(jnp.where(is_max, neg_inf, blk), axis=0, keepdims=True)
        group_blocks.append(blk)
        group_scores.append(m1 + jnp.where(n_max >= 2.0, m1, m2))

    masked = []
    for gi in range(N_GROUPS):
        rank = jnp.zeros((1, tt), F32)
        for gj in range(N_GROUPS):
            if gj == gi:
                continue
            ahead = (group_scores[gj] >= group_scores[gi]) if gj < gi else \
                    (group_scores[gj] > group_scores[gi])
            rank += jnp.where(ahead, 1.0, 0.0)
        masked.append(jnp.where(rank < float(TOPK_GROUPS), group_blocks[gi], neg_inf))
    sub = lax.broadcasted_iota(jnp.int32, (GROUP_SIZE, tt), 0)
    ranks = [jnp.zeros((GROUP_SIZE, tt), F32) for _ in range(N_GROUPS)]
    for ej in range(N_EXPERTS):
        gj, oj = divmod(ej, GROUP_SIZE)
        other = masked[gj][oj:oj + 1, :]
        for gi in range(N_GROUPS):
            blk = masked[gi]
            if gi > gj:
                ahead = jnp.where(other >= blk, 1.0, 0.0)
            elif gi < gj:
                ahead = jnp.where(other > blk, 1.0, 0.0)
            else:
                ahead = jnp.where(sub > oj, jnp.where(other >= blk, 1.0, 0.0),
                                  jnp.where(other > blk, 1.0, 0.0))
            ranks[gi] = ranks[gi] + ahead
    rank = jnp.concatenate(ranks, axis=0)
    chosen = jnp.where(rank < float(TOP_K), 1.0, 0.0)
    sw = chosen * scores
    gate_ref[...] = sw / jnp.sum(sw, axis=0, keepdims=True) * ROUTED_SCALE

    before = (lax.broadcasted_iota(jnp.int32, (MOE_SUB, MOE_SUB), 0)
              < lax.broadcasted_iota(jnp.int32, (MOE_SUB, MOE_SUB), 1))
    prefix_ones = jnp.where(before, 1.0, 0.0).astype(BF16)
    all_ones = jnp.ones((MOE_SUB, MOE_SUB), BF16)
    lane = lax.broadcasted_iota(jnp.int32, (N_EXPERTS, LANES), 1)
    positions, counts = [], jnp.zeros((N_EXPERTS, LANES), F32)
    for s in range(tt // MOE_SUB):
        blk = chosen[:, MOE_SUB * s:MOE_SUB * (s + 1)].astype(BF16)
        prefix = jnp.dot(blk, prefix_ones, preferred_element_type=F32)
        positions.append(jnp.where(blk > 0, prefix, -1.0))
        total = jnp.dot(blk, all_ones, preferred_element_type=F32)
        counts = counts + jnp.where(lane == s, total, 0.0)
    pos_ref[...] = jnp.concatenate(positions, axis=1)
    per_expert = jnp.max(counts, axis=1, keepdims=True)
    fexp_ref[...] = jnp.broadcast_to(jnp.where(per_expert > float(MOE_CAP), 1, 0),
                                     fexp_ref.shape).astype(jnp.int32)
    per_group = jnp.concatenate(
        [jnp.max(counts[MOE_GROUP * g:MOE_GROUP * (g + 1), :], axis=0, keepdims=True)
         for g in range(N_EXPERTS // MOE_GROUP)], axis=0)
    fsub_ref[...] = jnp.where(per_group > float(MOE_CAP), 1, 0).astype(jnp.int32)
    group_worst = jnp.max(per_group, axis=1, keepdims=True)
    fdense_ref[...] = jnp.broadcast_to(jnp.where(group_worst > float(2 * MOE_CAP), 1, 0),
                                       fdense_ref.shape).astype(jnp.int32)


def _route(logits_t, bias_tile, tt):
    t = logits_t.shape[1]
    n_tiles = t // tt
    n_groups = N_EXPERTS // MOE_GROUP
    tok = pl.BlockSpec((N_EXPERTS, tt), lambda i: (0, i))
    return pl.pallas_call(
        _route_kernel,
        grid=(n_tiles,),
        in_specs=[pl.BlockSpec((LANES, tt), lambda i: (0, i)),
                  pl.BlockSpec((N_EXPERTS, tt), lambda i: (0, 0))],
        out_specs=[tok, tok,
                   pl.BlockSpec((N_EXPERTS, LANES), lambda i: (i, 0)),
                   pl.BlockSpec((n_groups, LANES), lambda i: (i, 0)),
                   pl.BlockSpec((n_groups, LANES), lambda i: (i, 0))],
        out_shape=[jax.ShapeDtypeStruct((N_EXPERTS, t), F32),
                   jax.ShapeDtypeStruct((N_EXPERTS, t), F32),
                   jax.ShapeDtypeStruct((n_tiles * N_EXPERTS, LANES), jnp.int32),
                   jax.ShapeDtypeStruct((n_tiles * n_groups, LANES), jnp.int32),
                   jax.ShapeDtypeStruct((n_tiles * n_groups, LANES), jnp.int32)],
        compiler_params=_params(("arbitrary",)),
        name="route_topk",
    )(logits_t, bias_tile)


def _moe_kernel(fexp_ref, fsub_ref, fdense_ref, h2_ref, gate_ref, pos_ref, w13_ref, w2_ref,
                o_ref, xy_ref, gb_ref, pt_ref):
    tile, eg = pl.program_id(0), pl.program_id(1)
    n_groups = pl.num_programs(1)
    tm = h2_ref.shape[0]
    n_sub = tm // MOE_SUB
    ones = jnp.ones((MOE_SUB, D_EXPERT), BF16)

    @pl.when(eg == 0)
    def _():
        o_ref[...] = jnp.zeros_like(o_ref)

    @pl.when((tile == 0) & (eg == 0))
    def _():
        xy_ref[...] = jnp.zeros_like(xy_ref)
        gb_ref[...] = jnp.zeros_like(gb_ref)

    def lane_broadcast(m):
        hi = m.astype(BF16)
        lo = (m - hi.astype(F32)).astype(BF16)
        return (jnp.dot(hi, ones, preferred_element_type=F32)
                + jnp.dot(lo, ones, preferred_element_type=F32))

    def hidden(x, gate_rows, j):
        hid = jnp.dot(x, w13_ref[j], preferred_element_type=F32)
        return (_silu(hid[:, :D_EXPERT]) * hid[:, D_EXPERT:] * gate_rows).astype(BF16)

    def pipelined(n, first, second):
        pending = first(0)
        for i in range(1, n):
            nxt = first(i)
            second(i - 1, pending)
            pending = nxt
        second(n - 1, pending)

    cap = MOE_CAP
    slab_rows = n_sub * cap
    go_dense = fdense_ref[tile * n_groups + eg] != 0

    def needs_second(s):
        return fsub_ref[(tile * n_groups + eg) * n_sub + s] != 0

    def slab_slice(j, q, s=None):
        start = (2 * j + q) * slab_rows
        if s is None:
            return slice(start, start + slab_rows)
        return slice(start + s * cap, start + (s + 1) * cap)

    def gather(s, q):
        cols = slice(MOE_SUB * s, MOE_SUB * (s + 1))
        slot = lax.broadcasted_iota(jnp.int32, (cap, MOE_SUB), 0).astype(F32) + float(q * cap)
        pos, gate = pos_ref[:, cols], gate_ref[:, cols]
        hot = [jnp.where(slot == pos[j:j + 1, :], 1.0, 0.0) for j in range(MOE_GROUP)]
        p = jnp.concatenate(hot, axis=0)
        pg = jnp.concatenate([hot[j] * gate[j:j + 1, :] for j in range(MOE_GROUP)], axis=0)
        xg = jnp.dot(p.astype(BF16), h2_ref[cols, :],
                     preferred_element_type=F32).astype(BF16)
        gb = lane_broadcast(pg)
        for j in range(MOE_GROUP):
            xy_ref[slab_slice(j, q, s), :] = xg[j * cap:(j + 1) * cap]
            gb_ref[slab_slice(j, q, s), :] = gb[j * cap:(j + 1) * cap]
        pt_ref[s, :, q * MOE_GROUP * cap:(q + 1) * MOE_GROUP * cap] = p.T.astype(BF16)

    def scatter(s, q):
        y = jnp.concatenate([xy_ref[slab_slice(j, q, s), :] for j in range(MOE_GROUP)], axis=0)
        o_ref[MOE_SUB * s:MOE_SUB * (s + 1), :] += jnp.dot(
            pt_ref[s, :, q * MOE_GROUP * cap:(q + 1) * MOE_GROUP * cap], y,
            preferred_element_type=F32)

    def project(j, q, act):
        xy_ref[slab_slice(j, q), :] = jnp.dot(
            act, w2_ref[j], preferred_element_type=F32).astype(BF16)

    @pl.when(jnp.logical_not(go_dense))
    def _():
        for s in range(n_sub):
            gather(s, 0)
        pipelined(MOE_GROUP,
                  lambda j: hidden(xy_ref[slab_slice(j, 0), :], gb_ref[slab_slice(j, 0), :], j),
                  lambda j, act: project(j, 0, act))
        for s in range(n_sub):
            scatter(s, 0)
        for s in range(n_sub):
            pl.when(needs_second(s))(functools.partial(gather, s, 1))
        for j in range(MOE_GROUP):
            @pl.when(fexp_ref[tile * N_EXPERTS + eg * MOE_GROUP + j] != 0)
            def _(j=j):
                project(j, 1, hidden(xy_ref[slab_slice(j, 1), :],
                                     gb_ref[slab_slice(j, 1), :], j))
        for s in range(n_sub):
            pl.when(needs_second(s))(functools.partial(scatter, s, 1))

    @pl.when(go_dense)
    def _():
        eye = (lax.broadcasted_iota(jnp.int32, (MOE_SUB, MOE_SUB), 0)
               == lax.broadcasted_iota(jnp.int32, (MOE_SUB, MOE_SUB), 1))
        for j in range(MOE_GROUP):
            for s in range(n_sub):
                cols = slice(MOE_SUB * s, MOE_SUB * (s + 1))
                gb_ref[cols, :] = lane_broadcast(
                    jnp.where(eye, gate_ref[j:j + 1, cols], 0.0))
            act = hidden(h2_ref[...], gb_ref[0:tm, :], j)
            o_ref[...] += jnp.dot(act, w2_ref[j], preferred_element_type=F32)


def _moe(fexp, fsub, fdense, h2, gates_t, pos_t, w13e, w2e, tm):
    t, d = h2.shape
    ne = w13e.shape[0]
    assert ne % MOE_GROUP == 0 and tm % MOE_SUB == 0
    n_sub = tm // MOE_SUB
    rows = MOE_GROUP * 2 * n_sub * MOE_CAP
    assert rows >= tm, "gate scratch is reused by the dense path"
    tok = pl.BlockSpec((tm, d), lambda i, e, *_: (i, 0))
    per_expert = pl.BlockSpec((MOE_GROUP, tm), lambda i, e, *_: (e, i))
    return pl.pallas_call(
        _moe_kernel,
        grid_spec=pltpu.PrefetchScalarGridSpec(
            num_scalar_prefetch=3,
            grid=(t // tm, ne // MOE_GROUP),
            in_specs=[tok, per_expert, per_expert,
                      pl.BlockSpec((MOE_GROUP, d, 2 * D_EXPERT), lambda i, e, *_: (e, 0, 0)),
                      pl.BlockSpec((MOE_GROUP, D_EXPERT, d), lambda i, e, *_: (e, 0, 0))],
            out_specs=tok,
            scratch_shapes=[pltpu.VMEM((rows, d), BF16),
                            pltpu.VMEM((rows, D_EXPERT), F32),
                            pltpu.VMEM((n_sub, MOE_SUB, 2 * MOE_GROUP * MOE_CAP), BF16)]),
        out_shape=jax.ShapeDtypeStruct((t, d), F32),
        compiler_params=_params(("arbitrary", "arbitrary")),
        name="moe_experts",
    )(fexp, fsub, fdense, h2, gates_t, pos_t, w13e, w2e)


def _final_kernel(base_ref, routed_ref, g2_ref, nf_ref, o_ref):
    xo = base_ref[...] + g2_ref[...] * routed_ref[...]
    o_ref[...] = (xo * lax.rsqrt(jnp.mean(xo * xo, axis=-1, keepdims=True) + RMS_EPS)
                  * nf_ref[...])


def _final(base, routed, g2, nf, tm, tiles_per_batch):
    t, d = base.shape
    tok = pl.BlockSpec((tm, d), lambda i: (i, 0))
    return pl.pallas_call(
        _final_kernel,
        grid=(t // tm,),
        in_specs=[tok, tok,
                  pl.BlockSpec((None, 1, d), lambda i: (i // tiles_per_batch, 0, 0)),
                  pl.BlockSpec((1, d), lambda i: (0, 0))],
        out_specs=tok,
        out_shape=jax.ShapeDtypeStruct((t, d), F32),
        compiler_params=_params(("arbitrary",)),
        name="residual_final_norm",
    )(base, routed, g2, nf)


def _tile(n, want):
    t = min(n, want)
    assert n % t == 0, (n, t)
    return t


def kernel(x, c, w_ada, b_ada, norm1_g, w_in, mu_shift, w0, w_decay_up, a0, w_a_up, w_g_up,
           k_k, k_a, r_k, lnx_g, lnx_b, sgu_ln_g, sgu_ln_b, w_spatial, b_spatial, w_out,
           norm2_g, w_router, e_bias, w1_e, w3_e, w2_e, w1_s, w3_s, w2_s, norm_f_g):
    b, s, d = x.shape
    t = b * s
    assert w_ada.shape[0] == 1, "single-layer block"
    assert d == 2 * D_RWKV and s % SGU_BLOCK == 0
    row = lambda vec: vec.reshape(1, -1)

    c_pad = jnp.pad(c, ((0, 8 - b), (0, 0)))
    mod = _mod(c_pad, w_ada[0], b_ada)[:b]
    sh1, sc1, g1, sh2, sc2, g2 = [m[:, None, :] for m in jnp.split(mod, 6, axis=-1)]

    ps, pb = _inproj(x, sc1, sh1, norm1_g, w_in[0].astype(BF16), mu_shift, _tile(s, 512))

    wlora = jnp.zeros((LANES, 2 * D_RWKV), F32)
    wlora = wlora.at[:D_DECAY_LORA, :D_RWKV].set(w_decay_up[0])
    wlora = wlora.at[D_DECAY_LORA:, D_RWKV:].set(w_a_up[0])
    r, wdec, kh, v, kk, bb, g, bonus = _prep(
        ps.reshape(t, N_SHIFT), w0, a0, wlora, w_g_up[0], k_k, k_a, row(r_k), _tile(t, 512))

    seq = lambda z: z.reshape(b, s, D_RWKV)
    w_c, rh_c, u0_c, y0_c, bh_c, kh_c, pl_c = _chunk(r, wdec, kh, v, kk, bb)
    y = _rwkv(seq(w_c), seq(rh_c), seq(u0_c), seq(y0_c), seq(bh_c), seq(kh_c), seq(v),
              seq(pl_c), _tile(s, 256))

    bias_tile = jnp.repeat(b_spatial[0].T, HEAD, axis=1)
    yb = _sgu(pb.reshape(t, 2 * D_SGU), sgu_ln_g, sgu_ln_b, w_spatial[0], bias_tile,
              _tile(t, 512))

    tm_mid = _tile(s, 512)
    wr_t = jnp.pad(w_router[0].T, ((0, LANES - N_EXPERTS), (0, 0)))
    w13s = jnp.concatenate([w1_s[0], w3_s[0]], axis=1).astype(BF16)
    wo = w_out[0].astype(BF16)
    h2, logits_t, base = _mid(
        y.reshape(t, D_RWKV), g, bonus, lnx_g, lnx_b, yb, x.reshape(t, d),
        g1, sc2, sh2, g2, norm2_g, wo[:D_RWKV], wo[D_RWKV:], wr_t, w13s,
        w2_s[0].astype(BF16), tm_mid, s // tm_mid)

    tm_moe = _tile(t, 1024)
    gates_t, pos_t, fexp, fsub, fdense = _route(
        logits_t, jnp.broadcast_to(e_bias[0][:, None], (N_EXPERTS, tm_moe)), tm_moe)

    w13e = jnp.concatenate([w1_e[0], w3_e[0]], axis=2).astype(BF16)
    routed = _moe(fexp[:, 0], fsub[:, :tm_moe // MOE_SUB].reshape(-1), fdense[:, 0],
                  h2, gates_t, pos_t, w13e, w2_e[0].astype(BF16), tm_moe)
    tm_fin = _tile(s, 512)
    out = _final(base, routed, g2, row(norm_f_g), tm_fin, s // tm_fin)
    return out.reshape(b, s, d)
```

```python
import functools

import jax
import jax.numpy as jnp
from jax import lax
from jax.experimental import pallas as pl
from jax.experimental.pallas import tpu as pltpu

F32 = jnp.float32
BF16 = jnp.bfloat16

HEAD = 64
LANES = 128
SUBLANES = 8
D_RWKV = 512
D_SGU = 512
N_RWKV_PAIRS = D_RWKV // LANES
D_DECAY_LORA = 64
D_AAA_LORA = 64
D_GATE_LORA = 128
N_SHIFT = 3 * D_RWKV + D_DECAY_LORA + D_AAA_LORA + D_GATE_LORA
SGU_BLOCK = 128
CHUNK = 64
N_EXPERTS = 64
N_GROUPS = 8
GROUP_SIZE = N_EXPERTS // N_GROUPS
TOPK_GROUPS = 4
TOP_K = 8
D_EXPERT = 256
ROUTED_SCALE = 2.5
RMS_EPS = 1e-6
LN_EPS = 1e-5
LN_X_EPS = 64e-5
VMEM_LIMIT_BYTES = 56 * 1024 * 1024


def _params(semantics):
    return pltpu.CompilerParams(dimension_semantics=semantics,
                                vmem_limit_bytes=VMEM_LIMIT_BYTES)


def _split3(x):
    hi = x.astype(BF16)
    r1 = x - hi.astype(F32)
    mid = r1.astype(BF16)
    lo = (r1 - mid.astype(F32)).astype(BF16)
    return hi, mid, lo


def _dot_exact_rhs(x, rhs_bf16):
    return jnp.dot(jnp.concatenate(_split3(x), axis=1),
                   jnp.concatenate([rhs_bf16] * 3, axis=0), preferred_element_type=F32)


def _dot3(a, b, dims=(((1,), (0,)), ((), ()))):
    ((ca,), (cb,)), _ = dims
    a_hi = a.astype(BF16)
    a_lo = (a - a_hi.astype(F32)).astype(BF16)
    b_hi = b.astype(BF16)
    b_lo = (b - b_hi.astype(F32)).astype(BF16)
    dg = functools.partial(lax.dot_general, dimension_numbers=dims,
                           preferred_element_type=F32)
    return (dg(jnp.concatenate([a_hi, a_lo], axis=ca), jnp.concatenate([b_hi, b_hi], axis=cb))
            + dg(a_hi, b_lo))


def _head_ones():
    r = lax.broadcasted_iota(jnp.int32, (LANES, LANES), 0) // HEAD
    c = lax.broadcasted_iota(jnp.int32, (LANES, LANES), 1) // HEAD
    return jnp.where(r == c, 1.0, 0.0).astype(BF16)


def _head_sum(z, ones):
    parts = [_dot_exact_rhs(z[:, LANES * p:LANES * (p + 1)], ones)
             for p in range(z.shape[1] // LANES)]
    return jnp.concatenate(parts, axis=-1)


def _sigmoid(x):
    return 1.0 / (1.0 + jnp.exp(-x))


def _silu(x):
    return x * _sigmoid(x)


def _mod_kernel(c_ref, w_ref, b_ref, o_ref):
    s = _silu(c_ref[...])
    o_ref[...] = _dot3(s, w_ref[...]) + b_ref[...]


def _mod(c_pad, w_ada, b_ada):
    rows, d = c_pad.shape
    n = w_ada.shape[1]
    tn = 1024
    return pl.pallas_call(
        _mod_kernel,
        grid=(n // tn,),
        in_specs=[pl.BlockSpec((rows, d), lambda j: (0, 0)),
                  pl.BlockSpec((d, tn), lambda j: (0, j)),
                  pl.BlockSpec((1, tn), lambda j: (0, j))],
        out_specs=pl.BlockSpec((rows, tn), lambda j: (0, j)),
        out_shape=jax.ShapeDtypeStruct((rows, n), F32),
        compiler_params=_params(("arbitrary",)),
        name="adaln_mod",
    )(c_pad, w_ada, b_ada)


def _inproj_kernel(x_ref, sc_ref, sh_ref, g_ref, w_ref, mu_ref, ps_ref, pb_ref, carry_ref):
    @pl.when(pl.program_id(1) == 0)
    def _():
        carry_ref[...] = jnp.zeros_like(carry_ref)

    x = x_ref[...]
    tm = x.shape[0]
    h = x * lax.rsqrt(jnp.mean(x * x, axis=-1, keepdims=True) + RMS_EPS) * g_ref[...]
    h = h * (1.0 + sc_ref[...]) + sh_ref[...]
    proj = jnp.dot(h.astype(BF16), w_ref[...], preferred_element_type=F32)
    ps = proj[:, :N_SHIFT]
    prev = pltpu.roll(ps, 1, 0)
    first = lax.broadcasted_iota(jnp.int32, ps.shape, 0) == 0
    prev = jnp.where(first, carry_ref[0:1, :], prev)
    carry_ref[0:1, :] = ps[tm - 1:tm, :]
    ps_ref[...] = ps + (prev - ps) * mu_ref[...]
    pb_ref[...] = proj[:, N_SHIFT:]


def _inproj(x, sc1, sh1, g, w_in_bf16, mu, tm):
    b, s, d = x.shape
    d_in = w_in_bf16.shape[1]
    row = lambda bi, i: (bi, 0, 0)
    return pl.pallas_call(
        _inproj_kernel,
        grid=(b, s // tm),
        in_specs=[pl.BlockSpec((None, tm, d), lambda bi, i: (bi, i, 0)),
                  pl.BlockSpec((None, 1, d), row),
                  pl.BlockSpec((None, 1, d), row),
                  pl.BlockSpec((1, d), lambda bi, i: (0, 0)),
                  pl.BlockSpec((d, d_in), lambda bi, i: (0, 0)),
                  pl.BlockSpec((1, N_SHIFT), lambda bi, i: (0, 0))],
        out_specs=[pl.BlockSpec((None, tm, N_SHIFT), lambda bi, i: (bi, i, 0)),
                   pl.BlockSpec((None, tm, d_in - N_SHIFT), lambda bi, i: (bi, i, 0))],
        out_shape=[jax.ShapeDtypeStruct((b, s, N_SHIFT), F32),
                   jax.ShapeDtypeStruct((b, s, d_in - N_SHIFT), F32)],
        scratch_shapes=[pltpu.VMEM((8, N_SHIFT), F32)],
        compiler_params=_params(("arbitrary", "arbitrary")),
        name="inproj_shift",
    )(x, sc1, sh1, g, w_in_bf16, mu)


def _prep_kernel(ps_ref, w0_ref, a0_ref, wlora_ref, wg_ref, kk_ref, ka_ref, rk_ref,
                 r_o, w_o, k_o, v_o, kk_o, b_o, g_o, bonus_o):
    ps = ps_ref[...]
    c1, c2, c3 = D_RWKV, 2 * D_RWKV, 3 * D_RWKV
    r, k, v = ps[:, :c1], ps[:, c1:c2], ps[:, c2:c3]
    xwa = ps[:, c3:c3 + LANES]
    xg = ps[:, c3 + LANES:]
    is_w = lax.broadcasted_iota(jnp.int32, xwa.shape, 1) < D_DECAY_LORA
    lora = _dot3(jnp.where(is_w, jnp.tanh(xwa), xwa), wlora_ref[...])
    wlog = -jax.nn.softplus(-(w0_ref[...] + lora[:, :c1])) - 0.5
    log_decay = -jnp.exp(wlog)
    a = _sigmoid(a0_ref[...] + lora[:, c1:])
    g = _dot3(_sigmoid(xg), wg_ref[...])
    ones = _head_ones()
    kk = k * kk_ref[...]
    kk = kk / jnp.maximum(jnp.sqrt(_head_sum(kk * kk, ones)), 1e-12)
    kh = k * (1.0 + (a - 1.0) * ka_ref[...])
    r_o[...] = r
    w_o[...] = log_decay
    k_o[...] = kh
    v_o[...] = v
    kk_o[...] = kk
    b_o[...] = kk * a
    g_o[...] = g
    bonus_o[...] = _head_sum(r * kh * rk_ref[...], ones) * v


def _prep(ps, w0, a0, wlora, wg, k_k, k_a, r_k, tm):
    t = ps.shape[0]
    const = lambda shape: pl.BlockSpec(shape, lambda i: (0, 0))
    out = jax.ShapeDtypeStruct((t, D_RWKV), F32)
    return pl.pallas_call(
        _prep_kernel,
        grid=(t // tm,),
        in_specs=[pl.BlockSpec((tm, N_SHIFT), lambda i: (i, 0)),
                  const((1, D_RWKV)), const((1, D_RWKV)),
                  const((LANES, 2 * D_RWKV)), const((D_GATE_LORA, D_RWKV)),
                  const((1, D_RWKV)), const((1, D_RWKV)), const((1, D_RWKV))],
        out_specs=[pl.BlockSpec((tm, D_RWKV), lambda i: (i, 0))] * 8,
        out_shape=[out] * 8,
        compiler_params=_params(("arbitrary",)),
        name="rwkv_prep",
    )(ps, w0, a0, wlora, wg, k_k, k_a, r_k)


RWKV_CHUNK = 16
CHUNK_TILE = 128

_NN = (((1,), (0,)), ((), ()))
_NT = (((1,), (1,)), ((), ()))


def _dot1(a, b, dims=_NN):
    return lax.dot_general(a.astype(BF16), b.astype(BF16), dims, preferred_element_type=F32)


def _dot_exact_lhs(lhs_bf16, x):
    return jnp.dot(jnp.concatenate([lhs_bf16] * 3, axis=1),
                   jnp.concatenate(_split3(x), axis=0), preferred_element_type=F32)


def _chunk_kernel(r_ref, lw_ref, k_ref, v_ref, kk_ref, b_ref,
                  w_o, rh_o, u0_o, y0_o, bh_o, kh_o, pl_o):
    n = CHUNK_TILE
    tpos = lax.broadcasted_iota(jnp.int32, (n, n), 0)
    spos = lax.broadcasted_iota(jnp.int32, (n, n), 1)
    same = (tpos // RWKV_CHUNK) == (spos // RWKV_CHUNK)
    incl = same & (spos <= tpos)
    strict = same & (spos < tpos)
    tri_ones = jnp.where(incl, 1.0, 0.0).astype(BF16)
    blk_ones = jnp.where(same, 1.0, 0.0).astype(BF16)
    eye = jnp.where(tpos == spos, 1.0, 0.0)
    lane_head = lax.broadcasted_iota(jnp.int32, (n, LANES), 1) // HEAD

    pairs = range(N_RWKV_PAIRS)
    slabs = [slice(LANES * p, LANES * (p + 1)) for p in pairs]
    lw, r, k, v, kk, b = ([ref[:, sl] for sl in slabs]
                          for ref in (lw_ref, r_ref, k_ref, v_ref, kk_ref, b_ref))
    cum = [_dot_exact_lhs(tri_ones, x) for x in lw]
    tot = [_dot_exact_lhs(blk_ones, x) for x in lw]
    at = [-kk[p] * jnp.exp(cum[p] - lw[p]) for p in pairs]
    rt = [r[p] * jnp.exp(cum[p]) for p in pairs]
    inv = [jnp.exp(-cum[p]) for p in pairs]
    bt = [b[p] * inv[p] for p in pairs]
    kt = [k[p] * inv[p] for p in pairs]

    chains = [(p, h) for p in pairs for h in range(LANES // HEAD)]
    mine = [lane_head == h for h in range(LANES // HEAD)]
    a_h = [jnp.where(mine[h], at[p], 0.0) for p, h in chains]
    r_h = [jnp.where(mine[h], rt[p], 0.0) for p, h in chains]
    v_h = [jnp.where(mine[h], v[p], 0.0) for p, h in chains]
    each = range(len(chains))
    a_ab = [jnp.where(strict, _dot3(a_h[c], bt[chains[c][0]], _NT), 0.0) for c in each]
    a_ak = [jnp.where(strict, _dot3(a_h[c], kt[chains[c][0]], _NT), 0.0) for c in each]
    tinv = [eye + a for a in a_ab]
    x = a_ab
    for _ in range(RWKV_CHUNK.bit_length() - 2):
        x = [_dot1(xc, xc) for xc in x]
        tinv = [tc + _dot1(tc, xc) for tc, xc in zip(tinv, x)]
    akv = [_dot3(a_ak[c], v_h[c]) for c in each]
    m_rb = [jnp.where(incl, _dot1(r_h[c], bt[chains[c][0]], _NT), 0.0) for c in each]
    m_rk = [jnp.where(incl, _dot1(r_h[c], kt[chains[c][0]], _NT), 0.0) for c in each]
    w_h = [_dot3(tinv[c], a_h[c]) for c in each]
    u0_h = [_dot3(tinv[c], akv[c]) for c in each]
    rh_h = [r_h[c] + _dot1(m_rb[c], w_h[c]) for c in each]
    y0_h = [_dot1(m_rb[c], u0_h[c]) + _dot1(m_rk[c], v_h[c]) for c in each]

    for p in pairs:
        c0, c1 = 2 * p, 2 * p + 1
        rem = jnp.exp(tot[p] - cum[p])
        w_o[:, slabs[p]] = w_h[c0] + w_h[c1]
        rh_o[:, slabs[p]] = rh_h[c0] + rh_h[c1]
        u0_o[:, slabs[p]] = u0_h[c0] + u0_h[c1]
        y0_o[:, slabs[p]] = y0_h[c0] + y0_h[c1]
        bh_o[:, slabs[p]] = b[p] * rem
        kh_o[:, slabs[p]] = k[p] * rem
        pl_o[:, slabs[p]] = jnp.exp(tot[p])


def _chunk(r, lw, k, v, kk, b):
    t, d = r.shape
    spec = pl.BlockSpec((CHUNK_TILE, d), lambda i: (i, 0))
    out = jax.ShapeDtypeStruct((t, d), F32)
    return pl.pallas_call(
        _chunk_kernel,
        grid=(t // CHUNK_TILE,),
        in_specs=[spec] * 6,
        out_specs=[spec] * 7,
        out_shape=[out] * 7,
        compiler_params=_params(("arbitrary",)),
        name="rwkv_chunk_prep",
    )(r, lw, k, v, kk, b)


def _rwkv_kernel(w_ref, rh_ref, u0_ref, y0_ref, bh_ref, kh_ref, v_ref, pl_ref,
                 y_ref, state_ref):
    @pl.when(pl.program_id(0) == 0)
    def _():
        state_ref[...] = jnp.zeros_like(state_ref)

    nb, tt, _ = w_ref.shape
    L = RWKV_CHUNK
    assert 6 * L + SUBLANES <= LANES
    row_head = lax.broadcasted_iota(jnp.int32, (LANES, LANES), 0) // HEAD
    col_head = lax.broadcasted_iota(jnp.int32, (LANES, LANES), 1) // HEAD
    same_head = row_head == col_head
    pad = jnp.zeros((LANES - 6 * L - SUBLANES, LANES), F32)
    zeros_tail = jnp.zeros((LANES - 6 * L, LANES), BF16)

    def split(x):
        hi = x.astype(BF16)
        return hi, (x - hi.astype(F32)).astype(BF16)

    def chunk(ci, carry):
        t0 = pl.multiple_of(ci * L, L)
        streams = [(bi, p) for bi in range(nb) for p in range(N_RWKV_PAIRS)]

        def rows(ref, n=L):
            return [ref[bi, pl.ds(t0, n), LANES * p:LANES * (p + 1)] for bi, p in streams]

        each = range(len(streams))
        s = [state_ref[bi * N_RWKV_PAIRS + p] for bi, p in streams]
        w, rh, u0, y0, bh, kh, v = (rows(ref) for ref in
                                   (w_ref, rh_ref, u0_ref, y0_ref, bh_ref, kh_ref, v_ref))
        p_rows = rows(pl_ref, SUBLANES)
        left, p_col = [], []
        for c in each:
            (b_hi, b_lo), (k_hi, k_lo) = split(bh[c]), split(kh[c])
            block = jnp.concatenate([x.astype(F32) for x in (b_hi, b_lo, b_hi, k_hi, k_lo, k_hi)]
                                    + [p_rows[c], pad], axis=0).T
            left.append(block.astype(BF16))
            p_col.append(jnp.broadcast_to(block[:, 6 * L:6 * L + 1], (LANES, LANES)))
        g = []
        for c in each:
            (l_hi, l_lo), (s_hi, s_lo) = split(jnp.concatenate([w[c], rh[c]], axis=0)), split(s[c])
            g.append(jnp.dot(jnp.concatenate([l_hi, l_lo], axis=1),
                             jnp.concatenate([s_hi, s_hi], axis=0), preferred_element_type=F32)
                     + jnp.dot(l_hi, s_lo, preferred_element_type=F32))
        u = [g[c][:L] + u0[c] for c in each]
        upd = []
        for c in each:
            (u_hi, u_lo), (v_hi, v_lo) = split(u[c]), split(v[c])
            right = jnp.concatenate([u_hi, u_hi, u_lo, v_hi, v_hi, v_lo, zeros_tail], axis=0)
            upd.append(jnp.dot(left[c], right, preferred_element_type=F32))
        for c, (bi, p) in enumerate(streams):
            y_ref[bi, pl.ds(t0, L), LANES * p:LANES * (p + 1)] = g[c][L:] + y0[c]
            state_ref[bi * N_RWKV_PAIRS + p] = (p_col[c] * s[c]
                                                + jnp.where(same_head, upd[c], 0.0))
        return carry

    lax.fori_loop(0, tt // L, chunk, 0)


def _rwkv(w, rh, u0, y0, bh, kh, v, p_l, tt):
    nb, s, d = w.shape
    spec = pl.BlockSpec((nb, tt, d), lambda i: (0, i, 0))
    return pl.pallas_call(
        _rwkv_kernel,
        grid=(s // tt,),
        in_specs=[spec] * 8,
        out_specs=spec,
        out_shape=jax.ShapeDtypeStruct((nb, s, d), F32),
        scratch_shapes=[pltpu.VMEM((nb * N_RWKV_PAIRS, LANES, LANES), F32)],
        compiler_params=_params(("arbitrary",)),
        name="rwkv_scan",
    )(w, rh, u0, y0, bh, kh, v, p_l)


def _gelu_tanh(x):
    return 0.5 * x * (1.0 + jnp.tanh(0.7978845608028654 * (x + 0.044715 * (x * x * x))))


def _sgu_kernel(pb_ref, g_ref, b_ref, ws_ref, bias_ref, o_ref):
    z = _gelu_tanh(pb_ref[...])
    u, v = z[:, :D_SGU], z[:, D_SGU:]
    mu = jnp.mean(v, axis=-1, keepdims=True)
    vc = v - mu
    var = jnp.mean(vc * vc, axis=-1, keepdims=True)
    vn = (vc * lax.rsqrt(var + LN_EPS) * g_ref[...] + b_ref[...]).astype(BF16)
    tpos = lax.broadcasted_iota(jnp.int32, (SGU_BLOCK, SGU_BLOCK), 0) // CHUNK
    spos = lax.broadcasted_iota(jnp.int32, (SGU_BLOCK, SGU_BLOCK), 1) // CHUNK
    causal = spos <= tpos
    first_head = lax.broadcasted_iota(jnp.int32, (SGU_BLOCK, LANES), 1) < HEAD
    for p in range(D_SGU // LANES):
        sl = slice(LANES * p, LANES * (p + 1))
        ws0 = jnp.where(causal, ws_ref[2 * p], 0.0).astype(BF16)
        ws1 = jnp.where(causal, ws_ref[2 * p + 1], 0.0).astype(BF16)
        for n in range(pb_ref.shape[0] // SGU_BLOCK):
            rows = slice(SGU_BLOCK * n, SGU_BLOCK * (n + 1))
            vb = vn[rows, sl]
            sp = jnp.where(first_head,
                           jnp.dot(ws0, vb, preferred_element_type=F32),
                           jnp.dot(ws1, vb, preferred_element_type=F32))
            o_ref[rows, sl] = (u[rows, sl] * (sp + bias_ref[:, sl])).astype(o_ref.dtype)


def _sgu(pb, ln_g, ln_b, w_spatial, bias_tile, tm):
    t = pb.shape[0]
    nh = w_spatial.shape[0]
    return pl.pallas_call(
        _sgu_kernel,
        grid=(t // tm,),
        in_specs=[pl.BlockSpec((tm, 2 * D_SGU), lambda i: (i, 0)),
                  pl.BlockSpec((1, D_SGU), lambda i: (0, 0)),
                  pl.BlockSpec((1, D_SGU), lambda i: (0, 0)),
                  pl.BlockSpec((nh, SGU_BLOCK, SGU_BLOCK), lambda i: (0, 0, 0)),
                  pl.BlockSpec((SGU_BLOCK, D_SGU), lambda i: (0, 0))],
        out_specs=pl.BlockSpec((tm, D_SGU), lambda i: (i, 0)),
        out_shape=jax.ShapeDtypeStruct((t, D_SGU), BF16),
        compiler_params=_params(("arbitrary",)),
        name="sgu_mix",
    )(pb, ln_g, ln_b, w_spatial, bias_tile)


def _mid_kernel(y_ref, g_ref, bonus_ref, lng_ref, lnb_ref, yb_ref, x_ref,
                g1_ref, sc2_ref, sh2_ref, g2_ref, n2_ref,
                woa_ref, wob_ref, wrt_ref, w1_ref, w3_ref, w2_ref,
                h2_ref, logit_ref, base_ref):
    ones = _head_ones()
    y = y_ref[...]
    mu = _head_sum(y, ones) * (1.0 / HEAD)
    yc = y - mu
    var = _head_sum(yc * yc, ones) * (1.0 / HEAD)
    ya = yc * lax.rsqrt(var + LN_X_EPS) * lng_ref[...] + lnb_ref[...]
    ya = ((ya + bonus_ref[...]) * g_ref[...]).astype(BF16)
    yo = (jnp.dot(ya, woa_ref[...], preferred_element_type=F32)
          + jnp.dot(yb_ref[...], wob_ref[...], preferred_element_type=F32))
    x1 = x_ref[...] + g1_ref[...] * yo
    h2 = x1 * lax.rsqrt(jnp.mean(x1 * x1, axis=-1, keepdims=True) + RMS_EPS) * n2_ref[...]
    h2 = h2 * (1.0 + sc2_ref[...]) + sh2_ref[...]
    h2b = h2.astype(BF16)
    h2_ref[...] = h2b
    logit_ref[...] = _dot3(wrt_ref[...], h2, dims=(((1,), (1,)), ((), ())))
    act = (_silu(jnp.dot(h2b, w1_ref[...], preferred_element_type=F32))
           * jnp.dot(h2b, w3_ref[...], preferred_element_type=F32)).astype(BF16)
    ms = jnp.dot(act, w2_ref[...], preferred_element_type=F32)
    base_ref[...] = x1 + g2_ref[...] * ms


def _mid(y, g, bonus, lnx_g, lnx_b, yb, x, g1, sc2, sh2, g2, n2g,
         wo_a, wo_b, wr_t, w1s, w3s, w2s, tm, tiles_per_batch):
    t, d = x.shape
    tok = lambda w: pl.BlockSpec((tm, w), lambda i: (i, 0))
    const = lambda shape: pl.BlockSpec(shape, lambda i: (0,) * len(shape))
    mod = pl.BlockSpec((None, 1, d), lambda i: (i // tiles_per_batch, 0, 0))
    return pl.pallas_call(
        _mid_kernel,
        grid=(t // tm,),
        in_specs=[tok(D_RWKV), tok(D_RWKV), tok(D_RWKV),
                  const((1, D_RWKV)), const((1, D_RWKV)),
                  tok(D_SGU), tok(d), mod, mod, mod, mod, const((1, d)),
                  const(wo_a.shape), const(wo_b.shape), const(wr_t.shape),
                  const(w1s.shape), const(w3s.shape), const(w2s.shape)],
        out_specs=[tok(d), pl.BlockSpec((LANES, tm), lambda i: (0, i)), tok(d)],
        out_shape=[jax.ShapeDtypeStruct((t, d), BF16),
                   jax.ShapeDtypeStruct((LANES, t), F32),
                   jax.ShapeDtypeStruct((t, d), F32)],
        compiler_params=_params(("arbitrary",)),
        name="outproj_norm_router_shared",
    )(y, g, bonus, lnx_g, lnx_b, yb, x, g1, sc2, sh2, g2, n2g,
      wo_a, wo_b, wr_t, w1s, w3s, w2s)


MOE_SUB = 128
MOE_CAP = 32
MOE_GROUP = 8


def _route_kernel(logit_ref, bias_ref, gate_ref, pos_ref, fexp_ref, fsub_ref, fdense_ref):
    tt = logit_ref.shape[1]
    scores = _sigmoid(logit_ref[0:N_EXPERTS, :])
    biased = scores + bias_ref[...]
    neg_inf = -jnp.inf

    group_blocks, group_scores = [], []
    for gi in range(N_GROUPS):
        blk = biased[GROUP_SIZE * gi:GROUP_SIZE * (gi + 1), :]
        m1 = jnp.max(blk, axis=0, keepdims=True)
        is_max = blk == m1
        n_max = jnp.sum(jnp.where(is_max, 1.0, 0.0), axis=0, keepdims=True)
        m2 = jnp.max(jnp.where(is_max, neg_inf, blk), axis=0, keepdims=True)
        group_blocks.append(blk)
        group_scores.append(m1 + jnp.where(n_max >= 2.0, m1, m2))

    masked = []
    for gi in range(N_GROUPS):
        rank = jnp.zeros((1, tt), F32)
        for gj in range(N_GROUPS):
            if gj == gi:
                continue
            ahead = (group_scores[gj] >= group_scores[gi]) if gj < gi else \
                    (group_scores[gj] > group_scores[gi])
            rank += jnp.where(ahead, 1.0, 0.0)
        masked.append(jnp.where(rank < float(TOPK_GROUPS), group_blocks[gi], neg_inf))
    sub = lax.broadcasted_iota(jnp.int32, (GROUP_SIZE, tt), 0)
    ranks = [jnp.zeros((GROUP_SIZE, tt), F32) for _ in range(N_GROUPS)]
    for ej in range(N_EXPERTS):
        gj, oj = divmod(ej, GROUP_SIZE)
        other = masked[gj][oj:oj + 1, :]
        for gi in range(N_GROUPS):
            blk = masked[gi]
            if gi > gj:
                ahead = jnp.where(other >= blk, 1.0, 0.0)
            elif gi < gj:
                ahead = jnp.where(other > blk, 1.0, 0.0)
            else:
                ahead = jnp.where(sub > oj, jnp.where(other >= blk, 1.0, 0.0),
                                  jnp.where(other > blk, 1.0, 0.0))
            ranks[gi] = ranks[gi] + ahead
    rank = jnp.concatenate(ranks, axis=0)
    chosen = jnp.where(rank < float(TOP_K), 1.0, 0.0)
    sw = chosen * scores
    gate_ref[...] = sw / jnp.sum(sw, axis=0, keepdims=True) * ROUTED_SCALE

    before = (lax.broadcasted_iota(jnp.int32, (MOE_SUB, MOE_SUB), 0)
              < lax.broadcasted_iota(jnp.int32, (MOE_SUB, MOE_SUB), 1))
    prefix_ones = jnp.where(before, 1.0, 0.0).astype(BF16)
    all_ones = jnp.ones((MOE_SUB, MOE_SUB), BF16)
    lane = lax.broadcasted_iota(jnp.int32, (N_EXPERTS, LANES), 1)
    positions, counts = [], jnp.zeros((N_EXPERTS, LANES), F32)
    for s in range(tt // MOE_SUB):
        blk = chosen[:, MOE_SUB * s:MOE_SUB * (s + 1)].astype(BF16)
        prefix = jnp.dot(blk, prefix_ones, preferred_element_type=F32)
        positions.append(jnp.where(blk > 0, prefix, -1.0))
        total = jnp.dot(blk, all_ones, preferred_element_type=F32)
        counts = counts + jnp.where(lane == s, total, 0.0)
    pos_ref[...] = jnp.concatenate(positions, axis=1)
    per_expert = jnp.max(counts, axis=1, keepdims=True)
    fexp_ref[...] = jnp.broadcast_to(jnp.where(per_expert > float(MOE_CAP), 1, 0),
                                     fexp_ref.shape).astype(jnp.int32)
    per_group = jnp.concatenate(
        [jnp.max(counts[MOE_GROUP * g:MOE_GROUP * (g + 1), :], axis=0, keepdims=True)
         for g in range(N_EXPERTS // MOE_GROUP)], axis=0)
    fsub_ref[...] = jnp.where(per_group > float(MOE_CAP), 1, 0).astype(jnp.int32)
    group_worst = jnp.max(per_group, axis=1, keepdims=True)
    fdense_ref[...] = jnp.broadcast_to(jnp.where(group_worst > float(2 * MOE_CAP), 1, 0),
                                       fdense_ref.shape).astype(jnp.int32)


def _route(logits_t, bias_tile, tt):
    t = logits_t.shape[1]
    n_tiles = t // tt
    n_groups = N_EXPERTS // MOE_GROUP
    tok = pl.BlockSpec((N_EXPERTS, tt), lambda i: (0, i))
    return pl.pallas_call(
        _route_kernel,
        grid=(n_tiles,),
        in_specs=[pl.BlockSpec((LANES, tt), lambda i: (0, i)),
                  pl.BlockSpec((N_EXPERTS, tt), lambda i: (0, 0))],
        out_specs=[tok, tok,
                   pl.BlockSpec((N_EXPERTS, LANES), lambda i: (i, 0)),
                   pl.BlockSpec((n_groups, LANES), lambda i: (i, 0)),
                   pl.BlockSpec((n_groups, LANES), lambda i: (i, 0))],
        out_shape=[jax.ShapeDtypeStruct((N_EXPERTS, t), F32),
                   jax.ShapeDtypeStruct((N_EXPERTS, t), F32),
                   jax.ShapeDtypeStruct((n_tiles * N_EXPERTS, LANES), jnp.int32),
                   jax.ShapeDtypeStruct((n_tiles * n_groups, LANES), jnp.int32),
                   jax.ShapeDtypeStruct((n_tiles * n_groups, LANES), jnp.int32)],
        compiler_params=_params(("arbitrary",)),
        name="route_topk",
    )(logits_t, bias_tile)


def _moe_kernel(fexp_ref, fsub_ref, fdense_ref, h2_ref, gate_ref, pos_ref,
                w1_ref, w3_ref, w2_ref, o_ref, xy_ref, gb_ref, pt_ref):
    tile, eg = pl.program_id(0), pl.program_id(1)
    n_groups = pl.num_programs(1)
    tm = h2_ref.shape[0]
    n_sub = tm // MOE_SUB
    ones = jnp.ones((MOE_SUB, D_EXPERT), BF16)

    @pl.when(eg == 0)
    def _():
        o_ref[...] = jnp.zeros_like(o_ref)

    @pl.when((tile == 0) & (eg == 0))
    def _():
        xy_ref[...] = jnp.zeros_like(xy_ref)
        gb_ref[...] = jnp.zeros_like(gb_ref)

    def lane_broadcast(m):
        hi = m.astype(BF16)
        lo = (m - hi.astype(F32)).astype(BF16)
        return (jnp.dot(hi, ones, preferred_element_type=F32)
                + jnp.dot(lo, ones, preferred_element_type=F32))

    def hidden(x, gate_rows, j):
        return (_silu(jnp.dot(x, w1_ref[j], preferred_element_type=F32))
                * jnp.dot(x, w3_ref[j], preferred_element_type=F32) * gate_rows).astype(BF16)

    def pipelined(n, first, second):
        pending = first(0)
        for i in range(1, n):
            nxt = first(i)
            second(i - 1, pending)
            pending = nxt
        second(n - 1, pending)

    cap = MOE_CAP
    slab_rows = n_sub * cap
    go_dense = fdense_ref[tile * n_groups + eg] != 0

    def needs_second(s):
        return fsub_ref[(tile * n_groups + eg) * n_sub + s] != 0

    def slab_slice(j, q, s=None):
        start = (2 * j + q) * slab_rows
        if s is None:
            return slice(start, start + slab_rows)
        return slice(start + s * cap, start + (s + 1) * cap)

    def gather(s, q):
        cols = slice(MOE_SUB * s, MOE_SUB * (s + 1))
        slot = lax.broadcasted_iota(jnp.int32, (cap, MOE_SUB), 0).astype(F32) + float(q * cap)
        pos, gate = pos_ref[:, cols], gate_ref[:, cols]
        hot = [jnp.where(slot == pos[j:j + 1, :], 1.0, 0.0) for j in range(MOE_GROUP)]
        p = jnp.concatenate(hot, axis=0)
        pg = jnp.concatenate([hot[j] * gate[j:j + 1, :] for j in range(MOE_GROUP)], axis=0)
        xg = jnp.dot(p.astype(BF16), h2_ref[cols, :],
                     preferred_element_type=F32).astype(BF16)
        gb = lane_broadcast(pg)
        for j in range(MOE_GROUP):
            xy_ref[slab_slice(j, q, s), :] = xg[j * cap:(j + 1) * cap]
            gb_ref[slab_slice(j, q, s), :] = gb[j * cap:(j + 1) * cap]
        pt_ref[s, :, q * MOE_GROUP * cap:(q + 1) * MOE_GROUP * cap] = p.T.astype(BF16)

    def scatter(s, q):
        y = jnp.concatenate([xy_ref[slab_slice(j, q, s), :] for j in range(MOE_GROUP)], axis=0)
        o_ref[MOE_SUB * s:MOE_SUB * (s + 1), :] += jnp.dot(
            pt_ref[s, :, q * MOE_GROUP * cap:(q + 1) * MOE_GROUP * cap], y,
            preferred_element_type=F32)

    def project(j, q, act):
        xy_ref[slab_slice(j, q), :] = jnp.dot(
            act, w2_ref[j], preferred_element_type=F32).astype(BF16)

    @pl.when(jnp.logical_not(go_dense))
    def _():
        for s in range(n_sub):
            gather(s, 0)
        pipelined(MOE_GROUP,
                  lambda j: hidden(xy_ref[slab_slice(j, 0), :], gb_ref[slab_slice(j, 0), :], j),
                  lambda j, act: project(j, 0, act))
        for s in range(n_sub):
            scatter(s, 0)
        for s in range(n_sub):
            pl.when(needs_second(s))(functools.partial(gather, s, 1))
        for j in range(MOE_GROUP):
            @pl.when(fexp_ref[tile * N_EXPERTS + eg * MOE_GROUP + j] != 0)
            def _(j=j):
                project(j, 1, hidden(xy_ref[slab_slice(j, 1), :],
                                     gb_ref[slab_slice(j, 1), :], j))
        for s in range(n_sub):
            pl.when(needs_second(s))(functools.partial(scatter, s, 1))

    @pl.when(go_dense)
    def _():
        eye = (lax.broadcasted_iota(jnp.int32, (MOE_SUB, MOE_SUB), 0)
               == lax.broadcasted_iota(jnp.int32, (MOE_SUB, MOE_SUB), 1))
        for j in range(MOE_GROUP):
            for s in range(n_sub):
                cols = slice(MOE_SUB * s, MOE_SUB * (s + 1))
                gb_ref[cols, :] = lane_broadcast(
                    jnp.where(eye, gate_ref[j:j + 1, cols], 0.0))
            act = hidden(h2_ref[...], gb_ref[0:tm, :], j)
            o_ref[...] += jnp.dot(act, w2_ref[j], preferred_element_type=F32)


def _moe(fexp, fsub, fdense, h2, gates_t, pos_t, w1e, w3e, w2e, tm):
    t, d = h2.shape
    ne = w1e.shape[0]
    assert ne % MOE_GROUP == 0 and tm % MOE_SUB == 0
    n_sub = tm // MOE_SUB
    rows = MOE_GROUP * 2 * n_sub * MOE_CAP
    assert rows >= tm, "gate scratch is reused by the dense path"
    tok = pl.BlockSpec((tm, d), lambda i, e, *_: (i, 0))
    per_expert = pl.BlockSpec((MOE_GROUP, tm), lambda i, e, *_: (e, i))
    return pl.pallas_call(
        _moe_kernel,
        grid_spec=pltpu.PrefetchScalarGridSpec(
            num_scalar_prefetch=3,
            grid=(t // tm, ne // MOE_GROUP),
            in_specs=[tok, per_expert, per_expert,
                      pl.BlockSpec((MOE_GROUP, d, D_EXPERT), lambda i, e, *_: (e, 0, 0)),
                      pl.BlockSpec((MOE_GROUP, d, D_EXPERT), lambda i, e, *_: (e, 0, 0)),
                      pl.BlockSpec((MOE_GROUP, D_EXPERT, d), lambda i, e, *_: (e, 0, 0))],
            out_specs=tok,
            scratch_shapes=[pltpu.VMEM((rows, d), BF16),
                            pltpu.VMEM((rows, D_EXPERT), F32),
                            pltpu.VMEM((n_sub, MOE_SUB, 2 * MOE_GROUP * MOE_CAP), BF16)]),
        out_shape=jax.ShapeDtypeStruct((t, d), F32),
        compiler_params=_params(("arbitrary", "arbitrary")),
        name="moe_experts",
    )(fexp, fsub, fdense, h2, gates_t, pos_t, w1e, w3e, w2e)


def _final_kernel(base_ref, routed_ref, g2_ref, nf_ref, o_ref):
    xo = base_ref[...] + g2_ref[...] * routed_ref[...]
    o_ref[...] = (xo * lax.rsqrt(jnp.mean(xo * xo, axis=-1, keepdims=True) + RMS_EPS)
                  * nf_ref[...])


def _final(base, routed, g2, nf, tm, tiles_per_batch):
    t, d = base.shape
    tok = pl.BlockSpec((tm, d), lambda i: (i, 0))
    return pl.pallas_call(
        _final_kernel,
        grid=(t // tm,),
        in_specs=[tok, tok,
                  pl.BlockSpec((None, 1, d), lambda i: (i // tiles_per_batch, 0, 0)),
                  pl.BlockSpec((1, d), lambda i: (0, 0))],
        out_specs=tok,
        out_shape=jax.ShapeDtypeStruct((t, d), F32),
        compiler_params=_params(("arbitrary",)),
        name="residual_final_norm",
    )(base, routed, g2, nf)


def _tile(n, want):
    t = min(n, want)
    assert n % t == 0, (n, t)
    return t


def kernel(x, c, w_ada, b_ada, norm1_g, w_in, mu_shift, w0, w_decay_up, a0, w_a_up, w_g_up,
           k_k, k_a, r_k, lnx_g, lnx_b, sgu_ln_g, sgu_ln_b, w_spatial, b_spatial, w_out,
           norm2_g, w_router, e_bias, w1_e, w3_e, w2_e, w1_s, w3_s, w2_s, norm_f_g):
    b, s, d = x.shape
    t = b * s
    assert w_ada.shape[0] == 1, "single-layer block"
    assert d == 2 * D_RWKV and s % SGU_BLOCK == 0
    row = lambda vec: vec.reshape(1, -1)

    c_pad = jnp.pad(c, ((0, 8 - b), (0, 0)))
    mod = _mod(c_pad, w_ada[0], b_ada)[:b]
    sh1, sc1, g1, sh2, sc2, g2 = [m[:, None, :] for m in jnp.split(mod, 6, axis=-1)]

    ps, pb = _inproj(x, sc1, sh1, norm1_g, w_in[0].astype(BF16), mu_shift, _tile(s, 512))

    wlora = jnp.zeros((LANES, 2 * D_RWKV), F32)
    wlora = wlora.at[:D_DECAY_LORA, :D_RWKV].set(w_decay_up[0])
    wlora = wlora.at[D_DECAY_LORA:, D_RWKV:].set(w_a_up[0])
    r, wdec, kh, v, kk, bb, g, bonus = _prep(
        ps.reshape(t, N_SHIFT), w0, a0, wlora, w_g_up[0], k_k, k_a, row(r_k), _tile(t, 512))

    seq = lambda z: z.reshape(b, s, D_RWKV)
    w_c, rh_c, u0_c, y0_c, bh_c, kh_c, pl_c = _chunk(r, wdec, kh, v, kk, bb)
    y = _rwkv(seq(w_c), seq(rh_c), seq(u0_c), seq(y0_c), seq(bh_c), seq(kh_c), seq(v),
              seq(pl_c), _tile(s, 256))

    bias_tile = jnp.repeat(b_spatial[0].T, HEAD, axis=1)
    yb = _sgu(pb.reshape(t, 2 * D_SGU), sgu_ln_g, sgu_ln_b, w_spatial[0], bias_tile,
              _tile(t, 512))

    tm_mid = _tile(s, 512)
    wr_t = jnp.pad(w_router[0].T, ((0, LANES - N_EXPERTS), (0, 0)))
    wo = w_out[0].astype(BF16)
    h2, logits_t, base = _mid(
        y.reshape(t, D_RWKV), g, bonus, lnx_g, lnx_b, yb, x.reshape(t, d),
        g1, sc2, sh2, g2, norm2_g, wo[:D_RWKV], wo[D_RWKV:], wr_t,
        w1_s[0].astype(BF16), w3_s[0].astype(BF16), w2_s[0].astype(BF16), tm_mid, s // tm_mid)

    tm_moe = _tile(t, 1024)
    gates_t, pos_t, fexp, fsub, fdense = _route(
        logits_t, jnp.broadcast_to(e_bias[0][:, None], (N_EXPERTS, tm_moe)), tm_moe)

    routed = _moe(fexp[:, 0], fsub[:, :tm_moe // MOE_SUB].reshape(-1), fdense[:, 0],
                  h2, gates_t, pos_t, w1_e[0].astype(BF16), w3_e[0].astype(BF16),
                  w2_e[0].astype(BF16), tm_moe)
    tm_fin = _tile(s, 512)
    out = _final(base, routed, g2, row(norm_f_g), tm_fin, s // tm_fin)
    return out.reshape(b, s, d)
```

```python
import functools

import jax
import jax.numpy as jnp
from jax import lax
from jax.experimental import pallas as pl
from jax.experimental.pallas import tpu as pltpu

F32 = jnp.float32
BF16 = jnp.bfloat16

HEAD = 64
LANES = 128
SUBLANES = 8
D_RWKV = 512
D_SGU = 512
N_RWKV_PAIRS = D_RWKV // LANES
D_DECAY_LORA = 64
D_AAA_LORA = 64
D_GATE_LORA = 128
N_SHIFT = 3 * D_RWKV + D_DECAY_LORA + D_AAA_LORA + D_GATE_LORA
SGU_BLOCK = 128
CHUNK = 64
N_EXPERTS = 64
N_GROUPS = 8
GROUP_SIZE = N_EXPERTS // N_GROUPS
TOPK_GROUPS = 4
TOP_K = 8
D_EXPERT = 256
ROUTED_SCALE = 2.5
RMS_EPS = 1e-6
LN_EPS = 1e-5
LN_X_EPS = 64e-5
VMEM_LIMIT_BYTES = 56 * 1024 * 1024


def _params(semantics):
    return pltpu.CompilerParams(dimension_semantics=semantics,
                                vmem_limit_bytes=VMEM_LIMIT_BYTES)


def _split3(x):
    hi = x.astype(BF16)
    r1 = x - hi.astype(F32)
    mid = r1.astype(BF16)
    lo = (r1 - mid.astype(F32)).astype(BF16)
    return hi, mid, lo


def _dot_exact_rhs(x, rhs_bf16):
    return jnp.dot(jnp.concatenate(_split3(x), axis=1),
                   jnp.concatenate([rhs_bf16] * 3, axis=0), preferred_element_type=F32)


def _dot3(a, b, dims=(((1,), (0,)), ((), ()))):
    ((ca,), (cb,)), _ = dims
    a_hi = a.astype(BF16)
    a_lo = (a - a_hi.astype(F32)).astype(BF16)
    b_hi = b.astype(BF16)
    b_lo = (b - b_hi.astype(F32)).astype(BF16)
    dg = functools.partial(lax.dot_general, dimension_numbers=dims,
                           preferred_element_type=F32)
    return (dg(jnp.concatenate([a_hi, a_lo], axis=ca), jnp.concatenate([b_hi, b_hi], axis=cb))
            + dg(a_hi, b_lo))


def _head_ones():
    r = lax.broadcasted_iota(jnp.int32, (LANES, LANES), 0) // HEAD
    c = lax.broadcasted_iota(jnp.int32, (LANES, LANES), 1) // HEAD
    return jnp.where(r == c, 1.0, 0.0).astype(BF16)


def _head_sum(z, ones):
    parts = [_dot_exact_rhs(z[:, LANES * p:LANES * (p + 1)], ones)
             for p in range(z.shape[1] // LANES)]
    return jnp.concatenate(parts, axis=-1)


def _sigmoid(x):
    return 1.0 / (1.0 + jnp.exp(-x))


def _silu(x):
    return x * _sigmoid(x)


def _mod_kernel(c_ref, w_ref, b_ref, o_ref):
    s = _silu(c_ref[...])
    o_ref[...] = _dot3(s, w_ref[...]) + b_ref[...]


def _mod(c_pad, w_ada, b_ada):
    rows, d = c_pad.shape
    n = w_ada.shape[1]
    tn = 1024
    return pl.pallas_call(
        _mod_kernel,
        grid=(n // tn,),
        in_specs=[pl.BlockSpec((rows, d), lambda j: (0, 0)),
                  pl.BlockSpec((d, tn), lambda j: (0, j)),
                  pl.BlockSpec((1, tn), lambda j: (0, j))],
        out_specs=pl.BlockSpec((rows, tn), lambda j: (0, j)),
        out_shape=jax.ShapeDtypeStruct((rows, n), F32),
        compiler_params=_params(("arbitrary",)),
        name="adaln_mod",
    )(c_pad, w_ada, b_ada)


def _inproj_kernel(x_ref, sc_ref, sh_ref, g_ref, w_ref, mu_ref, ps_ref, pb_ref, carry_ref):
    @pl.when(pl.program_id(1) == 0)
    def _():
        carry_ref[...] = jnp.zeros_like(carry_ref)

    x = x_ref[...]
    tm = x.shape[0]
    h = x * lax.rsqrt(jnp.mean(x * x, axis=-1, keepdims=True) + RMS_EPS) * g_ref[...]
    h = h * (1.0 + sc_ref[...]) + sh_ref[...]
    proj = jnp.dot(h.astype(BF16), w_ref[...], preferred_element_type=F32)
    ps = proj[:, :N_SHIFT]
    prev = pltpu.roll(ps, 1, 0)
    first = lax.broadcasted_iota(jnp.int32, ps.shape, 0) == 0
    prev = jnp.where(first, carry_ref[0:1, :], prev)
    carry_ref[0:1, :] = ps[tm - 1:tm, :]
    ps_ref[...] = ps + (prev - ps) * mu_ref[...]
    pb_ref[...] = proj[:, N_SHIFT:]


def _inproj(x, sc1, sh1, g, w_in_bf16, mu, tm):
    b, s, d = x.shape
    d_in = w_in_bf16.shape[1]
    row = lambda bi, i: (bi, 0, 0)
    return pl.pallas_call(
        _inproj_kernel,
        grid=(b, s // tm),
        in_specs=[pl.BlockSpec((None, tm, d), lambda bi, i: (bi, i, 0)),
                  pl.BlockSpec((None, 1, d), row),
                  pl.BlockSpec((None, 1, d), row),
                  pl.BlockSpec((1, d), lambda bi, i: (0, 0)),
                  pl.BlockSpec((d, d_in), lambda bi, i: (0, 0)),
                  pl.BlockSpec((1, N_SHIFT), lambda bi, i: (0, 0))],
        out_specs=[pl.BlockSpec((None, tm, N_SHIFT), lambda bi, i: (bi, i, 0)),
                   pl.BlockSpec((None, tm, d_in - N_SHIFT), lambda bi, i: (bi, i, 0))],
        out_shape=[jax.ShapeDtypeStruct((b, s, N_SHIFT), F32),
                   jax.ShapeDtypeStruct((b, s, d_in - N_SHIFT), F32)],
        scratch_shapes=[pltpu.VMEM((8, N_SHIFT), F32)],
        compiler_params=_params(("arbitrary", "arbitrary")),
        name="inproj_shift",
    )(x, sc1, sh1, g, w_in_bf16, mu)


def _prep_kernel(ps_ref, w0_ref, a0_ref, wlora_ref, wg_ref, kk_ref, ka_ref, rk_ref,
                 r_o, w_o, k_o, v_o, kk_o, b_o, g_o, bonus_o):
    ps = ps_ref[...]
    c1, c2, c3 = D_RWKV, 2 * D_RWKV, 3 * D_RWKV
    r, k, v = ps[:, :c1], ps[:, c1:c2], ps[:, c2:c3]
    xwa = ps[:, c3:c3 + LANES]
    xg = ps[:, c3 + LANES:]
    is_w = lax.broadcasted_iota(jnp.int32, xwa.shape, 1) < D_DECAY_LORA
    lora = _dot3(jnp.where(is_w, jnp.tanh(xwa), xwa), wlora_ref[...])
    wlog = -jax.nn.softplus(-(w0_ref[...] + lora[:, :c1])) - 0.5
    log_decay = -jnp.exp(wlog)
    a = _sigmoid(a0_ref[...] + lora[:, c1:])
    g = _dot3(_sigmoid(xg), wg_ref[...])
    ones = _head_ones()
    kk = k * kk_ref[...]
    kk = kk / jnp.maximum(jnp.sqrt(_head_sum(kk * kk, ones)), 1e-12)
    kh = k * (1.0 + (a - 1.0) * ka_ref[...])
    r_o[...] = r
    w_o[...] = log_decay
    k_o[...] = kh
    v_o[...] = v
    kk_o[...] = kk
    b_o[...] = kk * a
    g_o[...] = g
    bonus_o[...] = _head_sum(r * kh * rk_ref[...], ones) * v


def _prep(ps, w0, a0, wlora, wg, k_k, k_a, r_k, tm):
    t = ps.shape[0]
    const = lambda shape: pl.BlockSpec(shape, lambda i: (0, 0))
    out = jax.ShapeDtypeStruct((t, D_RWKV), F32)
    return pl.pallas_call(
        _prep_kernel,
        grid=(t // tm,),
        in_specs=[pl.BlockSpec((tm, N_SHIFT), lambda i: (i, 0)),
                  const((1, D_RWKV)), const((1, D_RWKV)),
                  const((LANES, 2 * D_RWKV)), const((D_GATE_LORA, D_RWKV)),
                  const((1, D_RWKV)), const((1, D_RWKV)), const((1, D_RWKV))],
        out_specs=[pl.BlockSpec((tm, D_RWKV), lambda i: (i, 0))] * 8,
        out_shape=[out] * 8,
        compiler_params=_params(("arbitrary",)),
        name="rwkv_prep",
    )(ps, w0, a0, wlora, wg, k_k, k_a, r_k)


RWKV_CHUNK = 16
CHUNK_TILE = 128

_NN = (((1,), (0,)), ((), ()))
_NT = (((1,), (1,)), ((), ()))


def _dot1(a, b, dims=_NN):
    return lax.dot_general(a.astype(BF16), b.astype(BF16), dims, preferred_element_type=F32)


def _dot_exact_lhs(lhs_bf16, x):
    return jnp.dot(jnp.concatenate([lhs_bf16] * 3, axis=1),
                   jnp.concatenate(_split3(x), axis=0), preferred_element_type=F32)


def _chunk_kernel(r_ref, lw_ref, k_ref, v_ref, kk_ref, b_ref,
                  w_o, rh_o, u0_o, y0_o, bh_o, kh_o, pl_o):
    n = CHUNK_TILE
    tpos = lax.broadcasted_iota(jnp.int32, (n, n), 0)
    spos = lax.broadcasted_iota(jnp.int32, (n, n), 1)
    same = (tpos // RWKV_CHUNK) == (spos // RWKV_CHUNK)
    incl = same & (spos <= tpos)
    strict = same & (spos < tpos)
    tri_ones = jnp.where(incl, 1.0, 0.0).astype(BF16)
    blk_ones = jnp.where(same, 1.0, 0.0).astype(BF16)
    eye = jnp.where(tpos == spos, 1.0, 0.0)
    lane_head = lax.broadcasted_iota(jnp.int32, (n, LANES), 1) // HEAD

    pairs = range(N_RWKV_PAIRS)
    slabs = [slice(LANES * p, LANES * (p + 1)) for p in pairs]
    lw, r, k, v, kk, b = ([ref[:, sl] for sl in slabs]
                          for ref in (lw_ref, r_ref, k_ref, v_ref, kk_ref, b_ref))
    cum = [_dot_exact_lhs(tri_ones, x) for x in lw]
    tot = [_dot_exact_lhs(blk_ones, x) for x in lw]
    at = [-kk[p] * jnp.exp(cum[p] - lw[p]) for p in pairs]
    rt = [r[p] * jnp.exp(cum[p]) for p in pairs]
    inv = [jnp.exp(-cum[p]) for p in pairs]
    bt = [b[p] * inv[p] for p in pairs]
    kt = [k[p] * inv[p] for p in pairs]

    chains = [(p, h) for p in pairs for h in range(LANES // HEAD)]
    mine = [lane_head == h for h in range(LANES // HEAD)]
    a_h = [jnp.where(mine[h], at[p], 0.0) for p, h in chains]
    r_h = [jnp.where(mine[h], rt[p], 0.0) for p, h in chains]
    v_h = [jnp.where(mine[h], v[p], 0.0) for p, h in chains]
    each = range(len(chains))
    n = CHUNK_TILE
    btkt = [jnp.concatenate([bt[p], kt[p]], axis=0) for p in pairs]
    a_bk = [_dot3(a_h[c], btkt[chains[c][0]], _NT) for c in each]
    a_ab = [jnp.where(strict, m[:, :n], 0.0) for m in a_bk]
    a_ak = [jnp.where(strict, m[:, n:], 0.0) for m in a_bk]
    tinv = [eye + a for a in a_ab]
    x = a_ab
    for _ in range(RWKV_CHUNK.bit_length() - 2):
        x = [_dot1(xc, xc) for xc in x]
        tinv = [tc + _dot1(tc, xc) for tc, xc in zip(tinv, x)]
    akv = [_dot3(a_ak[c], v_h[c]) for c in each]
    m_bk = [_dot1(r_h[c], btkt[chains[c][0]], _NT) for c in each]
    m_rb = [jnp.where(incl, m[:, :n], 0.0) for m in m_bk]
    m_rk = [jnp.where(incl, m[:, n:], 0.0) for m in m_bk]
    wu = [_dot3(tinv[c], jnp.concatenate([a_h[c], akv[c]], axis=1)) for c in each]
    w_h = [m[:, :LANES] for m in wu]
    u0_h = [m[:, LANES:] for m in wu]
    ry = [_dot1(m_rb[c], wu[c]) for c in each]
    rh_h = [r_h[c] + ry[c][:, :LANES] for c in each]
    y0_h = [ry[c][:, LANES:] + _dot1(m_rk[c], v_h[c]) for c in each]

    for p in pairs:
        c0, c1 = 2 * p, 2 * p + 1
        rem = jnp.exp(tot[p] - cum[p])
        w_o[:, slabs[p]] = w_h[c0] + w_h[c1]
        rh_o[:, slabs[p]] = rh_h[c0] + rh_h[c1]
        u0_o[:, slabs[p]] = u0_h[c0] + u0_h[c1]
        y0_o[:, slabs[p]] = y0_h[c0] + y0_h[c1]
        bh_o[:, slabs[p]] = b[p] * rem
        kh_o[:, slabs[p]] = k[p] * rem
        pl_o[:, slabs[p]] = jnp.exp(tot[p])


def _chunk(r, lw, k, v, kk, b):
    t, d = r.shape
    spec = pl.BlockSpec((CHUNK_TILE, d), lambda i: (i, 0))
    out = jax.ShapeDtypeStruct((t, d), F32)
    return pl.pallas_call(
        _chunk_kernel,
        grid=(t // CHUNK_TILE,),
        in_specs=[spec] * 6,
        out_specs=[spec] * 7,
        out_shape=[out] * 7,
        compiler_params=_params(("arbitrary",)),
        name="rwkv_chunk_prep",
    )(r, lw, k, v, kk, b)


def _rwkv_kernel(w_ref, rh_ref, u0_ref, y0_ref, bh_ref, kh_ref, v_ref, pl_ref,
                 y_ref, state_ref):
    @pl.when(pl.program_id(0) == 0)
    def _():
        state_ref[...] = jnp.zeros_like(state_ref)

    nb, tt, _ = w_ref.shape
    L = RWKV_CHUNK
    assert 6 * L + SUBLANES <= LANES
    row_head = lax.broadcasted_iota(jnp.int32, (LANES, LANES), 0) // HEAD
    col_head = lax.broadcasted_iota(jnp.int32, (LANES, LANES), 1) // HEAD
    same_head = row_head == col_head
    pad = jnp.zeros((LANES - 6 * L - SUBLANES, LANES), F32)
    zeros_tail = jnp.zeros((LANES - 6 * L, LANES), BF16)

    def split(x):
        hi = x.astype(BF16)
        return hi, (x - hi.astype(F32)).astype(BF16)

    def chunk(ci, carry):
        t0 = pl.multiple_of(ci * L, L)
        streams = [(bi, p) for bi in range(nb) for p in range(N_RWKV_PAIRS)]

        def rows(ref, n=L):
            return [ref[bi, pl.ds(t0, n), LANES * p:LANES * (p + 1)] for bi, p in streams]

        each = range(len(streams))
        s = [state_ref[bi * N_RWKV_PAIRS + p] for bi, p in streams]
        w, rh, u0, y0, bh, kh, v = (rows(ref) for ref in
                                   (w_ref, rh_ref, u0_ref, y0_ref, bh_ref, kh_ref, v_ref))
        p_rows = rows(pl_ref, SUBLANES)
        left, p_col = [], []
        for c in each:
            (b_hi, b_lo), (k_hi, k_lo) = split(bh[c]), split(kh[c])
            block = jnp.concatenate([x.astype(F32) for x in (b_hi, b_lo, b_hi, k_hi, k_lo, k_hi)]
                                    + [p_rows[c], pad], axis=0).T
            left.append(block.astype(BF16))
            p_col.append(jnp.broadcast_to(block[:, 6 * L:6 * L + 1], (LANES, LANES)))
        g = []
        for c in each:
            (l_hi, l_lo), (s_hi, s_lo) = split(jnp.concatenate([w[c], rh[c]], axis=0)), split(s[c])
            g.append(jnp.dot(jnp.concatenate([l_hi, l_lo], axis=1),
                             jnp.concatenate([s_hi, s_hi], axis=0), preferred_element_type=F32)
                     + jnp.dot(l_hi, s_lo, preferred_element_type=F32))
        u = [g[c][:L] + u0[c] for c in each]
        upd = []
        for c in each:
            (u_hi, u_lo), (v_hi, v_lo) = split(u[c]), split(v[c])
            right = jnp.concatenate([u_hi, u_hi, u_lo, v_hi, v_hi, v_lo, zeros_tail], axis=0)
            upd.append(jnp.dot(left[c], right, preferred_element_type=F32))
        for c, (bi, p) in enumerate(streams):
            y_ref[bi, pl.ds(t0, L), LANES * p:LANES * (p + 1)] = g[c][L:] + y0[c]
            state_ref[bi * N_RWKV_PAIRS + p] = (p_col[c] * s[c]
                                                + jnp.where(same_head, upd[c], 0.0))
        return carry

    lax.fori_loop(0, tt // L, chunk, 0)


def _rwkv(w, rh, u0, y0, bh, kh, v, p_l, tt):
    nb, s, d = w.shape
    spec = pl.BlockSpec((nb, tt, d), lambda i: (0, i, 0))
    return pl.pallas_call(
        _rwkv_kernel,
        grid=(s // tt,),
        in_specs=[spec] * 8,
        out_specs=spec,
        out_shape=jax.ShapeDtypeStruct((nb, s, d), F32),
        scratch_shapes=[pltpu.VMEM((nb * N_RWKV_PAIRS, LANES, LANES), F32)],
        compiler_params=_params(("arbitrary",)),
        name="rwkv_scan",
    )(w, rh, u0, y0, bh, kh, v, p_l)


def _gelu_tanh(x):
    return 0.5 * x * (1.0 + jnp.tanh(0.7978845608028654 * (x + 0.044715 * (x * x * x))))


def _sgu_kernel(pb_ref, g_ref, b_ref, ws_ref, bias_ref, o_ref):
    z = _gelu_tanh(pb_ref[...])
    u, v = z[:, :D_SGU], z[:, D_SGU:]
    mu = jnp.mean(v, axis=-1, keepdims=True)
    vc = v - mu
    var = jnp.mean(vc * vc, axis=-1, keepdims=True)
    vn = (vc * lax.rsqrt(var + LN_EPS) * g_ref[...] + b_ref[...]).astype(BF16)
    tpos = lax.broadcasted_iota(jnp.int32, (SGU_BLOCK, SGU_BLOCK), 0) // CHUNK
    spos = lax.broadcasted_iota(jnp.int32, (SGU_BLOCK, SGU_BLOCK), 1) // CHUNK
    causal = spos <= tpos
    first_head = lax.broadcasted_iota(jnp.int32, (SGU_BLOCK, LANES), 1) < HEAD
    for p in range(D_SGU // LANES):
        sl = slice(LANES * p, LANES * (p + 1))
        ws0 = jnp.where(causal, ws_ref[2 * p], 0.0).astype(BF16)
        ws1 = jnp.where(causal, ws_ref[2 * p + 1], 0.0).astype(BF16)
        for n in range(pb_ref.shape[0] // SGU_BLOCK):
            rows = slice(SGU_BLOCK * n, SGU_BLOCK * (n + 1))
            vb = vn[rows, sl]
            sp = jnp.where(first_head,
                           jnp.dot(ws0, vb, preferred_element_type=F32),
                           jnp.dot(ws1, vb, preferred_element_type=F32))
            o_ref[rows, sl] = (u[rows, sl] * (sp + bias_ref[:, sl])).astype(o_ref.dtype)


def _sgu(pb, ln_g, ln_b, w_spatial, bias_tile, tm):
    t = pb.shape[0]
    nh = w_spatial.shape[0]
    return pl.pallas_call(
        _sgu_kernel,
        grid=(t // tm,),
        in_specs=[pl.BlockSpec((tm, 2 * D_SGU), lambda i: (i, 0)),
                  pl.BlockSpec((1, D_SGU), lambda i: (0, 0)),
                  pl.BlockSpec((1, D_SGU), lambda i: (0, 0)),
                  pl.BlockSpec((nh, SGU_BLOCK, SGU_BLOCK), lambda i: (0, 0, 0)),
                  pl.BlockSpec((SGU_BLOCK, D_SGU), lambda i: (0, 0))],
        out_specs=pl.BlockSpec((tm, D_SGU), lambda i: (i, 0)),
        out_shape=jax.ShapeDtypeStruct((t, D_SGU), BF16),
        compiler_params=_params(("arbitrary",)),
        name="sgu_mix",
    )(pb, ln_g, ln_b, w_spatial, bias_tile)


def _mid_kernel(y_ref, g_ref, bonus_ref, lng_ref, lnb_ref, yb_ref, x_ref,
                g1_ref, sc2_ref, sh2_ref, g2_ref, n2_ref,
                woa_ref, wob_ref, wrt_ref, w1_ref, w3_ref, w2_ref,
                h2_ref, logit_ref, base_ref):
    ones = _head_ones()
    y = y_ref[...]
    mu = _head_sum(y, ones) * (1.0 / HEAD)
    yc = y - mu
    var = _head_sum(yc * yc, ones) * (1.0 / HEAD)
    ya = yc * lax.rsqrt(var + LN_X_EPS) * lng_ref[...] + lnb_ref[...]
    ya = ((ya + bonus_ref[...]) * g_ref[...]).astype(BF16)
    yo = (jnp.dot(ya, woa_ref[...], preferred_element_type=F32)
          + jnp.dot(yb_ref[...], wob_ref[...], preferred_element_type=F32))
    x1 = x_ref[...] + g1_ref[...] * yo
    h2 = x1 * lax.rsqrt(jnp.mean(x1 * x1, axis=-1, keepdims=True) + RMS_EPS) * n2_ref[...]
    h2 = h2 * (1.0 + sc2_ref[...]) + sh2_ref[...]
    h2b = h2.astype(BF16)
    h2_ref[...] = h2b
    logit_ref[...] = _dot3(wrt_ref[...], h2, dims=(((1,), (1,)), ((), ())))
    act = (_silu(jnp.dot(h2b, w1_ref[...], preferred_element_type=F32))
           * jnp.dot(h2b, w3_ref[...], preferred_element_type=F32)).astype(BF16)
    ms = jnp.dot(act, w2_ref[...], preferred_element_type=F32)
    base_ref[...] = x1 + g2_ref[...] * ms


def _mid(y, g, bonus, lnx_g, lnx_b, yb, x, g1, sc2, sh2, g2, n2g,
         wo_a, wo_b, wr_t, w1s, w3s, w2s, tm, tiles_per_batch):
    t, d = x.shape
    tok = lambda w: pl.BlockSpec((tm, w), lambda i: (i, 0))
    const = lambda shape: pl.BlockSpec(shape, lambda i: (0,) * len(shape))
    mod = pl.BlockSpec((None, 1, d), lambda i: (i // tiles_per_batch, 0, 0))
    return pl.pallas_call(
        _mid_kernel,
        grid=(t // tm,),
        in_specs=[tok(D_RWKV), tok(D_RWKV), tok(D_RWKV),
                  const((1, D_RWKV)), const((1, D_RWKV)),
                  tok(D_SGU), tok(d), mod, mod, mod, mod, const((1, d)),
                  const(wo_a.shape), const(wo_b.shape), const(wr_t.shape),
                  const(w1s.shape), const(w3s.shape), const(w2s.shape)],
        out_specs=[tok(d), pl.BlockSpec((LANES, tm), lambda i: (0, i)), tok(d)],
        out_shape=[jax.ShapeDtypeStruct((t, d), BF16),
                   jax.ShapeDtypeStruct((LANES, t), F32),
                   jax.ShapeDtypeStruct((t, d), F32)],
        compiler_params=_params(("arbitrary",)),
        name="outproj_norm_router_shared",
    )(y, g, bonus, lnx_g, lnx_b, yb, x, g1, sc2, sh2, g2, n2g,
      wo_a, wo_b, wr_t, w1s, w3s, w2s)


MOE_SUB = 128
MOE_CAP = 32
MOE_GROUP = 8


def _route_kernel(logit_ref, bias_ref, gate_ref, pos_ref, fexp_ref, fsub_ref, fdense_ref):
    tt = logit_ref.shape[1]
    scores = _sigmoid(logit_ref[0:N_EXPERTS, :])
    biased = scores + bias_ref[...]
    neg_inf = -jnp.inf

    group_blocks, group_scores = [], []
    for gi in range(N_GROUPS):
        blk = biased[GROUP_SIZE * gi:GROUP_SIZE * (gi + 1), :]
        m1 = jnp.max(blk, axis=0, keepdims=True)
        is_max = blk == m1
        n_max = jnp.sum(jnp.where(is_max, 1.0, 0.0), axis=0, keepdims=True)
        m2 = jnp.max(jnp.where(is_max, neg_inf, blk), axis=0, keepdims=True)
        group_blocks.append(blk)
        group_scores.append(m1 + jnp.where(n_max >= 2.0, m1, m2))

    masked = []
    for gi in range(N_GROUPS):
        rank = jnp.zeros((1, tt), F32)
        for gj in range(N_GROUPS):
            if gj == gi:
                continue
            ahead = (group_scores[gj] >= group_scores[gi]) if gj < gi else \
                    (group_scores[gj] > group_scores[gi])
            rank += jnp.where(ahead, 1.0, 0.0)
        masked.append(jnp.where(rank < float(TOPK_GROUPS), group_blocks[gi], neg_inf))
    sub = lax.broadcasted_iota(jnp.int32, (GROUP_SIZE, tt), 0)
    ranks = [jnp.zeros((GROUP_SIZE, tt), F32) for _ in range(N_GROUPS)]
    for ej in range(N_EXPERTS):
        gj, oj = divmod(ej, GROUP_SIZE)
        other = masked[gj][oj:oj + 1, :]
        for gi in range(N_GROUPS):
            blk = masked[gi]
            if gi > gj:
                ahead = jnp.where(other >= blk, 1.0, 0.0)
            elif gi < gj:
                ahead = jnp.where(other > blk, 1.0, 0.0)
            else:
                ahead = jnp.where(sub > oj, jnp.where(other >= blk, 1.0, 0.0),
                                  jnp.where(other > blk, 1.0, 0.0))
            ranks[gi] = ranks[gi] + ahead
    rank = jnp.concatenate(ranks, axis=0)
    chosen = jnp.where(rank < float(TOP_K), 1.0, 0.0)
    sw = chosen * scores
    gate_ref[...] = sw / jnp.sum(sw, axis=0, keepdims=True) * ROUTED_SCALE

    before = (lax.broadcasted_iota(jnp.int32, (MOE_SUB, MOE_SUB), 0)
              < lax.broadcasted_iota(jnp.int32, (MOE_SUB, MOE_SUB), 1))
    prefix_ones = jnp.where(before, 1.0, 0.0).astype(BF16)
    all_ones = jnp.ones((MOE_SUB, MOE_SUB), BF16)
    lane = lax.broadcasted_iota(jnp.int32, (N_EXPERTS, LANES), 1)
    positions, counts = [], jnp.zeros((N_EXPERTS, LANES), F32)
    for s in range(tt // MOE_SUB):
        blk = chosen[:, MOE_SUB * s:MOE_SUB * (s + 1)].astype(BF16)
        prefix = jnp.dot(blk, prefix_ones, preferred_element_type=F32)
        positions.append(jnp.where(blk > 0, prefix, -1.0))
        total = jnp.dot(blk, all_ones, preferred_element_type=F32)
        counts = counts + jnp.where(lane == s, total, 0.0)
    pos_ref[...] = jnp.concatenate(positions, axis=1)
    per_expert = jnp.max(counts, axis=1, keepdims=True)
    fexp_ref[...] = jnp.broadcast_to(jnp.where(per_expert > float(MOE_CAP), 1, 0),
                                     fexp_ref.shape).astype(jnp.int32)
    per_group = jnp.concatenate(
        [jnp.max(counts[MOE_GROUP * g:MOE_GROUP * (g + 1), :], axis=0, keepdims=True)
         for g in range(N_EXPERTS // MOE_GROUP)], axis=0)
    fsub_ref[...] = jnp.where(per_group > float(MOE_CAP), 1, 0).astype(jnp.int32)
    group_worst = jnp.max(per_group, axis=1, keepdims=True)
    fdense_ref[...] = jnp.broadcast_to(jnp.where(group_worst > float(2 * MOE_CAP), 1, 0),
                                       fdense_ref.shape).astype(jnp.int32)


def _route(logits_t, bias_tile, tt):
    t = logits_t.shape[1]
    n_tiles = t // tt
    n_groups = N_EXPERTS // MOE_GROUP
    tok = pl.BlockSpec((N_EXPERTS, tt), lambda i: (0, i))
    return pl.pallas_call(
        _route_kernel,
        grid=(n_tiles,),
        in_specs=[pl.BlockSpec((LANES, tt), lambda i: (0, i)),
                  pl.BlockSpec((N_EXPERTS, tt), lambda i: (0, 0))],
        out_specs=[tok, tok,
                   pl.BlockSpec((N_EXPERTS, LANES), lambda i: (i, 0)),
                   pl.BlockSpec((n_groups, LANES), lambda i: (i, 0)),
                   pl.BlockSpec((n_groups, LANES), lambda i: (i, 0))],
        out_shape=[jax.ShapeDtypeStruct((N_EXPERTS, t), F32),
                   jax.ShapeDtypeStruct((N_EXPERTS, t), F32),
                   jax.ShapeDtypeStruct((n_tiles * N_EXPERTS, LANES), jnp.int32),
                   jax.ShapeDtypeStruct((n_tiles * n_groups, LANES), jnp.int32),
                   jax.ShapeDtypeStruct((n_tiles * n_groups, LANES), jnp.int32)],
        compiler_params=_params(("arbitrary",)),
        name="route_topk",
    )(logits_t, bias_tile)


def _moe_kernel(fexp_ref, fsub_ref, fdense_ref, h2_ref, gate_ref, pos_ref,
                w1_ref, w3_ref, w2_ref, o_ref, xy_ref, gb_ref, pt_ref):
    tile, eg = pl.program_id(0), pl.program_id(1)
    n_groups = pl.num_programs(1)
    tm = h2_ref.shape[0]
    n_sub = tm // MOE_SUB

    @pl.when(eg == 0)
    def _():
        o_ref[...] = jnp.zeros_like(o_ref)

    @pl.when((tile == 0) & (eg == 0))
    def _():
        xy_ref[...] = jnp.zeros_like(xy_ref)
        gb_ref[...] = jnp.zeros_like(gb_ref)

    def lane_broadcast(m):
        return jnp.broadcast_to(jnp.sum(m, axis=-1, keepdims=True), (m.shape[0], D_EXPERT))

    def hidden(x, gate_rows, j):
        return (_silu(jnp.dot(x, w1_ref[j], preferred_element_type=F32))
                * jnp.dot(x, w3_ref[j], preferred_element_type=F32) * gate_rows).astype(BF16)

    def pipelined(n, first, second):
        pending = first(0)
        for i in range(1, n):
            nxt = first(i)
            second(i - 1, pending)
            pending = nxt
        second(n - 1, pending)

    cap = MOE_CAP
    slab_rows = n_sub * cap
    go_dense = fdense_ref[tile * n_groups + eg] != 0

    def needs_second(s):
        return fsub_ref[(tile * n_groups + eg) * n_sub + s] != 0

    def slab_slice(j, q, s=None):
        start = (2 * j + q) * slab_rows
        if s is None:
            return slice(start, start + slab_rows)
        return slice(start + s * cap, start + (s + 1) * cap)

    def gather(s, q):
        cols = slice(MOE_SUB * s, MOE_SUB * (s + 1))
        slot = lax.broadcasted_iota(jnp.int32, (cap, MOE_SUB), 0).astype(F32) + float(q * cap)
        pos, gate = pos_ref[:, cols], gate_ref[:, cols]
        hot = [jnp.where(slot == pos[j:j + 1, :], 1.0, 0.0) for j in range(MOE_GROUP)]
        p = jnp.concatenate(hot, axis=0)
        pg = jnp.concatenate([hot[j] * gate[j:j + 1, :] for j in range(MOE_GROUP)], axis=0)
        xg = jnp.dot(p.astype(BF16), h2_ref[cols, :],
                     preferred_element_type=F32).astype(BF16)
        gb = lane_broadcast(pg)
        for j in range(MOE_GROUP):
            xy_ref[slab_slice(j, q, s), :] = xg[j * cap:(j + 1) * cap]
            gb_ref[slab_slice(j, q, s), :] = gb[j * cap:(j + 1) * cap]
        pt_ref[s, :, q * MOE_GROUP * cap:(q + 1) * MOE_GROUP * cap] = p.T.astype(BF16)

    def scatter(s, q):
        y = jnp.concatenate([xy_ref[slab_slice(j, q, s), :] for j in range(MOE_GROUP)], axis=0)
        o_ref[MOE_SUB * s:MOE_SUB * (s + 1), :] += jnp.dot(
            pt_ref[s, :, q * MOE_GROUP * cap:(q + 1) * MOE_GROUP * cap], y,
            preferred_element_type=F32)

    def project(j, q, act):
        xy_ref[slab_slice(j, q), :] = jnp.dot(
            act, w2_ref[j], preferred_element_type=F32).astype(BF16)

    @pl.when(jnp.logical_not(go_dense))
    def _():
        for s in range(n_sub):
            gather(s, 0)
        pipelined(MOE_GROUP,
                  lambda j: hidden(xy_ref[slab_slice(j, 0), :], gb_ref[slab_slice(j, 0), :], j),
                  lambda j, act: project(j, 0, act))
        for s in range(n_sub):
            scatter(s, 0)
        for s in range(n_sub):
            pl.when(needs_second(s))(functools.partial(gather, s, 1))
        for j in range(MOE_GROUP):
            @pl.when(fexp_ref[tile * N_EXPERTS + eg * MOE_GROUP + j] != 0)
            def _(j=j):
                project(j, 1, hidden(xy_ref[slab_slice(j, 1), :],
                                     gb_ref[slab_slice(j, 1), :], j))
        for s in range(n_sub):
            pl.when(needs_second(s))(functools.partial(scatter, s, 1))

    @pl.when(go_dense)
    def _():
        eye = (lax.broadcasted_iota(jnp.int32, (MOE_SUB, MOE_SUB), 0)
               == lax.broadcasted_iota(jnp.int32, (MOE_SUB, MOE_SUB), 1))
        for j in range(MOE_GROUP):
            for s in range(n_sub):
                cols = slice(MOE_SUB * s, MOE_SUB * (s + 1))
                gb_ref[cols, :] = lane_broadcast(
                    jnp.where(eye, gate_ref[j:j + 1, cols], 0.0))
            act = hidden(h2_ref[...], gb_ref[0:tm, :], j)
            o_ref[...] += jnp.dot(act, w2_ref[j], preferred_element_type=F32)


def _moe(fexp, fsub, fdense, h2, gates_t, pos_t, w1e, w3e, w2e, tm):
    t, d = h2.shape
    ne = w1e.shape[0]
    assert ne % MOE_GROUP == 0 and tm % MOE_SUB == 0
    n_sub = tm // MOE_SUB
    rows = MOE_GROUP * 2 * n_sub * MOE_CAP
    assert rows >= tm, "gate scratch is reused by the dense path"
    tok = pl.BlockSpec((tm, d), lambda i, e, *_: (i, 0))
    per_expert = pl.BlockSpec((MOE_GROUP, tm), lambda i, e, *_: (e, i))
    return pl.pallas_call(
        _moe_kernel,
        grid_spec=pltpu.PrefetchScalarGridSpec(
            num_scalar_prefetch=3,
            grid=(t // tm, ne // MOE_GROUP),
            in_specs=[tok, per_expert, per_expert,
                      pl.BlockSpec((MOE_GROUP, d, D_EXPERT), lambda i, e, *_: (e, 0, 0)),
                      pl.BlockSpec((MOE_GROUP, d, D_EXPERT), lambda i, e, *_: (e, 0, 0)),
                      pl.BlockSpec((MOE_GROUP, D_EXPERT, d), lambda i, e, *_: (e, 0, 0))],
            out_specs=tok,
            scratch_shapes=[pltpu.VMEM((rows, d), BF16),
                            pltpu.VMEM((rows, D_EXPERT), F32),
                            pltpu.VMEM((n_sub, MOE_SUB, 2 * MOE_GROUP * MOE_CAP), BF16)]),
        out_shape=jax.ShapeDtypeStruct((t, d), F32),
        compiler_params=_params(("arbitrary", "arbitrary")),
        name="moe_experts",
    )(fexp, fsub, fdense, h2, gates_t, pos_t, w1e, w3e, w2e)


def _final_kernel(base_ref, routed_ref, g2_ref, nf_ref, o_ref):
    xo = base_ref[...] + g2_ref[...] * routed_ref[...]
    o_ref[...] = (xo * lax.rsqrt(jnp.mean(xo * xo, axis=-1, keepdims=True) + RMS_EPS)
                  * nf_ref[...])


def _final(base, routed, g2, nf, tm, tiles_per_batch):
    t, d = base.shape
    tok = pl.BlockSpec((tm, d), lambda i: (i, 0))
    return pl.pallas_call(
        _final_kernel,
        grid=(t // tm,),
        in_specs=[tok, tok,
                  pl.BlockSpec((None, 1, d), lambda i: (i // tiles_per_batch, 0, 0)),
                  pl.BlockSpec((1, d), lambda i: (0, 0))],
        out_specs=tok,
        out_shape=jax.ShapeDtypeStruct((t, d), F32),
        compiler_params=_params(("arbitrary",)),
        name="residual_final_norm",
    )(base, routed, g2, nf)


def _tile(n, want):
    t = min(n, want)
    assert n % t == 0, (n, t)
    return t


def kernel(x, c, w_ada, b_ada, norm1_g, w_in, mu_shift, w0, w_decay_up, a0, w_a_up, w_g_up,
           k_k, k_a, r_k, lnx_g, lnx_b, sgu_ln_g, sgu_ln_b, w_spatial, b_spatial, w_out,
           norm2_g, w_router, e_bias, w1_e, w3_e, w2_e, w1_s, w3_s, w2_s, norm_f_g):
    b, s, d = x.shape
    t = b * s
    assert w_ada.shape[0] == 1, "single-layer block"
    assert d == 2 * D_RWKV and s % SGU_BLOCK == 0
    row = lambda vec: vec.reshape(1, -1)

    c_pad = jnp.pad(c, ((0, 8 - b), (0, 0)))
    mod = _mod(c_pad, w_ada[0], b_ada)[:b]
    sh1, sc1, g1, sh2, sc2, g2 = [m[:, None, :] for m in jnp.split(mod, 6, axis=-1)]

    ps, pb = _inproj(x, sc1, sh1, norm1_g, w_in[0].astype(BF16), mu_shift, _tile(s, 512))

    wlora = jnp.zeros((LANES, 2 * D_RWKV), F32)
    wlora = wlora.at[:D_DECAY_LORA, :D_RWKV].set(w_decay_up[0])
    wlora = wlora.at[D_DECAY_LORA:, D_RWKV:].set(w_a_up[0])
    r, wdec, kh, v, kk, bb, g, bonus = _prep(
        ps.reshape(t, N_SHIFT), w0, a0, wlora, w_g_up[0], k_k, k_a, row(r_k), _tile(t, 512))

    seq = lambda z: z.reshape(b, s, D_RWKV)
    w_c, rh_c, u0_c, y0_c, bh_c, kh_c, pl_c = _chunk(r, wdec, kh, v, kk, bb)
    y = _rwkv(seq(w_c), seq(rh_c), seq(u0_c), seq(y0_c), seq(bh_c), seq(kh_c), seq(v),
              seq(pl_c), _tile(s, 256))

    bias_tile = jnp.repeat(b_spatial[0].T, HEAD, axis=1)
    yb = _sgu(pb.reshape(t, 2 * D_SGU), sgu_ln_g, sgu_ln_b, w_spatial[0], bias_tile,
              _tile(t, 512))

    tm_mid = _tile(s, 512)
    wr_t = jnp.pad(w_router[0].T, ((0, LANES - N_EXPERTS), (0, 0)))
    wo = w_out[0].astype(BF16)
    h2, logits_t, base = _mid(
        y.reshape(t, D_RWKV), g, bonus, lnx_g, lnx_b, yb, x.reshape(t, d),
        g1, sc2, sh2, g2, norm2_g, wo[:D_RWKV], wo[D_RWKV:], wr_t,
        w1_s[0].astype(BF16), w3_s[0].astype(BF16), w2_s[0].astype(BF16), tm_mid, s // tm_mid)

    tm_moe = _tile(t, 1024)
    gates_t, pos_t, fexp, fsub, fdense = _route(
        logits_t, jnp.broadcast_to(e_bias[0][:, None], (N_EXPERTS, tm_moe)), tm_moe)

    routed = _moe(fexp[:, 0], fsub[:, :tm_moe // MOE_SUB].reshape(-1), fdense[:, 0],
                  h2, gates_t, pos_t, w1_e[0].astype(BF16), w3_e[0].astype(BF16),
                  w2_e[0].astype(BF16), tm_moe)
    tm_fin = _tile(s, 512)
    out = _final(base, routed, g2, row(norm_f_g), tm_fin, s // tm_fin)
    return out.reshape(b, s, d)
```

```python
import functools

import jax
import jax.numpy as jnp
from jax import lax
from jax.experimental import pallas as pl
from jax.experimental.pallas import tpu as pltpu

F32 = jnp.float32
BF16 = jnp.bfloat16

HEAD = 64
LANES = 128
SUBLANES = 8
D_RWKV = 512
D_SGU = 512
N_RWKV_PAIRS = D_RWKV // LANES
D_DECAY_LORA = 64
D_AAA_LORA = 64
D_GATE_LORA = 128
N_SHIFT = 3 * D_RWKV + D_DECAY_LORA + D_AAA_LORA + D_GATE_LORA
SGU_BLOCK = 128
CHUNK = 64
N_EXPERTS = 64
N_GROUPS = 8
GROUP_SIZE = N_EXPERTS // N_GROUPS
TOPK_GROUPS = 4
TOP_K = 8
D_EXPERT = 256
ROUTED_SCALE = 2.5
RMS_EPS = 1e-6
LN_EPS = 1e-5
LN_X_EPS = 64e-5
VMEM_LIMIT_BYTES = 56 * 1024 * 1024


def _params(semantics):
    return pltpu.CompilerParams(dimension_semantics=semantics,
                                vmem_limit_bytes=VMEM_LIMIT_BYTES)


def _split3(x):
    hi = x.astype(BF16)
    r1 = x - hi.astype(F32)
    mid = r1.astype(BF16)
    lo = (r1 - mid.astype(F32)).astype(BF16)
    return hi, mid, lo


def _dot_exact_rhs(x, rhs_bf16):
    return jnp.dot(jnp.concatenate(_split3(x), axis=1),
                   jnp.concatenate([rhs_bf16] * 3, axis=0), preferred_element_type=F32)


def _dot3(a, b, dims=(((1,), (0,)), ((), ()))):
    ((ca,), (cb,)), _ = dims
    a_hi = a.astype(BF16)
    a_lo = (a - a_hi.astype(F32)).astype(BF16)
    b_hi = b.astype(BF16)
    b_lo = (b - b_hi.astype(F32)).astype(BF16)
    dg = functools.partial(lax.dot_general, dimension_numbers=dims,
                           preferred_element_type=F32)
    return (dg(jnp.concatenate([a_hi, a_lo], axis=ca), jnp.concatenate([b_hi, b_hi], axis=cb))
            + dg(a_hi, b_lo))


def _head_ones():
    r = lax.broadcasted_iota(jnp.int32, (LANES, LANES), 0) // HEAD
    c = lax.broadcasted_iota(jnp.int32, (LANES, LANES), 1) // HEAD
    return jnp.where(r == c, 1.0, 0.0).astype(BF16)


def _head_sum(z, ones):
    parts = [_dot_exact_rhs(z[:, LANES * p:LANES * (p + 1)], ones)
             for p in range(z.shape[1] // LANES)]
    return jnp.concatenate(parts, axis=-1)


def _sigmoid(x):
    return 1.0 / (1.0 + jnp.exp(-x))


def _silu(x):
    return x * _sigmoid(x)


def _mod_kernel(c_ref, w_ref, b_ref, o_ref):
    s = _silu(c_ref[...])
    o_ref[...] = _dot3(s, w_ref[...]) + b_ref[...]


def _mod(c_pad, w_ada, b_ada):
    rows, d = c_pad.shape
    n = w_ada.shape[1]
    tn = 1024
    return pl.pallas_call(
        _mod_kernel,
        grid=(n // tn,),
        in_specs=[pl.BlockSpec((rows, d), lambda j: (0, 0)),
                  pl.BlockSpec((d, tn), lambda j: (0, j)),
                  pl.BlockSpec((1, tn), lambda j: (0, j))],
        out_specs=pl.BlockSpec((rows, tn), lambda j: (0, j)),
        out_shape=jax.ShapeDtypeStruct((rows, n), F32),
        compiler_params=_params(("arbitrary",)),
        name="adaln_mod",
    )(c_pad, w_ada, b_ada)


def _inproj_kernel(x_ref, sc_ref, sh_ref, g_ref, w_ref, mu_ref, ps_ref, pb_ref, carry_ref):
    @pl.when(pl.program_id(1) == 0)
    def _():
        carry_ref[...] = jnp.zeros_like(carry_ref)

    x = x_ref[...]
    tm = x.shape[0]
    h = x * lax.rsqrt(jnp.mean(x * x, axis=-1, keepdims=True) + RMS_EPS) * g_ref[...]
    h = h * (1.0 + sc_ref[...]) + sh_ref[...]
    proj = jnp.dot(h.astype(BF16), w_ref[...], preferred_element_type=F32)
    ps = proj[:, :N_SHIFT]
    prev = pltpu.roll(ps, 1, 0)
    first = lax.broadcasted_iota(jnp.int32, ps.shape, 0) == 0
    prev = jnp.where(first, carry_ref[0:1, :], prev)
    carry_ref[0:1, :] = ps[tm - 1:tm, :]
    ps_ref[...] = ps + (prev - ps) * mu_ref[...]
    pb_ref[...] = proj[:, N_SHIFT:]


def _inproj(x, sc1, sh1, g, w_in_bf16, mu, tm):
    b, s, d = x.shape
    d_in = w_in_bf16.shape[1]
    row = lambda bi, i: (bi, 0, 0)
    return pl.pallas_call(
        _inproj_kernel,
        grid=(b, s // tm),
        in_specs=[pl.BlockSpec((None, tm, d), lambda bi, i: (bi, i, 0)),
                  pl.BlockSpec((None, 1, d), row),
                  pl.BlockSpec((None, 1, d), row),
                  pl.BlockSpec((1, d), lambda bi, i: (0, 0)),
                  pl.BlockSpec((d, d_in), lambda bi, i: (0, 0)),
                  pl.BlockSpec((1, N_SHIFT), lambda bi, i: (0, 0))],
        out_specs=[pl.BlockSpec((None, tm, N_SHIFT), lambda bi, i: (bi, i, 0)),
                   pl.BlockSpec((None, tm, d_in - N_SHIFT), lambda bi, i: (bi, i, 0))],
        out_shape=[jax.ShapeDtypeStruct((b, s, N_SHIFT), F32),
                   jax.ShapeDtypeStruct((b, s, d_in - N_SHIFT), F32)],
        scratch_shapes=[pltpu.VMEM((8, N_SHIFT), F32)],
        compiler_params=_params(("arbitrary", "arbitrary")),
        name="inproj_shift",
    )(x, sc1, sh1, g, w_in_bf16, mu)


def _token_terms(ps, w0_ref, a0_ref, wlora_ref, wg_ref, kk_ref, ka_ref, rk_ref):
    c1, c2, c3 = D_RWKV, 2 * D_RWKV, 3 * D_RWKV
    r, k, v = ps[:, :c1], ps[:, c1:c2], ps[:, c2:c3]
    xwa = ps[:, c3:c3 + LANES]
    xg = ps[:, c3 + LANES:]
    is_w = lax.broadcasted_iota(jnp.int32, xwa.shape, 1) < D_DECAY_LORA
    lora = _dot3(jnp.where(is_w, jnp.tanh(xwa), xwa), wlora_ref[...])
    wlog = -jax.nn.softplus(-(w0_ref[...] + lora[:, :c1])) - 0.5
    log_decay = -jnp.exp(wlog)
    a = _sigmoid(a0_ref[...] + lora[:, c1:])
    g = _dot3(_sigmoid(xg), wg_ref[...])
    ones = _head_ones()
    kk = k * kk_ref[...]
    kk = kk / jnp.maximum(jnp.sqrt(_head_sum(kk * kk, ones)), 1e-12)
    kh = k * (1.0 + (a - 1.0) * ka_ref[...])
    bonus = _head_sum(r * kh * rk_ref[...], ones) * v
    return r, log_decay, kh, v, kk, kk * a, g, bonus


RWKV_CHUNK = 16
CHUNK_TILE = 128

_NN = (((1,), (0,)), ((), ()))
_NT = (((1,), (1,)), ((), ()))


def _dot1(a, b, dims=_NN):
    return lax.dot_general(a.astype(BF16), b.astype(BF16), dims, preferred_element_type=F32)


def _dot_exact_lhs(lhs_bf16, x):
    return jnp.dot(jnp.concatenate([lhs_bf16] * 3, axis=1),
                   jnp.concatenate(_split3(x), axis=0), preferred_element_type=F32)


def _chunk_kernel(ps_ref, w0_ref, a0_ref, wlora_ref, wg_ref, kk_ref, ka_ref, rk_ref,
                  w_o, rh_o, u0_o, y0_o, bh_o, kh_o, pl_o, v_o, g_o, bonus_o):
    r_all, lw_all, k_all, v_all, kk_all, b_all, g_all, bonus_all = _token_terms(
        ps_ref[...], w0_ref, a0_ref, wlora_ref, wg_ref, kk_ref, ka_ref, rk_ref)
    v_o[...] = v_all
    g_o[...] = g_all
    bonus_o[...] = bonus_all
    n = CHUNK_TILE
    tpos = lax.broadcasted_iota(jnp.int32, (n, n), 0)
    spos = lax.broadcasted_iota(jnp.int32, (n, n), 1)
    same = (tpos // RWKV_CHUNK) == (spos // RWKV_CHUNK)
    incl = same & (spos <= tpos)
    strict = same & (spos < tpos)
    tri_ones = jnp.where(incl, 1.0, 0.0).astype(BF16)
    blk_ones = jnp.where(same, 1.0, 0.0).astype(BF16)
    eye = jnp.where(tpos == spos, 1.0, 0.0)
    lane_head = lax.broadcasted_iota(jnp.int32, (n, LANES), 1) // HEAD

    pairs = range(N_RWKV_PAIRS)
    slabs = [slice(LANES * p, LANES * (p + 1)) for p in pairs]
    lw, r, k, v, kk, b = ([x[:, sl] for sl in slabs]
                          for x in (lw_all, r_all, k_all, v_all, kk_all, b_all))
    cum = [_dot_exact_lhs(tri_ones, x) for x in lw]
    tot = [_dot_exact_lhs(blk_ones, x) for x in lw]
    at = [-kk[p] * jnp.exp(cum[p] - lw[p]) for p in pairs]
    rt = [r[p] * jnp.exp(cum[p]) for p in pairs]
    inv = [jnp.exp(-cum[p]) for p in pairs]
    bt = [b[p] * inv[p] for p in pairs]
    kt = [k[p] * inv[p] for p in pairs]

    chains = [(p, h) for p in pairs for h in range(LANES // HEAD)]
    mine = [lane_head == h for h in range(LANES // HEAD)]
    a_h = [jnp.where(mine[h], at[p], 0.0) for p, h in chains]
    r_h = [jnp.where(mine[h], rt[p], 0.0) for p, h in chains]
    v_h = [jnp.where(mine[h], v[p], 0.0) for p, h in chains]
    each = range(len(chains))
    n = CHUNK_TILE
    btkt = [jnp.concatenate([bt[p], kt[p]], axis=0) for p in pairs]
    a_bk = [_dot3(a_h[c], btkt[chains[c][0]], _NT) for c in each]
    a_ab = [jnp.where(strict, m[:, :n], 0.0) for m in a_bk]
    a_ak = [jnp.where(strict, m[:, n:], 0.0) for m in a_bk]
    tinv = [eye + a for a in a_ab]
    x = a_ab
    for _ in range(RWKV_CHUNK.bit_length() - 2):
        x = [_dot1(xc, xc) for xc in x]
        tinv = [tc + _dot1(tc, xc) for tc, xc in zip(tinv, x)]
    akv = [_dot3(a_ak[c], v_h[c]) for c in each]
    m_bk = [_dot1(r_h[c], btkt[chains[c][0]], _NT) for c in each]
    m_rb = [jnp.where(incl, m[:, :n], 0.0) for m in m_bk]
    m_rk = [jnp.where(incl, m[:, n:], 0.0) for m in m_bk]
    wu = [_dot3(tinv[c], jnp.concatenate([a_h[c], akv[c]], axis=1)) for c in each]
    w_h = [m[:, :LANES] for m in wu]
    u0_h = [m[:, LANES:] for m in wu]
    ry = [_dot1(m_rb[c], wu[c]) for c in each]
    rh_h = [r_h[c] + ry[c][:, :LANES] for c in each]
    y0_h = [ry[c][:, LANES:] + _dot1(m_rk[c], v_h[c]) for c in each]

    for p in pairs:
        c0, c1 = 2 * p, 2 * p + 1
        rem = jnp.exp(tot[p] - cum[p])
        w_o[:, slabs[p]] = w_h[c0] + w_h[c1]
        rh_o[:, slabs[p]] = rh_h[c0] + rh_h[c1]
        u0_o[:, slabs[p]] = u0_h[c0] + u0_h[c1]
        y0_o[:, slabs[p]] = y0_h[c0] + y0_h[c1]
        bh_o[:, slabs[p]] = b[p] * rem
        kh_o[:, slabs[p]] = k[p] * rem
        pl_o[:, slabs[p]] = jnp.exp(tot[p])


def _chunk(ps, w0, a0, wlora, wg, k_k, k_a, r_k):
    t = ps.shape[0]
    const = lambda shape: pl.BlockSpec(shape, lambda i: (0, 0))
    spec = pl.BlockSpec((CHUNK_TILE, D_RWKV), lambda i: (i, 0))
    out = jax.ShapeDtypeStruct((t, D_RWKV), F32)
    return pl.pallas_call(
        _chunk_kernel,
        grid=(t // CHUNK_TILE,),
        in_specs=[pl.BlockSpec((CHUNK_TILE, N_SHIFT), lambda i: (i, 0)),
                  const((1, D_RWKV)), const((1, D_RWKV)),
                  const((LANES, 2 * D_RWKV)), const((D_GATE_LORA, D_RWKV)),
                  const((1, D_RWKV)), const((1, D_RWKV)), const((1, D_RWKV))],
        out_specs=[spec] * 10,
        out_shape=[out] * 10,
        compiler_params=_params(("arbitrary",)),
        name="rwkv_chunk_prep",
    )(ps, w0, a0, wlora, wg, k_k, k_a, r_k)


def _rwkv_kernel(w_ref, rh_ref, u0_ref, y0_ref, bh_ref, kh_ref, v_ref, pl_ref,
                 y_ref, state_ref):
    @pl.when(pl.program_id(0) == 0)
    def _():
        state_ref[...] = jnp.zeros_like(state_ref)

    nb, tt, _ = w_ref.shape
    L = RWKV_CHUNK
    assert 6 * L + SUBLANES <= LANES
    row_head = lax.broadcasted_iota(jnp.int32, (LANES, LANES), 0) // HEAD
    col_head = lax.broadcasted_iota(jnp.int32, (LANES, LANES), 1) // HEAD
    same_head = row_head == col_head
    pad = jnp.zeros((LANES - 6 * L - SUBLANES, LANES), F32)
    zeros_tail = jnp.zeros((LANES - 6 * L, LANES), BF16)

    def split(x):
        hi = x.astype(BF16)
        return hi, (x - hi.astype(F32)).astype(BF16)

    def chunk(ci, carry):
        t0 = pl.multiple_of(ci * L, L)
        streams = [(bi, p) for bi in range(nb) for p in range(N_RWKV_PAIRS)]

        def rows(ref, n=L):
            return [ref[bi, pl.ds(t0, n), LANES * p:LANES * (p + 1)] for bi, p in streams]

        each = range(len(streams))
        s = [state_ref[bi * N_RWKV_PAIRS + p] for bi, p in streams]
        w, rh, u0, y0, bh, kh, v = (rows(ref) for ref in
                                   (w_ref, rh_ref, u0_ref, y0_ref, bh_ref, kh_ref, v_ref))
        p_rows = rows(pl_ref, SUBLANES)
        left, p_col = [], []
        for c in each:
            (b_hi, b_lo), (k_hi, k_lo) = split(bh[c]), split(kh[c])
            block = jnp.concatenate([x.astype(F32) for x in (b_hi, b_lo, b_hi, k_hi, k_lo, k_hi)]
                                    + [p_rows[c], pad], axis=0).T
            left.append(block.astype(BF16))
            p_col.append(jnp.broadcast_to(block[:, 6 * L:6 * L + 1], (LANES, LANES)))
        g = []
        for c in each:
            (l_hi, l_lo), (s_hi, s_lo) = split(jnp.concatenate([w[c], rh[c]], axis=0)), split(s[c])
            g.append(jnp.dot(jnp.concatenate([l_hi, l_lo], axis=1),
                             jnp.concatenate([s_hi, s_hi], axis=0), preferred_element_type=F32)
                     + jnp.dot(l_hi, s_lo, preferred_element_type=F32))
        u = [g[c][:L] + u0[c] for c in each]
        upd = []
        for c in each:
            (u_hi, u_lo), (v_hi, v_lo) = split(u[c]), split(v[c])
            right = jnp.concatenate([u_hi, u_hi, u_lo, v_hi, v_hi, v_lo, zeros_tail], axis=0)
            upd.append(jnp.dot(left[c], right, preferred_element_type=F32))
        for c, (bi, p) in enumerate(streams):
            y_ref[bi, pl.ds(t0, L), LANES * p:LANES * (p + 1)] = g[c][L:] + y0[c]
            state_ref[bi * N_RWKV_PAIRS + p] = (p_col[c] * s[c]
                                                + jnp.where(same_head, upd[c], 0.0))
        return carry

    lax.fori_loop(0, tt // L, chunk, 0)


def _rwkv(w, rh, u0, y0, bh, kh, v, p_l, tt):
    nb, s, d = w.shape
    spec = pl.BlockSpec((nb, tt, d), lambda i: (0, i, 0))
    return pl.pallas_call(
        _rwkv_kernel,
        grid=(s // tt,),
        in_specs=[spec] * 8,
        out_specs=spec,
        out_shape=jax.ShapeDtypeStruct((nb, s, d), F32),
        scratch_shapes=[pltpu.VMEM((nb * N_RWKV_PAIRS, LANES, LANES), F32)],
        compiler_params=_params(("arbitrary",)),
        name="rwkv_scan",
    )(w, rh, u0, y0, bh, kh, v, p_l)


def _gelu_tanh(x):
    return 0.5 * x * (1.0 + jnp.tanh(0.7978845608028654 * (x + 0.044715 * (x * x * x))))


def _sgu_kernel(pb_ref, g_ref, b_ref, ws_ref, bias_ref, o_ref):
    z = _gelu_tanh(pb_ref[...])
    u, v = z[:, :D_SGU], z[:, D_SGU:]
    mu = jnp.mean(v, axis=-1, keepdims=True)
    vc = v - mu
    var = jnp.mean(vc * vc, axis=-1, keepdims=True)
    vn = (vc * lax.rsqrt(var + LN_EPS) * g_ref[...] + b_ref[...]).astype(BF16)
    tpos = lax.broadcasted_iota(jnp.int32, (SGU_BLOCK, SGU_BLOCK), 0) // CHUNK
    spos = lax.broadcasted_iota(jnp.int32, (SGU_BLOCK, SGU_BLOCK), 1) // CHUNK
    causal = spos <= tpos
    first_head = lax.broadcasted_iota(jnp.int32, (SGU_BLOCK, LANES), 1) < HEAD
    for p in range(D_SGU // LANES):
        sl = slice(LANES * p, LANES * (p + 1))
        ws0 = jnp.where(causal, ws_ref[2 * p], 0.0).astype(BF16)
        ws1 = jnp.where(causal, ws_ref[2 * p + 1], 0.0).astype(BF16)
        for n in range(pb_ref.shape[0] // SGU_BLOCK):
            rows = slice(SGU_BLOCK * n, SGU_BLOCK * (n + 1))
            vb = vn[rows, sl]
            sp = jnp.where(first_head,
                           jnp.dot(ws0, vb, preferred_element_type=F32),
                           jnp.dot(ws1, vb, preferred_element_type=F32))
            o_ref[rows, sl] = (u[rows, sl] * (sp + bias_ref[:, sl])).astype(o_ref.dtype)


def _sgu(pb, ln_g, ln_b, w_spatial, bias_tile, tm):
    t = pb.shape[0]
    nh = w_spatial.shape[0]
    return pl.pallas_call(
        _sgu_kernel,
        grid=(t // tm,),
        in_specs=[pl.BlockSpec((tm, 2 * D_SGU), lambda i: (i, 0)),
                  pl.BlockSpec((1, D_SGU), lambda i: (0, 0)),
                  pl.BlockSpec((1, D_SGU), lambda i: (0, 0)),
                  pl.BlockSpec((nh, SGU_BLOCK, SGU_BLOCK), lambda i: (0, 0, 0)),
                  pl.BlockSpec((SGU_BLOCK, D_SGU), lambda i: (0, 0))],
        out_specs=pl.BlockSpec((tm, D_SGU), lambda i: (i, 0)),
        out_shape=jax.ShapeDtypeStruct((t, D_SGU), BF16),
        compiler_params=_params(("arbitrary",)),
        name="sgu_mix",
    )(pb, ln_g, ln_b, w_spatial, bias_tile)


def _mid_kernel(y_ref, g_ref, bonus_ref, lng_ref, lnb_ref, yb_ref, x_ref,
                g1_ref, sc2_ref, sh2_ref, g2_ref, n2_ref,
                woa_ref, wob_ref, wrt_ref, w1_ref, w3_ref, w2_ref,
                h2_ref, logit_ref, base_ref):
    ones = _head_ones()
    y = y_ref[...]
    mu = _head_sum(y, ones) * (1.0 / HEAD)
    yc = y - mu
    var = _head_sum(yc * yc, ones) * (1.0 / HEAD)
    ya = yc * lax.rsqrt(var + LN_X_EPS) * lng_ref[...] + lnb_ref[...]
    ya = ((ya + bonus_ref[...]) * g_ref[...]).astype(BF16)
    yo = (jnp.dot(ya, woa_ref[...], preferred_element_type=F32)
          + jnp.dot(yb_ref[...], wob_ref[...], preferred_element_type=F32))
    x1 = x_ref[...] + g1_ref[...] * yo
    h2 = x1 * lax.rsqrt(jnp.mean(x1 * x1, axis=-1, keepdims=True) + RMS_EPS) * n2_ref[...]
    h2 = h2 * (1.0 + sc2_ref[...]) + sh2_ref[...]
    h2b = h2.astype(BF16)
    h2_ref[...] = h2b
    logit_ref[...] = _dot3(wrt_ref[...], h2, dims=(((1,), (1,)), ((), ())))
    act = (_silu(jnp.dot(h2b, w1_ref[...], preferred_element_type=F32))
           * jnp.dot(h2b, w3_ref[...], preferred_element_type=F32)).astype(BF16)
    ms = jnp.dot(act, w2_ref[...], preferred_element_type=F32)
    base_ref[...] = x1 + g2_ref[...] * ms


def _mid(y, g, bonus, lnx_g, lnx_b, yb, x, g1, sc2, sh2, g2, n2g,
         wo_a, wo_b, wr_t, w1s, w3s, w2s, tm, tiles_per_batch):
    t, d = x.shape
    tok = lambda w: pl.BlockSpec((tm, w), lambda i: (i, 0))
    const = lambda shape: pl.BlockSpec(shape, lambda i: (0,) * len(shape))
    mod = pl.BlockSpec((None, 1, d), lambda i: (i // tiles_per_batch, 0, 0))
    return pl.pallas_call(
        _mid_kernel,
        grid=(t // tm,),
        in_specs=[tok(D_RWKV), tok(D_RWKV), tok(D_RWKV),
                  const((1, D_RWKV)), const((1, D_RWKV)),
                  tok(D_SGU), tok(d), mod, mod, mod, mod, const((1, d)),
                  const(wo_a.shape), const(wo_b.shape), const(wr_t.shape),
                  const(w1s.shape), const(w3s.shape), const(w2s.shape)],
        out_specs=[tok(d), pl.BlockSpec((LANES, tm), lambda i: (0, i)), tok(d)],
        out_shape=[jax.ShapeDtypeStruct((t, d), BF16),
                   jax.ShapeDtypeStruct((LANES, t), F32),
                   jax.ShapeDtypeStruct((t, d), F32)],
        compiler_params=_params(("arbitrary",)),
        name="outproj_norm_router_shared",
    )(y, g, bonus, lnx_g, lnx_b, yb, x, g1, sc2, sh2, g2, n2g,
      wo_a, wo_b, wr_t, w1s, w3s, w2s)


MOE_SUB = 128
MOE_CAP = 32
MOE_GROUP = 8


def _route_kernel(logit_ref, bias_ref, gate_ref, pos_ref, fexp_ref, fsub_ref, fdense_ref):
    tt = logit_ref.shape[1]
    scores = _sigmoid(logit_ref[0:N_EXPERTS, :])
    biased = scores + bias_ref[...]
    neg_inf = -jnp.inf

    group_blocks, group_scores = [], []
    for gi in range(N_GROUPS):
        blk = biased[GROUP_SIZE * gi:GROUP_SIZE * (gi + 1), :]
        m1 = jnp.max(blk, axis=0, keepdims=True)
        is_max = blk == m1
        n_max = jnp.sum(jnp.where(is_max, 1.0, 0.0), axis=0, keepdims=True)
        m2 = jnp.max(jnp.where(is_max, neg_inf, blk), axis=0, keepdims=True)
        group_blocks.append(blk)
        group_scores.append(m1 + jnp.where(n_max >= 2.0, m1, m2))

    masked = []
    for gi in range(N_GROUPS):
        rank = jnp.zeros((1, tt), F32)
        for gj in range(N_GROUPS):
            if gj == gi:
                continue
            ahead = (group_scores[gj] >= group_scores[gi]) if gj < gi else \
                    (group_scores[gj] > group_scores[gi])
            rank += jnp.where(ahead, 1.0, 0.0)
        masked.append(jnp.where(rank < float(TOPK_GROUPS), group_blocks[gi], neg_inf))
    sub = lax.broadcasted_iota(jnp.int32, (GROUP_SIZE, tt), 0)
    ranks = [jnp.zeros((GROUP_SIZE, tt), F32) for _ in range(N_GROUPS)]
    for ej in range(N_EXPERTS):
        gj, oj = divmod(ej, GROUP_SIZE)
        other = masked[gj][oj:oj + 1, :]
        for gi in range(N_GROUPS):
            blk = masked[gi]
            if gi > gj:
                ahead = jnp.where(other >= blk, 1.0, 0.0)
            elif gi < gj:
                ahead = jnp.where(other > blk, 1.0, 0.0)
            else:
                ahead = jnp.where(sub > oj, jnp.where(other >= blk, 1.0, 0.0),
                                  jnp.where(other > blk, 1.0, 0.0))
            ranks[gi] = ranks[gi] + ahead
    rank = jnp.concatenate(ranks, axis=0)
    chosen = jnp.where(rank < float(TOP_K), 1.0, 0.0)
    sw = chosen * scores
    gate_ref[...] = sw / jnp.sum(sw, axis=0, keepdims=True) * ROUTED_SCALE

    before = (lax.broadcasted_iota(jnp.int32, (MOE_SUB, MOE_SUB), 0)
              < lax.broadcasted_iota(jnp.int32, (MOE_SUB, MOE_SUB), 1))
    prefix_ones = jnp.where(before, 1.0, 0.0).astype(BF16)
    all_ones = jnp.ones((MOE_SUB, MOE_SUB), BF16)
    lane = lax.broadcasted_iota(jnp.int32, (N_EXPERTS, LANES), 1)
    positions, counts = [], jnp.zeros((N_EXPERTS, LANES), F32)
    for s in range(tt // MOE_SUB):
        blk = chosen[:, MOE_SUB * s:MOE_SUB * (s + 1)].astype(BF16)
        prefix = jnp.dot(blk, prefix_ones, preferred_element_type=F32)
        positions.append(jnp.where(blk > 0, prefix, -1.0))
        total = jnp.dot(blk, all_ones, preferred_element_type=F32)
        counts = counts + jnp.where(lane == s, total, 0.0)
    pos_ref[...] = jnp.concatenate(positions, axis=1)
    per_expert = jnp.max(counts, axis=1, keepdims=True)
    fexp_ref[...] = jnp.broadcast_to(jnp.where(per_expert > float(MOE_CAP), 1, 0),
                                     fexp_ref.shape).astype(jnp.int32)
    per_group = jnp.concatenate(
        [jnp.max(counts[MOE_GROUP * g:MOE_GROUP * (g + 1), :], axis=0, keepdims=True)
         for g in range(N_EXPERTS // MOE_GROUP)], axis=0)
    fsub_ref[...] = jnp.where(per_group > float(MOE_CAP), 1, 0).astype(jnp.int32)
    group_worst = jnp.max(per_group, axis=1, keepdims=True)
    fdense_ref[...] = jnp.broadcast_to(jnp.where(group_worst > float(2 * MOE_CAP), 1, 0),
                                       fdense_ref.shape).astype(jnp.int32)


def _route(logits_t, bias_tile, tt):
    t = logits_t.shape[1]
    n_tiles = t // tt
    n_groups = N_EXPERTS // MOE_GROUP
    tok = pl.BlockSpec((N_EXPERTS, tt), lambda i: (0, i))
    return pl.pallas_call(
        _route_kernel,
        grid=(n_tiles,),
        in_specs=[pl.BlockSpec((LANES, tt), lambda i: (0, i)),
                  pl.BlockSpec((N_EXPERTS, tt), lambda i: (0, 0))],
        out_specs=[tok, tok,
                   pl.BlockSpec((N_EXPERTS, LANES), lambda i: (i, 0)),
                   pl.BlockSpec((n_groups, LANES), lambda i: (i, 0)),
                   pl.BlockSpec((n_groups, LANES), lambda i: (i, 0))],
        out_shape=[jax.ShapeDtypeStruct((N_EXPERTS, t), F32),
                   jax.ShapeDtypeStruct((N_EXPERTS, t), F32),
                   jax.ShapeDtypeStruct((n_tiles * N_EXPERTS, LANES), jnp.int32),
                   jax.ShapeDtypeStruct((n_tiles * n_groups, LANES), jnp.int32),
                   jax.ShapeDtypeStruct((n_tiles * n_groups, LANES), jnp.int32)],
        compiler_params=_params(("arbitrary",)),
        name="route_topk",
    )(logits_t, bias_tile)


def _moe_kernel(fexp_ref, fsub_ref, fdense_ref, h2_ref, gate_ref, pos_ref,
                w1_ref, w3_ref, w2_ref, o_ref, xy_ref, gb_ref, pt_ref):
    tile, eg = pl.program_id(0), pl.program_id(1)
    n_groups = pl.num_programs(1)
    tm = h2_ref.shape[0]
    n_sub = tm // MOE_SUB

    @pl.when(eg == 0)
    def _():
        o_ref[...] = jnp.zeros_like(o_ref)

    @pl.when((tile == 0) & (eg == 0))
    def _():
        xy_ref[...] = jnp.zeros_like(xy_ref)
        gb_ref[...] = jnp.zeros_like(gb_ref)

    def lane_broadcast(m):
        return jnp.broadcast_to(jnp.sum(m, axis=-1, keepdims=True), (m.shape[0], D_EXPERT))

    def hidden(x, gate_rows, j):
        return (_silu(jnp.dot(x, w1_ref[j], preferred_element_type=F32))
                * jnp.dot(x, w3_ref[j], preferred_element_type=F32) * gate_rows).astype(BF16)

    def pipelined(n, first, second):
        pending = first(0)
        for i in range(1, n):
            nxt = first(i)
            second(i - 1, pending)
            pending = nxt
        second(n - 1, pending)

    cap = MOE_CAP
    slab_rows = n_sub * cap
    go_dense = fdense_ref[tile * n_groups + eg] != 0

    def needs_second(s):
        return fsub_ref[(tile * n_groups + eg) * n_sub + s] != 0

    def slab_slice(j, q, s=None):
        start = (2 * j + q) * slab_rows
        if s is None:
            return slice(start, start + slab_rows)
        return slice(start + s * cap, start + (s + 1) * cap)

    def gather(s, q):
        cols = slice(MOE_SUB * s, MOE_SUB * (s + 1))
        slot = lax.broadcasted_iota(jnp.int32, (cap, MOE_SUB), 0).astype(F32) + float(q * cap)
        pos, gate = pos_ref[:, cols], gate_ref[:, cols]
        hot = [jnp.where(slot == pos[j:j + 1, :], 1.0, 0.0) for j in range(MOE_GROUP)]
        p = jnp.concatenate(hot, axis=0)
        pg = jnp.concatenate([hot[j] * gate[j:j + 1, :] for j in range(MOE_GROUP)], axis=0)
        xg = jnp.dot(p.astype(BF16), h2_ref[cols, :],
                     preferred_element_type=F32).astype(BF16)
        gb = lane_broadcast(pg)
        for j in range(MOE_GROUP):
            xy_ref[slab_slice(j, q, s), :] = xg[j * cap:(j + 1) * cap]
            gb_ref[slab_slice(j, q, s), :] = gb[j * cap:(j + 1) * cap]
        pt_ref[s, :, q * MOE_GROUP * cap:(q + 1) * MOE_GROUP * cap] = p.T.astype(BF16)

    def scatter(s, q):
        y = jnp.concatenate([xy_ref[slab_slice(j, q, s), :] for j in range(MOE_GROUP)], axis=0)
        o_ref[MOE_SUB * s:MOE_SUB * (s + 1), :] += jnp.dot(
            pt_ref[s, :, q * MOE_GROUP * cap:(q + 1) * MOE_GROUP * cap], y,
            preferred_element_type=F32)

    def project(j, q, act):
        xy_ref[slab_slice(j, q), :] = jnp.dot(
            act, w2_ref[j], preferred_element_type=F32).astype(BF16)

    @pl.when(jnp.logical_not(go_dense))
    def _():
        for s in range(n_sub):
            gather(s, 0)
        pipelined(MOE_GROUP,
                  lambda j: hidden(xy_ref[slab_slice(j, 0), :], gb_ref[slab_slice(j, 0), :], j),
                  lambda j, act: project(j, 0, act))
        for s in range(n_sub):
            scatter(s, 0)
        for s in range(n_sub):
            pl.when(needs_second(s))(functools.partial(gather, s, 1))
        for j in range(MOE_GROUP):
            @pl.when(fexp_ref[tile * N_EXPERTS + eg * MOE_GROUP + j] != 0)
            def _(j=j):
                project(j, 1, hidden(xy_ref[slab_slice(j, 1), :],
                                     gb_ref[slab_slice(j, 1), :], j))
        for s in range(n_sub):
            pl.when(needs_second(s))(functools.partial(scatter, s, 1))

    @pl.when(go_dense)
    def _():
        eye = (lax.broadcasted_iota(jnp.int32, (MOE_SUB, MOE_SUB), 0)
               == lax.broadcasted_iota(jnp.int32, (MOE_SUB, MOE_SUB), 1))
        for j in range(MOE_GROUP):
            for s in range(n_sub):
                cols = slice(MOE_SUB * s, MOE_SUB * (s + 1))
                gb_ref[cols, :] = lane_broadcast(
                    jnp.where(eye, gate_ref[j:j + 1, cols], 0.0))
            act = hidden(h2_ref[...], gb_ref[0:tm, :], j)
            o_ref[...] += jnp.dot(act, w2_ref[j], preferred_element_type=F32)


def _moe(fexp, fsub, fdense, h2, gates_t, pos_t, w1e, w3e, w2e, tm):
    t, d = h2.shape
    ne = w1e.shape[0]
    assert ne % MOE_GROUP == 0 and tm % MOE_SUB == 0
    n_sub = tm // MOE_SUB
    rows = MOE_GROUP * 2 * n_sub * MOE_CAP
    assert rows >= tm, "gate scratch is reused by the dense path"
    tok = pl.BlockSpec((tm, d), lambda i, e, *_: (i, 0))
    per_expert = pl.BlockSpec((MOE_GROUP, tm), lambda i, e, *_: (e, i))
    return pl.pallas_call(
        _moe_kernel,
        grid_spec=pltpu.PrefetchScalarGridSpec(
            num_scalar_prefetch=3,
            grid=(t // tm, ne // MOE_GROUP),
            in_specs=[tok, per_expert, per_expert,
                      pl.BlockSpec((MOE_GROUP, d, D_EXPERT), lambda i, e, *_: (e, 0, 0)),
                      pl.BlockSpec((MOE_GROUP, d, D_EXPERT), lambda i, e, *_: (e, 0, 0)),
                      pl.BlockSpec((MOE_GROUP, D_EXPERT, d), lambda i, e, *_: (e, 0, 0))],
            out_specs=tok,
            scratch_shapes=[pltpu.VMEM((rows, d), BF16),
                            pltpu.VMEM((rows, D_EXPERT), F32),
                            pltpu.VMEM((n_sub, MOE_SUB, 2 * MOE_GROUP * MOE_CAP), BF16)]),
        out_shape=jax.ShapeDtypeStruct((t, d), F32),
        compiler_params=_params(("arbitrary", "arbitrary")),
        name="moe_experts",
    )(fexp, fsub, fdense, h2, gates_t, pos_t, w1e, w3e, w2e)


def _final_kernel(base_ref, routed_ref, g2_ref, nf_ref, o_ref):
    xo = base_ref[...] + g2_ref[...] * routed_ref[...]
    o_ref[...] = (xo * lax.rsqrt(jnp.mean(xo * xo, axis=-1, keepdims=True) + RMS_EPS)
                  * nf_ref[...])


def _final(base, routed, g2, nf, tm, tiles_per_batch):
    t, d = base.shape
    tok = pl.BlockSpec((tm, d), lambda i: (i, 0))
    return pl.pallas_call(
        _final_kernel,
        grid=(t // tm,),
        in_specs=[tok, tok,
                  pl.BlockSpec((None, 1, d), lambda i: (i // tiles_per_batch, 0, 0)),
                  pl.BlockSpec((1, d), lambda i: (0, 0))],
        out_specs=tok,
        out_shape=jax.ShapeDtypeStruct((t, d), F32),
        compiler_params=_params(("arbitrary",)),
        name="residual_final_norm",
    )(base, routed, g2, nf)


def _tile(n, want):
    t = min(n, want)
    assert n % t == 0, (n, t)
    return t


def kernel(x, c, w_ada, b_ada, norm1_g, w_in, mu_shift, w0, w_decay_up, a0, w_a_up, w_g_up,
           k_k, k_a, r_k, lnx_g, lnx_b, sgu_ln_g, sgu_ln_b, w_spatial, b_spatial, w_out,
           norm2_g, w_router, e_bias, w1_e, w3_e, w2_e, w1_s, w3_s, w2_s, norm_f_g):
    b, s, d = x.shape
    t = b * s
    assert w_ada.shape[0] == 1, "single-layer block"
    assert d == 2 * D_RWKV and s % SGU_BLOCK == 0
    row = lambda vec: vec.reshape(1, -1)

    c_pad = jnp.pad(c, ((0, 8 - b), (0, 0)))
    mod = _mod(c_pad, w_ada[0], b_ada)[:b]
    sh1, sc1, g1, sh2, sc2, g2 = [m[:, None, :] for m in jnp.split(mod, 6, axis=-1)]

    ps, pb = _inproj(x, sc1, sh1, norm1_g, w_in[0].astype(BF16), mu_shift, _tile(s, 512))

    wlora = jnp.zeros((LANES, 2 * D_RWKV), F32)
    wlora = wlora.at[:D_DECAY_LORA, :D_RWKV].set(w_decay_up[0])
    wlora = wlora.at[D_DECAY_LORA:, D_RWKV:].set(w_a_up[0])
    seq = lambda z: z.reshape(b, s, D_RWKV)
    w_c, rh_c, u0_c, y0_c, bh_c, kh_c, pl_c, v, g, bonus = _chunk(
        ps.reshape(t, N_SHIFT), w0, a0, wlora, w_g_up[0], k_k, k_a, row(r_k))
    y = _rwkv(seq(w_c), seq(rh_c), seq(u0_c), seq(y0_c), seq(bh_c), seq(kh_c), seq(v),
              seq(pl_c), _tile(s, 256))

    bias_tile = jnp.repeat(b_spatial[0].T, HEAD, axis=1)
    yb = _sgu(pb.reshape(t, 2 * D_SGU), sgu_ln_g, sgu_ln_b, w_spatial[0], bias_tile,
              _tile(t, 512))

    tm_mid = _tile(s, 512)
    wr_t = jnp.pad(w_router[0].T, ((0, LANES - N_EXPERTS), (0, 0)))
    wo = w_out[0].astype(BF16)
    h2, logits_t, base = _mid(
        y.reshape(t, D_RWKV), g, bonus, lnx_g, lnx_b, yb, x.reshape(t, d),
        g1, sc2, sh2, g2, norm2_g, wo[:D_RWKV], wo[D_RWKV:], wr_t,
        w1_s[0].astype(BF16), w3_s[0].astype(BF16), w2_s[0].astype(BF16), tm_mid, s // tm_mid)

    tm_moe = _tile(t, 1024)
    gates_t, pos_t, fexp, fsub, fdense = _route(
        logits_t, jnp.broadcast_to(e_bias[0][:, None], (N_EXPERTS, tm_moe)), tm_moe)

    routed = _moe(fexp[:, 0], fsub[:, :tm_moe // MOE_SUB].reshape(-1), fdense[:, 0],
                  h2, gates_t, pos_t, w1_e[0].astype(BF16), w3_e[0].astype(BF16),
                  w2_e[0].astype(BF16), tm_moe)
    tm_fin = _tile(s, 512)
    out = _final(base, routed, g2, row(norm_f_g), tm_fin, s // tm_fin)
    return out.reshape(b, s, d)
```

```python
import functools

import jax
import jax.numpy as jnp
from jax import lax
from jax.experimental import pallas as pl
from jax.experimental.pallas import tpu as pltpu

F32 = jnp.float32
BF16 = jnp.bfloat16

HEAD = 64
LANES = 128
SUBLANES = 8
D_RWKV = 512
D_SGU = 512
N_RWKV_PAIRS = D_RWKV // LANES
D_DECAY_LORA = 64
D_AAA_LORA = 64
D_GATE_LORA = 128
N_SHIFT = 3 * D_RWKV + D_DECAY_LORA + D_AAA_LORA + D_GATE_LORA
SGU_BLOCK = 128
CHUNK = 64
N_EXPERTS = 64
N_GROUPS = 8
GROUP_SIZE = N_EXPERTS // N_GROUPS
TOPK_GROUPS = 4
TOP_K = 8
D_EXPERT = 256
ROUTED_SCALE = 2.5
RMS_EPS = 1e-6
LN_EPS = 1e-5
LN_X_EPS = 64e-5
VMEM_LIMIT_BYTES = 56 * 1024 * 1024


def _params(semantics):
    return pltpu.CompilerParams(dimension_semantics=semantics,
                                vmem_limit_bytes=VMEM_LIMIT_BYTES)


def _split3(x):
    hi = x.astype(BF16)
    r1 = x - hi.astype(F32)
    mid = r1.astype(BF16)
    lo = (r1 - mid.astype(F32)).astype(BF16)
    return hi, mid, lo


def _dot3(a, b, dims=(((1,), (0,)), ((), ()))):
    ((ca,), (cb,)), _ = dims
    a_hi = a.astype(BF16)
    a_lo = (a - a_hi.astype(F32)).astype(BF16)
    b_hi = b.astype(BF16)
    b_lo = (b - b_hi.astype(F32)).astype(BF16)
    dg = functools.partial(lax.dot_general, dimension_numbers=dims,
                           preferred_element_type=F32)
    return (dg(jnp.concatenate([a_hi, a_lo], axis=ca), jnp.concatenate([b_hi, b_hi], axis=cb))
            + dg(a_hi, b_lo))


def _head_ones():
    r = lax.broadcasted_iota(jnp.int32, (LANES, LANES), 0) // HEAD
    c = lax.broadcasted_iota(jnp.int32, (LANES, LANES), 1) // HEAD
    return jnp.where(r == c, 1.0, 0.0).astype(BF16)


def _head_sum(z, ones):
    both = jnp.concatenate([ones, ones], axis=0)
    parts = []
    for p in range(z.shape[1] // LANES):
        zp = z[:, LANES * p:LANES * (p + 1)]
        hi = zp.astype(BF16)
        lo = (zp - hi.astype(F32)).astype(BF16)
        parts.append(jnp.dot(jnp.concatenate([hi, lo], axis=1), both,
                             preferred_element_type=F32))
    return jnp.concatenate(parts, axis=-1)


def _sigmoid(x):
    return 1.0 / (1.0 + jnp.exp(-x))


def _silu(x):
    return x * _sigmoid(x)


def _mod_kernel(c_ref, w_ref, b_ref, o_ref):
    s = _silu(c_ref[...])
    o_ref[...] = _dot3(s, w_ref[...]) + b_ref[...]


def _mod(c_pad, w_ada, b_ada):
    rows, d = c_pad.shape
    n = w_ada.shape[1]
    tn = 1024
    return pl.pallas_call(
        _mod_kernel,
        grid=(n // tn,),
        in_specs=[pl.BlockSpec((rows, d), lambda j: (0, 0)),
                  pl.BlockSpec((d, tn), lambda j: (0, j)),
                  pl.BlockSpec((1, tn), lambda j: (0, j))],
        out_specs=pl.BlockSpec((rows, tn), lambda j: (0, j)),
        out_shape=jax.ShapeDtypeStruct((rows, n), F32),
        compiler_params=_params(("arbitrary",)),
        name="adaln_mod",
    )(c_pad, w_ada, b_ada)


def _inproj_kernel(x_ref, sc_ref, sh_ref, g_ref, w_ref, mu_ref, ps_ref, pb_ref, carry_ref):
    @pl.when(pl.program_id(1) == 0)
    def _():
        carry_ref[...] = jnp.zeros_like(carry_ref)

    x = x_ref[...]
    tm = x.shape[0]
    h = x * lax.rsqrt(jnp.mean(x * x, axis=-1, keepdims=True) + RMS_EPS) * g_ref[...]
    h = h * (1.0 + sc_ref[...]) + sh_ref[...]
    proj = jnp.dot(h.astype(BF16), w_ref[...], preferred_element_type=F32)
    ps = proj[:, :N_SHIFT]
    prev = pltpu.roll(ps, 1, 0)
    first = lax.broadcasted_iota(jnp.int32, ps.shape, 0) == 0
    prev = jnp.where(first, carry_ref[0:1, :], prev)
    carry_ref[0:1, :] = ps[tm - 1:tm, :]
    ps_ref[...] = ps + (prev - ps) * mu_ref[...]
    pb_ref[...] = proj[:, N_SHIFT:]


def _inproj(x, sc1, sh1, g, w_in_bf16, mu, tm):
    b, s, d = x.shape
    d_in = w_in_bf16.shape[1]
    row = lambda bi, i: (bi, 0, 0)
    return pl.pallas_call(
        _inproj_kernel,
        grid=(b, s // tm),
        in_specs=[pl.BlockSpec((None, tm, d), lambda bi, i: (bi, i, 0)),
                  pl.BlockSpec((None, 1, d), row),
                  pl.BlockSpec((None, 1, d), row),
                  pl.BlockSpec((1, d), lambda bi, i: (0, 0)),
                  pl.BlockSpec((d, d_in), lambda bi, i: (0, 0)),
                  pl.BlockSpec((1, N_SHIFT), lambda bi, i: (0, 0))],
        out_specs=[pl.BlockSpec((None, tm, N_SHIFT), lambda bi, i: (bi, i, 0)),
                   pl.BlockSpec((None, tm, d_in - N_SHIFT), lambda bi, i: (bi, i, 0))],
        out_shape=[jax.ShapeDtypeStruct((b, s, N_SHIFT), F32),
                   jax.ShapeDtypeStruct((b, s, d_in - N_SHIFT), F32)],
        scratch_shapes=[pltpu.VMEM((8, N_SHIFT), F32)],
        compiler_params=_params(("arbitrary", "arbitrary")),
        name="inproj_shift",
    )(x, sc1, sh1, g, w_in_bf16, mu)


def _token_terms(ps, w0_ref, a0_ref, wlora_ref, wg_ref, kk_ref, ka_ref, rk_ref):
    c1, c2, c3 = D_RWKV, 2 * D_RWKV, 3 * D_RWKV
    r, k, v = ps[:, :c1], ps[:, c1:c2], ps[:, c2:c3]
    xwa = ps[:, c3:c3 + LANES]
    xg = ps[:, c3 + LANES:]
    is_w = lax.broadcasted_iota(jnp.int32, xwa.shape, 1) < D_DECAY_LORA
    lora = _dot3(jnp.where(is_w, jnp.tanh(xwa), xwa), wlora_ref[...])
    wlog = -jax.nn.softplus(-(w0_ref[...] + lora[:, :c1])) - 0.5
    log_decay = -jnp.exp(wlog)
    a = _sigmoid(a0_ref[...] + lora[:, c1:])
    g = _dot3(_sigmoid(xg), wg_ref[...])
    ones = _head_ones()
    kk = k * kk_ref[...]
    kk = kk / jnp.maximum(jnp.sqrt(_head_sum(kk * kk, ones)), 1e-12)
    kh = k * (1.0 + (a - 1.0) * ka_ref[...])
    bonus = _head_sum(r * kh * rk_ref[...], ones) * v
    return r, log_decay, kh, v, kk, kk * a, g, bonus


RWKV_CHUNK = 32
CHUNK_TILE = 128

_NN = (((1,), (0,)), ((), ()))
_NT = (((1,), (1,)), ((), ()))


def _dot1(a, b, dims=_NN):
    return lax.dot_general(a.astype(BF16), b.astype(BF16), dims, preferred_element_type=F32)


def _dot_exact_lhs(lhs_bf16, x):
    return jnp.dot(jnp.concatenate([lhs_bf16] * 3, axis=1),
                   jnp.concatenate(_split3(x), axis=0), preferred_element_type=F32)


def _chunk_kernel(ps_ref, w0_ref, a0_ref, wlora_ref, wg_ref, kk_ref, ka_ref, rk_ref,
                  w_o, rh_o, u0_o, y0_o, bh_o, kh_o, pl_o, v_o, g_o, bonus_o):
    r_all, lw_all, k_all, v_all, kk_all, b_all, g_all, bonus_all = _token_terms(
        ps_ref[...], w0_ref, a0_ref, wlora_ref, wg_ref, kk_ref, ka_ref, rk_ref)
    v_o[...] = v_all
    g_o[...] = g_all
    bonus_o[...] = bonus_all
    n = CHUNK_TILE
    tpos = lax.broadcasted_iota(jnp.int32, (n, n), 0)
    spos = lax.broadcasted_iota(jnp.int32, (n, n), 1)
    same = (tpos // RWKV_CHUNK) == (spos // RWKV_CHUNK)
    incl = same & (spos <= tpos)
    strict = same & (spos < tpos)
    tri_ones = jnp.where(incl, 1.0, 0.0).astype(BF16)
    blk_ones = jnp.where(same, 1.0, 0.0).astype(BF16)
    eye = jnp.where(tpos == spos, 1.0, 0.0)
    lane_head = lax.broadcasted_iota(jnp.int32, (n, LANES), 1) // HEAD

    pairs = range(N_RWKV_PAIRS)
    slabs = [slice(LANES * p, LANES * (p + 1)) for p in pairs]
    lw, r, k, v, kk, b = ([x[:, sl] for sl in slabs]
                          for x in (lw_all, r_all, k_all, v_all, kk_all, b_all))
    cum = [_dot_exact_lhs(tri_ones, x) for x in lw]
    tot = [_dot_exact_lhs(blk_ones, x) for x in lw]
    at = [-kk[p] * jnp.exp(cum[p] - lw[p]) for p in pairs]
    rt = [r[p] * jnp.exp(cum[p]) for p in pairs]
    inv = [jnp.exp(-cum[p]) for p in pairs]
    bt = [b[p] * inv[p] for p in pairs]
    kt = [k[p] * inv[p] for p in pairs]

    chains = [(p, h) for p in pairs for h in range(LANES // HEAD)]
    mine = [lane_head == h for h in range(LANES // HEAD)]
    a_h = [jnp.where(mine[h], at[p], 0.0) for p, h in chains]
    r_h = [jnp.where(mine[h], rt[p], 0.0) for p, h in chains]
    v_h = [jnp.where(mine[h], v[p], 0.0) for p, h in chains]
    each = range(len(chains))
    n = CHUNK_TILE
    btkt = [jnp.concatenate([bt[p], kt[p]], axis=0) for p in pairs]
    a_bk = [_dot3(a_h[c], btkt[chains[c][0]], _NT) for c in each]
    a_ab = [jnp.where(strict, m[:, :n], 0.0) for m in a_bk]
    a_ak = [jnp.where(strict, m[:, n:], 0.0) for m in a_bk]
    tinv = [eye + a for a in a_ab]
    x = a_ab
    for _ in range(RWKV_CHUNK.bit_length() - 2):
        x = [_dot1(xc, xc) for xc in x]
        tinv = [tc + _dot1(tc, xc) for tc, xc in zip(tinv, x)]
    akv = [_dot3(a_ak[c], v_h[c]) for c in each]
    m_bk = [_dot1(r_h[c], btkt[chains[c][0]], _NT) for c in each]
    m_rb = [jnp.where(incl, m[:, :n], 0.0) for m in m_bk]
    m_rk = [jnp.where(incl, m[:, n:], 0.0) for m in m_bk]
    wu = [_dot3(tinv[c], jnp.concatenate([a_h[c], akv[c]], axis=1)) for c in each]
    w_h = [m[:, :LANES] for m in wu]
    u0_h = [m[:, LANES:] for m in wu]
    ry = [_dot1(m_rb[c], wu[c]) for c in each]
    rh_h = [r_h[c] + ry[c][:, :LANES] for c in each]
    y0_h = [ry[c][:, LANES:] + _dot1(m_rk[c], v_h[c]) for c in each]

    for p in pairs:
        c0, c1 = 2 * p, 2 * p + 1
        rem = jnp.exp(tot[p] - cum[p])
        w_o[:, slabs[p]] = w_h[c0] + w_h[c1]
        rh_o[:, slabs[p]] = rh_h[c0] + rh_h[c1]
        u0_o[:, slabs[p]] = u0_h[c0] + u0_h[c1]
        y0_o[:, slabs[p]] = y0_h[c0] + y0_h[c1]
        bh_o[:, slabs[p]] = b[p] * rem
        kh_o[:, slabs[p]] = k[p] * rem
        pl_o[:, slabs[p]] = jnp.exp(tot[p])


def _chunk(ps, w0, a0, wlora, wg, k_k, k_a, r_k):
    t = ps.shape[0]
    const = lambda shape: pl.BlockSpec(shape, lambda i: (0, 0))
    spec = pl.BlockSpec((CHUNK_TILE, D_RWKV), lambda i: (i, 0))
    out = jax.ShapeDtypeStruct((t, D_RWKV), F32)
    return pl.pallas_call(
        _chunk_kernel,
        grid=(t // CHUNK_TILE,),
        in_specs=[pl.BlockSpec((CHUNK_TILE, N_SHIFT), lambda i: (i, 0)),
                  const((1, D_RWKV)), const((1, D_RWKV)),
                  const((LANES, 2 * D_RWKV)), const((D_GATE_LORA, D_RWKV)),
                  const((1, D_RWKV)), const((1, D_RWKV)), const((1, D_RWKV))],
        out_specs=[spec] * 10,
        out_shape=[out] * 10,
        compiler_params=_params(("arbitrary",)),
        name="rwkv_chunk_prep",
    )(ps, w0, a0, wlora, wg, k_k, k_a, r_k)


def _rwkv_kernel(w_ref, rh_ref, u0_ref, y0_ref, bh_ref, kh_ref, v_ref, pl_ref,
                 y_ref, state_ref):
    @pl.when(pl.program_id(0) == 0)
    def _():
        state_ref[...] = jnp.zeros_like(state_ref)

    nb, tt, _ = w_ref.shape
    L = RWKV_CHUNK
    depth = LANES * pl.cdiv(6 * L + SUBLANES, LANES)
    row_head = lax.broadcasted_iota(jnp.int32, (LANES, LANES), 0) // HEAD
    col_head = lax.broadcasted_iota(jnp.int32, (LANES, LANES), 1) // HEAD
    same_head = row_head == col_head
    pad = jnp.zeros((depth - 6 * L - SUBLANES, LANES), F32)
    zeros_tail = jnp.zeros((depth - 6 * L, LANES), BF16)

    def split(x):
        hi = x.astype(BF16)
        return hi, (x - hi.astype(F32)).astype(BF16)

    def chunk(ci, carry):
        t0 = pl.multiple_of(ci * L, L)
        streams = [(bi, p) for bi in range(nb) for p in range(N_RWKV_PAIRS)]

        def rows(ref, n=L):
            return [ref[bi, pl.ds(t0, n), LANES * p:LANES * (p + 1)] for bi, p in streams]

        each = range(len(streams))
        s = [state_ref[bi * N_RWKV_PAIRS + p] for bi, p in streams]
        w, rh, u0, y0, bh, kh, v = (rows(ref) for ref in
                                   (w_ref, rh_ref, u0_ref, y0_ref, bh_ref, kh_ref, v_ref))
        p_rows = rows(pl_ref, SUBLANES)
        left, p_col = [], []
        for c in each:
            (b_hi, b_lo), (k_hi, k_lo) = split(bh[c]), split(kh[c])
            block = jnp.concatenate([x.astype(F32) for x in (b_hi, b_lo, b_hi, k_hi, k_lo, k_hi)]
                                    + [p_rows[c], pad], axis=0).T
            left.append(block.astype(BF16))
            p_col.append(jnp.broadcast_to(block[:, 6 * L:6 * L + 1], (LANES, LANES)))
        g = []
        for c in each:
            (l_hi, l_lo), (s_hi, s_lo) = split(jnp.concatenate([w[c], rh[c]], axis=0)), split(s[c])
            g.append(jnp.dot(jnp.concatenate([l_hi, l_lo], axis=1),
                             jnp.concatenate([s_hi, s_hi], axis=0), preferred_element_type=F32)
                     + jnp.dot(l_hi, s_lo, preferred_element_type=F32))
        u = [g[c][:L] + u0[c] for c in each]
        upd = []
        for c in each:
            (u_hi, u_lo), (v_hi, v_lo) = split(u[c]), split(v[c])
            right = jnp.concatenate([u_hi, u_hi, u_lo, v_hi, v_hi, v_lo, zeros_tail], axis=0)
            upd.append(jnp.dot(left[c], right, preferred_element_type=F32))
        for c, (bi, p) in enumerate(streams):
            y_ref[bi, pl.ds(t0, L), LANES * p:LANES * (p + 1)] = g[c][L:] + y0[c]
            state_ref[bi * N_RWKV_PAIRS + p] = (p_col[c] * s[c]
                                                + jnp.where(same_head, upd[c], 0.0))
        return carry

    lax.fori_loop(0, tt // L, chunk, 0)


def _rwkv(w, rh, u0, y0, bh, kh, v, p_l, tt):
    nb, s, d = w.shape
    spec = pl.BlockSpec((nb, tt, d), lambda i: (0, i, 0))
    return pl.pallas_call(
        _rwkv_kernel,
        grid=(s // tt,),
        in_specs=[spec] * 8,
        out_specs=spec,
        out_shape=jax.ShapeDtypeStruct((nb, s, d), F32),
        scratch_shapes=[pltpu.VMEM((nb * N_RWKV_PAIRS, LANES, LANES), F32)],
        compiler_params=_params(("arbitrary",)),
        name="rwkv_scan",
    )(w, rh, u0, y0, bh, kh, v, p_l)


def _gelu_tanh(x):
    return 0.5 * x * (1.0 + jnp.tanh(0.7978845608028654 * (x + 0.044715 * (x * x * x))))


def _sgu_kernel(pb_ref, g_ref, b_ref, ws_ref, bias_ref, o_ref):
    z = _gelu_tanh(pb_ref[...])
    u, v = z[:, :D_SGU], z[:, D_SGU:]
    mu = jnp.mean(v, axis=-1, keepdims=True)
    vc = v - mu
    var = jnp.mean(vc * vc, axis=-1, keepdims=True)
    vn = (vc * lax.rsqrt(var + LN_EPS) * g_ref[...] + b_ref[...]).astype(BF16)
    tpos = lax.broadcasted_iota(jnp.int32, (SGU_BLOCK, SGU_BLOCK), 0) // CHUNK
    spos = lax.broadcasted_iota(jnp.int32, (SGU_BLOCK, SGU_BLOCK), 1) // CHUNK
    causal = spos <= tpos
    first_head = lax.broadcasted_iota(jnp.int32, (SGU_BLOCK, LANES), 1) < HEAD
    for p in range(D_SGU // LANES):
        sl = slice(LANES * p, LANES * (p + 1))
        ws0 = jnp.where(causal, ws_ref[2 * p], 0.0).astype(BF16)
        ws1 = jnp.where(causal, ws_ref[2 * p + 1], 0.0).astype(BF16)
        for n in range(pb_ref.shape[0] // SGU_BLOCK):
            rows = slice(SGU_BLOCK * n, SGU_BLOCK * (n + 1))
            vb = vn[rows, sl]
            sp = jnp.where(first_head,
                           jnp.dot(ws0, vb, preferred_element_type=F32),
                           jnp.dot(ws1, vb, preferred_element_type=F32))
            o_ref[rows, sl] = (u[rows, sl] * (sp + bias_ref[:, sl])).astype(o_ref.dtype)


def _sgu(pb, ln_g, ln_b, w_spatial, bias_tile, tm):
    t = pb.shape[0]
    nh = w_spatial.shape[0]
    return pl.pallas_call(
        _sgu_kernel,
        grid=(t // tm,),
        in_specs=[pl.BlockSpec((tm, 2 * D_SGU), lambda i: (i, 0)),
                  pl.BlockSpec((1, D_SGU), lambda i: (0, 0)),
                  pl.BlockSpec((1, D_SGU), lambda i: (0, 0)),
                  pl.BlockSpec((nh, SGU_BLOCK, SGU_BLOCK), lambda i: (0, 0, 0)),
                  pl.BlockSpec((SGU_BLOCK, D_SGU), lambda i: (0, 0))],
        out_specs=pl.BlockSpec((tm, D_SGU), lambda i: (i, 0)),
        out_shape=jax.ShapeDtypeStruct((t, D_SGU), BF16),
        compiler_params=_params(("arbitrary",)),
        name="sgu_mix",
    )(pb, ln_g, ln_b, w_spatial, bias_tile)


def _mid_kernel(y_ref, g_ref, bonus_ref, lng_ref, lnb_ref, yb_ref, x_ref,
                g1_ref, sc2_ref, sh2_ref, g2_ref, n2_ref,
                woa_ref, wob_ref, wrt_ref, w1_ref, w3_ref, w2_ref,
                h2_ref, logit_ref, base_ref):
    ones = _head_ones()
    y = y_ref[...]
    mu = _head_sum(y, ones) * (1.0 / HEAD)
    yc = y - mu
    var = _head_sum(yc * yc, ones) * (1.0 / HEAD)
    ya = yc * lax.rsqrt(var + LN_X_EPS) * lng_ref[...] + lnb_ref[...]
    ya = ((ya + bonus_ref[...]) * g_ref[...]).astype(BF16)
    yo = (jnp.dot(ya, woa_ref[...], preferred_element_type=F32)
          + jnp.dot(yb_ref[...], wob_ref[...], preferred_element_type=F32))
    x1 = x_ref[...] + g1_ref[...] * yo
    h2 = x1 * lax.rsqrt(jnp.mean(x1 * x1, axis=-1, keepdims=True) + RMS_EPS) * n2_ref[...]
    h2 = h2 * (1.0 + sc2_ref[...]) + sh2_ref[...]
    h2b = h2.astype(BF16)
    h2_ref[...] = h2b
    logit_ref[...] = _dot3(wrt_ref[...], h2, dims=(((1,), (1,)), ((), ())))
    act = (_silu(jnp.dot(h2b, w1_ref[...], preferred_element_type=F32))
           * jnp.dot(h2b, w3_ref[...], preferred_element_type=F32)).astype(BF16)
    ms = jnp.dot(act, w2_ref[...], preferred_element_type=F32)
    base_ref[...] = x1 + g2_ref[...] * ms


def _mid(y, g, bonus, lnx_g, lnx_b, yb, x, g1, sc2, sh2, g2, n2g,
         wo_a, wo_b, wr_t, w1s, w3s, w2s, tm, tiles_per_batch):
    t, d = x.shape
    tok = lambda w: pl.BlockSpec((tm, w), lambda i: (i, 0))
    const = lambda shape: pl.BlockSpec(shape, lambda i: (0,) * len(shape))
    mod = pl.BlockSpec((None, 1, d), lambda i: (i // tiles_per_batch, 0, 0))
    return pl.pallas_call(
        _mid_kernel,
        grid=(t // tm,),
        in_specs=[tok(D_RWKV), tok(D_RWKV), tok(D_RWKV),
                  const((1, D_RWKV)), const((1, D_RWKV)),
                  tok(D_SGU), tok(d), mod, mod, mod, mod, const((1, d)),
                  const(wo_a.shape), const(wo_b.shape), const(wr_t.shape),
                  const(w1s.shape), const(w3s.shape), const(w2s.shape)],
        out_specs=[tok(d), pl.BlockSpec((LANES, tm), lambda i: (0, i)), tok(d)],
        out_shape=[jax.ShapeDtypeStruct((t, d), BF16),
                   jax.ShapeDtypeStruct((LANES, t), F32),
                   jax.ShapeDtypeStruct((t, d), F32)],
        compiler_params=_params(("arbitrary",)),
        name="outproj_norm_router_shared",
    )(y, g, bonus, lnx_g, lnx_b, yb, x, g1, sc2, sh2, g2, n2g,
      wo_a, wo_b, wr_t, w1s, w3s, w2s)


MOE_SUB = 128
MOE_CAP = 32
MOE_GROUP = 8


def _route_kernel(logit_ref, bias_ref, gate_ref, pos_ref, fexp_ref, fsub_ref, fdense_ref):
    tt = logit_ref.shape[1]
    scores = _sigmoid(logit_ref[0:N_EXPERTS, :])
    biased = scores + bias_ref[...]
    neg_inf = -jnp.inf

    group_blocks, group_scores = [], []
    for gi in range(N_GROUPS):
        blk = biased[GROUP_SIZE * gi:GROUP_SIZE * (gi + 1), :]
        m1 = jnp.max(blk, axis=0, keepdims=True)
        is_max = blk == m1
        n_max = jnp.sum(jnp.where(is_max, 1.0, 0.0), axis=0, keepdims=True)
        m2 = jnp.max(jnp.where(is_max, neg_inf, blk), axis=0, keepdims=True)
        group_blocks.append(blk)
        group_scores.append(m1 + jnp.where(n_max >= 2.0, m1, m2))

    masked = []
    for gi in range(N_GROUPS):
        rank = jnp.zeros((1, tt), F32)
        for gj in range(N_GROUPS):
            if gj == gi:
                continue
            ahead = (group_scores[gj] >= group_scores[gi]) if gj < gi else \
                    (group_scores[gj] > group_scores[gi])
            rank += jnp.where(ahead, 1.0, 0.0)
        masked.append(jnp.where(rank < float(TOPK_GROUPS), group_blocks[gi], neg_inf))
    sub = lax.broadcasted_iota(jnp.int32, (GROUP_SIZE, tt), 0)
    ranks = [jnp.zeros((GROUP_SIZE, tt), F32) for _ in range(N_GROUPS)]
    for ej in range(N_EXPERTS):
        gj, oj = divmod(ej, GROUP_SIZE)
        other = masked[gj][oj:oj + 1, :]
        for gi in range(N_GROUPS):
            blk = masked[gi]
            if gi > gj:
                ahead = jnp.where(other >= blk, 1.0, 0.0)
            elif gi < gj:
                ahead = jnp.where(other > blk, 1.0, 0.0)
            else:
                ahead = jnp.where(sub > oj, jnp.where(other >= blk, 1.0, 0.0),
                                  jnp.where(other > blk, 1.0, 0.0))
            ranks[gi] = ranks[gi] + ahead
    rank = jnp.concatenate(ranks, axis=0)
    chosen = jnp.where(rank < float(TOP_K), 1.0, 0.0)
    sw = chosen * scores
    gate_ref[...] = sw / jnp.sum(sw, axis=0, keepdims=True) * ROUTED_SCALE

    before = (lax.broadcasted_iota(jnp.int32, (MOE_SUB, MOE_SUB), 0)
              < lax.broadcasted_iota(jnp.int32, (MOE_SUB, MOE_SUB), 1))
    prefix_ones = jnp.where(before, 1.0, 0.0).astype(BF16)
    all_ones = jnp.ones((MOE_SUB, MOE_SUB), BF16)
    lane = lax.broadcasted_iota(jnp.int32, (N_EXPERTS, LANES), 1)
    positions, counts = [], jnp.zeros((N_EXPERTS, LANES), F32)
    for s in range(tt // MOE_SUB):
        blk = chosen[:, MOE_SUB * s:MOE_SUB * (s + 1)].astype(BF16)
        prefix = jnp.dot(blk, prefix_ones, preferred_element_type=F32)
        positions.append(jnp.where(blk > 0, prefix, -1.0))
        total = jnp.dot(blk, all_ones, preferred_element_type=F32)
        counts = counts + jnp.where(lane == s, total, 0.0)
    pos_ref[...] = jnp.concatenate(positions, axis=1)
    per_expert = jnp.max(counts, axis=1, keepdims=True)
    fexp_ref[...] = jnp.broadcast_to(jnp.where(per_expert > float(MOE_CAP), 1, 0),
                                     fexp_ref.shape).astype(jnp.int32)
    per_group = jnp.concatenate(
        [jnp.max(counts[MOE_GROUP * g:MOE_GROUP * (g + 1), :], axis=0, keepdims=True)
         for g in range(N_EXPERTS // MOE_GROUP)], axis=0)
    fsub_ref[...] = jnp.where(per_group > float(MOE_CAP), 1, 0).astype(jnp.int32)
    group_worst = jnp.max(per_group, axis=1, keepdims=True)
    fdense_ref[...] = jnp.broadcast_to(jnp.where(group_worst > float(2 * MOE_CAP), 1, 0),
                                       fdense_ref.shape).astype(jnp.int32)


def _route(logits_t, bias_tile, tt):
    t = logits_t.shape[1]
    n_tiles = t // tt
    n_groups = N_EXPERTS // MOE_GROUP
    tok = pl.BlockSpec((N_EXPERTS, tt), lambda i: (0, i))
    return pl.pallas_call(
        _route_kernel,
        grid=(n_tiles,),
        in_specs=[pl.BlockSpec((LANES, tt), lambda i: (0, i)),
                  pl.BlockSpec((N_EXPERTS, tt), lambda i: (0, 0))],
        out_specs=[tok, tok,
                   pl.BlockSpec((N_EXPERTS, LANES), lambda i: (i, 0)),
                   pl.BlockSpec((n_groups, LANES), lambda i: (i, 0)),
                   pl.BlockSpec((n_groups, LANES), lambda i: (i, 0))],
        out_shape=[jax.ShapeDtypeStruct((N_EXPERTS, t), F32),
                   jax.ShapeDtypeStruct((N_EXPERTS, t), F32),
                   jax.ShapeDtypeStruct((n_tiles * N_EXPERTS, LANES), jnp.int32),
                   jax.ShapeDtypeStruct((n_tiles * n_groups, LANES), jnp.int32),
                   jax.ShapeDtypeStruct((n_tiles * n_groups, LANES), jnp.int32)],
        compiler_params=_params(("arbitrary",)),
        name="route_topk",
    )(logits_t, bias_tile)


def _moe_kernel(fexp_ref, fsub_ref, fdense_ref, h2_ref, gate_ref, pos_ref,
                w1_ref, w3_ref, w2_ref, o_ref, xy_ref, gb_ref, pt_ref):
    tile, eg = pl.program_id(0), pl.program_id(1)
    n_groups = pl.num_programs(1)
    tm = h2_ref.shape[0]
    n_sub = tm // MOE_SUB

    @pl.when(eg == 0)
    def _():
        o_ref[...] = jnp.zeros_like(o_ref)

    @pl.when((tile == 0) & (eg == 0))
    def _():
        xy_ref[...] = jnp.zeros_like(xy_ref)
        gb_ref[...] = jnp.zeros_like(gb_ref)

    def lane_broadcast(m):
        return jnp.broadcast_to(jnp.sum(m, axis=-1, keepdims=True), (m.shape[0], D_EXPERT))

    def hidden(x, gate_rows, j):
        return (_silu(jnp.dot(x, w1_ref[j], preferred_element_type=F32))
                * jnp.dot(x, w3_ref[j], preferred_element_type=F32) * gate_rows).astype(BF16)

    def pipelined(n, first, second):
        pending = first(0)
        for i in range(1, n):
            nxt = first(i)
            second(i - 1, pending)
            pending = nxt
        second(n - 1, pending)

    cap = MOE_CAP
    slab_rows = n_sub * cap
    go_dense = fdense_ref[tile * n_groups + eg] != 0

    def needs_second(s):
        return fsub_ref[(tile * n_groups + eg) * n_sub + s] != 0

    def slab_slice(j, q, s=None):
        start = (2 * j + q) * slab_rows
        if s is None:
            return slice(start, start + slab_rows)
        return slice(start + s * cap, start + (s + 1) * cap)

    def gather(s, q):
        cols = slice(MOE_SUB * s, MOE_SUB * (s + 1))
        slot = lax.broadcasted_iota(jnp.int32, (cap, MOE_SUB), 0).astype(F32) + float(q * cap)
        pos, gate = pos_ref[:, cols], gate_ref[:, cols]
        hot = [jnp.where(slot == pos[j:j + 1, :], 1.0, 0.0) for j in range(MOE_GROUP)]
        p = jnp.concatenate(hot, axis=0)
        pg = jnp.concatenate([hot[j] * gate[j:j + 1, :] for j in range(MOE_GROUP)], axis=0)
        xg = jnp.dot(p.astype(BF16), h2_ref[cols, :],
                     preferred_element_type=F32).astype(BF16)
        gb = lane_broadcast(pg)
        for j in range(MOE_GROUP):
            xy_ref[slab_slice(j, q, s), :] = xg[j * cap:(j + 1) * cap]
            gb_ref[slab_slice(j, q, s), :] = gb[j * cap:(j + 1) * cap]
        pt_ref[s, :, q * MOE_GROUP * cap:(q + 1) * MOE_GROUP * cap] = p.T.astype(BF16)

    def scatter(s, q):
        y = jnp.concatenate([xy_ref[slab_slice(j, q, s), :] for j in range(MOE_GROUP)], axis=0)
        o_ref[MOE_SUB * s:MOE_SUB * (s + 1), :] += jnp.dot(
            pt_ref[s, :, q * MOE_GROUP * cap:(q + 1) * MOE_GROUP * cap], y,
            preferred_element_type=F32)

    def project(j, q, act):
        xy_ref[slab_slice(j, q), :] = jnp.dot(
            act, w2_ref[j], preferred_element_type=F32).astype(BF16)

    @pl.when(jnp.logical_not(go_dense))
    def _():
        for s in range(n_sub):
            gather(s, 0)
        pipelined(MOE_GROUP,
                  lambda j: hidden(xy_ref[slab_slice(j, 0), :], gb_ref[slab_slice(j, 0), :], j),
                  lambda j, act: project(j, 0, act))
        for s in range(n_sub):
            scatter(s, 0)
        for s in range(n_sub):
            pl.when(needs_second(s))(functools.partial(gather, s, 1))
        for j in range(MOE_GROUP):
            @pl.when(fexp_ref[tile * N_EXPERTS + eg * MOE_GROUP + j] != 0)
            def _(j=j):
                project(j, 1, hidden(xy_ref[slab_slice(j, 1), :],
                                     gb_ref[slab_slice(j, 1), :], j))
        for s in range(n_sub):
            pl.when(needs_second(s))(functools.partial(scatter, s, 1))

    @pl.when(go_dense)
    def _():
        eye = (lax.broadcasted_iota(jnp.int32, (MOE_SUB, MOE_SUB), 0)
               == lax.broadcasted_iota(jnp.int32, (MOE_SUB, MOE_SUB), 1))
        for j in range(MOE_GROUP):
            for s in range(n_sub):
                cols = slice(MOE_SUB * s, MOE_SUB * (s + 1))
                gb_ref[cols, :] = lane_broadcast(
                    jnp.where(eye, gate_ref[j:j + 1, cols], 0.0))
            act = hidden(h2_ref[...], gb_ref[0:tm, :], j)
            o_ref[...] += jnp.dot(act, w2_ref[j], preferred_element_type=F32)


def _moe(fexp, fsub, fdense, h2, gates_t, pos_t, w1e, w3e, w2e, tm):
    t, d = h2.shape
    ne = w1e.shape[0]
    assert ne % MOE_GROUP == 0 and tm % MOE_SUB == 0
    n_sub = tm // MOE_SUB
    rows = MOE_GROUP * 2 * n_sub * MOE_CAP
    assert rows >= tm, "gate scratch is reused by the dense path"
    tok = pl.BlockSpec((tm, d), lambda i, e, *_: (i, 0))
    per_expert = pl.BlockSpec((MOE_GROUP, tm), lambda i, e, *_: (e, i))
    return pl.pallas_call(
        _moe_kernel,
        grid_spec=pltpu.PrefetchScalarGridSpec(
            num_scalar_prefetch=3,
            grid=(t // tm, ne // MOE_GROUP),
            in_specs=[tok, per_expert, per_expert,
                      pl.BlockSpec((MOE_GROUP, d, D_EXPERT), lambda i, e, *_: (e, 0, 0)),
                      pl.BlockSpec((MOE_GROUP, d, D_EXPERT), lambda i, e, *_: (e, 0, 0)),
                      pl.BlockSpec((MOE_GROUP, D_EXPERT, d), lambda i, e, *_: (e, 0, 0))],
            out_specs=tok,
            scratch_shapes=[pltpu.VMEM((rows, d), BF16),
                            pltpu.VMEM((rows, D_EXPERT), F32),
                            pltpu.VMEM((n_sub, MOE_SUB, 2 * MOE_GROUP * MOE_CAP), BF16)]),
        out_shape=jax.ShapeDtypeStruct((t, d), F32),
        compiler_params=_params(("arbitrary", "arbitrary")),
        name="moe_experts",
    )(fexp, fsub, fdense, h2, gates_t, pos_t, w1e, w3e, w2e)


def _final_kernel(base_ref, routed_ref, g2_ref, nf_ref, o_ref):
    xo = base_ref[...] + g2_ref[...] * routed_ref[...]
    o_ref[...] = (xo * lax.rsqrt(jnp.mean(xo * xo, axis=-1, keepdims=True) + RMS_EPS)
                  * nf_ref[...])


def _final(base, routed, g2, nf, tm, tiles_per_batch):
    t, d = base.shape
    tok = pl.BlockSpec((tm, d), lambda i: (i, 0))
    return pl.pallas_call(
        _final_kernel,
        grid=(t // tm,),
        in_specs=[tok, tok,
                  pl.BlockSpec((None, 1, d), lambda i: (i // tiles_per_batch, 0, 0)),
                  pl.BlockSpec((1, d), lambda i: (0, 0))],
        out_specs=tok,
        out_shape=jax.ShapeDtypeStruct((t, d), F32),
        compiler_params=_params(("arbitrary",)),
        name="residual_final_norm",
    )(base, routed, g2, nf)


def _tile(n, want):
    t = min(n, want)
    assert n % t == 0, (n, t)
    return t


def kernel(x, c, w_ada, b_ada, norm1_g, w_in, mu_shift, w0, w_decay_up, a0, w_a_up, w_g_up,
           k_k, k_a, r_k, lnx_g, lnx_b, sgu_ln_g, sgu_ln_b, w_spatial, b_spatial, w_out,
           norm2_g, w_router, e_bias, w1_e, w3_e, w2_e, w1_s, w3_s, w2_s, norm_f_g):
    b, s, d = x.shape
    t = b * s
    assert w_ada.shape[0] == 1, "single-layer block"
    assert d == 2 * D_RWKV and s % SGU_BLOCK == 0
    row = lambda vec: vec.reshape(1, -1)

    c_pad = jnp.pad(c, ((0, 8 - b), (0, 0)))
    mod = _mod(c_pad, w_ada[0], b_ada)[:b]
    sh1, sc1, g1, sh2, sc2, g2 = [m[:, None, :] for m in jnp.split(mod, 6, axis=-1)]

    ps, pb = _inproj(x, sc1, sh1, norm1_g, w_in[0].astype(BF16), mu_shift, _tile(s, 512))

    wlora = jnp.zeros((LANES, 2 * D_RWKV), F32)
    wlora = wlora.at[:D_DECAY_LORA, :D_RWKV].set(w_decay_up[0])
    wlora = wlora.at[D_DECAY_LORA:, D_RWKV:].set(w_a_up[0])
    seq = lambda z: z.reshape(b, s, D_RWKV)
    w_c, rh_c, u0_c, y0_c, bh_c, kh_c, pl_c, v, g, bonus = _chunk(
        ps.reshape(t, N_SHIFT), w0, a0, wlora, w_g_up[0], k_k, k_a, row(r_k))
    y = _rwkv(seq(w_c), seq(rh_c), seq(u0_c), seq(y0_c), seq(bh_c), seq(kh_c), seq(v),
              seq(pl_c), _tile(s, 256))

    bias_tile = jnp.repeat(b_spatial[0].T, HEAD, axis=1)
    yb = _sgu(pb.reshape(t, 2 * D_SGU), sgu_ln_g, sgu_ln_b, w_spatial[0], bias_tile,
              _tile(t, 512))

    tm_mid = _tile(s, 512)
    wr_t = jnp.pad(w_router[0].T, ((0, LANES - N_EXPERTS), (0, 0)))
    wo = w_out[0].astype(BF16)
    h2, logits_t, base = _mid(
        y.reshape(t, D_RWKV), g, bonus, lnx_g, lnx_b, yb, x.reshape(t, d),
        g1, sc2, sh2, g2, norm2_g, wo[:D_RWKV], wo[D_RWKV:], wr_t,
        w1_s[0].astype(BF16), w3_s[0].astype(BF16), w2_s[0].astype(BF16), tm_mid, s // tm_mid)

    tm_moe = _tile(t, 1024)
    gates_t, pos_t, fexp, fsub, fdense = _route(
        logits_t, jnp.broadcast_to(e_bias[0][:, None], (N_EXPERTS, tm_moe)), tm_moe)

    routed = _moe(fexp[:, 0], fsub[:, :tm_moe // MOE_SUB].reshape(-1), fdense[:, 0],
                  h2, gates_t, pos_t, w1_e[0].astype(BF16), w3_e[0].astype(BF16),
                  w2_e[0].astype(BF16), tm_moe)
    tm_fin = _tile(s, 512)
    out = _final(base, routed, g2, row(norm_f_g), tm_fin, s // tm_fin)
    return out.reshape(b, s, d)
```

```python
import functools

import jax
import jax.numpy as jnp
from jax import lax
from jax.experimental import pallas as pl
from jax.experimental.pallas import tpu as pltpu

F32 = jnp.float32
BF16 = jnp.bfloat16

HEAD = 64
LANES = 128
SUBLANES = 8
D_RWKV = 512
D_SGU = 512
N_RWKV_PAIRS = D_RWKV // LANES
D_DECAY_LORA = 64
D_AAA_LORA = 64
D_GATE_LORA = 128
N_SHIFT = 3 * D_RWKV + D_DECAY_LORA + D_AAA_LORA + D_GATE_LORA
SGU_BLOCK = 128
CHUNK = 64
N_EXPERTS = 64
N_GROUPS = 8
GROUP_SIZE = N_EXPERTS // N_GROUPS
TOPK_GROUPS = 4
TOP_K = 8
D_EXPERT = 256
ROUTED_SCALE = 2.5
RMS_EPS = 1e-6
LN_EPS = 1e-5
LN_X_EPS = 64e-5
VMEM_LIMIT_BYTES = 56 * 1024 * 1024

TILE_MOD_COLS = 1024
TILE_INPROJ = 512
TILE_SCAN = 256
TILE_SGU = 512
TILE_MID = 512
TILE_MOE = 1024
TILE_FINAL = 512


def _params(semantics):
    return pltpu.CompilerParams(dimension_semantics=semantics,
                                vmem_limit_bytes=VMEM_LIMIT_BYTES)


def _dot3(a, b, dims=(((1,), (0,)), ((), ()))):
    ((ca,), (cb,)), _ = dims
    a_hi = a.astype(BF16)
    a_lo = (a - a_hi.astype(F32)).astype(BF16)
    b_hi = b.astype(BF16)
    b_lo = (b - b_hi.astype(F32)).astype(BF16)
    dg = functools.partial(lax.dot_general, dimension_numbers=dims,
                           preferred_element_type=F32)
    return (dg(jnp.concatenate([a_hi, a_lo], axis=ca), jnp.concatenate([b_hi, b_hi], axis=cb))
            + dg(a_hi, b_lo))


def _head_ones():
    r = lax.broadcasted_iota(jnp.int32, (LANES, LANES), 0) // HEAD
    c = lax.broadcasted_iota(jnp.int32, (LANES, LANES), 1) // HEAD
    return jnp.where(r == c, 1.0, 0.0).astype(BF16)


def _head_sum(z, ones):
    both = jnp.concatenate([ones, ones], axis=0)
    parts = []
    for p in range(z.shape[1] // LANES):
        zp = z[:, LANES * p:LANES * (p + 1)]
        hi = zp.astype(BF16)
        lo = (zp - hi.astype(F32)).astype(BF16)
        parts.append(jnp.dot(jnp.concatenate([hi, lo], axis=1), both,
                             preferred_element_type=F32))
    return jnp.concatenate(parts, axis=-1)


def _sigmoid(x):
    return 1.0 / (1.0 + jnp.exp(-x))


def _silu(x):
    return x * _sigmoid(x)


def _mod_kernel(c_ref, w_ref, b_ref, o_ref):
    s = _silu(c_ref[...])
    o_ref[...] = _dot3(s, w_ref[...]) + b_ref[...]


def _mod(c_pad, w_ada, b_ada):
    rows, d = c_pad.shape
    n = w_ada.shape[1]
    tn = TILE_MOD_COLS
    return pl.pallas_call(
        _mod_kernel,
        grid=(n // tn,),
        in_specs=[pl.BlockSpec((rows, d), lambda j: (0, 0)),
                  pl.BlockSpec((d, tn), lambda j: (0, j)),
                  pl.BlockSpec((1, tn), lambda j: (0, j))],
        out_specs=pl.BlockSpec((rows, tn), lambda j: (0, j)),
        out_shape=jax.ShapeDtypeStruct((rows, n), F32),
        compiler_params=_params(("arbitrary",)),
        name="adaln_mod",
    )(c_pad, w_ada, b_ada)


def _inproj_kernel(x_ref, sc_ref, sh_ref, g_ref, w_ref, mu_ref, ps_ref, pb_ref, carry_ref):
    @pl.when(pl.program_id(1) == 0)
    def _():
        carry_ref[...] = jnp.zeros_like(carry_ref)

    x = x_ref[...]
    tm = x.shape[0]
    h = x * lax.rsqrt(jnp.mean(x * x, axis=-1, keepdims=True) + RMS_EPS) * g_ref[...]
    h = h * (1.0 + sc_ref[...]) + sh_ref[...]
    proj = jnp.dot(h.astype(BF16), w_ref[...], preferred_element_type=F32)
    ps = proj[:, :N_SHIFT]
    prev = pltpu.roll(ps, 1, 0)
    first = lax.broadcasted_iota(jnp.int32, ps.shape, 0) == 0
    prev = jnp.where(first, carry_ref[0:1, :], prev)
    carry_ref[0:1, :] = ps[tm - 1:tm, :]
    ps_ref[...] = ps + (prev - ps) * mu_ref[...]
    pb_ref[...] = proj[:, N_SHIFT:]


def _inproj(x, sc1, sh1, g, w_in_bf16, mu, tm):
    b, s, d = x.shape
    d_in = w_in_bf16.shape[1]
    row = lambda bi, i: (bi, 0, 0)
    return pl.pallas_call(
        _inproj_kernel,
        grid=(b, s // tm),
        in_specs=[pl.BlockSpec((None, tm, d), lambda bi, i: (bi, i, 0)),
                  pl.BlockSpec((None, 1, d), row),
                  pl.BlockSpec((None, 1, d), row),
                  pl.BlockSpec((1, d), lambda bi, i: (0, 0)),
                  pl.BlockSpec((d, d_in), lambda bi, i: (0, 0)),
                  pl.BlockSpec((1, N_SHIFT), lambda bi, i: (0, 0))],
        out_specs=[pl.BlockSpec((None, tm, N_SHIFT), lambda bi, i: (bi, i, 0)),
                   pl.BlockSpec((None, tm, d_in - N_SHIFT), lambda bi, i: (bi, i, 0))],
        out_shape=[jax.ShapeDtypeStruct((b, s, N_SHIFT), F32),
                   jax.ShapeDtypeStruct((b, s, d_in - N_SHIFT), F32)],
        scratch_shapes=[pltpu.VMEM((8, N_SHIFT), F32)],
        compiler_params=_params(("arbitrary", "arbitrary")),
        name="inproj_shift",
    )(x, sc1, sh1, g, w_in_bf16, mu)


def _token_terms(ps, w0_ref, a0_ref, wlora_ref, wg_ref, kk_ref, ka_ref, rk_ref):
    c1, c2, c3 = D_RWKV, 2 * D_RWKV, 3 * D_RWKV
    r, k, v = ps[:, :c1], ps[:, c1:c2], ps[:, c2:c3]
    xwa = ps[:, c3:c3 + LANES]
    xg = ps[:, c3 + LANES:]
    is_w = lax.broadcasted_iota(jnp.int32, xwa.shape, 1) < D_DECAY_LORA
    lora = _dot3(jnp.where(is_w, jnp.tanh(xwa), xwa), wlora_ref[...])
    wlog = -jax.nn.softplus(-(w0_ref[...] + lora[:, :c1])) - 0.5
    log_decay = -jnp.exp(wlog)
    a = _sigmoid(a0_ref[...] + lora[:, c1:])
    g = _dot3(_sigmoid(xg), wg_ref[...])
    ones = _head_ones()
    kk = k * kk_ref[...]
    kk = kk / jnp.maximum(jnp.sqrt(_head_sum(kk * kk, ones)), 1e-12)
    kh = k * (1.0 + (a - 1.0) * ka_ref[...])
    bonus = _head_sum(r * kh * rk_ref[...], ones) * v
    return r, log_decay, kh, v, kk, kk * a, g, bonus


RWKV_CHUNK = 32
CHUNK_TILE = 128
CHUNK_GROUP_PAIRS = 4

_NN = (((1,), (0,)), ((), ()))
_NT = (((1,), (1,)), ((), ()))


def _dot1(a, b, dims=_NN):
    return lax.dot_general(a.astype(BF16), b.astype(BF16), dims, preferred_element_type=F32)


def _dot_exact_lhs(lhs_bf16, x):
    hi = x.astype(BF16)
    rest = x - hi.astype(F32)
    mid = rest.astype(BF16)
    lo = (rest - mid.astype(F32)).astype(BF16)
    return jnp.dot(jnp.concatenate([lhs_bf16] * 3, axis=1),
                   jnp.concatenate([hi, mid, lo], axis=0), preferred_element_type=F32)


def _chunk_kernel(ps_ref, w0_ref, a0_ref, wlora_ref, wg_ref, kk_ref, ka_ref, rk_ref,
                  w_o, rh_o, u0_o, y0_o, bh_o, kh_o, pl_o, v_o, g_o, bonus_o):
    r_all, lw_all, k_all, v_all, kk_all, b_all, g_all, bonus_all = _token_terms(
        ps_ref[...], w0_ref, a0_ref, wlora_ref, wg_ref, kk_ref, ka_ref, rk_ref)
    v_o[...] = v_all
    g_o[...] = g_all
    bonus_o[...] = bonus_all
    n = CHUNK_TILE
    tpos = lax.broadcasted_iota(jnp.int32, (n, n), 0)
    spos = lax.broadcasted_iota(jnp.int32, (n, n), 1)
    same = (tpos // RWKV_CHUNK) == (spos // RWKV_CHUNK)
    incl = same & (spos <= tpos)
    strict = same & (spos < tpos)
    tri_blk_ones = jnp.concatenate([jnp.where(incl, 1.0, 0.0), jnp.where(same, 1.0, 0.0)],
                                   axis=0).astype(BF16)
    eye = jnp.where(tpos == spos, 1.0, 0.0)
    lane_head = lax.broadcasted_iota(jnp.int32, (n, LANES), 1) // HEAD

    for first in range(0, N_RWKV_PAIRS, CHUNK_GROUP_PAIRS):
        pairs = range(CHUNK_GROUP_PAIRS)
        slabs = [slice(LANES * (first + p), LANES * (first + p + 1)) for p in pairs]
        lw, r, k, v, kk, b = ([x[:, sl] for sl in slabs]
                              for x in (lw_all, r_all, k_all, v_all, kk_all, b_all))
        sums = [_dot_exact_lhs(tri_blk_ones, x) for x in lw]
        cum = [m[:n] for m in sums]
        tot = [m[n:] for m in sums]
        at = [-kk[p] * jnp.exp(cum[p] - lw[p]) for p in pairs]
        rt = [r[p] * jnp.exp(cum[p]) for p in pairs]
        inv = [jnp.exp(-cum[p]) for p in pairs]
        bt = [b[p] * inv[p] for p in pairs]
        kt = [k[p] * inv[p] for p in pairs]

        chains = [(p, h) for p in pairs for h in range(LANES // HEAD)]
        mine = [lane_head == h for h in range(LANES // HEAD)]
        a_h = [jnp.where(mine[h], at[p], 0.0) for p, h in chains]
        r_h = [jnp.where(mine[h], rt[p], 0.0) for p, h in chains]
        v_h = [jnp.where(mine[h], v[p], 0.0) for p, h in chains]
        each = range(len(chains))
        n = CHUNK_TILE
        btkt = [jnp.concatenate([bt[p], kt[p]], axis=0) for p in pairs]
        a_bk = [_dot3(a_h[c], btkt[chains[c][0]], _NT) for c in each]
        a_ab = [jnp.where(strict, m[:, :n], 0.0) for m in a_bk]
        a_ak = [jnp.where(strict, m[:, n:], 0.0) for m in a_bk]
        tinv = [eye + a for a in a_ab]
        x = a_ab
        for _ in range(RWKV_CHUNK.bit_length() - 2):
            x = [_dot1(xc, xc) for xc in x]
            tinv = [tc + _dot1(tc, xc) for tc, xc in zip(tinv, x)]
        akv = [_dot3(a_ak[c], v_h[c]) for c in each]
        m_bk = [_dot1(r_h[c], btkt[chains[c][0]], _NT) for c in each]
        m_rb = [jnp.where(incl, m[:, :n], 0.0) for m in m_bk]
        m_rk = [jnp.where(incl, m[:, n:], 0.0) for m in m_bk]
        wu = [_dot3(tinv[c], jnp.concatenate([a_h[c], akv[c]], axis=1)) for c in each]
        w_h = [m[:, :LANES] for m in wu]
        u0_h = [m[:, LANES:] for m in wu]
        ry = [_dot1(m_rb[c], wu[c]) for c in each]
        rh_h = [r_h[c] + ry[c][:, :LANES] for c in each]
        y0_h = [ry[c][:, LANES:] + _dot1(m_rk[c], v_h[c]) for c in each]

        for p in pairs:
            c0, c1 = 2 * p, 2 * p + 1
            rem = jnp.exp(tot[p] - cum[p])
            w_o[:, slabs[p]] = w_h[c0] + w_h[c1]
            rh_o[:, slabs[p]] = rh_h[c0] + rh_h[c1]
            u0_o[:, slabs[p]] = u0_h[c0] + u0_h[c1]
            y0_o[:, slabs[p]] = y0_h[c0] + y0_h[c1]
            bh_o[:, slabs[p]] = b[p] * rem
            kh_o[:, slabs[p]] = k[p] * rem
            pl_o[:, slabs[p]] = jnp.exp(tot[p])


def _chunk(ps, w0, a0, wlora, wg, k_k, k_a, r_k):
    t = ps.shape[0]
    const = lambda shape: pl.BlockSpec(shape, lambda i: (0, 0))
    spec = pl.BlockSpec((CHUNK_TILE, D_RWKV), lambda i: (i, 0))
    out = jax.ShapeDtypeStruct((t, D_RWKV), F32)
    return pl.pallas_call(
        _chunk_kernel,
        grid=(t // CHUNK_TILE,),
        in_specs=[pl.BlockSpec((CHUNK_TILE, N_SHIFT), lambda i: (i, 0)),
                  const((1, D_RWKV)), const((1, D_RWKV)),
                  const((LANES, 2 * D_RWKV)), const((D_GATE_LORA, D_RWKV)),
                  const((1, D_RWKV)), const((1, D_RWKV)), const((1, D_RWKV))],
        out_specs=[spec] * 10,
        out_shape=[out] * 10,
        compiler_params=_params(("arbitrary",)),
        name="rwkv_chunk_prep",
    )(ps, w0, a0, wlora, wg, k_k, k_a, r_k)


def _rwkv_kernel(w_ref, rh_ref, u0_ref, y0_ref, bh_ref, kh_ref, v_ref, pl_ref,
                 y_ref, state_ref):
    @pl.when(pl.program_id(0) == 0)
    def _():
        state_ref[...] = jnp.zeros_like(state_ref)

    nb, tt, _ = w_ref.shape
    L = RWKV_CHUNK
    depth = LANES * pl.cdiv(6 * L + SUBLANES, LANES)
    row_head = lax.broadcasted_iota(jnp.int32, (LANES, LANES), 0) // HEAD
    col_head = lax.broadcasted_iota(jnp.int32, (LANES, LANES), 1) // HEAD
    same_head = row_head == col_head
    pad = jnp.zeros((depth - 6 * L - SUBLANES, LANES), F32)
    zeros_tail = jnp.zeros((depth - 6 * L, LANES), BF16)

    def split(x):
        hi = x.astype(BF16)
        return hi, (x - hi.astype(F32)).astype(BF16)

    def chunk(ci, carry):
        t0 = pl.multiple_of(ci * L, L)
        streams = [(bi, p) for bi in range(nb) for p in range(N_RWKV_PAIRS)]

        def rows(ref, n=L):
            return [ref[bi, pl.ds(t0, n), LANES * p:LANES * (p + 1)] for bi, p in streams]

        each = range(len(streams))
        s = [state_ref[bi * N_RWKV_PAIRS + p] for bi, p in streams]
        w, rh, u0, y0, bh, kh, v = (rows(ref) for ref in
                                   (w_ref, rh_ref, u0_ref, y0_ref, bh_ref, kh_ref, v_ref))
        p_rows = rows(pl_ref, SUBLANES)
        left, p_col = [], []
        for c in each:
            (b_hi, b_lo), (k_hi, k_lo) = split(bh[c]), split(kh[c])
            block = jnp.concatenate([x.astype(F32) for x in (b_hi, b_lo, b_hi, k_hi, k_lo, k_hi)]
                                    + [p_rows[c], pad], axis=0).T
            left.append(block.astype(BF16))
            p_col.append(jnp.broadcast_to(block[:, 6 * L:6 * L + 1], (LANES, LANES)))
        g = []
        for c in each:
            (l_hi, l_lo), (s_hi, s_lo) = split(jnp.concatenate([w[c], rh[c]], axis=0)), split(s[c])
            g.append(jnp.dot(jnp.concatenate([l_hi, l_lo], axis=1),
                             jnp.concatenate([s_hi, s_hi], axis=0), preferred_element_type=F32)
                     + jnp.dot(l_hi, s_lo, preferred_element_type=F32))
        u = [g[c][:L] + u0[c] for c in each]
        upd = []
        for c in each:
            (u_hi, u_lo), (v_hi, v_lo) = split(u[c]), split(v[c])
            right = jnp.concatenate([u_hi, u_hi, u_lo, v_hi, v_hi, v_lo, zeros_tail], axis=0)
            upd.append(jnp.dot(left[c], right, preferred_element_type=F32))
        for c, (bi, p) in enumerate(streams):
            y_ref[bi, pl.ds(t0, L), LANES * p:LANES * (p + 1)] = g[c][L:] + y0[c]
            state_ref[bi * N_RWKV_PAIRS + p] = (p_col[c] * s[c]
                                                + jnp.where(same_head, upd[c], 0.0))
        return carry

    lax.fori_loop(0, tt // L, chunk, 0)


def _rwkv(w, rh, u0, y0, bh, kh, v, p_l, tt):
    nb, s, d = w.shape
    spec = pl.BlockSpec((nb, tt, d), lambda i: (0, i, 0))
    return pl.pallas_call(
        _rwkv_kernel,
        grid=(s // tt,),
        in_specs=[spec] * 8,
        out_specs=spec,
        out_shape=jax.ShapeDtypeStruct((nb, s, d), F32),
        scratch_shapes=[pltpu.VMEM((nb * N_RWKV_PAIRS, LANES, LANES), F32)],
        compiler_params=_params(("arbitrary",)),
        name="rwkv_scan",
    )(w, rh, u0, y0, bh, kh, v, p_l)


def _gelu_tanh(x):
    return 0.5 * x * (1.0 + jnp.tanh(0.7978845608028654 * (x + 0.044715 * (x * x * x))))


def _sgu_kernel(pb_ref, g_ref, b_ref, ws_ref, bias_ref, o_ref):
    z = _gelu_tanh(pb_ref[...])
    u, v = z[:, :D_SGU], z[:, D_SGU:]
    mu = jnp.mean(v, axis=-1, keepdims=True)
    vc = v - mu
    var = jnp.mean(vc * vc, axis=-1, keepdims=True)
    vn = (vc * lax.rsqrt(var + LN_EPS) * g_ref[...] + b_ref[...]).astype(BF16)
    tpos = lax.broadcasted_iota(jnp.int32, (SGU_BLOCK, SGU_BLOCK), 0) // CHUNK
    spos = lax.broadcasted_iota(jnp.int32, (SGU_BLOCK, SGU_BLOCK), 1) // CHUNK
    causal = spos <= tpos
    first_head = lax.broadcasted_iota(jnp.int32, (SGU_BLOCK, LANES), 1) < HEAD
    for p in range(D_SGU // LANES):
        sl = slice(LANES * p, LANES * (p + 1))
        ws0 = jnp.where(causal, ws_ref[2 * p], 0.0).astype(BF16)
        ws1 = jnp.where(causal, ws_ref[2 * p + 1], 0.0).astype(BF16)
        for n in range(pb_ref.shape[0] // SGU_BLOCK):
            rows = slice(SGU_BLOCK * n, SGU_BLOCK * (n + 1))
            vb = vn[rows, sl]
            sp = jnp.where(first_head,
                           jnp.dot(ws0, vb, preferred_element_type=F32),
                           jnp.dot(ws1, vb, preferred_element_type=F32))
            o_ref[rows, sl] = (u[rows, sl] * (sp + bias_ref[:, sl])).astype(o_ref.dtype)


def _sgu(pb, ln_g, ln_b, w_spatial, bias_tile, tm):
    t = pb.shape[0]
    nh = w_spatial.shape[0]
    return pl.pallas_call(
        _sgu_kernel,
        grid=(t // tm,),
        in_specs=[pl.BlockSpec((tm, 2 * D_SGU), lambda i: (i, 0)),
                  pl.BlockSpec((1, D_SGU), lambda i: (0, 0)),
                  pl.BlockSpec((1, D_SGU), lambda i: (0, 0)),
                  pl.BlockSpec((nh, SGU_BLOCK, SGU_BLOCK), lambda i: (0, 0, 0)),
                  pl.BlockSpec((SGU_BLOCK, D_SGU), lambda i: (0, 0))],
        out_specs=pl.BlockSpec((tm, D_SGU), lambda i: (i, 0)),
        out_shape=jax.ShapeDtypeStruct((t, D_SGU), BF16),
        compiler_params=_params(("arbitrary",)),
        name="sgu_mix",
    )(pb, ln_g, ln_b, w_spatial, bias_tile)


def _mid_kernel(y_ref, g_ref, bonus_ref, lng_ref, lnb_ref, yb_ref, x_ref,
                g1_ref, sc2_ref, sh2_ref, g2_ref, n2_ref,
                woa_ref, wob_ref, wrt_ref, w1_ref, w3_ref, w2_ref,
                h2_ref, logit_ref, base_ref):
    ones = _head_ones()
    y = y_ref[...]
    mu = _head_sum(y, ones) * (1.0 / HEAD)
    yc = y - mu
    var = _head_sum(yc * yc, ones) * (1.0 / HEAD)
    ya = yc * lax.rsqrt(var + LN_X_EPS) * lng_ref[...] + lnb_ref[...]
    ya = ((ya + bonus_ref[...]) * g_ref[...]).astype(BF16)
    yo = (jnp.dot(ya, woa_ref[...], preferred_element_type=F32)
          + jnp.dot(yb_ref[...], wob_ref[...], preferred_element_type=F32))
    x1 = x_ref[...] + g1_ref[...] * yo
    h2 = x1 * lax.rsqrt(jnp.mean(x1 * x1, axis=-1, keepdims=True) + RMS_EPS) * n2_ref[...]
    h2 = h2 * (1.0 + sc2_ref[...]) + sh2_ref[...]
    h2b = h2.astype(BF16)
    h2_ref[...] = h2b
    logit_ref[...] = _dot3(wrt_ref[...], h2, dims=(((1,), (1,)), ((), ())))
    act = (_silu(jnp.dot(h2b, w1_ref[...], preferred_element_type=F32))
           * jnp.dot(h2b, w3_ref[...], preferred_element_type=F32)).astype(BF16)
    ms = jnp.dot(act, w2_ref[...], preferred_element_type=F32)
    base_ref[...] = x1 + g2_ref[...] * ms


def _mid(y, g, bonus, lnx_g, lnx_b, yb, x, g1, sc2, sh2, g2, n2g,
         wo_a, wo_b, wr_t, w1s, w3s, w2s, tm, tiles_per_batch):
    t, d = x.shape
    tok = lambda w: pl.BlockSpec((tm, w), lambda i: (i, 0))
    const = lambda shape: pl.BlockSpec(shape, lambda i: (0,) * len(shape))
    mod = pl.BlockSpec((None, 1, d), lambda i: (i // tiles_per_batch, 0, 0))
    return pl.pallas_call(
        _mid_kernel,
        grid=(t // tm,),
        in_specs=[tok(D_RWKV), tok(D_RWKV), tok(D_RWKV),
                  const((1, D_RWKV)), const((1, D_RWKV)),
                  tok(D_SGU), tok(d), mod, mod, mod, mod, const((1, d)),
                  const(wo_a.shape), const(wo_b.shape), const(wr_t.shape),
                  const(w1s.shape), const(w3s.shape), const(w2s.shape)],
        out_specs=[tok(d), pl.BlockSpec((LANES, tm), lambda i: (0, i)), tok(d)],
        out_shape=[jax.ShapeDtypeStruct((t, d), BF16),
                   jax.ShapeDtypeStruct((LANES, t), F32),
                   jax.ShapeDtypeStruct((t, d), F32)],
        compiler_params=_params(("arbitrary",)),
        name="outproj_norm_router_shared",
    )(y, g, bonus, lnx_g, lnx_b, yb, x, g1, sc2, sh2, g2, n2g,
      wo_a, wo_b, wr_t, w1s, w3s, w2s)


MOE_SUB = 128
MOE_CAP = 32
MOE_GROUP = 8


def _route_kernel(logit_ref, bias_ref, gate_ref, pos_ref, fexp_ref, fsub_ref, fdense_ref):
    tt = logit_ref.shape[1]
    scores = _sigmoid(logit_ref[0:N_EXPERTS, :])
    biased = scores + bias_ref[...]
    neg_inf = -jnp.inf

    group_blocks, group_scores = [], []
    for gi in range(N_GROUPS):
        blk = biased[GROUP_SIZE * gi:GROUP_SIZE * (gi + 1), :]
        m1 = jnp.max(blk, axis=0, keepdims=True)
        is_max = blk == m1
        n_max = jnp.sum(jnp.where(is_max, 1.0, 0.0), axis=0, keepdims=True)
        m2 = jnp.max(jnp.where(is_max, neg_inf, blk), axis=0, keepdims=True)
        group_blocks.append(blk)
        group_scores.append(m1 + jnp.where(n_max >= 2.0, m1, m2))

    masked = []
    for gi in range(N_GROUPS):
        rank = jnp.zeros((1, tt), F32)
        for gj in range(N_GROUPS):
            if gj == gi:
                continue
            ahead = (group_scores[gj] >= group_scores[gi]) if gj < gi else \
                    (group_scores[gj] > group_scores[gi])
            rank += jnp.where(ahead, 1.0, 0.0)
        masked.append(jnp.where(rank < float(TOPK_GROUPS), group_blocks[gi], neg_inf))
    sub = lax.broadcasted_iota(jnp.int32, (GROUP_SIZE, tt), 0)
    ranks = [jnp.zeros((GROUP_SIZE, tt), F32) for _ in range(N_GROUPS)]
    for ej in range(N_EXPERTS):
        gj, oj = divmod(ej, GROUP_SIZE)
        other = masked[gj][oj:oj + 1, :]
        for gi in range(N_GROUPS):
            blk = masked[gi]
            if gi > gj:
                ahead = jnp.where(other >= blk, 1.0, 0.0)
            elif gi < gj:
                ahead = jnp.where(other > blk, 1.0, 0.0)
            else:
                ahead = jnp.where(sub > oj, jnp.where(other >= blk, 1.0, 0.0),
                                  jnp.where(other > blk, 1.0, 0.0))
            ranks[gi] = ranks[gi] + ahead
    rank = jnp.concatenate(ranks, axis=0)
    chosen = jnp.where(rank < float(TOP_K), 1.0, 0.0)
    sw = chosen * scores
    gate_ref[...] = sw / jnp.sum(sw, axis=0, keepdims=True) * ROUTED_SCALE

    before = (lax.broadcasted_iota(jnp.int32, (MOE_SUB, MOE_SUB), 0)
              < lax.broadcasted_iota(jnp.int32, (MOE_SUB, MOE_SUB), 1))
    prefix_ones = jnp.where(before, 1.0, 0.0).astype(BF16)
    all_ones = jnp.ones((MOE_SUB, MOE_SUB), BF16)
    lane = lax.broadcasted_iota(jnp.int32, (N_EXPERTS, LANES), 1)
    positions, counts = [], jnp.zeros((N_EXPERTS, LANES), F32)
    for s in range(tt // MOE_SUB):
        blk = chosen[:, MOE_SUB * s:MOE_SUB * (s + 1)].astype(BF16)
        prefix = jnp.dot(blk, prefix_ones, preferred_element_type=F32)
        positions.append(jnp.where(blk > 0, prefix, -1.0))
        total = jnp.dot(blk, all_ones, preferred_element_type=F32)
        counts = counts + jnp.where(lane == s, total, 0.0)
    pos_ref[...] = jnp.concatenate(positions, axis=1)
    per_expert = jnp.max(counts, axis=1, keepdims=True)
    fexp_ref[...] = jnp.broadcast_to(jnp.where(per_expert > float(MOE_CAP), 1, 0),
                                     fexp_ref.shape).astype(jnp.int32)
    per_group = jnp.concatenate(
        [jnp.max(counts[MOE_GROUP * g:MOE_GROUP * (g + 1), :], axis=0, keepdims=True)
         for g in range(N_EXPERTS // MOE_GROUP)], axis=0)
    fsub_ref[...] = jnp.where(per_group > float(MOE_CAP), 1, 0).astype(jnp.int32)
    group_worst = jnp.max(per_group, axis=1, keepdims=True)
    fdense_ref[...] = jnp.broadcast_to(jnp.where(group_worst > float(2 * MOE_CAP), 1, 0),
                                       fdense_ref.shape).astype(jnp.int32)


def _route(logits_t, bias_tile, tt):
    t = logits_t.shape[1]
    n_tiles = t // tt
    n_groups = N_EXPERTS // MOE_GROUP
    tok = pl.BlockSpec((N_EXPERTS, tt), lambda i: (0, i))
    return pl.pallas_call(
        _route_kernel,
        grid=(n_tiles,),
        in_specs=[pl.BlockSpec((LANES, tt), lambda i: (0, i)),
                  pl.BlockSpec((N_EXPERTS, tt), lambda i: (0, 0))],
        out_specs=[tok, tok,
                   pl.BlockSpec((N_EXPERTS, LANES), lambda i: (i, 0)),
                   pl.BlockSpec((n_groups, LANES), lambda i: (i, 0)),
                   pl.BlockSpec((n_groups, LANES), lambda i: (i, 0))],
        out_shape=[jax.ShapeDtypeStruct((N_EXPERTS, t), F32),
                   jax.ShapeDtypeStruct((N_EXPERTS, t), F32),
                   jax.ShapeDtypeStruct((n_tiles * N_EXPERTS, LANES), jnp.int32),
                   jax.ShapeDtypeStruct((n_tiles * n_groups, LANES), jnp.int32),
                   jax.ShapeDtypeStruct((n_tiles * n_groups, LANES), jnp.int32)],
        compiler_params=_params(("arbitrary",)),
        name="route_topk",
    )(logits_t, bias_tile)


def _moe_kernel(fexp_ref, fsub_ref, fdense_ref, h2_ref, gate_ref, pos_ref,
                w1_ref, w3_ref, w2_ref, o_ref, xy_ref, gb_ref, pt_ref):
    tile, eg = pl.program_id(0), pl.program_id(1)
    n_groups = pl.num_programs(1)
    tm = h2_ref.shape[0]
    n_sub = tm // MOE_SUB

    @pl.when(eg == 0)
    def _():
        o_ref[...] = jnp.zeros_like(o_ref)

    @pl.when((tile == 0) & (eg == 0))
    def _():
        xy_ref[...] = jnp.zeros_like(xy_ref)
        gb_ref[...] = jnp.zeros_like(gb_ref)

    def lane_broadcast(m):
        return jnp.broadcast_to(jnp.sum(m, axis=-1, keepdims=True), (m.shape[0], D_EXPERT))

    def hidden(x, gate_rows, j):
        return (_silu(jnp.dot(x, w1_ref[j], preferred_element_type=F32))
                * jnp.dot(x, w3_ref[j], preferred_element_type=F32) * gate_rows).astype(BF16)

    def pipelined(n, first, second):
        pending = first(0)
        for i in range(1, n):
            nxt = first(i)
            second(i - 1, pending)
            pending = nxt
        second(n - 1, pending)

    cap = MOE_CAP
    slab_rows = n_sub * cap
    go_dense = fdense_ref[tile * n_groups + eg] != 0

    def needs_second(s):
        return fsub_ref[(tile * n_groups + eg) * n_sub + s] != 0

    def slab_slice(j, q, s=None):
        start = (2 * j + q) * slab_rows
        if s is None:
            return slice(start, start + slab_rows)
        return slice(start + s * cap, start + (s + 1) * cap)

    def gather(s, q):
        cols = slice(MOE_SUB * s, MOE_SUB * (s + 1))
        slot = lax.broadcasted_iota(jnp.int32, (cap, MOE_SUB), 0).astype(F32) + float(q * cap)
        pos, gate = pos_ref[:, cols], gate_ref[:, cols]
        hot = [jnp.where(slot == pos[j:j + 1, :], 1.0, 0.0) for j in range(MOE_GROUP)]
        p = jnp.concatenate(hot, axis=0)
        pg = jnp.concatenate([hot[j] * gate[j:j + 1, :] for j in range(MOE_GROUP)], axis=0)
        xg = jnp.dot(p.astype(BF16), h2_ref[cols, :],
                     preferred_element_type=F32).astype(BF16)
        gb = lane_broadcast(pg)
        for j in range(MOE_GROUP):
            xy_ref[slab_slice(j, q, s), :] = xg[j * cap:(j + 1) * cap]
            gb_ref[slab_slice(j, q, s), :] = gb[j * cap:(j + 1) * cap]
        pt_ref[s, :, q * MOE_GROUP * cap:(q + 1) * MOE_GROUP * cap] = p.T.astype(BF16)

    def scatter(s, q):
        y = jnp.concatenate([xy_ref[slab_slice(j, q, s), :] for j in range(MOE_GROUP)], axis=0)
        o_ref[MOE_SUB * s:MOE_SUB * (s + 1), :] += jnp.dot(
            pt_ref[s, :, q * MOE_GROUP * cap:(q + 1) * MOE_GROUP * cap], y,
            preferred_element_type=F32)

    def project(j, q, act):
        xy_ref[slab_slice(j, q), :] = jnp.dot(
            act, w2_ref[j], preferred_element_type=F32).astype(BF16)

    @pl.when(jnp.logical_not(go_dense))
    def _():
        for s in range(n_sub):
            gather(s, 0)
        pipelined(MOE_GROUP,
                  lambda j: hidden(xy_ref[slab_slice(j, 0), :], gb_ref[slab_slice(j, 0), :], j),
                  lambda j, act: project(j, 0, act))
        for s in range(n_sub):
            scatter(s, 0)
        for s in range(n_sub):
            pl.when(needs_second(s))(functools.partial(gather, s, 1))
        for j in range(MOE_GROUP):
            @pl.when(fexp_ref[tile * N_EXPERTS + eg * MOE_GROUP + j] != 0)
            def _(j=j):
                project(j, 1, hidden(xy_ref[slab_slice(j, 1), :],
                                     gb_ref[slab_slice(j, 1), :], j))
        for s in range(n_sub):
            pl.when(needs_second(s))(functools.partial(scatter, s, 1))

    @pl.when(go_dense)
    def _():
        eye = (lax.broadcasted_iota(jnp.int32, (MOE_SUB, MOE_SUB), 0)
               == lax.broadcasted_iota(jnp.int32, (MOE_SUB, MOE_SUB), 1))
        for j in range(MOE_GROUP):
            for s in range(n_sub):
                cols = slice(MOE_SUB * s, MOE_SUB * (s + 1))
                gb_ref[cols, :] = lane_broadcast(
                    jnp.where(eye, gate_ref[j:j + 1, cols], 0.0))
            act = hidden(h2_ref[...], gb_ref[0:tm, :], j)
            o_ref[...] += jnp.dot(act, w2_ref[j], preferred_element_type=F32)


def _moe(fexp, fsub, fdense, h2, gates_t, pos_t, w1e, w3e, w2e, tm):
    t, d = h2.shape
    ne = w1e.shape[0]
    assert ne % MOE_GROUP == 0 and tm % MOE_SUB == 0
    n_sub = tm // MOE_SUB
    rows = MOE_GROUP * 2 * n_sub * MOE_CAP
    assert rows >= tm, "gate scratch is reused by the dense path"
    tok = pl.BlockSpec((tm, d), lambda i, e, *_: (i, 0))
    per_expert = pl.BlockSpec((MOE_GROUP, tm), lambda i, e, *_: (e, i))
    return pl.pallas_call(
        _moe_kernel,
        grid_spec=pltpu.PrefetchScalarGridSpec(
            num_scalar_prefetch=3,
            grid=(t // tm, ne // MOE_GROUP),
            in_specs=[tok, per_expert, per_expert,
                      pl.BlockSpec((MOE_GROUP, d, D_EXPERT), lambda i, e, *_: (e, 0, 0)),
                      pl.BlockSpec((MOE_GROUP, d, D_EXPERT), lambda i, e, *_: (e, 0, 0)),
                      pl.BlockSpec((MOE_GROUP, D_EXPERT, d), lambda i, e, *_: (e, 0, 0))],
            out_specs=tok,
            scratch_shapes=[pltpu.VMEM((rows, d), BF16),
                            pltpu.VMEM((rows, D_EXPERT), F32),
                            pltpu.VMEM((n_sub, MOE_SUB, 2 * MOE_GROUP * MOE_CAP), BF16)]),
        out_shape=jax.ShapeDtypeStruct((t, d), F32),
        compiler_params=_params(("arbitrary", "arbitrary")),
        name="moe_experts",
    )(fexp, fsub, fdense, h2, gates_t, pos_t, w1e, w3e, w2e)


def _final_kernel(base_ref, routed_ref, g2_ref, nf_ref, o_ref):
    xo = base_ref[...] + g2_ref[...] * routed_ref[...]
    o_ref[...] = (xo * lax.rsqrt(jnp.mean(xo * xo, axis=-1, keepdims=True) + RMS_EPS)
                  * nf_ref[...])


def _final(base, routed, g2, nf, tm, tiles_per_batch):
    t, d = base.shape
    tok = pl.BlockSpec((tm, d), lambda i: (i, 0))
    return pl.pallas_call(
        _final_kernel,
        grid=(t // tm,),
        in_specs=[tok, tok,
                  pl.BlockSpec((None, 1, d), lambda i: (i // tiles_per_batch, 0, 0)),
                  pl.BlockSpec((1, d), lambda i: (0, 0))],
        out_specs=tok,
        out_shape=jax.ShapeDtypeStruct((t, d), F32),
        compiler_params=_params(("arbitrary",)),
        name="residual_final_norm",
    )(base, routed, g2, nf)


def _tile(n, want):
    t = min(n, want)
    assert n % t == 0, (n, t)
    return t


def kernel(x, c, w_ada, b_ada, norm1_g, w_in, mu_shift, w0, w_decay_up, a0, w_a_up, w_g_up,
           k_k, k_a, r_k, lnx_g, lnx_b, sgu_ln_g, sgu_ln_b, w_spatial, b_spatial, w_out,
           norm2_g, w_router, e_bias, w1_e, w3_e, w2_e, w1_s, w3_s, w2_s, norm_f_g):
    b, s, d = x.shape
    t = b * s
    assert w_ada.shape[0] == 1, "single-layer block"
    assert d == 2 * D_RWKV and s % SGU_BLOCK == 0
    row = lambda vec: vec.reshape(1, -1)

    c_pad = jnp.pad(c, ((0, 8 - b), (0, 0)))
    mod = _mod(c_pad, w_ada[0], b_ada)[:b]
    sh1, sc1, g1, sh2, sc2, g2 = [m[:, None, :] for m in jnp.split(mod, 6, axis=-1)]

    ps, pb = _inproj(x, sc1, sh1, norm1_g, w_in[0].astype(BF16), mu_shift,
                     _tile(s, TILE_INPROJ))

    wlora = jnp.zeros((LANES, 2 * D_RWKV), F32)
    wlora = wlora.at[:D_DECAY_LORA, :D_RWKV].set(w_decay_up[0])
    wlora = wlora.at[D_DECAY_LORA:, D_RWKV:].set(w_a_up[0])
    seq = lambda z: z.reshape(b, s, D_RWKV)
    w_c, rh_c, u0_c, y0_c, bh_c, kh_c, pl_c, v, g, bonus = _chunk(
        ps.reshape(t, N_SHIFT), w0, a0, wlora, w_g_up[0], k_k, k_a, row(r_k))
    y = _rwkv(seq(w_c), seq(rh_c), seq(u0_c), seq(y0_c), seq(bh_c), seq(kh_c), seq(v),
              seq(pl_c), _tile(s, TILE_SCAN))

    bias_tile = jnp.repeat(b_spatial[0].T, HEAD, axis=1)
    yb = _sgu(pb.reshape(t, 2 * D_SGU), sgu_ln_g, sgu_ln_b, w_spatial[0], bias_tile,
              _tile(t, TILE_SGU))

    tm_mid = _tile(s, TILE_MID)
    wr_t = jnp.pad(w_router[0].T, ((0, LANES - N_EXPERTS), (0, 0)))
    wo = w_out[0].astype(BF16)
    h2, logits_t, base = _mid(
        y.reshape(t, D_RWKV), g, bonus, lnx_g, lnx_b, yb, x.reshape(t, d),
        g1, sc2, sh2, g2, norm2_g, wo[:D_RWKV], wo[D_RWKV:], wr_t,
        w1_s[0].astype(BF16), w3_s[0].astype(BF16), w2_s[0].astype(BF16), tm_mid, s // tm_mid)

    tm_moe = _tile(t, TILE_MOE)
    gates_t, pos_t, fexp, fsub, fdense = _route(
        logits_t, jnp.broadcast_to(e_bias[0][:, None], (N_EXPERTS, tm_moe)), tm_moe)

    routed = _moe(fexp[:, 0], fsub[:, :tm_moe // MOE_SUB].reshape(-1), fdense[:, 0],
                  h2, gates_t, pos_t, w1_e[0].astype(BF16), w3_e[0].astype(BF16),
                  w2_e[0].astype(BF16), tm_moe)
    tm_fin = _tile(s, TILE_FINAL)
    out = _final(base, routed, g2, row(norm_f_g), tm_fin, s // tm_fin)
    return out.reshape(b, s, d)
```

```python
import functools

import jax
import jax.numpy as jnp
from jax import lax
from jax.experimental import pallas as pl
from jax.experimental.pallas import tpu as pltpu

F32 = jnp.float32
BF16 = jnp.bfloat16

HEAD = 64
LANES = 128
SUBLANES = 8
D_RWKV = 512
D_SGU = 512
N_RWKV_PAIRS = D_RWKV // LANES
D_DECAY_LORA = 64
D_AAA_LORA = 64
D_GATE_LORA = 128
N_SHIFT = 3 * D_RWKV + D_DECAY_LORA + D_AAA_LORA + D_GATE_LORA
SGU_BLOCK = 128
CHUNK = 64
N_EXPERTS = 64
N_GROUPS = 8
GROUP_SIZE = N_EXPERTS // N_GROUPS
TOPK_GROUPS = 4
TOP_K = 8
D_EXPERT = 256
ROUTED_SCALE = 2.5
RMS_EPS = 1e-6
LN_EPS = 1e-5
LN_X_EPS = 64e-5
VMEM_LIMIT_BYTES = 56 * 1024 * 1024

TILE_MOD_COLS = 1024
TILE_INPROJ = 512
TILE_SCAN = 256
TILE_MID = 512
TILE_MOE = 1024
TILE_FINAL = 512


def _params(semantics):
    return pltpu.CompilerParams(dimension_semantics=semantics,
                                vmem_limit_bytes=VMEM_LIMIT_BYTES)


def _dot3(a, b, dims=(((1,), (0,)), ((), ()))):
    ((ca,), (cb,)), _ = dims
    a_hi = a.astype(BF16)
    a_lo = (a - a_hi.astype(F32)).astype(BF16)
    b_hi = b.astype(BF16)
    b_lo = (b - b_hi.astype(F32)).astype(BF16)
    dg = functools.partial(lax.dot_general, dimension_numbers=dims,
                           preferred_element_type=F32)
    return (dg(jnp.concatenate([a_hi, a_lo], axis=ca), jnp.concatenate([b_hi, b_hi], axis=cb))
            + dg(a_hi, b_lo))


def _head_ones():
    r = lax.broadcasted_iota(jnp.int32, (LANES, LANES), 0) // HEAD
    c = lax.broadcasted_iota(jnp.int32, (LANES, LANES), 1) // HEAD
    return jnp.where(r == c, 1.0, 0.0).astype(BF16)


def _head_sum(z, ones):
    both = jnp.concatenate([ones, ones], axis=0)
    parts = []
    for p in range(z.shape[1] // LANES):
        zp = z[:, LANES * p:LANES * (p + 1)]
        hi = zp.astype(BF16)
        lo = (zp - hi.astype(F32)).astype(BF16)
        parts.append(jnp.dot(jnp.concatenate([hi, lo], axis=1), both,
                             preferred_element_type=F32))
    return jnp.concatenate(parts, axis=-1)


def _sigmoid(x):
    return 1.0 / (1.0 + jnp.exp(-x))


def _silu(x):
    return x * _sigmoid(x)


def _mod_kernel(c_ref, w_ref, b_ref, o_ref):
    s = _silu(c_ref[...])
    o_ref[...] = _dot3(s, w_ref[...]) + b_ref[...]


def _mod(c_pad, w_ada, b_ada):
    rows, d = c_pad.shape
    n = w_ada.shape[1]
    tn = TILE_MOD_COLS
    return pl.pallas_call(
        _mod_kernel,
        grid=(n // tn,),
        in_specs=[pl.BlockSpec((rows, d), lambda j: (0, 0)),
                  pl.BlockSpec((d, tn), lambda j: (0, j)),
                  pl.BlockSpec((1, tn), lambda j: (0, j))],
        out_specs=pl.BlockSpec((rows, tn), lambda j: (0, j)),
        out_shape=jax.ShapeDtypeStruct((rows, n), F32),
        compiler_params=_params(("arbitrary",)),
        name="adaln_mod",
    )(c_pad, w_ada, b_ada)


def _inproj_kernel(x_ref, sc_ref, sh_ref, g_ref, w_ref, mu_ref,
                   sgu_g_ref, sgu_b_ref, ws_ref, sgu_bias_ref, ps_ref, yb_ref, carry_ref):
    @pl.when(pl.program_id(1) == 0)
    def _():
        carry_ref[...] = jnp.zeros_like(carry_ref)

    x = x_ref[...]
    tm = x.shape[0]
    h = x * lax.rsqrt(jnp.mean(x * x, axis=-1, keepdims=True) + RMS_EPS) * g_ref[...]
    h = h * (1.0 + sc_ref[...]) + sh_ref[...]
    proj = jnp.dot(h.astype(BF16), w_ref[...], preferred_element_type=F32)
    ps = proj[:, :N_SHIFT]
    prev = pltpu.roll(ps, 1, 0)
    first = lax.broadcasted_iota(jnp.int32, ps.shape, 0) == 0
    prev = jnp.where(first, carry_ref[0:1, :], prev)
    carry_ref[0:1, :] = ps[tm - 1:tm, :]
    ps_ref[...] = ps + (prev - ps) * mu_ref[...]
    _sgu_apply(proj[:, N_SHIFT:], sgu_g_ref, sgu_b_ref, ws_ref, sgu_bias_ref, yb_ref)


def _inproj(x, sc1, sh1, g, w_in_bf16, mu, sgu_g, sgu_b, w_spatial, sgu_bias, tm):
    b, s, d = x.shape
    d_in = w_in_bf16.shape[1]
    assert d_in == N_SHIFT + 2 * D_SGU and tm % SGU_BLOCK == 0
    row = lambda bi, i: (bi, 0, 0)
    const = lambda shape: pl.BlockSpec(shape, lambda bi, i: (0,) * len(shape))
    return pl.pallas_call(
        _inproj_kernel,
        grid=(b, s // tm),
        in_specs=[pl.BlockSpec((None, tm, d), lambda bi, i: (bi, i, 0)),
                  pl.BlockSpec((None, 1, d), row),
                  pl.BlockSpec((None, 1, d), row),
                  const((1, d)), const((d, d_in)), const((1, N_SHIFT)),
                  const((1, D_SGU)), const((1, D_SGU)), const(w_spatial.shape),
                  const((SGU_BLOCK, D_SGU))],
        out_specs=[pl.BlockSpec((None, tm, N_SHIFT), lambda bi, i: (bi, i, 0)),
                   pl.BlockSpec((None, tm, D_SGU), lambda bi, i: (bi, i, 0))],
        out_shape=[jax.ShapeDtypeStruct((b, s, N_SHIFT), F32),
                   jax.ShapeDtypeStruct((b, s, D_SGU), BF16)],
        scratch_shapes=[pltpu.VMEM((8, N_SHIFT), F32)],
        compiler_params=_params(("arbitrary", "arbitrary")),
        name="inproj_shift_sgu",
    )(x, sc1, sh1, g, w_in_bf16, mu, sgu_g, sgu_b, w_spatial, sgu_bias)


def _token_terms(ps, w0_ref, a0_ref, wlora_ref, wg_ref, kk_ref, ka_ref, rk_ref):
    c1, c2, c3 = D_RWKV, 2 * D_RWKV, 3 * D_RWKV
    r, k, v = ps[:, :c1], ps[:, c1:c2], ps[:, c2:c3]
    xwa = ps[:, c3:c3 + LANES]
    xg = ps[:, c3 + LANES:]
    is_w = lax.broadcasted_iota(jnp.int32, xwa.shape, 1) < D_DECAY_LORA
    lora = _dot3(jnp.where(is_w, jnp.tanh(xwa), xwa), wlora_ref[...])
    wlog = -jax.nn.softplus(-(w0_ref[...] + lora[:, :c1])) - 0.5
    log_decay = -jnp.exp(wlog)
    a = _sigmoid(a0_ref[...] + lora[:, c1:])
    g = _dot3(_sigmoid(xg), wg_ref[...])
    ones = _head_ones()
    kk = k * kk_ref[...]
    kk = kk / jnp.maximum(jnp.sqrt(_head_sum(kk * kk, ones)), 1e-12)
    kh = k * (1.0 + (a - 1.0) * ka_ref[...])
    bonus = _head_sum(r * kh * rk_ref[...], ones) * v
    return r, log_decay, kh, v, kk, kk * a, g, bonus


RWKV_CHUNK = 32
CHUNK_TILE = 128
CHUNK_GROUP_PAIRS = 4

_NN = (((1,), (0,)), ((), ()))
_NT = (((1,), (1,)), ((), ()))


def _dot1(a, b, dims=_NN):
    return lax.dot_general(a.astype(BF16), b.astype(BF16), dims, preferred_element_type=F32)


def _dot_exact_lhs(lhs_bf16, x):
    hi = x.astype(BF16)
    rest = x - hi.astype(F32)
    mid = rest.astype(BF16)
    lo = (rest - mid.astype(F32)).astype(BF16)
    return jnp.dot(jnp.concatenate([lhs_bf16] * 3, axis=1),
                   jnp.concatenate([hi, mid, lo], axis=0), preferred_element_type=F32)


def _chunk_kernel(ps_ref, w0_ref, a0_ref, wlora_ref, wg_ref, kk_ref, ka_ref, rk_ref,
                  w_o, rh_o, u0_o, y0_o, bh_o, kh_o, pl_o, v_o, g_o, bonus_o):
    r_all, lw_all, k_all, v_all, kk_all, b_all, g_all, bonus_all = _token_terms(
        ps_ref[...], w0_ref, a0_ref, wlora_ref, wg_ref, kk_ref, ka_ref, rk_ref)
    v_o[...] = v_all
    g_o[...] = g_all
    bonus_o[...] = bonus_all
    n = CHUNK_TILE
    tpos = lax.broadcasted_iota(jnp.int32, (n, n), 0)
    spos = lax.broadcasted_iota(jnp.int32, (n, n), 1)
    same = (tpos // RWKV_CHUNK) == (spos // RWKV_CHUNK)
    incl = same & (spos <= tpos)
    strict = same & (spos < tpos)
    tri_ones = jnp.where(incl, 1.0, 0.0).astype(BF16)
    blk_ones = jnp.where(same, 1.0, 0.0).astype(BF16)
    eye = jnp.where(tpos == spos, 1.0, 0.0)
    lane_head = lax.broadcasted_iota(jnp.int32, (n, LANES), 1) // HEAD

    for first in range(0, N_RWKV_PAIRS, CHUNK_GROUP_PAIRS):
        pairs = range(CHUNK_GROUP_PAIRS)
        slabs = [slice(LANES * (first + p), LANES * (first + p + 1)) for p in pairs]
        lw, r, k, v, kk, b = ([x[:, sl] for sl in slabs]
                              for x in (lw_all, r_all, k_all, v_all, kk_all, b_all))
        cum = [_dot_exact_lhs(tri_ones, x) for x in lw]
        tot = [_dot_exact_lhs(blk_ones, x) for x in lw]
        at = [-kk[p] * jnp.exp(cum[p] - lw[p]) for p in pairs]
        rt = [r[p] * jnp.exp(cum[p]) for p in pairs]
        inv = [jnp.exp(-cum[p]) for p in pairs]
        bt = [b[p] * inv[p] for p in pairs]
        kt = [k[p] * inv[p] for p in pairs]

        chains = [(p, h) for p in pairs for h in range(LANES // HEAD)]
        mine = [lane_head == h for h in range(LANES // HEAD)]
        a_h = [jnp.where(mine[h], at[p], 0.0) for p, h in chains]
        r_h = [jnp.where(mine[h], rt[p], 0.0) for p, h in chains]
        v_h = [jnp.where(mine[h], v[p], 0.0) for p, h in chains]
        each = range(len(chains))
        n = CHUNK_TILE
        btkt = [jnp.concatenate([bt[p], kt[p]], axis=0) for p in pairs]
        a_bk = [_dot3(a_h[c], btkt[chains[c][0]], _NT) for c in each]
        a_ab = [jnp.where(strict, m[:, :n], 0.0) for m in a_bk]
        a_ak = [jnp.where(strict, m[:, n:], 0.0) for m in a_bk]
        tinv = [eye + a for a in a_ab]
        x = a_ab
        for _ in range(RWKV_CHUNK.bit_length() - 2):
            x = [_dot1(xc, xc) for xc in x]
            tinv = [tc + _dot1(tc, xc) for tc, xc in zip(tinv, x)]
        akv = [_dot3(a_ak[c], v_h[c]) for c in each]
        m_bk = [_dot1(r_h[c], btkt[chains[c][0]], _NT) for c in each]
        m_rb = [jnp.where(incl, m[:, :n], 0.0) for m in m_bk]
        m_rk = [jnp.where(incl, m[:, n:], 0.0) for m in m_bk]
        wu = [_dot3(tinv[c], jnp.concatenate([a_h[c], akv[c]], axis=1)) for c in each]
        w_h = [m[:, :LANES] for m in wu]
        u0_h = [m[:, LANES:] for m in wu]
        ry = [_dot1(m_rb[c], wu[c]) for c in each]
        rh_h = [r_h[c] + ry[c][:, :LANES] for c in each]
        y0_h = [ry[c][:, LANES:] + _dot1(m_rk[c], v_h[c]) for c in each]

        for p in pairs:
            c0, c1 = 2 * p, 2 * p + 1
            rem = jnp.exp(tot[p] - cum[p])
            w_o[:, slabs[p]] = w_h[c0] + w_h[c1]
            rh_o[:, slabs[p]] = rh_h[c0] + rh_h[c1]
            u0_o[:, slabs[p]] = u0_h[c0] + u0_h[c1]
            y0_o[:, slabs[p]] = y0_h[c0] + y0_h[c1]
            bh_o[:, slabs[p]] = b[p] * rem
            kh_o[:, slabs[p]] = k[p] * rem
            pl_o[:, slabs[p]] = jnp.exp(tot[p])


def _chunk(ps, w0, a0, wlora, wg, k_k, k_a, r_k):
    t = ps.shape[0]
    const = lambda shape: pl.BlockSpec(shape, lambda i: (0, 0))
    spec = pl.BlockSpec((CHUNK_TILE, D_RWKV), lambda i: (i, 0))
    out = jax.ShapeDtypeStruct((t, D_RWKV), F32)
    return pl.pallas_call(
        _chunk_kernel,
        grid=(t // CHUNK_TILE,),
        in_specs=[pl.BlockSpec((CHUNK_TILE, N_SHIFT), lambda i: (i, 0)),
                  const((1, D_RWKV)), const((1, D_RWKV)),
                  const((LANES, 2 * D_RWKV)), const((D_GATE_LORA, D_RWKV)),
                  const((1, D_RWKV)), const((1, D_RWKV)), const((1, D_RWKV))],
        out_specs=[spec] * 10,
        out_shape=[out] * 10,
        compiler_params=_params(("arbitrary",)),
        name="rwkv_chunk_prep",
    )(ps, w0, a0, wlora, wg, k_k, k_a, r_k)


def _rwkv_kernel(w_ref, rh_ref, u0_ref, y0_ref, bh_ref, kh_ref, v_ref, pl_ref,
                 y_ref, state_ref):
    @pl.when(pl.program_id(0) == 0)
    def _():
        state_ref[...] = jnp.zeros_like(state_ref)

    nb, tt, _ = w_ref.shape
    L = RWKV_CHUNK
    depth = LANES * pl.cdiv(6 * L + SUBLANES, LANES)
    row_head = lax.broadcasted_iota(jnp.int32, (LANES, LANES), 0) // HEAD
    col_head = lax.broadcasted_iota(jnp.int32, (LANES, LANES), 1) // HEAD
    same_head = row_head == col_head
    pad = jnp.zeros((depth - 6 * L - SUBLANES, LANES), F32)
    zeros_tail = jnp.zeros((depth - 6 * L, LANES), BF16)

    def split(x):
        hi = x.astype(BF16)
        return hi, (x - hi.astype(F32)).astype(BF16)

    def chunk(ci, carry):
        t0 = pl.multiple_of(ci * L, L)
        streams = [(bi, p) for bi in range(nb) for p in range(N_RWKV_PAIRS)]

        def rows(ref, n=L):
            return [ref[bi, pl.ds(t0, n), LANES * p:LANES * (p + 1)] for bi, p in streams]

        each = range(len(streams))
        s = [state_ref[bi * N_RWKV_PAIRS + p] for bi, p in streams]
        w, rh, u0, y0, bh, kh, v = (rows(ref) for ref in
                                   (w_ref, rh_ref, u0_ref, y0_ref, bh_ref, kh_ref, v_ref))
        p_rows = rows(pl_ref, SUBLANES)
        left, p_col = [], []
        for c in each:
            (b_hi, b_lo), (k_hi, k_lo) = split(bh[c]), split(kh[c])
            block = jnp.concatenate([x.astype(F32) for x in (b_hi, b_lo, b_hi, k_hi, k_lo, k_hi)]
                                    + [p_rows[c], pad], axis=0).T
            left.append(block.astype(BF16))
            p_col.append(jnp.broadcast_to(block[:, 6 * L:6 * L + 1], (LANES, LANES)))
        g = []
        for c in each:
            (l_hi, l_lo), (s_hi, s_lo) = split(jnp.concatenate([w[c], rh[c]], axis=0)), split(s[c])
            g.append(jnp.dot(jnp.concatenate([l_hi, l_lo], axis=1),
                             jnp.concatenate([s_hi, s_hi], axis=0), preferred_element_type=F32)
                     + jnp.dot(l_hi, s_lo, preferred_element_type=F32))
        u = [g[c][:L] + u0[c] for c in each]
        upd = []
        for c in each:
            (u_hi, u_lo), (v_hi, v_lo) = split(u[c]), split(v[c])
            right = jnp.concatenate([u_hi, u_hi, u_lo, v_hi, v_hi, v_lo, zeros_tail], axis=0)
            upd.append(jnp.dot(left[c], right, preferred_element_type=F32))
        for c, (bi, p) in enumerate(streams):
            y_ref[bi, pl.ds(t0, L), LANES * p:LANES * (p + 1)] = g[c][L:] + y0[c]
            state_ref[bi * N_RWKV_PAIRS + p] = (p_col[c] * s[c]
                                                + jnp.where(same_head, upd[c], 0.0))
        return carry

    lax.fori_loop(0, tt // L, chunk, 0)


def _rwkv(w, rh, u0, y0, bh, kh, v, p_l, tt):
    nb, s, d = w.shape
    spec = pl.BlockSpec((nb, tt, d), lambda i: (0, i, 0))
    return pl.pallas_call(
        _rwkv_kernel,
        grid=(s // tt,),
        in_specs=[spec] * 8,
        out_specs=spec,
        out_shape=jax.ShapeDtypeStruct((nb, s, d), F32),
        scratch_shapes=[pltpu.VMEM((nb * N_RWKV_PAIRS, LANES, LANES), F32)],
        compiler_params=_params(("arbitrary",)),
        name="rwkv_scan",
    )(w, rh, u0, y0, bh, kh, v, p_l)


def _gelu_tanh(x):
    return 0.5 * x * (1.0 + jnp.tanh(0.7978845608028654 * (x + 0.044715 * (x * x * x))))


def _sgu_apply(pb, g_ref, b_ref, ws_ref, bias_ref, o_ref):
    z = _gelu_tanh(pb)
    u, v = z[:, :D_SGU], z[:, D_SGU:]
    mu = jnp.mean(v, axis=-1, keepdims=True)
    vc = v - mu
    var = jnp.mean(vc * vc, axis=-1, keepdims=True)
    vn = (vc * lax.rsqrt(var + LN_EPS) * g_ref[...] + b_ref[...]).astype(BF16)
    tpos = lax.broadcasted_iota(jnp.int32, (SGU_BLOCK, SGU_BLOCK), 0) // CHUNK
    spos = lax.broadcasted_iota(jnp.int32, (SGU_BLOCK, SGU_BLOCK), 1) // CHUNK
    causal = spos <= tpos
    first_head = lax.broadcasted_iota(jnp.int32, (SGU_BLOCK, LANES), 1) < HEAD
    for p in range(D_SGU // LANES):
        sl = slice(LANES * p, LANES * (p + 1))
        ws0 = jnp.where(causal, ws_ref[2 * p], 0.0).astype(BF16)
        ws1 = jnp.where(causal, ws_ref[2 * p + 1], 0.0).astype(BF16)
        for n in range(pb.shape[0] // SGU_BLOCK):
            rows = slice(SGU_BLOCK * n, SGU_BLOCK * (n + 1))
            vb = vn[rows, sl]
            sp = jnp.where(first_head,
                           jnp.dot(ws0, vb, preferred_element_type=F32),
                           jnp.dot(ws1, vb, preferred_element_type=F32))
            o_ref[rows, sl] = (u[rows, sl] * (sp + bias_ref[:, sl])).astype(o_ref.dtype)


def _mid_kernel(y_ref, g_ref, bonus_ref, lng_ref, lnb_ref, yb_ref, x_ref,
                g1_ref, sc2_ref, sh2_ref, g2_ref, n2_ref,
                woa_ref, wob_ref, wrt_ref, w1_ref, w3_ref, w2_ref,
                h2_ref, logit_ref, base_ref):
    ones = _head_ones()
    y = y_ref[...]
    mu = _head_sum(y, ones) * (1.0 / HEAD)
    yc = y - mu
    var = _head_sum(yc * yc, ones) * (1.0 / HEAD)
    ya = yc * lax.rsqrt(var + LN_X_EPS) * lng_ref[...] + lnb_ref[...]
    ya = ((ya + bonus_ref[...]) * g_ref[...]).astype(BF16)
    yo = (jnp.dot(ya, woa_ref[...], preferred_element_type=F32)
          + jnp.dot(yb_ref[...], wob_ref[...], preferred_element_type=F32))
    x1 = x_ref[...] + g1_ref[...] * yo
    h2 = x1 * lax.rsqrt(jnp.mean(x1 * x1, axis=-1, keepdims=True) + RMS_EPS) * n2_ref[...]
    h2 = h2 * (1.0 + sc2_ref[...]) + sh2_ref[...]
    h2b = h2.astype(BF16)
    h2_ref[...] = h2b
    logit_ref[...] = _dot3(wrt_ref[...], h2, dims=(((1,), (1,)), ((), ())))
    act = (_silu(jnp.dot(h2b, w1_ref[...], preferred_element_type=F32))
           * jnp.dot(h2b, w3_ref[...], preferred_element_type=F32)).astype(BF16)
    ms = jnp.dot(act, w2_ref[...], preferred_element_type=F32)
    base_ref[...] = x1 + g2_ref[...] * ms


def _mid(y, g, bonus, lnx_g, lnx_b, yb, x, g1, sc2, sh2, g2, n2g,
         wo_a, wo_b, wr_t, w1s, w3s, w2s, tm, tiles_per_batch):
    t, d = x.shape
    tok = lambda w: pl.BlockSpec((tm, w), lambda i: (i, 0))
    const = lambda shape: pl.BlockSpec(shape, lambda i: (0,) * len(shape))
    mod = pl.BlockSpec((None, 1, d), lambda i: (i // tiles_per_batch, 0, 0))
    return pl.pallas_call(
        _mid_kernel,
        grid=(t // tm,),
        in_specs=[tok(D_RWKV), tok(D_RWKV), tok(D_RWKV),
                  const((1, D_RWKV)), const((1, D_RWKV)),
                  tok(D_SGU), tok(d), mod, mod, mod, mod, const((1, d)),
                  const(wo_a.shape), const(wo_b.shape), const(wr_t.shape),
                  const(w1s.shape), const(w3s.shape), const(w2s.shape)],
        out_specs=[tok(d), pl.BlockSpec((LANES, tm), lambda i: (0, i)), tok(d)],
        out_shape=[jax.ShapeDtypeStruct((t, d), BF16),
                   jax.ShapeDtypeStruct((LANES, t), F32),
                   jax.ShapeDtypeStruct((t, d), F32)],
        compiler_params=_params(("arbitrary",)),
        name="outproj_norm_router_shared",
    )(y, g, bonus, lnx_g, lnx_b, yb, x, g1, sc2, sh2, g2, n2g,
      wo_a, wo_b, wr_t, w1s, w3s, w2s)


MOE_SUB = 128
MOE_CAP = 32
MOE_GROUP = 8


def _route_kernel(logit_ref, bias_ref, gate_ref, pos_ref, fexp_ref, fsub_ref, fdense_ref):
    tt = logit_ref.shape[1]
    scores = _sigmoid(logit_ref[0:N_EXPERTS, :])
    biased = scores + bias_ref[...]
    neg_inf = -jnp.inf

    group_blocks, group_scores = [], []
    for gi in range(N_GROUPS):
        blk = biased[GROUP_SIZE * gi:GROUP_SIZE * (gi + 1), :]
        m1 = jnp.max(blk, axis=0, keepdims=True)
        is_max = blk == m1
        n_max = jnp.sum(jnp.where(is_max, 1.0, 0.0), axis=0, keepdims=True)
        m2 = jnp.max(jnp.where(is_max, neg_inf, blk), axis=0, keepdims=True)
        group_blocks.append(blk)
        group_scores.append(m1 + jnp.where(n_max >= 2.0, m1, m2))

    masked = []
    for gi in range(N_GROUPS):
        rank = jnp.zeros((1, tt), F32)
        for gj in range(N_GROUPS):
            if gj == gi:
                continue
            ahead = (group_scores[gj] >= group_scores[gi]) if gj < gi else \
                    (group_scores[gj] > group_scores[gi])
            rank += jnp.where(ahead, 1.0, 0.0)
        masked.append(jnp.where(rank < float(TOPK_GROUPS), group_blocks[gi], neg_inf))
    sub = lax.broadcasted_iota(jnp.int32, (GROUP_SIZE, tt), 0)
    ranks = [jnp.zeros((GROUP_SIZE, tt), F32) for _ in range(N_GROUPS)]
    for ej in range(N_EXPERTS):
        gj, oj = divmod(ej, GROUP_SIZE)
        other = masked[gj][oj:oj + 1, :]
        for gi in range(N_GROUPS):
            blk = masked[gi]
            if gi > gj:
                ahead = jnp.where(other >= blk, 1.0, 0.0)
            elif gi < gj:
                ahead = jnp.where(other > blk, 1.0, 0.0)
            else:
                ahead = jnp.where(sub > oj, jnp.where(other >= blk, 1.0, 0.0),
                                  jnp.where(other > blk, 1.0, 0.0))
            ranks[gi] = ranks[gi] + ahead
    rank = jnp.concatenate(ranks, axis=0)
    chosen = jnp.where(rank < float(TOP_K), 1.0, 0.0)
    sw = chosen * scores
    gate_ref[...] = sw / jnp.sum(sw, axis=0, keepdims=True) * ROUTED_SCALE

    before = (lax.broadcasted_iota(jnp.int32, (MOE_SUB, MOE_SUB), 0)
              < lax.broadcasted_iota(jnp.int32, (MOE_SUB, MOE_SUB), 1))
    prefix_ones = jnp.where(before, 1.0, 0.0).astype(BF16)
    all_ones = jnp.ones((MOE_SUB, MOE_SUB), BF16)
    lane = lax.broadcasted_iota(jnp.int32, (N_EXPERTS, LANES), 1)
    positions, counts = [], jnp.zeros((N_EXPERTS, LANES), F32)
    for s in range(tt // MOE_SUB):
        blk = chosen[:, MOE_SUB * s:MOE_SUB * (s + 1)].astype(BF16)
        prefix = jnp.dot(blk, prefix_ones, preferred_element_type=F32)
        positions.append(jnp.where(blk > 0, prefix, -1.0))
        total = jnp.dot(blk, all_ones, preferred_element_type=F32)
        counts = counts + jnp.where(lane == s, total, 0.0)
    pos_ref[...] = jnp.concatenate(positions, axis=1)
    per_expert = jnp.max(counts, axis=1, keepdims=True)
    fexp_ref[...] = jnp.broadcast_to(jnp.where(per_expert > float(MOE_CAP), 1, 0),
                                     fexp_ref.shape).astype(jnp.int32)
    per_group = jnp.concatenate(
        [jnp.max(counts[MOE_GROUP * g:MOE_GROUP * (g + 1), :], axis=0, keepdims=True)
         for g in range(N_EXPERTS // MOE_GROUP)], axis=0)
    fsub_ref[...] = jnp.where(per_group > float(MOE_CAP), 1, 0).astype(jnp.int32)
    group_worst = jnp.max(per_group, axis=1, keepdims=True)
    fdense_ref[...] = jnp.broadcast_to(jnp.where(group_worst > float(2 * MOE_CAP), 1, 0),
                                       fdense_ref.shape).astype(jnp.int32)


def _route(logits_t, bias_tile, tt):
    t = logits_t.shape[1]
    n_tiles = t // tt
    n_groups = N_EXPERTS // MOE_GROUP
    tok = pl.BlockSpec((N_EXPERTS, tt), lambda i: (0, i))
    return pl.pallas_call(
        _route_kernel,
        grid=(n_tiles,),
        in_specs=[pl.BlockSpec((LANES, tt), lambda i: (0, i)),
                  pl.BlockSpec((N_EXPERTS, tt), lambda i: (0, 0))],
        out_specs=[tok, tok,
                   pl.BlockSpec((N_EXPERTS, LANES), lambda i: (i, 0)),
                   pl.BlockSpec((n_groups, LANES), lambda i: (i, 0)),
                   pl.BlockSpec((n_groups, LANES), lambda i: (i, 0))],
        out_shape=[jax.ShapeDtypeStruct((N_EXPERTS, t), F32),
                   jax.ShapeDtypeStruct((N_EXPERTS, t), F32),
                   jax.ShapeDtypeStruct((n_tiles * N_EXPERTS, LANES), jnp.int32),
                   jax.ShapeDtypeStruct((n_tiles * n_groups, LANES), jnp.int32),
                   jax.ShapeDtypeStruct((n_tiles * n_groups, LANES), jnp.int32)],
        compiler_params=_params(("arbitrary",)),
        name="route_topk",
    )(logits_t, bias_tile)


def _moe_kernel(fexp_ref, fsub_ref, fdense_ref, h2_ref, gate_ref, pos_ref,
                w1_ref, w3_ref, w2_ref, o_ref, xy_ref, gb_ref, pt_ref):
    tile, eg = pl.program_id(0), pl.program_id(1)
    n_groups = pl.num_programs(1)
    tm = h2_ref.shape[0]
    n_sub = tm // MOE_SUB

    @pl.when(eg == 0)
    def _():
        o_ref[...] = jnp.zeros_like(o_ref)

    @pl.when((tile == 0) & (eg == 0))
    def _():
        xy_ref[...] = jnp.zeros_like(xy_ref)
        gb_ref[...] = jnp.zeros_like(gb_ref)

    def lane_broadcast(m):
        return jnp.broadcast_to(jnp.sum(m, axis=-1, keepdims=True), (m.shape[0], D_EXPERT))

    def hidden(x, gate_rows, j):
        return (_silu(jnp.dot(x, w1_ref[j], preferred_element_type=F32))
                * jnp.dot(x, w3_ref[j], preferred_element_type=F32) * gate_rows).astype(BF16)

    def pipelined(n, first, second):
        pending = first(0)
        for i in range(1, n):
            nxt = first(i)
            second(i - 1, pending)
            pending = nxt
        second(n - 1, pending)

    cap = MOE_CAP
    slab_rows = n_sub * cap
    go_dense = fdense_ref[tile * n_groups + eg] != 0

    def needs_second(s):
        return fsub_ref[(tile * n_groups + eg) * n_sub + s] != 0

    def slab_slice(j, q, s=None):
        start = (2 * j + q) * slab_rows
        if s is None:
            return slice(start, start + slab_rows)
        return slice(start + s * cap, start + (s + 1) * cap)

    def gather(s, q):
        cols = slice(MOE_SUB * s, MOE_SUB * (s + 1))
        slot = lax.broadcasted_iota(jnp.int32, (cap, MOE_SUB), 0).astype(F32) + float(q * cap)
        pos, gate = pos_ref[:, cols], gate_ref[:, cols]
        hot = [jnp.where(slot == pos[j:j + 1, :], 1.0, 0.0) for j in range(MOE_GROUP)]
        p = jnp.concatenate(hot, axis=0)
        pg = jnp.concatenate([hot[j] * gate[j:j + 1, :] for j in range(MOE_GROUP)], axis=0)
        xg = jnp.dot(p.astype(BF16), h2_ref[cols, :],
                     preferred_element_type=F32).astype(BF16)
        gb = lane_broadcast(pg)
        for j in range(MOE_GROUP):
            xy_ref[slab_slice(j, q, s), :] = xg[j * cap:(j + 1) * cap]
            gb_ref[slab_slice(j, q, s), :] = gb[j * cap:(j + 1) * cap]
        pt_ref[s, :, q * MOE_GROUP * cap:(q + 1) * MOE_GROUP * cap] = p.T.astype(BF16)

    def scatter(s, q):
        y = jnp.concatenate([xy_ref[slab_slice(j, q, s), :] for j in range(MOE_GROUP)], axis=0)
        o_ref[MOE_SUB * s:MOE_SUB * (s + 1), :] += jnp.dot(
            pt_ref[s, :, q * MOE_GROUP * cap:(q + 1) * MOE_GROUP * cap], y,
            preferred_element_type=F32)

    def project(j, q, act):
        xy_ref[slab_slice(j, q), :] = jnp.dot(
            act, w2_ref[j], preferred_element_type=F32).astype(BF16)

    @pl.when(jnp.logical_not(go_dense))
    def _():
        for s in range(n_sub):
            gather(s, 0)
        pipelined(MOE_GROUP,
                  lambda j: hidden(xy_ref[slab_slice(j, 0), :], gb_ref[slab_slice(j, 0), :], j),
                  lambda j, act: project(j, 0, act))
        for s in range(n_sub):
            scatter(s, 0)
        for s in range(n_sub):
            pl.when(needs_second(s))(functools.partial(gather, s, 1))
        for j in range(MOE_GROUP):
            @pl.when(fexp_ref[tile * N_EXPERTS + eg * MOE_GROUP + j] != 0)
            def _(j=j):
                project(j, 1, hidden(xy_ref[slab_slice(j, 1), :],
                                     gb_ref[slab_slice(j, 1), :], j))
        for s in range(n_sub):
            pl.when(needs_second(s))(functools.partial(scatter, s, 1))

    @pl.when(go_dense)
    def _():
        eye = (lax.broadcasted_iota(jnp.int32, (MOE_SUB, MOE_SUB), 0)
               == lax.broadcasted_iota(jnp.int32, (MOE_SUB, MOE_SUB), 1))
        for j in range(MOE_GROUP):
            for s in range(n_sub):
                cols = slice(MOE_SUB * s, MOE_SUB * (s + 1))
                gb_ref[cols, :] = lane_broadcast(
                    jnp.where(eye, gate_ref[j:j + 1, cols], 0.0))
            act = hidden(h2_ref[...], gb_ref[0:tm, :], j)
            o_ref[...] += jnp.dot(act, w2_ref[j], preferred_element_type=F32)


def _moe(fexp, fsub, fdense, h2, gates_t, pos_t, w1e, w3e, w2e, tm):
    t, d = h2.shape
    ne = w1e.shape[0]
    assert ne % MOE_GROUP == 0 and tm % MOE_SUB == 0
    n_sub = tm // MOE_SUB
    rows = MOE_GROUP * 2 * n_sub * MOE_CAP
    assert rows >= tm, "gate scratch is reused by the dense path"
    tok = pl.BlockSpec((tm, d), lambda i, e, *_: (i, 0))
    per_expert = pl.BlockSpec((MOE_GROUP, tm), lambda i, e, *_: (e, i))
    return pl.pallas_call(
        _moe_kernel,
        grid_spec=pltpu.PrefetchScalarGridSpec(
            num_scalar_prefetch=3,
            grid=(t // tm, ne // MOE_GROUP),
            in_specs=[tok, per_expert, per_expert,
                      pl.BlockSpec((MOE_GROUP, d, D_EXPERT), lambda i, e, *_: (e, 0, 0)),
                      pl.BlockSpec((MOE_GROUP, d, D_EXPERT), lambda i, e, *_: (e, 0, 0)),
                      pl.BlockSpec((MOE_GROUP, D_EXPERT, d), lambda i, e, *_: (e, 0, 0))],
            out_specs=tok,
            scratch_shapes=[pltpu.VMEM((rows, d), BF16),
                            pltpu.VMEM((rows, D_EXPERT), F32),
                            pltpu.VMEM((n_sub, MOE_SUB, 2 * MOE_GROUP * MOE_CAP), BF16)]),
        out_shape=jax.ShapeDtypeStruct((t, d), F32),
        compiler_params=_params(("arbitrary", "arbitrary")),
        name="moe_experts",
    )(fexp, fsub, fdense, h2, gates_t, pos_t, w1e, w3e, w2e)


def _final_kernel(base_ref, routed_ref, g2_ref, nf_ref, o_ref):
    xo = base_ref[...] + g2_ref[...] * routed_ref[...]
    o_ref[...] = (xo * lax.rsqrt(jnp.mean(xo * xo, axis=-1, keepdims=True) + RMS_EPS)
                  * nf_ref[...])


def _final(base, routed, g2, nf, tm, tiles_per_batch):
    t, d = base.shape
    tok = pl.BlockSpec((tm, d), lambda i: (i, 0))
    return pl.pallas_call(
        _final_kernel,
        grid=(t // tm,),
        in_specs=[tok, tok,
                  pl.BlockSpec((None, 1, d), lambda i: (i // tiles_per_batch, 0, 0)),
                  pl.BlockSpec((1, d), lambda i: (0, 0))],
        out_specs=tok,
        out_shape=jax.ShapeDtypeStruct((t, d), F32),
        compiler_params=_params(("arbitrary",)),
        name="residual_final_norm",
    )(base, routed, g2, nf)


def _tile(n, want):
    t = min(n, want)
    assert n % t == 0, (n, t)
    return t


def kernel(x, c, w_ada, b_ada, norm1_g, w_in, mu_shift, w0, w_decay_up, a0, w_a_up, w_g_up,
           k_k, k_a, r_k, lnx_g, lnx_b, sgu_ln_g, sgu_ln_b, w_spatial, b_spatial, w_out,
           norm2_g, w_router, e_bias, w1_e, w3_e, w2_e, w1_s, w3_s, w2_s, norm_f_g):
    b, s, d = x.shape
    t = b * s
    assert w_ada.shape[0] == 1, "single-layer block"
    assert d == 2 * D_RWKV and s % SGU_BLOCK == 0
    row = lambda vec: vec.reshape(1, -1)

    c_pad = jnp.pad(c, ((0, 8 - b), (0, 0)))
    mod = _mod(c_pad, w_ada[0], b_ada)[:b]
    sh1, sc1, g1, sh2, sc2, g2 = [m[:, None, :] for m in jnp.split(mod, 6, axis=-1)]

    bias_tile = jnp.repeat(b_spatial[0].T, HEAD, axis=1)
    ps, yb = _inproj(x, sc1, sh1, norm1_g, w_in[0].astype(BF16), mu_shift,
                     sgu_ln_g, sgu_ln_b, w_spatial[0], bias_tile, _tile(s, TILE_INPROJ))

    wlora = jnp.zeros((LANES, 2 * D_RWKV), F32)
    wlora = wlora.at[:D_DECAY_LORA, :D_RWKV].set(w_decay_up[0])
    wlora = wlora.at[D_DECAY_LORA:, D_RWKV:].set(w_a_up[0])
    seq = lambda z: z.reshape(b, s, D_RWKV)
    w_c, rh_c, u0_c, y0_c, bh_c, kh_c, pl_c, v, g, bonus = _chunk(
        ps.reshape(t, N_SHIFT), w0, a0, wlora, w_g_up[0], k_k, k_a, row(r_k))
    y = _rwkv(seq(w_c), seq(rh_c), seq(u0_c), seq(y0_c), seq(bh_c), seq(kh_c), seq(v),
              seq(pl_c), _tile(s, TILE_SCAN))

    tm_mid = _tile(s, TILE_MID)
    wr_t = jnp.pad(w_router[0].T, ((0, LANES - N_EXPERTS), (0, 0)))
    wo = w_out[0].astype(BF16)
    h2, logits_t, base = _mid(
        y.reshape(t, D_RWKV), g, bonus, lnx_g, lnx_b, yb.reshape(t, D_SGU), x.reshape(t, d),
        g1, sc2, sh2, g2, norm2_g, wo[:D_RWKV], wo[D_RWKV:], wr_t,
        w1_s[0].astype(BF16), w3_s[0].astype(BF16), w2_s[0].astype(BF16), tm_mid, s // tm_mid)

    tm_moe = _tile(t, TILE_MOE)
    gates_t, pos_t, fexp, fsub, fdense = _route(
        logits_t, jnp.broadcast_to(e_bias[0][:, None], (N_EXPERTS, tm_moe)), tm_moe)

    routed = _moe(fexp[:, 0], fsub[:, :tm_moe // MOE_SUB].reshape(-1), fdense[:, 0],
                  h2, gates_t, pos_t, w1_e[0].astype(BF16), w3_e[0].astype(BF16),
                  w2_e[0].astype(BF16), tm_moe)
    tm_fin = _tile(s, TILE_FINAL)
    out = _final(base, routed, g2, row(norm_f_g), tm_fin, s // tm_fin)
    return out.reshape(b, s, d)
```

```python
import functools

import jax
import jax.numpy as jnp
from jax import lax
from jax.experimental import pallas as pl
from jax.experimental.pallas import tpu as pltpu

F32 = jnp.float32
BF16 = jnp.bfloat16

HEAD = 64
LANES = 128
SUBLANES = 8
D_RWKV = 512
D_SGU = 512
N_RWKV_PAIRS = D_RWKV // LANES
D_DECAY_LORA = 64
D_AAA_LORA = 64
D_GATE_LORA = 128
N_SHIFT = 3 * D_RWKV + D_DECAY_LORA + D_AAA_LORA + D_GATE_LORA
SGU_BLOCK = 128
CHUNK = 64
N_EXPERTS = 64
N_GROUPS = 8
GROUP_SIZE = N_EXPERTS // N_GROUPS
TOPK_GROUPS = 4
TOP_K = 8
D_EXPERT = 256
ROUTED_SCALE = 2.5
RMS_EPS = 1e-6
LN_EPS = 1e-5
LN_X_EPS = 64e-5
VMEM_LIMIT_BYTES = 56 * 1024 * 1024

TILE_MOD_COLS = 1024
TILE_INPROJ = 512
TILE_SCAN = 256
TILE_MID = 512
TILE_MOE = 1024


def _params(semantics):
    return pltpu.CompilerParams(dimension_semantics=semantics,
                                vmem_limit_bytes=VMEM_LIMIT_BYTES)


def _dot3(a, b, dims=(((1,), (0,)), ((), ()))):
    ((ca,), (cb,)), _ = dims
    a_hi = a.astype(BF16)
    a_lo = (a - a_hi.astype(F32)).astype(BF16)
    b_hi = b.astype(BF16)
    b_lo = (b - b_hi.astype(F32)).astype(BF16)
    dg = functools.partial(lax.dot_general, dimension_numbers=dims,
                           preferred_element_type=F32)
    return (dg(jnp.concatenate([a_hi, a_lo], axis=ca), jnp.concatenate([b_hi, b_hi], axis=cb))
            + dg(a_hi, b_lo))


def _head_ones():
    r = lax.broadcasted_iota(jnp.int32, (LANES, LANES), 0) // HEAD
    c = lax.broadcasted_iota(jnp.int32, (LANES, LANES), 1) // HEAD
    return jnp.where(r == c, 1.0, 0.0).astype(BF16)


def _head_sum(z, ones):
    both = jnp.concatenate([ones, ones], axis=0)
    parts = []
    for p in range(z.shape[1] // LANES):
        zp = z[:, LANES * p:LANES * (p + 1)]
        hi = zp.astype(BF16)
        lo = (zp - hi.astype(F32)).astype(BF16)
        parts.append(jnp.dot(jnp.concatenate([hi, lo], axis=1), both,
                             preferred_element_type=F32))
    return jnp.concatenate(parts, axis=-1)


def _sigmoid(x):
    return 1.0 / (1.0 + jnp.exp(-x))


def _silu(x):
    return x * _sigmoid(x)


def _mod_kernel(c_ref, w_ref, b_ref, o_ref):
    s = _silu(c_ref[...])
    o_ref[...] = _dot3(s, w_ref[...]) + b_ref[...]


def _mod(c_pad, w_ada, b_ada):
    rows, d = c_pad.shape
    n = w_ada.shape[1]
    tn = TILE_MOD_COLS
    return pl.pallas_call(
        _mod_kernel,
        grid=(n // tn,),
        in_specs=[pl.BlockSpec((rows, d), lambda j: (0, 0)),
                  pl.BlockSpec((d, tn), lambda j: (0, j)),
                  pl.BlockSpec((1, tn), lambda j: (0, j))],
        out_specs=pl.BlockSpec((rows, tn), lambda j: (0, j)),
        out_shape=jax.ShapeDtypeStruct((rows, n), F32),
        compiler_params=_params(("arbitrary",)),
        name="adaln_mod",
    )(c_pad, w_ada, b_ada)


def _inproj_kernel(x_ref, sc_ref, sh_ref, g_ref, w_ref, mu_ref,
                   sgu_g_ref, sgu_b_ref, ws_ref, sgu_bias_ref, ps_ref, yb_ref, carry_ref):
    @pl.when(pl.program_id(1) == 0)
    def _():
        carry_ref[...] = jnp.zeros_like(carry_ref)

    x = x_ref[...]
    tm = x.shape[0]
    h = x * lax.rsqrt(jnp.mean(x * x, axis=-1, keepdims=True) + RMS_EPS) * g_ref[...]
    h = h * (1.0 + sc_ref[...]) + sh_ref[...]
    proj = jnp.dot(h.astype(BF16), w_ref[...], preferred_element_type=F32)
    ps = proj[:, :N_SHIFT]
    prev = pltpu.roll(ps, 1, 0)
    first = lax.broadcasted_iota(jnp.int32, ps.shape, 0) == 0
    prev = jnp.where(first, carry_ref[0:1, :], prev)
    carry_ref[0:1, :] = ps[tm - 1:tm, :]
    ps_ref[...] = ps + (prev - ps) * mu_ref[...]
    _sgu_apply(proj[:, N_SHIFT:], sgu_g_ref, sgu_b_ref, ws_ref, sgu_bias_ref, yb_ref)


def _inproj(x, sc1, sh1, g, w_in_bf16, mu, sgu_g, sgu_b, w_spatial, sgu_bias, tm):
    b, s, d = x.shape
    d_in = w_in_bf16.shape[1]
    assert d_in == N_SHIFT + 2 * D_SGU and tm % SGU_BLOCK == 0
    row = lambda bi, i: (bi, 0, 0)
    const = lambda shape: pl.BlockSpec(shape, lambda bi, i: (0,) * len(shape))
    return pl.pallas_call(
        _inproj_kernel,
        grid=(b, s // tm),
        in_specs=[pl.BlockSpec((None, tm, d), lambda bi, i: (bi, i, 0)),
                  pl.BlockSpec((None, 1, d), row),
                  pl.BlockSpec((None, 1, d), row),
                  const((1, d)), const((d, d_in)), const((1, N_SHIFT)),
                  const((1, D_SGU)), const((1, D_SGU)), const(w_spatial.shape),
                  const((SGU_BLOCK, D_SGU))],
        out_specs=[pl.BlockSpec((None, tm, N_SHIFT), lambda bi, i: (bi, i, 0)),
                   pl.BlockSpec((None, tm, D_SGU), lambda bi, i: (bi, i, 0))],
        out_shape=[jax.ShapeDtypeStruct((b, s, N_SHIFT), F32),
                   jax.ShapeDtypeStruct((b, s, D_SGU), BF16)],
        scratch_shapes=[pltpu.VMEM((8, N_SHIFT), F32)],
        compiler_params=_params(("arbitrary", "arbitrary")),
        name="inproj_shift_sgu",
    )(x, sc1, sh1, g, w_in_bf16, mu, sgu_g, sgu_b, w_spatial, sgu_bias)


def _token_terms(ps, w0_ref, a0_ref, wlora_ref, wg_ref, kk_ref, ka_ref, rk_ref):
    c1, c2, c3 = D_RWKV, 2 * D_RWKV, 3 * D_RWKV
    r, k, v = ps[:, :c1], ps[:, c1:c2], ps[:, c2:c3]
    xwa = ps[:, c3:c3 + LANES]
    xg = ps[:, c3 + LANES:]
    is_w = lax.broadcasted_iota(jnp.int32, xwa.shape, 1) < D_DECAY_LORA
    lora = _dot3(jnp.where(is_w, jnp.tanh(xwa), xwa), wlora_ref[...])
    wlog = -jax.nn.softplus(-(w0_ref[...] + lora[:, :c1])) - 0.5
    log_decay = -jnp.exp(wlog)
    a = _sigmoid(a0_ref[...] + lora[:, c1:])
    g = _dot3(_sigmoid(xg), wg_ref[...])
    ones = _head_ones()
    kk = k * kk_ref[...]
    kk = kk / jnp.maximum(jnp.sqrt(_head_sum(kk * kk, ones)), 1e-12)
    kh = k * (1.0 + (a - 1.0) * ka_ref[...])
    bonus = _head_sum(r * kh * rk_ref[...], ones) * v
    return r, log_decay, kh, v, kk, kk * a, g, bonus


RWKV_CHUNK = 32
CHUNK_TILE = 128
CHUNK_GROUP_PAIRS = 4

_NN = (((1,), (0,)), ((), ()))
_NT = (((1,), (1,)), ((), ()))


def _dot1(a, b, dims=_NN):
    return lax.dot_general(a.astype(BF16), b.astype(BF16), dims, preferred_element_type=F32)


def _dot_exact_lhs(lhs_bf16, x):
    hi = x.astype(BF16)
    rest = x - hi.astype(F32)
    mid = rest.astype(BF16)
    lo = (rest - mid.astype(F32)).astype(BF16)
    return jnp.dot(jnp.concatenate([lhs_bf16] * 3, axis=1),
                   jnp.concatenate([hi, mid, lo], axis=0), preferred_element_type=F32)


def _chunk_kernel(ps_ref, w0_ref, a0_ref, wlora_ref, wg_ref, kk_ref, ka_ref, rk_ref,
                  w_o, rh_o, u0_o, y0_o, bh_o, kh_o, pl_o, v_o, g_o, bonus_o):
    r_all, lw_all, k_all, v_all, kk_all, b_all, g_all, bonus_all = _token_terms(
        ps_ref[...], w0_ref, a0_ref, wlora_ref, wg_ref, kk_ref, ka_ref, rk_ref)
    v_o[...] = v_all
    g_o[...] = g_all
    bonus_o[...] = bonus_all
    n = CHUNK_TILE
    tpos = lax.broadcasted_iota(jnp.int32, (n, n), 0)
    spos = lax.broadcasted_iota(jnp.int32, (n, n), 1)
    same = (tpos // RWKV_CHUNK) == (spos // RWKV_CHUNK)
    incl = same & (spos <= tpos)
    strict = same & (spos < tpos)
    tri_ones = jnp.where(incl, 1.0, 0.0).astype(BF16)
    blk_ones = jnp.where(same, 1.0, 0.0).astype(BF16)
    eye = jnp.where(tpos == spos, 1.0, 0.0)
    lane_head = lax.broadcasted_iota(jnp.int32, (n, LANES), 1) // HEAD

    for first in range(0, N_RWKV_PAIRS, CHUNK_GROUP_PAIRS):
        pairs = range(CHUNK_GROUP_PAIRS)
        slabs = [slice(LANES * (first + p), LANES * (first + p + 1)) for p in pairs]
        lw, r, k, v, kk, b = ([x[:, sl] for sl in slabs]
                              for x in (lw_all, r_all, k_all, v_all, kk_all, b_all))
        cum = [_dot_exact_lhs(tri_ones, x) for x in lw]
        tot = [_dot_exact_lhs(blk_ones, x) for x in lw]
        at = [-kk[p] * jnp.exp(cum[p] - lw[p]) for p in pairs]
        rt = [r[p] * jnp.exp(cum[p]) for p in pairs]
        inv = [jnp.exp(-cum[p]) for p in pairs]
        bt = [b[p] * inv[p] for p in pairs]
        kt = [k[p] * inv[p] for p in pairs]

        chains = [(p, h) for p in pairs for h in range(LANES // HEAD)]
        mine = [lane_head == h for h in range(LANES // HEAD)]
        a_h = [jnp.where(mine[h], at[p], 0.0) for p, h in chains]
        r_h = [jnp.where(mine[h], rt[p], 0.0) for p, h in chains]
        v_h = [jnp.where(mine[h], v[p], 0.0) for p, h in chains]
        each = range(len(chains))
        n = CHUNK_TILE
        btkt = [jnp.concatenate([bt[p], kt[p]], axis=0) for p in pairs]
        a_bk = [_dot3(a_h[c], btkt[chains[c][0]], _NT) for c in each]
        a_ab = [jnp.where(strict, m[:, :n], 0.0) for m in a_bk]
        a_ak = [jnp.where(strict, m[:, n:], 0.0) for m in a_bk]
        tinv = [eye + a for a in a_ab]
        x = a_ab
        for _ in range(RWKV_CHUNK.bit_length() - 2):
            x = [_dot1(xc, xc) for xc in x]
            tinv = [tc + _dot1(tc, xc) for tc, xc in zip(tinv, x)]
        akv = [_dot3(a_ak[c], v_h[c]) for c in each]
        m_bk = [_dot1(r_h[c], btkt[chains[c][0]], _NT) for c in each]
        m_rb = [jnp.where(incl, m[:, :n], 0.0) for m in m_bk]
        m_rk = [jnp.where(incl, m[:, n:], 0.0) for m in m_bk]
        wu = [_dot3(tinv[c], jnp.concatenate([a_h[c], akv[c]], axis=1)) for c in each]
        w_h = [m[:, :LANES] for m in wu]
        u0_h = [m[:, LANES:] for m in wu]
        ry = [_dot1(m_rb[c], wu[c]) for c in each]
        rh_h = [r_h[c] + ry[c][:, :LANES] for c in each]
        y0_h = [ry[c][:, LANES:] + _dot1(m_rk[c], v_h[c]) for c in each]

        for p in pairs:
            c0, c1 = 2 * p, 2 * p + 1
            rem = jnp.exp(tot[p] - cum[p])
            w_o[:, slabs[p]] = w_h[c0] + w_h[c1]
            rh_o[:, slabs[p]] = rh_h[c0] + rh_h[c1]
            u0_o[:, slabs[p]] = u0_h[c0] + u0_h[c1]
            y0_o[:, slabs[p]] = y0_h[c0] + y0_h[c1]
            bh_o[:, slabs[p]] = b[p] * rem
            kh_o[:, slabs[p]] = k[p] * rem
            pl_o[:, slabs[p]] = jnp.exp(tot[p])


def _chunk(ps, w0, a0, wlora, wg, k_k, k_a, r_k):
    t = ps.shape[0]
    const = lambda shape: pl.BlockSpec(shape, lambda i: (0, 0))
    spec = pl.BlockSpec((CHUNK_TILE, D_RWKV), lambda i: (i, 0))
    out = jax.ShapeDtypeStruct((t, D_RWKV), F32)
    return pl.pallas_call(
        _chunk_kernel,
        grid=(t // CHUNK_TILE,),
        in_specs=[pl.BlockSpec((CHUNK_TILE, N_SHIFT), lambda i: (i, 0)),
                  const((1, D_RWKV)), const((1, D_RWKV)),
                  const((LANES, 2 * D_RWKV)), const((D_GATE_LORA, D_RWKV)),
                  const((1, D_RWKV)), const((1, D_RWKV)), const((1, D_RWKV))],
        out_specs=[spec] * 10,
        out_shape=[out] * 10,
        compiler_params=_params(("arbitrary",)),
        name="rwkv_chunk_prep",
    )(ps, w0, a0, wlora, wg, k_k, k_a, r_k)


def _rwkv_kernel(w_ref, rh_ref, u0_ref, y0_ref, bh_ref, kh_ref, v_ref, pl_ref,
                 y_ref, state_ref):
    @pl.when(pl.program_id(0) == 0)
    def _():
        state_ref[...] = jnp.zeros_like(state_ref)

    nb, tt, _ = w_ref.shape
    L = RWKV_CHUNK
    depth = LANES * pl.cdiv(6 * L + SUBLANES, LANES)
    row_head = lax.broadcasted_iota(jnp.int32, (LANES, LANES), 0) // HEAD
    col_head = lax.broadcasted_iota(jnp.int32, (LANES, LANES), 1) // HEAD
    same_head = row_head == col_head
    pad = jnp.zeros((depth - 6 * L - SUBLANES, LANES), F32)
    zeros_tail = jnp.zeros((depth - 6 * L, LANES), BF16)

    def split(x):
        hi = x.astype(BF16)
        return hi, (x - hi.astype(F32)).astype(BF16)

    def chunk(ci, carry):
        t0 = pl.multiple_of(ci * L, L)
        streams = [(bi, p) for bi in range(nb) for p in range(N_RWKV_PAIRS)]

        def rows(ref, n=L):
            return [ref[bi, pl.ds(t0, n), LANES * p:LANES * (p + 1)] for bi, p in streams]

        each = range(len(streams))
        s = [state_ref[bi * N_RWKV_PAIRS + p] for bi, p in streams]
        w, rh, u0, y0, bh, kh, v = (rows(ref) for ref in
                                   (w_ref, rh_ref, u0_ref, y0_ref, bh_ref, kh_ref, v_ref))
        p_rows = rows(pl_ref, SUBLANES)
        left, p_col = [], []
        for c in each:
            (b_hi, b_lo), (k_hi, k_lo) = split(bh[c]), split(kh[c])
            block = jnp.concatenate([x.astype(F32) for x in (b_hi, b_lo, b_hi, k_hi, k_lo, k_hi)]
                                    + [p_rows[c], pad], axis=0).T
            left.append(block.astype(BF16))
            p_col.append(jnp.broadcast_to(block[:, 6 * L:6 * L + 1], (LANES, LANES)))
        g = []
        for c in each:
            (l_hi, l_lo), (s_hi, s_lo) = split(jnp.concatenate([w[c], rh[c]], axis=0)), split(s[c])
            g.append(jnp.dot(jnp.concatenate([l_hi, l_lo], axis=1),
                             jnp.concatenate([s_hi, s_hi], axis=0), preferred_element_type=F32)
                     + jnp.dot(l_hi, s_lo, preferred_element_type=F32))
        u = [g[c][:L] + u0[c] for c in each]
        upd = []
        for c in each:
            (u_hi, u_lo), (v_hi, v_lo) = split(u[c]), split(v[c])
            right = jnp.concatenate([u_hi, u_hi, u_lo, v_hi, v_hi, v_lo, zeros_tail], axis=0)
            upd.append(jnp.dot(left[c], right, preferred_element_type=F32))
        for c, (bi, p) in enumerate(streams):
            y_ref[bi, pl.ds(t0, L), LANES * p:LANES * (p + 1)] = g[c][L:] + y0[c]
            state_ref[bi * N_RWKV_PAIRS + p] = (p_col[c] * s[c]
                                                + jnp.where(same_head, upd[c], 0.0))
        return carry

    lax.fori_loop(0, tt // L, chunk, 0)


def _rwkv(w, rh, u0, y0, bh, kh, v, p_l, tt):
    nb, s, d = w.shape
    spec = pl.BlockSpec((nb, tt, d), lambda i: (0, i, 0))
    return pl.pallas_call(
        _rwkv_kernel,
        grid=(s // tt,),
        in_specs=[spec] * 8,
        out_specs=spec,
        out_shape=jax.ShapeDtypeStruct((nb, s, d), F32),
        scratch_shapes=[pltpu.VMEM((nb * N_RWKV_PAIRS, LANES, LANES), F32)],
        compiler_params=_params(("arbitrary",)),
        name="rwkv_scan",
    )(w, rh, u0, y0, bh, kh, v, p_l)


def _gelu_tanh(x):
    return 0.5 * x * (1.0 + jnp.tanh(0.7978845608028654 * (x + 0.044715 * (x * x * x))))


def _sgu_apply(pb, g_ref, b_ref, ws_ref, bias_ref, o_ref):
    z = _gelu_tanh(pb)
    u, v = z[:, :D_SGU], z[:, D_SGU:]
    mu = jnp.mean(v, axis=-1, keepdims=True)
    vc = v - mu
    var = jnp.mean(vc * vc, axis=-1, keepdims=True)
    vn = (vc * lax.rsqrt(var + LN_EPS) * g_ref[...] + b_ref[...]).astype(BF16)
    tpos = lax.broadcasted_iota(jnp.int32, (SGU_BLOCK, SGU_BLOCK), 0) // CHUNK
    spos = lax.broadcasted_iota(jnp.int32, (SGU_BLOCK, SGU_BLOCK), 1) // CHUNK
    causal = spos <= tpos
    first_head = lax.broadcasted_iota(jnp.int32, (SGU_BLOCK, LANES), 1) < HEAD
    for p in range(D_SGU // LANES):
        sl = slice(LANES * p, LANES * (p + 1))
        ws0 = jnp.where(causal, ws_ref[2 * p], 0.0).astype(BF16)
        ws1 = jnp.where(causal, ws_ref[2 * p + 1], 0.0).astype(BF16)
        for n in range(pb.shape[0] // SGU_BLOCK):
            rows = slice(SGU_BLOCK * n, SGU_BLOCK * (n + 1))
            vb = vn[rows, sl]
            sp = jnp.where(first_head,
                           jnp.dot(ws0, vb, preferred_element_type=F32),
                           jnp.dot(ws1, vb, preferred_element_type=F32))
            o_ref[rows, sl] = (u[rows, sl] * (sp + bias_ref[:, sl])).astype(o_ref.dtype)


def _mid_kernel(y_ref, g_ref, bonus_ref, lng_ref, lnb_ref, yb_ref, x_ref,
                g1_ref, sc2_ref, sh2_ref, g2_ref, n2_ref,
                woa_ref, wob_ref, wrt_ref, w1_ref, w3_ref, w2_ref,
                h2_ref, logit_ref, base_ref):
    ones = _head_ones()
    y = y_ref[...]
    mu = _head_sum(y, ones) * (1.0 / HEAD)
    yc = y - mu
    var = _head_sum(yc * yc, ones) * (1.0 / HEAD)
    ya = yc * lax.rsqrt(var + LN_X_EPS) * lng_ref[...] + lnb_ref[...]
    ya = ((ya + bonus_ref[...]) * g_ref[...]).astype(BF16)
    yo = (jnp.dot(ya, woa_ref[...], preferred_element_type=F32)
          + jnp.dot(yb_ref[...], wob_ref[...], preferred_element_type=F32))
    x1 = x_ref[...] + g1_ref[...] * yo
    h2 = x1 * lax.rsqrt(jnp.mean(x1 * x1, axis=-1, keepdims=True) + RMS_EPS) * n2_ref[...]
    h2 = h2 * (1.0 + sc2_ref[...]) + sh2_ref[...]
    h2b = h2.astype(BF16)
    h2_ref[...] = h2b
    logit_ref[...] = _dot3(wrt_ref[...], h2, dims=(((1,), (1,)), ((), ())))
    act = (_silu(jnp.dot(h2b, w1_ref[...], preferred_element_type=F32))
           * jnp.dot(h2b, w3_ref[...], preferred_element_type=F32)).astype(BF16)
    ms = jnp.dot(act, w2_ref[...], preferred_element_type=F32)
    base_ref[...] = x1 + g2_ref[...] * ms


def _mid(y, g, bonus, lnx_g, lnx_b, yb, x, g1, sc2, sh2, g2, n2g,
         wo_a, wo_b, wr_t, w1s, w3s, w2s, tm, tiles_per_batch):
    t, d = x.shape
    tok = lambda w: pl.BlockSpec((tm, w), lambda i: (i, 0))
    const = lambda shape: pl.BlockSpec(shape, lambda i: (0,) * len(shape))
    mod = pl.BlockSpec((None, 1, d), lambda i: (i // tiles_per_batch, 0, 0))
    return pl.pallas_call(
        _mid_kernel,
        grid=(t // tm,),
        in_specs=[tok(D_RWKV), tok(D_RWKV), tok(D_RWKV),
                  const((1, D_RWKV)), const((1, D_RWKV)),
                  tok(D_SGU), tok(d), mod, mod, mod, mod, const((1, d)),
                  const(wo_a.shape), const(wo_b.shape), const(wr_t.shape),
                  const(w1s.shape), const(w3s.shape), const(w2s.shape)],
        out_specs=[tok(d), pl.BlockSpec((LANES, tm), lambda i: (0, i)), tok(d)],
        out_shape=[jax.ShapeDtypeStruct((t, d), BF16),
                   jax.ShapeDtypeStruct((LANES, t), F32),
                   jax.ShapeDtypeStruct((t, d), F32)],
        compiler_params=_params(("arbitrary",)),
        name="outproj_norm_router_shared",
    )(y, g, bonus, lnx_g, lnx_b, yb, x, g1, sc2, sh2, g2, n2g,
      wo_a, wo_b, wr_t, w1s, w3s, w2s)


MOE_SUB = 128
MOE_CAP = 32
MOE_GROUP = 8


def _route_kernel(logit_ref, bias_ref, gate_ref, pos_ref, fexp_ref, fsub_ref, fdense_ref):
    tt = logit_ref.shape[1]
    scores = _sigmoid(logit_ref[0:N_EXPERTS, :])
    biased = scores + bias_ref[...]
    neg_inf = -jnp.inf

    group_blocks, group_scores = [], []
    for gi in range(N_GROUPS):
        blk = biased[GROUP_SIZE * gi:GROUP_SIZE * (gi + 1), :]
        m1 = jnp.max(blk, axis=0, keepdims=True)
        is_max = blk == m1
        n_max = jnp.sum(jnp.where(is_max, 1.0, 0.0), axis=0, keepdims=True)
        m2 = jnp.max(jnp.where(is_max, neg_inf, blk), axis=0, keepdims=True)
        group_blocks.append(blk)
        group_scores.append(m1 + jnp.where(n_max >= 2.0, m1, m2))

    masked = []
    for gi in range(N_GROUPS):
        rank = jnp.zeros((1, tt), F32)
        for gj in range(N_GROUPS):
            if gj == gi:
                continue
            ahead = (group_scores[gj] >= group_scores[gi]) if gj < gi else \
                    (group_scores[gj] > group_scores[gi])
            rank += jnp.where(ahead, 1.0, 0.0)
        masked.append(jnp.where(rank < float(TOPK_GROUPS), group_blocks[gi], neg_inf))
    sub = lax.broadcasted_iota(jnp.int32, (GROUP_SIZE, tt), 0)
    ranks = [jnp.zeros((GROUP_SIZE, tt), F32) for _ in range(N_GROUPS)]
    for ej in range(N_EXPERTS):
        gj, oj = divmod(ej, GROUP_SIZE)
        other = masked[gj][oj:oj + 1, :]
        for gi in range(N_GROUPS):
            blk = masked[gi]
            if gi > gj:
                ahead = jnp.where(other >= blk, 1.0, 0.0)
            elif gi < gj:
                ahead = jnp.where(other > blk, 1.0, 0.0)
            else:
                ahead = jnp.where(sub > oj, jnp.where(other >= blk, 1.0, 0.0),
                                  jnp.where(other > blk, 1.0, 0.0))
            ranks[gi] = ranks[gi] + ahead
    rank = jnp.concatenate(ranks, axis=0)
    chosen = jnp.where(rank < float(TOP_K), 1.0, 0.0)
    sw = chosen * scores
    gate_ref[...] = sw / jnp.sum(sw, axis=0, keepdims=True) * ROUTED_SCALE

    before = (lax.broadcasted_iota(jnp.int32, (MOE_SUB, MOE_SUB), 0)
              < lax.broadcasted_iota(jnp.int32, (MOE_SUB, MOE_SUB), 1))
    prefix_ones = jnp.where(before, 1.0, 0.0).astype(BF16)
    all_ones = jnp.ones((MOE_SUB, MOE_SUB), BF16)
    lane = lax.broadcasted_iota(jnp.int32, (N_EXPERTS, LANES), 1)
    positions, counts = [], jnp.zeros((N_EXPERTS, LANES), F32)
    for s in range(tt // MOE_SUB):
        blk = chosen[:, MOE_SUB * s:MOE_SUB * (s + 1)].astype(BF16)
        prefix = jnp.dot(blk, prefix_ones, preferred_element_type=F32)
        positions.append(jnp.where(blk > 0, prefix, -1.0))
        total = jnp.dot(blk, all_ones, preferred_element_type=F32)
        counts = counts + jnp.where(lane == s, total, 0.0)
    pos_ref[...] = jnp.concatenate(positions, axis=1)
    per_expert = jnp.max(counts, axis=1, keepdims=True)
    fexp_ref[...] = jnp.broadcast_to(jnp.where(per_expert > float(MOE_CAP), 1, 0),
                                     fexp_ref.shape).astype(jnp.int32)
    per_group = jnp.concatenate(
        [jnp.max(counts[MOE_GROUP * g:MOE_GROUP * (g + 1), :], axis=0, keepdims=True)
         for g in range(N_EXPERTS // MOE_GROUP)], axis=0)
    fsub_ref[...] = jnp.where(per_group > float(MOE_CAP), 1, 0).astype(jnp.int32)
    group_worst = jnp.max(per_group, axis=1, keepdims=True)
    fdense_ref[...] = jnp.broadcast_to(jnp.where(group_worst > float(2 * MOE_CAP), 1, 0),
                                       fdense_ref.shape).astype(jnp.int32)


def _route(logits_t, bias_tile, tt):
    t = logits_t.shape[1]
    n_tiles = t // tt
    n_groups = N_EXPERTS // MOE_GROUP
    tok = pl.BlockSpec((N_EXPERTS, tt), lambda i: (0, i))
    return pl.pallas_call(
        _route_kernel,
        grid=(n_tiles,),
        in_specs=[pl.BlockSpec((LANES, tt), lambda i: (0, i)),
                  pl.BlockSpec((N_EXPERTS, tt), lambda i: (0, 0))],
        out_specs=[tok, tok,
                   pl.BlockSpec((N_EXPERTS, LANES), lambda i: (i, 0)),
                   pl.BlockSpec((n_groups, LANES), lambda i: (i, 0)),
                   pl.BlockSpec((n_groups, LANES), lambda i: (i, 0))],
        out_shape=[jax.ShapeDtypeStruct((N_EXPERTS, t), F32),
                   jax.ShapeDtypeStruct((N_EXPERTS, t), F32),
                   jax.ShapeDtypeStruct((n_tiles * N_EXPERTS, LANES), jnp.int32),
                   jax.ShapeDtypeStruct((n_tiles * n_groups, LANES), jnp.int32),
                   jax.ShapeDtypeStruct((n_tiles * n_groups, LANES), jnp.int32)],
        compiler_params=_params(("arbitrary",)),
        name="route_topk",
    )(logits_t, bias_tile)


def _moe_kernel(fexp_ref, fsub_ref, fdense_ref, h2_ref, gate_ref, pos_ref,
                w1_ref, w3_ref, w2_ref, base_ref, g2_ref, nf_ref, o_ref, xy_ref, gb_ref, pt_ref):
    tile, eg = pl.program_id(0), pl.program_id(1)
    n_groups = pl.num_programs(1)
    tm = h2_ref.shape[0]
    n_sub = tm // MOE_SUB

    @pl.when(eg == 0)
    def _():
        o_ref[...] = jnp.zeros_like(o_ref)

    @pl.when((tile == 0) & (eg == 0))
    def _():
        xy_ref[...] = jnp.zeros_like(xy_ref)
        gb_ref[...] = jnp.zeros_like(gb_ref)

    def lane_broadcast(m):
        return jnp.broadcast_to(jnp.sum(m, axis=-1, keepdims=True), m.shape)

    def hidden(x, gate_rows, j):
        gate = jnp.concatenate([gate_rows] * (D_EXPERT // LANES), axis=1)
        return (_silu(jnp.dot(x, w1_ref[j], preferred_element_type=F32))
                * jnp.dot(x, w3_ref[j], preferred_element_type=F32) * gate).astype(BF16)

    def pipelined(n, first, second):
        pending = first(0)
        for i in range(1, n):
            nxt = first(i)
            second(i - 1, pending)
            pending = nxt
        second(n - 1, pending)

    cap = MOE_CAP
    slab_rows = n_sub * cap
    go_dense = fdense_ref[tile * n_groups + eg] != 0

    def needs_second(s):
        return fsub_ref[(tile * n_groups + eg) * n_sub + s] != 0

    def slab_slice(j, q, s=None):
        start = (2 * j + q) * slab_rows
        if s is None:
            return slice(start, start + slab_rows)
        return slice(start + s * cap, start + (s + 1) * cap)

    def gather(s, q):
        cols = slice(MOE_SUB * s, MOE_SUB * (s + 1))
        slot = lax.broadcasted_iota(jnp.int32, (cap, MOE_SUB), 0).astype(F32) + float(q * cap)
        pos, gate = pos_ref[:, cols], gate_ref[:, cols]
        hot = [jnp.where(slot == pos[j:j + 1, :], 1.0, 0.0) for j in range(MOE_GROUP)]
        p = jnp.concatenate(hot, axis=0)
        pg = jnp.concatenate([hot[j] * gate[j:j + 1, :] for j in range(MOE_GROUP)], axis=0)
        xg = jnp.dot(p.astype(BF16), h2_ref[cols, :],
                     preferred_element_type=F32).astype(BF16)
        gb = lane_broadcast(pg)
        for j in range(MOE_GROUP):
            xy_ref[slab_slice(j, q, s), :] = xg[j * cap:(j + 1) * cap]
            gb_ref[slab_slice(j, q, s), :] = gb[j * cap:(j + 1) * cap]
        pt_ref[s, :, q * MOE_GROUP * cap:(q + 1) * MOE_GROUP * cap] = p.T.astype(BF16)

    def scatter(s, q):
        y = jnp.concatenate([xy_ref[slab_slice(j, q, s), :] for j in range(MOE_GROUP)], axis=0)
        o_ref[MOE_SUB * s:MOE_SUB * (s + 1), :] += jnp.dot(
            pt_ref[s, :, q * MOE_GROUP * cap:(q + 1) * MOE_GROUP * cap], y,
            preferred_element_type=F32)

    def project(j, q, act):
        xy_ref[slab_slice(j, q), :] = jnp.dot(
            act, w2_ref[j], preferred_element_type=F32).astype(BF16)

    @pl.when(jnp.logical_not(go_dense))
    def _():
        for s in range(n_sub):
            gather(s, 0)
        pipelined(MOE_GROUP,
                  lambda j: hidden(xy_ref[slab_slice(j, 0), :], gb_ref[slab_slice(j, 0), :], j),
                  lambda j, act: project(j, 0, act))
        for s in range(n_sub):
            scatter(s, 0)
        for s in range(n_sub):
            pl.when(needs_second(s))(functools.partial(gather, s, 1))
        for j in range(MOE_GROUP):
            @pl.when(fexp_ref[tile * N_EXPERTS + eg * MOE_GROUP + j] != 0)
            def _(j=j):
                project(j, 1, hidden(xy_ref[slab_slice(j, 1), :],
                                     gb_ref[slab_slice(j, 1), :], j))
        for s in range(n_sub):
            pl.when(needs_second(s))(functools.partial(scatter, s, 1))

    @pl.when(go_dense)
    def _():
        eye = (lax.broadcasted_iota(jnp.int32, (MOE_SUB, MOE_SUB), 0)
               == lax.broadcasted_iota(jnp.int32, (MOE_SUB, MOE_SUB), 1))
        for j in range(MOE_GROUP):
            for s in range(n_sub):
                cols = slice(MOE_SUB * s, MOE_SUB * (s + 1))
                gb_ref[cols, :] = lane_broadcast(
                    jnp.where(eye, gate_ref[j:j + 1, cols], 0.0))
            act = hidden(h2_ref[...], gb_ref[0:tm, :], j)
            o_ref[...] += jnp.dot(act, w2_ref[j], preferred_element_type=F32)

    @pl.when(eg == n_groups - 1)
    def _():
        xo = base_ref[...] + g2_ref[...] * o_ref[...]
        o_ref[...] = (xo * lax.rsqrt(jnp.mean(xo * xo, axis=-1, keepdims=True) + RMS_EPS)
                      * nf_ref[...])


def _moe(fexp, fsub, fdense, h2, gates_t, pos_t, w1e, w3e, w2e, base, g2, nf, tm,
         tiles_per_batch):
    t, d = h2.shape
    ne = w1e.shape[0]
    assert ne % MOE_GROUP == 0 and tm % MOE_SUB == 0
    n_sub = tm // MOE_SUB
    rows = MOE_GROUP * 2 * n_sub * MOE_CAP
    assert rows >= tm, "gate scratch is reused by the dense path"
    tok = pl.BlockSpec((tm, d), lambda i, e, *_: (i, 0))
    per_expert = pl.BlockSpec((MOE_GROUP, tm), lambda i, e, *_: (e, i))
    return pl.pallas_call(
        _moe_kernel,
        grid_spec=pltpu.PrefetchScalarGridSpec(
            num_scalar_prefetch=3,
            grid=(t // tm, ne // MOE_GROUP),
            in_specs=[tok, per_expert, per_expert,
                      pl.BlockSpec((MOE_GROUP, d, D_EXPERT), lambda i, e, *_: (e, 0, 0)),
                      pl.BlockSpec((MOE_GROUP, d, D_EXPERT), lambda i, e, *_: (e, 0, 0)),
                      pl.BlockSpec((MOE_GROUP, D_EXPERT, d), lambda i, e, *_: (e, 0, 0)),
                      pl.BlockSpec((tm, d), lambda i, e, *_: (i, 0),
                                   pipeline_mode=pl.Buffered(1)),
                      pl.BlockSpec((None, 1, d), lambda i, e, *_: (i // tiles_per_batch, 0, 0)),
                      pl.BlockSpec((1, d), lambda i, e, *_: (0, 0))],
            out_specs=tok,
            scratch_shapes=[pltpu.VMEM((rows, d), BF16),
                            pltpu.VMEM((rows, LANES), F32),
                            pltpu.VMEM((n_sub, MOE_SUB, 2 * MOE_GROUP * MOE_CAP), BF16)]),
        out_shape=jax.ShapeDtypeStruct((t, d), F32),
        compiler_params=_params(("arbitrary", "arbitrary")),
        name="moe_experts_final",
    )(fexp, fsub, fdense, h2, gates_t, pos_t, w1e, w3e, w2e, base, g2, nf)


def _tile(n, want):
    t = min(n, want)
    assert n % t == 0, (n, t)
    return t


def kernel(x, c, w_ada, b_ada, norm1_g, w_in, mu_shift, w0, w_decay_up, a0, w_a_up, w_g_up,
           k_k, k_a, r_k, lnx_g, lnx_b, sgu_ln_g, sgu_ln_b, w_spatial, b_spatial, w_out,
           norm2_g, w_router, e_bias, w1_e, w3_e, w2_e, w1_s, w3_s, w2_s, norm_f_g):
    b, s, d = x.shape
    t = b * s
    assert w_ada.shape[0] == 1, "single-layer block"
    assert d == 2 * D_RWKV and s % SGU_BLOCK == 0
    row = lambda vec: vec.reshape(1, -1)

    c_pad = jnp.pad(c, ((0, 8 - b), (0, 0)))
    mod = _mod(c_pad, w_ada[0], b_ada)[:b]
    sh1, sc1, g1, sh2, sc2, g2 = [m[:, None, :] for m in jnp.split(mod, 6, axis=-1)]

    bias_tile = jnp.repeat(b_spatial[0].T, HEAD, axis=1)
    ps, yb = _inproj(x, sc1, sh1, norm1_g, w_in[0].astype(BF16), mu_shift,
                     sgu_ln_g, sgu_ln_b, w_spatial[0], bias_tile, _tile(s, TILE_INPROJ))

    wlora = jnp.zeros((LANES, 2 * D_RWKV), F32)
    wlora = wlora.at[:D_DECAY_LORA, :D_RWKV].set(w_decay_up[0])
    wlora = wlora.at[D_DECAY_LORA:, D_RWKV:].set(w_a_up[0])
    seq = lambda z: z.reshape(b, s, D_RWKV)
    w_c, rh_c, u0_c, y0_c, bh_c, kh_c, pl_c, v, g, bonus = _chunk(
        ps.reshape(t, N_SHIFT), w0, a0, wlora, w_g_up[0], k_k, k_a, row(r_k))
    y = _rwkv(seq(w_c), seq(rh_c), seq(u0_c), seq(y0_c), seq(bh_c), seq(kh_c), seq(v),
              seq(pl_c), _tile(s, TILE_SCAN))

    tm_mid = _tile(s, TILE_MID)
    wr_t = jnp.pad(w_router[0].T, ((0, LANES - N_EXPERTS), (0, 0)))
    wo = w_out[0].astype(BF16)
    h2, logits_t, base = _mid(
        y.reshape(t, D_RWKV), g, bonus, lnx_g, lnx_b, yb.reshape(t, D_SGU), x.reshape(t, d),
        g1, sc2, sh2, g2, norm2_g, wo[:D_RWKV], wo[D_RWKV:], wr_t,
        w1_s[0].astype(BF16), w3_s[0].astype(BF16), w2_s[0].astype(BF16), tm_mid, s // tm_mid)

    tm_moe = _tile(s, TILE_MOE)
    gates_t, pos_t, fexp, fsub, fdense = _route(
        logits_t, jnp.broadcast_to(e_bias[0][:, None], (N_EXPERTS, tm_moe)), tm_moe)

    out = _moe(fexp[:, 0], fsub[:, :tm_moe // MOE_SUB].reshape(-1), fdense[:, 0],
               h2, gates_t, pos_t, w1_e[0].astype(BF16), w3_e[0].astype(BF16),
               w2_e[0].astype(BF16), base, g2, row(norm_f_g), tm_moe, s // tm_moe)
    return out.reshape(b, s, d)
```

```python
import functools

import jax
import jax.numpy as jnp
from jax import lax
from jax.experimental import pallas as pl
from jax.experimental.pallas import tpu as pltpu

F32 = jnp.float32
BF16 = jnp.bfloat16

HEAD = 64
LANES = 128
SUBLANES = 8
D_RWKV = 512
D_SGU = 512
N_RWKV_PAIRS = D_RWKV // LANES
D_DECAY_LORA = 64
D_AAA_LORA = 64
D_GATE_LORA = 128
N_SHIFT = 3 * D_RWKV + D_DECAY_LORA + D_AAA_LORA + D_GATE_LORA
SGU_BLOCK = 128
CHUNK = 64
N_EXPERTS = 64
N_GROUPS = 8
GROUP_SIZE = N_EXPERTS // N_GROUPS
TOPK_GROUPS = 4
TOP_K = 8
D_EXPERT = 256
ROUTED_SCALE = 2.5
RMS_EPS = 1e-6
LN_EPS = 1e-5
LN_X_EPS = 64e-5
VMEM_LIMIT_BYTES = 56 * 1024 * 1024

TILE_MOD_COLS = 1024
TILE_INPROJ = 512
TILE_SCAN = 512
TILE_MID = 512
TILE_MOE = 1024
TILE_FINAL = 512


def _params(semantics):
    return pltpu.CompilerParams(dimension_semantics=semantics,
                                vmem_limit_bytes=VMEM_LIMIT_BYTES)


def _dot3(a, b, dims=(((1,), (0,)), ((), ()))):
    ((ca,), (cb,)), _ = dims
    a_hi = a.astype(BF16)
    a_lo = (a - a_hi.astype(F32)).astype(BF16)
    b_hi = b.astype(BF16)
    b_lo = (b - b_hi.astype(F32)).astype(BF16)
    dg = functools.partial(lax.dot_general, dimension_numbers=dims,
                           preferred_element_type=F32)
    return (dg(jnp.concatenate([a_hi, a_lo], axis=ca), jnp.concatenate([b_hi, b_hi], axis=cb))
            + dg(a_hi, b_lo))


def _head_ones():
    r = lax.broadcasted_iota(jnp.int32, (LANES, LANES), 0) // HEAD
    c = lax.broadcasted_iota(jnp.int32, (LANES, LANES), 1) // HEAD
    return jnp.where(r == c, 1.0, 0.0).astype(BF16)


def _head_sum(z, ones):
    both = jnp.concatenate([ones, ones], axis=0)
    parts = []
    for p in range(z.shape[1] // LANES):
        zp = z[:, LANES * p:LANES * (p + 1)]
        hi = zp.astype(BF16)
        lo = (zp - hi.astype(F32)).astype(BF16)
        parts.append(jnp.dot(jnp.concatenate([hi, lo], axis=1), both,
                             preferred_element_type=F32))
    return jnp.concatenate(parts, axis=-1)


def _sigmoid(x):
    return 1.0 / (1.0 + jnp.exp(-x))


def _silu(x):
    return x * _sigmoid(x)


def _mod_kernel(c_ref, w_ref, b_ref, o_ref):
    s = _silu(c_ref[...])
    o_ref[...] = _dot3(s, w_ref[...]) + b_ref[...]


def _mod(c_pad, w_ada, b_ada):
    rows, d = c_pad.shape
    n = w_ada.shape[1]
    tn = TILE_MOD_COLS
    return pl.pallas_call(
        _mod_kernel,
        grid=(n // tn,),
        in_specs=[pl.BlockSpec((rows, d), lambda j: (0, 0)),
                  pl.BlockSpec((d, tn), lambda j: (0, j)),
                  pl.BlockSpec((1, tn), lambda j: (0, j))],
        out_specs=pl.BlockSpec((rows, tn), lambda j: (0, j)),
        out_shape=jax.ShapeDtypeStruct((rows, n), F32),
        compiler_params=_params(("arbitrary",)),
        name="adaln_mod",
    )(c_pad, w_ada, b_ada)


def _inproj_kernel(x_ref, sc_ref, sh_ref, g_ref, w_ref, mu_ref,
                   sgu_g_ref, sgu_b_ref, ws_ref, sgu_bias_ref, ps_ref, yb_ref, carry_ref):
    @pl.when(pl.program_id(1) == 0)
    def _():
        carry_ref[...] = jnp.zeros_like(carry_ref)

    x = x_ref[...]
    tm = x.shape[0]
    h = x * lax.rsqrt(jnp.mean(x * x, axis=-1, keepdims=True) + RMS_EPS) * g_ref[...]
    h = h * (1.0 + sc_ref[...]) + sh_ref[...]
    proj = jnp.dot(h.astype(BF16), w_ref[...], preferred_element_type=F32)
    ps = proj[:, :N_SHIFT]
    prev = pltpu.roll(ps, 1, 0)
    first = lax.broadcasted_iota(jnp.int32, ps.shape, 0) == 0
    prev = jnp.where(first, carry_ref[0:1, :], prev)
    carry_ref[0:1, :] = ps[tm - 1:tm, :]
    ps_ref[...] = ps + (prev - ps) * mu_ref[...]
    _sgu_apply(proj[:, N_SHIFT:], sgu_g_ref, sgu_b_ref, ws_ref, sgu_bias_ref, yb_ref)


def _inproj(x, sc1, sh1, g, w_in_bf16, mu, sgu_g, sgu_b, w_spatial, sgu_bias, tm):
    b, s, d = x.shape
    d_in = w_in_bf16.shape[1]
    assert d_in == N_SHIFT + 2 * D_SGU and tm % SGU_BLOCK == 0
    row = lambda bi, i: (bi, 0, 0)
    const = lambda shape: pl.BlockSpec(shape, lambda bi, i: (0,) * len(shape))
    return pl.pallas_call(
        _inproj_kernel,
        grid=(b, s // tm),
        in_specs=[pl.BlockSpec((None, tm, d), lambda bi, i: (bi, i, 0)),
                  pl.BlockSpec((None, 1, d), row),
                  pl.BlockSpec((None, 1, d), row),
                  const((1, d)), const((d, d_in)), const((1, N_SHIFT)),
                  const((1, D_SGU)), const((1, D_SGU)), const(w_spatial.shape),
                  const((SGU_BLOCK, D_SGU))],
        out_specs=[pl.BlockSpec((None, tm, N_SHIFT), lambda bi, i: (bi, i, 0)),
                   pl.BlockSpec((None, tm, D_SGU), lambda bi, i: (bi, i, 0))],
        out_shape=[jax.ShapeDtypeStruct((b, s, N_SHIFT), F32),
                   jax.ShapeDtypeStruct((b, s, D_SGU), BF16)],
        scratch_shapes=[pltpu.VMEM((8, N_SHIFT), F32)],
        compiler_params=_params(("arbitrary", "arbitrary")),
        name="inproj_shift_sgu",
    )(x, sc1, sh1, g, w_in_bf16, mu, sgu_g, sgu_b, w_spatial, sgu_bias)


def _token_terms(ps, w0_ref, a0_ref, wlora_ref, wg_ref, kk_ref, ka_ref, rk_ref):
    c1, c2, c3 = D_RWKV, 2 * D_RWKV, 3 * D_RWKV
    r, k, v = ps[:, :c1], ps[:, c1:c2], ps[:, c2:c3]
    xwa = ps[:, c3:c3 + LANES]
    xg = ps[:, c3 + LANES:]
    is_w = lax.broadcasted_iota(jnp.int32, xwa.shape, 1) < D_DECAY_LORA
    lora = _dot3(jnp.where(is_w, jnp.tanh(xwa), xwa), wlora_ref[...])
    wlog = -jax.nn.softplus(-(w0_ref[...] + lora[:, :c1])) - 0.5
    log_decay = -jnp.exp(wlog)
    a = _sigmoid(a0_ref[...] + lora[:, c1:])
    g = _dot3(_sigmoid(xg), wg_ref[...])
    ones = _head_ones()
    kk = k * kk_ref[...]
    kk = kk / jnp.maximum(jnp.sqrt(_head_sum(kk * kk, ones)), 1e-12)
    kh = k * (1.0 + (a - 1.0) * ka_ref[...])
    bonus = _head_sum(r * kh * rk_ref[...], ones) * v
    return r, log_decay, kh, v, kk, kk * a, g, bonus


RWKV_CHUNK = 32
CHUNK_TILE = 128
CHUNK_GROUP_PAIRS = 4

_NN = (((1,), (0,)), ((), ()))
_NT = (((1,), (1,)), ((), ()))


def _dot1(a, b, dims=_NN):
    return lax.dot_general(a.astype(BF16), b.astype(BF16), dims, preferred_element_type=F32)


def _dot_exact_lhs(lhs_bf16, x):
    hi = x.astype(BF16)
    rest = x - hi.astype(F32)
    mid = rest.astype(BF16)
    lo = (rest - mid.astype(F32)).astype(BF16)
    return jnp.dot(jnp.concatenate([lhs_bf16] * 3, axis=1),
                   jnp.concatenate([hi, mid, lo], axis=0), preferred_element_type=F32)


def _chunk_kernel(ps_ref, w0_ref, a0_ref, wlora_ref, wg_ref, kk_ref, ka_ref, rk_ref,
                  w_o, rh_o, u0_o, y0_o, bh_o, kh_o, pl_o, v_o, g_o, bonus_o):
    r_all, lw_all, k_all, v_all, kk_all, b_all, g_all, bonus_all = _token_terms(
        ps_ref[...], w0_ref, a0_ref, wlora_ref, wg_ref, kk_ref, ka_ref, rk_ref)
    v_o[...] = v_all
    g_o[...] = g_all
    bonus_o[...] = bonus_all
    n = CHUNK_TILE
    tpos = lax.broadcasted_iota(jnp.int32, (n, n), 0)
    spos = lax.broadcasted_iota(jnp.int32, (n, n), 1)
    same = (tpos // RWKV_CHUNK) == (spos // RWKV_CHUNK)
    incl = same & (spos <= tpos)
    strict = same & (spos < tpos)
    tri_ones = jnp.where(incl, 1.0, 0.0).astype(BF16)
    blk_ones = jnp.where(same, 1.0, 0.0).astype(BF16)
    eye = jnp.where(tpos == spos, 1.0, 0.0)
    lane_head = lax.broadcasted_iota(jnp.int32, (n, LANES), 1) // HEAD

    for first in range(0, N_RWKV_PAIRS, CHUNK_GROUP_PAIRS):
        pairs = range(CHUNK_GROUP_PAIRS)
        slabs = [slice(LANES * (first + p), LANES * (first + p + 1)) for p in pairs]
        lw, r, k, v, kk, b = ([x[:, sl] for sl in slabs]
                              for x in (lw_all, r_all, k_all, v_all, kk_all, b_all))
        cum = [_dot_exact_lhs(tri_ones, x) for x in lw]
        tot = [_dot_exact_lhs(blk_ones, x) for x in lw]
        at = [-kk[p] * jnp.exp(cum[p] - lw[p]) for p in pairs]
        rt = [r[p] * jnp.exp(cum[p]) for p in pairs]
        inv = [jnp.exp(-cum[p]) for p in pairs]
        bt = [b[p] * inv[p] for p in pairs]
        kt = [k[p] * inv[p] for p in pairs]

        chains = [(p, h) for p in pairs for h in range(LANES // HEAD)]
        mine = [lane_head == h for h in range(LANES // HEAD)]
        a_h = [jnp.where(mine[h], at[p], 0.0) for p, h in chains]
        r_h = [jnp.where(mine[h], rt[p], 0.0) for p, h in chains]
        v_h = [jnp.where(mine[h], v[p], 0.0) for p, h in chains]
        each = range(len(chains))
        btkt = [jnp.concatenate([bt[p], kt[p]], axis=0) for p in pairs]
        a_bk = [_dot3(a_h[c], btkt[chains[c][0]], _NT) for c in each]
        a_ab = [jnp.where(strict, m[:, :n], 0.0) for m in a_bk]
        a_ak = [jnp.where(strict, m[:, n:], 0.0) for m in a_bk]
        tinv = [eye + a for a in a_ab]
        x = a_ab
        for _ in range(RWKV_CHUNK.bit_length() - 2):
            x = [_dot1(xc, xc) for xc in x]
            tinv = [tc + _dot1(tc, xc) for tc, xc in zip(tinv, x)]
        akv = [_dot3(a_ak[c], v_h[c]) for c in each]
        m_bk = [_dot1(r_h[c], btkt[chains[c][0]], _NT) for c in each]
        m_rb = [jnp.where(incl, m[:, :n], 0.0) for m in m_bk]
        m_rk = [jnp.where(incl, m[:, n:], 0.0) for m in m_bk]
        wu = [_dot3(tinv[c], jnp.concatenate([a_h[c], akv[c]], axis=1)) for c in each]
        w_h = [m[:, :LANES] for m in wu]
        u0_h = [m[:, LANES:] for m in wu]
        ry = [_dot1(m_rb[c], wu[c]) for c in each]
        rh_h = [r_h[c] + ry[c][:, :LANES] for c in each]
        y0_h = [ry[c][:, LANES:] + _dot1(m_rk[c], v_h[c]) for c in each]

        for p in pairs:
            c0, c1 = 2 * p, 2 * p + 1
            rem = jnp.exp(tot[p] - cum[p])
            w_o[:, slabs[p]] = w_h[c0] + w_h[c1]
            rh_o[:, slabs[p]] = rh_h[c0] + rh_h[c1]
            u0_o[:, slabs[p]] = u0_h[c0] + u0_h[c1]
            y0_o[:, slabs[p]] = y0_h[c0] + y0_h[c1]
            bh_o[:, slabs[p]] = b[p] * rem
            kh_o[:, slabs[p]] = k[p] * rem
            pl_o[:, slabs[p]] = jnp.exp(tot[p])


def _chunk(ps, w0, a0, wlora, wg, k_k, k_a, r_k):
    t = ps.shape[0]
    const = lambda shape: pl.BlockSpec(shape, lambda i: (0, 0))
    spec = pl.BlockSpec((CHUNK_TILE, D_RWKV), lambda i: (i, 0))
    out = jax.ShapeDtypeStruct((t, D_RWKV), F32)
    return pl.pallas_call(
        _chunk_kernel,
        grid=(t // CHUNK_TILE,),
        in_specs=[pl.BlockSpec((CHUNK_TILE, N_SHIFT), lambda i: (i, 0)),
                  const((1, D_RWKV)), const((1, D_RWKV)),
                  const((LANES, 2 * D_RWKV)), const((D_GATE_LORA, D_RWKV)),
                  const((1, D_RWKV)), const((1, D_RWKV)), const((1, D_RWKV))],
        out_specs=[spec] * 10,
        out_shape=[out] * 10,
        compiler_params=_params(("arbitrary",)),
        name="rwkv_chunk_prep",
    )(ps, w0, a0, wlora, wg, k_k, k_a, r_k)


def _rwkv_kernel(w_ref, rh_ref, u0_ref, y0_ref, bh_ref, kh_ref, v_ref, pl_ref,
                 y_ref, state_ref):
    @pl.when(pl.program_id(0) == 0)
    def _():
        state_ref[...] = jnp.zeros_like(state_ref)

    nb, tt, _ = w_ref.shape
    L = RWKV_CHUNK
    depth = LANES * pl.cdiv(6 * L + SUBLANES, LANES)
    row_head = lax.broadcasted_iota(jnp.int32, (LANES, LANES), 0) // HEAD
    col_head = lax.broadcasted_iota(jnp.int32, (LANES, LANES), 1) // HEAD
    same_head = row_head == col_head
    pad = jnp.zeros((depth - 6 * L - SUBLANES, LANES), F32)
    zeros_tail = jnp.zeros((depth - 6 * L, LANES), BF16)

    def split(x):
        hi = x.astype(BF16)
        return hi, (x - hi.astype(F32)).astype(BF16)

    def chunk(ci, carry):
        t0 = pl.multiple_of(ci * L, L)
        streams = [(bi, p) for bi in range(nb) for p in range(N_RWKV_PAIRS)]

        def rows(ref, n=L):
            return [ref[bi, pl.ds(t0, n), LANES * p:LANES * (p + 1)] for bi, p in streams]

        each = range(len(streams))
        s = [state_ref[bi * N_RWKV_PAIRS + p] for bi, p in streams]
        w, rh, u0, y0, bh, kh, v = (rows(ref) for ref in
                                   (w_ref, rh_ref, u0_ref, y0_ref, bh_ref, kh_ref, v_ref))
        p_rows = rows(pl_ref, SUBLANES)
        left, p_col = [], []
        for c in each:
            (b_hi, b_lo), (k_hi, k_lo) = split(bh[c]), split(kh[c])
            block = jnp.concatenate([x.astype(F32) for x in (b_hi, b_lo, b_hi, k_hi, k_lo, k_hi)]
                                    + [p_rows[c], pad], axis=0).T
            left.append(block.astype(BF16))
            p_col.append(jnp.broadcast_to(block[:, 6 * L:6 * L + 1], (LANES, LANES)))
        g = []
        for c in each:
            (l_hi, l_lo), (s_hi, s_lo) = split(jnp.concatenate([w[c], rh[c]], axis=0)), split(s[c])
            g.append(jnp.dot(jnp.concatenate([l_hi, l_lo], axis=1),
                             jnp.concatenate([s_hi, s_hi], axis=0), preferred_element_type=F32)
                     + jnp.dot(l_hi, s_lo, preferred_element_type=F32))
        u = [g[c][:L] + u0[c] for c in each]
        upd = []
        for c in each:
            (u_hi, u_lo), (v_hi, v_lo) = split(u[c]), split(v[c])
            right = jnp.concatenate([u_hi, u_hi, u_lo, v_hi, v_hi, v_lo, zeros_tail], axis=0)
            upd.append(jnp.dot(left[c], right, preferred_element_type=F32))
        for c, (bi, p) in enumerate(streams):
            y_ref[bi, pl.ds(t0, L), LANES * p:LANES * (p + 1)] = g[c][L:] + y0[c]
            state_ref[bi * N_RWKV_PAIRS + p] = (p_col[c] * s[c]
                                                + jnp.where(same_head, upd[c], 0.0))
        return carry

    lax.fori_loop(0, tt // L, chunk, 0)


def _rwkv(w, rh, u0, y0, bh, kh, v, p_l, tt):
    nb, s, d = w.shape
    spec = pl.BlockSpec((nb, tt, d), lambda i: (0, i, 0))
    return pl.pallas_call(
        _rwkv_kernel,
        grid=(s // tt,),
        in_specs=[spec] * 8,
        out_specs=spec,
        out_shape=jax.ShapeDtypeStruct((nb, s, d), F32),
        scratch_shapes=[pltpu.VMEM((nb * N_RWKV_PAIRS, LANES, LANES), F32)],
        compiler_params=_params(("arbitrary",)),
        name="rwkv_scan",
    )(w, rh, u0, y0, bh, kh, v, p_l)


def _gelu_tanh(x):
    return 0.5 * x * (1.0 + jnp.tanh(0.7978845608028654 * (x + 0.044715 * (x * x * x))))


def _sgu_apply(pb, g_ref, b_ref, ws_ref, bias_ref, o_ref):
    z = _gelu_tanh(pb)
    u, v = z[:, :D_SGU], z[:, D_SGU:]
    mu = jnp.mean(v, axis=-1, keepdims=True)
    vc = v - mu
    var = jnp.mean(vc * vc, axis=-1, keepdims=True)
    vn = (vc * lax.rsqrt(var + LN_EPS) * g_ref[...] + b_ref[...]).astype(BF16)
    tpos = lax.broadcasted_iota(jnp.int32, (SGU_BLOCK, SGU_BLOCK), 0) // CHUNK
    spos = lax.broadcasted_iota(jnp.int32, (SGU_BLOCK, SGU_BLOCK), 1) // CHUNK
    causal = spos <= tpos
    first_head = lax.broadcasted_iota(jnp.int32, (SGU_BLOCK, LANES), 1) < HEAD
    for p in range(D_SGU // LANES):
        sl = slice(LANES * p, LANES * (p + 1))
        ws0 = jnp.where(causal, ws_ref[2 * p], 0.0).astype(BF16)
        ws1 = jnp.where(causal, ws_ref[2 * p + 1], 0.0).astype(BF16)
        for n in range(pb.shape[0] // SGU_BLOCK):
            rows = slice(SGU_BLOCK * n, SGU_BLOCK * (n + 1))
            vb = vn[rows, sl]
            sp = jnp.where(first_head,
                           jnp.dot(ws0, vb, preferred_element_type=F32),
                           jnp.dot(ws1, vb, preferred_element_type=F32))
            o_ref[rows, sl] = (u[rows, sl] * (sp + bias_ref[:, sl])).astype(o_ref.dtype)


def _mid_kernel(y_ref, g_ref, bonus_ref, lng_ref, lnb_ref, yb_ref, x_ref,
                g1_ref, sc2_ref, sh2_ref, g2_ref, n2_ref,
                woa_ref, wob_ref, wrt_ref, w1_ref, w3_ref, w2_ref,
                h2_ref, logit_ref, base_ref):
    ones = _head_ones()
    y = y_ref[...]
    mu = _head_sum(y, ones) * (1.0 / HEAD)
    yc = y - mu
    var = _head_sum(yc * yc, ones) * (1.0 / HEAD)
    ya = yc * lax.rsqrt(var + LN_X_EPS) * lng_ref[...] + lnb_ref[...]
    ya = ((ya + bonus_ref[...]) * g_ref[...]).astype(BF16)
    yo = (jnp.dot(ya, woa_ref[...], preferred_element_type=F32)
          + jnp.dot(yb_ref[...], wob_ref[...], preferred_element_type=F32))
    x1 = x_ref[...] + g1_ref[...] * yo
    h2 = x1 * lax.rsqrt(jnp.mean(x1 * x1, axis=-1, keepdims=True) + RMS_EPS) * n2_ref[...]
    h2 = h2 * (1.0 + sc2_ref[...]) + sh2_ref[...]
    h2b = h2.astype(BF16)
    h2_ref[...] = h2b
    logit_ref[...] = _dot3(wrt_ref[...], h2, dims=(((1,), (1,)), ((), ())))
    act = (_silu(jnp.dot(h2b, w1_ref[...], preferred_element_type=F32))
           * jnp.dot(h2b, w3_ref[...], preferred_element_type=F32)).astype(BF16)
    ms = jnp.dot(act, w2_ref[...], preferred_element_type=F32)
    base_ref[...] = x1 + g2_ref[...] * ms


def _mid(y, g, bonus, lnx_g, lnx_b, yb, x, g1, sc2, sh2, g2, n2g,
         wo_a, wo_b, wr_t, w1s, w3s, w2s, tm, tiles_per_batch):
    t, d = x.shape
    tok = lambda w: pl.BlockSpec((tm, w), lambda i: (i, 0))
    const = lambda shape: pl.BlockSpec(shape, lambda i: (0,) * len(shape))
    mod = pl.BlockSpec((None, 1, d), lambda i: (i // tiles_per_batch, 0, 0))
    return pl.pallas_call(
        _mid_kernel,
        grid=(t // tm,),
        in_specs=[tok(D_RWKV), tok(D_RWKV), tok(D_RWKV),
                  const((1, D_RWKV)), const((1, D_RWKV)),
                  tok(D_SGU), tok(d), mod, mod, mod, mod, const((1, d)),
                  const(wo_a.shape), const(wo_b.shape), const(wr_t.shape),
                  const(w1s.shape), const(w3s.shape), const(w2s.shape)],
        out_specs=[tok(d), pl.BlockSpec((LANES, tm), lambda i: (0, i)), tok(d)],
        out_shape=[jax.ShapeDtypeStruct((t, d), BF16),
                   jax.ShapeDtypeStruct((LANES, t), F32),
                   jax.ShapeDtypeStruct((t, d), F32)],
        compiler_params=_params(("arbitrary",)),
        name="outproj_norm_router_shared",
    )(y, g, bonus, lnx_g, lnx_b, yb, x, g1, sc2, sh2, g2, n2g,
      wo_a, wo_b, wr_t, w1s, w3s, w2s)


MOE_SUB = 128
MOE_CAP = 32
MOE_GROUP = 8


def _route_kernel(logit_ref, bias_ref, gate_ref, pos_ref, fexp_ref, fsub_ref, fdense_ref):
    tt = logit_ref.shape[1]
    scores = _sigmoid(logit_ref[0:N_EXPERTS, :])
    biased = scores + bias_ref[...]
    neg_inf = -jnp.inf

    group_blocks, group_scores = [], []
    for gi in range(N_GROUPS):
        blk = biased[GROUP_SIZE * gi:GROUP_SIZE * (gi + 1), :]
        m1 = jnp.max(blk, axis=0, keepdims=True)
        is_max = blk == m1
        n_max = jnp.sum(jnp.where(is_max, 1.0, 0.0), axis=0, keepdims=True)
        m2 = jnp.max(jnp.where(is_max, neg_inf, blk), axis=0, keepdims=True)
        group_blocks.append(blk)
        group_scores.append(m1 + jnp.where(n_max >= 2.0, m1, m2))

    masked = []
    for gi in range(N_GROUPS):
        rank = jnp.zeros((1, tt), F32)
        for gj in range(N_GROUPS):
            if gj == gi:
                continue
            ahead = (group_scores[gj] >= group_scores[gi]) if gj < gi else \
                    (group_scores[gj] > group_scores[gi])
            rank += jnp.where(ahead, 1.0, 0.0)
        masked.append(jnp.where(rank < float(TOPK_GROUPS), group_blocks[gi], neg_inf))
    sub = lax.broadcasted_iota(jnp.int32, (GROUP_SIZE, tt), 0)
    ranks = [jnp.zeros((GROUP_SIZE, tt), F32) for _ in range(N_GROUPS)]
    for ej in range(N_EXPERTS):
        gj, oj = divmod(ej, GROUP_SIZE)
        other = masked[gj][oj:oj + 1, :]
        for gi in range(N_GROUPS):
            blk = masked[gi]
            if gi > gj:
                ahead = jnp.where(other >= blk, 1.0, 0.0)
            elif gi < gj:
                ahead = jnp.where(other > blk, 1.0, 0.0)
            else:
                ahead = jnp.where(sub > oj, jnp.where(other >= blk, 1.0, 0.0),
                                  jnp.where(other > blk, 1.0, 0.0))
            ranks[gi] = ranks[gi] + ahead
    rank = jnp.concatenate(ranks, axis=0)
    chosen = jnp.where(rank < float(TOP_K), 1.0, 0.0)
    sw = chosen * scores
    gate_ref[...] = sw / jnp.sum(sw, axis=0, keepdims=True) * ROUTED_SCALE

    before = (lax.broadcasted_iota(jnp.int32, (MOE_SUB, MOE_SUB), 0)
              < lax.broadcasted_iota(jnp.int32, (MOE_SUB, MOE_SUB), 1))
    prefix_ones = jnp.where(before, 1.0, 0.0).astype(BF16)
    all_ones = jnp.ones((MOE_SUB, MOE_SUB), BF16)
    lane = lax.broadcasted_iota(jnp.int32, (N_EXPERTS, LANES), 1)
    positions, counts = [], jnp.zeros((N_EXPERTS, LANES), F32)
    for s in range(tt // MOE_SUB):
        blk = chosen[:, MOE_SUB * s:MOE_SUB * (s + 1)].astype(BF16)
        prefix = jnp.dot(blk, prefix_ones, preferred_element_type=F32)
        positions.append(jnp.where(blk > 0, prefix, -1.0))
        total = jnp.dot(blk, all_ones, preferred_element_type=F32)
        counts = counts + jnp.where(lane == s, total, 0.0)
    pos_ref[...] = jnp.concatenate(positions, axis=1)
    per_expert = jnp.max(counts, axis=1, keepdims=True)
    fexp_ref[...] = jnp.broadcast_to(jnp.where(per_expert > float(MOE_CAP), 1, 0),
                                     fexp_ref.shape).astype(jnp.int32)
    per_group = jnp.concatenate(
        [jnp.max(counts[MOE_GROUP * g:MOE_GROUP * (g + 1), :], axis=0, keepdims=True)
         for g in range(N_EXPERTS // MOE_GROUP)], axis=0)
    fsub_ref[...] = jnp.where(per_group > float(MOE_CAP), 1, 0).astype(jnp.int32)
    group_worst = jnp.max(per_group, axis=1, keepdims=True)
    fdense_ref[...] = jnp.broadcast_to(jnp.where(group_worst > float(2 * MOE_CAP), 1, 0),
                                       fdense_ref.shape).astype(jnp.int32)


def _route(logits_t, bias_tile, tt):
    t = logits_t.shape[1]
    n_tiles = t // tt
    n_groups = N_EXPERTS // MOE_GROUP
    tok = pl.BlockSpec((N_EXPERTS, tt), lambda i: (0, i))
    return pl.pallas_call(
        _route_kernel,
        grid=(n_tiles,),
        in_specs=[pl.BlockSpec((LANES, tt), lambda i: (0, i)),
                  pl.BlockSpec((N_EXPERTS, tt), lambda i: (0, 0))],
        out_specs=[tok, tok,
                   pl.BlockSpec((N_EXPERTS, LANES), lambda i: (i, 0)),
                   pl.BlockSpec((n_groups, LANES), lambda i: (i, 0)),
                   pl.BlockSpec((n_groups, LANES), lambda i: (i, 0))],
        out_shape=[jax.ShapeDtypeStruct((N_EXPERTS, t), F32),
                   jax.ShapeDtypeStruct((N_EXPERTS, t), F32),
                   jax.ShapeDtypeStruct((n_tiles * N_EXPERTS, LANES), jnp.int32),
                   jax.ShapeDtypeStruct((n_tiles * n_groups, LANES), jnp.int32),
                   jax.ShapeDtypeStruct((n_tiles * n_groups, LANES), jnp.int32)],
        compiler_params=_params(("arbitrary",)),
        name="route_topk",
    )(logits_t, bias_tile)


def _moe_kernel(fexp_ref, fsub_ref, fdense_ref, h2_ref, gate_ref, pos_ref,
                w1_ref, w3_ref, w2_ref, o_ref, xy_ref, gb_ref, pt_ref):
    tile, eg = pl.program_id(0), pl.program_id(1)
    n_groups = pl.num_programs(1)
    tm = h2_ref.shape[0]
    n_sub = tm // MOE_SUB

    @pl.when(eg == 0)
    def _():
        o_ref[...] = jnp.zeros_like(o_ref)

    @pl.when((tile == 0) & (eg == 0))
    def _():
        xy_ref[...] = jnp.zeros_like(xy_ref)
        gb_ref[...] = jnp.zeros_like(gb_ref)

    def lane_broadcast(m):
        return jnp.broadcast_to(jnp.sum(m, axis=-1, keepdims=True), (m.shape[0], D_EXPERT))

    def hidden(x, gate_rows, j):
        return (_silu(jnp.dot(x, w1_ref[j], preferred_element_type=F32))
                * jnp.dot(x, w3_ref[j], preferred_element_type=F32) * gate_rows).astype(BF16)

    def pipelined(n, first, second):
        pending = first(0)
        for i in range(1, n):
            nxt = first(i)
            second(i - 1, pending)
            pending = nxt
        second(n - 1, pending)

    cap = MOE_CAP
    slab_rows = n_sub * cap
    go_dense = fdense_ref[tile * n_groups + eg] != 0

    def needs_second(s):
        return fsub_ref[(tile * n_groups + eg) * n_sub + s] != 0

    def slab_slice(j, q, s=None):
        start = (2 * j + q) * slab_rows
        if s is None:
            return slice(start, start + slab_rows)
        return slice(start + s * cap, start + (s + 1) * cap)

    def gather(s, q):
        cols = slice(MOE_SUB * s, MOE_SUB * (s + 1))
        slot = lax.broadcasted_iota(jnp.int32, (cap, MOE_SUB), 0).astype(F32) + float(q * cap)
        pos, gate = pos_ref[:, cols], gate_ref[:, cols]
        hot = [jnp.where(slot == pos[j:j + 1, :], 1.0, 0.0) for j in range(MOE_GROUP)]
        p = jnp.concatenate(hot, axis=0)
        pg = jnp.concatenate([hot[j] * gate[j:j + 1, :] for j in range(MOE_GROUP)], axis=0)
        xg = jnp.dot(p.astype(BF16), h2_ref[cols, :],
                     preferred_element_type=F32).astype(BF16)
        gb = lane_broadcast(pg)
        for j in range(MOE_GROUP):
            xy_ref[slab_slice(j, q, s), :] = xg[j * cap:(j + 1) * cap]
            gb_ref[slab_slice(j, q, s), :] = gb[j * cap:(j + 1) * cap]
        pt_ref[s, :, q * MOE_GROUP * cap:(q + 1) * MOE_GROUP * cap] = p.T.astype(BF16)

    def scatter(s, q):
        y = jnp.concatenate([xy_ref[slab_slice(j, q, s), :] for j in range(MOE_GROUP)], axis=0)
        o_ref[MOE_SUB * s:MOE_SUB * (s + 1), :] += jnp.dot(
            pt_ref[s, :, q * MOE_GROUP * cap:(q + 1) * MOE_GROUP * cap], y,
            preferred_element_type=F32)

    def project(j, q, act):
        xy_ref[slab_slice(j, q), :] = jnp.dot(
            act, w2_ref[j], preferred_element_type=F32).astype(BF16)

    @pl.when(jnp.logical_not(go_dense))
    def _():
        for s in range(n_sub):
            gather(s, 0)
        pipelined(MOE_GROUP,
                  lambda j: hidden(xy_ref[slab_slice(j, 0), :], gb_ref[slab_slice(j, 0), :], j),
                  lambda j, act: project(j, 0, act))
        for s in range(n_sub):
            scatter(s, 0)
        for s in range(n_sub):
            pl.when(needs_second(s))(functools.partial(gather, s, 1))
        for j in range(MOE_GROUP):
            @pl.when(fexp_ref[tile * N_EXPERTS + eg * MOE_GROUP + j] != 0)
            def _(j=j):
                project(j, 1, hidden(xy_ref[slab_slice(j, 1), :],
                                     gb_ref[slab_slice(j, 1), :], j))
        for s in range(n_sub):
            pl.when(needs_second(s))(functools.partial(scatter, s, 1))

    @pl.when(go_dense)
    def _():
        eye = (lax.broadcasted_iota(jnp.int32, (MOE_SUB, MOE_SUB), 0)
               == lax.broadcasted_iota(jnp.int32, (MOE_SUB, MOE_SUB), 1))
        for j in range(MOE_GROUP):
            for s in range(n_sub):
                cols = slice(MOE_SUB * s, MOE_SUB * (s + 1))
                gb_ref[cols, :] = lane_broadcast(
                    jnp.where(eye, gate_ref[j:j + 1, cols], 0.0))
            act = hidden(h2_ref[...], gb_ref[0:tm, :], j)
            o_ref[...] += jnp.dot(act, w2_ref[j], preferred_element_type=F32)


def _moe(fexp, fsub, fdense, h2, gates_t, pos_t, w1e, w3e, w2e, tm):
    t, d = h2.shape
    ne = w1e.shape[0]
    assert ne % MOE_GROUP == 0 and tm % MOE_SUB == 0
    n_sub = tm // MOE_SUB
    rows = MOE_GROUP * 2 * n_sub * MOE_CAP
    assert rows >= tm, "gate scratch is reused by the dense path"
    tok = pl.BlockSpec((tm, d), lambda i, e, *_: (i, 0))
    per_expert = pl.BlockSpec((MOE_GROUP, tm), lambda i, e, *_: (e, i))
    return pl.pallas_call(
        _moe_kernel,
        grid_spec=pltpu.PrefetchScalarGridSpec(
            num_scalar_prefetch=3,
            grid=(t // tm, ne // MOE_GROUP),
            in_specs=[tok, per_expert, per_expert,
                      pl.BlockSpec((MOE_GROUP, d, D_EXPERT), lambda i, e, *_: (e, 0, 0)),
                      pl.BlockSpec((MOE_GROUP, d, D_EXPERT), lambda i, e, *_: (e, 0, 0)),
                      pl.BlockSpec((MOE_GROUP, D_EXPERT, d), lambda i, e, *_: (e, 0, 0))],
            out_specs=tok,
            scratch_shapes=[pltpu.VMEM((rows, d), BF16),
                            pltpu.VMEM((rows, D_EXPERT), F32),
                            pltpu.VMEM((n_sub, MOE_SUB, 2 * MOE_GROUP * MOE_CAP), BF16)]),
        out_shape=jax.ShapeDtypeStruct((t, d), F32),
        compiler_params=_params(("arbitrary", "arbitrary")),
        name="moe_experts",
    )(fexp, fsub, fdense, h2, gates_t, pos_t, w1e, w3e, w2e)


def _final_kernel(base_ref, routed_ref, g2_ref, nf_ref, o_ref):
    xo = base_ref[...] + g2_ref[...] * routed_ref[...]
    o_ref[...] = (xo * lax.rsqrt(jnp.mean(xo * xo, axis=-1, keepdims=True) + RMS_EPS)
                  * nf_ref[...])


def _final(base, routed, g2, nf, tm, tiles_per_batch):
    t, d = base.shape
    tok = pl.BlockSpec((tm, d), lambda i: (i, 0))
    return pl.pallas_call(
        _final_kernel,
        grid=(t // tm,),
        in_specs=[tok, tok,
                  pl.BlockSpec((None, 1, d), lambda i: (i // tiles_per_batch, 0, 0)),
                  pl.BlockSpec((1, d), lambda i: (0, 0))],
        out_specs=tok,
        out_shape=jax.ShapeDtypeStruct((t, d), F32),
        compiler_params=_params(("arbitrary",)),
        name="residual_final_norm",
    )(base, routed, g2, nf)


def _tile(n, want):
    t = min(n, want)
    assert n % t == 0, (n, t)
    return t


def kernel(x, c, w_ada, b_ada, norm1_g, w_in, mu_shift, w0, w_decay_up, a0, w_a_up, w_g_up,
           k_k, k_a, r_k, lnx_g, lnx_b, sgu_ln_g, sgu_ln_b, w_spatial, b_spatial, w_out,
           norm2_g, w_router, e_bias, w1_e, w3_e, w2_e, w1_s, w3_s, w2_s, norm_f_g):
    b, s, d = x.shape
    t = b * s
    assert w_ada.shape[0] == 1, "single-layer block"
    assert d == 2 * D_RWKV and s % SGU_BLOCK == 0
    row = lambda vec: vec.reshape(1, -1)

    c_pad = jnp.pad(c, ((0, 8 - b), (0, 0)))
    mod = _mod(c_pad, w_ada[0], b_ada)[:b]
    sh1, sc1, g1, sh2, sc2, g2 = [m[:, None, :] for m in jnp.split(mod, 6, axis=-1)]

    bias_tile = jnp.repeat(b_spatial[0].T, HEAD, axis=1)
    ps, yb = _inproj(x, sc1, sh1, norm1_g, w_in[0].astype(BF16), mu_shift,
                     sgu_ln_g, sgu_ln_b, w_spatial[0], bias_tile, _tile(s, TILE_INPROJ))

    wlora = jnp.zeros((LANES, 2 * D_RWKV), F32)
    wlora = wlora.at[:D_DECAY_LORA, :D_RWKV].set(w_decay_up[0])
    wlora = wlora.at[D_DECAY_LORA:, D_RWKV:].set(w_a_up[0])
    seq = lambda z: z.reshape(b, s, D_RWKV)
    w_c, rh_c, u0_c, y0_c, bh_c, kh_c, pl_c, v, g, bonus = _chunk(
        ps.reshape(t, N_SHIFT), w0, a0, wlora, w_g_up[0], k_k, k_a, row(r_k))
    y = _rwkv(seq(w_c), seq(rh_c), seq(u0_c), seq(y0_c), seq(bh_c), seq(kh_c), seq(v),
              seq(pl_c), _tile(s, TILE_SCAN))

    tm_mid = _tile(s, TILE_MID)
    wr_t = jnp.pad(w_router[0].T, ((0, LANES - N_EXPERTS), (0, 0)))
    wo = w_out[0].astype(BF16)
    h2, logits_t, base = _mid(
        y.reshape(t, D_RWKV), g, bonus, lnx_g, lnx_b, yb.reshape(t, D_SGU), x.reshape(t, d),
        g1, sc2, sh2, g2, norm2_g, wo[:D_RWKV], wo[D_RWKV:], wr_t,
        w1_s[0].astype(BF16), w3_s[0].astype(BF16), w2_s[0].astype(BF16), tm_mid, s // tm_mid)

    tm_moe = _tile(t, TILE_MOE)
    gates_t, pos_t, fexp, fsub, fdense = _route(
        logits_t, jnp.broadcast_to(e_bias[0][:, None], (N_EXPERTS, tm_moe)), tm_moe)

    routed = _moe(fexp[:, 0], fsub[:, :tm_moe // MOE_SUB].reshape(-1), fdense[:, 0],
                  h2, gates_t, pos_t, w1_e[0].astype(BF16), w3_e[0].astype(BF16),
                  w2_e[0].astype(BF16), tm_moe)
    tm_fin = _tile(s, TILE_FINAL)
    out = _final(base, routed, g2, row(norm_f_g), tm_fin, s // tm_fin)
    return out.reshape(b, s, d)
```

```python
import functools

import jax
import jax.numpy as jnp
from jax import lax
from jax.experimental import pallas as pl
from jax.experimental.pallas import tpu as pltpu

F32 = jnp.float32
BF16 = jnp.bfloat16

HEAD = 64
LANES = 128
SUBLANES = 8
D_RWKV = 512
D_SGU = 512
N_RWKV_PAIRS = D_RWKV // LANES
D_DECAY_LORA = 64
D_AAA_LORA = 64
D_GATE_LORA = 128
N_SHIFT = 3 * D_RWKV + D_DECAY_LORA + D_AAA_LORA + D_GATE_LORA
SGU_BLOCK = 128
CHUNK = 64
N_EXPERTS = 64
N_GROUPS = 8
GROUP_SIZE = N_EXPERTS // N_GROUPS
TOPK_GROUPS = 4
TOP_K = 8
D_EXPERT = 256
ROUTED_SCALE = 2.5
RMS_EPS = 1e-6
LN_EPS = 1e-5
LN_X_EPS = 64e-5
VMEM_LIMIT_BYTES = 56 * 1024 * 1024

TILE_MOD_COLS = 1024
TILE_INPROJ = 512
TILE_SCAN = 512
TILE_MID = 512
TILE_MOE = 1024
TILE_FINAL = 512


def _params(semantics):
    return pltpu.CompilerParams(dimension_semantics=semantics,
                                vmem_limit_bytes=VMEM_LIMIT_BYTES)


def _dot3(a, b, dims=(((1,), (0,)), ((), ()))):
    ((ca,), (cb,)), _ = dims
    a_hi = a.astype(BF16)
    a_lo = (a - a_hi.astype(F32)).astype(BF16)
    b_hi = b.astype(BF16)
    b_lo = (b - b_hi.astype(F32)).astype(BF16)
    dg = functools.partial(lax.dot_general, dimension_numbers=dims,
                           preferred_element_type=F32)
    return (dg(jnp.concatenate([a_hi, a_lo], axis=ca), jnp.concatenate([b_hi, b_hi], axis=cb))
            + dg(a_hi, b_lo))


def _head_ones():
    r = lax.broadcasted_iota(jnp.int32, (LANES, LANES), 0) // HEAD
    c = lax.broadcasted_iota(jnp.int32, (LANES, LANES), 1) // HEAD
    return jnp.where(r == c, 1.0, 0.0).astype(BF16)


def _head_sum(z, ones):
    both = jnp.concatenate([ones, ones], axis=0)
    parts = []
    for p in range(z.shape[1] // LANES):
        zp = z[:, LANES * p:LANES * (p + 1)]
        hi = zp.astype(BF16)
        lo = (zp - hi.astype(F32)).astype(BF16)
        parts.append(jnp.dot(jnp.concatenate([hi, lo], axis=1), both,
                             preferred_element_type=F32))
    return jnp.concatenate(parts, axis=-1)


def _sigmoid(x):
    return 1.0 / (1.0 + jnp.exp(-x))


def _silu(x):
    return x * _sigmoid(x)


def _mod_kernel(c_ref, w_ref, b_ref, o_ref):
    s = _silu(c_ref[...])
    o_ref[...] = _dot3(s, w_ref[...]) + b_ref[...]


def _mod(c_pad, w_ada, b_ada):
    rows, d = c_pad.shape
    n = w_ada.shape[1]
    tn = TILE_MOD_COLS
    return pl.pallas_call(
        _mod_kernel,
        grid=(n // tn,),
        in_specs=[pl.BlockSpec((rows, d), lambda j: (0, 0)),
                  pl.BlockSpec((d, tn), lambda j: (0, j)),
                  pl.BlockSpec((1, tn), lambda j: (0, j))],
        out_specs=pl.BlockSpec((rows, tn), lambda j: (0, j)),
        out_shape=jax.ShapeDtypeStruct((rows, n), F32),
        compiler_params=_params(("arbitrary",)),
        name="adaln_mod",
    )(c_pad, w_ada, b_ada)


def _inproj_kernel(x_ref, sc_ref, sh_ref, g_ref, w_ref, mu_ref,
                   sgu_g_ref, sgu_b_ref, ws_ref, sgu_bias_ref, ps_ref, yb_ref, carry_ref):
    @pl.when(pl.program_id(1) == 0)
    def _():
        carry_ref[...] = jnp.zeros_like(carry_ref)

    x = x_ref[...]
    tm = x.shape[0]
    h = x * lax.rsqrt(jnp.mean(x * x, axis=-1, keepdims=True) + RMS_EPS) * g_ref[...]
    h = h * (1.0 + sc_ref[...]) + sh_ref[...]
    proj = jnp.dot(h.astype(BF16), w_ref[...], preferred_element_type=F32)
    ps = proj[:, :N_SHIFT]
    prev = pltpu.roll(ps, 1, 0)
    first = lax.broadcasted_iota(jnp.int32, ps.shape, 0) == 0
    prev = jnp.where(first, carry_ref[0:1, :], prev)
    carry_ref[0:1, :] = ps[tm - 1:tm, :]
    ps_ref[...] = ps + (prev - ps) * mu_ref[...]
    _sgu_apply(proj[:, N_SHIFT:], sgu_g_ref, sgu_b_ref, ws_ref, sgu_bias_ref, yb_ref)


def _inproj(x, sc1, sh1, g, w_in_bf16, mu, sgu_g, sgu_b, w_spatial, sgu_bias, tm):
    b, s, d = x.shape
    d_in = w_in_bf16.shape[1]
    assert d_in == N_SHIFT + 2 * D_SGU and tm % SGU_BLOCK == 0
    row = lambda bi, i: (bi, 0, 0)
    const = lambda shape: pl.BlockSpec(shape, lambda bi, i: (0,) * len(shape))
    return pl.pallas_call(
        _inproj_kernel,
        grid=(b, s // tm),
        in_specs=[pl.BlockSpec((None, tm, d), lambda bi, i: (bi, i, 0)),
                  pl.BlockSpec((None, 1, d), row),
                  pl.BlockSpec((None, 1, d), row),
                  const((1, d)), const((d, d_in)), const((1, N_SHIFT)),
                  const((1, D_SGU)), const((1, D_SGU)), const(w_spatial.shape),
                  const((SGU_BLOCK, D_SGU))],
        out_specs=[pl.BlockSpec((None, tm, N_SHIFT), lambda bi, i: (bi, i, 0)),
                   pl.BlockSpec((None, tm, D_SGU), lambda bi, i: (bi, i, 0))],
        out_shape=[jax.ShapeDtypeStruct((b, s, N_SHIFT), F32),
                   jax.ShapeDtypeStruct((b, s, D_SGU), BF16)],
        scratch_shapes=[pltpu.VMEM((8, N_SHIFT), F32)],
        compiler_params=_params(("arbitrary", "arbitrary")),
        name="inproj_shift_sgu",
    )(x, sc1, sh1, g, w_in_bf16, mu, sgu_g, sgu_b, w_spatial, sgu_bias)


def _token_terms(ps, w0_ref, a0_ref, wlora_ref, wg_ref, kk_ref, ka_ref, rk_ref):
    c1, c2, c3 = D_RWKV, 2 * D_RWKV, 3 * D_RWKV
    r, k, v = ps[:, :c1], ps[:, c1:c2], ps[:, c2:c3]
    xwa = ps[:, c3:c3 + LANES]
    xg = ps[:, c3 + LANES:]
    is_w = lax.broadcasted_iota(jnp.int32, xwa.shape, 1) < D_DECAY_LORA
    lora = _dot1(jnp.where(is_w, jnp.tanh(xwa), xwa), wlora_ref[...])
    wlog = -jax.nn.softplus(-(w0_ref[...] + lora[:, :c1])) - 0.5
    log_decay = -jnp.exp(wlog)
    a = _sigmoid(a0_ref[...] + lora[:, c1:])
    g = _dot1(_sigmoid(xg), wg_ref[...])
    ones = _head_ones()
    kk = k * kk_ref[...]
    kk = kk / jnp.maximum(jnp.sqrt(_head_sum(kk * kk, ones)), 1e-12)
    kh = k * (1.0 + (a - 1.0) * ka_ref[...])
    bonus = _head_sum(r * kh * rk_ref[...], ones) * v
    return r, log_decay, kh, v, kk, kk * a, g, bonus


RWKV_CHUNK = 32
CHUNK_TILE = 128
CHUNK_GROUP_PAIRS = 4

_NN = (((1,), (0,)), ((), ()))
_NT = (((1,), (1,)), ((), ()))


def _dot1(a, b, dims=_NN):
    return lax.dot_general(a.astype(BF16), b.astype(BF16), dims, preferred_element_type=F32)


def _dot_exact_lhs(lhs_bf16, x):
    hi = x.astype(BF16)
    rest = x - hi.astype(F32)
    mid = rest.astype(BF16)
    lo = (rest - mid.astype(F32)).astype(BF16)
    return jnp.dot(jnp.concatenate([lhs_bf16] * 3, axis=1),
                   jnp.concatenate([hi, mid, lo], axis=0), preferred_element_type=F32)


def _chunk_kernel(ps_ref, w0_ref, a0_ref, wlora_ref, wg_ref, kk_ref, ka_ref, rk_ref,
                  w_o, rh_o, u0_o, y0_o, bh_o, kh_o, pl_o, v_o, g_o, bonus_o):
    r_all, lw_all, k_all, v_all, kk_all, b_all, g_all, bonus_all = _token_terms(
        ps_ref[...], w0_ref, a0_ref, wlora_ref, wg_ref, kk_ref, ka_ref, rk_ref)
    v_o[...] = v_all
    g_o[...] = g_all
    bonus_o[...] = bonus_all
    n = CHUNK_TILE
    tpos = lax.broadcasted_iota(jnp.int32, (n, n), 0)
    spos = lax.broadcasted_iota(jnp.int32, (n, n), 1)
    same = (tpos // RWKV_CHUNK) == (spos // RWKV_CHUNK)
    incl = same & (spos <= tpos)
    strict = same & (spos < tpos)
    tri_ones = jnp.where(incl, 1.0, 0.0).astype(BF16)
    blk_ones = jnp.where(same, 1.0, 0.0).astype(BF16)
    eye = jnp.where(tpos == spos, 1.0, 0.0)
    lane_head = lax.broadcasted_iota(jnp.int32, (n, LANES), 1) // HEAD

    for first in range(0, N_RWKV_PAIRS, CHUNK_GROUP_PAIRS):
        pairs = range(CHUNK_GROUP_PAIRS)
        slabs = [slice(LANES * (first + p), LANES * (first + p + 1)) for p in pairs]
        lw, r, k, v, kk, b = ([x[:, sl] for sl in slabs]
                              for x in (lw_all, r_all, k_all, v_all, kk_all, b_all))
        cum = [_dot_exact_lhs(tri_ones, x) for x in lw]
        tot = [_dot_exact_lhs(blk_ones, x) for x in lw]
        at = [-kk[p] * jnp.exp(cum[p] - lw[p]) for p in pairs]
        rt = [r[p] * jnp.exp(cum[p]) for p in pairs]
        inv = [jnp.exp(-cum[p]) for p in pairs]
        bt = [b[p] * inv[p] for p in pairs]
        kt = [k[p] * inv[p] for p in pairs]

        chains = [(p, h) for p in pairs for h in range(LANES // HEAD)]
        mine = [lane_head == h for h in range(LANES // HEAD)]
        a_h = [jnp.where(mine[h], at[p], 0.0) for p, h in chains]
        r_h = [jnp.where(mine[h], rt[p], 0.0) for p, h in chains]
        v_h = [jnp.where(mine[h], v[p], 0.0) for p, h in chains]
        each = range(len(chains))
        btkt = [jnp.concatenate([bt[p], kt[p]], axis=0) for p in pairs]
        a_bk = [_dot3(a_h[c], btkt[chains[c][0]], _NT) for c in each]
        a_ab = [jnp.where(strict, m[:, :n], 0.0) for m in a_bk]
        a_ak = [jnp.where(strict, m[:, n:], 0.0) for m in a_bk]
        tinv = [eye + a for a in a_ab]
        x = a_ab
        for _ in range(RWKV_CHUNK.bit_length() - 2):
            x = [_dot1(xc, xc) for xc in x]
            tinv = [tc + _dot1(tc, xc) for tc, xc in zip(tinv, x)]
        akv = [_dot3(a_ak[c], v_h[c]) for c in each]
        m_bk = [_dot1(r_h[c], btkt[chains[c][0]], _NT) for c in each]
        m_rb = [jnp.where(incl, m[:, :n], 0.0) for m in m_bk]
        m_rk = [jnp.where(incl, m[:, n:], 0.0) for m in m_bk]
        wu = [_dot3(tinv[c], jnp.concatenate([a_h[c], akv[c]], axis=1)) for c in each]
        w_h = [m[:, :LANES] for m in wu]
        u0_h = [m[:, LANES:] for m in wu]
        ry = [_dot1(m_rb[c], wu[c]) for c in each]
        rh_h = [r_h[c] + ry[c][:, :LANES] for c in each]
        y0_h = [ry[c][:, LANES:] + _dot1(m_rk[c], v_h[c]) for c in each]

        for p in pairs:
            c0, c1 = 2 * p, 2 * p + 1
            rem = jnp.exp(tot[p] - cum[p])
            w_o[:, slabs[p]] = w_h[c0] + w_h[c1]
            rh_o[:, slabs[p]] = rh_h[c0] + rh_h[c1]
            u0_o[:, slabs[p]] = u0_h[c0] + u0_h[c1]
            y0_o[:, slabs[p]] = y0_h[c0] + y0_h[c1]
            bh_o[:, slabs[p]] = b[p] * rem
            kh_o[:, slabs[p]] = k[p] * rem
            pl_o[:, slabs[p]] = jnp.exp(tot[p])


def _chunk(ps, w0, a0, wlora, wg, k_k, k_a, r_k):
    t = ps.shape[0]
    const = lambda shape: pl.BlockSpec(shape, lambda i: (0, 0))
    spec = pl.BlockSpec((CHUNK_TILE, D_RWKV), lambda i: (i, 0))
    out = jax.ShapeDtypeStruct((t, D_RWKV), F32)
    return pl.pallas_call(
        _chunk_kernel,
        grid=(t // CHUNK_TILE,),
        in_specs=[pl.BlockSpec((CHUNK_TILE, N_SHIFT), lambda i: (i, 0)),
                  const((1, D_RWKV)), const((1, D_RWKV)),
                  const((LANES, 2 * D_RWKV)), const((D_GATE_LORA, D_RWKV)),
                  const((1, D_RWKV)), const((1, D_RWKV)), const((1, D_RWKV))],
        out_specs=[spec] * 10,
        out_shape=[out] * 10,
        compiler_params=_params(("arbitrary",)),
        name="rwkv_chunk_prep",
    )(ps, w0, a0, wlora, wg, k_k, k_a, r_k)


def _rwkv_kernel(w_ref, rh_ref, u0_ref, y0_ref, bh_ref, kh_ref, v_ref, pl_ref,
                 y_ref, state_ref):
    @pl.when(pl.program_id(0) == 0)
    def _():
        state_ref[...] = jnp.zeros_like(state_ref)

    nb, tt, _ = w_ref.shape
    L = RWKV_CHUNK
    depth = LANES * pl.cdiv(6 * L + SUBLANES, LANES)
    row_head = lax.broadcasted_iota(jnp.int32, (LANES, LANES), 0) // HEAD
    col_head = lax.broadcasted_iota(jnp.int32, (LANES, LANES), 1) // HEAD
    same_head = row_head == col_head
    pad = jnp.zeros((depth - 6 * L - SUBLANES, LANES), F32)
    zeros_tail = jnp.zeros((depth - 6 * L, LANES), BF16)

    def split(x):
        hi = x.astype(BF16)
        return hi, (x - hi.astype(F32)).astype(BF16)

    def chunk(ci, carry):
        t0 = pl.multiple_of(ci * L, L)
        streams = [(bi, p) for bi in range(nb) for p in range(N_RWKV_PAIRS)]

        def rows(ref, n=L):
            return [ref[bi, pl.ds(t0, n), LANES * p:LANES * (p + 1)] for bi, p in streams]

        each = range(len(streams))
        s = [state_ref[bi * N_RWKV_PAIRS + p] for bi, p in streams]
        w, rh, u0, y0, bh, kh, v = (rows(ref) for ref in
                                   (w_ref, rh_ref, u0_ref, y0_ref, bh_ref, kh_ref, v_ref))
        p_rows = rows(pl_ref, SUBLANES)
        left, p_col = [], []
        for c in each:
            (b_hi, b_lo), (k_hi, k_lo) = split(bh[c]), split(kh[c])
            block = jnp.concatenate([x.astype(F32) for x in (b_hi, b_lo, b_hi, k_hi, k_lo, k_hi)]
                                    + [p_rows[c], pad], axis=0).T
            left.append(block.astype(BF16))
            p_col.append(jnp.broadcast_to(block[:, 6 * L:6 * L + 1], (LANES, LANES)))
        g = []
        for c in each:
            (l_hi, l_lo), (s_hi, s_lo) = split(jnp.concatenate([w[c], rh[c]], axis=0)), split(s[c])
            g.append(jnp.dot(jnp.concatenate([l_hi, l_lo], axis=1),
                             jnp.concatenate([s_hi, s_hi], axis=0), preferred_element_type=F32)
                     + jnp.dot(l_hi, s_lo, preferred_element_type=F32))
        u = [g[c][:L] + u0[c] for c in each]
        upd = []
        for c in each:
            (u_hi, u_lo), (v_hi, v_lo) = split(u[c]), split(v[c])
            right = jnp.concatenate([u_hi, u_hi, u_lo, v_hi, v_hi, v_lo, zeros_tail], axis=0)
            upd.append(jnp.dot(left[c], right, preferred_element_type=F32))
        for c, (bi, p) in enumerate(streams):
            y_ref[bi, pl.ds(t0, L), LANES * p:LANES * (p + 1)] = g[c][L:] + y0[c]
            state_ref[bi * N_RWKV_PAIRS + p] = (p_col[c] * s[c]
                                                + jnp.where(same_head, upd[c], 0.0))
        return carry

    lax.fori_loop(0, tt // L, chunk, 0)


def _rwkv(w, rh, u0, y0, bh, kh, v, p_l, tt):
    nb, s, d = w.shape
    spec = pl.BlockSpec((nb, tt, d), lambda i: (0, i, 0))
    return pl.pallas_call(
        _rwkv_kernel,
        grid=(s // tt,),
        in_specs=[spec] * 8,
        out_specs=spec,
        out_shape=jax.ShapeDtypeStruct((nb, s, d), F32),
        scratch_shapes=[pltpu.VMEM((nb * N_RWKV_PAIRS, LANES, LANES), F32)],
        compiler_params=_params(("arbitrary",)),
        name="rwkv_scan",
    )(w, rh, u0, y0, bh, kh, v, p_l)


def _gelu_tanh(x):
    return 0.5 * x * (1.0 + jnp.tanh(0.7978845608028654 * (x + 0.044715 * (x * x * x))))


def _sgu_apply(pb, g_ref, b_ref, ws_ref, bias_ref, o_ref):
    z = _gelu_tanh(pb)
    u, v = z[:, :D_SGU], z[:, D_SGU:]
    mu = jnp.mean(v, axis=-1, keepdims=True)
    vc = v - mu
    var = jnp.mean(vc * vc, axis=-1, keepdims=True)
    vn = (vc * lax.rsqrt(var + LN_EPS) * g_ref[...] + b_ref[...]).astype(BF16)
    tpos = lax.broadcasted_iota(jnp.int32, (SGU_BLOCK, SGU_BLOCK), 0) // CHUNK
    spos = lax.broadcasted_iota(jnp.int32, (SGU_BLOCK, SGU_BLOCK), 1) // CHUNK
    causal = spos <= tpos
    first_head = lax.broadcasted_iota(jnp.int32, (SGU_BLOCK, LANES), 1) < HEAD
    for p in range(D_SGU // LANES):
        sl = slice(LANES * p, LANES * (p + 1))
        ws0 = jnp.where(causal, ws_ref[2 * p], 0.0).astype(BF16)
        ws1 = jnp.where(causal, ws_ref[2 * p + 1], 0.0).astype(BF16)
        for n in range(pb.shape[0] // SGU_BLOCK):
            rows = slice(SGU_BLOCK * n, SGU_BLOCK * (n + 1))
            vb = vn[rows, sl]
            sp = jnp.where(first_head,
                           jnp.dot(ws0, vb, preferred_element_type=F32),
                           jnp.dot(ws1, vb, preferred_element_type=F32))
            o_ref[rows, sl] = (u[rows, sl] * (sp + bias_ref[:, sl])).astype(o_ref.dtype)


def _mid_kernel(y_ref, g_ref, bonus_ref, lng_ref, lnb_ref, yb_ref, x_ref,
                g1_ref, sc2_ref, sh2_ref, g2_ref, n2_ref,
                woa_ref, wob_ref, wrt_ref, w1_ref, w3_ref, w2_ref,
                h2_ref, logit_ref, base_ref):
    ones = _head_ones()
    y = y_ref[...]
    mu = _head_sum(y, ones) * (1.0 / HEAD)
    yc = y - mu
    var = _head_sum(yc * yc, ones) * (1.0 / HEAD)
    ya = yc * lax.rsqrt(var + LN_X_EPS) * lng_ref[...] + lnb_ref[...]
    ya = ((ya + bonus_ref[...]) * g_ref[...]).astype(BF16)
    yo = (jnp.dot(ya, woa_ref[...], preferred_element_type=F32)
          + jnp.dot(yb_ref[...], wob_ref[...], preferred_element_type=F32))
    x1 = x_ref[...] + g1_ref[...] * yo
    h2 = x1 * lax.rsqrt(jnp.mean(x1 * x1, axis=-1, keepdims=True) + RMS_EPS) * n2_ref[...]
    h2 = h2 * (1.0 + sc2_ref[...]) + sh2_ref[...]
    h2b = h2.astype(BF16)
    h2_ref[...] = h2b
    logit_ref[...] = _dot1(wrt_ref[...], h2b, _NT)
    act = (_silu(jnp.dot(h2b, w1_ref[...], preferred_element_type=F32))
           * jnp.dot(h2b, w3_ref[...], preferred_element_type=F32)).astype(BF16)
    ms = jnp.dot(act, w2_ref[...], preferred_element_type=F32)
    base_ref[...] = x1 + g2_ref[...] * ms


def _mid(y, g, bonus, lnx_g, lnx_b, yb, x, g1, sc2, sh2, g2, n2g,
         wo_a, wo_b, wr_t, w1s, w3s, w2s, tm, tiles_per_batch):
    t, d = x.shape
    tok = lambda w: pl.BlockSpec((tm, w), lambda i: (i, 0))
    const = lambda shape: pl.BlockSpec(shape, lambda i: (0,) * len(shape))
    mod = pl.BlockSpec((None, 1, d), lambda i: (i // tiles_per_batch, 0, 0))
    return pl.pallas_call(
        _mid_kernel,
        grid=(t // tm,),
        in_specs=[tok(D_RWKV), tok(D_RWKV), tok(D_RWKV),
                  const((1, D_RWKV)), const((1, D_RWKV)),
                  tok(D_SGU), tok(d), mod, mod, mod, mod, const((1, d)),
                  const(wo_a.shape), const(wo_b.shape), const(wr_t.shape),
                  const(w1s.shape), const(w3s.shape), const(w2s.shape)],
        out_specs=[tok(d), pl.BlockSpec((LANES, tm), lambda i: (0, i)), tok(d)],
        out_shape=[jax.ShapeDtypeStruct((t, d), BF16),
                   jax.ShapeDtypeStruct((LANES, t), F32),
                   jax.ShapeDtypeStruct((t, d), F32)],
        compiler_params=_params(("arbitrary",)),
        name="outproj_norm_router_shared",
    )(y, g, bonus, lnx_g, lnx_b, yb, x, g1, sc2, sh2, g2, n2g,
      wo_a, wo_b, wr_t, w1s, w3s, w2s)


MOE_SUB = 128
MOE_CAP = 32
MOE_GROUP = 8


def _route_kernel(logit_ref, bias_ref, gate_ref, pos_ref, fexp_ref, fsub_ref, fdense_ref):
    tt = logit_ref.shape[1]
    scores = _sigmoid(logit_ref[0:N_EXPERTS, :])
    biased = scores + bias_ref[...]
    neg_inf = -jnp.inf

    group_blocks, group_scores = [], []
    for gi in range(N_GROUPS):
        blk = biased[GROUP_SIZE * gi:GROUP_SIZE * (gi + 1), :]
        m1 = jnp.max(blk, axis=0, keepdims=True)
        is_max = blk == m1
        n_max = jnp.sum(jnp.where(is_max, 1.0, 0.0), axis=0, keepdims=True)
        m2 = jnp.max(jnp.where(is_max, neg_inf, blk), axis=0, keepdims=True)
        group_blocks.append(blk)
        group_scores.append(m1 + jnp.where(n_max >= 2.0, m1, m2))

    masked = []
    for gi in range(N_GROUPS):
        rank = jnp.zeros((1, tt), F32)
        for gj in range(N_GROUPS):
            if gj == gi:
                continue
            ahead = (group_scores[gj] >= group_scores[gi]) if gj < gi else \
                    (group_scores[gj] > group_scores[gi])
            rank += jnp.where(ahead, 1.0, 0.0)
        masked.append(jnp.where(rank < float(TOPK_GROUPS), group_blocks[gi], neg_inf))
    sub = lax.broadcasted_iota(jnp.int32, (GROUP_SIZE, tt), 0)
    ranks = [jnp.zeros((GROUP_SIZE, tt), F32) for _ in range(N_GROUPS)]
    for ej in range(N_EXPERTS):
        gj, oj = divmod(ej, GROUP_SIZE)
        other = masked[gj][oj:oj + 1, :]
        for gi in range(N_GROUPS):
            blk = masked[gi]
            if gi > gj:
                ahead = jnp.where(other >= blk, 1.0, 0.0)
            elif gi < gj:
                ahead = jnp.where(other > blk, 1.0, 0.0)
            else:
                ahead = jnp.where(sub > oj, jnp.where(other >= blk, 1.0, 0.0),
                                  jnp.where(other > blk, 1.0, 0.0))
            ranks[gi] = ranks[gi] + ahead
    rank = jnp.concatenate(ranks, axis=0)
    chosen = jnp.where(rank < float(TOP_K), 1.0, 0.0)
    sw = chosen * scores
    gate_ref[...] = sw / jnp.sum(sw, axis=0, keepdims=True) * ROUTED_SCALE

    before = (lax.broadcasted_iota(jnp.int32, (MOE_SUB, MOE_SUB), 0)
              < lax.broadcasted_iota(jnp.int32, (MOE_SUB, MOE_SUB), 1))
    prefix_ones = jnp.where(before, 1.0, 0.0).astype(BF16)
    all_ones = jnp.ones((MOE_SUB, MOE_SUB), BF16)
    lane = lax.broadcasted_iota(jnp.int32, (N_EXPERTS, LANES), 1)
    positions, counts = [], jnp.zeros((N_EXPERTS, LANES), F32)
    for s in range(tt // MOE_SUB):
        blk = chosen[:, MOE_SUB * s:MOE_SUB * (s + 1)].astype(BF16)
        prefix = jnp.dot(blk, prefix_ones, preferred_element_type=F32)
        positions.append(jnp.where(blk > 0, prefix, -1.0))
        total = jnp.dot(blk, all_ones, preferred_element_type=F32)
        counts = counts + jnp.where(lane == s, total, 0.0)
    pos_ref[...] = jnp.concatenate(positions, axis=1)
    per_expert = jnp.max(counts, axis=1, keepdims=True)
    fexp_ref[...] = jnp.broadcast_to(jnp.where(per_expert > float(MOE_CAP), 1, 0),
                                     fexp_ref.shape).astype(jnp.int32)
    per_group = jnp.concatenate(
        [jnp.max(counts[MOE_GROUP * g:MOE_GROUP * (g + 1), :], axis=0, keepdims=True)
         for g in range(N_EXPERTS // MOE_GROUP)], axis=0)
    fsub_ref[...] = jnp.where(per_group > float(MOE_CAP), 1, 0).astype(jnp.int32)
    group_worst = jnp.max(per_group, axis=1, keepdims=True)
    fdense_ref[...] = jnp.broadcast_to(jnp.where(group_worst > float(2 * MOE_CAP), 1, 0),
                                       fdense_ref.shape).astype(jnp.int32)


def _route(logits_t, bias_tile, tt):
    t = logits_t.shape[1]
    n_tiles = t // tt
    n_groups = N_EXPERTS // MOE_GROUP
    tok = pl.BlockSpec((N_EXPERTS, tt), lambda i: (0, i))
    return pl.pallas_call(
        _route_kernel,
        grid=(n_tiles,),
        in_specs=[pl.BlockSpec((LANES, tt), lambda i: (0, i)),
                  pl.BlockSpec((N_EXPERTS, tt), lambda i: (0, 0))],
        out_specs=[tok, tok,
                   pl.BlockSpec((N_EXPERTS, LANES), lambda i: (i, 0)),
                   pl.BlockSpec((n_groups, LANES), lambda i: (i, 0)),
                   pl.BlockSpec((n_groups, LANES), lambda i: (i, 0))],
        out_shape=[jax.ShapeDtypeStruct((N_EXPERTS, t), F32),
                   jax.ShapeDtypeStruct((N_EXPERTS, t), F32),
                   jax.ShapeDtypeStruct((n_tiles * N_EXPERTS, LANES), jnp.int32),
                   jax.ShapeDtypeStruct((n_tiles * n_groups, LANES), jnp.int32),
                   jax.ShapeDtypeStruct((n_tiles * n_groups, LANES), jnp.int32)],
        compiler_params=_params(("arbitrary",)),
        name="route_topk",
    )(logits_t, bias_tile)


def _moe_kernel(fexp_ref, fsub_ref, fdense_ref, h2_ref, gate_ref, pos_ref,
                w1_ref, w3_ref, w2_ref, o_ref, xy_ref, gb_ref, pt_ref):
    tile, eg = pl.program_id(0), pl.program_id(1)
    n_groups = pl.num_programs(1)
    tm = h2_ref.shape[0]
    n_sub = tm // MOE_SUB

    @pl.when(eg == 0)
    def _():
        o_ref[...] = jnp.zeros_like(o_ref)

    @pl.when((tile == 0) & (eg == 0))
    def _():
        xy_ref[...] = jnp.zeros_like(xy_ref)
        gb_ref[...] = jnp.zeros_like(gb_ref)

    def lane_broadcast(m):
        return jnp.broadcast_to(jnp.sum(m, axis=-1, keepdims=True), (m.shape[0], D_EXPERT))

    def hidden(x, gate_rows, j):
        return (_silu(jnp.dot(x, w1_ref[j], preferred_element_type=F32))
                * jnp.dot(x, w3_ref[j], preferred_element_type=F32) * gate_rows).astype(BF16)

    def pipelined(n, first, second):
        pending = first(0)
        for i in range(1, n):
            nxt = first(i)
            second(i - 1, pending)
            pending = nxt
        second(n - 1, pending)

    cap = MOE_CAP
    slab_rows = n_sub * cap
    go_dense = fdense_ref[tile * n_groups + eg] != 0

    def needs_second(s):
        return fsub_ref[(tile * n_groups + eg) * n_sub + s] != 0

    def slab_slice(j, q, s=None):
        start = (2 * j + q) * slab_rows
        if s is None:
            return slice(start, start + slab_rows)
        return slice(start + s * cap, start + (s + 1) * cap)

    def gather(s, q):
        cols = slice(MOE_SUB * s, MOE_SUB * (s + 1))
        slot = lax.broadcasted_iota(jnp.int32, (cap, MOE_SUB), 0).astype(F32) + float(q * cap)
        pos, gate = pos_ref[:, cols], gate_ref[:, cols]
        hot = [jnp.where(slot == pos[j:j + 1, :], 1.0, 0.0) for j in range(MOE_GROUP)]
        p = jnp.concatenate(hot, axis=0)
        pg = jnp.concatenate([hot[j] * gate[j:j + 1, :] for j in range(MOE_GROUP)], axis=0)
        xg = jnp.dot(p.astype(BF16), h2_ref[cols, :],
                     preferred_element_type=F32).astype(BF16)
        gb = lane_broadcast(pg)
        for j in range(MOE_GROUP):
            xy_ref[slab_slice(j, q, s), :] = xg[j * cap:(j + 1) * cap]
            gb_ref[slab_slice(j, q, s), :] = gb[j * cap:(j + 1) * cap]
        pt_ref[s, :, q * MOE_GROUP * cap:(q + 1) * MOE_GROUP * cap] = p.T.astype(BF16)

    def scatter(s, q):
        y = jnp.concatenate([xy_ref[slab_slice(j, q, s), :] for j in range(MOE_GROUP)], axis=0)
        o_ref[MOE_SUB * s:MOE_SUB * (s + 1), :] += jnp.dot(
            pt_ref[s, :, q * MOE_GROUP * cap:(q + 1) * MOE_GROUP * cap], y,
            preferred_element_type=F32)

    def project(j, q, act):
        xy_ref[slab_slice(j, q), :] = jnp.dot(
            act, w2_ref[j], preferred_element_type=F32).astype(BF16)

    @pl.when(jnp.logical_not(go_dense))
    def _():
        for s in range(n_sub):
            gather(s, 0)
        pipelined(MOE_GROUP,
                  lambda j: hidden(xy_ref[slab_slice(j, 0), :], gb_ref[slab_slice(j, 0), :], j),
                  lambda j, act: project(j, 0, act))
        for s in range(n_sub):
            scatter(s, 0)
        for s in range(n_sub):
            pl.when(needs_second(s))(functools.partial(gather, s, 1))
        for j in range(MOE_GROUP):
            @pl.when(fexp_ref[tile * N_EXPERTS + eg * MOE_GROUP + j] != 0)
            def _(j=j):
                project(j, 1, hidden(xy_ref[slab_slice(j, 1), :],
                                     gb_ref[slab_slice(j, 1), :], j))
        for s in range(n_sub):
            pl.when(needs_second(s))(functools.partial(scatter, s, 1))

    @pl.when(go_dense)
    def _():
        eye = (lax.broadcasted_iota(jnp.int32, (MOE_SUB, MOE_SUB), 0)
               == lax.broadcasted_iota(jnp.int32, (MOE_SUB, MOE_SUB), 1))
        for j in range(MOE_GROUP):
            for s in range(n_sub):
                cols = slice(MOE_SUB * s, MOE_SUB * (s + 1))
                gb_ref[cols, :] = lane_broadcast(
                    jnp.where(eye, gate_ref[j:j + 1, cols], 0.0))
            act = hidden(h2_ref[...], gb_ref[0:tm, :], j)
            o_ref[...] += jnp.dot(act, w2_ref[j], preferred_element_type=F32)


def _moe(fexp, fsub, fdense, h2, gates_t, pos_t, w1e, w3e, w2e, tm):
    t, d = h2.shape
    ne = w1e.shape[0]
    assert ne % MOE_GROUP == 0 and tm % MOE_SUB == 0
    n_sub = tm // MOE_SUB
    rows = MOE_GROUP * 2 * n_sub * MOE_CAP
    assert rows >= tm, "gate scratch is reused by the dense path"
    tok = pl.BlockSpec((tm, d), lambda i, e, *_: (i, 0))
    per_expert = pl.BlockSpec((MOE_GROUP, tm), lambda i, e, *_: (e, i))
    return pl.pallas_call(
        _moe_kernel,
        grid_spec=pltpu.PrefetchScalarGridSpec(
            num_scalar_prefetch=3,
            grid=(t // tm, ne // MOE_GROUP),
            in_specs=[tok, per_expert, per_expert,
                      pl.BlockSpec((MOE_GROUP, d, D_EXPERT), lambda i, e, *_: (e, 0, 0)),
                      pl.BlockSpec((MOE_GROUP, d, D_EXPERT), lambda i, e, *_: (e, 0, 0)),
                      pl.BlockSpec((MOE_GROUP, D_EXPERT, d), lambda i, e, *_: (e, 0, 0))],
            out_specs=tok,
            scratch_shapes=[pltpu.VMEM((rows, d), BF16),
                            pltpu.VMEM((rows, D_EXPERT), F32),
                            pltpu.VMEM((n_sub, MOE_SUB, 2 * MOE_GROUP * MOE_CAP), BF16)]),
        out_shape=jax.ShapeDtypeStruct((t, d), F32),
        compiler_params=_params(("arbitrary", "arbitrary")),
        name="moe_experts",
    )(fexp, fsub, fdense, h2, gates_t, pos_t, w1e, w3e, w2e)


def _final_kernel(base_ref, routed_ref, g2_ref, nf_ref, o_ref):
    xo = base_ref[...] + g2_ref[...] * routed_ref[...]
    o_ref[...] = (xo * lax.rsqrt(jnp.mean(xo * xo, axis=-1, keepdims=True) + RMS_EPS)
                  * nf_ref[...])


def _final(base, routed, g2, nf, tm, tiles_per_batch):
    t, d = base.shape
    tok = pl.BlockSpec((tm, d), lambda i: (i, 0))
    return pl.pallas_call(
        _final_kernel,
        grid=(t // tm,),
        in_specs=[tok, tok,
                  pl.BlockSpec((None, 1, d), lambda i: (i // tiles_per_batch, 0, 0)),
                  pl.BlockSpec((1, d), lambda i: (0, 0))],
        out_specs=tok,
        out_shape=jax.ShapeDtypeStruct((t, d), F32),
        compiler_params=_params(("arbitrary",)),
        name="residual_final_norm",
    )(base, routed, g2, nf)


def _tile(n, want):
    t = min(n, want)
    assert n % t == 0, (n, t)
    return t


def kernel(x, c, w_ada, b_ada, norm1_g, w_in, mu_shift, w0, w_decay_up, a0, w_a_up, w_g_up,
           k_k, k_a, r_k, lnx_g, lnx_b, sgu_ln_g, sgu_ln_b, w_spatial, b_spatial, w_out,
           norm2_g, w_router, e_bias, w1_e, w3_e, w2_e, w1_s, w3_s, w2_s, norm_f_g):
    b, s, d = x.shape
    t = b * s
    assert w_ada.shape[0] == 1, "single-layer block"
    assert d == 2 * D_RWKV and s % SGU_BLOCK == 0
    row = lambda vec: vec.reshape(1, -1)

    c_pad = jnp.pad(c, ((0, 8 - b), (0, 0)))
    mod = _mod(c_pad, w_ada[0], b_ada)[:b]
    sh1, sc1, g1, sh2, sc2, g2 = [m[:, None, :] for m in jnp.split(mod, 6, axis=-1)]

    bias_tile = jnp.repeat(b_spatial[0].T, HEAD, axis=1)
    ps, yb = _inproj(x, sc1, sh1, norm1_g, w_in[0].astype(BF16), mu_shift,
                     sgu_ln_g, sgu_ln_b, w_spatial[0], bias_tile, _tile(s, TILE_INPROJ))

    wlora = jnp.zeros((LANES, 2 * D_RWKV), F32)
    wlora = wlora.at[:D_DECAY_LORA, :D_RWKV].set(w_decay_up[0])
    wlora = wlora.at[D_DECAY_LORA:, D_RWKV:].set(w_a_up[0])
    seq = lambda z: z.reshape(b, s, D_RWKV)
    w_c, rh_c, u0_c, y0_c, bh_c, kh_c, pl_c, v, g, bonus = _chunk(
        ps.reshape(t, N_SHIFT), w0, a0, wlora, w_g_up[0], k_k, k_a, row(r_k))
    y = _rwkv(seq(w_c), seq(rh_c), seq(u0_c), seq(y0_c), seq(bh_c), seq(kh_c), seq(v),
              seq(pl_c), _tile(s, TILE_SCAN))

    tm_mid = _tile(s, TILE_MID)
    wr_t = jnp.pad(w_router[0].T, ((0, LANES - N_EXPERTS), (0, 0)))
    wo = w_out[0].astype(BF16)
    h2, logits_t, base = _mid(
        y.reshape(t, D_RWKV), g, bonus, lnx_g, lnx_b, yb.reshape(t, D_SGU), x.reshape(t, d),
        g1, sc2, sh2, g2, norm2_g, wo[:D_RWKV], wo[D_RWKV:], wr_t,
        w1_s[0].astype(BF16), w3_s[0].astype(BF16), w2_s[0].astype(BF16), tm_mid, s // tm_mid)

    tm_moe = _tile(t, TILE_MOE)
    gates_t, pos_t, fexp, fsub, fdense = _route(
        logits_t, jnp.broadcast_to(e_bias[0][:, None], (N_EXPERTS, tm_moe)), tm_moe)

    routed = _moe(fexp[:, 0], fsub[:, :tm_moe // MOE_SUB].reshape(-1), fdense[:, 0],
                  h2, gates_t, pos_t, w1_e[0].astype(BF16), w3_e[0].astype(BF16),
                  w2_e[0].astype(BF16), tm_moe)
    tm_fin = _tile(s, TILE_FINAL)
    out = _final(base, routed, g2, row(norm_f_g), tm_fin, s // tm_fin)
    return out.reshape(b, s, d)
```

```python
import functools

import jax
import jax.numpy as jnp
from jax import lax
from jax.experimental import pallas as pl
from jax.experimental.pallas import tpu as pltpu

F32 = jnp.float32
BF16 = jnp.bfloat16

HEAD = 64
LANES = 128
SUBLANES = 8
D_RWKV = 512
D_SGU = 512
N_RWKV_PAIRS = D_RWKV // LANES
D_DECAY_LORA = 64
D_AAA_LORA = 64
D_GATE_LORA = 128
N_SHIFT = 3 * D_RWKV + D_DECAY_LORA + D_AAA_LORA + D_GATE_LORA
SGU_BLOCK = 128
CHUNK = 64
N_EXPERTS = 64
N_GROUPS = 8
GROUP_SIZE = N_EXPERTS // N_GROUPS
TOPK_GROUPS = 4
TOP_K = 8
D_EXPERT = 256
ROUTED_SCALE = 2.5
RMS_EPS = 1e-6
LN_EPS = 1e-5
LN_X_EPS = 64e-5
VMEM_LIMIT_BYTES = 56 * 1024 * 1024

TILE_MOD_COLS = 1024
TILE_INPROJ = 512
TILE_SCAN = 512
TILE_MID = 512
TILE_MOE = 1024
TILE_FINAL = 512


def _params(semantics):
    return pltpu.CompilerParams(dimension_semantics=semantics,
                                vmem_limit_bytes=VMEM_LIMIT_BYTES)


def _dot3(a, b, dims=(((1,), (0,)), ((), ()))):
    ((ca,), (cb,)), _ = dims
    a_hi = a.astype(BF16)
    a_lo = (a - a_hi.astype(F32)).astype(BF16)
    b_hi = b.astype(BF16)
    b_lo = (b - b_hi.astype(F32)).astype(BF16)
    dg = functools.partial(lax.dot_general, dimension_numbers=dims,
                           preferred_element_type=F32)
    return (dg(jnp.concatenate([a_hi, a_lo], axis=ca), jnp.concatenate([b_hi, b_hi], axis=cb))
            + dg(a_hi, b_lo))


def _head_ones():
    r = lax.broadcasted_iota(jnp.int32, (LANES, LANES), 0) // HEAD
    c = lax.broadcasted_iota(jnp.int32, (LANES, LANES), 1) // HEAD
    return jnp.where(r == c, 1.0, 0.0).astype(BF16)


def _head_sum(z, ones):
    both = jnp.concatenate([ones, ones], axis=0)
    parts = []
    for p in range(z.shape[1] // LANES):
        zp = z[:, LANES * p:LANES * (p + 1)]
        hi = zp.astype(BF16)
        lo = (zp - hi.astype(F32)).astype(BF16)
        parts.append(jnp.dot(jnp.concatenate([hi, lo], axis=1), both,
                             preferred_element_type=F32))
    return jnp.concatenate(parts, axis=-1)


def _sigmoid(x):
    return 1.0 / (1.0 + jnp.exp(-x))


def _silu(x):
    return x * _sigmoid(x)


def _mod_kernel(c_ref, w_ref, b_ref, o_ref):
    s = _silu(c_ref[...])
    o_ref[...] = _dot3(s, w_ref[...]) + b_ref[...]


def _mod(c_pad, w_ada, b_ada):
    rows, d = c_pad.shape
    n = w_ada.shape[1]
    tn = TILE_MOD_COLS
    return pl.pallas_call(
        _mod_kernel,
        grid=(n // tn,),
        in_specs=[pl.BlockSpec((rows, d), lambda j: (0, 0)),
                  pl.BlockSpec((d, tn), lambda j: (0, j)),
                  pl.BlockSpec((1, tn), lambda j: (0, j))],
        out_specs=pl.BlockSpec((rows, tn), lambda j: (0, j)),
        out_shape=jax.ShapeDtypeStruct((rows, n), F32),
        compiler_params=_params(("arbitrary",)),
        name="adaln_mod",
    )(c_pad, w_ada, b_ada)


def _inproj_kernel(x_ref, sc_ref, sh_ref, g_ref, w_ref, mu_ref,
                   sgu_g_ref, sgu_b_ref, ws_ref, sgu_bias_ref, ps_ref, yb_ref, carry_ref):
    @pl.when(pl.program_id(1) == 0)
    def _():
        carry_ref[...] = jnp.zeros_like(carry_ref)

    x = x_ref[...]
    tm = x.shape[0]
    h = x * lax.rsqrt(jnp.mean(x * x, axis=-1, keepdims=True) + RMS_EPS) * g_ref[...]
    h = h * (1.0 + sc_ref[...]) + sh_ref[...]
    proj = jnp.dot(h.astype(BF16), w_ref[...], preferred_element_type=F32)
    ps = proj[:, :N_SHIFT]
    prev = pltpu.roll(ps, 1, 0)
    first = lax.broadcasted_iota(jnp.int32, ps.shape, 0) == 0
    prev = jnp.where(first, carry_ref[0:1, :], prev)
    carry_ref[0:1, :] = ps[tm - 1:tm, :]
    ps_ref[...] = ps + (prev - ps) * mu_ref[...]
    _sgu_apply(proj[:, N_SHIFT:], sgu_g_ref, sgu_b_ref, ws_ref, sgu_bias_ref, yb_ref)


def _inproj(x, sc1, sh1, g, w_in_bf16, mu, sgu_g, sgu_b, w_spatial, sgu_bias, tm):
    b, s, d = x.shape
    d_in = w_in_bf16.shape[1]
    assert d_in == N_SHIFT + 2 * D_SGU and tm % SGU_BLOCK == 0
    row = lambda bi, i: (bi, 0, 0)
    const = lambda shape: pl.BlockSpec(shape, lambda bi, i: (0,) * len(shape))
    return pl.pallas_call(
        _inproj_kernel,
        grid=(b, s // tm),
        in_specs=[pl.BlockSpec((None, tm, d), lambda bi, i: (bi, i, 0)),
                  pl.BlockSpec((None, 1, d), row),
                  pl.BlockSpec((None, 1, d), row),
                  const((1, d)), const((d, d_in)), const((1, N_SHIFT)),
                  const((1, D_SGU)), const((1, D_SGU)), const(w_spatial.shape),
                  const((SGU_BLOCK, D_SGU))],
        out_specs=[pl.BlockSpec((None, tm, N_SHIFT), lambda bi, i: (bi, i, 0)),
                   pl.BlockSpec((None, tm, D_SGU), lambda bi, i: (bi, i, 0))],
        out_shape=[jax.ShapeDtypeStruct((b, s, N_SHIFT), F32),
                   jax.ShapeDtypeStruct((b, s, D_SGU), BF16)],
        scratch_shapes=[pltpu.VMEM((8, N_SHIFT), F32)],
        compiler_params=_params(("arbitrary", "arbitrary")),
        name="inproj_shift_sgu",
    )(x, sc1, sh1, g, w_in_bf16, mu, sgu_g, sgu_b, w_spatial, sgu_bias)


def _token_terms(ps, w0_ref, a0_ref, wlora_ref, wg_ref, kk_ref, ka_ref, rk_ref):
    c1, c2, c3 = D_RWKV, 2 * D_RWKV, 3 * D_RWKV
    r, k, v = ps[:, :c1], ps[:, c1:c2], ps[:, c2:c3]
    xwa = ps[:, c3:c3 + LANES]
    xg = ps[:, c3 + LANES:]
    is_w = lax.broadcasted_iota(jnp.int32, xwa.shape, 1) < D_DECAY_LORA
    lora = _dot1(jnp.where(is_w, jnp.tanh(xwa), xwa), wlora_ref[...])
    wlog = -jax.nn.softplus(-(w0_ref[...] + lora[:, :c1])) - 0.5
    log_decay = -jnp.exp(wlog)
    a = _sigmoid(a0_ref[...] + lora[:, c1:])
    g = _dot1(_sigmoid(xg), wg_ref[...])
    ones = _head_ones()
    kk = k * kk_ref[...]
    kk = kk / jnp.maximum(jnp.sqrt(_head_sum(kk * kk, ones)), 1e-12)
    kh = k * (1.0 + (a - 1.0) * ka_ref[...])
    bonus = _head_sum(r * kh * rk_ref[...], ones) * v
    return r, log_decay, kh, v, kk, kk * a, g, bonus


RWKV_CHUNK = 32
CHUNK_TILE = 128
CHUNK_GROUP_PAIRS = 4

_NN = (((1,), (0,)), ((), ()))
_NT = (((1,), (1,)), ((), ()))


def _dot1(a, b, dims=_NN):
    return lax.dot_general(a.astype(BF16), b.astype(BF16), dims, preferred_element_type=F32)


def _dot_exact_lhs(lhs_bf16, x):
    hi = x.astype(BF16)
    rest = x - hi.astype(F32)
    mid = rest.astype(BF16)
    lo = (rest - mid.astype(F32)).astype(BF16)
    return jnp.dot(jnp.concatenate([lhs_bf16] * 3, axis=1),
                   jnp.concatenate([hi, mid, lo], axis=0), preferred_element_type=F32)


def _chunk_kernel(ps_ref, w0_ref, a0_ref, wlora_ref, wg_ref, kk_ref, ka_ref, rk_ref,
                  w_o, rh_o, u0_o, y0_o, bh_o, kh_o, pl_o, v_o, g_o, bonus_o):
    r_all, lw_all, k_all, v_all, kk_all, b_all, g_all, bonus_all = _token_terms(
        ps_ref[...], w0_ref, a0_ref, wlora_ref, wg_ref, kk_ref, ka_ref, rk_ref)
    v_o[...] = v_all
    g_o[...] = g_all
    bonus_o[...] = bonus_all
    n = CHUNK_TILE
    tpos = lax.broadcasted_iota(jnp.int32, (n, n), 0)
    spos = lax.broadcasted_iota(jnp.int32, (n, n), 1)
    same = (tpos // RWKV_CHUNK) == (spos // RWKV_CHUNK)
    incl = same & (spos <= tpos)
    strict = same & (spos < tpos)
    tri_ones = jnp.where(incl, 1.0, 0.0).astype(BF16)
    blk_ones = jnp.where(same, 1.0, 0.0).astype(BF16)
    eye = jnp.where(tpos == spos, 1.0, 0.0)
    lane_head = lax.broadcasted_iota(jnp.int32, (n, LANES), 1) // HEAD

    for first in range(0, N_RWKV_PAIRS, CHUNK_GROUP_PAIRS):
        pairs = range(CHUNK_GROUP_PAIRS)
        slabs = [slice(LANES * (first + p), LANES * (first + p + 1)) for p in pairs]
        lw, r, k, v, kk, b = ([x[:, sl] for sl in slabs]
                              for x in (lw_all, r_all, k_all, v_all, kk_all, b_all))
        cum = [_dot_exact_lhs(tri_ones, x) for x in lw]
        tot = [_dot_exact_lhs(blk_ones, x) for x in lw]
        at = [-kk[p] * jnp.exp(cum[p] - lw[p]) for p in pairs]
        rt = [r[p] * jnp.exp(cum[p]) for p in pairs]
        inv = [jnp.exp(-cum[p]) for p in pairs]
        bt = [b[p] * inv[p] for p in pairs]
        kt = [k[p] * inv[p] for p in pairs]

        chains = [(p, h) for p in pairs for h in range(LANES // HEAD)]
        mine = [lane_head == h for h in range(LANES // HEAD)]
        a_h = [jnp.where(mine[h], at[p], 0.0) for p, h in chains]
        r_h = [jnp.where(mine[h], rt[p], 0.0) for p, h in chains]
        v_h = [jnp.where(mine[h], v[p], 0.0) for p, h in chains]
        each = range(len(chains))
        btkt = [jnp.concatenate([bt[p], kt[p]], axis=0) for p in pairs]
        a_bk = [_dot3(a_h[c], btkt[chains[c][0]], _NT) for c in each]
        a_ab = [jnp.where(strict, m[:, :n], 0.0) for m in a_bk]
        a_ak = [jnp.where(strict, m[:, n:], 0.0) for m in a_bk]
        tinv = [eye + a for a in a_ab]
        x = a_ab
        for _ in range(RWKV_CHUNK.bit_length() - 2):
            x = [_dot1(xc, xc) for xc in x]
            tinv = [tc + _dot1(tc, xc) for tc, xc in zip(tinv, x)]
        akv = [_dot3(a_ak[c], v_h[c]) for c in each]
        m_bk = [_dot1(r_h[c], btkt[chains[c][0]], _NT) for c in each]
        m_rb = [jnp.where(incl, m[:, :n], 0.0) for m in m_bk]
        m_rk = [jnp.where(incl, m[:, n:], 0.0) for m in m_bk]
        wu = [_dot3(tinv[c], jnp.concatenate([a_h[c], akv[c]], axis=1)) for c in each]
        w_h = [m[:, :LANES] for m in wu]
        u0_h = [m[:, LANES:] for m in wu]
        ry = [_dot1(m_rb[c], wu[c]) for c in each]
        rh_h = [r_h[c] + ry[c][:, :LANES] for c in each]
        y0_h = [ry[c][:, LANES:] + _dot1(m_rk[c], v_h[c]) for c in each]

        for p in pairs:
            c0, c1 = 2 * p, 2 * p + 1
            rem = jnp.exp(tot[p] - cum[p])
            w_o[:, slabs[p]] = w_h[c0] + w_h[c1]
            rh_o[:, slabs[p]] = rh_h[c0] + rh_h[c1]
            u0_o[:, slabs[p]] = u0_h[c0] + u0_h[c1]
            y0_o[:, slabs[p]] = y0_h[c0] + y0_h[c1]
            bh_o[:, slabs[p]] = b[p] * rem
            kh_o[:, slabs[p]] = k[p] * rem
            pl_o[:, slabs[p]] = jnp.exp(tot[p])


def _chunk(ps, w0, a0, wlora, wg, k_k, k_a, r_k):
    t = ps.shape[0]
    const = lambda shape: pl.BlockSpec(shape, lambda i: (0, 0))
    spec = pl.BlockSpec((CHUNK_TILE, D_RWKV), lambda i: (i, 0))
    out = jax.ShapeDtypeStruct((t, D_RWKV), F32)
    return pl.pallas_call(
        _chunk_kernel,
        grid=(t // CHUNK_TILE,),
        in_specs=[pl.BlockSpec((CHUNK_TILE, N_SHIFT), lambda i: (i, 0)),
                  const((1, D_RWKV)), const((1, D_RWKV)),
                  const((LANES, 2 * D_RWKV)), const((D_GATE_LORA, D_RWKV)),
                  const((1, D_RWKV)), const((1, D_RWKV)), const((1, D_RWKV))],
        out_specs=[spec] * 10,
        out_shape=[out] * 10,
        compiler_params=_params(("arbitrary",)),
        name="rwkv_chunk_prep",
    )(ps, w0, a0, wlora, wg, k_k, k_a, r_k)


def _rwkv_kernel(w_ref, rh_ref, u0_ref, y0_ref, bh_ref, kh_ref, v_ref, pl_ref,
                 y_ref, state_ref):
    @pl.when(pl.program_id(0) == 0)
    def _():
        state_ref[...] = jnp.zeros_like(state_ref)

    nb, tt, _ = w_ref.shape
    L = RWKV_CHUNK
    depth = LANES * pl.cdiv(6 * L + SUBLANES, LANES)
    row_head = lax.broadcasted_iota(jnp.int32, (LANES, LANES), 0) // HEAD
    col_head = lax.broadcasted_iota(jnp.int32, (LANES, LANES), 1) // HEAD
    same_head = row_head == col_head
    pad = jnp.zeros((depth - 6 * L - SUBLANES, LANES), F32)
    zeros_tail = jnp.zeros((depth - 6 * L, LANES), BF16)

    def split(x):
        hi = x.astype(BF16)
        return hi, (x - hi.astype(F32)).astype(BF16)

    def chunk(ci, carry):
        t0 = pl.multiple_of(ci * L, L)
        streams = [(bi, p) for bi in range(nb) for p in range(N_RWKV_PAIRS)]

        def rows(ref, n=L):
            return [ref[bi, pl.ds(t0, n), LANES * p:LANES * (p + 1)] for bi, p in streams]

        each = range(len(streams))
        s = [state_ref[bi * N_RWKV_PAIRS + p] for bi, p in streams]
        w, rh, u0, y0, bh, kh, v = (rows(ref) for ref in
                                   (w_ref, rh_ref, u0_ref, y0_ref, bh_ref, kh_ref, v_ref))
        p_rows = rows(pl_ref, SUBLANES)
        left, p_col = [], []
        for c in each:
            (b_hi, b_lo), (k_hi, k_lo) = split(bh[c]), split(kh[c])
            block = jnp.concatenate([x.astype(F32) for x in (b_hi, b_lo, b_hi, k_hi, k_lo, k_hi)]
                                    + [p_rows[c], pad], axis=0).T
            left.append(block.astype(BF16))
            p_col.append(jnp.broadcast_to(block[:, 6 * L:6 * L + 1], (LANES, LANES)))
        g = []
        for c in each:
            (l_hi, l_lo), (s_hi, s_lo) = split(jnp.concatenate([w[c], rh[c]], axis=0)), split(s[c])
            g.append(jnp.dot(jnp.concatenate([l_hi, l_lo], axis=1),
                             jnp.concatenate([s_hi, s_hi], axis=0), preferred_element_type=F32)
                     + jnp.dot(l_hi, s_lo, preferred_element_type=F32))
        u = [g[c][:L] + u0[c] for c in each]
        upd = []
        for c in each:
            (u_hi, u_lo), (v_hi, v_lo) = split(u[c]), split(v[c])
            right = jnp.concatenate([u_hi, u_hi, u_lo, v_hi, v_hi, v_lo, zeros_tail], axis=0)
            upd.append(jnp.dot(left[c], right, preferred_element_type=F32))
        for c, (bi, p) in enumerate(streams):
            y_ref[bi, pl.ds(t0, L), LANES * p:LANES * (p + 1)] = g[c][L:] + y0[c]
            state_ref[bi * N_RWKV_PAIRS + p] = (p_col[c] * s[c]
                                                + jnp.where(same_head, upd[c], 0.0))
        return carry

    lax.fori_loop(0, tt // L, chunk, 0)


def _rwkv(w, rh, u0, y0, bh, kh, v, p_l, tt):
    nb, s, d = w.shape
    spec = pl.BlockSpec((nb, tt, d), lambda i: (0, i, 0))
    return pl.pallas_call(
        _rwkv_kernel,
        grid=(s // tt,),
        in_specs=[spec] * 8,
        out_specs=spec,
        out_shape=jax.ShapeDtypeStruct((nb, s, d), F32),
        scratch_shapes=[pltpu.VMEM((nb * N_RWKV_PAIRS, LANES, LANES), F32)],
        compiler_params=_params(("arbitrary",)),
        name="rwkv_scan",
    )(w, rh, u0, y0, bh, kh, v, p_l)


def _gelu_tanh(x):
    return 0.5 * x * (1.0 + jnp.tanh(0.7978845608028654 * (x + 0.044715 * (x * x * x))))


def _sgu_apply(pb, g_ref, b_ref, ws_ref, bias_ref, o_ref):
    z = _gelu_tanh(pb)
    u, v = z[:, :D_SGU], z[:, D_SGU:]
    mu = jnp.mean(v, axis=-1, keepdims=True)
    vc = v - mu
    var = jnp.mean(vc * vc, axis=-1, keepdims=True)
    vn = (vc * lax.rsqrt(var + LN_EPS) * g_ref[...] + b_ref[...]).astype(BF16)
    tpos = lax.broadcasted_iota(jnp.int32, (SGU_BLOCK, SGU_BLOCK), 0) // CHUNK
    spos = lax.broadcasted_iota(jnp.int32, (SGU_BLOCK, SGU_BLOCK), 1) // CHUNK
    causal = spos <= tpos
    first_head = lax.broadcasted_iota(jnp.int32, (SGU_BLOCK, LANES), 1) < HEAD
    for p in range(D_SGU // LANES):
        sl = slice(LANES * p, LANES * (p + 1))
        ws0 = jnp.where(causal, ws_ref[2 * p], 0.0).astype(BF16)
        ws1 = jnp.where(causal, ws_ref[2 * p + 1], 0.0).astype(BF16)
        for n in range(pb.shape[0] // SGU_BLOCK):
            rows = slice(SGU_BLOCK * n, SGU_BLOCK * (n + 1))
            vb = vn[rows, sl]
            sp = jnp.where(first_head,
                           jnp.dot(ws0, vb, preferred_element_type=F32),
                           jnp.dot(ws1, vb, preferred_element_type=F32))
            o_ref[rows, sl] = (u[rows, sl] * (sp + bias_ref[:, sl])).astype(o_ref.dtype)


def _mid_kernel(y_ref, g_ref, bonus_ref, lng_ref, lnb_ref, yb_ref, x_ref,
                g1_ref, sc2_ref, sh2_ref, g2_ref, n2_ref,
                woa_ref, wob_ref, wrt_ref, w1_ref, w3_ref, w2_ref,
                h2_ref, logit_ref, base_ref):
    ones = _head_ones()
    y = y_ref[...]
    mu = _head_sum(y, ones) * (1.0 / HEAD)
    yc = y - mu
    var = _head_sum(yc * yc, ones) * (1.0 / HEAD)
    ya = yc * lax.rsqrt(var + LN_X_EPS) * lng_ref[...] + lnb_ref[...]
    ya = ((ya + bonus_ref[...]) * g_ref[...]).astype(BF16)
    yo = (jnp.dot(ya, woa_ref[...], preferred_element_type=F32)
          + jnp.dot(yb_ref[...], wob_ref[...], preferred_element_type=F32))
    x1 = x_ref[...] + g1_ref[...] * yo
    h2 = x1 * lax.rsqrt(jnp.mean(x1 * x1, axis=-1, keepdims=True) + RMS_EPS) * n2_ref[...]
    h2 = h2 * (1.0 + sc2_ref[...]) + sh2_ref[...]
    h2b = h2.astype(BF16)
    h2_ref[...] = h2b
    logit_ref[...] = _dot1(wrt_ref[...], h2b, _NT)
    act = (_silu(jnp.dot(h2b, w1_ref[...], preferred_element_type=F32))
           * jnp.dot(h2b, w3_ref[...], preferred_element_type=F32)).astype(BF16)
    ms = jnp.dot(act, w2_ref[...], preferred_element_type=F32)
    base_ref[...] = x1 + g2_ref[...] * ms


def _mid(y, g, bonus, lnx_g, lnx_b, yb, x, g1, sc2, sh2, g2, n2g,
         wo_a, wo_b, wr_t, w1s, w3s, w2s, tm, tiles_per_batch):
    t, d = x.shape
    tok = lambda w: pl.BlockSpec((tm, w), lambda i: (i, 0))
    const = lambda shape: pl.BlockSpec(shape, lambda i: (0,) * len(shape))
    mod = pl.BlockSpec((None, 1, d), lambda i: (i // tiles_per_batch, 0, 0))
    return pl.pallas_call(
        _mid_kernel,
        grid=(t // tm,),
        in_specs=[tok(D_RWKV), tok(D_RWKV), tok(D_RWKV),
                  const((1, D_RWKV)), const((1, D_RWKV)),
                  tok(D_SGU), tok(d), mod, mod, mod, mod, const((1, d)),
                  const(wo_a.shape), const(wo_b.shape), const(wr_t.shape),
                  const(w1s.shape), const(w3s.shape), const(w2s.shape)],
        out_specs=[tok(d), pl.BlockSpec((LANES, tm), lambda i: (0, i)), tok(d)],
        out_shape=[jax.ShapeDtypeStruct((t, d), BF16),
                   jax.ShapeDtypeStruct((LANES, t), F32),
                   jax.ShapeDtypeStruct((t, d), F32)],
        compiler_params=_params(("arbitrary",)),
        name="outproj_norm_router_shared",
    )(y, g, bonus, lnx_g, lnx_b, yb, x, g1, sc2, sh2, g2, n2g,
      wo_a, wo_b, wr_t, w1s, w3s, w2s)


MOE_SUB = 128
MOE_CAP = 32
MOE_GROUP = 8


def _route_kernel(logit_ref, bias_ref, gate_ref, pos_ref, fexp_ref, fsub_ref, fdense_ref):
    tt = logit_ref.shape[1]
    scores = _sigmoid(logit_ref[0:N_EXPERTS, :])
    biased = scores + bias_ref[...]
    neg_inf = -jnp.inf

    group_blocks, group_scores = [], []
    for gi in range(N_GROUPS):
        blk = biased[GROUP_SIZE * gi:GROUP_SIZE * (gi + 1), :]
        m1 = jnp.max(blk, axis=0, keepdims=True)
        is_max = blk == m1
        n_max = jnp.sum(jnp.where(is_max, 1.0, 0.0), axis=0, keepdims=True)
        m2 = jnp.max(jnp.where(is_max, neg_inf, blk), axis=0, keepdims=True)
        group_blocks.append(blk)
        group_scores.append(m1 + jnp.where(n_max >= 2.0, m1, m2))

    masked = []
    for gi in range(N_GROUPS):
        rank = jnp.zeros((1, tt), F32)
        for gj in range(N_GROUPS):
            if gj == gi:
                continue
            ahead = (group_scores[gj] >= group_scores[gi]) if gj < gi else \
                    (group_scores[gj] > group_scores[gi])
            rank += jnp.where(ahead, 1.0, 0.0)
        masked.append(jnp.where(rank < float(TOPK_GROUPS), group_blocks[gi], neg_inf))
    sub = lax.broadcasted_iota(jnp.int32, (GROUP_SIZE, tt), 0).astype(F32)
    index = [sub + float(GROUP_SIZE * gi) for gi in range(N_GROUPS)]
    picked = [jnp.zeros((GROUP_SIZE, tt), F32) for _ in range(N_GROUPS)]

    def over_experts(op, blocks):
        acc = blocks[0]
        for blk in blocks[1:]:
            acc = op(acc, blk)
        return acc

    for _ in range(TOP_K):
        best = jnp.max(over_experts(jnp.maximum, masked), axis=0, keepdims=True)
        first = jnp.min(over_experts(jnp.minimum, [
            jnp.where(masked[gi] == best, index[gi], float(N_EXPERTS))
            for gi in range(N_GROUPS)]), axis=0, keepdims=True)
        for gi in range(N_GROUPS):
            hit = index[gi] == first
            picked[gi] = jnp.where(hit, 1.0, picked[gi])
            masked[gi] = jnp.where(hit, neg_inf, masked[gi])
    chosen = jnp.concatenate(picked, axis=0)
    sw = chosen * scores
    gate_ref[...] = sw / jnp.sum(sw, axis=0, keepdims=True) * ROUTED_SCALE

    before = (lax.broadcasted_iota(jnp.int32, (MOE_SUB, MOE_SUB), 0)
              < lax.broadcasted_iota(jnp.int32, (MOE_SUB, MOE_SUB), 1))
    prefix_ones = jnp.where(before, 1.0, 0.0).astype(BF16)
    all_ones = jnp.ones((MOE_SUB, MOE_SUB), BF16)
    lane = lax.broadcasted_iota(jnp.int32, (N_EXPERTS, LANES), 1)
    positions, counts = [], jnp.zeros((N_EXPERTS, LANES), F32)
    for s in range(tt // MOE_SUB):
        blk = chosen[:, MOE_SUB * s:MOE_SUB * (s + 1)].astype(BF16)
        prefix = jnp.dot(blk, prefix_ones, preferred_element_type=F32)
        positions.append(jnp.where(blk > 0, prefix, -1.0))
        total = jnp.dot(blk, all_ones, preferred_element_type=F32)
        counts = counts + jnp.where(lane == s, total, 0.0)
    pos_ref[...] = jnp.concatenate(positions, axis=1)
    per_expert = jnp.max(counts, axis=1, keepdims=True)
    fexp_ref[...] = jnp.broadcast_to(jnp.where(per_expert > float(MOE_CAP), 1, 0),
                                     fexp_ref.shape).astype(jnp.int32)
    per_group = jnp.concatenate(
        [jnp.max(counts[MOE_GROUP * g:MOE_GROUP * (g + 1), :], axis=0, keepdims=True)
         for g in range(N_EXPERTS // MOE_GROUP)], axis=0)
    fsub_ref[...] = jnp.where(per_group > float(MOE_CAP), 1, 0).astype(jnp.int32)
    group_worst = jnp.max(per_group, axis=1, keepdims=True)
    fdense_ref[...] = jnp.broadcast_to(jnp.where(group_worst > float(2 * MOE_CAP), 1, 0),
                                       fdense_ref.shape).astype(jnp.int32)


def _route(logits_t, bias_tile, tt):
    t = logits_t.shape[1]
    n_tiles = t // tt
    n_groups = N_EXPERTS // MOE_GROUP
    tok = pl.BlockSpec((N_EXPERTS, tt), lambda i: (0, i))
    return pl.pallas_call(
        _route_kernel,
        grid=(n_tiles,),
        in_specs=[pl.BlockSpec((LANES, tt), lambda i: (0, i)),
                  pl.BlockSpec((N_EXPERTS, tt), lambda i: (0, 0))],
        out_specs=[tok, tok,
                   pl.BlockSpec((N_EXPERTS, LANES), lambda i: (i, 0)),
                   pl.BlockSpec((n_groups, LANES), lambda i: (i, 0)),
                   pl.BlockSpec((n_groups, LANES), lambda i: (i, 0))],
        out_shape=[jax.ShapeDtypeStruct((N_EXPERTS, t), F32),
                   jax.ShapeDtypeStruct((N_EXPERTS, t), F32),
                   jax.ShapeDtypeStruct((n_tiles * N_EXPERTS, LANES), jnp.int32),
                   jax.ShapeDtypeStruct((n_tiles * n_groups, LANES), jnp.int32),
                   jax.ShapeDtypeStruct((n_tiles * n_groups, LANES), jnp.int32)],
        compiler_params=_params(("arbitrary",)),
        name="route_topk",
    )(logits_t, bias_tile)


def _moe_kernel(fexp_ref, fsub_ref, fdense_ref, h2_ref, gate_ref, pos_ref,
                w1_ref, w3_ref, w2_ref, o_ref, xy_ref, gb_ref, pt_ref):
    tile, eg = pl.program_id(0), pl.program_id(1)
    n_groups = pl.num_programs(1)
    tm = h2_ref.shape[0]
    n_sub = tm // MOE_SUB

    @pl.when(eg == 0)
    def _():
        o_ref[...] = jnp.zeros_like(o_ref)

    @pl.when((tile == 0) & (eg == 0))
    def _():
        xy_ref[...] = jnp.zeros_like(xy_ref)
        gb_ref[...] = jnp.zeros_like(gb_ref)

    def lane_broadcast(m):
        return jnp.broadcast_to(jnp.sum(m, axis=-1, keepdims=True), (m.shape[0], D_EXPERT))

    def hidden(x, gate_rows, j):
        return (_silu(jnp.dot(x, w1_ref[j], preferred_element_type=F32))
                * jnp.dot(x, w3_ref[j], preferred_element_type=F32) * gate_rows).astype(BF16)

    def pipelined(n, first, second):
        pending = first(0)
        for i in range(1, n):
            nxt = first(i)
            second(i - 1, pending)
            pending = nxt
        second(n - 1, pending)

    cap = MOE_CAP
    slab_rows = n_sub * cap
    go_dense = fdense_ref[tile * n_groups + eg] != 0

    def needs_second(s):
        return fsub_ref[(tile * n_groups + eg) * n_sub + s] != 0

    def slab_slice(j, q, s=None):
        start = (2 * j + q) * slab_rows
        if s is None:
            return slice(start, start + slab_rows)
        return slice(start + s * cap, start + (s + 1) * cap)

    def gather(s, q):
        cols = slice(MOE_SUB * s, MOE_SUB * (s + 1))
        slot = lax.broadcasted_iota(jnp.int32, (cap, MOE_SUB), 0).astype(F32) + float(q * cap)
        pos, gate = pos_ref[:, cols], gate_ref[:, cols]
        hot = [jnp.where(slot == pos[j:j + 1, :], 1.0, 0.0) for j in range(MOE_GROUP)]
        p = jnp.concatenate(hot, axis=0)
        pg = jnp.concatenate([hot[j] * gate[j:j + 1, :] for j in range(MOE_GROUP)], axis=0)
        xg = jnp.dot(p.astype(BF16), h2_ref[cols, :],
                     preferred_element_type=F32).astype(BF16)
        gb = lane_broadcast(pg)
        for j in range(MOE_GROUP):
            xy_ref[slab_slice(j, q, s), :] = xg[j * cap:(j + 1) * cap]
            gb_ref[slab_slice(j, q, s), :] = gb[j * cap:(j + 1) * cap]
        pt_ref[s, :, q * MOE_GROUP * cap:(q + 1) * MOE_GROUP * cap] = p.T.astype(BF16)

    def scatter(s, q):
        y = jnp.concatenate([xy_ref[slab_slice(j, q, s), :] for j in range(MOE_GROUP)], axis=0)
        o_ref[MOE_SUB * s:MOE_SUB * (s + 1), :] += jnp.dot(
            pt_ref[s, :, q * MOE_GROUP * cap:(q + 1) * MOE_GROUP * cap], y,
            preferred_element_type=F32)

    def project(j, q, act):
        xy_ref[slab_slice(j, q), :] = jnp.dot(
            act, w2_ref[j], preferred_element_type=F32).astype(BF16)

    @pl.when(jnp.logical_not(go_dense))
    def _():
        for s in range(n_sub):
            gather(s, 0)
        pipelined(MOE_GROUP,
                  lambda j: hidden(xy_ref[slab_slice(j, 0), :], gb_ref[slab_slice(j, 0), :], j),
                  lambda j, act: project(j, 0, act))
        for s in range(n_sub):
            scatter(s, 0)
        for s in range(n_sub):
            pl.when(needs_second(s))(functools.partial(gather, s, 1))
        for j in range(MOE_GROUP):
            @pl.when(fexp_ref[tile * N_EXPERTS + eg * MOE_GROUP + j] != 0)
            def _(j=j):
                project(j, 1, hidden(xy_ref[slab_slice(j, 1), :],
                                     gb_ref[slab_slice(j, 1), :], j))
        for s in range(n_sub):
            pl.when(needs_second(s))(functools.partial(scatter, s, 1))

    @pl.when(go_dense)
    def _():
        eye = (lax.broadcasted_iota(jnp.int32, (MOE_SUB, MOE_SUB), 0)
               == lax.broadcasted_iota(jnp.int32, (MOE_SUB, MOE_SUB), 1))
        for j in range(MOE_GROUP):
            for s in range(n_sub):
                cols = slice(MOE_SUB * s, MOE_SUB * (s + 1))
                gb_ref[cols, :] = lane_broadcast(
                    jnp.where(eye, gate_ref[j:j + 1, cols], 0.0))
            act = hidden(h2_ref[...], gb_ref[0:tm, :], j)
            o_ref[...] += jnp.dot(act, w2_ref[j], preferred_element_type=F32)


def _moe(fexp, fsub, fdense, h2, gates_t, pos_t, w1e, w3e, w2e, tm):
    t, d = h2.shape
    ne = w1e.shape[0]
    assert ne % MOE_GROUP == 0 and tm % MOE_SUB == 0
    n_sub = tm // MOE_SUB
    rows = MOE_GROUP * 2 * n_sub * MOE_CAP
    assert rows >= tm, "gate scratch is reused by the dense path"
    tok = pl.BlockSpec((tm, d), lambda i, e, *_: (i, 0))
    per_expert = pl.BlockSpec((MOE_GROUP, tm), lambda i, e, *_: (e, i))
    return pl.pallas_call(
        _moe_kernel,
        grid_spec=pltpu.PrefetchScalarGridSpec(
            num_scalar_prefetch=3,
            grid=(t // tm, ne // MOE_GROUP),
            in_specs=[tok, per_expert, per_expert,
                      pl.BlockSpec((MOE_GROUP, d, D_EXPERT), lambda i, e, *_: (e, 0, 0)),
                      pl.BlockSpec((MOE_GROUP, d, D_EXPERT), lambda i, e, *_: (e, 0, 0)),
                      pl.BlockSpec((MOE_GROUP, D_EXPERT, d), lambda i, e, *_: (e, 0, 0))],
            out_specs=tok,
            scratch_shapes=[pltpu.VMEM((rows, d), BF16),
                            pltpu.VMEM((rows, D_EXPERT), F32),
                            pltpu.VMEM((n_sub, MOE_SUB, 2 * MOE_GROUP * MOE_CAP), BF16)]),
        out_shape=jax.ShapeDtypeStruct((t, d), F32),
        compiler_params=_params(("arbitrary", "arbitrary")),
        name="moe_experts",
    )(fexp, fsub, fdense, h2, gates_t, pos_t, w1e, w3e, w2e)


def _final_kernel(base_ref, routed_ref, g2_ref, nf_ref, o_ref):
    xo = base_ref[...] + g2_ref[...] * routed_ref[...]
    o_ref[...] = (xo * lax.rsqrt(jnp.mean(xo * xo, axis=-1, keepdims=True) + RMS_EPS)
                  * nf_ref[...])


def _final(base, routed, g2, nf, tm, tiles_per_batch):
    t, d = base.shape
    tok = pl.BlockSpec((tm, d), lambda i: (i, 0))
    return pl.pallas_call(
        _final_kernel,
        grid=(t // tm,),
        in_specs=[tok, tok,
                  pl.BlockSpec((None, 1, d), lambda i: (i // tiles_per_batch, 0, 0)),
                  pl.BlockSpec((1, d), lambda i: (0, 0))],
        out_specs=tok,
        out_shape=jax.ShapeDtypeStruct((t, d), F32),
        compiler_params=_params(("arbitrary",)),
        name="residual_final_norm",
    )(base, routed, g2, nf)


def _tile(n, want):
    t = min(n, want)
    assert n % t == 0, (n, t)
    return t


def kernel(x, c, w_ada, b_ada, norm1_g, w_in, mu_shift, w0, w_decay_up, a0, w_a_up, w_g_up,
           k_k, k_a, r_k, lnx_g, lnx_b, sgu_ln_g, sgu_ln_b, w_spatial, b_spatial, w_out,
           norm2_g, w_router, e_bias, w1_e, w3_e, w2_e, w1_s, w3_s, w2_s, norm_f_g):
    b, s, d = x.shape
    t = b * s
    assert w_ada.shape[0] == 1, "single-layer block"
    assert d == 2 * D_RWKV and s % SGU_BLOCK == 0
    row = lambda vec: vec.reshape(1, -1)

    c_pad = jnp.pad(c, ((0, 8 - b), (0, 0)))
    mod = _mod(c_pad, w_ada[0], b_ada)[:b]
    sh1, sc1, g1, sh2, sc2, g2 = [m[:, None, :] for m in jnp.split(mod, 6, axis=-1)]

    bias_tile = jnp.repeat(b_spatial[0].T, HEAD, axis=1)
    ps, yb = _inproj(x, sc1, sh1, norm1_g, w_in[0].astype(BF16), mu_shift,
                     sgu_ln_g, sgu_ln_b, w_spatial[0], bias_tile, _tile(s, TILE_INPROJ))

    wlora = jnp.zeros((LANES, 2 * D_RWKV), F32)
    wlora = wlora.at[:D_DECAY_LORA, :D_RWKV].set(w_decay_up[0])
    wlora = wlora.at[D_DECAY_LORA:, D_RWKV:].set(w_a_up[0])
    seq = lambda z: z.reshape(b, s, D_RWKV)
    w_c, rh_c, u0_c, y0_c, bh_c, kh_c, pl_c, v, g, bonus = _chunk(
        ps.reshape(t, N_SHIFT), w0, a0, wlora, w_g_up[0], k_k, k_a, row(r_k))
    y = _rwkv(seq(w_c), seq(rh_c), seq(u0_c), seq(y0_c), seq(bh_c), seq(kh_c), seq(v),
              seq(pl_c), _tile(s, TILE_SCAN))

    tm_mid = _tile(s, TILE_MID)
    wr_t = jnp.pad(w_router[0].T, ((0, LANES - N_EXPERTS), (0, 0)))
    wo = w_out[0].astype(BF16)
    h2, logits_t, base = _mid(
        y.reshape(t, D_RWKV), g, bonus, lnx_g, lnx_b, yb.reshape(t, D_SGU), x.reshape(t, d),
        g1, sc2, sh2, g2, norm2_g, wo[:D_RWKV], wo[D_RWKV:], wr_t,
        w1_s[0].astype(BF16), w3_s[0].astype(BF16), w2_s[0].astype(BF16), tm_mid, s // tm_mid)

    tm_moe = _tile(t, TILE_MOE)
    gates_t, pos_t, fexp, fsub, fdense = _route(
        logits_t, jnp.broadcast_to(e_bias[0][:, None], (N_EXPERTS, tm_moe)), tm_moe)

    routed = _moe(fexp[:, 0], fsub[:, :tm_moe // MOE_SUB].reshape(-1), fdense[:, 0],
                  h2, gates_t, pos_t, w1_e[0].astype(BF16), w3_e[0].astype(BF16),
                  w2_e[0].astype(BF16), tm_moe)
    tm_fin = _tile(s, TILE_FINAL)
    out = _final(base, routed, g2, row(norm_f_g), tm_fin, s // tm_fin)
    return out.reshape(b, s, d)
```

```python
import functools

import jax
import jax.numpy as jnp
from jax import lax
from jax.experimental import pallas as pl
from jax.experimental.pallas import tpu as pltpu

F32 = jnp.float32
BF16 = jnp.bfloat16

HEAD = 64
LANES = 128
SUBLANES = 8
D_RWKV = 512
D_SGU = 512
N_RWKV_PAIRS = D_RWKV // LANES
D_DECAY_LORA = 64
D_AAA_LORA = 64
D_GATE_LORA = 128
N_SHIFT = 3 * D_RWKV + D_DECAY_LORA + D_AAA_LORA + D_GATE_LORA
SGU_BLOCK = 128
CHUNK = 64
N_EXPERTS = 64
N_GROUPS = 8
GROUP_SIZE = N_EXPERTS // N_GROUPS
TOPK_GROUPS = 4
TOP_K = 8
D_EXPERT = 256
ROUTED_SCALE = 2.5
RMS_EPS = 1e-6
LN_EPS = 1e-5
LN_X_EPS = 64e-5
VMEM_LIMIT_BYTES = 56 * 1024 * 1024

TILE_MOD_COLS = 1024
TILE_INPROJ = 512
TILE_SCAN = 512
TILE_MID = 512
TILE_MOE = 1024
TILE_FINAL = 512


def _params(semantics):
    return pltpu.CompilerParams(dimension_semantics=semantics,
                                vmem_limit_bytes=VMEM_LIMIT_BYTES)


def _dot3(a, b, dims=(((1,), (0,)), ((), ()))):
    ((ca,), (cb,)), _ = dims
    a_hi = a.astype(BF16)
    a_lo = (a - a_hi.astype(F32)).astype(BF16)
    b_hi = b.astype(BF16)
    b_lo = (b - b_hi.astype(F32)).astype(BF16)
    dg = functools.partial(lax.dot_general, dimension_numbers=dims,
                           preferred_element_type=F32)
    return (dg(jnp.concatenate([a_hi, a_lo], axis=ca), jnp.concatenate([b_hi, b_hi], axis=cb))
            + dg(a_hi, b_lo))


def _head_ones():
    r = lax.broadcasted_iota(jnp.int32, (LANES, LANES), 0) // HEAD
    c = lax.broadcasted_iota(jnp.int32, (LANES, LANES), 1) // HEAD
    return jnp.where(r == c, 1.0, 0.0).astype(BF16)


def _head_sum(z, ones):
    both = jnp.concatenate([ones, ones], axis=0)
    parts = []
    for p in range(z.shape[1] // LANES):
        zp = z[:, LANES * p:LANES * (p + 1)]
        hi = zp.astype(BF16)
        lo = (zp - hi.astype(F32)).astype(BF16)
        parts.append(jnp.dot(jnp.concatenate([hi, lo], axis=1), both,
                             preferred_element_type=F32))
    return jnp.concatenate(parts, axis=-1)


def _sigmoid(x):
    return 1.0 / (1.0 + jnp.exp(-x))


def _silu(x):
    return x * _sigmoid(x)


def _mod_kernel(c_ref, w_ref, b_ref, o_ref):
    s = _silu(c_ref[...])
    o_ref[...] = _dot3(s, w_ref[...]) + b_ref[...]


def _mod(c_pad, w_ada, b_ada):
    rows, d = c_pad.shape
    n = w_ada.shape[1]
    tn = TILE_MOD_COLS
    return pl.pallas_call(
        _mod_kernel,
        grid=(n // tn,),
        in_specs=[pl.BlockSpec((rows, d), lambda j: (0, 0)),
                  pl.BlockSpec((d, tn), lambda j: (0, j)),
                  pl.BlockSpec((1, tn), lambda j: (0, j))],
        out_specs=pl.BlockSpec((rows, tn), lambda j: (0, j)),
        out_shape=jax.ShapeDtypeStruct((rows, n), F32),
        compiler_params=_params(("arbitrary",)),
        name="adaln_mod",
    )(c_pad, w_ada, b_ada)


def _inproj_kernel(x_ref, sc_ref, sh_ref, g_ref, w_ref, mu_ref,
                   sgu_g_ref, sgu_b_ref, ws_ref, sgu_bias_ref, ps_ref, yb_ref, carry_ref):
    @pl.when(pl.program_id(1) == 0)
    def _():
        carry_ref[...] = jnp.zeros_like(carry_ref)

    x = x_ref[...]
    tm = x.shape[0]
    h = x * lax.rsqrt(jnp.mean(x * x, axis=-1, keepdims=True) + RMS_EPS) * g_ref[...]
    h = h * (1.0 + sc_ref[...]) + sh_ref[...]
    proj = jnp.dot(h.astype(BF16), w_ref[...], preferred_element_type=F32)
    ps = proj[:, :N_SHIFT]
    prev = pltpu.roll(ps, 1, 0)
    first = lax.broadcasted_iota(jnp.int32, ps.shape, 0) == 0
    prev = jnp.where(first, carry_ref[0:1, :], prev)
    carry_ref[0:1, :] = ps[tm - 1:tm, :]
    ps_ref[...] = ps + (prev - ps) * mu_ref[...]
    _sgu_apply(proj[:, N_SHIFT:], sgu_g_ref, sgu_b_ref, ws_ref, sgu_bias_ref, yb_ref)


def _inproj(x, sc1, sh1, g, w_in_bf16, mu, sgu_g, sgu_b, w_spatial, sgu_bias, tm):
    b, s, d = x.shape
    d_in = w_in_bf16.shape[1]
    assert d_in == N_SHIFT + 2 * D_SGU and tm % SGU_BLOCK == 0
    row = lambda bi, i: (bi, 0, 0)
    const = lambda shape: pl.BlockSpec(shape, lambda bi, i: (0,) * len(shape))
    return pl.pallas_call(
        _inproj_kernel,
        grid=(b, s // tm),
        in_specs=[pl.BlockSpec((None, tm, d), lambda bi, i: (bi, i, 0)),
                  pl.BlockSpec((None, 1, d), row),
                  pl.BlockSpec((None, 1, d), row),
                  const((1, d)), const((d, d_in)), const((1, N_SHIFT)),
                  const((1, D_SGU)), const((1, D_SGU)), const(w_spatial.shape),
                  const((SGU_BLOCK, D_SGU))],
        out_specs=[pl.BlockSpec((None, tm, N_SHIFT), lambda bi, i: (bi, i, 0)),
                   pl.BlockSpec((None, tm, D_SGU), lambda bi, i: (bi, i, 0))],
        out_shape=[jax.ShapeDtypeStruct((b, s, N_SHIFT), F32),
                   jax.ShapeDtypeStruct((b, s, D_SGU), BF16)],
        scratch_shapes=[pltpu.VMEM((8, N_SHIFT), F32)],
        compiler_params=_params(("arbitrary", "arbitrary")),
        name="inproj_shift_sgu",
    )(x, sc1, sh1, g, w_in_bf16, mu, sgu_g, sgu_b, w_spatial, sgu_bias)


def _token_terms(ps, w0_ref, a0_ref, wlora_ref, wg_ref, kk_ref, ka_ref, rk_ref):
    c1, c2, c3 = D_RWKV, 2 * D_RWKV, 3 * D_RWKV
    r, k, v = ps[:, :c1], ps[:, c1:c2], ps[:, c2:c3]
    xwa = ps[:, c3:c3 + LANES]
    xg = ps[:, c3 + LANES:]
    is_w = lax.broadcasted_iota(jnp.int32, xwa.shape, 1) < D_DECAY_LORA
    lora = _dot1(jnp.where(is_w, jnp.tanh(xwa), xwa), wlora_ref[...])
    wlog = -jax.nn.softplus(-(w0_ref[...] + lora[:, :c1])) - 0.5
    log_decay = -jnp.exp(wlog)
    a = _sigmoid(a0_ref[...] + lora[:, c1:])
    g = _dot1(_sigmoid(xg), wg_ref[...])
    ones = _head_ones()
    kk = k * kk_ref[...]
    kk = kk / jnp.maximum(jnp.sqrt(_head_sum(kk * kk, ones)), 1e-12)
    kh = k * (1.0 + (a - 1.0) * ka_ref[...])
    bonus = _head_sum(r * kh * rk_ref[...], ones) * v
    return r, log_decay, kh, v, kk, kk * a, g, bonus


RWKV_CHUNK = 64
CHUNK_TILE = 128
CHUNK_GROUP_PAIRS = 4

_NN = (((1,), (0,)), ((), ()))
_NT = (((1,), (1,)), ((), ()))


def _dot1(a, b, dims=_NN):
    return lax.dot_general(a.astype(BF16), b.astype(BF16), dims, preferred_element_type=F32)


def _dot_exact_lhs(lhs_bf16, x):
    hi = x.astype(BF16)
    rest = x - hi.astype(F32)
    mid = rest.astype(BF16)
    lo = (rest - mid.astype(F32)).astype(BF16)
    return jnp.dot(jnp.concatenate([lhs_bf16] * 3, axis=1),
                   jnp.concatenate([hi, mid, lo], axis=0), preferred_element_type=F32)


def _chunk_kernel(ps_ref, w0_ref, a0_ref, wlora_ref, wg_ref, kk_ref, ka_ref, rk_ref,
                  w_o, rh_o, u0_o, y0_o, bh_o, kh_o, pl_o, v_o, g_o, bonus_o):
    r_all, lw_all, k_all, v_all, kk_all, b_all, g_all, bonus_all = _token_terms(
        ps_ref[...], w0_ref, a0_ref, wlora_ref, wg_ref, kk_ref, ka_ref, rk_ref)
    v_o[...] = v_all
    g_o[...] = g_all
    bonus_o[...] = bonus_all
    n = CHUNK_TILE
    tpos = lax.broadcasted_iota(jnp.int32, (n, n), 0)
    spos = lax.broadcasted_iota(jnp.int32, (n, n), 1)
    same = (tpos // RWKV_CHUNK) == (spos // RWKV_CHUNK)
    incl = same & (spos <= tpos)
    strict = same & (spos < tpos)
    tri_ones = jnp.where(incl, 1.0, 0.0).astype(BF16)
    blk_ones = jnp.where(same, 1.0, 0.0).astype(BF16)
    eye = jnp.where(tpos == spos, 1.0, 0.0)
    lane_head = lax.broadcasted_iota(jnp.int32, (n, LANES), 1) // HEAD

    for first in range(0, N_RWKV_PAIRS, CHUNK_GROUP_PAIRS):
        pairs = range(CHUNK_GROUP_PAIRS)
        slabs = [slice(LANES * (first + p), LANES * (first + p + 1)) for p in pairs]
        lw, r, k, v, kk, b = ([x[:, sl] for sl in slabs]
                              for x in (lw_all, r_all, k_all, v_all, kk_all, b_all))
        cum = [_dot_exact_lhs(tri_ones, x) for x in lw]
        tot = [_dot_exact_lhs(blk_ones, x) for x in lw]
        at = [-kk[p] * jnp.exp(cum[p] - lw[p]) for p in pairs]
        rt = [r[p] * jnp.exp(cum[p]) for p in pairs]
        inv = [jnp.exp(-cum[p]) for p in pairs]
        bt = [b[p] * inv[p] for p in pairs]
        kt = [k[p] * inv[p] for p in pairs]

        chains = [(p, h) for p in pairs for h in range(LANES // HEAD)]
        mine = [lane_head == h for h in range(LANES // HEAD)]
        a_h = [jnp.where(mine[h], at[p], 0.0) for p, h in chains]
        r_h = [jnp.where(mine[h], rt[p], 0.0) for p, h in chains]
        v_h = [jnp.where(mine[h], v[p], 0.0) for p, h in chains]
        each = range(len(chains))
        btkt = [jnp.concatenate([bt[p], kt[p]], axis=0) for p in pairs]
        a_bk = [_dot3(a_h[c], btkt[chains[c][0]], _NT) for c in each]
        a_ab = [jnp.where(strict, m[:, :n], 0.0) for m in a_bk]
        a_ak = [jnp.where(strict, m[:, n:], 0.0) for m in a_bk]
        tinv = [eye + a for a in a_ab]
        x = a_ab
        for _ in range(RWKV_CHUNK.bit_length() - 2):
            x = [_dot1(xc, xc) for xc in x]
            tinv = [tc + _dot1(tc, xc) for tc, xc in zip(tinv, x)]
        akv = [_dot3(a_ak[c], v_h[c]) for c in each]
        m_bk = [_dot1(r_h[c], btkt[chains[c][0]], _NT) for c in each]
        m_rb = [jnp.where(incl, m[:, :n], 0.0) for m in m_bk]
        m_rk = [jnp.where(incl, m[:, n:], 0.0) for m in m_bk]
        wu = [_dot3(tinv[c], jnp.concatenate([a_h[c], akv[c]], axis=1)) for c in each]
        w_h = [m[:, :LANES] for m in wu]
        u0_h = [m[:, LANES:] for m in wu]
        ry = [_dot1(m_rb[c], wu[c]) for c in each]
        rh_h = [r_h[c] + ry[c][:, :LANES] for c in each]
        y0_h = [ry[c][:, LANES:] + _dot1(m_rk[c], v_h[c]) for c in each]

        for p in pairs:
            c0, c1 = 2 * p, 2 * p + 1
            rem = jnp.exp(tot[p] - cum[p])
            w_o[:, slabs[p]] = w_h[c0] + w_h[c1]
            rh_o[:, slabs[p]] = rh_h[c0] + rh_h[c1]
            u0_o[:, slabs[p]] = u0_h[c0] + u0_h[c1]
            y0_o[:, slabs[p]] = y0_h[c0] + y0_h[c1]
            bh_o[:, slabs[p]] = b[p] * rem
            kh_o[:, slabs[p]] = k[p] * rem
            pl_o[:, slabs[p]] = jnp.exp(tot[p])


def _chunk(ps, w0, a0, wlora, wg, k_k, k_a, r_k):
    t = ps.shape[0]
    const = lambda shape: pl.BlockSpec(shape, lambda i: (0, 0))
    spec = pl.BlockSpec((CHUNK_TILE, D_RWKV), lambda i: (i, 0))
    out = jax.ShapeDtypeStruct((t, D_RWKV), F32)
    return pl.pallas_call(
        _chunk_kernel,
        grid=(t // CHUNK_TILE,),
        in_specs=[pl.BlockSpec((CHUNK_TILE, N_SHIFT), lambda i: (i, 0)),
                  const((1, D_RWKV)), const((1, D_RWKV)),
                  const((LANES, 2 * D_RWKV)), const((D_GATE_LORA, D_RWKV)),
                  const((1, D_RWKV)), const((1, D_RWKV)), const((1, D_RWKV))],
        out_specs=[spec] * 10,
        out_shape=[out] * 10,
        compiler_params=_params(("arbitrary",)),
        name="rwkv_chunk_prep",
    )(ps, w0, a0, wlora, wg, k_k, k_a, r_k)


def _rwkv_kernel(w_ref, rh_ref, u0_ref, y0_ref, bh_ref, kh_ref, v_ref, pl_ref,
                 y_ref, state_ref):
    @pl.when(pl.program_id(0) == 0)
    def _():
        state_ref[...] = jnp.zeros_like(state_ref)

    nb, tt, _ = w_ref.shape
    L = RWKV_CHUNK
    depth = LANES * pl.cdiv(6 * L + SUBLANES, LANES)
    row_head = lax.broadcasted_iota(jnp.int32, (LANES, LANES), 0) // HEAD
    col_head = lax.broadcasted_iota(jnp.int32, (LANES, LANES), 1) // HEAD
    same_head = row_head == col_head
    pad = jnp.zeros((depth - 6 * L - SUBLANES, LANES), F32)
    zeros_tail = jnp.zeros((depth - 6 * L, LANES), BF16)

    def split(x):
        hi = x.astype(BF16)
        return hi, (x - hi.astype(F32)).astype(BF16)

    def chunk(ci, carry):
        t0 = pl.multiple_of(ci * L, L)
        streams = [(bi, p) for bi in range(nb) for p in range(N_RWKV_PAIRS)]

        def rows(ref, n=L):
            return [ref[bi, pl.ds(t0, n), LANES * p:LANES * (p + 1)] for bi, p in streams]

        each = range(len(streams))
        s = [state_ref[bi * N_RWKV_PAIRS + p] for bi, p in streams]
        w, rh, u0, y0, bh, kh, v = (rows(ref) for ref in
                                   (w_ref, rh_ref, u0_ref, y0_ref, bh_ref, kh_ref, v_ref))
        p_rows = rows(pl_ref, SUBLANES)
        left, p_col = [], []
        for c in each:
            (b_hi, b_lo), (k_hi, k_lo) = split(bh[c]), split(kh[c])
            block = jnp.concatenate([x.astype(F32) for x in (b_hi, b_lo, b_hi, k_hi, k_lo, k_hi)]
                                    + [p_rows[c], pad], axis=0).T
            left.append(block.astype(BF16))
            p_col.append(jnp.broadcast_to(block[:, 6 * L:6 * L + 1], (LANES, LANES)))
        g = []
        for c in each:
            (l_hi, l_lo), (s_hi, s_lo) = split(jnp.concatenate([w[c], rh[c]], axis=0)), split(s[c])
            g.append(jnp.dot(jnp.concatenate([l_hi, l_lo], axis=1),
                             jnp.concatenate([s_hi, s_hi], axis=0), preferred_element_type=F32)
                     + jnp.dot(l_hi, s_lo, preferred_element_type=F32))
        u = [g[c][:L] + u0[c] for c in each]
        upd = []
        for c in each:
            (u_hi, u_lo), (v_hi, v_lo) = split(u[c]), split(v[c])
            right = jnp.concatenate([u_hi, u_hi, u_lo, v_hi, v_hi, v_lo, zeros_tail], axis=0)
            upd.append(jnp.dot(left[c], right, preferred_element_type=F32))
        for c, (bi, p) in enumerate(streams):
            y_ref[bi, pl.ds(t0, L), LANES * p:LANES * (p + 1)] = g[c][L:] + y0[c]
            state_ref[bi * N_RWKV_PAIRS + p] = (p_col[c] * s[c]
                                                + jnp.where(same_head, upd[c], 0.0))
        return carry

    lax.fori_loop(0, tt // L, chunk, 0)


def _rwkv(w, rh, u0, y0, bh, kh, v, p_l, tt):
    nb, s, d = w.shape
    spec = pl.BlockSpec((nb, tt, d), lambda i: (0, i, 0))
    return pl.pallas_call(
        _rwkv_kernel,
        grid=(s // tt,),
        in_specs=[spec] * 8,
        out_specs=spec,
        out_shape=jax.ShapeDtypeStruct((nb, s, d), F32),
        scratch_shapes=[pltpu.VMEM((nb * N_RWKV_PAIRS, LANES, LANES), F32)],
        compiler_params=_params(("arbitrary",)),
        name="rwkv_scan",
    )(w, rh, u0, y0, bh, kh, v, p_l)


def _gelu_tanh(x):
    return 0.5 * x * (1.0 + jnp.tanh(0.7978845608028654 * (x + 0.044715 * (x * x * x))))


def _sgu_apply(pb, g_ref, b_ref, ws_ref, bias_ref, o_ref):
    z = _gelu_tanh(pb)
    u, v = z[:, :D_SGU], z[:, D_SGU:]
    mu = jnp.mean(v, axis=-1, keepdims=True)
    vc = v - mu
    var = jnp.mean(vc * vc, axis=-1, keepdims=True)
    vn = (vc * lax.rsqrt(var + LN_EPS) * g_ref[...] + b_ref[...]).astype(BF16)
    tpos = lax.broadcasted_iota(jnp.int32, (SGU_BLOCK, SGU_BLOCK), 0) // CHUNK
    spos = lax.broadcasted_iota(jnp.int32, (SGU_BLOCK, SGU_BLOCK), 1) // CHUNK
    causal = spos <= tpos
    first_head = lax.broadcasted_iota(jnp.int32, (SGU_BLOCK, LANES), 1) < HEAD
    for p in range(D_SGU // LANES):
        sl = slice(LANES * p, LANES * (p + 1))
        ws0 = jnp.where(causal, ws_ref[2 * p], 0.0).astype(BF16)
        ws1 = jnp.where(causal, ws_ref[2 * p + 1], 0.0).astype(BF16)
        for n in range(pb.shape[0] // SGU_BLOCK):
            rows = slice(SGU_BLOCK * n, SGU_BLOCK * (n + 1))
            vb = vn[rows, sl]
            sp = jnp.where(first_head,
                           jnp.dot(ws0, vb, preferred_element_type=F32),
                           jnp.dot(ws1, vb, preferred_element_type=F32))
            o_ref[rows, sl] = (u[rows, sl] * (sp + bias_ref[:, sl])).astype(o_ref.dtype)


def _mid_kernel(y_ref, g_ref, bonus_ref, lng_ref, lnb_ref, yb_ref, x_ref,
                g1_ref, sc2_ref, sh2_ref, g2_ref, n2_ref,
                woa_ref, wob_ref, wrt_ref, w1_ref, w3_ref, w2_ref,
                h2_ref, logit_ref, base_ref):
    ones = _head_ones()
    y = y_ref[...]
    mu = _head_sum(y, ones) * (1.0 / HEAD)
    yc = y - mu
    var = _head_sum(yc * yc, ones) * (1.0 / HEAD)
    ya = yc * lax.rsqrt(var + LN_X_EPS) * lng_ref[...] + lnb_ref[...]
    ya = ((ya + bonus_ref[...]) * g_ref[...]).astype(BF16)
    yo = (jnp.dot(ya, woa_ref[...], preferred_element_type=F32)
          + jnp.dot(yb_ref[...], wob_ref[...], preferred_element_type=F32))
    x1 = x_ref[...] + g1_ref[...] * yo
    h2 = x1 * lax.rsqrt(jnp.mean(x1 * x1, axis=-1, keepdims=True) + RMS_EPS) * n2_ref[...]
    h2 = h2 * (1.0 + sc2_ref[...]) + sh2_ref[...]
    h2b = h2.astype(BF16)
    h2_ref[...] = h2b
    logit_ref[...] = _dot1(wrt_ref[...], h2b, _NT)
    act = (_silu(jnp.dot(h2b, w1_ref[...], preferred_element_type=F32))
           * jnp.dot(h2b, w3_ref[...], preferred_element_type=F32)).astype(BF16)
    ms = jnp.dot(act, w2_ref[...], preferred_element_type=F32)
    base_ref[...] = x1 + g2_ref[...] * ms


def _mid(y, g, bonus, lnx_g, lnx_b, yb, x, g1, sc2, sh2, g2, n2g,
         wo_a, wo_b, wr_t, w1s, w3s, w2s, tm, tiles_per_batch):
    t, d = x.shape
    tok = lambda w: pl.BlockSpec((tm, w), lambda i: (i, 0))
    const = lambda shape: pl.BlockSpec(shape, lambda i: (0,) * len(shape))
    mod = pl.BlockSpec((None, 1, d), lambda i: (i // tiles_per_batch, 0, 0))
    return pl.pallas_call(
        _mid_kernel,
        grid=(t // tm,),
        in_specs=[tok(D_RWKV), tok(D_RWKV), tok(D_RWKV),
                  const((1, D_RWKV)), const((1, D_RWKV)),
                  tok(D_SGU), tok(d), mod, mod, mod, mod, const((1, d)),
                  const(wo_a.shape), const(wo_b.shape), const(wr_t.shape),
                  const(w1s.shape), const(w3s.shape), const(w2s.shape)],
        out_specs=[tok(d), pl.BlockSpec((LANES, tm), lambda i: (0, i)), tok(d)],
        out_shape=[jax.ShapeDtypeStruct((t, d), BF16),
                   jax.ShapeDtypeStruct((LANES, t), F32),
                   jax.ShapeDtypeStruct((t, d), F32)],
        compiler_params=_params(("arbitrary",)),
        name="outproj_norm_router_shared",
    )(y, g, bonus, lnx_g, lnx_b, yb, x, g1, sc2, sh2, g2, n2g,
      wo_a, wo_b, wr_t, w1s, w3s, w2s)


MOE_SUB = 128
MOE_CAP = 32
MOE_GROUP = 8


def _route_kernel(logit_ref, bias_ref, gate_ref, pos_ref, fexp_ref, fsub_ref, fdense_ref):
    tt = logit_ref.shape[1]
    scores = _sigmoid(logit_ref[0:N_EXPERTS, :])
    biased = scores + bias_ref[...]
    neg_inf = -jnp.inf

    group_blocks, group_scores = [], []
    for gi in range(N_GROUPS):
        blk = biased[GROUP_SIZE * gi:GROUP_SIZE * (gi + 1), :]
        m1 = jnp.max(blk, axis=0, keepdims=True)
        is_max = blk == m1
        n_max = jnp.sum(jnp.where(is_max, 1.0, 0.0), axis=0, keepdims=True)
        m2 = jnp.max(jnp.where(is_max, neg_inf, blk), axis=0, keepdims=True)
        group_blocks.append(blk)
        group_scores.append(m1 + jnp.where(n_max >= 2.0, m1, m2))

    masked = []
    for gi in range(N_GROUPS):
        rank = jnp.zeros((1, tt), F32)
        for gj in range(N_GROUPS):
            if gj == gi:
                continue
            ahead = (group_scores[gj] >= group_scores[gi]) if gj < gi else \
                    (group_scores[gj] > group_scores[gi])
            rank += jnp.where(ahead, 1.0, 0.0)
        masked.append(jnp.where(rank < float(TOPK_GROUPS), group_blocks[gi], neg_inf))
    sub = lax.broadcasted_iota(jnp.int32, (GROUP_SIZE, tt), 0).astype(F32)
    index = [sub + float(GROUP_SIZE * gi) for gi in range(N_GROUPS)]
    picked = [jnp.zeros((GROUP_SIZE, tt), F32) for _ in range(N_GROUPS)]

    def over_experts(op, blocks):
        acc = blocks[0]
        for blk in blocks[1:]:
            acc = op(acc, blk)
        return acc

    for _ in range(TOP_K):
        best = jnp.max(over_experts(jnp.maximum, masked), axis=0, keepdims=True)
        first = jnp.min(over_experts(jnp.minimum, [
            jnp.where(masked[gi] == best, index[gi], float(N_EXPERTS))
            for gi in range(N_GROUPS)]), axis=0, keepdims=True)
        for gi in range(N_GROUPS):
            hit = index[gi] == first
            picked[gi] = jnp.where(hit, 1.0, picked[gi])
            masked[gi] = jnp.where(hit, neg_inf, masked[gi])
    chosen = jnp.concatenate(picked, axis=0)
    sw = chosen * scores
    gate_ref[...] = sw / jnp.sum(sw, axis=0, keepdims=True) * ROUTED_SCALE

    before = (lax.broadcasted_iota(jnp.int32, (MOE_SUB, MOE_SUB), 0)
              < lax.broadcasted_iota(jnp.int32, (MOE_SUB, MOE_SUB), 1))
    prefix_ones = jnp.where(before, 1.0, 0.0).astype(BF16)
    all_ones = jnp.ones((MOE_SUB, MOE_SUB), BF16)
    lane = lax.broadcasted_iota(jnp.int32, (N_EXPERTS, LANES), 1)
    positions, counts = [], jnp.zeros((N_EXPERTS, LANES), F32)
    for s in range(tt // MOE_SUB):
        blk = chosen[:, MOE_SUB * s:MOE_SUB * (s + 1)].astype(BF16)
        prefix = jnp.dot(blk, prefix_ones, preferred_element_type=F32)
        positions.append(jnp.where(blk > 0, prefix, -1.0))
        total = jnp.dot(blk, all_ones, preferred_element_type=F32)
        counts = counts + jnp.where(lane == s, total, 0.0)
    pos_ref[...] = jnp.concatenate(positions, axis=1)
    per_expert = jnp.max(counts, axis=1, keepdims=True)
    fexp_ref[...] = jnp.broadcast_to(jnp.where(per_expert > float(MOE_CAP), 1, 0),
                                     fexp_ref.shape).astype(jnp.int32)
    per_group = jnp.concatenate(
        [jnp.max(counts[MOE_GROUP * g:MOE_GROUP * (g + 1), :], axis=0, keepdims=True)
         for g in range(N_EXPERTS // MOE_GROUP)], axis=0)
    fsub_ref[...] = jnp.where(per_group > float(MOE_CAP), 1, 0).astype(jnp.int32)
    group_worst = jnp.max(per_group, axis=1, keepdims=True)
    fdense_ref[...] = jnp.broadcast_to(jnp.where(group_worst > float(2 * MOE_CAP), 1, 0),
                                       fdense_ref.shape).astype(jnp.int32)


def _route(logits_t, bias_tile, tt):
    t = logits_t.shape[1]
    n_tiles = t // tt
    n_groups = N_EXPERTS // MOE_GROUP
    tok = pl.BlockSpec((N_EXPERTS, tt), lambda i: (0, i))
    return pl.pallas_call(
        _route_kernel,
        grid=(n_tiles,),
        in_specs=[pl.BlockSpec((LANES, tt), lambda i: (0, i)),
                  pl.BlockSpec((N_EXPERTS, tt), lambda i: (0, 0))],
        out_specs=[tok, tok,
                   pl.BlockSpec((N_EXPERTS, LANES), lambda i: (i, 0)),
                   pl.BlockSpec((n_groups, LANES), lambda i: (i, 0)),
                   pl.BlockSpec((n_groups, LANES), lambda i: (i, 0))],
        out_shape=[jax.ShapeDtypeStruct((N_EXPERTS, t), F32),
                   jax.ShapeDtypeStruct((N_EXPERTS, t), F32),
                   jax.ShapeDtypeStruct((n_tiles * N_EXPERTS, LANES), jnp.int32),
                   jax.ShapeDtypeStruct((n_tiles * n_groups, LANES), jnp.int32),
                   jax.ShapeDtypeStruct((n_tiles * n_groups, LANES), jnp.int32)],
        compiler_params=_params(("arbitrary",)),
        name="route_topk",
    )(logits_t, bias_tile)


def _moe_kernel(fexp_ref, fsub_ref, fdense_ref, h2_ref, gate_ref, pos_ref,
                w1_ref, w3_ref, w2_ref, o_ref, xy_ref, gb_ref, pt_ref):
    tile, eg = pl.program_id(0), pl.program_id(1)
    n_groups = pl.num_programs(1)
    tm = h2_ref.shape[0]
    n_sub = tm // MOE_SUB

    @pl.when(eg == 0)
    def _():
        o_ref[...] = jnp.zeros_like(o_ref)

    @pl.when((tile == 0) & (eg == 0))
    def _():
        xy_ref[...] = jnp.zeros_like(xy_ref)
        gb_ref[...] = jnp.zeros_like(gb_ref)

    def lane_broadcast(m):
        return jnp.broadcast_to(jnp.sum(m, axis=-1, keepdims=True), (m.shape[0], D_EXPERT))

    def hidden(x, gate_rows, j):
        return (_silu(jnp.dot(x, w1_ref[j], preferred_element_type=F32))
                * jnp.dot(x, w3_ref[j], preferred_element_type=F32) * gate_rows).astype(BF16)

    def pipelined(n, first, second):
        pending = first(0)
        for i in range(1, n):
            nxt = first(i)
            second(i - 1, pending)
            pending = nxt
        second(n - 1, pending)

    cap = MOE_CAP
    slab_rows = n_sub * cap
    go_dense = fdense_ref[tile * n_groups + eg] != 0

    def needs_second(s):
        return fsub_ref[(tile * n_groups + eg) * n_sub + s] != 0

    def slab_slice(j, q, s=None):
        start = (2 * j + q) * slab_rows
        if s is None:
            return slice(start, start + slab_rows)
        return slice(start + s * cap, start + (s + 1) * cap)

    def gather(s, q):
        cols = slice(MOE_SUB * s, MOE_SUB * (s + 1))
        slot = lax.broadcasted_iota(jnp.int32, (cap, MOE_SUB), 0).astype(F32) + float(q * cap)
        pos, gate = pos_ref[:, cols], gate_ref[:, cols]
        hot = [jnp.where(slot == pos[j:j + 1, :], 1.0, 0.0) for j in range(MOE_GROUP)]
        p = jnp.concatenate(hot, axis=0)
        pg = jnp.concatenate([hot[j] * gate[j:j + 1, :] for j in range(MOE_GROUP)], axis=0)
        xg = jnp.dot(p.astype(BF16), h2_ref[cols, :],
                     preferred_element_type=F32).astype(BF16)
        gb = lane_broadcast(pg)
        for j in range(MOE_GROUP):
            xy_ref[slab_slice(j, q, s), :] = xg[j * cap:(j + 1) * cap]
            gb_ref[slab_slice(j, q, s), :] = gb[j * cap:(j + 1) * cap]
        pt_ref[s, :, q * MOE_GROUP * cap:(q + 1) * MOE_GROUP * cap] = p.T.astype(BF16)

    def scatter(s, q):
        y = jnp.concatenate([xy_ref[slab_slice(j, q, s), :] for j in range(MOE_GROUP)], axis=0)
        o_ref[MOE_SUB * s:MOE_SUB * (s + 1), :] += jnp.dot(
            pt_ref[s, :, q * MOE_GROUP * cap:(q + 1) * MOE_GROUP * cap], y,
            preferred_element_type=F32)

    def project(j, q, act):
        xy_ref[slab_slice(j, q), :] = jnp.dot(
            act, w2_ref[j], preferred_element_type=F32).astype(BF16)

    @pl.when(jnp.logical_not(go_dense))
    def _():
        for s in range(n_sub):
            gather(s, 0)
        pipelined(MOE_GROUP,
                  lambda j: hidden(xy_ref[slab_slice(j, 0), :], gb_ref[slab_slice(j, 0), :], j),
                  lambda j, act: project(j, 0, act))
        for s in range(n_sub):
            scatter(s, 0)
        for s in range(n_sub):
            pl.when(needs_second(s))(functools.partial(gather, s, 1))
        for j in range(MOE_GROUP):
            @pl.when(fexp_ref[tile * N_EXPERTS + eg * MOE_GROUP + j] != 0)
            def _(j=j):
                project(j, 1, hidden(xy_ref[slab_slice(j, 1), :],
                                     gb_ref[slab_slice(j, 1), :], j))
        for s in range(n_sub):
            pl.when(needs_second(s))(functools.partial(scatter, s, 1))

    @pl.when(go_dense)
    def _():
        eye = (lax.broadcasted_iota(jnp.int32, (MOE_SUB, MOE_SUB), 0)
               == lax.broadcasted_iota(jnp.int32, (MOE_SUB, MOE_SUB), 1))
        for j in range(MOE_GROUP):
            for s in range(n_sub):
                cols = slice(MOE_SUB * s, MOE_SUB * (s + 1))
                gb_ref[cols, :] = lane_broadcast(
                    jnp.where(eye, gate_ref[j:j + 1, cols], 0.0))
            act = hidden(h2_ref[...], gb_ref[0:tm, :], j)
            o_ref[...] += jnp.dot(act, w2_ref[j], preferred_element_type=F32)


def _moe(fexp, fsub, fdense, h2, gates_t, pos_t, w1e, w3e, w2e, tm):
    t, d = h2.shape
    ne = w1e.shape[0]
    assert ne % MOE_GROUP == 0 and tm % MOE_SUB == 0
    n_sub = tm // MOE_SUB
    rows = MOE_GROUP * 2 * n_sub * MOE_CAP
    assert rows >= tm, "gate scratch is reused by the dense path"
    tok = pl.BlockSpec((tm, d), lambda i, e, *_: (i, 0))
    per_expert = pl.BlockSpec((MOE_GROUP, tm), lambda i, e, *_: (e, i))
    return pl.pallas_call(
        _moe_kernel,
        grid_spec=pltpu.PrefetchScalarGridSpec(
            num_scalar_prefetch=3,
            grid=(t // tm, ne // MOE_GROUP),
            in_specs=[tok, per_expert, per_expert,
                      pl.BlockSpec((MOE_GROUP, d, D_EXPERT), lambda i, e, *_: (e, 0, 0)),
                      pl.BlockSpec((MOE_GROUP, d, D_EXPERT), lambda i, e, *_: (e, 0, 0)),
                      pl.BlockSpec((MOE_GROUP, D_EXPERT, d), lambda i, e, *_: (e, 0, 0))],
            out_specs=tok,
            scratch_shapes=[pltpu.VMEM((rows, d), BF16),
                            pltpu.VMEM((rows, D_EXPERT), F32),
                            pltpu.VMEM((n_sub, MOE_SUB, 2 * MOE_GROUP * MOE_CAP), BF16)]),
        out_shape=jax.ShapeDtypeStruct((t, d), F32),
        compiler_params=_params(("arbitrary", "arbitrary")),
        name="moe_experts",
    )(fexp, fsub, fdense, h2, gates_t, pos_t, w1e, w3e, w2e)


def _final_kernel(base_ref, routed_ref, g2_ref, nf_ref, o_ref):
    xo = base_ref[...] + g2_ref[...] * routed_ref[...]
    o_ref[...] = (xo * lax.rsqrt(jnp.mean(xo * xo, axis=-1, keepdims=True) + RMS_EPS)
                  * nf_ref[...])


def _final(base, routed, g2, nf, tm, tiles_per_batch):
    t, d = base.shape
    tok = pl.BlockSpec((tm, d), lambda i: (i, 0))
    return pl.pallas_call(
        _final_kernel,
        grid=(t // tm,),
        in_specs=[tok, tok,
                  pl.BlockSpec((None, 1, d), lambda i: (i // tiles_per_batch, 0, 0)),
                  pl.BlockSpec((1, d), lambda i: (0, 0))],
        out_specs=tok,
        out_shape=jax.ShapeDtypeStruct((t, d), F32),
        compiler_params=_params(("arbitrary",)),
        name="residual_final_norm",
    )(base, routed, g2, nf)


def _tile(n, want):
    t = min(n, want)
    assert n % t == 0, (n, t)
    return t


def kernel(x, c, w_ada, b_ada, norm1_g, w_in, mu_shift, w0, w_decay_up, a0, w_a_up, w_g_up,
           k_k, k_a, r_k, lnx_g, lnx_b, sgu_ln_g, sgu_ln_b, w_spatial, b_spatial, w_out,
           norm2_g, w_router, e_bias, w1_e, w3_e, w2_e, w1_s, w3_s, w2_s, norm_f_g):
    b, s, d = x.shape
    t = b * s
    assert w_ada.shape[0] == 1, "single-layer block"
    assert d == 2 * D_RWKV and s % SGU_BLOCK == 0
    row = lambda vec: vec.reshape(1, -1)

    c_pad = jnp.pad(c, ((0, 8 - b), (0, 0)))
    mod = _mod(c_pad, w_ada[0], b_ada)[:b]
    sh1, sc1, g1, sh2, sc2, g2 = [m[:, None, :] for m in jnp.split(mod, 6, axis=-1)]

    bias_tile = jnp.repeat(b_spatial[0].T, HEAD, axis=1)
    ps, yb = _inproj(x, sc1, sh1, norm1_g, w_in[0].astype(BF16), mu_shift,
                     sgu_ln_g, sgu_ln_b, w_spatial[0], bias_tile, _tile(s, TILE_INPROJ))

    wlora = jnp.zeros((LANES, 2 * D_RWKV), F32)
    wlora = wlora.at[:D_DECAY_LORA, :D_RWKV].set(w_decay_up[0])
    wlora = wlora.at[D_DECAY_LORA:, D_RWKV:].set(w_a_up[0])
    seq = lambda z: z.reshape(b, s, D_RWKV)
    w_c, rh_c, u0_c, y0_c, bh_c, kh_c, pl_c, v, g, bonus = _chunk(
        ps.reshape(t, N_SHIFT), w0, a0, wlora, w_g_up[0], k_k, k_a, row(r_k))
    y = _rwkv(seq(w_c), seq(rh_c), seq(u0_c), seq(y0_c), seq(bh_c), seq(kh_c), seq(v),
              seq(pl_c), _tile(s, TILE_SCAN))

    tm_mid = _tile(s, TILE_MID)
    wr_t = jnp.pad(w_router[0].T, ((0, LANES - N_EXPERTS), (0, 0)))
    wo = w_out[0].astype(BF16)
    h2, logits_t, base = _mid(
        y.reshape(t, D_RWKV), g, bonus, lnx_g, lnx_b, yb.reshape(t, D_SGU), x.reshape(t, d),
        g1, sc2, sh2, g2, norm2_g, wo[:D_RWKV], wo[D_RWKV:], wr_t,
        w1_s[0].astype(BF16), w3_s[0].astype(BF16), w2_s[0].astype(BF16), tm_mid, s // tm_mid)

    tm_moe = _tile(t, TILE_MOE)
    gates_t, pos_t, fexp, fsub, fdense = _route(
        logits_t, jnp.broadcast_to(e_bias[0][:, None], (N_EXPERTS, tm_moe)), tm_moe)

    routed = _moe(fexp[:, 0], fsub[:, :tm_moe // MOE_SUB].reshape(-1), fdense[:, 0],
                  h2, gates_t, pos_t, w1_e[0].astype(BF16), w3_e[0].astype(BF16),
                  w2_e[0].astype(BF16), tm_moe)
    tm_fin = _tile(s, TILE_FINAL)
    out = _final(base, routed, g2, row(norm_f_g), tm_fin, s // tm_fin)
    return out.reshape(b, s, d)
```

```python
import functools

import jax
import jax.numpy as jnp
from jax import lax
from jax.experimental import pallas as pl
from jax.experimental.pallas import tpu as pltpu

F32 = jnp.float32
BF16 = jnp.bfloat16

HEAD = 64
LANES = 128
SUBLANES = 8
D_RWKV = 512
D_SGU = 512
N_RWKV_PAIRS = D_RWKV // LANES
D_DECAY_LORA = 64
D_AAA_LORA = 64
D_GATE_LORA = 128
N_SHIFT = 3 * D_RWKV + D_DECAY_LORA + D_AAA_LORA + D_GATE_LORA
SGU_BLOCK = 128
CHUNK = 64
N_EXPERTS = 64
N_GROUPS = 8
GROUP_SIZE = N_EXPERTS // N_GROUPS
TOPK_GROUPS = 4
TOP_K = 8
D_EXPERT = 256
ROUTED_SCALE = 2.5
RMS_EPS = 1e-6
LN_EPS = 1e-5
LN_X_EPS = 64e-5
VMEM_LIMIT_BYTES = 56 * 1024 * 1024

TILE_MOD_COLS = 1024
TILE_INPROJ = 512
TILE_SCAN = 512
TILE_MID = 512
TILE_MOE = 1024
TILE_FINAL = 512


def _params(semantics):
    return pltpu.CompilerParams(dimension_semantics=semantics,
                                vmem_limit_bytes=VMEM_LIMIT_BYTES)


def _dot3(a, b, dims=(((1,), (0,)), ((), ()))):
    ((ca,), (cb,)), _ = dims
    a_hi = a.astype(BF16)
    a_lo = (a - a_hi.astype(F32)).astype(BF16)
    b_hi = b.astype(BF16)
    b_lo = (b - b_hi.astype(F32)).astype(BF16)
    dg = functools.partial(lax.dot_general, dimension_numbers=dims,
                           preferred_element_type=F32)
    return (dg(jnp.concatenate([a_hi, a_lo], axis=ca), jnp.concatenate([b_hi, b_hi], axis=cb))
            + dg(a_hi, b_lo))


def _head_ones():
    r = lax.broadcasted_iota(jnp.int32, (LANES, LANES), 0) // HEAD
    c = lax.broadcasted_iota(jnp.int32, (LANES, LANES), 1) // HEAD
    return jnp.where(r == c, 1.0, 0.0).astype(BF16)


def _head_sum(z, ones):
    both = jnp.concatenate([ones, ones], axis=0)
    parts = []
    for p in range(z.shape[1] // LANES):
        zp = z[:, LANES * p:LANES * (p + 1)]
        hi = zp.astype(BF16)
        lo = (zp - hi.astype(F32)).astype(BF16)
        parts.append(jnp.dot(jnp.concatenate([hi, lo], axis=1), both,
                             preferred_element_type=F32))
    return jnp.concatenate(parts, axis=-1)


def _sigmoid(x):
    return 1.0 / (1.0 + jnp.exp(-x))


def _silu(x):
    return x * _sigmoid(x)


def _mod_kernel(c_ref, w_ref, b_ref, o_ref):
    s = _silu(c_ref[...])
    o_ref[...] = _dot3(s, w_ref[...]) + b_ref[...]


def _mod(c_pad, w_ada, b_ada):
    rows, d = c_pad.shape
    n = w_ada.shape[1]
    tn = TILE_MOD_COLS
    return pl.pallas_call(
        _mod_kernel,
        grid=(n // tn,),
        in_specs=[pl.BlockSpec((rows, d), lambda j: (0, 0)),
                  pl.BlockSpec((d, tn), lambda j: (0, j)),
                  pl.BlockSpec((1, tn), lambda j: (0, j))],
        out_specs=pl.BlockSpec((rows, tn), lambda j: (0, j)),
        out_shape=jax.ShapeDtypeStruct((rows, n), F32),
        compiler_params=_params(("arbitrary",)),
        name="adaln_mod",
    )(c_pad, w_ada, b_ada)


def _inproj_kernel(x_ref, sc_ref, sh_ref, g_ref, w_ref, mu_ref,
                   sgu_g_ref, sgu_b_ref, ws_ref, sgu_bias_ref, ps_ref, yb_ref, carry_ref):
    @pl.when(pl.program_id(1) == 0)
    def _():
        carry_ref[...] = jnp.zeros_like(carry_ref)

    tm = x_ref.shape[0]
    halves = [slice(0, tm // 2), slice(tm // 2, tm)]
    proj = []
    for rows in halves:
        x = x_ref[rows, :]
        h = x * lax.rsqrt(jnp.mean(x * x, axis=-1, keepdims=True) + RMS_EPS) * g_ref[...]
        h = h * (1.0 + sc_ref[...]) + sh_ref[...]
        proj.append(jnp.dot(h.astype(BF16), w_ref[...], preferred_element_type=F32))
    for rows, p in zip(halves, proj):
        _sgu_apply(p[:, N_SHIFT:], sgu_g_ref, sgu_b_ref, ws_ref, sgu_bias_ref,
                   yb_ref.at[rows, :])
    ps = jnp.concatenate([p[:, :N_SHIFT] for p in proj], axis=0)
    prev = pltpu.roll(ps, 1, 0)
    first = lax.broadcasted_iota(jnp.int32, ps.shape, 0) == 0
    prev = jnp.where(first, carry_ref[0:1, :], prev)
    carry_ref[0:1, :] = ps[tm - 1:tm, :]
    ps_ref[...] = ps + (prev - ps) * mu_ref[...]


def _inproj(x, sc1, sh1, g, w_in_bf16, mu, sgu_g, sgu_b, w_spatial, sgu_bias, tm):
    b, s, d = x.shape
    d_in = w_in_bf16.shape[1]
    assert d_in == N_SHIFT + 2 * D_SGU and tm % (2 * SGU_BLOCK) == 0
    row = lambda bi, i: (bi, 0, 0)
    const = lambda shape: pl.BlockSpec(shape, lambda bi, i: (0,) * len(shape))
    return pl.pallas_call(
        _inproj_kernel,
        grid=(b, s // tm),
        in_specs=[pl.BlockSpec((None, tm, d), lambda bi, i: (bi, i, 0)),
                  pl.BlockSpec((None, 1, d), row),
                  pl.BlockSpec((None, 1, d), row),
                  const((1, d)), const((d, d_in)), const((1, N_SHIFT)),
                  const((1, D_SGU)), const((1, D_SGU)), const(w_spatial.shape),
                  const((SGU_BLOCK, D_SGU))],
        out_specs=[pl.BlockSpec((None, tm, N_SHIFT), lambda bi, i: (bi, i, 0)),
                   pl.BlockSpec((None, tm, D_SGU), lambda bi, i: (bi, i, 0))],
        out_shape=[jax.ShapeDtypeStruct((b, s, N_SHIFT), F32),
                   jax.ShapeDtypeStruct((b, s, D_SGU), BF16)],
        scratch_shapes=[pltpu.VMEM((8, N_SHIFT), F32)],
        compiler_params=_params(("arbitrary", "arbitrary")),
        name="inproj_shift_sgu",
    )(x, sc1, sh1, g, w_in_bf16, mu, sgu_g, sgu_b, w_spatial, sgu_bias)


def _token_terms(ps, w0_ref, a0_ref, wlora_ref, wg_ref, kk_ref, ka_ref, rk_ref):
    c1, c2, c3 = D_RWKV, 2 * D_RWKV, 3 * D_RWKV
    r, k, v = ps[:, :c1], ps[:, c1:c2], ps[:, c2:c3]
    xwa = ps[:, c3:c3 + LANES]
    xg = ps[:, c3 + LANES:]
    is_w = lax.broadcasted_iota(jnp.int32, xwa.shape, 1) < D_DECAY_LORA
    lora = _dot1(jnp.where(is_w, jnp.tanh(xwa), xwa), wlora_ref[...])
    wlog = -jax.nn.softplus(-(w0_ref[...] + lora[:, :c1])) - 0.5
    log_decay = -jnp.exp(wlog)
    a = _sigmoid(a0_ref[...] + lora[:, c1:])
    g = _dot1(_sigmoid(xg), wg_ref[...])
    ones = _head_ones()
    kk = k * kk_ref[...]
    kk = kk / jnp.maximum(jnp.sqrt(_head_sum(kk * kk, ones)), 1e-12)
    kh = k * (1.0 + (a - 1.0) * ka_ref[...])
    bonus = _head_sum(r * kh * rk_ref[...], ones) * v
    return r, log_decay, kh, v, kk, kk * a, g, bonus


RWKV_CHUNK = 64
CHUNK_TILE = 128
CHUNK_GROUP_PAIRS = 4

_NN = (((1,), (0,)), ((), ()))
_NT = (((1,), (1,)), ((), ()))


def _dot1(a, b, dims=_NN):
    return lax.dot_general(a.astype(BF16), b.astype(BF16), dims, preferred_element_type=F32)


def _dot_exact_lhs(lhs_bf16, x):
    hi = x.astype(BF16)
    rest = x - hi.astype(F32)
    mid = rest.astype(BF16)
    lo = (rest - mid.astype(F32)).astype(BF16)
    return jnp.dot(jnp.concatenate([lhs_bf16] * 3, axis=1),
                   jnp.concatenate([hi, mid, lo], axis=0), preferred_element_type=F32)


def _chunk_kernel(ps_ref, w0_ref, a0_ref, wlora_ref, wg_ref, kk_ref, ka_ref, rk_ref,
                  w_o, rh_o, u0_o, y0_o, bh_o, kh_o, pl_o, v_o, g_o, bonus_o):
    r_all, lw_all, k_all, v_all, kk_all, b_all, g_all, bonus_all = _token_terms(
        ps_ref[...], w0_ref, a0_ref, wlora_ref, wg_ref, kk_ref, ka_ref, rk_ref)
    v_o[...] = v_all
    g_o[...] = g_all
    bonus_o[...] = bonus_all
    n = CHUNK_TILE
    tpos = lax.broadcasted_iota(jnp.int32, (n, n), 0)
    spos = lax.broadcasted_iota(jnp.int32, (n, n), 1)
    same = (tpos // RWKV_CHUNK) == (spos // RWKV_CHUNK)
    incl = same & (spos <= tpos)
    strict = same & (spos < tpos)
    tri_ones = jnp.where(incl, 1.0, 0.0).astype(BF16)
    blk_ones = jnp.where(same, 1.0, 0.0).astype(BF16)
    eye = jnp.where(tpos == spos, 1.0, 0.0)
    lane_head = lax.broadcasted_iota(jnp.int32, (n, LANES), 1) // HEAD

    for first in range(0, N_RWKV_PAIRS, CHUNK_GROUP_PAIRS):
        pairs = range(CHUNK_GROUP_PAIRS)
        slabs = [slice(LANES * (first + p), LANES * (first + p + 1)) for p in pairs]
        lw, r, k, v, kk, b = ([x[:, sl] for sl in slabs]
                              for x in (lw_all, r_all, k_all, v_all, kk_all, b_all))
        cum = [_dot_exact_lhs(tri_ones, x) for x in lw]
        tot = [_dot_exact_lhs(blk_ones, x) for x in lw]
        at = [-kk[p] * jnp.exp(cum[p] - lw[p]) for p in pairs]
        rt = [r[p] * jnp.exp(cum[p]) for p in pairs]
        inv = [jnp.exp(-cum[p]) for p in pairs]
        bt = [b[p] * inv[p] for p in pairs]
        kt = [k[p] * inv[p] for p in pairs]

        chains = [(p, h) for p in pairs for h in range(LANES // HEAD)]
        mine = [lane_head == h for h in range(LANES // HEAD)]
        a_h = [jnp.where(mine[h], at[p], 0.0) for p, h in chains]
        r_h = [jnp.where(mine[h], rt[p], 0.0) for p, h in chains]
        v_h = [jnp.where(mine[h], v[p], 0.0) for p, h in chains]
        each = range(len(chains))
        btkt = [jnp.concatenate([bt[p], kt[p]], axis=0) for p in pairs]
        a_bk = [_dot3(a_h[c], btkt[chains[c][0]], _NT) for c in each]
        a_ab = [jnp.where(strict, m[:, :n], 0.0) for m in a_bk]
        a_ak = [jnp.where(strict, m[:, n:], 0.0) for m in a_bk]
        tinv = [eye + a for a in a_ab]
        x = a_ab
        for _ in range(RWKV_CHUNK.bit_length() - 2):
            x = [_dot1(xc, xc) for xc in x]
            tinv = [tc + _dot1(tc, xc) for tc, xc in zip(tinv, x)]
        akv = [_dot3(a_ak[c], v_h[c]) for c in each]
        m_bk = [_dot1(r_h[c], btkt[chains[c][0]], _NT) for c in each]
        m_rb = [jnp.where(incl, m[:, :n], 0.0) for m in m_bk]
        m_rk = [jnp.where(incl, m[:, n:], 0.0) for m in m_bk]
        wu = [_dot3(tinv[c], jnp.concatenate([a_h[c], akv[c]], axis=1)) for c in each]
        w_h = [m[:, :LANES] for m in wu]
        u0_h = [m[:, LANES:] for m in wu]
        ry = [_dot1(m_rb[c], wu[c]) for c in each]
        rh_h = [r_h[c] + ry[c][:, :LANES] for c in each]
        y0_h = [ry[c][:, LANES:] + _dot1(m_rk[c], v_h[c]) for c in each]

        for p in pairs:
            c0, c1 = 2 * p, 2 * p + 1
            rem = jnp.exp(tot[p] - cum[p])
            w_o[:, slabs[p]] = w_h[c0] + w_h[c1]
            rh_o[:, slabs[p]] = rh_h[c0] + rh_h[c1]
            u0_o[:, slabs[p]] = u0_h[c0] + u0_h[c1]
            y0_o[:, slabs[p]] = y0_h[c0] + y0_h[c1]
            bh_o[:, slabs[p]] = b[p] * rem
            kh_o[:, slabs[p]] = k[p] * rem
            pl_o[:, slabs[p]] = jnp.exp(tot[p])


def _chunk(ps, w0, a0, wlora, wg, k_k, k_a, r_k):
    t = ps.shape[0]
    const = lambda shape: pl.BlockSpec(shape, lambda i: (0, 0))
    spec = pl.BlockSpec((CHUNK_TILE, D_RWKV), lambda i: (i, 0))
    out = jax.ShapeDtypeStruct((t, D_RWKV), F32)
    return pl.pallas_call(
        _chunk_kernel,
        grid=(t // CHUNK_TILE,),
        in_specs=[pl.BlockSpec((CHUNK_TILE, N_SHIFT), lambda i: (i, 0)),
                  const((1, D_RWKV)), const((1, D_RWKV)),
                  const((LANES, 2 * D_RWKV)), const((D_GATE_LORA, D_RWKV)),
                  const((1, D_RWKV)), const((1, D_RWKV)), const((1, D_RWKV))],
        out_specs=[spec] * 10,
        out_shape=[out] * 10,
        compiler_params=_params(("arbitrary",)),
        name="rwkv_chunk_prep",
    )(ps, w0, a0, wlora, wg, k_k, k_a, r_k)


def _rwkv_kernel(w_ref, rh_ref, u0_ref, y0_ref, bh_ref, kh_ref, v_ref, pl_ref,
                 y_ref, state_ref):
    @pl.when(pl.program_id(0) == 0)
    def _():
        state_ref[...] = jnp.zeros_like(state_ref)

    nb, tt, _ = w_ref.shape
    L = RWKV_CHUNK
    depth = LANES * pl.cdiv(6 * L + SUBLANES, LANES)
    row_head = lax.broadcasted_iota(jnp.int32, (LANES, LANES), 0) // HEAD
    col_head = lax.broadcasted_iota(jnp.int32, (LANES, LANES), 1) // HEAD
    same_head = row_head == col_head
    pad = jnp.zeros((depth - 6 * L - SUBLANES, LANES), F32)
    zeros_tail = jnp.zeros((depth - 6 * L, LANES), BF16)

    def split(x):
        hi = x.astype(BF16)
        return hi, (x - hi.astype(F32)).astype(BF16)

    def chunk(ci, carry):
        t0 = pl.multiple_of(ci * L, L)
        streams = [(bi, p) for bi in range(nb) for p in range(N_RWKV_PAIRS)]

        def rows(ref, n=L):
            return [ref[bi, pl.ds(t0, n), LANES * p:LANES * (p + 1)] for bi, p in streams]

        each = range(len(streams))
        s = [state_ref[bi * N_RWKV_PAIRS + p] for bi, p in streams]
        w, rh, u0, y0, bh, kh, v = (rows(ref) for ref in
                                   (w_ref, rh_ref, u0_ref, y0_ref, bh_ref, kh_ref, v_ref))
        p_rows = rows(pl_ref, SUBLANES)
        left, p_col = [], []
        for c in each:
            (b_hi, b_lo), (k_hi, k_lo) = split(bh[c]), split(kh[c])
            block = jnp.concatenate([x.astype(F32) for x in (b_hi, b_lo, b_hi, k_hi, k_lo, k_hi)]
                                    + [p_rows[c], pad], axis=0).T
            left.append(block.astype(BF16))
            p_col.append(jnp.broadcast_to(block[:, 6 * L:6 * L + 1], (LANES, LANES)))
        g = []
        for c in each:
            (l_hi, l_lo), (s_hi, s_lo) = split(jnp.concatenate([w[c], rh[c]], axis=0)), split(s[c])
            g.append(jnp.dot(jnp.concatenate([l_hi, l_lo], axis=1),
                             jnp.concatenate([s_hi, s_hi], axis=0), preferred_element_type=F32)
                     + jnp.dot(l_hi, s_lo, preferred_element_type=F32))
        u = [g[c][:L] + u0[c] for c in each]
        upd = []
        for c in each:
            (u_hi, u_lo), (v_hi, v_lo) = split(u[c]), split(v[c])
            right = jnp.concatenate([u_hi, u_hi, u_lo, v_hi, v_hi, v_lo, zeros_tail], axis=0)
            upd.append(jnp.dot(left[c], right, preferred_element_type=F32))
        for c, (bi, p) in enumerate(streams):
            y_ref[bi, pl.ds(t0, L), LANES * p:LANES * (p + 1)] = g[c][L:] + y0[c]
            state_ref[bi * N_RWKV_PAIRS + p] = (p_col[c] * s[c]
                                                + jnp.where(same_head, upd[c], 0.0))
        return carry

    lax.fori_loop(0, tt // L, chunk, 0)


def _rwkv(w, rh, u0, y0, bh, kh, v, p_l, tt):
    nb, s, d = w.shape
    spec = pl.BlockSpec((nb, tt, d), lambda i: (0, i, 0))
    return pl.pallas_call(
        _rwkv_kernel,
        grid=(s // tt,),
        in_specs=[spec] * 8,
        out_specs=spec,
        out_shape=jax.ShapeDtypeStruct((nb, s, d), F32),
        scratch_shapes=[pltpu.VMEM((nb * N_RWKV_PAIRS, LANES, LANES), F32)],
        compiler_params=_params(("arbitrary",)),
        name="rwkv_scan",
    )(w, rh, u0, y0, bh, kh, v, p_l)


def _gelu_tanh(x):
    return 0.5 * x * (1.0 + jnp.tanh(0.7978845608028654 * (x + 0.044715 * (x * x * x))))


def _sgu_apply(pb, g_ref, b_ref, ws_ref, bias_ref, o_ref):
    z = _gelu_tanh(pb)
    u, v = z[:, :D_SGU], z[:, D_SGU:]
    mu = jnp.mean(v, axis=-1, keepdims=True)
    vc = v - mu
    var = jnp.mean(vc * vc, axis=-1, keepdims=True)
    vn = (vc * lax.rsqrt(var + LN_EPS) * g_ref[...] + b_ref[...]).astype(BF16)
    tpos = lax.broadcasted_iota(jnp.int32, (SGU_BLOCK, SGU_BLOCK), 0) // CHUNK
    spos = lax.broadcasted_iota(jnp.int32, (SGU_BLOCK, SGU_BLOCK), 1) // CHUNK
    causal = spos <= tpos
    first_head = lax.broadcasted_iota(jnp.int32, (SGU_BLOCK, LANES), 1) < HEAD
    for p in range(D_SGU // LANES):
        sl = slice(LANES * p, LANES * (p + 1))
        ws0 = jnp.where(causal, ws_ref[2 * p], 0.0).astype(BF16)
        ws1 = jnp.where(causal, ws_ref[2 * p + 1], 0.0).astype(BF16)
        for n in range(pb.shape[0] // SGU_BLOCK):
            rows = slice(SGU_BLOCK * n, SGU_BLOCK * (n + 1))
            vb = vn[rows, sl]
            sp = jnp.where(first_head,
                           jnp.dot(ws0, vb, preferred_element_type=F32),
                           jnp.dot(ws1, vb, preferred_element_type=F32))
            o_ref[rows, sl] = (u[rows, sl] * (sp + bias_ref[:, sl])).astype(o_ref.dtype)


def _mid_kernel(y_ref, g_ref, bonus_ref, lng_ref, lnb_ref, yb_ref, x_ref,
                g1_ref, sc2_ref, sh2_ref, g2_ref, n2_ref,
                woa_ref, wob_ref, wrt_ref, w1_ref, w3_ref, w2_ref,
                h2_ref, logit_ref, base_ref):
    ones = _head_ones()
    y = y_ref[...]
    mu = _head_sum(y, ones) * (1.0 / HEAD)
    yc = y - mu
    var = _head_sum(yc * yc, ones) * (1.0 / HEAD)
    ya = yc * lax.rsqrt(var + LN_X_EPS) * lng_ref[...] + lnb_ref[...]
    ya = ((ya + bonus_ref[...]) * g_ref[...]).astype(BF16)
    yo = (jnp.dot(ya, woa_ref[...], preferred_element_type=F32)
          + jnp.dot(yb_ref[...], wob_ref[...], preferred_element_type=F32))
    x1 = x_ref[...] + g1_ref[...] * yo
    h2 = x1 * lax.rsqrt(jnp.mean(x1 * x1, axis=-1, keepdims=True) + RMS_EPS) * n2_ref[...]
    h2 = h2 * (1.0 + sc2_ref[...]) + sh2_ref[...]
    h2b = h2.astype(BF16)
    h2_ref[...] = h2b
    logit_ref[...] = _dot1(wrt_ref[...], h2b, _NT)
    act = (_silu(jnp.dot(h2b, w1_ref[...], preferred_element_type=F32))
           * jnp.dot(h2b, w3_ref[...], preferred_element_type=F32)).astype(BF16)
    ms = jnp.dot(act, w2_ref[...], preferred_element_type=F32)
    base_ref[...] = x1 + g2_ref[...] * ms


def _mid(y, g, bonus, lnx_g, lnx_b, yb, x, g1, sc2, sh2, g2, n2g,
         wo_a, wo_b, wr_t, w1s, w3s, w2s, tm, tiles_per_batch):
    t, d = x.shape
    tok = lambda w: pl.BlockSpec((tm, w), lambda i: (i, 0))
    const = lambda shape: pl.BlockSpec(shape, lambda i: (0,) * len(shape))
    mod = pl.BlockSpec((None, 1, d), lambda i: (i // tiles_per_batch, 0, 0))
    return pl.pallas_call(
        _mid_kernel,
        grid=(t // tm,),
        in_specs=[tok(D_RWKV), tok(D_RWKV), tok(D_RWKV),
                  const((1, D_RWKV)), const((1, D_RWKV)),
                  tok(D_SGU), tok(d), mod, mod, mod, mod, const((1, d)),
                  const(wo_a.shape), const(wo_b.shape), const(wr_t.shape),
                  const(w1s.shape), const(w3s.shape), const(w2s.shape)],
        out_specs=[tok(d), pl.BlockSpec((LANES, tm), lambda i: (0, i)), tok(d)],
        out_shape=[jax.ShapeDtypeStruct((t, d), BF16),
                   jax.ShapeDtypeStruct((LANES, t), F32),
                   jax.ShapeDtypeStruct((t, d), F32)],
        compiler_params=_params(("arbitrary",)),
        name="outproj_norm_router_shared",
    )(y, g, bonus, lnx_g, lnx_b, yb, x, g1, sc2, sh2, g2, n2g,
      wo_a, wo_b, wr_t, w1s, w3s, w2s)


MOE_SUB = 128
MOE_CAP = 32
MOE_GROUP = 8


def _route_kernel(logit_ref, bias_ref, gate_ref, pos_ref, fexp_ref, fsub_ref, fdense_ref):
    tt = logit_ref.shape[1]
    scores = _sigmoid(logit_ref[0:N_EXPERTS, :])
    biased = scores + bias_ref[...]
    neg_inf = -jnp.inf

    group_blocks, group_scores = [], []
    for gi in range(N_GROUPS):
        blk = biased[GROUP_SIZE * gi:GROUP_SIZE * (gi + 1), :]
        m1 = jnp.max(blk, axis=0, keepdims=True)
        is_max = blk == m1
        n_max = jnp.sum(jnp.where(is_max, 1.0, 0.0), axis=0, keepdims=True)
        m2 = jnp.max(jnp.where(is_max, neg_inf, blk), axis=0, keepdims=True)
        group_blocks.append(blk)
        group_scores.append(m1 + jnp.where(n_max >= 2.0, m1, m2))

    masked = []
    for gi in range(N_GROUPS):
        rank = jnp.zeros((1, tt), F32)
        for gj in range(N_GROUPS):
            if gj == gi:
                continue
            ahead = (group_scores[gj] >= group_scores[gi]) if gj < gi else \
                    (group_scores[gj] > group_scores[gi])
            rank += jnp.where(ahead, 1.0, 0.0)
        masked.append(jnp.where(rank < float(TOPK_GROUPS), group_blocks[gi], neg_inf))
    sub = lax.broadcasted_iota(jnp.int32, (GROUP_SIZE, tt), 0).astype(F32)
    index = [sub + float(GROUP_SIZE * gi) for gi in range(N_GROUPS)]
    picked = [jnp.zeros((GROUP_SIZE, tt), F32) for _ in range(N_GROUPS)]

    def over_experts(op, blocks):
        acc = blocks[0]
        for blk in blocks[1:]:
            acc = op(acc, blk)
        return acc

    for _ in range(TOP_K):
        best = jnp.max(over_experts(jnp.maximum, masked), axis=0, keepdims=True)
        first = jnp.min(over_experts(jnp.minimum, [
            jnp.where(masked[gi] == best, index[gi], float(N_EXPERTS))
            for gi in range(N_GROUPS)]), axis=0, keepdims=True)
        for gi in range(N_GROUPS):
            hit = index[gi] == first
            picked[gi] = jnp.where(hit, 1.0, picked[gi])
            masked[gi] = jnp.where(hit, neg_inf, masked[gi])
    chosen = jnp.concatenate(picked, axis=0)
    sw = chosen * scores
    gate_ref[...] = sw / jnp.sum(sw, axis=0, keepdims=True) * ROUTED_SCALE

    before = (lax.broadcasted_iota(jnp.int32, (MOE_SUB, MOE_SUB), 0)
              < lax.broadcasted_iota(jnp.int32, (MOE_SUB, MOE_SUB), 1))
    prefix_ones = jnp.where(before, 1.0, 0.0).astype(BF16)
    all_ones = jnp.ones((MOE_SUB, MOE_SUB), BF16)
    lane = lax.broadcasted_iota(jnp.int32, (N_EXPERTS, LANES), 1)
    positions, counts = [], jnp.zeros((N_EXPERTS, LANES), F32)
    for s in range(tt // MOE_SUB):
        blk = chosen[:, MOE_SUB * s:MOE_SUB * (s + 1)].astype(BF16)
        prefix = jnp.dot(blk, prefix_ones, preferred_element_type=F32)
        positions.append(jnp.where(blk > 0, prefix, -1.0))
        total = jnp.dot(blk, all_ones, preferred_element_type=F32)
        counts = counts + jnp.where(lane == s, total, 0.0)
    pos_ref[...] = jnp.concatenate(positions, axis=1)
    per_expert = jnp.max(counts, axis=1, keepdims=True)
    fexp_ref[...] = jnp.broadcast_to(jnp.where(per_expert > float(MOE_CAP), 1, 0),
                                     fexp_ref.shape).astype(jnp.int32)
    per_group = jnp.concatenate(
        [jnp.max(counts[MOE_GROUP * g:MOE_GROUP * (g + 1), :], axis=0, keepdims=True)
         for g in range(N_EXPERTS // MOE_GROUP)], axis=0)
    fsub_ref[...] = jnp.where(per_group > float(MOE_CAP), 1, 0).astype(jnp.int32)
    group_worst = jnp.max(per_group, axis=1, keepdims=True)
    fdense_ref[...] = jnp.broadcast_to(jnp.where(group_worst > float(2 * MOE_CAP), 1, 0),
                                       fdense_ref.shape).astype(jnp.int32)


def _route(logits_t, bias_tile, tt):
    t = logits_t.shape[1]
    n_tiles = t // tt
    n_groups = N_EXPERTS // MOE_GROUP
    tok = pl.BlockSpec((N_EXPERTS, tt), lambda i: (0, i))
    return pl.pallas_call(
        _route_kernel,
        grid=(n_tiles,),
        in_specs=[pl.BlockSpec((LANES, tt), lambda i: (0, i)),
                  pl.BlockSpec((N_EXPERTS, tt), lambda i: (0, 0))],
        out_specs=[tok, tok,
                   pl.BlockSpec((N_EXPERTS, LANES), lambda i: (i, 0)),
                   pl.BlockSpec((n_groups, LANES), lambda i: (i, 0)),
                   pl.BlockSpec((n_groups, LANES), lambda i: (i, 0))],
        out_shape=[jax.ShapeDtypeStruct((N_EXPERTS, t), F32),
                   jax.ShapeDtypeStruct((N_EXPERTS, t), F32),
                   jax.ShapeDtypeStruct((n_tiles * N_EXPERTS, LANES), jnp.int32),
                   jax.ShapeDtypeStruct((n_tiles * n_groups, LANES), jnp.int32),
                   jax.ShapeDtypeStruct((n_tiles * n_groups, LANES), jnp.int32)],
        compiler_params=_params(("arbitrary",)),
        name="route_topk",
    )(logits_t, bias_tile)


def _moe_kernel(fexp_ref, fsub_ref, fdense_ref, h2_ref, gate_ref, pos_ref,
                w1_ref, w3_ref, w2_ref, o_ref, xy_ref, gb_ref, pt_ref):
    tile, eg = pl.program_id(0), pl.program_id(1)
    n_groups = pl.num_programs(1)
    tm = h2_ref.shape[0]
    n_sub = tm // MOE_SUB

    @pl.when(eg == 0)
    def _():
        o_ref[...] = jnp.zeros_like(o_ref)

    @pl.when((tile == 0) & (eg == 0))
    def _():
        xy_ref[...] = jnp.zeros_like(xy_ref)
        gb_ref[...] = jnp.zeros_like(gb_ref)

    def lane_broadcast(m):
        return jnp.broadcast_to(jnp.sum(m, axis=-1, keepdims=True), (m.shape[0], D_EXPERT))

    def hidden(x, gate_rows, j):
        return (_silu(jnp.dot(x, w1_ref[j], preferred_element_type=F32))
                * jnp.dot(x, w3_ref[j], preferred_element_type=F32) * gate_rows).astype(BF16)

    def pipelined(n, first, second):
        pending = first(0)
        for i in range(1, n):
            nxt = first(i)
            second(i - 1, pending)
            pending = nxt
        second(n - 1, pending)

    cap = MOE_CAP
    slab_rows = n_sub * cap
    go_dense = fdense_ref[tile * n_groups + eg] != 0

    def needs_second(s):
        return fsub_ref[(tile * n_groups + eg) * n_sub + s] != 0

    def slab_slice(j, q, s=None):
        start = (2 * j + q) * slab_rows
        if s is None:
            return slice(start, start + slab_rows)
        return slice(start + s * cap, start + (s + 1) * cap)

    def gather(s, q):
        cols = slice(MOE_SUB * s, MOE_SUB * (s + 1))
        slot = lax.broadcasted_iota(jnp.int32, (cap, MOE_SUB), 0).astype(F32) + float(q * cap)
        pos, gate = pos_ref[:, cols], gate_ref[:, cols]
        hot = [jnp.where(slot == pos[j:j + 1, :], 1.0, 0.0) for j in range(MOE_GROUP)]
        p = jnp.concatenate(hot, axis=0)
        pg = jnp.concatenate([hot[j] * gate[j:j + 1, :] for j in range(MOE_GROUP)], axis=0)
        xg = jnp.dot(p.astype(BF16), h2_ref[cols, :],
                     preferred_element_type=F32).astype(BF16)
        gb = lane_broadcast(pg)
        for j in range(MOE_GROUP):
            xy_ref[slab_slice(j, q, s), :] = xg[j * cap:(j + 1) * cap]
            gb_ref[slab_slice(j, q, s), :] = gb[j * cap:(j + 1) * cap]
        pt_ref[s, :, q * MOE_GROUP * cap:(q + 1) * MOE_GROUP * cap] = p.T.astype(BF16)

    def scatter(s, q):
        y = jnp.concatenate([xy_ref[slab_slice(j, q, s), :] for j in range(MOE_GROUP)], axis=0)
        o_ref[MOE_SUB * s:MOE_SUB * (s + 1), :] += jnp.dot(
            pt_ref[s, :, q * MOE_GROUP * cap:(q + 1) * MOE_GROUP * cap], y,
            preferred_element_type=F32)

    def project(j, q, act):
        xy_ref[slab_slice(j, q), :] = jnp.dot(
            act, w2_ref[j], preferred_element_type=F32).astype(BF16)

    @pl.when(jnp.logical_not(go_dense))
    def _():
        for s in range(n_sub):
            gather(s, 0)
        pipelined(MOE_GROUP,
                  lambda j: hidden(xy_ref[slab_slice(j, 0), :], gb_ref[slab_slice(j, 0), :], j),
                  lambda j, act: project(j, 0, act))
        for s in range(n_sub):
            scatter(s, 0)
        for s in range(n_sub):
            pl.when(needs_second(s))(functools.partial(gather, s, 1))
        for j in range(MOE_GROUP):
            @pl.when(fexp_ref[tile * N_EXPERTS + eg * MOE_GROUP + j] != 0)
            def _(j=j):
                project(j, 1, hidden(xy_ref[slab_slice(j, 1), :],
                                     gb_ref[slab_slice(j, 1), :], j))
        for s in range(n_sub):
            pl.when(needs_second(s))(functools.partial(scatter, s, 1))

    @pl.when(go_dense)
    def _():
        eye = (lax.broadcasted_iota(jnp.int32, (MOE_SUB, MOE_SUB), 0)
               == lax.broadcasted_iota(jnp.int32, (MOE_SUB, MOE_SUB), 1))
        for j in range(MOE_GROUP):
            for s in range(n_sub):
                cols = slice(MOE_SUB * s, MOE_SUB * (s + 1))
                gb_ref[cols, :] = lane_broadcast(
                    jnp.where(eye, gate_ref[j:j + 1, cols], 0.0))
            act = hidden(h2_ref[...], gb_ref[0:tm, :], j)
            o_ref[...] += jnp.dot(act, w2_ref[j], preferred_element_type=F32)


def _moe(fexp, fsub, fdense, h2, gates_t, pos_t, w1e, w3e, w2e, tm):
    t, d = h2.shape
    ne = w1e.shape[0]
    assert ne % MOE_GROUP == 0 and tm % MOE_SUB == 0
    n_sub = tm // MOE_SUB
    rows = MOE_GROUP * 2 * n_sub * MOE_CAP
    assert rows >= tm, "gate scratch is reused by the dense path"
    tok = pl.BlockSpec((tm, d), lambda i, e, *_: (i, 0))
    per_expert = pl.BlockSpec((MOE_GROUP, tm), lambda i, e, *_: (e, i))
    return pl.pallas_call(
        _moe_kernel,
        grid_spec=pltpu.PrefetchScalarGridSpec(
            num_scalar_prefetch=3,
            grid=(t // tm, ne // MOE_GROUP),
            in_specs=[tok, per_expert, per_expert,
                      pl.BlockSpec((MOE_GROUP, d, D_EXPERT), lambda i, e, *_: (e, 0, 0)),
                      pl.BlockSpec((MOE_GROUP, d, D_EXPERT), lambda i, e, *_: (e, 0, 0)),
                      pl.BlockSpec((MOE_GROUP, D_EXPERT, d), lambda i, e, *_: (e, 0, 0))],
            out_specs=tok,
            scratch_shapes=[pltpu.VMEM((rows, d), BF16),
                            pltpu.VMEM((rows, D_EXPERT), F32),
                            pltpu.VMEM((n_sub, MOE_SUB, 2 * MOE_GROUP * MOE_CAP), BF16)]),
        out_shape=jax.ShapeDtypeStruct((t, d), F32),
        compiler_params=_params(("arbitrary", "arbitrary")),
        name="moe_experts",
    )(fexp, fsub, fdense, h2, gates_t, pos_t, w1e, w3e, w2e)


def _final_kernel(base_ref, routed_ref, g2_ref, nf_ref, o_ref):
    xo = base_ref[...] + g2_ref[...] * routed_ref[...]
    o_ref[...] = (xo * lax.rsqrt(jnp.mean(xo * xo, axis=-1, keepdims=True) + RMS_EPS)
                  * nf_ref[...])


def _final(base, routed, g2, nf, tm, tiles_per_batch):
    t, d = base.shape
    tok = pl.BlockSpec((tm, d), lambda i: (i, 0))
    return pl.pallas_call(
        _final_kernel,
        grid=(t // tm,),
        in_specs=[tok, tok,
                  pl.BlockSpec((None, 1, d), lambda i: (i // tiles_per_batch, 0, 0)),
                  pl.BlockSpec((1, d), lambda i: (0, 0))],
        out_specs=tok,
        out_shape=jax.ShapeDtypeStruct((t, d), F32),
        compiler_params=_params(("arbitrary",)),
        name="residual_final_norm",
    )(base, routed, g2, nf)


def _tile(n, want):
    t = min(n, want)
    assert n % t == 0, (n, t)
    return t


def kernel(x, c, w_ada, b_ada, norm1_g, w_in, mu_shift, w0, w_decay_up, a0, w_a_up, w_g_up,
           k_k, k_a, r_k, lnx_g, lnx_b, sgu_ln_g, sgu_ln_b, w_spatial, b_spatial, w_out,
           norm2_g, w_router, e_bias, w1_e, w3_e, w2_e, w1_s, w3_s, w2_s, norm_f_g):
    b, s, d = x.shape
    t = b * s
    assert w_ada.shape[0] == 1, "single-layer block"
    assert d == 2 * D_RWKV and s % SGU_BLOCK == 0
    row = lambda vec: vec.reshape(1, -1)

    c_pad = jnp.pad(c, ((0, 8 - b), (0, 0)))
    mod = _mod(c_pad, w_ada[0], b_ada)[:b]
    sh1, sc1, g1, sh2, sc2, g2 = [m[:, None, :] for m in jnp.split(mod, 6, axis=-1)]

    bias_tile = jnp.repeat(b_spatial[0].T, HEAD, axis=1)
    ps, yb = _inproj(x, sc1, sh1, norm1_g, w_in[0].astype(BF16), mu_shift,
                     sgu_ln_g, sgu_ln_b, w_spatial[0], bias_tile, _tile(s, TILE_INPROJ))

    wlora = jnp.zeros((LANES, 2 * D_RWKV), F32)
    wlora = wlora.at[:D_DECAY_LORA, :D_RWKV].set(w_decay_up[0])
    wlora = wlora.at[D_DECAY_LORA:, D_RWKV:].set(w_a_up[0])
    seq = lambda z: z.reshape(b, s, D_RWKV)
    w_c, rh_c, u0_c, y0_c, bh_c, kh_c, pl_c, v, g, bonus = _chunk(
        ps.reshape(t, N_SHIFT), w0, a0, wlora, w_g_up[0], k_k, k_a, row(r_k))
    y = _rwkv(seq(w_c), seq(rh_c), seq(u0_c), seq(y0_c), seq(bh_c), seq(kh_c), seq(v),
              seq(pl_c), _tile(s, TILE_SCAN))

    tm_mid = _tile(s, TILE_MID)
    wr_t = jnp.pad(w_router[0].T, ((0, LANES - N_EXPERTS), (0, 0)))
    wo = w_out[0].astype(BF16)
    h2, logits_t, base = _mid(
        y.reshape(t, D_RWKV), g, bonus, lnx_g, lnx_b, yb.reshape(t, D_SGU), x.reshape(t, d),
        g1, sc2, sh2, g2, norm2_g, wo[:D_RWKV], wo[D_RWKV:], wr_t,
        w1_s[0].astype(BF16), w3_s[0].astype(BF16), w2_s[0].astype(BF16), tm_mid, s // tm_mid)

    tm_moe = _tile(t, TILE_MOE)
    gates_t, pos_t, fexp, fsub, fdense = _route(
        logits_t, jnp.broadcast_to(e_bias[0][:, None], (N_EXPERTS, tm_moe)), tm_moe)

    routed = _moe(fexp[:, 0], fsub[:, :tm_moe // MOE_SUB].reshape(-1), fdense[:, 0],
                  h2, gates_t, pos_t, w1_e[0].astype(BF16), w3_e[0].astype(BF16),
                  w2_e[0].astype(BF16), tm_moe)
    tm_fin = _tile(s, TILE_FINAL)
    out = _final(base, routed, g2, row(norm_f_g), tm_fin, s // tm_fin)
    return out.reshape(b, s, d)
```

```python
import functools

import jax
import jax.numpy as jnp
from jax import lax
from jax.experimental import pallas as pl
from jax.experimental.pallas import tpu as pltpu

F32 = jnp.float32
BF16 = jnp.bfloat16

HEAD = 64
LANES = 128
SUBLANES = 8
D_RWKV = 512
D_SGU = 512
N_RWKV_PAIRS = D_RWKV // LANES
D_DECAY_LORA = 64
D_AAA_LORA = 64
D_GATE_LORA = 128
N_SHIFT = 3 * D_RWKV + D_DECAY_LORA + D_AAA_LORA + D_GATE_LORA
SGU_BLOCK = 128
CHUNK = 64
N_EXPERTS = 64
N_GROUPS = 8
GROUP_SIZE = N_EXPERTS // N_GROUPS
TOPK_GROUPS = 4
TOP_K = 8
D_EXPERT = 256
ROUTED_SCALE = 2.5
RMS_EPS = 1e-6
LN_EPS = 1e-5
LN_X_EPS = 64e-5
VMEM_LIMIT_BYTES = 56 * 1024 * 1024

TILE_MOD_COLS = 1024
TILE_INPROJ = 512
TILE_SCAN = 512
TILE_MID = 512
TILE_MOE = 1024
TILE_FINAL = 512


def _params(semantics):
    return pltpu.CompilerParams(dimension_semantics=semantics,
                                vmem_limit_bytes=VMEM_LIMIT_BYTES)


def _dot3(a, b, dims=(((1,), (0,)), ((), ()))):
    ((ca,), (cb,)), _ = dims
    a_hi = a.astype(BF16)
    a_lo = (a - a_hi.astype(F32)).astype(BF16)
    b_hi = b.astype(BF16)
    b_lo = (b - b_hi.astype(F32)).astype(BF16)
    dg = functools.partial(lax.dot_general, dimension_numbers=dims,
                           preferred_element_type=F32)
    return (dg(jnp.concatenate([a_hi, a_lo], axis=ca), jnp.concatenate([b_hi, b_hi], axis=cb))
            + dg(a_hi, b_lo))


def _head_ones():
    r = lax.broadcasted_iota(jnp.int32, (LANES, LANES), 0) // HEAD
    c = lax.broadcasted_iota(jnp.int32, (LANES, LANES), 1) // HEAD
    return jnp.where(r == c, 1.0, 0.0).astype(BF16)


def _head_sum(z, ones):
    both = jnp.concatenate([ones, ones], axis=0)
    parts = []
    for p in range(z.shape[1] // LANES):
        zp = z[:, LANES * p:LANES * (p + 1)]
        hi = zp.astype(BF16)
        lo = (zp - hi.astype(F32)).astype(BF16)
        parts.append(jnp.dot(jnp.concatenate([hi, lo], axis=1), both,
                             preferred_element_type=F32))
    return jnp.concatenate(parts, axis=-1)


def _sigmoid(x):
    return 1.0 / (1.0 + jnp.exp(-x))


def _silu(x):
    return x * _sigmoid(x)


def _mod_kernel(c_ref, w_ref, b_ref, o_ref):
    s = _silu(c_ref[...])
    o_ref[...] = _dot3(s, w_ref[...]) + b_ref[...]


def _mod(c_pad, w_ada, b_ada):
    rows, d = c_pad.shape
    n = w_ada.shape[1]
    tn = TILE_MOD_COLS
    return pl.pallas_call(
        _mod_kernel,
        grid=(n // tn,),
        in_specs=[pl.BlockSpec((rows, d), lambda j: (0, 0)),
                  pl.BlockSpec((d, tn), lambda j: (0, j)),
                  pl.BlockSpec((1, tn), lambda j: (0, j))],
        out_specs=pl.BlockSpec((rows, tn), lambda j: (0, j)),
        out_shape=jax.ShapeDtypeStruct((rows, n), F32),
        compiler_params=_params(("arbitrary",)),
        name="adaln_mod",
    )(c_pad, w_ada, b_ada)


def _inproj_kernel(x_ref, sc_ref, sh_ref, g_ref, w_ref, mu_ref,
                   sgu_g_ref, sgu_b_ref, ws_ref, sgu_bias_ref, ps_ref, yb_ref, carry_ref):
    @pl.when(pl.program_id(1) == 0)
    def _():
        carry_ref[...] = jnp.zeros_like(carry_ref)

    tm = x_ref.shape[0]
    halves = [slice(0, tm // 2), slice(tm // 2, tm)]
    proj = []
    for rows in halves:
        x = x_ref[rows, :]
        h = x * lax.rsqrt(jnp.mean(x * x, axis=-1, keepdims=True) + RMS_EPS) * g_ref[...]
        h = h * (1.0 + sc_ref[...]) + sh_ref[...]
        proj.append(jnp.dot(h.astype(BF16), w_ref[...], preferred_element_type=F32))
    for rows, p in zip(halves, proj):
        _sgu_apply(p[:, N_SHIFT:], sgu_g_ref, sgu_b_ref, ws_ref, sgu_bias_ref,
                   yb_ref.at[rows, :])
    ps = jnp.concatenate([p[:, :N_SHIFT] for p in proj], axis=0)
    prev = pltpu.roll(ps, 1, 0)
    first = lax.broadcasted_iota(jnp.int32, ps.shape, 0) == 0
    prev = jnp.where(first, carry_ref[0:1, :], prev)
    carry_ref[0:1, :] = ps[tm - 1:tm, :]
    ps_ref[...] = ps + (prev - ps) * mu_ref[...]


def _inproj(x, sc1, sh1, g, w_in_bf16, mu, sgu_g, sgu_b, w_spatial, sgu_bias, tm):
    b, s, d = x.shape
    d_in = w_in_bf16.shape[1]
    assert d_in == N_SHIFT + 2 * D_SGU and tm % (2 * SGU_BLOCK) == 0
    row = lambda bi, i: (bi, 0, 0)
    const = lambda shape: pl.BlockSpec(shape, lambda bi, i: (0,) * len(shape))
    return pl.pallas_call(
        _inproj_kernel,
        grid=(b, s // tm),
        in_specs=[pl.BlockSpec((None, tm, d), lambda bi, i: (bi, i, 0)),
                  pl.BlockSpec((None, 1, d), row),
                  pl.BlockSpec((None, 1, d), row),
                  const((1, d)), const((d, d_in)), const((1, N_SHIFT)),
                  const((1, D_SGU)), const((1, D_SGU)), const(w_spatial.shape),
                  const((SGU_BLOCK, D_SGU))],
        out_specs=[pl.BlockSpec((None, tm, N_SHIFT), lambda bi, i: (bi, i, 0)),
                   pl.BlockSpec((None, tm, D_SGU), lambda bi, i: (bi, i, 0))],
        out_shape=[jax.ShapeDtypeStruct((b, s, N_SHIFT), F32),
                   jax.ShapeDtypeStruct((b, s, D_SGU), BF16)],
        scratch_shapes=[pltpu.VMEM((8, N_SHIFT), F32)],
        compiler_params=_params(("arbitrary", "arbitrary")),
        name="inproj_shift_sgu",
    )(x, sc1, sh1, g, w_in_bf16, mu, sgu_g, sgu_b, w_spatial, sgu_bias)


def _token_terms(ps, w0_ref, a0_ref, wlora_ref, wg_ref, kk_ref, ka_ref, rk_ref):
    c1, c2, c3 = D_RWKV, 2 * D_RWKV, 3 * D_RWKV
    r, k, v = ps[:, :c1], ps[:, c1:c2], ps[:, c2:c3]
    xwa = ps[:, c3:c3 + LANES]
    xg = ps[:, c3 + LANES:]
    is_w = lax.broadcasted_iota(jnp.int32, xwa.shape, 1) < D_DECAY_LORA
    lora = _dot1(jnp.where(is_w, jnp.tanh(xwa), xwa), wlora_ref[...])
    wlog = -jax.nn.softplus(-(w0_ref[...] + lora[:, :c1])) - 0.5
    log_decay = -jnp.exp(wlog)
    a = _sigmoid(a0_ref[...] + lora[:, c1:])
    g = _dot1(_sigmoid(xg), wg_ref[...])
    ones = _head_ones()
    kk = k * kk_ref[...]
    kk = kk / jnp.maximum(jnp.sqrt(_head_sum(kk * kk, ones)), 1e-12)
    kh = k * (1.0 + (a - 1.0) * ka_ref[...])
    bonus = _head_sum(r * kh * rk_ref[...], ones) * v
    return r, log_decay, kh, v, kk, kk * a, g, bonus


RWKV_CHUNK = 64
CHUNK_TILE = 128
CHUNK_GROUP_PAIRS = 4

_NN = (((1,), (0,)), ((), ()))
_NT = (((1,), (1,)), ((), ()))


def _dot1(a, b, dims=_NN):
    return lax.dot_general(a.astype(BF16), b.astype(BF16), dims, preferred_element_type=F32)


def _dot_exact_lhs(lhs_bf16, x):
    hi = x.astype(BF16)
    rest = x - hi.astype(F32)
    mid = rest.astype(BF16)
    lo = (rest - mid.astype(F32)).astype(BF16)
    return jnp.dot(jnp.concatenate([lhs_bf16] * 3, axis=1),
                   jnp.concatenate([hi, mid, lo], axis=0), preferred_element_type=F32)


def _chunk_kernel(ps_ref, w0_ref, a0_ref, wlora_ref, wg_ref, kk_ref, ka_ref, rk_ref,
                  w_o, rh_o, u0_o, y0_o, bh_o, kh_o, pl_o, v_o, g_o, bonus_o):
    r_all, lw_all, k_all, v_all, kk_all, b_all, g_all, bonus_all = _token_terms(
        ps_ref[...], w0_ref, a0_ref, wlora_ref, wg_ref, kk_ref, ka_ref, rk_ref)
    v_o[...] = v_all
    g_o[...] = g_all
    bonus_o[...] = bonus_all
    n = CHUNK_TILE
    tpos = lax.broadcasted_iota(jnp.int32, (n, n), 0)
    spos = lax.broadcasted_iota(jnp.int32, (n, n), 1)
    same = (tpos // RWKV_CHUNK) == (spos // RWKV_CHUNK)
    incl = same & (spos <= tpos)
    strict = same & (spos < tpos)
    tri_ones = jnp.where(incl, 1.0, 0.0).astype(BF16)
    blk_ones = jnp.where(same, 1.0, 0.0).astype(BF16)
    eye = jnp.where(tpos == spos, 1.0, 0.0)
    lane_head = lax.broadcasted_iota(jnp.int32, (n, LANES), 1) // HEAD

    for first in range(0, N_RWKV_PAIRS, CHUNK_GROUP_PAIRS):
        pairs = range(CHUNK_GROUP_PAIRS)
        slabs = [slice(LANES * (first + p), LANES * (first + p + 1)) for p in pairs]
        lw, r, k, v, kk, b = ([x[:, sl] for sl in slabs]
                              for x in (lw_all, r_all, k_all, v_all, kk_all, b_all))
        cum = [_dot_exact_lhs(tri_ones, x) for x in lw]
        tot = [_dot_exact_lhs(blk_ones, x) for x in lw]
        at = [-kk[p] * jnp.exp(cum[p] - lw[p]) for p in pairs]
        rt = [r[p] * jnp.exp(cum[p]) for p in pairs]
        inv = [jnp.exp(-cum[p]) for p in pairs]
        bt = [b[p] * inv[p] for p in pairs]
        kt = [k[p] * inv[p] for p in pairs]

        chains = [(p, h) for p in pairs for h in range(LANES // HEAD)]
        mine = [lane_head == h for h in range(LANES // HEAD)]
        a_h = [jnp.where(mine[h], at[p], 0.0) for p, h in chains]
        r_h = [jnp.where(mine[h], rt[p], 0.0) for p, h in chains]
        v_h = [jnp.where(mine[h], v[p], 0.0) for p, h in chains]
        each = range(len(chains))
        btkt = [jnp.concatenate([bt[p], kt[p]], axis=0) for p in pairs]
        a_bk = [_dot3(a_h[c], btkt[chains[c][0]], _NT) for c in each]
        a_ab = [jnp.where(strict, m[:, :n], 0.0) for m in a_bk]
        a_ak = [jnp.where(strict, m[:, n:], 0.0) for m in a_bk]
        tinv = [eye + a for a in a_ab]
        x = a_ab
        for _ in range(RWKV_CHUNK.bit_length() - 2):
            x = [_dot1(xc, xc) for xc in x]
            tinv = [tc + _dot1(tc, xc) for tc, xc in zip(tinv, x)]
        akv = [_dot3(a_ak[c], v_h[c]) for c in each]
        m_bk = [_dot1(r_h[c], btkt[chains[c][0]], _NT) for c in each]
        m_rb = [jnp.where(incl, m[:, :n], 0.0) for m in m_bk]
        m_rk = [jnp.where(incl, m[:, n:], 0.0) for m in m_bk]
        wu = [_dot3(tinv[c], jnp.concatenate([a_h[c], akv[c]], axis=1)) for c in each]
        w_h = [m[:, :LANES] for m in wu]
        u0_h = [m[:, LANES:] for m in wu]
        ry = [_dot1(m_rb[c], wu[c]) for c in each]
        rh_h = [r_h[c] + ry[c][:, :LANES] for c in each]
        y0_h = [ry[c][:, LANES:] + _dot1(m_rk[c], v_h[c]) for c in each]

        for p in pairs:
            c0, c1 = 2 * p, 2 * p + 1
            rem = jnp.exp(tot[p] - cum[p])
            w_o[:, slabs[p]] = w_h[c0] + w_h[c1]
            rh_o[:, slabs[p]] = rh_h[c0] + rh_h[c1]
            u0_o[:, slabs[p]] = u0_h[c0] + u0_h[c1]
            y0_o[:, slabs[p]] = y0_h[c0] + y0_h[c1]
            bh_o[:, slabs[p]] = b[p] * rem
            kh_o[:, slabs[p]] = k[p] * rem
            pl_o[:, slabs[p]] = jnp.exp(tot[p])


def _chunk(ps, w0, a0, wlora, wg, k_k, k_a, r_k):
    t = ps.shape[0]
    const = lambda shape: pl.BlockSpec(shape, lambda i: (0, 0))
    spec = pl.BlockSpec((CHUNK_TILE, D_RWKV), lambda i: (i, 0))
    out = jax.ShapeDtypeStruct((t, D_RWKV), F32)
    return pl.pallas_call(
        _chunk_kernel,
        grid=(t // CHUNK_TILE,),
        in_specs=[pl.BlockSpec((CHUNK_TILE, N_SHIFT), lambda i: (i, 0)),
                  const((1, D_RWKV)), const((1, D_RWKV)),
                  const((LANES, 2 * D_RWKV)), const((D_GATE_LORA, D_RWKV)),
                  const((1, D_RWKV)), const((1, D_RWKV)), const((1, D_RWKV))],
        out_specs=[spec] * 10,
        out_shape=[out] * 10,
        compiler_params=_params(("arbitrary",)),
        name="rwkv_chunk_prep",
    )(ps, w0, a0, wlora, wg, k_k, k_a, r_k)


def _rwkv_kernel(w_ref, rh_ref, u0_ref, y0_ref, bh_ref, kh_ref, v_ref, pl_ref,
                 y_ref, state_ref):
    @pl.when(pl.program_id(0) == 0)
    def _():
        state_ref[...] = jnp.zeros_like(state_ref)

    nb, tt, _ = w_ref.shape
    L = RWKV_CHUNK
    depth = LANES * pl.cdiv(6 * L + SUBLANES, LANES)
    row_head = lax.broadcasted_iota(jnp.int32, (LANES, LANES), 0) // HEAD
    col_head = lax.broadcasted_iota(jnp.int32, (LANES, LANES), 1) // HEAD
    same_head = row_head == col_head
    pad = jnp.zeros((depth - 6 * L - SUBLANES, LANES), F32)
    zeros_tail = jnp.zeros((depth - 6 * L, LANES), BF16)

    def split(x):
        hi = x.astype(BF16)
        return hi, (x - hi.astype(F32)).astype(BF16)

    def chunk(ci, carry):
        t0 = pl.multiple_of(ci * L, L)
        streams = [(bi, p) for bi in range(nb) for p in range(N_RWKV_PAIRS)]

        def rows(ref, n=L):
            return [ref[bi, pl.ds(t0, n), LANES * p:LANES * (p + 1)] for bi, p in streams]

        each = range(len(streams))
        s = [state_ref[bi * N_RWKV_PAIRS + p] for bi, p in streams]
        w, rh, u0, y0, bh, kh, v = (rows(ref) for ref in
                                   (w_ref, rh_ref, u0_ref, y0_ref, bh_ref, kh_ref, v_ref))
        p_rows = rows(pl_ref, SUBLANES)
        left, p_col = [], []
        for c in each:
            (b_hi, b_lo), (k_hi, k_lo) = split(bh[c]), split(kh[c])
            block = jnp.concatenate([x.astype(F32) for x in (b_hi, b_lo, b_hi, k_hi, k_lo, k_hi)]
                                    + [p_rows[c], pad], axis=0).T
            left.append(block.astype(BF16))
            p_col.append(jnp.broadcast_to(block[:, 6 * L:6 * L + 1], (LANES, LANES)))
        g = []
        for c in each:
            (l_hi, l_lo), (s_hi, s_lo) = split(jnp.concatenate([w[c], rh[c]], axis=0)), split(s[c])
            g.append(jnp.dot(jnp.concatenate([l_hi, l_lo], axis=1),
                             jnp.concatenate([s_hi, s_hi], axis=0), preferred_element_type=F32)
                     + jnp.dot(l_hi, s_lo, preferred_element_type=F32))
        u = [g[c][:L] + u0[c] for c in each]
        upd = []
        for c in each:
            (u_hi, u_lo), (v_hi, v_lo) = split(u[c]), split(v[c])
            right = jnp.concatenate([u_hi, u_hi, u_lo, v_hi, v_hi, v_lo, zeros_tail], axis=0)
            upd.append(jnp.dot(left[c], right, preferred_element_type=F32))
        for c, (bi, p) in enumerate(streams):
            y_ref[bi, pl.ds(t0, L), LANES * p:LANES * (p + 1)] = g[c][L:] + y0[c]
            state_ref[bi * N_RWKV_PAIRS + p] = (p_col[c] * s[c]
                                                + jnp.where(same_head, upd[c], 0.0))
        return carry

    lax.fori_loop(0, tt // L, chunk, 0)


def _rwkv(w, rh, u0, y0, bh, kh, v, p_l, tt):
    nb, s, d = w.shape
    spec = pl.BlockSpec((nb, tt, d), lambda i: (0, i, 0))
    return pl.pallas_call(
        _rwkv_kernel,
        grid=(s // tt,),
        in_specs=[spec] * 8,
        out_specs=spec,
        out_shape=jax.ShapeDtypeStruct((nb, s, d), F32),
        scratch_shapes=[pltpu.VMEM((nb * N_RWKV_PAIRS, LANES, LANES), F32)],
        compiler_params=_params(("arbitrary",)),
        name="rwkv_scan",
    )(w, rh, u0, y0, bh, kh, v, p_l)


def _gelu_tanh(x):
    return 0.5 * x * (1.0 + jnp.tanh(0.7978845608028654 * (x + 0.044715 * (x * x * x))))


def _sgu_apply(pb, g_ref, b_ref, ws_ref, bias_ref, o_ref):
    z = _gelu_tanh(pb)
    u, v = z[:, :D_SGU], z[:, D_SGU:]
    mu = jnp.mean(v, axis=-1, keepdims=True)
    vc = v - mu
    var = jnp.mean(vc * vc, axis=-1, keepdims=True)
    vn = (vc * lax.rsqrt(var + LN_EPS) * g_ref[...] + b_ref[...]).astype(BF16)
    tpos = lax.broadcasted_iota(jnp.int32, (SGU_BLOCK, SGU_BLOCK), 0) // CHUNK
    spos = lax.broadcasted_iota(jnp.int32, (SGU_BLOCK, SGU_BLOCK), 1) // CHUNK
    causal = spos <= tpos
    first_head = lax.broadcasted_iota(jnp.int32, (SGU_BLOCK, LANES), 1) < HEAD
    for p in range(D_SGU // LANES):
        sl = slice(LANES * p, LANES * (p + 1))
        ws0 = jnp.where(causal, ws_ref[2 * p], 0.0).astype(BF16)
        ws1 = jnp.where(causal, ws_ref[2 * p + 1], 0.0).astype(BF16)
        for n in range(pb.shape[0] // SGU_BLOCK):
            rows = slice(SGU_BLOCK * n, SGU_BLOCK * (n + 1))
            vb = vn[rows, sl]
            sp = jnp.where(first_head,
                           jnp.dot(ws0, vb, preferred_element_type=F32),
                           jnp.dot(ws1, vb, preferred_element_type=F32))
            o_ref[rows, sl] = (u[rows, sl] * (sp + bias_ref[:, sl])).astype(o_ref.dtype)


def _mid_kernel(y_ref, g_ref, bonus_ref, lng_ref, lnb_ref, yb_ref, x_ref,
                g1_ref, sc2_ref, sh2_ref, g2_ref, n2_ref,
                woa_ref, wob_ref, wrt_ref, w1_ref, w3_ref, w2_ref,
                h2_ref, logit_ref, base_ref):
    ones = _head_ones()
    y = y_ref[...]
    mu = _head_sum(y, ones) * (1.0 / HEAD)
    yc = y - mu
    var = _head_sum(yc * yc, ones) * (1.0 / HEAD)
    ya = yc * lax.rsqrt(var + LN_X_EPS) * lng_ref[...] + lnb_ref[...]
    ya = ((ya + bonus_ref[...]) * g_ref[...]).astype(BF16)
    yo = (jnp.dot(ya, woa_ref[...], preferred_element_type=F32)
          + jnp.dot(yb_ref[...], wob_ref[...], preferred_element_type=F32))
    x1 = x_ref[...] + g1_ref[...] * yo
    h2 = x1 * lax.rsqrt(jnp.mean(x1 * x1, axis=-1, keepdims=True) + RMS_EPS) * n2_ref[...]
    h2 = h2 * (1.0 + sc2_ref[...]) + sh2_ref[...]
    h2b = h2.astype(BF16)
    h2_ref[...] = h2b
    logit_ref[...] = _dot1(wrt_ref[...], h2b, _NT)
    act = (_silu(jnp.dot(h2b, w1_ref[...], preferred_element_type=F32))
           * jnp.dot(h2b, w3_ref[...], preferred_element_type=F32)).astype(BF16)
    ms = jnp.dot(act, w2_ref[...], preferred_element_type=F32)
    base_ref[...] = x1 + g2_ref[...] * ms


def _mid(y, g, bonus, lnx_g, lnx_b, yb, x, g1, sc2, sh2, g2, n2g,
         wo_a, wo_b, wr_t, w1s, w3s, w2s, tm, tiles_per_batch):
    t, d = x.shape
    tok = lambda w: pl.BlockSpec((tm, w), lambda i: (i, 0))
    const = lambda shape: pl.BlockSpec(shape, lambda i: (0,) * len(shape))
    mod = pl.BlockSpec((None, 1, d), lambda i: (i // tiles_per_batch, 0, 0))
    return pl.pallas_call(
        _mid_kernel,
        grid=(t // tm,),
        in_specs=[tok(D_RWKV), tok(D_RWKV), tok(D_RWKV),
                  const((1, D_RWKV)), const((1, D_RWKV)),
                  tok(D_SGU), tok(d), mod, mod, mod, mod, const((1, d)),
                  const(wo_a.shape), const(wo_b.shape), const(wr_t.shape),
                  const(w1s.shape), const(w3s.shape), const(w2s.shape)],
        out_specs=[tok(d), pl.BlockSpec((LANES, tm), lambda i: (0, i)), tok(d)],
        out_shape=[jax.ShapeDtypeStruct((t, d), BF16),
                   jax.ShapeDtypeStruct((LANES, t), F32),
                   jax.ShapeDtypeStruct((t, d), F32)],
        compiler_params=_params(("arbitrary",)),
        name="outproj_norm_router_shared",
    )(y, g, bonus, lnx_g, lnx_b, yb, x, g1, sc2, sh2, g2, n2g,
      wo_a, wo_b, wr_t, w1s, w3s, w2s)


MOE_SUB = 128
MOE_CAP = 32
MOE_GROUP = 8


def _route_kernel(logit_ref, bias_ref, gate_ref, pos_ref, fexp_ref, fpair_ref, fdense_ref):
    tt = logit_ref.shape[1]
    scores = _sigmoid(logit_ref[0:N_EXPERTS, :])
    biased = scores + bias_ref[...]
    neg_inf = -jnp.inf

    group_blocks, group_scores = [], []
    for gi in range(N_GROUPS):
        blk = biased[GROUP_SIZE * gi:GROUP_SIZE * (gi + 1), :]
        m1 = jnp.max(blk, axis=0, keepdims=True)
        is_max = blk == m1
        n_max = jnp.sum(jnp.where(is_max, 1.0, 0.0), axis=0, keepdims=True)
        m2 = jnp.max(jnp.where(is_max, neg_inf, blk), axis=0, keepdims=True)
        group_blocks.append(blk)
        group_scores.append(m1 + jnp.where(n_max >= 2.0, m1, m2))

    masked = []
    for gi in range(N_GROUPS):
        rank = jnp.zeros((1, tt), F32)
        for gj in range(N_GROUPS):
            if gj == gi:
                continue
            ahead = (group_scores[gj] >= group_scores[gi]) if gj < gi else \
                    (group_scores[gj] > group_scores[gi])
            rank += jnp.where(ahead, 1.0, 0.0)
        masked.append(jnp.where(rank < float(TOPK_GROUPS), group_blocks[gi], neg_inf))
    sub = lax.broadcasted_iota(jnp.int32, (GROUP_SIZE, tt), 0).astype(F32)
    index = [sub + float(GROUP_SIZE * gi) for gi in range(N_GROUPS)]
    picked = [jnp.zeros((GROUP_SIZE, tt), F32) for _ in range(N_GROUPS)]

    def over_experts(op, blocks):
        acc = blocks[0]
        for blk in blocks[1:]:
            acc = op(acc, blk)
        return acc

    for _ in range(TOP_K):
        best = jnp.max(over_experts(jnp.maximum, masked), axis=0, keepdims=True)
        first = jnp.min(over_experts(jnp.minimum, [
            jnp.where(masked[gi] == best, index[gi], float(N_EXPERTS))
            for gi in range(N_GROUPS)]), axis=0, keepdims=True)
        for gi in range(N_GROUPS):
            hit = index[gi] == first
            picked[gi] = jnp.where(hit, 1.0, picked[gi])
            masked[gi] = jnp.where(hit, neg_inf, masked[gi])
    chosen = jnp.concatenate(picked, axis=0)
    sw = chosen * scores
    gate_ref[...] = sw / jnp.sum(sw, axis=0, keepdims=True) * ROUTED_SCALE

    before = (lax.broadcasted_iota(jnp.int32, (MOE_SUB, MOE_SUB), 0)
              < lax.broadcasted_iota(jnp.int32, (MOE_SUB, MOE_SUB), 1))
    prefix_ones = jnp.where(before, 1.0, 0.0).astype(BF16)
    all_ones = jnp.ones((MOE_SUB, MOE_SUB), BF16)
    lane = lax.broadcasted_iota(jnp.int32, (N_EXPERTS, LANES), 1)
    positions, counts = [], jnp.zeros((N_EXPERTS, LANES), F32)
    for s in range(tt // MOE_SUB):
        blk = chosen[:, MOE_SUB * s:MOE_SUB * (s + 1)].astype(BF16)
        prefix = jnp.dot(blk, prefix_ones, preferred_element_type=F32)
        positions.append(jnp.where(blk > 0, prefix, -1.0))
        total = jnp.dot(blk, all_ones, preferred_element_type=F32)
        counts = counts + jnp.where(lane == s, total, 0.0)
    pos_ref[...] = jnp.concatenate(positions, axis=1)
    per_expert = jnp.max(counts, axis=1, keepdims=True)
    fexp_ref[...] = jnp.broadcast_to(jnp.where(per_expert > float(MOE_CAP), 1, 0),
                                     fexp_ref.shape).astype(jnp.int32)
    per_group = jnp.concatenate(
        [jnp.max(counts[MOE_GROUP * g:MOE_GROUP * (g + 1), :], axis=0, keepdims=True)
         for g in range(N_EXPERTS // MOE_GROUP)], axis=0)
    fpair_ref[...] = jnp.where(counts > float(MOE_CAP), 1, 0).astype(jnp.int32)
    group_worst = jnp.max(per_group, axis=1, keepdims=True)
    fdense_ref[...] = jnp.broadcast_to(jnp.where(group_worst > float(2 * MOE_CAP), 1, 0),
                                       fdense_ref.shape).astype(jnp.int32)


def _route(logits_t, bias_tile, tt):
    t = logits_t.shape[1]
    n_tiles = t // tt
    n_groups = N_EXPERTS // MOE_GROUP
    tok = pl.BlockSpec((N_EXPERTS, tt), lambda i: (0, i))
    return pl.pallas_call(
        _route_kernel,
        grid=(n_tiles,),
        in_specs=[pl.BlockSpec((LANES, tt), lambda i: (0, i)),
                  pl.BlockSpec((N_EXPERTS, tt), lambda i: (0, 0))],
        out_specs=[tok, tok,
                   pl.BlockSpec((N_EXPERTS, LANES), lambda i: (i, 0)),
                   pl.BlockSpec((N_EXPERTS, LANES), lambda i: (i, 0)),
                   pl.BlockSpec((n_groups, LANES), lambda i: (i, 0))],
        out_shape=[jax.ShapeDtypeStruct((N_EXPERTS, t), F32),
                   jax.ShapeDtypeStruct((N_EXPERTS, t), F32),
                   jax.ShapeDtypeStruct((n_tiles * N_EXPERTS, LANES), jnp.int32),
                   jax.ShapeDtypeStruct((n_tiles * N_EXPERTS, LANES), jnp.int32),
                   jax.ShapeDtypeStruct((n_tiles * n_groups, LANES), jnp.int32)],
        compiler_params=_params(("arbitrary",)),
        name="route_topk",
    )(logits_t, bias_tile)


def _moe_kernel(fexp_ref, fpair_ref, fdense_ref, h2_ref, gate_ref, pos_ref,
                w1_ref, w3_ref, w2_ref, o_ref, xy_ref, gb_ref, pt_ref):
    tile, eg = pl.program_id(0), pl.program_id(1)
    n_groups = pl.num_programs(1)
    tm = h2_ref.shape[0]
    n_sub = tm // MOE_SUB

    @pl.when(eg == 0)
    def _():
        o_ref[...] = jnp.zeros_like(o_ref)

    def lane_broadcast(m):
        return jnp.broadcast_to(jnp.sum(m, axis=-1, keepdims=True), (m.shape[0], D_EXPERT))

    def hidden(x, gate_rows, j):
        return (_silu(jnp.dot(x, w1_ref[j], preferred_element_type=F32))
                * jnp.dot(x, w3_ref[j], preferred_element_type=F32) * gate_rows).astype(BF16)

    def pipelined(n, first, second):
        pending = first(0)
        for i in range(1, n):
            nxt = first(i)
            second(i - 1, pending)
            pending = nxt
        second(n - 1, pending)

    cap = MOE_CAP
    slab_rows = n_sub * cap
    go_dense = fdense_ref[tile * n_groups + eg] != 0

    def overflows(j, s):
        return fpair_ref[(tile * N_EXPERTS + eg * MOE_GROUP + j) * n_sub + s] != 0

    def slab_slice(j, q, s=None):
        start = (2 * j + q) * slab_rows
        if s is None:
            return slice(start, start + slab_rows)
        return slice(start + s * cap, start + (s + 1) * cap)

    def one_hot(j, s, q):
        slot = lax.broadcasted_iota(jnp.int32, (cap, MOE_SUB), 0).astype(F32) + float(q * cap)
        return jnp.where(slot == pos_ref[j:j + 1, MOE_SUB * s:MOE_SUB * (s + 1)], 1.0, 0.0)

    def gather(s):
        cols = slice(MOE_SUB * s, MOE_SUB * (s + 1))
        gate = gate_ref[:, cols]
        hot = [one_hot(j, s, 0) for j in range(MOE_GROUP)]
        p = jnp.concatenate(hot, axis=0)
        pg = jnp.concatenate([hot[j] * gate[j:j + 1, :] for j in range(MOE_GROUP)], axis=0)
        xg = jnp.dot(p.astype(BF16), h2_ref[cols, :],
                     preferred_element_type=F32).astype(BF16)
        gb = lane_broadcast(pg)
        for j in range(MOE_GROUP):
            xy_ref[slab_slice(j, 0, s), :] = xg[j * cap:(j + 1) * cap]
            gb_ref[slab_slice(j, 0, s), :] = gb[j * cap:(j + 1) * cap]
        pt_ref[s] = p.T.astype(BF16)

    def scatter(s):
        y = jnp.concatenate([xy_ref[slab_slice(j, 0, s), :] for j in range(MOE_GROUP)], axis=0)
        o_ref[MOE_SUB * s:MOE_SUB * (s + 1), :] += jnp.dot(
            pt_ref[s], y, preferred_element_type=F32)

    def second_slab(j):
        for s in range(n_sub):
            cols = slice(MOE_SUB * s, MOE_SUB * (s + 1))
            hot = one_hot(j, s, 1)
            xy_ref[slab_slice(j, 1, s), :] = jnp.dot(
                hot.astype(BF16), h2_ref[cols, :], preferred_element_type=F32).astype(BF16)
            gb_ref[slab_slice(j, 1, s), :] = lane_broadcast(hot * gate_ref[j:j + 1, cols])
        project(j, 1, hidden(xy_ref[slab_slice(j, 1), :], gb_ref[slab_slice(j, 1), :], j))
        for s in range(n_sub):
            @pl.when(overflows(j, s))
            def _(s=s):
                o_ref[MOE_SUB * s:MOE_SUB * (s + 1), :] += jnp.dot(
                    one_hot(j, s, 1).T.astype(BF16), xy_ref[slab_slice(j, 1, s), :],
                    preferred_element_type=F32)

    def project(j, q, act):
        xy_ref[slab_slice(j, q), :] = jnp.dot(
            act, w2_ref[j], preferred_element_type=F32).astype(BF16)

    @pl.when(jnp.logical_not(go_dense))
    def _():
        for s in range(n_sub):
            gather(s)
        pipelined(MOE_GROUP,
                  lambda j: hidden(xy_ref[slab_slice(j, 0), :], gb_ref[slab_slice(j, 0), :], j),
                  lambda j, act: project(j, 0, act))
        for s in range(n_sub):
            scatter(s)
        for j in range(MOE_GROUP):
            pl.when(fexp_ref[tile * N_EXPERTS + eg * MOE_GROUP + j] != 0)(
                functools.partial(second_slab, j))

    @pl.when(go_dense)
    def _():
        eye = (lax.broadcasted_iota(jnp.int32, (MOE_SUB, MOE_SUB), 0)
               == lax.broadcasted_iota(jnp.int32, (MOE_SUB, MOE_SUB), 1))
        for j in range(MOE_GROUP):
            for s in range(n_sub):
                cols = slice(MOE_SUB * s, MOE_SUB * (s + 1))
                gb_ref[cols, :] = lane_broadcast(
                    jnp.where(eye, gate_ref[j:j + 1, cols], 0.0))
            act = hidden(h2_ref[...], gb_ref[0:tm, :], j)
            o_ref[...] += jnp.dot(act, w2_ref[j], preferred_element_type=F32)


def _moe(fexp, fpair, fdense, h2, gates_t, pos_t, w1e, w3e, w2e, tm):
    t, d = h2.shape
    ne = w1e.shape[0]
    assert ne % MOE_GROUP == 0 and tm % MOE_SUB == 0
    n_sub = tm // MOE_SUB
    rows = MOE_GROUP * 2 * n_sub * MOE_CAP
    assert rows >= tm, "gate scratch is reused by the dense path"
    tok = pl.BlockSpec((tm, d), lambda i, e, *_: (i, 0))
    per_expert = pl.BlockSpec((MOE_GROUP, tm), lambda i, e, *_: (e, i))
    return pl.pallas_call(
        _moe_kernel,
        grid_spec=pltpu.PrefetchScalarGridSpec(
            num_scalar_prefetch=3,
            grid=(t // tm, ne // MOE_GROUP),
            in_specs=[tok, per_expert, per_expert,
                      pl.BlockSpec((MOE_GROUP, d, D_EXPERT), lambda i, e, *_: (e, 0, 0)),
                      pl.BlockSpec((MOE_GROUP, d, D_EXPERT), lambda i, e, *_: (e, 0, 0)),
                      pl.BlockSpec((MOE_GROUP, D_EXPERT, d), lambda i, e, *_: (e, 0, 0))],
            out_specs=tok,
            scratch_shapes=[pltpu.VMEM((rows, d), BF16),
                            pltpu.VMEM((rows, D_EXPERT), F32),
                            pltpu.VMEM((n_sub, MOE_SUB, MOE_GROUP * MOE_CAP), BF16)]),
        out_shape=jax.ShapeDtypeStruct((t, d), F32),
        compiler_params=_params(("arbitrary", "arbitrary")),
        name="moe_experts",
    )(fexp, fpair, fdense, h2, gates_t, pos_t, w1e, w3e, w2e)


def _final_kernel(base_ref, routed_ref, g2_ref, nf_ref, o_ref):
    xo = base_ref[...] + g2_ref[...] * routed_ref[...]
    o_ref[...] = (xo * lax.rsqrt(jnp.mean(xo * xo, axis=-1, keepdims=True) + RMS_EPS)
                  * nf_ref[...])


def _final(base, routed, g2, nf, tm, tiles_per_batch):
    t, d = base.shape
    tok = pl.BlockSpec((tm, d), lambda i: (i, 0))
    return pl.pallas_call(
        _final_kernel,
        grid=(t // tm,),
        in_specs=[tok, tok,
                  pl.BlockSpec((None, 1, d), lambda i: (i // tiles_per_batch, 0, 0)),
                  pl.BlockSpec((1, d), lambda i: (0, 0))],
        out_specs=tok,
        out_shape=jax.ShapeDtypeStruct((t, d), F32),
        compiler_params=_params(("arbitrary",)),
        name="residual_final_norm",
    )(base, routed, g2, nf)


def _tile(n, want):
    t = min(n, want)
    assert n % t == 0, (n, t)
    return t


def kernel(x, c, w_ada, b_ada, norm1_g, w_in, mu_shift, w0, w_decay_up, a0, w_a_up, w_g_up,
           k_k, k_a, r_k, lnx_g, lnx_b, sgu_ln_g, sgu_ln_b, w_spatial, b_spatial, w_out,
           norm2_g, w_router, e_bias, w1_e, w3_e, w2_e, w1_s, w3_s, w2_s, norm_f_g):
    b, s, d = x.shape
    t = b * s
    assert w_ada.shape[0] == 1, "single-layer block"
    assert d == 2 * D_RWKV and s % SGU_BLOCK == 0
    row = lambda vec: vec.reshape(1, -1)

    c_pad = jnp.pad(c, ((0, 8 - b), (0, 0)))
    mod = _mod(c_pad, w_ada[0], b_ada)[:b]
    sh1, sc1, g1, sh2, sc2, g2 = [m[:, None, :] for m in jnp.split(mod, 6, axis=-1)]

    bias_tile = jnp.repeat(b_spatial[0].T, HEAD, axis=1)
    ps, yb = _inproj(x, sc1, sh1, norm1_g, w_in[0].astype(BF16), mu_shift,
                     sgu_ln_g, sgu_ln_b, w_spatial[0], bias_tile, _tile(s, TILE_INPROJ))

    wlora = jnp.zeros((LANES, 2 * D_RWKV), F32)
    wlora = wlora.at[:D_DECAY_LORA, :D_RWKV].set(w_decay_up[0])
    wlora = wlora.at[D_DECAY_LORA:, D_RWKV:].set(w_a_up[0])
    seq = lambda z: z.reshape(b, s, D_RWKV)
    w_c, rh_c, u0_c, y0_c, bh_c, kh_c, pl_c, v, g, bonus = _chunk(
        ps.reshape(t, N_SHIFT), w0, a0, wlora, w_g_up[0], k_k, k_a, row(r_k))
    y = _rwkv(seq(w_c), seq(rh_c), seq(u0_c), seq(y0_c), seq(bh_c), seq(kh_c), seq(v),
              seq(pl_c), _tile(s, TILE_SCAN))

    tm_mid = _tile(s, TILE_MID)
    wr_t = jnp.pad(w_router[0].T, ((0, LANES - N_EXPERTS), (0, 0)))
    wo = w_out[0].astype(BF16)
    h2, logits_t, base = _mid(
        y.reshape(t, D_RWKV), g, bonus, lnx_g, lnx_b, yb.reshape(t, D_SGU), x.reshape(t, d),
        g1, sc2, sh2, g2, norm2_g, wo[:D_RWKV], wo[D_RWKV:], wr_t,
        w1_s[0].astype(BF16), w3_s[0].astype(BF16), w2_s[0].astype(BF16), tm_mid, s // tm_mid)

    tm_moe = _tile(t, TILE_MOE)
    gates_t, pos_t, fexp, fpair, fdense = _route(
        logits_t, jnp.broadcast_to(e_bias[0][:, None], (N_EXPERTS, tm_moe)), tm_moe)

    routed = _moe(fexp[:, 0], fpair[:, :tm_moe // MOE_SUB].reshape(-1), fdense[:, 0],
                  h2, gates_t, pos_t, w1_e[0].astype(BF16), w3_e[0].astype(BF16),
                  w2_e[0].astype(BF16), tm_moe)
    tm_fin = _tile(s, TILE_FINAL)
    out = _final(base, routed, g2, row(norm_f_g), tm_fin, s // tm_fin)
    return out.reshape(b, s, d)
```

```python
import functools

import jax
import jax.numpy as jnp
from jax import lax
from jax.experimental import pallas as pl
from jax.experimental.pallas import tpu as pltpu

F32 = jnp.float32
BF16 = jnp.bfloat16

HEAD = 64
LANES = 128
SUBLANES = 8
D_RWKV = 512
D_SGU = 512
N_RWKV_PAIRS = D_RWKV // LANES
D_DECAY_LORA = 64
D_AAA_LORA = 64
D_GATE_LORA = 128
N_SHIFT = 3 * D_RWKV + D_DECAY_LORA + D_AAA_LORA + D_GATE_LORA
SGU_BLOCK = 128
CHUNK = 64
N_EXPERTS = 64
N_GROUPS = 8
GROUP_SIZE = N_EXPERTS // N_GROUPS
TOPK_GROUPS = 4
TOP_K = 8
D_EXPERT = 256
ROUTED_SCALE = 2.5
RMS_EPS = 1e-6
LN_EPS = 1e-5
LN_X_EPS = 64e-5
VMEM_LIMIT_BYTES = 56 * 1024 * 1024

TILE_MOD_COLS = 1024
TILE_INPROJ = 512
TILE_SCAN = 512
TILE_MID = 512
TILE_MOE = 1024
TILE_FINAL = 512


def _params(semantics):
    return pltpu.CompilerParams(dimension_semantics=semantics,
                                vmem_limit_bytes=VMEM_LIMIT_BYTES)


def _dot3(a, b, dims=(((1,), (0,)), ((), ()))):
    ((ca,), (cb,)), _ = dims
    a_hi = a.astype(BF16)
    a_lo = (a - a_hi.astype(F32)).astype(BF16)
    b_hi = b.astype(BF16)
    b_lo = (b - b_hi.astype(F32)).astype(BF16)
    dg = functools.partial(lax.dot_general, dimension_numbers=dims,
                           preferred_element_type=F32)
    return (dg(jnp.concatenate([a_hi, a_lo], axis=ca), jnp.concatenate([b_hi, b_hi], axis=cb))
            + dg(a_hi, b_lo))


def _head_ones():
    r = lax.broadcasted_iota(jnp.int32, (LANES, LANES), 0) // HEAD
    c = lax.broadcasted_iota(jnp.int32, (LANES, LANES), 1) // HEAD
    return jnp.where(r == c, 1.0, 0.0).astype(BF16)


def _head_sum(z, ones):
    both = jnp.concatenate([ones, ones], axis=0)
    parts = []
    for p in range(z.shape[1] // LANES):
        zp = z[:, LANES * p:LANES * (p + 1)]
        hi = zp.astype(BF16)
        lo = (zp - hi.astype(F32)).astype(BF16)
        parts.append(jnp.dot(jnp.concatenate([hi, lo], axis=1), both,
                             preferred_element_type=F32))
    return jnp.concatenate(parts, axis=-1)


def _sigmoid(x):
    return 1.0 / (1.0 + jnp.exp(-x))


def _silu(x):
    return x * _sigmoid(x)


def _mod_kernel(c_ref, w_ref, b_ref, o_ref):
    s = _silu(c_ref[...])
    o_ref[...] = _dot3(s, w_ref[...]) + b_ref[...]


def _mod(c_pad, w_ada, b_ada):
    rows, d = c_pad.shape
    n = w_ada.shape[1]
    tn = TILE_MOD_COLS
    return pl.pallas_call(
        _mod_kernel,
        grid=(n // tn,),
        in_specs=[pl.BlockSpec((rows, d), lambda j: (0, 0)),
                  pl.BlockSpec((d, tn), lambda j: (0, j)),
                  pl.BlockSpec((1, tn), lambda j: (0, j))],
        out_specs=pl.BlockSpec((rows, tn), lambda j: (0, j)),
        out_shape=jax.ShapeDtypeStruct((rows, n), F32),
        compiler_params=_params(("arbitrary",)),
        name="adaln_mod",
    )(c_pad, w_ada, b_ada)


def _inproj_kernel(x_ref, sc_ref, sh_ref, g_ref, w_ref, mu_ref,
                   sgu_g_ref, sgu_b_ref, ws_ref, sgu_bias_ref, ps_ref, yb_ref, carry_ref):
    @pl.when(pl.program_id(1) == 0)
    def _():
        carry_ref[...] = jnp.zeros_like(carry_ref)

    tm = x_ref.shape[0]
    halves = [slice(0, tm // 2), slice(tm // 2, tm)]
    proj = []
    for rows in halves:
        x = x_ref[rows, :]
        h = x * lax.rsqrt(jnp.mean(x * x, axis=-1, keepdims=True) + RMS_EPS) * g_ref[...]
        h = h * (1.0 + sc_ref[...]) + sh_ref[...]
        proj.append(jnp.dot(h.astype(BF16), w_ref[...], preferred_element_type=F32))
    for rows, p in zip(halves, proj):
        _sgu_apply(p[:, N_SHIFT:], sgu_g_ref, sgu_b_ref, ws_ref, sgu_bias_ref,
                   yb_ref.at[rows, :])
    ps = jnp.concatenate([p[:, :N_SHIFT] for p in proj], axis=0)
    prev = pltpu.roll(ps, 1, 0)
    first = lax.broadcasted_iota(jnp.int32, ps.shape, 0) == 0
    prev = jnp.where(first, carry_ref[0:1, :], prev)
    carry_ref[0:1, :] = ps[tm - 1:tm, :]
    ps_ref[...] = ps + (prev - ps) * mu_ref[...]


def _inproj(x, sc1, sh1, g, w_in_bf16, mu, sgu_g, sgu_b, w_spatial, sgu_bias, tm):
    b, s, d = x.shape
    d_in = w_in_bf16.shape[1]
    assert d_in == N_SHIFT + 2 * D_SGU and tm % (2 * SGU_BLOCK) == 0
    row = lambda bi, i: (bi, 0, 0)
    const = lambda shape: pl.BlockSpec(shape, lambda bi, i: (0,) * len(shape))
    return pl.pallas_call(
        _inproj_kernel,
        grid=(b, s // tm),
        in_specs=[pl.BlockSpec((None, tm, d), lambda bi, i: (bi, i, 0)),
                  pl.BlockSpec((None, 1, d), row),
                  pl.BlockSpec((None, 1, d), row),
                  const((1, d)), const((d, d_in)), const((1, N_SHIFT)),
                  const((1, D_SGU)), const((1, D_SGU)), const(w_spatial.shape),
                  const((SGU_BLOCK, D_SGU))],
        out_specs=[pl.BlockSpec((None, tm, N_SHIFT), lambda bi, i: (bi, i, 0)),
                   pl.BlockSpec((None, tm, D_SGU), lambda bi, i: (bi, i, 0))],
        out_shape=[jax.ShapeDtypeStruct((b, s, N_SHIFT), F32),
                   jax.ShapeDtypeStruct((b, s, D_SGU), BF16)],
        scratch_shapes=[pltpu.VMEM((8, N_SHIFT), F32)],
        compiler_params=_params(("arbitrary", "arbitrary")),
        name="inproj_shift_sgu",
    )(x, sc1, sh1, g, w_in_bf16, mu, sgu_g, sgu_b, w_spatial, sgu_bias)


def _token_terms(ps, w0_ref, a0_ref, wlora_ref, wg_ref, kk_ref, ka_ref, rk_ref):
    c1, c2, c3 = D_RWKV, 2 * D_RWKV, 3 * D_RWKV
    r, k, v = ps[:, :c1], ps[:, c1:c2], ps[:, c2:c3]
    xwa = ps[:, c3:c3 + LANES]
    xg = ps[:, c3 + LANES:]
    is_w = lax.broadcasted_iota(jnp.int32, xwa.shape, 1) < D_DECAY_LORA
    lora = _dot1(jnp.where(is_w, jnp.tanh(xwa), xwa), wlora_ref[...])
    wlog = -jax.nn.softplus(-(w0_ref[...] + lora[:, :c1])) - 0.5
    log_decay = -jnp.exp(wlog)
    a = _sigmoid(a0_ref[...] + lora[:, c1:])
    g = _dot1(_sigmoid(xg), wg_ref[...])
    ones = _head_ones()
    kk = k * kk_ref[...]
    kk = kk / jnp.maximum(jnp.sqrt(_head_sum(kk * kk, ones)), 1e-12)
    kh = k * (1.0 + (a - 1.0) * ka_ref[...])
    bonus = _head_sum(r * kh * rk_ref[...], ones) * v
    return r, log_decay, kh, v, kk, kk * a, g, bonus


RWKV_CHUNK = 64
CHUNK_TILE = 128
CHUNK_GROUP_PAIRS = 4

_NN = (((1,), (0,)), ((), ()))
_NT = (((1,), (1,)), ((), ()))


def _dot1(a, b, dims=_NN):
    return lax.dot_general(a.astype(BF16), b.astype(BF16), dims, preferred_element_type=F32)


def _dot_exact_lhs(lhs_bf16, x):
    hi = x.astype(BF16)
    rest = x - hi.astype(F32)
    mid = rest.astype(BF16)
    lo = (rest - mid.astype(F32)).astype(BF16)
    return jnp.dot(jnp.concatenate([lhs_bf16] * 3, axis=1),
                   jnp.concatenate([hi, mid, lo], axis=0), preferred_element_type=F32)


def _chunk_kernel(ps_ref, w0_ref, a0_ref, wlora_ref, wg_ref, kk_ref, ka_ref, rk_ref,
                  w_o, rh_o, u0_o, y0_o, bh_o, kh_o, pl_o, v_o, g_o, bonus_o):
    r_all, lw_all, k_all, v_all, kk_all, b_all, g_all, bonus_all = _token_terms(
        ps_ref[...], w0_ref, a0_ref, wlora_ref, wg_ref, kk_ref, ka_ref, rk_ref)
    v_o[...] = v_all
    g_o[...] = g_all
    bonus_o[...] = bonus_all
    n = CHUNK_TILE
    tpos = lax.broadcasted_iota(jnp.int32, (n, n), 0)
    spos = lax.broadcasted_iota(jnp.int32, (n, n), 1)
    same = (tpos // RWKV_CHUNK) == (spos // RWKV_CHUNK)
    incl = same & (spos <= tpos)
    strict = same & (spos < tpos)
    tri_ones = jnp.where(incl, 1.0, 0.0).astype(BF16)
    blk_ones = jnp.where(same, 1.0, 0.0).astype(BF16)
    eye = jnp.where(tpos == spos, 1.0, 0.0)
    lane_head = lax.broadcasted_iota(jnp.int32, (n, LANES), 1) // HEAD

    for first in range(0, N_RWKV_PAIRS, CHUNK_GROUP_PAIRS):
        pairs = range(CHUNK_GROUP_PAIRS)
        slabs = [slice(LANES * (first + p), LANES * (first + p + 1)) for p in pairs]
        lw, r, k, v, kk, b = ([x[:, sl] for sl in slabs]
                              for x in (lw_all, r_all, k_all, v_all, kk_all, b_all))
        cum = [_dot_exact_lhs(tri_ones, x) for x in lw]
        tot = [_dot_exact_lhs(blk_ones, x) for x in lw]
        at = [-kk[p] * jnp.exp(cum[p] - lw[p]) for p in pairs]
        rt = [r[p] * jnp.exp(cum[p]) for p in pairs]
        inv = [jnp.exp(-cum[p]) for p in pairs]
        bt = [b[p] * inv[p] for p in pairs]
        kt = [k[p] * inv[p] for p in pairs]

        chains = [(p, h) for p in pairs for h in range(LANES // HEAD)]
        mine = [lane_head == h for h in range(LANES // HEAD)]
        a_h = [jnp.where(mine[h], at[p], 0.0) for p, h in chains]
        r_h = [jnp.where(mine[h], rt[p], 0.0) for p, h in chains]
        v_h = [jnp.where(mine[h], v[p], 0.0) for p, h in chains]
        each = range(len(chains))
        btkt = [jnp.concatenate([bt[p], kt[p]], axis=0) for p in pairs]
        a_bk = [_dot3(a_h[c], btkt[chains[c][0]], _NT) for c in each]
        a_ab = [jnp.where(strict, m[:, :n], 0.0) for m in a_bk]
        a_ak = [jnp.where(strict, m[:, n:], 0.0) for m in a_bk]
        tinv = [eye + a for a in a_ab]
        x = a_ab
        for _ in range(RWKV_CHUNK.bit_length() - 2):
            x = [_dot1(xc, xc) for xc in x]
            tinv = [tc + _dot1(tc, xc) for tc, xc in zip(tinv, x)]
        akv = [_dot3(a_ak[c], v_h[c]) for c in each]
        m_bk = [_dot1(r_h[c], btkt[chains[c][0]], _NT) for c in each]
        m_rb = [jnp.where(incl, m[:, :n], 0.0) for m in m_bk]
        m_rk = [jnp.where(incl, m[:, n:], 0.0) for m in m_bk]
        wu = [_dot3(tinv[c], jnp.concatenate([a_h[c], akv[c]], axis=1)) for c in each]
        w_h = [m[:, :LANES] for m in wu]
        u0_h = [m[:, LANES:] for m in wu]
        zeros = jnp.zeros((n, LANES), F32)
        ry = [_dot1(jnp.concatenate([m_rb[c], m_rk[c]], axis=1),
                    jnp.concatenate([wu[c], jnp.concatenate([zeros, v_h[c]], axis=1)], axis=0))
              for c in each]
        rh_h = [r_h[c] + ry[c][:, :LANES] for c in each]
        y0_h = [ry[c][:, LANES:] for c in each]

        for p in pairs:
            c0, c1 = 2 * p, 2 * p + 1
            rem = jnp.exp(tot[p] - cum[p])
            w_o[:, slabs[p]] = w_h[c0] + w_h[c1]
            rh_o[:, slabs[p]] = rh_h[c0] + rh_h[c1]
            u0_o[:, slabs[p]] = u0_h[c0] + u0_h[c1]
            y0_o[:, slabs[p]] = y0_h[c0] + y0_h[c1]
            bh_o[:, slabs[p]] = b[p] * rem
            kh_o[:, slabs[p]] = k[p] * rem
            pl_o[:, slabs[p]] = jnp.exp(tot[p])


def _chunk(ps, w0, a0, wlora, wg, k_k, k_a, r_k):
    t = ps.shape[0]
    const = lambda shape: pl.BlockSpec(shape, lambda i: (0, 0))
    spec = pl.BlockSpec((CHUNK_TILE, D_RWKV), lambda i: (i, 0))
    out = jax.ShapeDtypeStruct((t, D_RWKV), F32)
    return pl.pallas_call(
        _chunk_kernel,
        grid=(t // CHUNK_TILE,),
        in_specs=[pl.BlockSpec((CHUNK_TILE, N_SHIFT), lambda i: (i, 0)),
                  const((1, D_RWKV)), const((1, D_RWKV)),
                  const((LANES, 2 * D_RWKV)), const((D_GATE_LORA, D_RWKV)),
                  const((1, D_RWKV)), const((1, D_RWKV)), const((1, D_RWKV))],
        out_specs=[spec] * 10,
        out_shape=[out] * 10,
        compiler_params=_params(("arbitrary",)),
        name="rwkv_chunk_prep",
    )(ps, w0, a0, wlora, wg, k_k, k_a, r_k)


def _rwkv_kernel(w_ref, rh_ref, u0_ref, y0_ref, bh_ref, kh_ref, v_ref, pl_ref,
                 y_ref, state_ref):
    @pl.when(pl.program_id(0) == 0)
    def _():
        state_ref[...] = jnp.zeros_like(state_ref)

    nb, tt, _ = w_ref.shape
    L = RWKV_CHUNK
    depth = LANES * pl.cdiv(6 * L + SUBLANES, LANES)
    row_head = lax.broadcasted_iota(jnp.int32, (LANES, LANES), 0) // HEAD
    col_head = lax.broadcasted_iota(jnp.int32, (LANES, LANES), 1) // HEAD
    same_head = row_head == col_head
    pad = jnp.zeros((depth - 6 * L - SUBLANES, LANES), F32)
    zeros_tail = jnp.zeros((depth - 6 * L, LANES), BF16)

    def split(x):
        hi = x.astype(BF16)
        return hi, (x - hi.astype(F32)).astype(BF16)

    def chunk(ci, carry):
        t0 = pl.multiple_of(ci * L, L)
        streams = [(bi, p) for bi in range(nb) for p in range(N_RWKV_PAIRS)]

        def rows(ref, n=L):
            return [ref[bi, pl.ds(t0, n), LANES * p:LANES * (p + 1)] for bi, p in streams]

        each = range(len(streams))
        s = [state_ref[bi * N_RWKV_PAIRS + p] for bi, p in streams]
        w, rh, u0, y0, bh, kh, v = (rows(ref) for ref in
                                   (w_ref, rh_ref, u0_ref, y0_ref, bh_ref, kh_ref, v_ref))
        p_rows = rows(pl_ref, SUBLANES)
        left, p_col = [], []
        for c in each:
            (b_hi, b_lo), (k_hi, k_lo) = split(bh[c]), split(kh[c])
            block = jnp.concatenate([x.astype(F32) for x in (b_hi, b_lo, b_hi, k_hi, k_lo, k_hi)]
                                    + [p_rows[c], pad], axis=0).T
            left.append(block.astype(BF16))
            p_col.append(jnp.broadcast_to(block[:, 6 * L:6 * L + 1], (LANES, LANES)))
        g = []
        for c in each:
            (l_hi, l_lo), (s_hi, s_lo) = split(jnp.concatenate([w[c], rh[c]], axis=0)), split(s[c])
            g.append(jnp.dot(jnp.concatenate([l_hi, l_lo], axis=1),
                             jnp.concatenate([s_hi, s_hi], axis=0), preferred_element_type=F32)
                     + jnp.dot(l_hi, s_lo, preferred_element_type=F32))
        u = [g[c][:L] + u0[c] for c in each]
        upd = []
        for c in each:
            (u_hi, u_lo), (v_hi, v_lo) = split(u[c]), split(v[c])
            right = jnp.concatenate([u_hi, u_hi, u_lo, v_hi, v_hi, v_lo, zeros_tail], axis=0)
            upd.append(jnp.dot(left[c], right, preferred_element_type=F32))
        for c, (bi, p) in enumerate(streams):
            y_ref[bi, pl.ds(t0, L), LANES * p:LANES * (p + 1)] = g[c][L:] + y0[c]
            state_ref[bi * N_RWKV_PAIRS + p] = (p_col[c] * s[c]
                                                + jnp.where(same_head, upd[c], 0.0))
        return carry

    lax.fori_loop(0, tt // L, chunk, 0)


def _rwkv(w, rh, u0, y0, bh, kh, v, p_l, tt):
    nb, s, d = w.shape
    spec = pl.BlockSpec((nb, tt, d), lambda i: (0, i, 0))
    return pl.pallas_call(
        _rwkv_kernel,
        grid=(s // tt,),
        in_specs=[spec] * 8,
        out_specs=spec,
        out_shape=jax.ShapeDtypeStruct((nb, s, d), F32),
        scratch_shapes=[pltpu.VMEM((nb * N_RWKV_PAIRS, LANES, LANES), F32)],
        compiler_params=_params(("arbitrary",)),
        name="rwkv_scan",
    )(w, rh, u0, y0, bh, kh, v, p_l)


def _gelu_tanh(x):
    return 0.5 * x * (1.0 + jnp.tanh(0.7978845608028654 * (x + 0.044715 * (x * x * x))))


def _sgu_apply(pb, g_ref, b_ref, ws_ref, bias_ref, o_ref):
    z = _gelu_tanh(pb)
    u, v = z[:, :D_SGU], z[:, D_SGU:]
    mu = jnp.mean(v, axis=-1, keepdims=True)
    vc = v - mu
    var = jnp.mean(vc * vc, axis=-1, keepdims=True)
    vn = (vc * lax.rsqrt(var + LN_EPS) * g_ref[...] + b_ref[...]).astype(BF16)
    tpos = lax.broadcasted_iota(jnp.int32, (SGU_BLOCK, SGU_BLOCK), 0) // CHUNK
    spos = lax.broadcasted_iota(jnp.int32, (SGU_BLOCK, SGU_BLOCK), 1) // CHUNK
    causal = spos <= tpos
    first_head = lax.broadcasted_iota(jnp.int32, (SGU_BLOCK, LANES), 1) < HEAD
    for p in range(D_SGU // LANES):
        sl = slice(LANES * p, LANES * (p + 1))
        ws0 = jnp.where(causal, ws_ref[2 * p], 0.0).astype(BF16)
        ws1 = jnp.where(causal, ws_ref[2 * p + 1], 0.0).astype(BF16)
        for n in range(pb.shape[0] // SGU_BLOCK):
            rows = slice(SGU_BLOCK * n, SGU_BLOCK * (n + 1))
            vb = vn[rows, sl]
            sp = jnp.where(first_head,
                           jnp.dot(ws0, vb, preferred_element_type=F32),
                           jnp.dot(ws1, vb, preferred_element_type=F32))
            o_ref[rows, sl] = (u[rows, sl] * (sp + bias_ref[:, sl])).astype(o_ref.dtype)


def _mid_kernel(y_ref, g_ref, bonus_ref, lng_ref, lnb_ref, yb_ref, x_ref,
                g1_ref, sc2_ref, sh2_ref, g2_ref, n2_ref,
                woa_ref, wob_ref, wrt_ref, w1_ref, w3_ref, w2_ref,
                h2_ref, logit_ref, base_ref):
    ones = _head_ones()
    y = y_ref[...]
    mu = _head_sum(y, ones) * (1.0 / HEAD)
    yc = y - mu
    var = _head_sum(yc * yc, ones) * (1.0 / HEAD)
    ya = yc * lax.rsqrt(var + LN_X_EPS) * lng_ref[...] + lnb_ref[...]
    ya = ((ya + bonus_ref[...]) * g_ref[...]).astype(BF16)
    yo = (jnp.dot(ya, woa_ref[...], preferred_element_type=F32)
          + jnp.dot(yb_ref[...], wob_ref[...], preferred_element_type=F32))
    x1 = x_ref[...] + g1_ref[...] * yo
    h2 = x1 * lax.rsqrt(jnp.mean(x1 * x1, axis=-1, keepdims=True) + RMS_EPS) * n2_ref[...]
    h2 = h2 * (1.0 + sc2_ref[...]) + sh2_ref[...]
    h2b = h2.astype(BF16)
    h2_ref[...] = h2b
    logit_ref[...] = _dot1(wrt_ref[...], h2b, _NT)
    act = (_silu(jnp.dot(h2b, w1_ref[...], preferred_element_type=F32))
           * jnp.dot(h2b, w3_ref[...], preferred_element_type=F32)).astype(BF16)
    ms = jnp.dot(act, w2_ref[...], preferred_element_type=F32)
    base_ref[...] = x1 + g2_ref[...] * ms


def _mid(y, g, bonus, lnx_g, lnx_b, yb, x, g1, sc2, sh2, g2, n2g,
         wo_a, wo_b, wr_t, w1s, w3s, w2s, tm, tiles_per_batch):
    t, d = x.shape
    tok = lambda w: pl.BlockSpec((tm, w), lambda i: (i, 0))
    const = lambda shape: pl.BlockSpec(shape, lambda i: (0,) * len(shape))
    mod = pl.BlockSpec((None, 1, d), lambda i: (i // tiles_per_batch, 0, 0))
    return pl.pallas_call(
        _mid_kernel,
        grid=(t // tm,),
        in_specs=[tok(D_RWKV), tok(D_RWKV), tok(D_RWKV),
                  const((1, D_RWKV)), const((1, D_RWKV)),
                  tok(D_SGU), tok(d), mod, mod, mod, mod, const((1, d)),
                  const(wo_a.shape), const(wo_b.shape), const(wr_t.shape),
                  const(w1s.shape), const(w3s.shape), const(w2s.shape)],
        out_specs=[tok(d), pl.BlockSpec((LANES, tm), lambda i: (0, i)), tok(d)],
        out_shape=[jax.ShapeDtypeStruct((t, d), BF16),
                   jax.ShapeDtypeStruct((LANES, t), F32),
                   jax.ShapeDtypeStruct((t, d), F32)],
        compiler_params=_params(("arbitrary",)),
        name="outproj_norm_router_shared",
    )(y, g, bonus, lnx_g, lnx_b, yb, x, g1, sc2, sh2, g2, n2g,
      wo_a, wo_b, wr_t, w1s, w3s, w2s)


MOE_SUB = 128
MOE_CAP = 32
MOE_GROUP = 8


def _route_kernel(logit_ref, bias_ref, gate_ref, pos_ref, fexp_ref, fpair_ref, fdense_ref):
    tt = logit_ref.shape[1]
    scores = _sigmoid(logit_ref[0:N_EXPERTS, :])
    biased = scores + bias_ref[...]
    neg_inf = -jnp.inf

    group_blocks, group_scores = [], []
    for gi in range(N_GROUPS):
        blk = biased[GROUP_SIZE * gi:GROUP_SIZE * (gi + 1), :]
        m1 = jnp.max(blk, axis=0, keepdims=True)
        is_max = blk == m1
        n_max = jnp.sum(jnp.where(is_max, 1.0, 0.0), axis=0, keepdims=True)
        m2 = jnp.max(jnp.where(is_max, neg_inf, blk), axis=0, keepdims=True)
        group_blocks.append(blk)
        group_scores.append(m1 + jnp.where(n_max >= 2.0, m1, m2))

    masked = []
    for gi in range(N_GROUPS):
        rank = jnp.zeros((1, tt), F32)
        for gj in range(N_GROUPS):
            if gj == gi:
                continue
            ahead = (group_scores[gj] >= group_scores[gi]) if gj < gi else \
                    (group_scores[gj] > group_scores[gi])
            rank += jnp.where(ahead, 1.0, 0.0)
        masked.append(jnp.where(rank < float(TOPK_GROUPS), group_blocks[gi], neg_inf))
    sub = lax.broadcasted_iota(jnp.int32, (GROUP_SIZE, tt), 0).astype(F32)
    index = [sub + float(GROUP_SIZE * gi) for gi in range(N_GROUPS)]
    picked = [jnp.zeros((GROUP_SIZE, tt), F32) for _ in range(N_GROUPS)]

    def over_experts(op, blocks):
        acc = blocks[0]
        for blk in blocks[1:]:
            acc = op(acc, blk)
        return acc

    for _ in range(TOP_K):
        best = jnp.max(over_experts(jnp.maximum, masked), axis=0, keepdims=True)
        first = jnp.min(over_experts(jnp.minimum, [
            jnp.where(masked[gi] == best, index[gi], float(N_EXPERTS))
            for gi in range(N_GROUPS)]), axis=0, keepdims=True)
        for gi in range(N_GROUPS):
            hit = index[gi] == first
            picked[gi] = jnp.where(hit, 1.0, picked[gi])
            masked[gi] = jnp.where(hit, neg_inf, masked[gi])
    chosen = jnp.concatenate(picked, axis=0)
    sw = chosen * scores
    gate_ref[...] = sw / jnp.sum(sw, axis=0, keepdims=True) * ROUTED_SCALE

    before = (lax.broadcasted_iota(jnp.int32, (MOE_SUB, MOE_SUB), 0)
              < lax.broadcasted_iota(jnp.int32, (MOE_SUB, MOE_SUB), 1))
    prefix_ones = jnp.where(before, 1.0, 0.0).astype(BF16)
    all_ones = jnp.ones((MOE_SUB, MOE_SUB), BF16)
    lane = lax.broadcasted_iota(jnp.int32, (N_EXPERTS, LANES), 1)
    positions, counts = [], jnp.zeros((N_EXPERTS, LANES), F32)
    for s in range(tt // MOE_SUB):
        blk = chosen[:, MOE_SUB * s:MOE_SUB * (s + 1)].astype(BF16)
        prefix = jnp.dot(blk, prefix_ones, preferred_element_type=F32)
        positions.append(jnp.where(blk > 0, prefix, -1.0))
        total = jnp.dot(blk, all_ones, preferred_element_type=F32)
        counts = counts + jnp.where(lane == s, total, 0.0)
    pos_ref[...] = jnp.concatenate(positions, axis=1)
    per_expert = jnp.max(counts, axis=1, keepdims=True)
    fexp_ref[...] = jnp.broadcast_to(jnp.where(per_expert > float(MOE_CAP), 1, 0),
                                     fexp_ref.shape).astype(jnp.int32)
    per_group = jnp.concatenate(
        [jnp.max(counts[MOE_GROUP * g:MOE_GROUP * (g + 1), :], axis=0, keepdims=True)
         for g in range(N_EXPERTS // MOE_GROUP)], axis=0)
    fpair_ref[...] = jnp.where(counts > float(MOE_CAP), 1, 0).astype(jnp.int32)
    group_worst = jnp.max(per_group, axis=1, keepdims=True)
    fdense_ref[...] = jnp.broadcast_to(jnp.where(group_worst > float(2 * MOE_CAP), 1, 0),
                                       fdense_ref.shape).astype(jnp.int32)


def _route(logits_t, bias_tile, tt):
    t = logits_t.shape[1]
    n_tiles = t // tt
    n_groups = N_EXPERTS // MOE_GROUP
    tok = pl.BlockSpec((N_EXPERTS, tt), lambda i: (0, i))
    return pl.pallas_call(
        _route_kernel,
        grid=(n_tiles,),
        in_specs=[pl.BlockSpec((LANES, tt), lambda i: (0, i)),
                  pl.BlockSpec((N_EXPERTS, tt), lambda i: (0, 0))],
        out_specs=[tok, tok,
                   pl.BlockSpec((N_EXPERTS, LANES), lambda i: (i, 0)),
                   pl.BlockSpec((N_EXPERTS, LANES), lambda i: (i, 0)),
                   pl.BlockSpec((n_groups, LANES), lambda i: (i, 0))],
        out_shape=[jax.ShapeDtypeStruct((N_EXPERTS, t), F32),
                   jax.ShapeDtypeStruct((N_EXPERTS, t), F32),
                   jax.ShapeDtypeStruct((n_tiles * N_EXPERTS, LANES), jnp.int32),
                   jax.ShapeDtypeStruct((n_tiles * N_EXPERTS, LANES), jnp.int32),
                   jax.ShapeDtypeStruct((n_tiles * n_groups, LANES), jnp.int32)],
        compiler_params=_params(("arbitrary",)),
        name="route_topk",
    )(logits_t, bias_tile)


def _moe_kernel(fexp_ref, fpair_ref, fdense_ref, h2_ref, gate_ref, pos_ref,
                w1_ref, w3_ref, w2_ref, o_ref, xy_ref, gb_ref, pt_ref):
    tile, eg = pl.program_id(0), pl.program_id(1)
    n_groups = pl.num_programs(1)
    tm = h2_ref.shape[0]
    n_sub = tm // MOE_SUB

    @pl.when(eg == 0)
    def _():
        o_ref[...] = jnp.zeros_like(o_ref)

    def lane_broadcast(m):
        return jnp.broadcast_to(jnp.sum(m, axis=-1, keepdims=True), (m.shape[0], D_EXPERT))

    def hidden(x, gate_rows, j):
        return (_silu(jnp.dot(x, w1_ref[j], preferred_element_type=F32))
                * jnp.dot(x, w3_ref[j], preferred_element_type=F32) * gate_rows).astype(BF16)

    def pipelined(n, first, second):
        pending = first(0)
        for i in range(1, n):
            nxt = first(i)
            second(i - 1, pending)
            pending = nxt
        second(n - 1, pending)

    cap = MOE_CAP
    slab_rows = n_sub * cap
    go_dense = fdense_ref[tile * n_groups + eg] != 0

    def overflows(j, s):
        return fpair_ref[(tile * N_EXPERTS + eg * MOE_GROUP + j) * n_sub + s] != 0

    def slab_slice(j, q, s=None):
        start = (2 * j + q) * slab_rows
        if s is None:
            return slice(start, start + slab_rows)
        return slice(start + s * cap, start + (s + 1) * cap)

    def one_hot(j, s, q):
        slot = lax.broadcasted_iota(jnp.int32, (cap, MOE_SUB), 0).astype(F32) + float(q * cap)
        return jnp.where(slot == pos_ref[j:j + 1, MOE_SUB * s:MOE_SUB * (s + 1)], 1.0, 0.0)

    def gather(s):
        cols = slice(MOE_SUB * s, MOE_SUB * (s + 1))
        gate = gate_ref[:, cols]
        hot = [one_hot(j, s, 0) for j in range(MOE_GROUP)]
        p = jnp.concatenate(hot, axis=0)
        pg = jnp.concatenate([hot[j] * gate[j:j + 1, :] for j in range(MOE_GROUP)], axis=0)
        xg = jnp.dot(p.astype(BF16), h2_ref[cols, :],
                     preferred_element_type=F32).astype(BF16)
        gb = lane_broadcast(pg)
        for j in range(MOE_GROUP):
            xy_ref[slab_slice(j, 0, s), :] = xg[j * cap:(j + 1) * cap]
            gb_ref[slab_slice(j, 0, s), :] = gb[j * cap:(j + 1) * cap]
        pt_ref[s] = p.T.astype(BF16)

    def scatter(s):
        y = jnp.concatenate([xy_ref[slab_slice(j, 0, s), :] for j in range(MOE_GROUP)], axis=0)
        o_ref[MOE_SUB * s:MOE_SUB * (s + 1), :] += jnp.dot(
            pt_ref[s], y, preferred_element_type=F32)

    def second_slab(j):
        for s in range(n_sub):
            cols = slice(MOE_SUB * s, MOE_SUB * (s + 1))
            hot = one_hot(j, s, 1)
            xy_ref[slab_slice(j, 1, s), :] = jnp.dot(
                hot.astype(BF16), h2_ref[cols, :], preferred_element_type=F32).astype(BF16)
            gb_ref[slab_slice(j, 1, s), :] = lane_broadcast(hot * gate_ref[j:j + 1, cols])
        project(j, 1, hidden(xy_ref[slab_slice(j, 1), :], gb_ref[slab_slice(j, 1), :], j))
        for s in range(n_sub):
            @pl.when(overflows(j, s))
            def _(s=s):
                o_ref[MOE_SUB * s:MOE_SUB * (s + 1), :] += jnp.dot(
                    one_hot(j, s, 1).T.astype(BF16), xy_ref[slab_slice(j, 1, s), :],
                    preferred_element_type=F32)

    def project(j, q, act):
        xy_ref[slab_slice(j, q), :] = jnp.dot(
            act, w2_ref[j], preferred_element_type=F32).astype(BF16)

    @pl.when(jnp.logical_not(go_dense))
    def _():
        for s in range(n_sub):
            gather(s)
        pipelined(MOE_GROUP,
                  lambda j: hidden(xy_ref[slab_slice(j, 0), :], gb_ref[slab_slice(j, 0), :], j),
                  lambda j, act: project(j, 0, act))
        for s in range(n_sub):
            scatter(s)
        for j in range(MOE_GROUP):
            pl.when(fexp_ref[tile * N_EXPERTS + eg * MOE_GROUP + j] != 0)(
                functools.partial(second_slab, j))

    @pl.when(go_dense)
    def _():
        eye = (lax.broadcasted_iota(jnp.int32, (MOE_SUB, MOE_SUB), 0)
               == lax.broadcasted_iota(jnp.int32, (MOE_SUB, MOE_SUB), 1))
        for j in range(MOE_GROUP):
            for s in range(n_sub):
                cols = slice(MOE_SUB * s, MOE_SUB * (s + 1))
                gb_ref[cols, :] = lane_broadcast(
                    jnp.where(eye, gate_ref[j:j + 1, cols], 0.0))
            act = hidden(h2_ref[...], gb_ref[0:tm, :], j)
            o_ref[...] += jnp.dot(act, w2_ref[j], preferred_element_type=F32)


def _moe(fexp, fpair, fdense, h2, gates_t, pos_t, w1e, w3e, w2e, tm):
    t, d = h2.shape
    ne = w1e.shape[0]
    assert ne % MOE_GROUP == 0 and tm % MOE_SUB == 0
    n_sub = tm // MOE_SUB
    rows = MOE_GROUP * 2 * n_sub * MOE_CAP
    assert rows >= tm, "gate scratch is reused by the dense path"
    tok = pl.BlockSpec((tm, d), lambda i, e, *_: (i, 0))
    per_expert = pl.BlockSpec((MOE_GROUP, tm), lambda i, e, *_: (e, i))
    return pl.pallas_call(
        _moe_kernel,
        grid_spec=pltpu.PrefetchScalarGridSpec(
            num_scalar_prefetch=3,
            grid=(t // tm, ne // MOE_GROUP),
            in_specs=[tok, per_expert, per_expert,
                      pl.BlockSpec((MOE_GROUP, d, D_EXPERT), lambda i, e, *_: (e, 0, 0)),
                      pl.BlockSpec((MOE_GROUP, d, D_EXPERT), lambda i, e, *_: (e, 0, 0)),
                      pl.BlockSpec((MOE_GROUP, D_EXPERT, d), lambda i, e, *_: (e, 0, 0))],
            out_specs=tok,
            scratch_shapes=[pltpu.VMEM((rows, d), BF16),
                            pltpu.VMEM((rows, D_EXPERT), F32),
                            pltpu.VMEM((n_sub, MOE_SUB, MOE_GROUP * MOE_CAP), BF16)]),
        out_shape=jax.ShapeDtypeStruct((t, d), F32),
        compiler_params=_params(("arbitrary", "arbitrary")),
        name="moe_experts",
    )(fexp, fpair, fdense, h2, gates_t, pos_t, w1e, w3e, w2e)


def _final_kernel(base_ref, routed_ref, g2_ref, nf_ref, o_ref):
    xo = base_ref[...] + g2_ref[...] * routed_ref[...]
    o_ref[...] = (xo * lax.rsqrt(jnp.mean(xo * xo, axis=-1, keepdims=True) + RMS_EPS)
                  * nf_ref[...])


def _final(base, routed, g2, nf, tm, tiles_per_batch):
    t, d = base.shape
    tok = pl.BlockSpec((tm, d), lambda i: (i, 0))
    return pl.pallas_call(
        _final_kernel,
        grid=(t // tm,),
        in_specs=[tok, tok,
                  pl.BlockSpec((None, 1, d), lambda i: (i // tiles_per_batch, 0, 0)),
                  pl.BlockSpec((1, d), lambda i: (0, 0))],
        out_specs=tok,
        out_shape=jax.ShapeDtypeStruct((t, d), F32),
        compiler_params=_params(("arbitrary",)),
        name="residual_final_norm",
    )(base, routed, g2, nf)


def _tile(n, want):
    t = min(n, want)
    assert n % t == 0, (n, t)
    return t


def kernel(x, c, w_ada, b_ada, norm1_g, w_in, mu_shift, w0, w_decay_up, a0, w_a_up, w_g_up,
           k_k, k_a, r_k, lnx_g, lnx_b, sgu_ln_g, sgu_ln_b, w_spatial, b_spatial, w_out,
           norm2_g, w_router, e_bias, w1_e, w3_e, w2_e, w1_s, w3_s, w2_s, norm_f_g):
    b, s, d = x.shape
    t = b * s
    assert w_ada.shape[0] == 1, "single-layer block"
    assert d == 2 * D_RWKV and s % SGU_BLOCK == 0
    row = lambda vec: vec.reshape(1, -1)

    c_pad = jnp.pad(c, ((0, 8 - b), (0, 0)))
    mod = _mod(c_pad, w_ada[0], b_ada)[:b]
    sh1, sc1, g1, sh2, sc2, g2 = [m[:, None, :] for m in jnp.split(mod, 6, axis=-1)]

    bias_tile = jnp.repeat(b_spatial[0].T, HEAD, axis=1)
    ps, yb = _inproj(x, sc1, sh1, norm1_g, w_in[0].astype(BF16), mu_shift,
                     sgu_ln_g, sgu_ln_b, w_spatial[0], bias_tile, _tile(s, TILE_INPROJ))

    wlora = jnp.zeros((LANES, 2 * D_RWKV), F32)
    wlora = wlora.at[:D_DECAY_LORA, :D_RWKV].set(w_decay_up[0])
    wlora = wlora.at[D_DECAY_LORA:, D_RWKV:].set(w_a_up[0])
    seq = lambda z: z.reshape(b, s, D_RWKV)
    w_c, rh_c, u0_c, y0_c, bh_c, kh_c, pl_c, v, g, bonus = _chunk(
        ps.reshape(t, N_SHIFT), w0, a0, wlora, w_g_up[0], k_k, k_a, row(r_k))
    y = _rwkv(seq(w_c), seq(rh_c), seq(u0_c), seq(y0_c), seq(bh_c), seq(kh_c), seq(v),
              seq(pl_c), _tile(s, TILE_SCAN))

    tm_mid = _tile(s, TILE_MID)
    wr_t = jnp.pad(w_router[0].T, ((0, LANES - N_EXPERTS), (0, 0)))
    wo = w_out[0].astype(BF16)
    h2, logits_t, base = _mid(
        y.reshape(t, D_RWKV), g, bonus, lnx_g, lnx_b, yb.reshape(t, D_SGU), x.reshape(t, d),
        g1, sc2, sh2, g2, norm2_g, wo[:D_RWKV], wo[D_RWKV:], wr_t,
        w1_s[0].astype(BF16), w3_s[0].astype(BF16), w2_s[0].astype(BF16), tm_mid, s // tm_mid)

    tm_moe = _tile(t, TILE_MOE)
    gates_t, pos_t, fexp, fpair, fdense = _route(
        logits_t, jnp.broadcast_to(e_bias[0][:, None], (N_EXPERTS, tm_moe)), tm_moe)

    routed = _moe(fexp[:, 0], fpair[:, :tm_moe // MOE_SUB].reshape(-1), fdense[:, 0],
                  h2, gates_t, pos_t, w1_e[0].astype(BF16), w3_e[0].astype(BF16),
                  w2_e[0].astype(BF16), tm_moe)
    tm_fin = _tile(s, TILE_FINAL)
    out = _final(base, routed, g2, row(norm_f_g), tm_fin, s // tm_fin)
    return out.reshape(b, s, d)
```
